```python
import math, functools
import jax, jax.numpy as jnp
from jax import lax
import numpy as np

D_MODEL = 1024
BATCH = 4
SEQ = 4096
DEPTH = 2
DEC_BATCH = 128
DEC_SEQ = 4
PAST_LEN = 2048
PAGE_SIZE = 128

A_HEADS = 8
A_HD = 64
A_W = A_HEADS * A_HD
MOBA_BLOCK = 256
MOBA_TOPK = 3
Q_BLOCK = 128
ATTN_SCALE = A_HD ** -0.5
S_HEADS = 8
S_HD = 64
S_W = S_HEADS * S_HD
S_GROUPS = 2
S_STATE = 128
S_CONV = 4
S_CONV_CH = S_W + 2 * S_GROUPS * S_STATE
SSD_CHUNK = 128
R_HEADS = 8
R_HD = 64
R_W = R_HEADS * R_HD
R_LORA_W = 64
R_LORA_A = 64
R_LORA_G = 128
R_IN_W = 3 * R_W + R_LORA_W + R_LORA_A + R_LORA_G
D_FF = ((8 * D_MODEL // 3 + 127) // 128) * 128
F_CONV = 3
N_IN = 3 * A_W + S_W + S_CONV_CH + S_HEADS + R_IN_W + 3 * D_MODEL
IN_SPLITS = (A_W, 2 * A_W, 3 * A_W, 3 * A_W + S_W, 3 * A_W + S_W + S_CONV_CH,
             3 * A_W + S_W + S_CONV_CH + S_HEADS, 3 * A_W + S_W + S_CONV_CH + S_HEADS + R_IN_W)
RW_SPLITS = (R_W, 2 * R_W, 3 * R_W, 3 * R_W + R_LORA_W, 3 * R_W + R_LORA_W + R_LORA_A)
NORM_EPS = 1e-6
GN_EPS = 64e-5
NEG_INF = -1e30

kernel_name = 'hybrid_moba_ssd_rwkv7_convffn_step'


def rmsnorm(x, g):
    xf = x.astype(jnp.float32)
    xf = xf * lax.rsqrt(jnp.mean(xf * xf, axis=-1, keepdims=True) + NORM_EPS)
    return (xf * g.astype(jnp.float32)).astype(x.dtype)


def group_rmsnorm(y, g, groups):
    bsz, seq, width = y.shape
    yr = y.astype(jnp.float32).reshape(bsz, seq, groups, width // groups)
    yr = yr * lax.rsqrt(jnp.mean(yr * yr, axis=-1, keepdims=True) + NORM_EPS)
    return yr.reshape(bsz, seq, width) * g.astype(jnp.float32)


def causal_dwconv(u, prefix, w, b):
    width = w.shape[0]
    seq = u.shape[1]
    up = jnp.concatenate([prefix.astype(u.dtype), u], axis=1)
    y = b + up[:, 0:seq] * w[0]
    for j in range(1, width):
        y = y + up[:, j:j + seq] * w[j]
    return y, up[:, seq:]


def alibi_slopes():
    return jnp.exp2(-8.0 * jnp.arange(1, A_HEADS + 1, dtype=jnp.float32) / A_HEADS)


def moba_attend(q, qpos, kb, vb, kmean, own_blk, slopes):
    f32 = jnp.float32
    nh, nb = kb.shape[0], kb.shape[1]
    qh = jnp.swapaxes(q, 0, 1).astype(f32)
    nq = qh.shape[1]
    gate = jnp.einsum('hqd,hnd->hqn', qh, kmean)
    gate = jnp.where(jnp.arange(nb) < own_blk, gate, NEG_INF)
    _, sel = lax.top_k(gate, MOBA_TOPK)
    slot_ok = jnp.arange(MOBA_TOPK) < own_blk
    sel = jnp.where(slot_ok, sel, 0)
    hidx = jnp.arange(nh)[:, None, None]
    kg = kb[hidx, sel].astype(f32)
    vg = vb[hidx, sel].astype(f32)
    s_past = jnp.einsum('hqd,hqjkd->hqjk', qh, kg) * ATTN_SCALE
    kpos_past = sel[..., None] * MOBA_BLOCK + jnp.arange(MOBA_BLOCK)
    s_past = s_past - slopes[:, None, None, None] * (qpos[None, :, None, None] - kpos_past).astype(f32)
    s_past = jnp.where(slot_ok[:, None], s_past, NEG_INF)
    k_own = kb[:, own_blk].astype(f32)
    v_own = vb[:, own_blk].astype(f32)
    dist = qpos[:, None] - (own_blk * MOBA_BLOCK + jnp.arange(MOBA_BLOCK))[None, :]
    s_own = jnp.einsum('hqd,hkd->hqk', qh, k_own) * ATTN_SCALE - slopes[:, None, None] * dist.astype(f32)
    s_own = jnp.where(dist >= 0, s_own, NEG_INF)
    probs = jax.nn.softmax(jnp.concatenate([s_past.reshape(nh, nq, MOBA_TOPK * MOBA_BLOCK), s_own], axis=-1), axis=-1)
    p_past = probs[..., :MOBA_TOPK * MOBA_BLOCK].reshape(nh, nq, MOBA_TOPK, MOBA_BLOCK)
    p_own = probs[..., MOBA_TOPK * MOBA_BLOCK:]
    o = jnp.einsum('hqjk,hqjkd->hqd', p_past, vg) + jnp.einsum('hqk,hkd->hqd', p_own, v_own)
    return jnp.swapaxes(o, 0, 1).astype(q.dtype)


def moba_prompt(q, k, v, slopes):
    bsz, seq, nh, hd = q.shape
    nb = max(-(-seq // MOBA_BLOCK), MOBA_TOPK)
    pad = nb * MOBA_BLOCK - seq

    def to_blocks(t):
        t = jnp.pad(t, ((0, 0), (0, pad), (0, 0), (0, 0)))
        return t.reshape(bsz, nb, MOBA_BLOCK, nh, hd).transpose(0, 3, 1, 2, 4)

    kb, vb = to_blocks(k), to_blocks(v)
    kmean = jnp.mean(kb.astype(jnp.float32), axis=3)
    nq = seq // Q_BLOCK
    qb = q.reshape(bsz * nq, Q_BLOCK, nh, hd)
    flat = jnp.arange(bsz * nq)

    def one(args):
        qq, n = args
        b = n // nq
        i = n % nq
        qpos = i * Q_BLOCK + jnp.arange(Q_BLOCK)
        return moba_attend(qq, qpos, kb[b], vb[b], kmean[b], (i * Q_BLOCK) // MOBA_BLOCK, slopes)

    o = lax.map(one, (qb, flat))
    return o.reshape(bsz, seq, nh * hd)


def moba_sample(q, k, v, cache_k_l, cache_v_l, page_table, slopes):
    dbsz, tnew, nh, hd = q.shape
    past = page_table.shape[1] * cache_k_l.shape[1]
    total = past + tnew
    nb = max(-(-total // MOBA_BLOCK), MOBA_TOPK)
    pad = nb * MOBA_BLOCK - total
    own = past // MOBA_BLOCK
    qpos = past + jnp.arange(tnew)

    def to_blocks(pool, new, pt):
        rows = jnp.concatenate([pool[pt].reshape(past, nh, hd).astype(new.dtype), new], axis=0)
        rows = jnp.pad(rows, ((0, pad), (0, 0), (0, 0)))
        return rows.reshape(nb, MOBA_BLOCK, nh, hd).transpose(2, 0, 1, 3)

    def one(args):
        qq, kn, vn, pt = args
        kb = to_blocks(cache_k_l, kn, pt)
        vb = to_blocks(cache_v_l, vn, pt)
        kmean = jnp.mean(kb.astype(jnp.float32), axis=2)
        return moba_attend(qq, qpos, kb, vb, kmean, own, slopes)

    o = lax.map(one, (q, k, v, page_table))
    return o.reshape(dbsz, tnew, nh * hd)


def ssd_chunked(x, dt, a, bm, cm, h0, chunk):
    bsz, seq, nh, hp = x.shape
    ng, ns = bm.shape[2], bm.shape[3]
    hpg = nh // ng
    nc = seq // chunk
    xr = x.reshape(bsz, nc, chunk, ng, hpg, hp)
    dtr = dt.reshape(bsz, nc, chunk, ng, hpg)
    br = bm.reshape(bsz, nc, chunk, ng, ns)
    cr = cm.reshape(bsz, nc, chunk, ng, ns)
    cum = jnp.cumsum(dtr * a.reshape(ng, hpg), axis=2)
    causal = jnp.tril(jnp.ones((chunk, chunk), dtype=bool))
    seg = cum[:, :, :, None] - cum[:, :, None, :]
    decay = jnp.exp(jnp.where(causal[:, :, None, None], seg, -jnp.inf))
    cb = jnp.einsum('bclgn,bcsgn->bclsg', cr, br)
    wts = cb[..., None] * decay * dtr[:, :, None]
    y = jnp.einsum('bclsgh,bcsghp->bclghp', wts, xr)
    to_end = jnp.exp(cum[:, :, -1:] - cum) * dtr
    st = jnp.einsum('bclgn,bclgh,bclghp->bcghpn', br, to_end, xr)
    cdec = jnp.exp(cum[:, :, -1])

    def step(h, inp):
        s_c, d_c = inp
        return h * d_c[..., None, None] + s_c, h

    h_fin, h_in = lax.scan(step, h0.reshape(bsz, ng, hpg, hp, ns), (jnp.moveaxis(st, 1, 0), jnp.moveaxis(cdec, 1, 0)))
    h_in = jnp.moveaxis(h_in, 0, 1)
    y = y + jnp.einsum('bclgn,bcghpn->bclghp', cr, h_in) * jnp.exp(cum)[..., None]
    return y.reshape(bsz, seq, nh, hp), h_fin.reshape(bsz, nh, hp, ns)


def rwkv7_scan(r, decay, k, v, kk, a, s0):
    def step(s, inp):
        r_t, w_t, k_t, v_t, kk_t, a_t = inp
        s = (s * w_t[:, :, None, :]
             - jnp.einsum('bhvk,bhk->bhv', s, kk_t)[..., None] * (kk_t * a_t)[:, :, None, :]
             + v_t[..., None] * k_t[:, :, None, :])
        return s, jnp.einsum('bhvk,bhk->bhv', s, r_t)

    xs = tuple(jnp.swapaxes(t, 0, 1) for t in (r, decay, k, v, kk, a))
    s_fin, ys = lax.scan(step, s0, xs)
    return jnp.swapaxes(ys, 0, 1), s_fin


def trunk_layer(x, lp, st, attend, ssd_chunk):
    f32 = jnp.float32
    bsz, seq, _ = x.shape
    xn = rmsnorm(x, lp['norm1_g'])
    q, k, v, z, xbc, dt_raw, rw_in, gate_in = jnp.split(xn @ lp['w_in'], IN_SPLITS, axis=-1)

    q = q.reshape(bsz, seq, A_HEADS, A_HD)
    k = k.reshape(bsz, seq, A_HEADS, A_HD)
    v = v.reshape(bsz, seq, A_HEADS, A_HD)
    o_a = attend(q, k, v)

    xbc, ssm_conv_new = causal_dwconv(xbc, st['ssm_conv'], lp['ssm_conv_w'], lp['ssm_conv_b'])
    xbc = jax.nn.silu(xbc).astype(f32)
    xs, bm, cm = jnp.split(xbc, (S_W, S_W + S_GROUPS * S_STATE), axis=-1)
    xs = xs.reshape(bsz, seq, S_HEADS, S_HD)
    bm = bm.reshape(bsz, seq, S_GROUPS, S_STATE)
    cm = cm.reshape(bsz, seq, S_GROUPS, S_STATE)
    dt = jax.nn.softplus((dt_raw + lp['ssm_dt_bias']).astype(f32))
    a_ssm = -jnp.exp(lp['ssm_a_log'].astype(f32))
    y, ssm_new = ssd_chunked(xs, dt, a_ssm, bm, cm, st['ssm'].astype(f32), ssd_chunk)
    y = (y + lp['ssm_d'].astype(f32)[:, None] * xs).reshape(bsz, seq, S_W)
    o_b = group_rmsnorm(y * jax.nn.silu(z.astype(f32)), lp['ssm_norm_g'], S_GROUPS).astype(x.dtype)

    prev = jnp.concatenate([st['rwkv_shift'].astype(rw_in.dtype), rw_in[:, :-1]], axis=1)
    rw_x = rw_in + (prev - rw_in) * lp['rw_mu']
    shift_new = rw_in[:, -1:]
    r, kr, vr, xw, xa, xg = jnp.split(rw_x, RW_SPLITS, axis=-1)
    w_log = lp['rw_w0'] + jnp.tanh(xw) @ lp['rw_w2']
    decay = jnp.exp(-jnp.exp((-jax.nn.softplus(-w_log) - 0.5).astype(f32)))
    a_rw = jax.nn.sigmoid((lp['rw_a0'] + xa @ lp['rw_a2']).astype(f32))
    g_rw = jax.nn.sigmoid(xg) @ lp['rw_g2']
    hs = (bsz, seq, R_HEADS, R_HD)
    r = r.astype(f32).reshape(hs)
    kr = kr.astype(f32).reshape(hs)
    vr = vr.astype(f32).reshape(hs)
    decay = decay.reshape(hs)
    a_rw = a_rw.reshape(hs)
    kk = kr * lp['rw_kk'].astype(f32).reshape(R_HEADS, R_HD)
    kk = kk / jnp.maximum(jnp.sqrt(jnp.sum(kk * kk, axis=-1, keepdims=True)), 1e-12)
    kr = kr * (1.0 + (a_rw - 1.0) * lp['rw_ka'].astype(f32).reshape(R_HEADS, R_HD))
    yr, rwkv_new = rwkv7_scan(r, decay, kr, vr, kk, a_rw, st['rwkv'].astype(f32))
    mean = jnp.mean(yr, axis=-1, keepdims=True)
    var = jnp.mean(jnp.square(yr - mean), axis=-1, keepdims=True)
    yr = ((yr - mean) * lax.rsqrt(var + GN_EPS) * lp['rw_ln_g'].astype(f32).reshape(R_HEADS, R_HD)
          + lp['rw_ln_b'].astype(f32).reshape(R_HEADS, R_HD))
    yr = yr + jnp.sum(r * kr * lp['rw_rk'].astype(f32), axis=-1, keepdims=True) * vr
    o_c = (yr.reshape(bsz, seq, R_W) * g_rw).astype(x.dtype)

    g_a, g_b, g_c = jnp.split(jax.nn.sigmoid(gate_in + lp['b_gate']), 3, axis=-1)
    merged = g_a * (o_a @ lp['w_pa']) + g_b * (o_b @ lp['w_pb']) + g_c * (o_c @ lp['w_pc'])
    x = x + merged @ lp['w_o']

    ug, uv = jnp.split(rmsnorm(x, lp['norm2_g']) @ lp['w_up'], 2, axis=-1)
    ug, ffn_conv_new = causal_dwconv(ug, st['ffn_conv'], lp['ffn_conv_w'], lp['ffn_conv_b'])
    x = x + (jax.nn.silu(ug) * uv) @ lp['w_down']
    odt = x.dtype
    new_state = (k, v, ssm_new.astype(odt), ssm_conv_new, rwkv_new.astype(odt), shift_new, ffn_conv_new)
    return x, new_state


def setup_inputs(seed: int = 0) -> dict:
    key = jax.random.key(seed)
    ks = jax.random.split(key, 48)
    f32 = jnp.float32
    n_pages = PAST_LEN // PAGE_SIZE
    n_pool = (DEC_BATCH * n_pages * 5) // 4

    def nrm(i, shape, scale=1.0):
        return scale * jax.random.normal(ks[i], shape, f32)

    def gain(i, shape):
        return 1.0 + 0.02 * jax.random.normal(ks[i], shape, f32)

    page_table = jax.random.permutation(ks[0], n_pool)[:DEC_BATCH * n_pages].reshape(DEC_BATCH, n_pages).astype(jnp.int32)
    dt0 = jnp.exp(jax.random.uniform(ks[1], (DEPTH, S_HEADS), f32, math.log(1e-3), math.log(1e-1)))
    return {
        'x_prompt': nrm(2, (BATCH, SEQ, D_MODEL)),
        'x_sample': nrm(3, (DEC_BATCH, DEC_SEQ, D_MODEL)),
        'cache_k': nrm(4, (DEPTH, n_pool, PAGE_SIZE, A_HEADS, A_HD)),
        'cache_v': nrm(5, (DEPTH, n_pool, PAGE_SIZE, A_HEADS, A_HD)),
        'state_ssm': nrm(6, (DEPTH, DEC_BATCH, S_HEADS, S_HD, S_STATE), 0.5),
        'state_ssm_conv': nrm(7, (DEPTH, DEC_BATCH, S_CONV - 1, S_CONV_CH)),
        'state_rwkv': nrm(8, (DEPTH, DEC_BATCH, R_HEADS, R_HD, R_HD), 0.5),
        'state_rwkv_shift': nrm(9, (DEPTH, DEC_BATCH, 1, R_IN_W)),
        'state_ffn_conv': nrm(10, (DEPTH, DEC_BATCH, F_CONV - 1, D_FF)),
        'page_table': page_table,
        'norm1_g': gain(11, (DEPTH, D_MODEL)),
        'w_in': nrm(12, (DEPTH, D_MODEL, N_IN), D_MODEL ** -0.5),
        'b_gate': nrm(13, (DEPTH, 3 * D_MODEL), 0.02),
        'w_pa': nrm(14, (DEPTH, A_W, D_MODEL), A_W ** -0.5),
        'ssm_conv_w': nrm(15, (DEPTH, S_CONV, S_CONV_CH), S_CONV ** -0.5),
        'ssm_conv_b': nrm(16, (DEPTH, S_CONV_CH), 0.02),
        'ssm_dt_bias': dt0 + jnp.log(-jnp.expm1(-dt0)),
        'ssm_a_log': jnp.log(jax.random.uniform(ks[17], (DEPTH, S_HEADS), f32, 1.0, 16.0)),
        'ssm_d': gain(18, (DEPTH, S_HEADS)),
        'ssm_norm_g': gain(19, (DEPTH, S_W)),
        'w_pb': nrm(20, (DEPTH, S_W, D_MODEL), S_W ** -0.5),
        'rw_mu': jax.random.uniform(ks[21], (DEPTH, R_IN_W), f32),
        'rw_w0': jax.random.uniform(ks[22], (DEPTH, R_W), f32, -5.0, 0.5),
        'rw_w2': nrm(23, (DEPTH, R_LORA_W, R_W), 0.1 * R_LORA_W ** -0.5),
        'rw_a0': nrm(24, (DEPTH, R_W), 0.1),
        'rw_a2': nrm(25, (DEPTH, R_LORA_A, R_W), R_LORA_A ** -0.5),
        'rw_g2': nrm(26, (DEPTH, R_LORA_G, R_W), R_LORA_G ** -0.5),
        'rw_kk': 0.85 + nrm(27, (DEPTH, R_W), 0.02),
        'rw_ka': gain(28, (DEPTH, R_W)),
        'rw_rk': nrm(29, (DEPTH, R_HEADS, R_HD), 0.1),
        'rw_ln_g': gain(30, (DEPTH, R_W)),
        'rw_ln_b': nrm(31, (DEPTH, R_W), 0.02),
        'w_pc': nrm(32, (DEPTH, R_W, D_MODEL), R_W ** -0.5),
        'w_o': nrm(33, (DEPTH, D_MODEL, D_MODEL), D_MODEL ** -0.5),
        'norm2_g': gain(34, (DEPTH, D_MODEL)),
        'w_up': nrm(35, (DEPTH, D_MODEL, 2 * D_FF), D_MODEL ** -0.5),
        'ffn_conv_w': nrm(36, (DEPTH, F_CONV, D_FF), F_CONV ** -0.5),
        'ffn_conv_b': nrm(37, (DEPTH, D_FF), 0.02),
        'w_down': nrm(38, (DEPTH, D_FF, D_MODEL), D_FF ** -0.5),
        'norm_f_g': gain(39, (D_MODEL,)),
    }


def reference(x_prompt, x_sample, cache_k, cache_v, state_ssm, state_ssm_conv, state_rwkv, state_rwkv_shift,
              state_ffn_conv, page_table, norm1_g, w_in, b_gate, w_pa, ssm_conv_w, ssm_conv_b, ssm_dt_bias,
              ssm_a_log, ssm_d, ssm_norm_g, w_pb, rw_mu, rw_w0, rw_w2, rw_a0, rw_a2, rw_g2, rw_kk, rw_ka, rw_rk,
              rw_ln_g, rw_ln_b, w_pc, w_o, norm2_g, w_up, ffn_conv_w, ffn_conv_b, w_down, norm_f_g):
    f32 = jnp.float32
    slopes = alibi_slopes()
    hp, hs = x_prompt, x_sample
    nbp = x_prompt.shape[0]
    new_p = [[] for _ in range(7)]
    new_s = [[] for _ in range(7)]
    for l in range(DEPTH):
        lp = {'norm1_g': norm1_g[l], 'w_in': w_in[l], 'b_gate': b_gate[l], 'w_pa': w_pa[l],
              'ssm_conv_w': ssm_conv_w[l], 'ssm_conv_b': ssm_conv_b[l], 'ssm_dt_bias': ssm_dt_bias[l],
              'ssm_a_log': ssm_a_log[l], 'ssm_d': ssm_d[l], 'ssm_norm_g': ssm_norm_g[l], 'w_pb': w_pb[l],
              'rw_mu': rw_mu[l], 'rw_w0': rw_w0[l], 'rw_w2': rw_w2[l], 'rw_a0': rw_a0[l], 'rw_a2': rw_a2[l],
              'rw_g2': rw_g2[l], 'rw_kk': rw_kk[l], 'rw_ka': rw_ka[l], 'rw_rk': rw_rk[l], 'rw_ln_g': rw_ln_g[l],
              'rw_ln_b': rw_ln_b[l], 'w_pc': w_pc[l], 'w_o': w_o[l], 'norm2_g': norm2_g[l], 'w_up': w_up[l],
              'ffn_conv_w': ffn_conv_w[l], 'ffn_conv_b': ffn_conv_b[l], 'w_down': w_down[l]}
        st_p = {'ssm': jnp.zeros((nbp, S_HEADS, S_HD, S_STATE), f32),
                'ssm_conv': jnp.zeros((nbp, S_CONV - 1, S_CONV_CH), hp.dtype),
                'rwkv': jnp.zeros((nbp, R_HEADS, R_HD, R_HD), f32),
                'rwkv_shift': jnp.zeros((nbp, 1, R_IN_W), hp.dtype),
                'ffn_conv': jnp.zeros((nbp, F_CONV - 1, D_FF), hp.dtype)}
        st_s = {'ssm': state_ssm[l], 'ssm_conv': state_ssm_conv[l], 'rwkv': state_rwkv[l],
                'rwkv_shift': state_rwkv_shift[l], 'ffn_conv': state_ffn_conv[l]}
        hp, sp = trunk_layer(hp, lp, st_p, functools.partial(moba_prompt, slopes=slopes),
                             min(SSD_CHUNK, hp.shape[1]))
        hs, ss = trunk_layer(hs, lp, st_s,
                             functools.partial(moba_sample, cache_k_l=cache_k[l], cache_v_l=cache_v[l],
                                               page_table=page_table, slopes=slopes),
                             hs.shape[1])
        for j in range(7):
            new_p[j].append(sp[j])
            new_s[j].append(ss[j])
    y_prompt = rmsnorm(hp, norm_f_g)
    y_sample = rmsnorm(hs, norm_f_g)
    k_prompt, k_sample = jnp.stack(new_p[0]), jnp.stack(new_s[0])
    v_prompt, v_sample = jnp.stack(new_p[1]), jnp.stack(new_s[1])
    ssm_prompt, ssm_sample = jnp.stack(new_p[2]), jnp.stack(new_s[2])
    ssm_conv_prompt, ssm_conv_sample = jnp.stack(new_p[3]), jnp.stack(new_s[3])
    rwkv_prompt, rwkv_sample = jnp.stack(new_p[4]), jnp.stack(new_s[4])
    rwkv_shift_prompt, rwkv_shift_sample = jnp.stack(new_p[5]), jnp.stack(new_s[5])
    ffn_conv_prompt, ffn_conv_sample = jnp.stack(new_p[6]), jnp.stack(new_s[6])
    return (y_prompt, y_sample, k_prompt, k_sample, v_prompt, v_sample, ssm_prompt, ssm_sample,
            ssm_conv_prompt, ssm_conv_sample, rwkv_prompt, rwkv_sample, rwkv_shift_prompt, rwkv_shift_sample,
            ffn_conv_prompt, ffn_conv_sample)
```

```python
import functools
import math

import jax
import jax.numpy as jnp
from jax import lax
from jax.experimental import pallas as pl
from jax.experimental.pallas import tpu as pltpu

F32 = jnp.float32
BF16 = jnp.bfloat16

D_MODEL = 1024
A_HEADS = 8
A_HD = 64
A_W = A_HEADS * A_HD
MOBA_BLOCK = 256
MOBA_TOPK = 3
Q_BLOCK = 128
ATTN_SCALE = A_HD ** -0.5
S_HEADS = 8
S_HD = 64
S_W = S_HEADS * S_HD
S_GROUPS = 2
S_STATE = 128
S_CONV = 4
S_CONV_CH = S_W + 2 * S_GROUPS * S_STATE
SSD_CHUNK = 128
R_HEADS = 8
R_HD = 64
R_W = R_HEADS * R_HD
R_LORA_W = 64
R_LORA_A = 64
R_LORA_G = 128
R_LORA = R_LORA_W + R_LORA_A + R_LORA_G
R_IN_W = 3 * R_W + R_LORA
D_FF = ((8 * D_MODEL // 3 + 127) // 128) * 128
F_CONV = 3
NORM_EPS = 1e-6
GN_EPS = 64e-5
NEG_INF = -1e30

LANES = 128
SUBLANES = 8
HALF = 64
HALF_SHIFT = 6
SUBLANE_SHIFT = 3
N_PAIRS = 4

C_RW = 0
C_DT = R_IN_W
C_Q = 2048
C_K = C_Q + A_W
C_V = C_K + A_W
C_Z = C_V + A_W
C_XBC = C_Z + S_W
C_GATE = C_XBC + S_CONV_CH
N_PROJ = C_GATE + 3 * D_MODEL

VMEM_LIMIT = 56 * 1024 * 1024


def _cparams(sem):
    return pltpu.CompilerParams(dimension_semantics=sem, vmem_limit_bytes=VMEM_LIMIT)


def _dot(a, b):
    return jnp.dot(a.astype(BF16), b.astype(BF16), preferred_element_type=F32)


def _dot_nt(a, b):
    return lax.dot_general(a.astype(BF16), b.astype(BF16), (((1,), (1,)), ((), ())),
                           preferred_element_type=F32)


def _dot_tn(a, b):
    return lax.dot_general(a.astype(BF16), b.astype(BF16), (((0,), (0,)), ((), ())),
                           preferred_element_type=F32)


def _dot_hi(a, b):
    return jnp.dot(a, b, preferred_element_type=F32, precision=lax.Precision.HIGHEST)


def _dot_nt_hi(a, b):
    return lax.dot_general(a, b, (((1,), (1,)), ((), ())), preferred_element_type=F32,
                           precision=lax.Precision.HIGHEST)


def _sigmoid(x):
    return 1.0 / (1.0 + jnp.exp(-x))


def _silu(x):
    return x * _sigmoid(x)


def _softplus(x):
    return jnp.maximum(x, 0.0) + jnp.log(1.0 + jnp.exp(-jnp.abs(x)))


def _head_ones(width, scale):
    r = lax.shift_right_logical(lax.broadcasted_iota(jnp.int32, (width, width), 0), HALF_SHIFT)
    c = lax.shift_right_logical(lax.broadcasted_iota(jnp.int32, (width, width), 1), HALF_SHIFT)
    return jnp.where(r == c, scale, 0.0).astype(F32)


def _rms_matmul_kernel(x_ref, g_ref, w_ref, o_ref, xn_ref):
    @pl.when(pl.program_id(1) == 0)
    def _():
        x = x_ref[...]
        ms = jnp.mean(x * x, axis=-1, keepdims=True)
        xn_ref[...] = (x * lax.rsqrt(ms + NORM_EPS) * g_ref[...]).astype(BF16)

    o_ref[...] = jnp.dot(xn_ref[...], w_ref[...], preferred_element_type=F32)


def _rms_matmul(x, g, w, tm, tn):
    t, d = x.shape
    n = w.shape[1]
    return pl.pallas_call(
        _rms_matmul_kernel,
        grid=(t // tm, n // tn),
        in_specs=[pl.BlockSpec((tm, d), lambda i, j: (i, 0)),
                  pl.BlockSpec((1, d), lambda i, j: (0, 0)),
                  pl.BlockSpec((d, tn), lambda i, j: (0, j))],
        out_specs=pl.BlockSpec((tm, tn), lambda i, j: (i, j)),
        out_shape=jax.ShapeDtypeStruct((t, n), F32),
        scratch_shapes=[pltpu.VMEM((tm, d), BF16)],
        compiler_params=_cparams(("parallel", "arbitrary")),
        name="in_proj",
    )(x, g, w)


def _topk_bias(gate, n_valid):
    lane = lax.broadcasted_iota(jnp.int32, gate.shape, 1)
    lane_f = lane.astype(F32)
    gm = jnp.where(lane < n_valid, gate, NEG_INF)
    selected = jnp.zeros(gate.shape, jnp.bool_)
    for j in range(MOBA_TOPK):
        m = jnp.max(gm, axis=-1, keepdims=True)
        idx = jnp.min(jnp.where(gm == m, lane_f, 1e9), axis=-1, keepdims=True)
        hit = lane_f == idx
        selected = jnp.logical_or(selected, jnp.logical_and(hit, j < n_valid))
        gm = jnp.where(hit, -jnp.inf, gm)
    return jnp.where(selected, 0.0, NEG_INF).astype(F32)


def _moba_prompt_kernel(q_ref, k_ref, v_ref, sl_ref, o_ref, kb_ref, vb_ref, kmean_ref, *, nb):
    i = pl.program_id(2)

    @pl.when(i == 0)
    def _():
        kb_ref[...] = k_ref[0].astype(BF16)
        vb_ref[...] = v_ref[0].astype(BF16)
        kmean_ref[...] = jnp.zeros_like(kmean_ref)
        for n in range(nb):
            kmean_ref[n:n + 1, :] = jnp.mean(k_ref[0, n * MOBA_BLOCK:(n + 1) * MOBA_BLOCK, :],
                                             axis=0, keepdims=True)

    own = (i * Q_BLOCK) // MOBA_BLOCK
    q = q_ref[0]
    lane = lax.broadcasted_iota(jnp.int32, (1, LANES), 1)
    is_a = lane < HALF
    q_h = (jnp.where(is_a, q, 0.0), jnp.where(is_a, 0.0, q))
    slope = (sl_ref[0, :, 0:1], sl_ref[0, :, HALF:HALF + 1])
    kmean = kmean_ref[...]
    bias_h = [_topk_bias(_dot_nt_hi(q_h[h], kmean), own).astype(BF16) for h in range(2)]
    qb_h = [q_h[h].astype(BF16) for h in range(2)]

    rq = lax.broadcasted_iota(jnp.int32, (Q_BLOCK, 1), 0)
    ck = lax.broadcasted_iota(jnp.int32, (1, MOBA_BLOCK), 1)

    k_own = kb_ref[pl.ds(pl.multiple_of(own * MOBA_BLOCK, MOBA_BLOCK), MOBA_BLOCK), :]
    v_own = vb_ref[pl.ds(pl.multiple_of(own * MOBA_BLOCK, MOBA_BLOCK), MOBA_BLOCK), :]
    dist = (i * Q_BLOCK - own * MOBA_BLOCK) + rq - ck
    dist_f = dist.astype(F32)
    carry = []
    for h in range(2):
        s = _dot_nt(qb_h[h], k_own) * ATTN_SCALE - slope[h] * dist_f
        s = jnp.where(dist >= 0, s, NEG_INF)
        m = jnp.max(s, axis=-1, keepdims=True)
        p = jnp.exp(s - m)
        carry += [m, jnp.sum(p, axis=-1, keepdims=True), _dot(p, v_own)]

    sel_row = lax.broadcasted_iota(jnp.int32, (LANES, MOBA_BLOCK), 0)

    def body(n, c):
        start = pl.multiple_of(n * MOBA_BLOCK, MOBA_BLOCK)
        kn = kb_ref[pl.ds(start, MOBA_BLOCK), :]
        vn = vb_ref[pl.ds(start, MOBA_BLOCK), :]
        onehot = jnp.where(sel_row == n, 1.0, 0.0).astype(BF16)
        qoff = (i * Q_BLOCK - n * MOBA_BLOCK + rq).astype(F32)
        ck_f = ck.astype(F32)
        out = []
        for h in range(2):
            m0, l0, a0 = c[3 * h:3 * h + 3]
            s = (_dot_nt(qb_h[h], kn) * ATTN_SCALE + slope[h] * ck_f - slope[h] * qoff
                 + jnp.dot(bias_h[h], onehot, preferred_element_type=F32))
            m1 = jnp.maximum(m0, jnp.max(s, axis=-1, keepdims=True))
            alpha = jnp.exp(m0 - m1)
            p = jnp.exp(s - m1)
            out += [m1, alpha * l0 + jnp.sum(p, axis=-1, keepdims=True), alpha * a0 + _dot(p, vn)]
        return tuple(out)

    c = lax.fori_loop(0, own, body, tuple(carry))
    o_ref[0] = jnp.where(is_a, c[2] / c[1], c[5] / c[4])


def _moba_prompt(proj3, slopes_pair):
    bsz, seq, _ = proj3.shape
    assert seq % MOBA_BLOCK == 0 and seq // MOBA_BLOCK >= MOBA_TOPK
    nb = seq // MOBA_BLOCK
    nq = seq // Q_BLOCK
    qc, kc, vc = C_Q // LANES, C_K // LANES, C_V // LANES
    return pl.pallas_call(
        functools.partial(_moba_prompt_kernel, nb=nb),
        grid=(bsz, N_PAIRS, nq),
        in_specs=[pl.BlockSpec((1, Q_BLOCK, LANES), lambda b, p, i: (b, i, qc + p)),
                  pl.BlockSpec((1, seq, LANES), lambda b, p, i: (b, 0, kc + p)),
                  pl.BlockSpec((1, seq, LANES), lambda b, p, i: (b, 0, vc + p)),
                  pl.BlockSpec((1, 1, LANES), lambda b, p, i: (p, 0, 0))],
        out_specs=pl.BlockSpec((1, Q_BLOCK, LANES), lambda b, p, i: (b, i, p)),
        out_shape=jax.ShapeDtypeStruct((bsz, seq, A_W), F32),
        scratch_shapes=[pltpu.VMEM((seq, LANES), BF16), pltpu.VMEM((seq, LANES), BF16),
                        pltpu.VMEM((LANES, LANES), F32)],
        compiler_params=_cparams(("parallel", "parallel", "arbitrary")),
        name="moba_prompt",
    )(proj3, proj3, proj3, slopes_pair)


def _moba_sample_kernel(pt_ref, q_ref, kn_ref, vn_ref, kp_ref, vp_ref, o_ref,
                        m_ref, l_ref, acc_ref, ksum_ref, kmean_ref, *, n_pages, page, tnew):
    del pt_ref
    p = pl.program_id(1)
    past = n_pages * page
    rows = A_HEADS * SUBLANES
    lane_head = lax.shift_right_logical(lax.broadcasted_iota(jnp.int32, (SUBLANES, A_W), 1), HALF_SHIFT)
    q8 = q_ref[0]
    qbd = jnp.concatenate([jnp.where(lane_head == h, q8, 0.0) for h in range(A_HEADS)], axis=0)
    r = lax.broadcasted_iota(jnp.int32, (rows, 1), 0)
    tok = jnp.bitwise_and(r, SUBLANES - 1).astype(F32)
    slope = jnp.exp2(-(8.0 / A_HEADS) * (lax.shift_right_logical(r, SUBLANE_SHIFT) + 1).astype(F32))

    kpage = kp_ref[0, 0]
    vpage = vp_ref[0, 0]
    ck = lax.broadcasted_iota(jnp.int32, (1, page), 1).astype(F32)
    kpos = ck + (p * page).astype(F32)
    s = _dot_nt(qbd, kpage) * ATTN_SCALE - slope * ((past + tok) - kpos)
    m = jnp.max(s, axis=-1, keepdims=True)
    e = jnp.exp(s - m)
    m_ref[p] = jnp.broadcast_to(m, (rows, LANES))
    l_ref[p] = jnp.broadcast_to(jnp.sum(e, axis=-1, keepdims=True), (rows, LANES))
    acc_ref[p] = _dot(e, vpage)
    ksum_ref[pl.ds(p, 1), :] = jnp.sum(kpage, axis=0, keepdims=True)

    @pl.when(p == n_pages - 1)
    def _():
        ppb = MOBA_BLOCK // page
        n_blk = past // MOBA_BLOCK
        kmean_ref[...] = jnp.zeros_like(kmean_ref)
        for n in range(n_blk):
            tot = ksum_ref[n * ppb:n * ppb + 1, :]
            for j in range(1, ppb):
                tot = tot + ksum_ref[n * ppb + j:n * ppb + j + 1, :]
            kmean_ref[n:n + 1, :] = tot * (1.0 / MOBA_BLOCK)
        bias = _topk_bias(_dot_nt_hi(qbd, kmean_ref[...]), n_blk)

        kn = kn_ref[0]
        vn = vn_ref[0]
        s_own = []
        for j in range(tnew):
            sj = jnp.sum(qbd * kn[j:j + 1, :], axis=-1, keepdims=True) * ATTN_SCALE - slope * (tok - j)
            s_own.append(jnp.where(tok >= j, sj, NEG_INF))
        mx = s_own[0]
        for j in range(1, tnew):
            mx = jnp.maximum(mx, s_own[j])
        mp = []
        for pg in range(n_pages):
            n = pg // ppb
            mpg = m_ref[pg][:, 0:1] + bias[:, n:n + 1]
            mp.append(mpg)
            mx = jnp.maximum(mx, mpg)
        lsum = jnp.zeros((rows, 1), F32)
        acc = jnp.zeros((rows, A_W), F32)
        for j in range(tnew):
            w = jnp.exp(s_own[j] - mx)
            lsum = lsum + w
            acc = acc + w * vn[j:j + 1, :]
        for pg in range(n_pages):
            w = jnp.exp(mp[pg] - mx)
            lsum = lsum + w * l_ref[pg][:, 0:1]
            acc = acc + w * acc_ref[pg]
        out = acc / lsum
        o8 = jnp.zeros((SUBLANES, A_W), F32)
        for h in range(A_HEADS):
            o8 = o8 + jnp.where(lane_head == h, out[h * SUBLANES:(h + 1) * SUBLANES, :], 0.0)
        o_ref[0] = o8


def _moba_sample(q8, k8, v8, cache_k4, cache_v4, page_table, layer, tnew):
    dbsz = q8.shape[0]
    n_pages = page_table.shape[1]
    page = cache_k4.shape[2]
    assert MOBA_BLOCK % page == 0 and (n_pages * page) % MOBA_BLOCK == 0
    assert (n_pages * page) // MOBA_BLOCK >= MOBA_TOPK and tnew <= SUBLANES
    rows = A_HEADS * SUBLANES
    tok_spec = pl.BlockSpec((1, SUBLANES, A_W), lambda b, p, pt: (b, 0, 0))
    page_spec = pl.BlockSpec((1, 1, page, A_W), lambda b, p, pt: (layer, pt[b * n_pages + p], 0, 0))
    grid_spec = pltpu.PrefetchScalarGridSpec(
        num_scalar_prefetch=1,
        grid=(dbsz, n_pages),
        in_specs=[tok_spec, tok_spec, tok_spec, page_spec, page_spec],
        out_specs=tok_spec,
        scratch_shapes=[pltpu.VMEM((n_pages, rows, LANES), F32), pltpu.VMEM((n_pages, rows, LANES), F32),
                        pltpu.VMEM((n_pages, rows, A_W), F32), pltpu.VMEM((n_pages, A_W), F32),
                        pltpu.VMEM((LANES, A_W), F32)],
    )
    return pl.pallas_call(
        functools.partial(_moba_sample_kernel, n_pages=n_pages, page=page, tnew=tnew),
        grid_spec=grid_spec,
        out_shape=jax.ShapeDtypeStruct((dbsz, SUBLANES, A_W), F32),
        compiler_params=_cparams(("parallel", "arbitrary")),
        name="moba_sample",
    )(page_table.reshape(-1), q8, k8, v8, cache_k4, cache_v4)


def _ssd_prompt_kernel(z_ref, xbc_ref, dt_ref, cw_ref, cb_ref, dtb_ref, alog_ref, dvec_ref, ng_ref,
                       o_ref, hout_ref, ext_ref, h_ref, *, L):
    c = pl.program_id(1)
    P = SUBLANES

    @pl.when(c == 0)
    def _():
        ext_ref[0:P, :] = jnp.zeros((P, S_CONV_CH), F32)
        h_ref[...] = jnp.zeros_like(h_ref)

    @pl.when(c > 0)
    def _():
        ext_ref[0:P, :] = ext_ref[L:L + P, :]

    ext_ref[P:P + L, :] = xbc_ref[...]
    acc = cb_ref[...] + cw_ref[S_CONV - 1:S_CONV, :] * ext_ref[P:P + L, :]
    for d in range(1, S_CONV):
        acc = acc + cw_ref[S_CONV - 1 - d:S_CONV - d, :] * ext_ref[P - d:P - d + L, :]
    xbc = _silu(acc)
    xs = xbc[:, :S_W]
    bm = xbc[:, S_W:S_W + S_GROUPS * S_STATE]
    cm = xbc[:, S_W + S_GROUPS * S_STATE:]

    lane = lax.broadcasted_iota(jnp.int32, (1, LANES), 1)
    is_a = lane < HALF
    dt = jnp.where(lane < S_HEADS, _softplus(dt_ref[...] + dtb_ref[...]), 0.0)
    da = dt * (-jnp.exp(alog_ref[...]))
    rr = lax.broadcasted_iota(jnp.int32, (L, L), 0)
    cc = lax.broadcasted_iota(jnp.int32, (L, L), 1)
    causal = rr >= cc
    cum = _dot_hi(jnp.where(causal, 1.0, 0.0).astype(F32), da)
    cum_t = cum.T
    dt_t = dt.T
    row_a = lax.broadcasted_iota(jnp.int32, (2 * S_HD, 1), 0) < S_HD

    ys = []
    cb_g = [None] * S_GROUPS
    for pr in range(N_PAIRS):
        g = (2 * pr * S_GROUPS) // S_HEADS
        bg = bm[:, g * S_STATE:(g + 1) * S_STATE]
        cg = cm[:, g * S_STATE:(g + 1) * S_STATE]
        if cb_g[g] is None:
            cb_g[g] = _dot_nt(cg, bg)
        xs_p = xs[:, pr * LANES:(pr + 1) * LANES]
        heads = (2 * pr, 2 * pr + 1)
        halves = (is_a, jnp.logical_not(is_a))
        yp = jnp.zeros((L, LANES), F32)
        cum_c = [cum[:, h:h + 1] for h in heads]
        for h, half, cc_h in zip(heads, halves, cum_c):
            seg = cc_h - cum_t[h:h + 1, :]
            dec = jnp.exp(jnp.where(causal, seg, -jnp.inf))
            wts = cb_g[g] * dec * dt_t[h:h + 1, :]
            yp = yp + _dot(wts, jnp.where(half, xs_p, 0.0))
        hp = h_ref[pr]
        yp = yp + _dot_nt(cg, hp) * jnp.where(is_a, jnp.exp(cum_c[0]), jnp.exp(cum_c[1]))
        last = [cum[L - 1:L, h:h + 1] for h in heads]
        te = jnp.where(is_a, jnp.exp(last[0] - cum_c[0]) * dt[:, heads[0]:heads[0] + 1],
                       jnp.exp(last[1] - cum_c[1]) * dt[:, heads[1]:heads[1] + 1])
        st = _dot_tn(xs_p * te, bg)
        h_ref[pr] = hp * jnp.where(row_a, jnp.exp(last[0]), jnp.exp(last[1])) + st
        ys.append(yp)
    y = jnp.concatenate(ys, axis=1) + dvec_ref[...] * xs
    yz = y * _silu(z_ref[...])
    gw = S_W // S_GROUPS
    for g in range(S_GROUPS):
        part = yz[:, g * gw:(g + 1) * gw]
        ms = jnp.mean(part * part, axis=-1, keepdims=True)
        o_ref[:, g * gw:(g + 1) * gw] = part * lax.rsqrt(ms + NORM_EPS) * ng_ref[:, g * gw:(g + 1) * gw]

    @pl.when(c == pl.num_programs(1) - 1)
    def _():
        hout_ref[0] = h_ref[...]


def _ssd_prompt(proj, bsz, seq, cw, cb, dtb, alog, dvec, ng):
    L = SSD_CHUNK
    nc = seq // L
    const = lambda shape: pl.BlockSpec(shape, lambda b, c: (0,) * len(shape))
    return pl.pallas_call(
        functools.partial(_ssd_prompt_kernel, L=L),
        grid=(bsz, nc),
        in_specs=[pl.BlockSpec((L, S_W), lambda b, c: (b * nc + c, C_Z // S_W)),
                  pl.BlockSpec((L, S_CONV_CH), lambda b, c: (b * nc + c, C_XBC // S_CONV_CH)),
                  pl.BlockSpec((L, LANES), lambda b, c: (b * nc + c, C_DT // LANES)),
                  const((S_CONV, S_CONV_CH)), const((1, S_CONV_CH)), const((1, LANES)), const((1, LANES)),
                  const((1, S_W)), const((1, S_W))],
        out_specs=[pl.BlockSpec((L, S_W), lambda b, c: (b * nc + c, 0)),
                   pl.BlockSpec((1, N_PAIRS, 2 * S_HD, S_STATE), lambda b, c: (b, 0, 0, 0))],
        out_shape=[jax.ShapeDtypeStruct((bsz * seq, S_W), F32),
                   jax.ShapeDtypeStruct((bsz, N_PAIRS, 2 * S_HD, S_STATE), F32)],
        scratch_shapes=[pltpu.VMEM((L + 2 * SUBLANES, S_CONV_CH), F32),
                        pltpu.VMEM((N_PAIRS, 2 * S_HD, S_STATE), F32)],
        compiler_params=_cparams(("parallel", "arbitrary")),
        name="ssd_prompt",
    )(proj, proj, proj, cw, cb, dtb, alog, dvec, ng)


def _ssd_sample_kernel(z_ref, xbc_ref, dt_ref, pre_ref, h0_ref, cw_ref, cb_ref, dtb_ref, alog_ref,
                       dvec_ref, ng_ref, o_ref, hout_ref, *, T, nb):
    lane = lax.broadcasted_iota(jnp.int32, (1, LANES), 1)
    a_row = -jnp.exp(alog_ref[...])
    rows2 = 2 * S_HD
    eye = (lax.broadcasted_iota(jnp.int32, (rows2, LANES), 0)
           == lax.broadcasted_iota(jnp.int32, (rows2, LANES), 1)).astype(F32)
    row_a = lax.broadcasted_iota(jnp.int32, (rows2, 1), 0) < S_HD
    gw = S_W // S_GROUPS

    def per_seq(b, carry):
        up = [pre_ref[j, b] for j in range(S_CONV - 1)] + [xbc_ref[t, b] for t in range(T)]
        xc, dts = [], []
        for t in range(T):
            acc = cb_ref[...] + cw_ref[0:1, :] * up[t]
            for j in range(1, S_CONV):
                acc = acc + cw_ref[j:j + 1, :] * up[t + j]
            xc.append(_silu(acc))
            dts.append(jnp.where(lane < S_HEADS, _softplus(dt_ref[t, b] + dtb_ref[...]), 0.0))
        ys = [[] for _ in range(T)]
        for pr in range(N_PAIRS):
            g = (2 * pr * S_GROUPS) // S_HEADS
            ha, hb = 2 * pr, 2 * pr + 1
            hs = h0_ref[b, pr]
            for t in range(T):
                x_row = xc[t][:, pr * LANES:(pr + 1) * LANES]
                b_row = xc[t][:, S_W + g * S_STATE:S_W + (g + 1) * S_STATE]
                c_row = xc[t][:, S_W + (S_GROUPS + g) * S_STATE:S_W + (S_GROUPS + g + 1) * S_STATE]
                dt_col = jnp.where(row_a, dts[t][:, ha:ha + 1], dts[t][:, hb:hb + 1])
                a_col = jnp.where(row_a, a_row[:, ha:ha + 1], a_row[:, hb:hb + 1])
                x_col = jnp.sum(eye * x_row, axis=-1, keepdims=True)
                hs = hs * jnp.exp(dt_col * a_col) + (x_col * dt_col) * b_row
                y_col = jnp.sum(hs * c_row, axis=-1, keepdims=True)
                ys[t].append(jnp.sum(eye * y_col, axis=0, keepdims=True))
            hout_ref[b, pr] = hs
        for t in range(T):
            y = jnp.concatenate(ys[t], axis=1) + dvec_ref[...] * xc[t][:, :S_W]
            yz = y * _silu(z_ref[t, b])
            for g in range(S_GROUPS):
                part = yz[:, g * gw:(g + 1) * gw]
                ms = jnp.mean(part * part, axis=-1, keepdims=True)
                o_ref[t, b, :, g * gw:(g + 1) * gw] = (part * lax.rsqrt(ms + NORM_EPS)
                                                       * ng_ref[:, g * gw:(g + 1) * gw])
        return carry

    lax.fori_loop(0, nb, per_seq, 0)


def _ssd_sample(proj4, pre4, h0, cw, cb, dtb, alog, dvec, ng, nb=16):
    T, dbsz = proj4.shape[:2]
    const = lambda shape: pl.BlockSpec(shape, lambda j: (0,) * len(shape))
    hspec = pl.BlockSpec((nb, N_PAIRS, 2 * S_HD, S_STATE), lambda j: (j, 0, 0, 0))
    return pl.pallas_call(
        functools.partial(_ssd_sample_kernel, T=T, nb=nb),
        grid=(dbsz // nb,),
        in_specs=[pl.BlockSpec((T, nb, 1, S_W), lambda j: (0, j, 0, C_Z // S_W)),
                  pl.BlockSpec((T, nb, 1, S_CONV_CH), lambda j: (0, j, 0, C_XBC // S_CONV_CH)),
                  pl.BlockSpec((T, nb, 1, LANES), lambda j: (0, j, 0, C_DT // LANES)),
                  pl.BlockSpec((S_CONV - 1, nb, 1, S_CONV_CH), lambda j: (0, j, 0, 0)),
                  hspec,
                  const((S_CONV, S_CONV_CH)), const((1, S_CONV_CH)), const((1, LANES)), const((1, LANES)),
                  const((1, S_W)), const((1, S_W))],
        out_specs=[pl.BlockSpec((T, nb, 1, S_W), lambda j: (0, j, 0, 0)), hspec],
        out_shape=[jax.ShapeDtypeStruct((T, dbsz, 1, S_W), F32),
                   jax.ShapeDtypeStruct((dbsz, N_PAIRS, 2 * S_HD, S_STATE), F32)],
        compiler_params=_cparams(("parallel",)),
        name="ssd_sample",
    )(proj4, proj4, proj4, pre4, h0, cw, cb, dtb, alog, dvec, ng)


def _rwkv_prep_kernel(u_ref, pre_ref, mu_ref, w0_ref, w2_ref, a0_ref, a2_ref, g2_ref, kk_ref, ka_ref,
                      r_out, w_out, k_out, v_out, kk_out, kka_out, g_out, ext_ref,
                      *, tm, P, stride, tiles_per_seq):
    i = pl.program_id(0)
    first = (i % tiles_per_seq) == 0

    @pl.when(first)
    def _():
        ext_ref[0:P, :] = pre_ref[...]

    @pl.when(jnp.logical_not(first))
    def _():
        ext_ref[0:P, :] = ext_ref[tm:tm + P, :]

    u = u_ref[...]
    ext_ref[P:P + tm, :] = u
    prev = ext_ref[P - stride:P - stride + tm, :]
    x = u + (prev - u) * mu_ref[...]
    r = x[:, 0:R_W]
    kr = x[:, R_W:2 * R_W]
    vr = x[:, 2 * R_W:3 * R_W]
    xl = x[:, 3 * R_W:]
    w_log = w0_ref[...] + _dot(jnp.tanh(xl), w2_ref[...])
    decay = jnp.exp(-jnp.exp(-_softplus(-w_log) - 0.5))
    a = _sigmoid(a0_ref[...] + _dot(xl, a2_ref[...]))
    g = _dot(_sigmoid(xl), g2_ref[...])
    kk = kr * kk_ref[...]
    ss = _dot_hi(kk * kk, _head_ones(R_W, 1.0))
    kk = kk / jnp.maximum(jnp.sqrt(ss), 1e-12)
    r_out[...] = r
    w_out[...] = decay
    k_out[...] = kr * (1.0 + (a - 1.0) * ka_ref[...])
    v_out[...] = vr
    kk_out[...] = kk
    kka_out[...] = kk * a
    g_out[...] = g


def _rwkv_prep(proj, pre, mu, w0, w2p, a0, a2p, g2p, kkw, kaw, tm, stride, tiles_per_seq):
    t = proj.shape[0]
    P = pre.shape[0]
    const = lambda shape: pl.BlockSpec(shape, lambda i: (0,) * len(shape))
    outs = pl.pallas_call(
        functools.partial(_rwkv_prep_kernel, tm=tm, P=P, stride=stride, tiles_per_seq=tiles_per_seq),
        grid=(t // tm,),
        in_specs=[pl.BlockSpec((tm, R_IN_W), lambda i: (i, 0)),
                  const((P, R_IN_W)), const((1, R_IN_W)), const((1, R_W)), const((R_LORA, R_W)),
                  const((1, R_W)), const((R_LORA, R_W)), const((R_LORA, R_W)), const((1, R_W)), const((1, R_W))],
        out_specs=[pl.BlockSpec((tm, R_W), lambda i: (i, 0))] * 7,
        out_shape=[jax.ShapeDtypeStruct((t, R_W), F32)] * 7,
        scratch_shapes=[pltpu.VMEM((tm + 2 * P, R_IN_W), F32)],
        compiler_params=_cparams(("arbitrary",)),
        name="rwkv_prep",
    )(proj, pre, mu, w0, w2p, a0, a2p, g2p, kkw, kaw)
    return outs


def _pair_consts():
    lane = lax.broadcasted_iota(jnp.int32, (R_HD, LANES), 1)
    row = lax.broadcasted_iota(jnp.int32, (R_HD, LANES), 0)
    is_a = lane < HALF
    eye2 = (jnp.bitwise_and(lane, HALF - 1) == row).astype(F32)
    return is_a, eye2


def _pair_sum(x, is_a):
    sa = jnp.sum(jnp.where(is_a, x, 0.0), axis=-1, keepdims=True)
    sb = jnp.sum(jnp.where(is_a, 0.0, x), axis=-1, keepdims=True)
    return jnp.where(is_a, sa, sb)


def _rwkv_step(s, rr, ww, kr, vv, kk, kka, is_a, eye2):
    v_col = _pair_sum(eye2 * vv, is_a)
    sk = _pair_sum(s * kk, is_a)
    s = s * ww - sk * kka + v_col * kr
    y_col = _pair_sum(s * rr, is_a)
    return s, jnp.sum(eye2 * y_col, axis=0, keepdims=True)


def _rwkv_scan_prompt_kernel(r_ref, w_ref, k_ref, v_ref, kk_ref, kka_ref, y_ref, sout_ref, s_ref, *, Lc):
    c = pl.program_id(1)

    @pl.when(c == 0)
    def _():
        s_ref[...] = jnp.zeros_like(s_ref)

    is_a, eye2 = _pair_consts()

    sub = lax.broadcasted_iota(jnp.int32, (SUBLANES, LANES), 0)

    def step(t8, states):
        t0 = pl.multiple_of(t8 * SUBLANES, SUBLANES)
        states = list(states)
        blocks = [[ref[pl.ds(t0, SUBLANES), pr * LANES:(pr + 1) * LANES]
                   for ref in (r_ref, w_ref, k_ref, v_ref, kk_ref, kka_ref)] for pr in range(N_PAIRS)]
        y_blk = [jnp.zeros((SUBLANES, LANES), F32) for _ in range(N_PAIRS)]
        for j in range(SUBLANES):
            for pr in range(N_PAIRS):
                rows = [blk[j:j + 1, :] for blk in blocks[pr]]
                states[pr], y_row = _rwkv_step(states[pr], *rows, is_a, eye2)
                y_blk[pr] = jnp.where(sub == j, y_row, y_blk[pr])
        for pr in range(N_PAIRS):
            y_ref[pl.ds(t0, SUBLANES), pr * LANES:(pr + 1) * LANES] = y_blk[pr]
        return tuple(states)

    states = lax.fori_loop(0, Lc // SUBLANES, step, tuple(s_ref[pr] for pr in range(N_PAIRS)))
    for pr in range(N_PAIRS):
        s_ref[pr] = states[pr]

    @pl.when(c == pl.num_programs(1) - 1)
    def _():
        sout_ref[0] = s_ref[...]


def _rwkv_scan_prompt(seqs, bsz, seq, Lc):
    nc = seq // Lc
    spec = pl.BlockSpec((Lc, R_W), lambda b, c: (b * nc + c, 0))
    return pl.pallas_call(
        functools.partial(_rwkv_scan_prompt_kernel, Lc=Lc),
        grid=(bsz, nc),
        in_specs=[spec] * 6,
        out_specs=[spec, pl.BlockSpec((1, N_PAIRS, R_HD, LANES), lambda b, c: (b, 0, 0, 0))],
        out_shape=[jax.ShapeDtypeStruct((bsz * seq, R_W), F32),
                   jax.ShapeDtypeStruct((bsz, N_PAIRS, R_HD, LANES), F32)],
        scratch_shapes=[pltpu.VMEM((N_PAIRS, R_HD, LANES), F32)],
        compiler_params=_cparams(("parallel", "arbitrary")),
        name="rwkv_scan_prompt",
    )(*seqs)


def _rwkv_scan_sample_kernel(r_ref, w_ref, k_ref, v_ref, kk_ref, kka_ref, s0_ref, y_ref, sout_ref, *, T, nb):
    is_a, eye2 = _pair_consts()

    def per_seq(b, carry):
        for pr in range(N_PAIRS):
            sl = slice(pr * LANES, (pr + 1) * LANES)
            s = s0_ref[b, pr]
            for t in range(T):
                rows = [ref[t, b, :, sl] for ref in (r_ref, w_ref, k_ref, v_ref, kk_ref, kka_ref)]
                s, y_row = _rwkv_step(s, *rows, is_a, eye2)
                y_ref[t, b, :, sl] = y_row
            sout_ref[b, pr] = s
        return carry

    lax.fori_loop(0, nb, per_seq, 0)


def _rwkv_scan_sample(seqs4, s0, nb=16):
    T, dbsz = seqs4[0].shape[:2]
    spec = pl.BlockSpec((T, nb, 1, R_W), lambda j: (0, j, 0, 0))
    sspec = pl.BlockSpec((nb, N_PAIRS, R_HD, LANES), lambda j: (j, 0, 0, 0))
    return pl.pallas_call(
        functools.partial(_rwkv_scan_sample_kernel, T=T, nb=nb),
        grid=(dbsz // nb,),
        in_specs=[spec] * 6 + [sspec],
        out_specs=[spec, sspec],
        out_shape=[jax.ShapeDtypeStruct((T, dbsz, 1, R_W), F32),
                   jax.ShapeDtypeStruct((dbsz, N_PAIRS, R_HD, LANES), F32)],
        compiler_params=_cparams(("parallel",)),
        name="rwkv_scan_sample",
    )(*seqs4, s0)


def _merge_kernel(x_ref, oa_ref, ob_ref, yr_ref, r_ref, k_ref, v_ref, g_ref, ga_ref, gb_ref, gc_ref,
                  bg_ref, lng_ref, lnb_ref, rk_ref, wpa_ref, wpb_ref, wpc_ref, wo_ref, o_ref):
    mean_m = _head_ones(R_W, 1.0 / R_HD)
    yr = yr_ref[...]
    d = yr - _dot_hi(yr, mean_m)
    var = _dot_hi(d * d, mean_m)
    yn = d * lax.rsqrt(var + GN_EPS) * lng_ref[...] + lnb_ref[...]
    v = v_ref[...]
    bonus = _dot_hi(r_ref[...] * k_ref[...] * rk_ref[...], _head_ones(R_W, 1.0))
    oc = (yn + bonus * v) * g_ref[...]
    merged = (_sigmoid(ga_ref[...] + bg_ref[:, 0:D_MODEL]) * _dot(oa_ref[...], wpa_ref[...])
              + _sigmoid(gb_ref[...] + bg_ref[:, D_MODEL:2 * D_MODEL]) * _dot(ob_ref[...], wpb_ref[...])
              + _sigmoid(gc_ref[...] + bg_ref[:, 2 * D_MODEL:]) * _dot(oc, wpc_ref[...]))
    o_ref[...] = x_ref[...] + _dot(merged, wo_ref[...])


def _merge(x, proj, oa, ob, yr, r, k2, v, g, bg, lng, lnb, rk, wpa, wpb, wpc, wo, tm):
    t = x.shape[0]
    row = lambda w: pl.BlockSpec((tm, w), lambda i: (i, 0))
    const = lambda shape: pl.BlockSpec(shape, lambda i: (0,) * len(shape))
    gcol = C_GATE // D_MODEL
    gate = lambda j: pl.BlockSpec((tm, D_MODEL), lambda i: (i, gcol + j))
    return pl.pallas_call(
        _merge_kernel,
        grid=(t // tm,),
        in_specs=[row(D_MODEL), row(A_W), row(S_W), row(R_W), row(R_W), row(R_W), row(R_W), row(R_W),
                  gate(0), gate(1), gate(2),
                  const((1, 3 * D_MODEL)), const((1, R_W)), const((1, R_W)), const((1, R_W)),
                  const((A_W, D_MODEL)), const((S_W, D_MODEL)), const((R_W, D_MODEL)),
                  const((D_MODEL, D_MODEL))],
        out_specs=row(D_MODEL),
        out_shape=jax.ShapeDtypeStruct((t, D_MODEL), F32),
        compiler_params=_cparams(("parallel",)),
        name="merge",
    )(x, oa, ob, yr, r, k2, v, g, proj, proj, proj, bg, lng, lnb, rk, wpa, wpb, wpc, wo)


def _ffn_kernel(x_ref, g_ref, wug_ref, wuv_ref, wd_ref, cw_ref, cb_ref, pre_ref, gf_ref,
                o_ref, tail_ref, xn_ref, ext_ref, *, tm, P, stride, tiles_per_seq, final_norm):
    i = pl.program_id(0)
    f = pl.program_id(1)
    first = (i % tiles_per_seq) == 0

    @pl.when(f == 0)
    def _():
        x = x_ref[...]
        ms = jnp.mean(x * x, axis=-1, keepdims=True)
        xn_ref[...] = (x * lax.rsqrt(ms + NORM_EPS) * g_ref[...]).astype(BF16)

    @pl.when(first)
    def _():
        ext_ref[f, 0:P, :] = pre_ref[...]

    @pl.when(jnp.logical_not(first))
    def _():
        ext_ref[f, 0:P, :] = ext_ref[f, tm:tm + P, :]

    xn = xn_ref[...]
    ug = jnp.dot(xn, wug_ref[...], preferred_element_type=F32)
    ext_ref[f, P:P + tm, :] = ug
    tail_ref[0] = ext_ref[f, tm:tm + P, :]
    acc = cb_ref[...] + cw_ref[F_CONV - 1:F_CONV, :] * ug
    for d in range(1, F_CONV):
        acc = acc + cw_ref[F_CONV - 1 - d:F_CONV - d, :] * ext_ref[f, P - d * stride:P - d * stride + tm, :]
    uv = jnp.dot(xn, wuv_ref[...], preferred_element_type=F32)
    contrib = _dot(_silu(acc) * uv, wd_ref[...])

    @pl.when(f == 0)
    def _():
        o_ref[...] = x_ref[...] + contrib

    @pl.when(f > 0)
    def _():
        o_ref[...] = o_ref[...] + contrib

    if final_norm:
        @pl.when(f == pl.num_programs(1) - 1)
        def _():
            y = o_ref[...]
            ms = jnp.mean(y * y, axis=-1, keepdims=True)
            o_ref[...] = y * lax.rsqrt(ms + NORM_EPS) * gf_ref[...]


def _ffn(x, g, wup, wd, cw, cb, pre, gf, tm, tf, stride, tiles_per_seq, final_norm):
    t = x.shape[0]
    P = pre.shape[0]
    nf = D_FF // tf
    return pl.pallas_call(
        functools.partial(_ffn_kernel, tm=tm, P=P, stride=stride, tiles_per_seq=tiles_per_seq,
                          final_norm=final_norm),
        grid=(t // tm, nf),
        in_specs=[pl.BlockSpec((tm, D_MODEL), lambda i, f: (i, 0)),
                  pl.BlockSpec((1, D_MODEL), lambda i, f: (0, 0)),
                  pl.BlockSpec((D_MODEL, tf), lambda i, f: (0, f)),
                  pl.BlockSpec((D_MODEL, tf), lambda i, f: (0, nf + f)),
                  pl.BlockSpec((tf, D_MODEL), lambda i, f: (f, 0)),
                  pl.BlockSpec((F_CONV, tf), lambda i, f: (0, f)),
                  pl.BlockSpec((1, tf), lambda i, f: (0, f)),
                  pl.BlockSpec((P, tf), lambda i, f: (0, f)),
                  pl.BlockSpec((1, D_MODEL), lambda i, f: (0, 0))],
        out_specs=[pl.BlockSpec((tm, D_MODEL), lambda i, f: (i, 0)),
                   pl.BlockSpec((1, P, tf), lambda i, f: (i, 0, f))],
        out_shape=[jax.ShapeDtypeStruct((t, D_MODEL), F32),
                   jax.ShapeDtypeStruct((t // tm, P, D_FF), F32)],
        scratch_shapes=[pltpu.VMEM((tm, D_MODEL), BF16), pltpu.VMEM((nf, tm + 2 * P, tf), F32)],
        compiler_params=_cparams(("arbitrary", "arbitrary")),
        name="conv_ffn",
    )(x, g, wup, wup, wd, cw, cb, pre, gf)


def _pack_rwkv_state(s):
    n = s.shape[0]
    return s.reshape(n, N_PAIRS, 2, R_HD, R_HD).transpose(0, 1, 3, 2, 4).reshape(n, N_PAIRS, R_HD, LANES)


def _unpack_rwkv_state(s):
    n = s.shape[0]
    return s.reshape(n, N_PAIRS, R_HD, 2, R_HD).transpose(0, 1, 3, 2, 4).reshape(n, R_HEADS, R_HD, R_HD)


def _prep_layer_params(l, p):
    w_in = p['w_in'][l]
    c_dt_src = 3 * A_W + S_W + S_CONV_CH
    c_rw_src = c_dt_src + S_HEADS
    c_gate_src = c_rw_src + R_IN_W
    w_proj = jnp.concatenate([
        w_in[:, c_rw_src:c_gate_src],
        w_in[:, c_dt_src:c_rw_src], jnp.zeros((D_MODEL, C_Q - C_DT - S_HEADS), F32),
        w_in[:, :c_dt_src],
        w_in[:, c_gate_src:]], axis=1).astype(BF16)
    pad_lane = lambda v: jnp.pad(v, (0, LANES - v.shape[0])).reshape(1, LANES)
    zl = lambda r0, w: jnp.zeros((R_LORA, R_W), F32).at[r0:r0 + w.shape[0]].set(w).astype(BF16)
    return dict(
        norm1_g=p['norm1_g'][l].reshape(1, D_MODEL), w_proj=w_proj,
        b_gate=p['b_gate'][l].reshape(1, 3 * D_MODEL),
        w_pa=p['w_pa'][l].astype(BF16), w_pb=p['w_pb'][l].astype(BF16), w_pc=p['w_pc'][l].astype(BF16),
        w_o=p['w_o'][l].astype(BF16),
        ssm_conv_w=p['ssm_conv_w'][l], ssm_conv_b=p['ssm_conv_b'][l].reshape(1, S_CONV_CH),
        ssm_dt_bias=pad_lane(p['ssm_dt_bias'][l]), ssm_a_log=pad_lane(p['ssm_a_log'][l]),
        ssm_dvec=jnp.repeat(p['ssm_d'][l], S_HD).reshape(1, S_W),
        ssm_norm_g=p['ssm_norm_g'][l].reshape(1, S_W),
        rw_mu=p['rw_mu'][l].reshape(1, R_IN_W), rw_w0=p['rw_w0'][l].reshape(1, R_W),
        rw_w2p=zl(0, p['rw_w2'][l]), rw_a0=p['rw_a0'][l].reshape(1, R_W),
        rw_a2p=zl(R_LORA_W, p['rw_a2'][l]), rw_g2p=zl(R_LORA_W + R_LORA_A, p['rw_g2'][l]),
        rw_kk=p['rw_kk'][l].reshape(1, R_W), rw_ka=p['rw_ka'][l].reshape(1, R_W),
        rw_rk=p['rw_rk'][l].reshape(1, R_W), rw_ln_g=p['rw_ln_g'][l].reshape(1, R_W),
        rw_ln_b=p['rw_ln_b'][l].reshape(1, R_W),
        norm2_g=p['norm2_g'][l].reshape(1, D_MODEL), w_up=p['w_up'][l].astype(BF16),
        w_down=p['w_down'][l].astype(BF16), ffn_conv_w=p['ffn_conv_w'][l],
        ffn_conv_b=p['ffn_conv_b'][l].reshape(1, D_FF))


def _row_tile(t):
    for tm in (512, 256, 128):
        if t % tm == 0:
            return tm
    raise ValueError(t)


def _ffn_tf():
    return D_FF // 2


def _prompt_layer(x, lp, bsz, seq, slopes_pair, gf, final_norm):
    t = bsz * seq
    tm = _row_tile(seq)
    tiles = seq // tm
    proj = _rms_matmul(x, lp['norm1_g'], lp['w_proj'], tm, 1024)
    proj3 = proj.reshape(bsz, seq, N_PROJ)
    oa = _moba_prompt(proj3, slopes_pair).reshape(t, A_W)
    ob, ssm_new = _ssd_prompt(proj, bsz, seq, lp['ssm_conv_w'], lp['ssm_conv_b'], lp['ssm_dt_bias'],
                              lp['ssm_a_log'], lp['ssm_dvec'], lp['ssm_norm_g'])
    r, w, k2, v, kk, kka, g = _rwkv_prep(
        proj, jnp.zeros((SUBLANES, R_IN_W), F32), lp['rw_mu'], lp['rw_w0'], lp['rw_w2p'], lp['rw_a0'],
        lp['rw_a2p'], lp['rw_g2p'], lp['rw_kk'], lp['rw_ka'], tm, 1, tiles)
    yr, rw_new = _rwkv_scan_prompt((r, w, k2, v, kk, kka), bsz, seq, min(seq, 512))
    x = _merge(x, proj, oa, ob, yr, r, k2, v, g, lp['b_gate'], lp['rw_ln_g'], lp['rw_ln_b'], lp['rw_rk'],
               lp['w_pa'], lp['w_pb'], lp['w_pc'], lp['w_o'], min(tm, 256))
    x, tail = _ffn(x, lp['norm2_g'], lp['w_up'], lp['w_down'], lp['ffn_conv_w'], lp['ffn_conv_b'],
                   jnp.zeros((SUBLANES, D_FF), F32), gf, tm, _ffn_tf(), 1, tiles, final_norm)
    k_new = proj3[:, :, C_K:C_K + A_W].reshape(bsz, seq, A_HEADS, A_HD)
    v_new = proj3[:, :, C_V:C_V + A_W].reshape(bsz, seq, A_HEADS, A_HD)
    ssm_conv_new = proj3[:, seq - (S_CONV - 1):, C_XBC:C_XBC + S_CONV_CH]
    shift_new = proj3[:, seq - 1:, C_RW:C_RW + R_IN_W]
    ffn_conv_new = tail.reshape(bsz, tiles, SUBLANES, D_FF)[:, tiles - 1, SUBLANES - (F_CONV - 1):]
    state = (k_new, v_new, ssm_new.reshape(bsz, S_HEADS, S_HD, S_STATE), ssm_conv_new,
             _unpack_rwkv_state(rw_new), shift_new, ffn_conv_new)
    return x, state


def _sample_layer(x, lp, dbsz, tnew, st, cache_k4, cache_v4, page_table, layer, gf, final_norm):
    t = tnew * dbsz
    ssm0, ssm_conv0, rwkv0, shift0, ffn_conv0 = st
    proj = _rms_matmul(x, lp['norm1_g'], lp['w_proj'], t, 1024)
    proj3 = proj.reshape(tnew, dbsz, N_PROJ)

    qkv = proj3[:, :, C_Q:C_Q + 3 * A_W].transpose(1, 0, 2)
    qkv8 = jnp.pad(qkv, ((0, 0), (0, SUBLANES - tnew), (0, 0)))
    oa8 = _moba_sample(qkv8[:, :, :A_W], qkv8[:, :, A_W:2 * A_W], qkv8[:, :, 2 * A_W:],
                       cache_k4, cache_v4, page_table, layer, tnew)
    oa = oa8[:, :tnew].transpose(1, 0, 2).reshape(t, A_W)

    pre4 = ssm_conv0.transpose(1, 0, 2).reshape(S_CONV - 1, dbsz, 1, S_CONV_CH)
    h0 = ssm0.reshape(dbsz, N_PAIRS, 2 * S_HD, S_STATE)
    ob4, ssm_new = _ssd_sample(proj.reshape(tnew, dbsz, 1, N_PROJ), pre4, h0, lp['ssm_conv_w'],
                               lp['ssm_conv_b'], lp['ssm_dt_bias'], lp['ssm_a_log'], lp['ssm_dvec'],
                               lp['ssm_norm_g'])
    ob = ob4.reshape(t, S_W)
    ssm_new = ssm_new.reshape(dbsz, S_HEADS, S_HD, S_STATE)

    r, w, k2, v, kk, kka, g = _rwkv_prep(
        proj, shift0.reshape(dbsz, R_IN_W), lp['rw_mu'], lp['rw_w0'], lp['rw_w2p'], lp['rw_a0'],
        lp['rw_a2p'], lp['rw_g2p'], lp['rw_kk'], lp['rw_ka'], t, dbsz, 1)
    to4 = lambda a: a.reshape(tnew, dbsz, 1, R_W)
    yr4, rw_new = _rwkv_scan_sample(tuple(to4(a) for a in (r, w, k2, v, kk, kka)), _pack_rwkv_state(rwkv0))
    yr = yr4.reshape(t, R_W)

    x = _merge(x, proj, oa, ob, yr, r, k2, v, g, lp['b_gate'], lp['rw_ln_g'], lp['rw_ln_b'], lp['rw_rk'],
               lp['w_pa'], lp['w_pb'], lp['w_pc'], lp['w_o'], min(t, 256))
    pre = ffn_conv0.transpose(1, 0, 2).reshape((F_CONV - 1) * dbsz, D_FF)
    x, tail = _ffn(x, lp['norm2_g'], lp['w_up'], lp['w_down'], lp['ffn_conv_w'], lp['ffn_conv_b'],
                   pre, gf, t, _ffn_tf(), dbsz, 1, final_norm)

    k_new = qkv[:, :, A_W:2 * A_W].reshape(dbsz, tnew, A_HEADS, A_HD)
    v_new = qkv[:, :, 2 * A_W:].reshape(dbsz, tnew, A_HEADS, A_HD)
    ssm_conv_new = proj3[tnew - (S_CONV - 1):, :, C_XBC:C_XBC + S_CONV_CH].transpose(1, 0, 2)
    shift_new = proj3[tnew - 1:, :, C_RW:C_RW + R_IN_W].transpose(1, 0, 2)
    ffn_conv_new = tail.reshape(F_CONV - 1, dbsz, D_FF).transpose(1, 0, 2)
    state = (k_new, v_new, ssm_new, ssm_conv_new, _unpack_rwkv_state(rw_new), shift_new, ffn_conv_new)
    return x, state


def kernel(x_prompt, x_sample, cache_k, cache_v, state_ssm, state_ssm_conv, state_rwkv, state_rwkv_shift, state_ffn_conv, page_table, norm1_g, w_in, b_gate, w_pa, ssm_conv_w, ssm_conv_b, ssm_dt_bias, ssm_a_log, ssm_d, ssm_norm_g, w_pb, rw_mu, rw_w0, rw_w2, rw_a0, rw_a2, rw_g2, rw_kk, rw_ka, rw_rk, rw_ln_g, rw_ln_b, w_pc, w_o, norm2_g, w_up, ffn_conv_w, ffn_conv_b, w_down, norm_f_g):
    params = dict(norm1_g=norm1_g, w_in=w_in, b_gate=b_gate, w_pa=w_pa, ssm_conv_w=ssm_conv_w,
                  ssm_conv_b=ssm_conv_b, ssm_dt_bias=ssm_dt_bias, ssm_a_log=ssm_a_log, ssm_d=ssm_d,
                  ssm_norm_g=ssm_norm_g, w_pb=w_pb, rw_mu=rw_mu, rw_w0=rw_w0, rw_w2=rw_w2, rw_a0=rw_a0,
                  rw_a2=rw_a2, rw_g2=rw_g2, rw_kk=rw_kk, rw_ka=rw_ka, rw_rk=rw_rk, rw_ln_g=rw_ln_g,
                  rw_ln_b=rw_ln_b, w_pc=w_pc, w_o=w_o, norm2_g=norm2_g, w_up=w_up, ffn_conv_w=ffn_conv_w,
                  ffn_conv_b=ffn_conv_b, w_down=w_down)
    depth = w_in.shape[0]
    bsz, seq, _ = x_prompt.shape
    dbsz, tnew, _ = x_sample.shape
    head = jnp.arange(A_HEADS, dtype=F32) + 1.0
    slopes = jnp.exp2(-8.0 * head / A_HEADS)
    slopes_pair = jnp.repeat(slopes, A_HD).reshape(N_PAIRS, 1, LANES)
    n_pool, page = cache_k.shape[1], cache_k.shape[2]
    cache_k4 = cache_k.reshape(depth, n_pool, page, A_W)
    cache_v4 = cache_v.reshape(depth, n_pool, page, A_W)
    gf = norm_f_g.reshape(1, D_MODEL)

    hp = x_prompt.reshape(bsz * seq, D_MODEL)
    hs = x_sample.transpose(1, 0, 2).reshape(tnew * dbsz, D_MODEL)
    new_p = [[] for _ in range(7)]
    new_s = [[] for _ in range(7)]
    for l in range(depth):
        lp = _prep_layer_params(l, params)
        last = l == depth - 1
        hp, sp = _prompt_layer(hp, lp, bsz, seq, slopes_pair, gf, last)
        st = (state_ssm[l], state_ssm_conv[l], state_rwkv[l], state_rwkv_shift[l], state_ffn_conv[l])
        hs, ss = _sample_layer(hs, lp, dbsz, tnew, st, cache_k4, cache_v4, page_table, l, gf, last)
        for j in range(7):
            new_p[j].append(sp[j])
            new_s[j].append(ss[j])
    y_prompt = hp.reshape(bsz, seq, D_MODEL)
    y_sample = hs.reshape(tnew, dbsz, D_MODEL).transpose(1, 0, 2)
    outs = [y_prompt, y_sample]
    for j in range(7):
        outs += [jnp.stack(new_p[j]), jnp.stack(new_s[j])]
    return tuple(outs)
```

```python
import functools
import math

import jax
import jax.numpy as jnp
from jax import lax
from jax.experimental import pallas as pl
from jax.experimental.pallas import tpu as pltpu

F32 = jnp.float32
BF16 = jnp.bfloat16

D_MODEL = 1024
A_HEADS = 8
A_HD = 64
A_W = A_HEADS * A_HD
MOBA_BLOCK = 256
MOBA_TOPK = 3
Q_BLOCK = MOBA_BLOCK
ATTN_SCALE = A_HD ** -0.5
S_HEADS = 8
S_HD = 64
S_W = S_HEADS * S_HD
S_GROUPS = 2
S_STATE = 128
S_CONV = 4
S_CONV_CH = S_W + 2 * S_GROUPS * S_STATE
SSD_CHUNK = 128
R_HEADS = 8
R_HD = 64
R_W = R_HEADS * R_HD
R_LORA_W = 64
R_LORA_A = 64
R_LORA_G = 128
R_LORA = R_LORA_W + R_LORA_A + R_LORA_G
R_IN_W = 3 * R_W + R_LORA
D_FF = ((8 * D_MODEL // 3 + 127) // 128) * 128
F_CONV = 3
NORM_EPS = 1e-6
GN_EPS = 64e-5
NEG_INF = -1e30

LANES = 128
SUBLANES = 8
HALF = 64
HALF_SHIFT = 6
SUBLANE_SHIFT = 3
N_PAIRS = 4

C_RW = 0
C_DT = R_IN_W
C_Q = 2048
C_K = C_Q + A_W
C_V = C_K + A_W
C_Z = C_V + A_W
C_XBC = C_Z + S_W
C_GATE = C_XBC + S_CONV_CH
N_PROJ = C_GATE + 3 * D_MODEL

VMEM_LIMIT = 56 * 1024 * 1024


def _cparams(sem):
    return pltpu.CompilerParams(dimension_semantics=sem, vmem_limit_bytes=VMEM_LIMIT)


def _dot(a, b):
    return jnp.dot(a.astype(BF16), b.astype(BF16), preferred_element_type=F32)


def _dot_nt(a, b):
    return lax.dot_general(a.astype(BF16), b.astype(BF16), (((1,), (1,)), ((), ())),
                           preferred_element_type=F32)


def _dot_tn(a, b):
    return lax.dot_general(a.astype(BF16), b.astype(BF16), (((0,), (0,)), ((), ())),
                           preferred_element_type=F32)


def _dot_hi(a, b):
    return jnp.dot(a, b, preferred_element_type=F32, precision=lax.Precision.HIGHEST)


def _dot_nt_hi(a, b):
    return lax.dot_general(a, b, (((1,), (1,)), ((), ())), preferred_element_type=F32,
                           precision=lax.Precision.HIGHEST)


def _sigmoid(x):
    return 1.0 / (1.0 + jnp.exp(-x))


def _silu(x):
    return x * _sigmoid(x)


def _softplus(x):
    return jnp.maximum(x, 0.0) + jnp.log(1.0 + jnp.exp(-jnp.abs(x)))


def _head_ones(width, scale):
    r = lax.shift_right_logical(lax.broadcasted_iota(jnp.int32, (width, width), 0), HALF_SHIFT)
    c = lax.shift_right_logical(lax.broadcasted_iota(jnp.int32, (width, width), 1), HALF_SHIFT)
    return jnp.where(r == c, scale, 0.0).astype(F32)


def _rms_matmul_kernel(x_ref, g_ref, w_ref, o_ref, xn_ref):
    @pl.when(pl.program_id(1) == 0)
    def _():
        x = x_ref[...]
        ms = jnp.mean(x * x, axis=-1, keepdims=True)
        xn_ref[...] = (x * lax.rsqrt(ms + NORM_EPS) * g_ref[...]).astype(BF16)

    o_ref[...] = jnp.dot(xn_ref[...], w_ref[...], preferred_element_type=F32)


def _rms_matmul(x, g, w, tm, tn):
    t, d = x.shape
    n = w.shape[1]
    return pl.pallas_call(
        _rms_matmul_kernel,
        grid=(t // tm, n // tn),
        in_specs=[pl.BlockSpec((tm, d), lambda i, j: (i, 0)),
                  pl.BlockSpec((1, d), lambda i, j: (0, 0)),
                  pl.BlockSpec((d, tn), lambda i, j: (0, j))],
        out_specs=pl.BlockSpec((tm, tn), lambda i, j: (i, j)),
        out_shape=jax.ShapeDtypeStruct((t, n), F32),
        scratch_shapes=[pltpu.VMEM((tm, d), BF16)],
        compiler_params=_cparams(("parallel", "arbitrary")),
        name="in_proj",
    )(x, g, w)


def _topk_bias(gate, n_valid):
    lane = lax.broadcasted_iota(jnp.int32, gate.shape, 1)
    lane_f = lane.astype(F32)
    gm = jnp.where(lane < n_valid, gate, NEG_INF)
    selected = jnp.zeros(gate.shape, jnp.bool_)
    for j in range(MOBA_TOPK):
        m = jnp.max(gm, axis=-1, keepdims=True)
        idx = jnp.min(jnp.where(gm == m, lane_f, 1e9), axis=-1, keepdims=True)
        hit = lane_f == idx
        selected = jnp.logical_or(selected, jnp.logical_and(hit, j < n_valid))
        gm = jnp.where(hit, -jnp.inf, gm)
    return jnp.where(selected, 0.0, NEG_INF).astype(F32)


def _moba_prompt_kernel(q_ref, k_ref, v_ref, sl_ref, o_ref, kb_ref, vb_ref, kmean_ref, *, nb):
    i = pl.program_id(2)

    @pl.when(i == 0)
    def _():
        kb_ref[...] = k_ref[0].astype(BF16)
        vb_ref[...] = v_ref[0].astype(BF16)
        kmean_ref[...] = jnp.zeros_like(kmean_ref)
        for n in range(nb):
            kmean_ref[n:n + 1, :] = jnp.mean(k_ref[0, n * MOBA_BLOCK:(n + 1) * MOBA_BLOCK, :],
                                             axis=0, keepdims=True)

    own = (i * Q_BLOCK) // MOBA_BLOCK
    q = q_ref[0]
    lane = lax.broadcasted_iota(jnp.int32, (1, LANES), 1)
    is_a = lane < HALF
    q_h = (jnp.where(is_a, q, 0.0), jnp.where(is_a, 0.0, q))
    slope = (sl_ref[0, :, 0:1], sl_ref[0, :, HALF:HALF + 1])
    kmean = kmean_ref[...]
    bias_h = [_topk_bias(_dot_nt_hi(q_h[h], kmean), own).astype(BF16) for h in range(2)]
    qb_h = [q_h[h].astype(BF16) for h in range(2)]

    rq = lax.broadcasted_iota(jnp.int32, (Q_BLOCK, 1), 0)
    ck = lax.broadcasted_iota(jnp.int32, (1, MOBA_BLOCK), 1)

    k_own = kb_ref[pl.ds(pl.multiple_of(own * MOBA_BLOCK, MOBA_BLOCK), MOBA_BLOCK), :]
    v_own = vb_ref[pl.ds(pl.multiple_of(own * MOBA_BLOCK, MOBA_BLOCK), MOBA_BLOCK), :]
    dist = (i * Q_BLOCK - own * MOBA_BLOCK) + rq - ck
    dist_f = dist.astype(F32)
    carry = []
    for h in range(2):
        s = _dot_nt(qb_h[h], k_own) * ATTN_SCALE - slope[h] * dist_f
        s = jnp.where(dist >= 0, s, NEG_INF)
        m = jnp.max(s, axis=-1, keepdims=True)
        p = jnp.exp(s - m)
        carry += [m, jnp.sum(p, axis=-1, keepdims=True), _dot(p, v_own)]

    sel_row = lax.broadcasted_iota(jnp.int32, (LANES, MOBA_BLOCK), 0)

    def body(n, c):
        start = pl.multiple_of(n * MOBA_BLOCK, MOBA_BLOCK)
        kn = kb_ref[pl.ds(start, MOBA_BLOCK), :]
        vn = vb_ref[pl.ds(start, MOBA_BLOCK), :]
        onehot = jnp.where(sel_row == n, 1.0, 0.0).astype(BF16)
        qoff = (i * Q_BLOCK - n * MOBA_BLOCK + rq).astype(F32)
        ck_f = ck.astype(F32)
        out = []
        for h in range(2):
            m0, l0, a0 = c[3 * h:3 * h + 3]
            s = (_dot_nt(qb_h[h], kn) * ATTN_SCALE + slope[h] * ck_f - slope[h] * qoff
                 + jnp.dot(bias_h[h], onehot, preferred_element_type=F32))
            m1 = jnp.maximum(m0, jnp.max(s, axis=-1, keepdims=True))
            alpha = jnp.exp(m0 - m1)
            p = jnp.exp(s - m1)
            out += [m1, alpha * l0 + jnp.sum(p, axis=-1, keepdims=True), alpha * a0 + _dot(p, vn)]
        return tuple(out)

    c = lax.fori_loop(0, own, body, tuple(carry))
    o_ref[0] = jnp.where(is_a, c[2] / c[1], c[5] / c[4])


def _moba_prompt(proj3, slopes_pair):
    bsz, seq, _ = proj3.shape
    assert seq % MOBA_BLOCK == 0 and seq // MOBA_BLOCK >= MOBA_TOPK
    nb = seq // MOBA_BLOCK
    nq = seq // Q_BLOCK
    qc, kc, vc = C_Q // LANES, C_K // LANES, C_V // LANES
    return pl.pallas_call(
        functools.partial(_moba_prompt_kernel, nb=nb),
        grid=(bsz, N_PAIRS, nq),
        in_specs=[pl.BlockSpec((1, Q_BLOCK, LANES), lambda b, p, i: (b, i, qc + p)),
                  pl.BlockSpec((1, seq, LANES), lambda b, p, i: (b, 0, kc + p)),
                  pl.BlockSpec((1, seq, LANES), lambda b, p, i: (b, 0, vc + p)),
                  pl.BlockSpec((1, 1, LANES), lambda b, p, i: (p, 0, 0))],
        out_specs=pl.BlockSpec((1, Q_BLOCK, LANES), lambda b, p, i: (b, i, p)),
        out_shape=jax.ShapeDtypeStruct((bsz, seq, A_W), F32),
        scratch_shapes=[pltpu.VMEM((seq, LANES), BF16), pltpu.VMEM((seq, LANES), BF16),
                        pltpu.VMEM((LANES, LANES), F32)],
        compiler_params=_cparams(("parallel", "parallel", "arbitrary")),
        name="moba_prompt",
    )(proj3, proj3, proj3, slopes_pair)


def _expand_heads(x):
    return jnp.concatenate([jnp.broadcast_to(x[h:h + 1, :], (SUBLANES, A_HD)) for h in range(A_HEADS)], axis=0)


def _moba_sample_kernel(pt_ref, q_ref, kn_ref, vn_ref, kp_ref, vp_ref, o_ref,
                        m_ref, l_ref, acc_ref, ksum_ref, *, n_pages, page, tnew):
    del pt_ref
    p = pl.program_id(1)
    past = n_pages * page
    rows = A_HEADS * SUBLANES
    cols = page * A_HEADS
    qh = q_ref[0] * ATTN_SCALE
    r = lax.broadcasted_iota(jnp.int32, (rows, 1), 0)
    head_r = lax.shift_right_logical(r, SUBLANE_SHIFT)
    tok = jnp.bitwise_and(r, SUBLANES - 1).astype(F32)
    slope = jnp.exp2(-(8.0 / A_HEADS) * (head_r + 1).astype(F32))

    k2 = kp_ref[0, 0].reshape(cols, A_HD)
    v2 = vp_ref[0, 0].reshape(cols, A_HD)
    c = lax.broadcasted_iota(jnp.int32, (1, cols), 1)
    key_f = lax.shift_right_logical(c, SUBLANE_SHIFT).astype(F32)
    head_c = jnp.bitwise_and(c, A_HEADS - 1)
    row_term = slope * ((p * page).astype(F32) - (past + tok))
    s = _dot_nt(qh, k2) + slope * key_f + row_term
    s = jnp.where(head_c == head_r, s, NEG_INF)
    m = jnp.max(s, axis=-1, keepdims=True)
    e = jnp.exp(s - m)
    m_ref[p] = jnp.broadcast_to(m, (rows, LANES))
    l_ref[p] = jnp.broadcast_to(jnp.sum(e, axis=-1, keepdims=True), (rows, LANES))
    acc_ref[p] = _dot(e, v2)
    ksum_ref[p] = jnp.sum(kp_ref[0, 0], axis=0)

    @pl.when(p == n_pages - 1)
    def _():
        ppb = MOBA_BLOCK // page
        n_blk = past // MOBA_BLOCK
        lane = lax.broadcasted_iota(jnp.int32, (rows, LANES), 1)
        gate = jnp.zeros((rows, LANES), F32)
        for n in range(n_blk):
            tot = ksum_ref[n * ppb]
            for j in range(1, ppb):
                tot = tot + ksum_ref[n * ppb + j]
            kmean = _expand_heads(tot * (1.0 / MOBA_BLOCK))
            gate = jnp.where(lane == n, jnp.sum(qh * kmean, axis=-1, keepdims=True), gate)
        bias = _topk_bias(gate, n_blk)

        s_own = []
        for j in range(tnew):
            sj = jnp.sum(qh * _expand_heads(kn_ref[0, j]), axis=-1, keepdims=True) - slope * (tok - j)
            s_own.append(jnp.where(tok >= j, sj, NEG_INF))
        mx = s_own[0]
        for j in range(1, tnew):
            mx = jnp.maximum(mx, s_own[j])
        mp = []
        for pg in range(n_pages):
            n = pg // ppb
            mpg = m_ref[pg][:, 0:1] + bias[:, n:n + 1]
            mp.append(mpg)
            mx = jnp.maximum(mx, mpg)
        lsum = jnp.zeros((rows, 1), F32)
        acc = jnp.zeros((rows, A_HD), F32)
        for j in range(tnew):
            w = jnp.exp(s_own[j] - mx)
            lsum = lsum + w
            acc = acc + w * _expand_heads(vn_ref[0, j])
        for pg in range(n_pages):
            w = jnp.exp(mp[pg] - mx)
            lsum = lsum + w * l_ref[pg][:, 0:1]
            acc = acc + w * acc_ref[pg]
        o_ref[0] = acc / lsum


def _moba_sample(qh, k8, v8, cache_k, cache_v, page_table, layer, tnew):
    dbsz = qh.shape[0]
    n_pages = page_table.shape[1]
    page = cache_k.shape[2]
    assert MOBA_BLOCK % page == 0 and (n_pages * page) % MOBA_BLOCK == 0
    assert (n_pages * page) // MOBA_BLOCK >= MOBA_TOPK and tnew <= SUBLANES
    rows = A_HEADS * SUBLANES
    q_spec = pl.BlockSpec((1, rows, A_HD), lambda b, p, pt: (b, 0, 0))
    tok_spec = pl.BlockSpec((1, SUBLANES, A_HEADS, A_HD), lambda b, p, pt: (b, 0, 0, 0))
    page_spec = pl.BlockSpec((1, 1, page, A_HEADS, A_HD),
                             lambda b, p, pt: (layer, pt[b * n_pages + p], 0, 0, 0))
    grid_spec = pltpu.PrefetchScalarGridSpec(
        num_scalar_prefetch=1,
        grid=(dbsz, n_pages),
        in_specs=[q_spec, tok_spec, tok_spec, page_spec, page_spec],
        out_specs=q_spec,
        scratch_shapes=[pltpu.VMEM((n_pages, rows, LANES), F32), pltpu.VMEM((n_pages, rows, LANES), F32),
                        pltpu.VMEM((n_pages, rows, A_HD), F32), pltpu.VMEM((n_pages, A_HEADS, A_HD), F32)],
    )
    return pl.pallas_call(
        functools.partial(_moba_sample_kernel, n_pages=n_pages, page=page, tnew=tnew),
        grid_spec=grid_spec,
        out_shape=jax.ShapeDtypeStruct((dbsz, rows, A_HD), F32),
        compiler_params=_cparams(("parallel", "arbitrary")),
        name="moba_sample",
    )(page_table.reshape(-1), qh, k8, v8, cache_k, cache_v)


def _ssd_prompt_kernel(z_ref, xbc_ref, dt_ref, cw_ref, cb_ref, dtb_ref, alog_ref, dvec_ref, ng_ref,
                       o_ref, hout_ref, ext_ref, h_ref, *, L):
    c = pl.program_id(1)
    P = SUBLANES

    @pl.when(c == 0)
    def _():
        ext_ref[0:P, :] = jnp.zeros((P, S_CONV_CH), F32)
        h_ref[...] = jnp.zeros_like(h_ref)

    @pl.when(c > 0)
    def _():
        ext_ref[0:P, :] = ext_ref[L:L + P, :]

    ext_ref[P:P + L, :] = xbc_ref[...]
    acc = cb_ref[...] + cw_ref[S_CONV - 1:S_CONV, :] * ext_ref[P:P + L, :]
    for d in range(1, S_CONV):
        acc = acc + cw_ref[S_CONV - 1 - d:S_CONV - d, :] * ext_ref[P - d:P - d + L, :]
    xbc = _silu(acc)
    xs = xbc[:, :S_W]
    bm = xbc[:, S_W:S_W + S_GROUPS * S_STATE]
    cm = xbc[:, S_W + S_GROUPS * S_STATE:]

    lane = lax.broadcasted_iota(jnp.int32, (1, LANES), 1)
    is_a = lane < HALF
    dt = jnp.where(lane < S_HEADS, _softplus(dt_ref[...] + dtb_ref[...]), 0.0)
    da = dt * (-jnp.exp(alog_ref[...]))
    rr = lax.broadcasted_iota(jnp.int32, (L, L), 0)
    cc = lax.broadcasted_iota(jnp.int32, (L, L), 1)
    causal = rr >= cc
    cum = _dot_hi(jnp.where(causal, 1.0, 0.0).astype(F32), da)
    cum_t = cum.T
    dt_t = dt.T
    row_a = lax.broadcasted_iota(jnp.int32, (2 * S_HD, 1), 0) < S_HD

    ys = []
    cb_g = [None] * S_GROUPS
    for pr in range(N_PAIRS):
        g = (2 * pr * S_GROUPS) // S_HEADS
        bg = bm[:, g * S_STATE:(g + 1) * S_STATE]
        cg = cm[:, g * S_STATE:(g + 1) * S_STATE]
        if cb_g[g] is None:
            cb_g[g] = _dot_nt(cg, bg)
        xs_p = xs[:, pr * LANES:(pr + 1) * LANES]
        heads = (2 * pr, 2 * pr + 1)
        halves = (is_a, jnp.logical_not(is_a))
        yp = jnp.zeros((L, LANES), F32)
        cum_c = [cum[:, h:h + 1] for h in heads]
        for h, half, cc_h in zip(heads, halves, cum_c):
            seg = cc_h - cum_t[h:h + 1, :]
            dec = jnp.exp(jnp.where(causal, seg, -jnp.inf))
            wts = cb_g[g] * dec * dt_t[h:h + 1, :]
            yp = yp + _dot(wts, jnp.where(half, xs_p, 0.0))
        hp = h_ref[pr]
        yp = yp + _dot_nt(cg, hp) * jnp.where(is_a, jnp.exp(cum_c[0]), jnp.exp(cum_c[1]))
        last = [cum[L - 1:L, h:h + 1] for h in heads]
        te = jnp.where(is_a, jnp.exp(last[0] - cum_c[0]) * dt[:, heads[0]:heads[0] + 1],
                       jnp.exp(last[1] - cum_c[1]) * dt[:, heads[1]:heads[1] + 1])
        st = _dot_tn(xs_p * te, bg)
        h_ref[pr] = hp * jnp.where(row_a, jnp.exp(last[0]), jnp.exp(last[1])) + st
        ys.append(yp)
    y = jnp.concatenate(ys, axis=1) + dvec_ref[...] * xs
    yz = y * _silu(z_ref[...])
    gw = S_W // S_GROUPS
    for g in range(S_GROUPS):
        part = yz[:, g * gw:(g + 1) * gw]
        ms = jnp.mean(part * part, axis=-1, keepdims=True)
        o_ref[:, g * gw:(g + 1) * gw] = part * lax.rsqrt(ms + NORM_EPS) * ng_ref[:, g * gw:(g + 1) * gw]

    @pl.when(c == pl.num_programs(1) - 1)
    def _():
        hout_ref[0] = h_ref[...]


def _ssd_prompt(proj, bsz, seq, cw, cb, dtb, alog, dvec, ng):
    L = SSD_CHUNK
    nc = seq // L
    const = lambda shape: pl.BlockSpec(shape, lambda b, c: (0,) * len(shape))
    return pl.pallas_call(
        functools.partial(_ssd_prompt_kernel, L=L),
        grid=(bsz, nc),
        in_specs=[pl.BlockSpec((L, S_W), lambda b, c: (b * nc + c, C_Z // S_W)),
                  pl.BlockSpec((L, S_CONV_CH), lambda b, c: (b * nc + c, C_XBC // S_CONV_CH)),
                  pl.BlockSpec((L, LANES), lambda b, c: (b * nc + c, C_DT // LANES)),
                  const((S_CONV, S_CONV_CH)), const((1, S_CONV_CH)), const((1, LANES)), const((1, LANES)),
                  const((1, S_W)), const((1, S_W))],
        out_specs=[pl.BlockSpec((L, S_W), lambda b, c: (b * nc + c, 0)),
                   pl.BlockSpec((1, N_PAIRS, 2 * S_HD, S_STATE), lambda b, c: (b, 0, 0, 0))],
        out_shape=[jax.ShapeDtypeStruct((bsz * seq, S_W), F32),
                   jax.ShapeDtypeStruct((bsz, N_PAIRS, 2 * S_HD, S_STATE), F32)],
        scratch_shapes=[pltpu.VMEM((L + 2 * SUBLANES, S_CONV_CH), F32),
                        pltpu.VMEM((N_PAIRS, 2 * S_HD, S_STATE), F32)],
        compiler_params=_cparams(("parallel", "arbitrary")),
        name="ssd_prompt",
    )(proj, proj, proj, cw, cb, dtb, alog, dvec, ng)


def _ssd_sample_kernel(z_ref, xbc_ref, dt_ref, pre_ref, h0_ref, cw_ref, cb_ref, dtb_ref, alog_ref,
                       dvec_ref, ng_ref, o_ref, hout_ref, *, T, nb):
    lane = lax.broadcasted_iota(jnp.int32, (1, LANES), 1)
    a_row = -jnp.exp(alog_ref[...])
    rows2 = 2 * S_HD
    eye = (lax.broadcasted_iota(jnp.int32, (rows2, LANES), 0)
           == lax.broadcasted_iota(jnp.int32, (rows2, LANES), 1)).astype(F32)
    row_a = lax.broadcasted_iota(jnp.int32, (rows2, 1), 0) < S_HD
    gw = S_W // S_GROUPS

    def per_seq(b, carry):
        up = [pre_ref[j, b] for j in range(S_CONV - 1)] + [xbc_ref[t, b] for t in range(T)]
        xc, dts = [], []
        for t in range(T):
            acc = cb_ref[...] + cw_ref[0:1, :] * up[t]
            for j in range(1, S_CONV):
                acc = acc + cw_ref[j:j + 1, :] * up[t + j]
            xc.append(_silu(acc))
            dts.append(jnp.where(lane < S_HEADS, _softplus(dt_ref[t, b] + dtb_ref[...]), 0.0))
        ys = [[] for _ in range(T)]
        for pr in range(N_PAIRS):
            g = (2 * pr * S_GROUPS) // S_HEADS
            ha, hb = 2 * pr, 2 * pr + 1
            hs = h0_ref[b, pr]
            for t in range(T):
                x_row = xc[t][:, pr * LANES:(pr + 1) * LANES]
                b_row = xc[t][:, S_W + g * S_STATE:S_W + (g + 1) * S_STATE]
                c_row = xc[t][:, S_W + (S_GROUPS + g) * S_STATE:S_W + (S_GROUPS + g + 1) * S_STATE]
                dt_col = jnp.where(row_a, dts[t][:, ha:ha + 1], dts[t][:, hb:hb + 1])
                a_col = jnp.where(row_a, a_row[:, ha:ha + 1], a_row[:, hb:hb + 1])
                x_col = jnp.sum(eye * x_row, axis=-1, keepdims=True)
                hs = hs * jnp.exp(dt_col * a_col) + (x_col * dt_col) * b_row
                y_col = jnp.sum(hs * c_row, axis=-1, keepdims=True)
                ys[t].append(jnp.sum(eye * y_col, axis=0, keepdims=True))
            hout_ref[b, pr] = hs
        for t in range(T):
            y = jnp.concatenate(ys[t], axis=1) + dvec_ref[...] * xc[t][:, :S_W]
            yz = y * _silu(z_ref[t, b])
            for g in range(S_GROUPS):
                part = yz[:, g * gw:(g + 1) * gw]
                ms = jnp.mean(part * part, axis=-1, keepdims=True)
                o_ref[t, b, :, g * gw:(g + 1) * gw] = (part * lax.rsqrt(ms + NORM_EPS)
                                                       * ng_ref[:, g * gw:(g + 1) * gw])
        return carry

    lax.fori_loop(0, nb, per_seq, 0)


def _ssd_sample(proj4, pre4, h0, cw, cb, dtb, alog, dvec, ng, nb=16):
    T, dbsz = proj4.shape[:2]
    const = lambda shape: pl.BlockSpec(shape, lambda j: (0,) * len(shape))
    hspec = pl.BlockSpec((nb, N_PAIRS, 2 * S_HD, S_STATE), lambda j: (j, 0, 0, 0))
    return pl.pallas_call(
        functools.partial(_ssd_sample_kernel, T=T, nb=nb),
        grid=(dbsz // nb,),
        in_specs=[pl.BlockSpec((T, nb, 1, S_W), lambda j: (0, j, 0, C_Z // S_W)),
                  pl.BlockSpec((T, nb, 1, S_CONV_CH), lambda j: (0, j, 0, C_XBC // S_CONV_CH)),
                  pl.BlockSpec((T, nb, 1, LANES), lambda j: (0, j, 0, C_DT // LANES)),
                  pl.BlockSpec((S_CONV - 1, nb, 1, S_CONV_CH), lambda j: (0, j, 0, 0)),
                  hspec,
                  const((S_CONV, S_CONV_CH)), const((1, S_CONV_CH)), const((1, LANES)), const((1, LANES)),
                  const((1, S_W)), const((1, S_W))],
        out_specs=[pl.BlockSpec((T, nb, 1, S_W), lambda j: (0, j, 0, 0)), hspec],
        out_shape=[jax.ShapeDtypeStruct((T, dbsz, 1, S_W), F32),
                   jax.ShapeDtypeStruct((dbsz, N_PAIRS, 2 * S_HD, S_STATE), F32)],
        compiler_params=_cparams(("parallel",)),
        name="ssd_sample",
    )(proj4, proj4, proj4, pre4, h0, cw, cb, dtb, alog, dvec, ng)


def _rwkv_prep_kernel(u_ref, pre_ref, mu_ref, w0_ref, w2_ref, a0_ref, a2_ref, g2_ref, kk_ref, ka_ref,
                      r_out, w_out, k_out, v_out, kk_out, kka_out, g_out, ext_ref,
                      *, tm, P, stride, tiles_per_seq):
    i = pl.program_id(0)
    first = (i % tiles_per_seq) == 0

    @pl.when(first)
    def _():
        ext_ref[0:P, :] = pre_ref[...]

    @pl.when(jnp.logical_not(first))
    def _():
        ext_ref[0:P, :] = ext_ref[tm:tm + P, :]

    u = u_ref[...]
    ext_ref[P:P + tm, :] = u
    prev = ext_ref[P - stride:P - stride + tm, :]
    x = u + (prev - u) * mu_ref[...]
    r = x[:, 0:R_W]
    kr = x[:, R_W:2 * R_W]
    vr = x[:, 2 * R_W:3 * R_W]
    xl = x[:, 3 * R_W:]
    w_log = w0_ref[...] + _dot(jnp.tanh(xl), w2_ref[...])
    log_decay = -jnp.exp(-_softplus(-w_log) - 0.5)
    a = _sigmoid(a0_ref[...] + _dot(xl, a2_ref[...]))
    g = _dot(_sigmoid(xl), g2_ref[...])
    kk = kr * kk_ref[...]
    ss = _dot_hi(kk * kk, _head_ones(R_W, 1.0))
    kk = kk / jnp.maximum(jnp.sqrt(ss), 1e-12)
    r_out[...] = r
    w_out[...] = log_decay
    k_out[...] = kr * (1.0 + (a - 1.0) * ka_ref[...])
    v_out[...] = vr
    kk_out[...] = kk
    kka_out[...] = kk * a
    g_out[...] = g


def _rwkv_prep(proj, pre, mu, w0, w2p, a0, a2p, g2p, kkw, kaw, tm, stride, tiles_per_seq):
    t = proj.shape[0]
    P = pre.shape[0]
    const = lambda shape: pl.BlockSpec(shape, lambda i: (0,) * len(shape))
    outs = pl.pallas_call(
        functools.partial(_rwkv_prep_kernel, tm=tm, P=P, stride=stride, tiles_per_seq=tiles_per_seq),
        grid=(t // tm,),
        in_specs=[pl.BlockSpec((tm, R_IN_W), lambda i: (i, 0)),
                  const((P, R_IN_W)), const((1, R_IN_W)), const((1, R_W)), const((R_LORA, R_W)),
                  const((1, R_W)), const((R_LORA, R_W)), const((R_LORA, R_W)), const((1, R_W)), const((1, R_W))],
        out_specs=[pl.BlockSpec((tm, R_W), lambda i: (i, 0))] * 7,
        out_shape=[jax.ShapeDtypeStruct((t, R_W), F32)] * 7,
        scratch_shapes=[pltpu.VMEM((tm + 2 * P, R_IN_W), F32)],
        compiler_params=_cparams(("arbitrary",)),
        name="rwkv_prep",
    )(proj, pre, mu, w0, w2p, a0, a2p, g2p, kkw, kaw)
    return outs


def _pair_consts():
    lane = lax.broadcasted_iota(jnp.int32, (R_HD, LANES), 1)
    row = lax.broadcasted_iota(jnp.int32, (R_HD, LANES), 0)
    is_a = lane < HALF
    eye2 = (jnp.bitwise_and(lane, HALF - 1) == row).astype(F32)
    return is_a, eye2


def _pair_sum(x, is_a):
    sa = jnp.sum(jnp.where(is_a, x, 0.0), axis=-1, keepdims=True)
    sb = jnp.sum(jnp.where(is_a, 0.0, x), axis=-1, keepdims=True)
    return jnp.where(is_a, sa, sb)


def _rwkv_step(s, rr, ww, kr, vv, kk, kka, is_a, eye2):
    v_col = _pair_sum(eye2 * vv, is_a)
    sk = _pair_sum(s * kk, is_a)
    s = s * ww - sk * kka + v_col * kr
    y_col = _pair_sum(s * rr, is_a)
    return s, jnp.sum(eye2 * y_col, axis=0, keepdims=True)


def _unit_lower_inverse(n):
    L = n.shape[0]
    eye = (lax.broadcasted_iota(jnp.int32, (L, L), 0) == lax.broadcasted_iota(jnp.int32, (L, L), 1)).astype(F32)
    t = eye + n
    pw = n
    for _ in range(int(math.log2(L)) - 1):
        pw = _dot(pw, pw)
        t = t + _dot(t, pw)
    return t


def _rwkv_chunk_prompt_kernel(r_ref, lw_ref, k_ref, v_ref, kk_ref, kka_ref, y_ref, sout_ref, s_ref, *, L):
    c = pl.program_id(1)

    @pl.when(c == 0)
    def _():
        s_ref[...] = jnp.zeros_like(s_ref)

    lane = lax.broadcasted_iota(jnp.int32, (1, LANES), 1)
    is_a = lane < HALF
    rr = lax.broadcasted_iota(jnp.int32, (L, L), 0)
    cc = lax.broadcasted_iota(jnp.int32, (L, L), 1)
    incl = rr >= cc
    strict = rr > cc
    tri = jnp.where(incl, 1.0, 0.0).astype(F32)
    r2 = lax.broadcasted_iota(jnp.int32, (LANES, LANES), 0) < HALF
    c2 = lax.broadcasted_iota(jnp.int32, (LANES, LANES), 1) < HALF
    same_head = r2 == c2

    for pr in range(N_PAIRS):
        sl = slice(pr * LANES, (pr + 1) * LANES)
        r, lw, k, v, kk, kka = (ref[:, sl] for ref in (r_ref, lw_ref, k_ref, v_ref, kk_ref, kka_ref))
        cum = _dot_hi(tri, lw)
        last = cum[L - 1:L, :]
        inv_p = jnp.exp(-cum)
        to_end = jnp.exp(last - cum)
        b_t = kk * jnp.exp(cum - lw)
        a_t = -kka * inv_p
        k_t = k * inv_p
        r_t = r * jnp.exp(cum)
        s0 = s_ref[pr]

        heads = []
        for half in (is_a, jnp.logical_not(is_a)):
            bm = jnp.where(half, b_t, 0.0)
            rm = jnp.where(half, r_t, 0.0)
            n = jnp.where(strict, _dot_nt(bm, a_t), 0.0)
            mk = jnp.where(strict, _dot_nt(bm, k_t), 0.0)
            ra = jnp.where(incl, _dot_nt(rm, a_t), 0.0)
            rk = jnp.where(incl, _dot_nt(rm, k_t), 0.0)
            heads.append((_unit_lower_inverse(n), mk, ra, rk))
        rhs = _dot_nt(b_t, s0) + jnp.where(is_a, _dot(heads[0][1], v), _dot(heads[1][1], v))
        u = jnp.where(is_a, _dot(heads[0][0], rhs), _dot(heads[1][0], rhs))
        y = _dot_nt(r_t, s0) + jnp.where(is_a, _dot(heads[0][2], u) + _dot(heads[0][3], v),
                                         _dot(heads[1][2], u) + _dot(heads[1][3], v))
        y_ref[:, sl] = y
        upd = _dot_tn(u, -kka * to_end) + _dot_tn(v, k * to_end)
        s_ref[pr] = s0 * jnp.exp(last) + jnp.where(same_head, upd, 0.0)

    @pl.when(c == pl.num_programs(1) - 1)
    def _():
        sout_ref[0] = s_ref[...]


def _rwkv_scan_prompt(seqs, bsz, seq, L):
    nc = seq // L
    spec = pl.BlockSpec((L, R_W), lambda b, c: (b * nc + c, 0))
    return pl.pallas_call(
        functools.partial(_rwkv_chunk_prompt_kernel, L=L),
        grid=(bsz, nc),
        in_specs=[spec] * 6,
        out_specs=[spec, pl.BlockSpec((1, N_PAIRS, LANES, LANES), lambda b, c: (b, 0, 0, 0))],
        out_shape=[jax.ShapeDtypeStruct((bsz * seq, R_W), F32),
                   jax.ShapeDtypeStruct((bsz, N_PAIRS, LANES, LANES), F32)],
        scratch_shapes=[pltpu.VMEM((N_PAIRS, LANES, LANES), F32)],
        compiler_params=_cparams(("parallel", "arbitrary")),
        name="rwkv_scan_prompt",
    )(*seqs)


def _rwkv_scan_sample_kernel(r_ref, w_ref, k_ref, v_ref, kk_ref, kka_ref, s0_ref, y_ref, sout_ref, *, T, nb):
    is_a, eye2 = _pair_consts()

    def per_seq(b, carry):
        for pr in range(N_PAIRS):
            sl = slice(pr * LANES, (pr + 1) * LANES)
            s = s0_ref[b, pr]
            for t in range(T):
                rows = [ref[t, b, :, sl] for ref in (r_ref, w_ref, k_ref, v_ref, kk_ref, kka_ref)]
                rows[1] = jnp.exp(rows[1])
                s, y_row = _rwkv_step(s, *rows, is_a, eye2)
                y_ref[t, b, :, sl] = y_row
            sout_ref[b, pr] = s
        return carry

    lax.fori_loop(0, nb, per_seq, 0)


def _rwkv_scan_sample(seqs4, s0, nb=16):
    T, dbsz = seqs4[0].shape[:2]
    spec = pl.BlockSpec((T, nb, 1, R_W), lambda j: (0, j, 0, 0))
    sspec = pl.BlockSpec((nb, N_PAIRS, R_HD, LANES), lambda j: (j, 0, 0, 0))
    return pl.pallas_call(
        functools.partial(_rwkv_scan_sample_kernel, T=T, nb=nb),
        grid=(dbsz // nb,),
        in_specs=[spec] * 6 + [sspec],
        out_specs=[spec, sspec],
        out_shape=[jax.ShapeDtypeStruct((T, dbsz, 1, R_W), F32),
                   jax.ShapeDtypeStruct((dbsz, N_PAIRS, R_HD, LANES), F32)],
        compiler_params=_cparams(("parallel",)),
        name="rwkv_scan_sample",
    )(*seqs4, s0)


def _merge_kernel(x_ref, oa_ref, ob_ref, yr_ref, r_ref, k_ref, v_ref, g_ref, ga_ref, gb_ref, gc_ref,
                  bg_ref, lng_ref, lnb_ref, rk_ref, wpa_ref, wpb_ref, wpc_ref, wo_ref, o_ref):
    mean_m = _head_ones(R_W, 1.0 / R_HD)
    yr = yr_ref[...]
    d = yr - _dot_hi(yr, mean_m)
    var = _dot_hi(d * d, mean_m)
    yn = d * lax.rsqrt(var + GN_EPS) * lng_ref[...] + lnb_ref[...]
    v = v_ref[...]
    bonus = _dot_hi(r_ref[...] * k_ref[...] * rk_ref[...], _head_ones(R_W, 1.0))
    oc = (yn + bonus * v) * g_ref[...]
    merged = (_sigmoid(ga_ref[...] + bg_ref[:, 0:D_MODEL]) * _dot(oa_ref[...], wpa_ref[...])
              + _sigmoid(gb_ref[...] + bg_ref[:, D_MODEL:2 * D_MODEL]) * _dot(ob_ref[...], wpb_ref[...])
              + _sigmoid(gc_ref[...] + bg_ref[:, 2 * D_MODEL:]) * _dot(oc, wpc_ref[...]))
    o_ref[...] = x_ref[...] + _dot(merged, wo_ref[...])


def _merge(x, proj, oa, ob, yr, r, k2, v, g, bg, lng, lnb, rk, wpa, wpb, wpc, wo, tm):
    t = x.shape[0]
    row = lambda w: pl.BlockSpec((tm, w), lambda i: (i, 0))
    const = lambda shape: pl.BlockSpec(shape, lambda i: (0,) * len(shape))
    gcol = C_GATE // D_MODEL
    gate = lambda j: pl.BlockSpec((tm, D_MODEL), lambda i: (i, gcol + j))
    return pl.pallas_call(
        _merge_kernel,
        grid=(t // tm,),
        in_specs=[row(D_MODEL), row(A_W), row(S_W), row(R_W), row(R_W), row(R_W), row(R_W), row(R_W),
                  gate(0), gate(1), gate(2),
                  const((1, 3 * D_MODEL)), const((1, R_W)), const((1, R_W)), const((1, R_W)),
                  const((A_W, D_MODEL)), const((S_W, D_MODEL)), const((R_W, D_MODEL)),
                  const((D_MODEL, D_MODEL))],
        out_specs=row(D_MODEL),
        out_shape=jax.ShapeDtypeStruct((t, D_MODEL), F32),
        compiler_params=_cparams(("parallel",)),
        name="merge",
    )(x, oa, ob, yr, r, k2, v, g, proj, proj, proj, bg, lng, lnb, rk, wpa, wpb, wpc, wo)


def _ffn_kernel(x_ref, g_ref, wug_ref, wuv_ref, wd_ref, cw_ref, cb_ref, pre_ref, gf_ref,
                o_ref, tail_ref, xn_ref, ext_ref, *, tm, P, stride, tiles_per_seq, final_norm):
    i = pl.program_id(0)
    f = pl.program_id(1)
    first = (i % tiles_per_seq) == 0

    @pl.when(f == 0)
    def _():
        x = x_ref[...]
        ms = jnp.mean(x * x, axis=-1, keepdims=True)
        xn_ref[...] = (x * lax.rsqrt(ms + NORM_EPS) * g_ref[...]).astype(BF16)

    @pl.when(first)
    def _():
        ext_ref[f, 0:P, :] = pre_ref[...]

    @pl.when(jnp.logical_not(first))
    def _():
        ext_ref[f, 0:P, :] = ext_ref[f, tm:tm + P, :]

    xn = xn_ref[...]
    ug = jnp.dot(xn, wug_ref[...], preferred_element_type=F32)
    ext_ref[f, P:P + tm, :] = ug
    tail_ref[0] = ext_ref[f, tm:tm + P, :]
    acc = cb_ref[...] + cw_ref[F_CONV - 1:F_CONV, :] * ug
    for d in range(1, F_CONV):
        acc = acc + cw_ref[F_CONV - 1 - d:F_CONV - d, :] * ext_ref[f, P - d * stride:P - d * stride + tm, :]
    uv = jnp.dot(xn, wuv_ref[...], preferred_element_type=F32)
    contrib = _dot(_silu(acc) * uv, wd_ref[...])

    @pl.when(f == 0)
    def _():
        o_ref[...] = x_ref[...] + contrib

    @pl.when(f > 0)
    def _():
        o_ref[...] = o_ref[...] + contrib

    if final_norm:
        @pl.when(f == pl.num_programs(1) - 1)
        def _():
            y = o_ref[...]
            ms = jnp.mean(y * y, axis=-1, keepdims=True)
            o_ref[...] = y * lax.rsqrt(ms + NORM_EPS) * gf_ref[...]


def _ffn(x, g, wup, wd, cw, cb, pre, gf, tm, tf, stride, tiles_per_seq, final_norm):
    t = x.shape[0]
    P = pre.shape[0]
    nf = D_FF // tf
    return pl.pallas_call(
        functools.partial(_ffn_kernel, tm=tm, P=P, stride=stride, tiles_per_seq=tiles_per_seq,
                          final_norm=final_norm),
        grid=(t // tm, nf),
        in_specs=[pl.BlockSpec((tm, D_MODEL), lambda i, f: (i, 0)),
                  pl.BlockSpec((1, D_MODEL), lambda i, f: (0, 0)),
                  pl.BlockSpec((D_MODEL, tf), lambda i, f: (0, f)),
                  pl.BlockSpec((D_MODEL, tf), lambda i, f: (0, nf + f)),
                  pl.BlockSpec((tf, D_MODEL), lambda i, f: (f, 0)),
                  pl.BlockSpec((F_CONV, tf), lambda i, f: (0, f)),
                  pl.BlockSpec((1, tf), lambda i, f: (0, f)),
                  pl.BlockSpec((P, tf), lambda i, f: (0, f)),
                  pl.BlockSpec((1, D_MODEL), lambda i, f: (0, 0))],
        out_specs=[pl.BlockSpec((tm, D_MODEL), lambda i, f: (i, 0)),
                   pl.BlockSpec((1, P, tf), lambda i, f: (i, 0, f))],
        out_shape=[jax.ShapeDtypeStruct((t, D_MODEL), F32),
                   jax.ShapeDtypeStruct((t // tm, P, D_FF), F32)],
        scratch_shapes=[pltpu.VMEM((tm, D_MODEL), BF16), pltpu.VMEM((nf, tm + 2 * P, tf), F32)],
        compiler_params=_cparams(("arbitrary", "arbitrary")),
        name="conv_ffn",
    )(x, g, wup, wup, wd, cw, cb, pre, gf)


def _pack_rwkv_state(s):
    n = s.shape[0]
    return s.reshape(n, N_PAIRS, 2, R_HD, R_HD).transpose(0, 1, 3, 2, 4).reshape(n, N_PAIRS, R_HD, LANES)


def _unpack_rwkv_state(s):
    n = s.shape[0]
    return s.reshape(n, N_PAIRS, R_HD, 2, R_HD).transpose(0, 1, 3, 2, 4).reshape(n, R_HEADS, R_HD, R_HD)


def _unpack_rwkv_blockdiag(s):
    n = s.shape[0]
    return jnp.stack([s[:, :, :R_HD, :R_HD], s[:, :, R_HD:, R_HD:]], axis=2).reshape(n, R_HEADS, R_HD, R_HD)


def _prep_layer_params(l, p):
    w_in = p['w_in'][l]
    c_dt_src = 3 * A_W + S_W + S_CONV_CH
    c_rw_src = c_dt_src + S_HEADS
    c_gate_src = c_rw_src + R_IN_W
    w_proj = jnp.concatenate([
        w_in[:, c_rw_src:c_gate_src],
        w_in[:, c_dt_src:c_rw_src], jnp.zeros((D_MODEL, C_Q - C_DT - S_HEADS), F32),
        w_in[:, :c_dt_src],
        w_in[:, c_gate_src:]], axis=1).astype(BF16)
    pad_lane = lambda v: jnp.pad(v, (0, LANES - v.shape[0])).reshape(1, LANES)
    zl = lambda r0, w: jnp.zeros((R_LORA, R_W), F32).at[r0:r0 + w.shape[0]].set(w).astype(BF16)
    return dict(
        norm1_g=p['norm1_g'][l].reshape(1, D_MODEL), w_proj=w_proj,
        b_gate=p['b_gate'][l].reshape(1, 3 * D_MODEL),
        w_pa=p['w_pa'][l].astype(BF16), w_pb=p['w_pb'][l].astype(BF16), w_pc=p['w_pc'][l].astype(BF16),
        w_o=p['w_o'][l].astype(BF16),
        ssm_conv_w=p['ssm_conv_w'][l], ssm_conv_b=p['ssm_conv_b'][l].reshape(1, S_CONV_CH),
        ssm_dt_bias=pad_lane(p['ssm_dt_bias'][l]), ssm_a_log=pad_lane(p['ssm_a_log'][l]),
        ssm_dvec=jnp.repeat(p['ssm_d'][l], S_HD).reshape(1, S_W),
        ssm_norm_g=p['ssm_norm_g'][l].reshape(1, S_W),
        rw_mu=p['rw_mu'][l].reshape(1, R_IN_W), rw_w0=p['rw_w0'][l].reshape(1, R_W),
        rw_w2p=zl(0, p['rw_w2'][l]), rw_a0=p['rw_a0'][l].reshape(1, R_W),
        rw_a2p=zl(R_LORA_W, p['rw_a2'][l]), rw_g2p=zl(R_LORA_W + R_LORA_A, p['rw_g2'][l]),
        rw_kk=p['rw_kk'][l].reshape(1, R_W), rw_ka=p['rw_ka'][l].reshape(1, R_W),
        rw_rk=p['rw_rk'][l].reshape(1, R_W), rw_ln_g=p['rw_ln_g'][l].reshape(1, R_W),
        rw_ln_b=p['rw_ln_b'][l].reshape(1, R_W),
        norm2_g=p['norm2_g'][l].reshape(1, D_MODEL), w_up=p['w_up'][l].astype(BF16),
        w_down=p['w_down'][l].astype(BF16), ffn_conv_w=p['ffn_conv_w'][l],
        ffn_conv_b=p['ffn_conv_b'][l].reshape(1, D_FF))


def _row_tile(t):
    for tm in (512, 256, 128):
        if t % tm == 0:
            return tm
    raise ValueError(t)


def _ffn_tf():
    return D_FF // 2


def _prompt_layer(x, lp, bsz, seq, slopes_pair, gf, final_norm):
    t = bsz * seq
    tm = _row_tile(seq)
    tiles = seq // tm
    proj = _rms_matmul(x, lp['norm1_g'], lp['w_proj'], tm, 1024)
    proj3 = proj.reshape(bsz, seq, N_PROJ)
    oa = _moba_prompt(proj3, slopes_pair).reshape(t, A_W)
    ob, ssm_new = _ssd_prompt(proj, bsz, seq, lp['ssm_conv_w'], lp['ssm_conv_b'], lp['ssm_dt_bias'],
                              lp['ssm_a_log'], lp['ssm_dvec'], lp['ssm_norm_g'])
    r, w, k2, v, kk, kka, g = _rwkv_prep(
        proj, jnp.zeros((SUBLANES, R_IN_W), F32), lp['rw_mu'], lp['rw_w0'], lp['rw_w2p'], lp['rw_a0'],
        lp['rw_a2p'], lp['rw_g2p'], lp['rw_kk'], lp['rw_ka'], tm, 1, tiles)
    yr, rw_new = _rwkv_scan_prompt((r, w, k2, v, kk, kka), bsz, seq, LANES)
    x = _merge(x, proj, oa, ob, yr, r, k2, v, g, lp['b_gate'], lp['rw_ln_g'], lp['rw_ln_b'], lp['rw_rk'],
               lp['w_pa'], lp['w_pb'], lp['w_pc'], lp['w_o'], min(tm, 256))
    x, tail = _ffn(x, lp['norm2_g'], lp['w_up'], lp['w_down'], lp['ffn_conv_w'], lp['ffn_conv_b'],
                   jnp.zeros((SUBLANES, D_FF), F32), gf, tm, _ffn_tf(), 1, tiles, final_norm)
    k_new = proj3[:, :, C_K:C_K + A_W].reshape(bsz, seq, A_HEADS, A_HD)
    v_new = proj3[:, :, C_V:C_V + A_W].reshape(bsz, seq, A_HEADS, A_HD)
    ssm_conv_new = proj3[:, seq - (S_CONV - 1):, C_XBC:C_XBC + S_CONV_CH]
    shift_new = proj3[:, seq - 1:, C_RW:C_RW + R_IN_W]
    ffn_conv_new = tail.reshape(bsz, tiles, SUBLANES, D_FF)[:, tiles - 1, SUBLANES - (F_CONV - 1):]
    state = (k_new, v_new, ssm_new.reshape(bsz, S_HEADS, S_HD, S_STATE), ssm_conv_new,
             _unpack_rwkv_blockdiag(rw_new), shift_new, ffn_conv_new)
    return x, state


def _sample_layer(x, lp, dbsz, tnew, st, cache_k4, cache_v4, page_table, layer, gf, final_norm):
    t = tnew * dbsz
    ssm0, ssm_conv0, rwkv0, shift0, ffn_conv0 = st
    proj = _rms_matmul(x, lp['norm1_g'], lp['w_proj'], t, 1024)
    proj3 = proj.reshape(tnew, dbsz, N_PROJ)

    qkv = proj3[:, :, C_Q:C_Q + 3 * A_W].transpose(1, 0, 2)
    qkv8 = jnp.pad(qkv, ((0, 0), (0, SUBLANES - tnew), (0, 0))).reshape(dbsz, SUBLANES, 3, A_HEADS, A_HD)
    qh = qkv8[:, :, 0].transpose(0, 2, 1, 3).reshape(dbsz, A_HEADS * SUBLANES, A_HD)
    oh = _moba_sample(qh, qkv8[:, :, 1], qkv8[:, :, 2], cache_k4, cache_v4, page_table, layer, tnew)
    oa = oh.reshape(dbsz, A_HEADS, SUBLANES, A_HD)[:, :, :tnew].transpose(2, 0, 1, 3).reshape(t, A_W)

    pre4 = ssm_conv0.transpose(1, 0, 2).reshape(S_CONV - 1, dbsz, 1, S_CONV_CH)
    h0 = ssm0.reshape(dbsz, N_PAIRS, 2 * S_HD, S_STATE)
    ob4, ssm_new = _ssd_sample(proj.reshape(tnew, dbsz, 1, N_PROJ), pre4, h0, lp['ssm_conv_w'],
                               lp['ssm_conv_b'], lp['ssm_dt_bias'], lp['ssm_a_log'], lp['ssm_dvec'],
                               lp['ssm_norm_g'])
    ob = ob4.reshape(t, S_W)
    ssm_new = ssm_new.reshape(dbsz, S_HEADS, S_HD, S_STATE)

    r, w, k2, v, kk, kka, g = _rwkv_prep(
        proj, shift0.reshape(dbsz, R_IN_W), lp['rw_mu'], lp['rw_w0'], lp['rw_w2p'], lp['rw_a0'],
        lp['rw_a2p'], lp['rw_g2p'], lp['rw_kk'], lp['rw_ka'], t, dbsz, 1)
    to4 = lambda a: a.reshape(tnew, dbsz, 1, R_W)
    yr4, rw_new = _rwkv_scan_sample(tuple(to4(a) for a in (r, w, k2, v, kk, kka)), _pack_rwkv_state(rwkv0))
    yr = yr4.reshape(t, R_W)

    x = _merge(x, proj, oa, ob, yr, r, k2, v, g, lp['b_gate'], lp['rw_ln_g'], lp['rw_ln_b'], lp['rw_rk'],
               lp['w_pa'], lp['w_pb'], lp['w_pc'], lp['w_o'], min(t, 256))
    pre = ffn_conv0.transpose(1, 0, 2).reshape((F_CONV - 1) * dbsz, D_FF)
    x, tail = _ffn(x, lp['norm2_g'], lp['w_up'], lp['w_down'], lp['ffn_conv_w'], lp['ffn_conv_b'],
                   pre, gf, t, _ffn_tf(), dbsz, 1, final_norm)

    k_new = qkv[:, :, A_W:2 * A_W].reshape(dbsz, tnew, A_HEADS, A_HD)
    v_new = qkv[:, :, 2 * A_W:].reshape(dbsz, tnew, A_HEADS, A_HD)
    ssm_conv_new = proj3[tnew - (S_CONV - 1):, :, C_XBC:C_XBC + S_CONV_CH].transpose(1, 0, 2)
    shift_new = proj3[tnew - 1:, :, C_RW:C_RW + R_IN_W].transpose(1, 0, 2)
    ffn_conv_new = tail.reshape(F_CONV - 1, dbsz, D_FF).transpose(1, 0, 2)
    state = (k_new, v_new, ssm_new, ssm_conv_new, _unpack_rwkv_state(rw_new), shift_new, ffn_conv_new)
    return x, state


def kernel(x_prompt, x_sample, cache_k, cache_v, state_ssm, state_ssm_conv, state_rwkv, state_rwkv_shift, state_ffn_conv, page_table, norm1_g, w_in, b_gate, w_pa, ssm_conv_w, ssm_conv_b, ssm_dt_bias, ssm_a_log, ssm_d, ssm_norm_g, w_pb, rw_mu, rw_w0, rw_w2, rw_a0, rw_a2, rw_g2, rw_kk, rw_ka, rw_rk, rw_ln_g, rw_ln_b, w_pc, w_o, norm2_g, w_up, ffn_conv_w, ffn_conv_b, w_down, norm_f_g):
    params = dict(norm1_g=norm1_g, w_in=w_in, b_gate=b_gate, w_pa=w_pa, ssm_conv_w=ssm_conv_w,
                  ssm_conv_b=ssm_conv_b, ssm_dt_bias=ssm_dt_bias, ssm_a_log=ssm_a_log, ssm_d=ssm_d,
                  ssm_norm_g=ssm_norm_g, w_pb=w_pb, rw_mu=rw_mu, rw_w0=rw_w0, rw_w2=rw_w2, rw_a0=rw_a0,
                  rw_a2=rw_a2, rw_g2=rw_g2, rw_kk=rw_kk, rw_ka=rw_ka, rw_rk=rw_rk, rw_ln_g=rw_ln_g,
                  rw_ln_b=rw_ln_b, w_pc=w_pc, w_o=w_o, norm2_g=norm2_g, w_up=w_up, ffn_conv_w=ffn_conv_w,
                  ffn_conv_b=ffn_conv_b, w_down=w_down)
    depth = w_in.shape[0]
    bsz, seq, _ = x_prompt.shape
    dbsz, tnew, _ = x_sample.shape
    head = jnp.arange(A_HEADS, dtype=F32) + 1.0
    slopes = jnp.exp2(-8.0 * head / A_HEADS)
    slopes_pair = jnp.repeat(slopes, A_HD).reshape(N_PAIRS, 1, LANES)
    cache_k4, cache_v4 = cache_k, cache_v
    gf = norm_f_g.reshape(1, D_MODEL)

    hp = x_prompt.reshape(bsz * seq, D_MODEL)
    hs = x_sample.transpose(1, 0, 2).reshape(tnew * dbsz, D_MODEL)
    new_p = [[] for _ in range(7)]
    new_s = [[] for _ in range(7)]
    for l in range(depth):
        lp = _prep_layer_params(l, params)
        last = l == depth - 1
        hp, sp = _prompt_layer(hp, lp, bsz, seq, slopes_pair, gf, last)
        st = (state_ssm[l], state_ssm_conv[l], state_rwkv[l], state_rwkv_shift[l], state_ffn_conv[l])
        hs, ss = _sample_layer(hs, lp, dbsz, tnew, st, cache_k4, cache_v4, page_table, l, gf, last)
        for j in range(7):
            new_p[j].append(sp[j])
            new_s[j].append(ss[j])
    y_prompt = hp.reshape(bsz, seq, D_MODEL)
    y_sample = hs.reshape(tnew, dbsz, D_MODEL).transpose(1, 0, 2)
    outs = [y_prompt, y_sample]
    for j in range(7):
        outs += [jnp.stack(new_p[j]), jnp.stack(new_s[j])]
    return tuple(outs)
```

```python
import functools
import math

import jax
import jax.numpy as jnp
from jax import lax
from jax.experimental import pallas as pl
from jax.experimental.pallas import tpu as pltpu

F32 = jnp.float32
BF16 = jnp.bfloat16

D_MODEL = 1024
A_HEADS = 8
A_HD = 64
A_W = A_HEADS * A_HD
MOBA_BLOCK = 256
MOBA_TOPK = 3
Q_BLOCK = MOBA_BLOCK
ATTN_SCALE = A_HD ** -0.5
S_HEADS = 8
S_HD = 64
S_W = S_HEADS * S_HD
S_GROUPS = 2
S_STATE = 128
S_CONV = 4
S_CONV_CH = S_W + 2 * S_GROUPS * S_STATE
SSD_CHUNK = 128
R_HEADS = 8
R_HD = 64
R_W = R_HEADS * R_HD
R_LORA_W = 64
R_LORA_A = 64
R_LORA_G = 128
R_LORA = R_LORA_W + R_LORA_A + R_LORA_G
R_IN_W = 3 * R_W + R_LORA
D_FF = ((8 * D_MODEL // 3 + 127) // 128) * 128
F_CONV = 3
NORM_EPS = 1e-6
GN_EPS = 64e-5
NEG_INF = -1e30

LANES = 128
SUBLANES = 8
HALF = 64
HALF_SHIFT = 6
SUBLANE_SHIFT = 3
N_PAIRS = 4

C_RW = 0
C_DT = R_IN_W
C_Q = 2048
C_K = C_Q + A_W
C_V = C_K + A_W
C_Z = C_V + A_W
C_XBC = C_Z + S_W
C_GATE = C_XBC + S_CONV_CH
N_PROJ = C_GATE + 3 * D_MODEL

VMEM_LIMIT = 56 * 1024 * 1024


def _cparams(sem):
    return pltpu.CompilerParams(dimension_semantics=sem, vmem_limit_bytes=VMEM_LIMIT)


def _dot(a, b):
    return jnp.dot(a.astype(BF16), b.astype(BF16), preferred_element_type=F32)


def _dot_nt(a, b):
    return lax.dot_general(a.astype(BF16), b.astype(BF16), (((1,), (1,)), ((), ())),
                           preferred_element_type=F32)


def _dot_tn(a, b):
    return lax.dot_general(a.astype(BF16), b.astype(BF16), (((0,), (0,)), ((), ())),
                           preferred_element_type=F32)


def _dot_hi(a, b):
    return jnp.dot(a, b, preferred_element_type=F32, precision=lax.Precision.HIGHEST)


def _dot_nt_hi(a, b):
    return lax.dot_general(a, b, (((1,), (1,)), ((), ())), preferred_element_type=F32,
                           precision=lax.Precision.HIGHEST)


def _sigmoid(x):
    return 1.0 / (1.0 + jnp.exp(-x))


def _silu(x):
    return x * _sigmoid(x)


def _softplus(x):
    return jnp.maximum(x, 0.0) + jnp.log(1.0 + jnp.exp(-jnp.abs(x)))


def _head_ones(width, scale):
    r = lax.shift_right_logical(lax.broadcasted_iota(jnp.int32, (width, width), 0), HALF_SHIFT)
    c = lax.shift_right_logical(lax.broadcasted_iota(jnp.int32, (width, width), 1), HALF_SHIFT)
    return jnp.where(r == c, scale, 0.0).astype(F32)


def _rms_matmul_kernel(x_ref, g_ref, w_ref, o_ref, xn_ref):
    @pl.when(pl.program_id(1) == 0)
    def _():
        x = x_ref[...]
        ms = jnp.mean(x * x, axis=-1, keepdims=True)
        xn_ref[...] = (x * lax.rsqrt(ms + NORM_EPS) * g_ref[...]).astype(BF16)

    o_ref[...] = jnp.dot(xn_ref[...], w_ref[...], preferred_element_type=F32)


def _rms_matmul(x, g, w, tm, tn):
    t, d = x.shape
    n = w.shape[1]
    return pl.pallas_call(
        _rms_matmul_kernel,
        grid=(t // tm, n // tn),
        in_specs=[pl.BlockSpec((tm, d), lambda i, j: (i, 0)),
                  pl.BlockSpec((1, d), lambda i, j: (0, 0)),
                  pl.BlockSpec((d, tn), lambda i, j: (0, j))],
        out_specs=pl.BlockSpec((tm, tn), lambda i, j: (i, j)),
        out_shape=jax.ShapeDtypeStruct((t, n), F32),
        scratch_shapes=[pltpu.VMEM((tm, d), BF16)],
        compiler_params=_cparams(("parallel", "arbitrary")),
        name="in_proj",
    )(x, g, w)


def _topk_bias(gate, n_valid, axis=1):
    pos = lax.broadcasted_iota(jnp.int32, gate.shape, axis)
    pos_f = pos.astype(F32)
    gm = jnp.where(pos < n_valid, gate, NEG_INF)
    selected = jnp.zeros(gate.shape, jnp.bool_)
    for j in range(MOBA_TOPK):
        m = jnp.max(gm, axis=axis, keepdims=True)
        idx = jnp.min(jnp.where(gm == m, pos_f, 1e9), axis=axis, keepdims=True)
        hit = pos_f == idx
        selected = jnp.logical_or(selected, jnp.logical_and(hit, j < n_valid))
        gm = jnp.where(hit, -jnp.inf, gm)
    return jnp.where(selected, 0.0, NEG_INF).astype(F32)


def _moba_prompt_kernel(q_ref, k_ref, v_ref, sl_ref, o_ref, kb_ref, vt_ref, kmean_ref, sel_ref, acc_ref, *, nb):
    i = pl.program_id(2)
    B = MOBA_BLOCK

    @pl.when(i == 0)
    def _():
        kb_ref[...] = k_ref[0].astype(BF16)
        kmean_ref[...] = jnp.zeros_like(kmean_ref)
        for n in range(nb):
            kmean_ref[n:n + 1, :] = jnp.mean(k_ref[0, n * B:(n + 1) * B, :], axis=0, keepdims=True)
            vt_ref[n] = v_ref[0, n * B:(n + 1) * B, :].T.astype(BF16)

    own = i
    q = q_ref[0]
    lane = lax.broadcasted_iota(jnp.int32, (1, LANES), 1)
    halves = (lane < HALF, lane >= HALF)
    log2e = 1.0 / math.log(2.0)
    slope2 = (sl_ref[0, :, 0:1] * log2e, sl_ref[0, :, HALF:HALF + 1] * log2e)
    kmean = kmean_ref[...]
    qb = []
    for h in range(2):
        q_m = jnp.where(halves[h], q, 0.0)
        bias_t = _topk_bias(_dot_nt_hi(kmean, q_m), own, axis=0)
        for n in range(nb):
            sel_ref[h, n] = jnp.broadcast_to(bias_t[n:n + 1, :], (SUBLANES, Q_BLOCK))
        qb.append((q_m * (ATTN_SCALE * log2e)).astype(BF16))

    rel = (lax.broadcasted_iota(jnp.int32, (B, Q_BLOCK), 0) - lax.broadcasted_iota(jnp.int32, (B, Q_BLOCK), 1))
    alibi = [slope2[h] * rel.astype(F32) for h in range(2)]
    row_a = lax.broadcasted_iota(jnp.int32, (LANES, 1), 0) < HALF

    start = pl.multiple_of(own * B, B)
    k_own = kb_ref[pl.ds(start, B), :]
    vt_own = vt_ref[own]
    carry = []
    for h in range(2):
        s = _dot_nt(k_own, qb[h]) + alibi[h]
        s = jnp.where(rel <= 0, s, NEG_INF)
        m = jnp.max(s, axis=0, keepdims=True)
        p = jnp.exp2(s - m)
        carry += [m, jnp.sum(p, axis=0, keepdims=True)]
        acc_ref[h] = jnp.dot(vt_own, p.astype(BF16), preferred_element_type=F32)

    def body(n, c):
        kn = kb_ref[pl.ds(pl.multiple_of(n * B, B), B), :]
        vtn = vt_ref[n]
        shift = ((n - i) * B).astype(F32)
        out = []
        for h in range(2):
            m0, l0 = c[2 * h:2 * h + 2]
            row = sel_ref[h, n][0:1, :] + slope2[h] * shift
            s = _dot_nt(kn, qb[h]) + alibi[h] + row
            m1 = jnp.maximum(m0, jnp.max(s, axis=0, keepdims=True))
            alpha = jnp.exp2(m0 - m1)
            p = jnp.exp2(s - m1)
            out += [m1, alpha * l0 + jnp.sum(p, axis=0, keepdims=True)]
            acc_ref[h] = alpha * acc_ref[h] + jnp.dot(vtn, p.astype(BF16), preferred_element_type=F32)
        return tuple(out)

    c = lax.fori_loop(0, own, body, tuple(carry))
    out_t = jnp.where(row_a, acc_ref[0] / c[1], acc_ref[1] / c[3])
    o_ref[0] = out_t.T


def _moba_prompt(proj3, slopes_pair):
    bsz, seq, _ = proj3.shape
    assert seq % MOBA_BLOCK == 0 and seq // MOBA_BLOCK >= MOBA_TOPK
    nb = seq // MOBA_BLOCK
    nbp = -(-nb // SUBLANES) * SUBLANES
    nq = seq // Q_BLOCK
    qc, kc, vc = C_Q // LANES, C_K // LANES, C_V // LANES
    return pl.pallas_call(
        functools.partial(_moba_prompt_kernel, nb=nb),
        grid=(bsz, N_PAIRS, nq),
        in_specs=[pl.BlockSpec((1, Q_BLOCK, LANES), lambda b, p, i: (b, i, qc + p)),
                  pl.BlockSpec((1, seq, LANES), lambda b, p, i: (b, 0, kc + p)),
                  pl.BlockSpec((1, seq, LANES), lambda b, p, i: (b, 0, vc + p)),
                  pl.BlockSpec((1, 1, LANES), lambda b, p, i: (p, 0, 0))],
        out_specs=pl.BlockSpec((1, Q_BLOCK, LANES), lambda b, p, i: (b, i, p)),
        out_shape=jax.ShapeDtypeStruct((bsz, seq, A_W), F32),
        scratch_shapes=[pltpu.VMEM((seq, LANES), BF16), pltpu.VMEM((nb, LANES, MOBA_BLOCK), BF16),
                        pltpu.VMEM((nbp, LANES), F32), pltpu.VMEM((2, nb, SUBLANES, Q_BLOCK), F32),
                        pltpu.VMEM((2, LANES, Q_BLOCK), F32)],
        compiler_params=_cparams(("parallel", "parallel", "arbitrary")),
        name="moba_prompt",
    )(proj3, proj3, proj3, slopes_pair)


def _moba_sample_kernel(pt_ref, q_ref, kn_ref, vn_ref, *refs, n_blk, ppb, page, tnew):
    del pt_ref
    n_pages = n_blk * ppb
    k_refs, v_refs = refs[:n_pages], refs[n_pages:2 * n_pages]
    o_ref, m_ref, l_ref, g_ref, acc_ref = refs[2 * n_pages:]
    past = n_blk * MOBA_BLOCK
    rows = A_HEADS * SUBLANES
    lane_head = lax.shift_right_logical(lax.broadcasted_iota(jnp.int32, (SUBLANES, A_W), 1), HALF_SHIFT)
    q8 = q_ref[0] * ATTN_SCALE
    qbd = jnp.concatenate([jnp.where(lane_head == h, q8, 0.0) for h in range(A_HEADS)], axis=0)
    q_hi = qbd.astype(BF16)
    q_lo = (qbd - q_hi.astype(F32)).astype(BF16)
    r = lax.broadcasted_iota(jnp.int32, (rows, 1), 0)
    tok = jnp.bitwise_and(r, SUBLANES - 1).astype(F32)
    slope = jnp.exp2(-(8.0 / A_HEADS) * (lax.shift_right_logical(r, SUBLANE_SHIFT) + 1).astype(F32))
    key = lax.broadcasted_iota(jnp.int32, (1, MOBA_BLOCK), 1).astype(F32)

    for n in range(n_blk):
        pages = range(n * ppb, (n + 1) * ppb)
        kt = jnp.concatenate([k_refs[j][0, 0].reshape(A_W, page) for j in pages], axis=1).astype(BF16)
        vt = jnp.concatenate([v_refs[j][0, 0].reshape(A_W, page) for j in pages], axis=1).astype(BF16)
        s_raw = jnp.dot(q_hi, kt, preferred_element_type=F32)
        s_fix = jnp.dot(q_lo, kt, preferred_element_type=F32)
        g_ref[n] = jnp.broadcast_to(jnp.sum(s_raw + s_fix, axis=-1, keepdims=True) * (1.0 / MOBA_BLOCK),
                                    (rows, LANES))
        s = s_raw - slope * ((past - n * MOBA_BLOCK + tok) - key)
        m = jnp.max(s, axis=-1, keepdims=True)
        e = jnp.exp(s - m)
        m_ref[n] = jnp.broadcast_to(m, (rows, LANES))
        l_ref[n] = jnp.broadcast_to(jnp.sum(e, axis=-1, keepdims=True), (rows, LANES))
        acc_ref[n] = lax.dot_general(e.astype(BF16), vt, (((1,), (1,)), ((), ())), preferred_element_type=F32)

    lane = lax.broadcasted_iota(jnp.int32, (rows, LANES), 1)
    gate = jnp.zeros((rows, LANES), F32)
    for j in range(n_blk):
        gate = jnp.where(lane == j, g_ref[j], gate)
    bias = _topk_bias(gate, n_blk)

    kn = kn_ref[0]
    vn = vn_ref[0]
    s_own = []
    for j in range(tnew):
        sj = jnp.sum(qbd * kn[j:j + 1, :], axis=-1, keepdims=True) - slope * (tok - j)
        s_own.append(jnp.where(tok >= j, sj, NEG_INF))
    mx = s_own[0]
    for j in range(1, tnew):
        mx = jnp.maximum(mx, s_own[j])
    mb = []
    for j in range(n_blk):
        mj = m_ref[j][:, 0:1] + bias[:, j:j + 1]
        mb.append(mj)
        mx = jnp.maximum(mx, mj)
    lsum = jnp.zeros((rows, 1), F32)
    acc = jnp.zeros((rows, A_W), F32)
    for j in range(tnew):
        w = jnp.exp(s_own[j] - mx)
        lsum = lsum + w
        acc = acc + w * vn[j:j + 1, :]
    for j in range(n_blk):
        w = jnp.exp(mb[j] - mx)
        lsum = lsum + w * l_ref[j][:, 0:1]
        acc = acc + w * acc_ref[j]
    out = acc / lsum
    o8 = jnp.zeros((SUBLANES, A_W), F32)
    for h in range(A_HEADS):
        o8 = o8 + jnp.where(lane_head == h, out[h * SUBLANES:(h + 1) * SUBLANES, :], 0.0)
    o_ref[0] = o8


def _moba_sample(q8, k8, v8, cache_kt, cache_vt, page_table, layer, tnew):
    dbsz = q8.shape[0]
    n_pages = page_table.shape[1]
    page = cache_kt.shape[4]
    assert MOBA_BLOCK % page == 0 and (n_pages * page) % MOBA_BLOCK == 0 and page % LANES == 0
    ppb = MOBA_BLOCK // page
    n_blk = n_pages // ppb
    assert n_blk >= MOBA_TOPK and tnew <= SUBLANES
    rows = A_HEADS * SUBLANES
    tok_spec = pl.BlockSpec((1, SUBLANES, A_W), lambda b, pt: (b, 0, 0))

    def page_spec(j):
        return pl.BlockSpec((1, 1, A_HEADS, A_HD, page), lambda b, pt: (layer, pt[b * n_pages + j], 0, 0, 0))

    pages = [page_spec(j) for j in range(n_pages)]
    grid_spec = pltpu.PrefetchScalarGridSpec(
        num_scalar_prefetch=1,
        grid=(dbsz,),
        in_specs=[tok_spec, tok_spec, tok_spec] + pages + pages,
        out_specs=tok_spec,
        scratch_shapes=[pltpu.VMEM((n_blk, rows, LANES), F32), pltpu.VMEM((n_blk, rows, LANES), F32),
                        pltpu.VMEM((n_blk, rows, LANES), F32), pltpu.VMEM((n_blk, rows, A_W), F32)],
    )
    return pl.pallas_call(
        functools.partial(_moba_sample_kernel, n_blk=n_blk, ppb=ppb, page=page, tnew=tnew),
        grid_spec=grid_spec,
        out_shape=jax.ShapeDtypeStruct((dbsz, SUBLANES, A_W), F32),
        compiler_params=_cparams(("parallel",)),
        name="moba_sample",
    )(page_table.reshape(-1), q8, k8, v8, *([cache_kt] * n_pages), *([cache_vt] * n_pages))


def _ssd_prompt_kernel(z_ref, xbc_ref, dt_ref, cw_ref, cb_ref, dtb_ref, alog_ref, dvec_ref, ng_ref,
                       o_ref, hout_ref, ext_ref, h_ref, *, L):
    c = pl.program_id(1)
    P = SUBLANES

    @pl.when(c == 0)
    def _():
        ext_ref[0:P, :] = jnp.zeros((P, S_CONV_CH), F32)
        h_ref[...] = jnp.zeros_like(h_ref)

    @pl.when(c > 0)
    def _():
        ext_ref[0:P, :] = ext_ref[L:L + P, :]

    ext_ref[P:P + L, :] = xbc_ref[...]
    acc = cb_ref[...] + cw_ref[S_CONV - 1:S_CONV, :] * ext_ref[P:P + L, :]
    for d in range(1, S_CONV):
        acc = acc + cw_ref[S_CONV - 1 - d:S_CONV - d, :] * ext_ref[P - d:P - d + L, :]
    xbc = _silu(acc)
    xs = xbc[:, :S_W]
    bm = xbc[:, S_W:S_W + S_GROUPS * S_STATE]
    cm = xbc[:, S_W + S_GROUPS * S_STATE:]

    lane = lax.broadcasted_iota(jnp.int32, (1, LANES), 1)
    is_a = lane < HALF
    dt = jnp.where(lane < S_HEADS, _softplus(dt_ref[...] + dtb_ref[...]), 0.0)
    da = dt * (-jnp.exp(alog_ref[...]))
    rr = lax.broadcasted_iota(jnp.int32, (L, L), 0)
    cc = lax.broadcasted_iota(jnp.int32, (L, L), 1)
    causal = rr >= cc
    cum = _dot_hi(jnp.where(causal, 1.0, 0.0).astype(F32), da)
    cum_t = cum.T
    dt_t = dt.T
    row_a = lax.broadcasted_iota(jnp.int32, (2 * S_HD, 1), 0) < S_HD

    ys = []
    cb_g = [None] * S_GROUPS
    for pr in range(N_PAIRS):
        g = (2 * pr * S_GROUPS) // S_HEADS
        bg = bm[:, g * S_STATE:(g + 1) * S_STATE]
        cg = cm[:, g * S_STATE:(g + 1) * S_STATE]
        if cb_g[g] is None:
            cb_g[g] = _dot_nt(cg, bg)
        xs_p = xs[:, pr * LANES:(pr + 1) * LANES]
        heads = (2 * pr, 2 * pr + 1)
        halves = (is_a, jnp.logical_not(is_a))
        yp = jnp.zeros((L, LANES), F32)
        cum_c = [cum[:, h:h + 1] for h in heads]
        for h, half, cc_h in zip(heads, halves, cum_c):
            seg = cc_h - cum_t[h:h + 1, :]
            dec = jnp.exp(jnp.where(causal, seg, -jnp.inf))
            wts = cb_g[g] * dec * dt_t[h:h + 1, :]
            yp = yp + _dot(wts, jnp.where(half, xs_p, 0.0))
        hp = h_ref[pr]
        yp = yp + _dot_nt(cg, hp) * jnp.where(is_a, jnp.exp(cum_c[0]), jnp.exp(cum_c[1]))
        last = [cum[L - 1:L, h:h + 1] for h in heads]
        te = jnp.where(is_a, jnp.exp(last[0] - cum_c[0]) * dt[:, heads[0]:heads[0] + 1],
                       jnp.exp(last[1] - cum_c[1]) * dt[:, heads[1]:heads[1] + 1])
        st = _dot_tn(xs_p * te, bg)
        h_ref[pr] = hp * jnp.where(row_a, jnp.exp(last[0]), jnp.exp(last[1])) + st
        ys.append(yp)
    y = jnp.concatenate(ys, axis=1) + dvec_ref[...] * xs
    yz = y * _silu(z_ref[...])
    gw = S_W // S_GROUPS
    for g in range(S_GROUPS):
        part = yz[:, g * gw:(g + 1) * gw]
        ms = jnp.mean(part * part, axis=-1, keepdims=True)
        o_ref[:, g * gw:(g + 1) * gw] = part * lax.rsqrt(ms + NORM_EPS) * ng_ref[:, g * gw:(g + 1) * gw]

    @pl.when(c == pl.num_programs(1) - 1)
    def _():
        hout_ref[0] = h_ref[...]


def _ssd_prompt(proj, bsz, seq, cw, cb, dtb, alog, dvec, ng):
    L = SSD_CHUNK
    nc = seq // L
    const = lambda shape: pl.BlockSpec(shape, lambda b, c: (0,) * len(shape))
    return pl.pallas_call(
        functools.partial(_ssd_prompt_kernel, L=L),
        grid=(bsz, nc),
        in_specs=[pl.BlockSpec((L, S_W), lambda b, c: (b * nc + c, C_Z // S_W)),
                  pl.BlockSpec((L, S_CONV_CH), lambda b, c: (b * nc + c, C_XBC // S_CONV_CH)),
                  pl.BlockSpec((L, LANES), lambda b, c: (b * nc + c, C_DT // LANES)),
                  const((S_CONV, S_CONV_CH)), const((1, S_CONV_CH)), const((1, LANES)), const((1, LANES)),
                  const((1, S_W)), const((1, S_W))],
        out_specs=[pl.BlockSpec((L, S_W), lambda b, c: (b * nc + c, 0)),
                   pl.BlockSpec((1, N_PAIRS, 2 * S_HD, S_STATE), lambda b, c: (b, 0, 0, 0))],
        out_shape=[jax.ShapeDtypeStruct((bsz * seq, S_W), F32),
                   jax.ShapeDtypeStruct((bsz, N_PAIRS, 2 * S_HD, S_STATE), F32)],
        scratch_shapes=[pltpu.VMEM((L + 2 * SUBLANES, S_CONV_CH), F32),
                        pltpu.VMEM((N_PAIRS, 2 * S_HD, S_STATE), F32)],
        compiler_params=_cparams(("parallel", "arbitrary")),
        name="ssd_prompt",
    )(proj, proj, proj, cw, cb, dtb, alog, dvec, ng)


def _ssd_sample_kernel(z_ref, xbc_ref, dt_ref, pre_ref, h0_ref, cw_ref, cb_ref, dtb_ref, alog_ref,
                       dvec_ref, ng_ref, o_ref, hout_ref, *, T, nb):
    lane = lax.broadcasted_iota(jnp.int32, (1, LANES), 1)
    a_row = -jnp.exp(alog_ref[...])
    rows2 = 2 * S_HD
    eye = (lax.broadcasted_iota(jnp.int32, (rows2, LANES), 0)
           == lax.broadcasted_iota(jnp.int32, (rows2, LANES), 1)).astype(F32)
    row_a = lax.broadcasted_iota(jnp.int32, (rows2, 1), 0) < S_HD
    gw = S_W // S_GROUPS

    def per_seq(b, carry):
        up = [pre_ref[j, b] for j in range(S_CONV - 1)] + [xbc_ref[t, b] for t in range(T)]
        xc, dts = [], []
        for t in range(T):
            acc = cb_ref[...] + cw_ref[0:1, :] * up[t]
            for j in range(1, S_CONV):
                acc = acc + cw_ref[j:j + 1, :] * up[t + j]
            xc.append(_silu(acc))
            dts.append(jnp.where(lane < S_HEADS, _softplus(dt_ref[t, b] + dtb_ref[...]), 0.0))
        ys = [[] for _ in range(T)]
        for pr in range(N_PAIRS):
            g = (2 * pr * S_GROUPS) // S_HEADS
            ha, hb = 2 * pr, 2 * pr + 1
            hs = h0_ref[b, pr]
            for t in range(T):
                x_row = xc[t][:, pr * LANES:(pr + 1) * LANES]
                b_row = xc[t][:, S_W + g * S_STATE:S_W + (g + 1) * S_STATE]
                c_row = xc[t][:, S_W + (S_GROUPS + g) * S_STATE:S_W + (S_GROUPS + g + 1) * S_STATE]
                dt_col = jnp.where(row_a, dts[t][:, ha:ha + 1], dts[t][:, hb:hb + 1])
                a_col = jnp.where(row_a, a_row[:, ha:ha + 1], a_row[:, hb:hb + 1])
                x_col = jnp.sum(eye * x_row, axis=-1, keepdims=True)
                hs = hs * jnp.exp(dt_col * a_col) + (x_col * dt_col) * b_row
                y_col = jnp.sum(hs * c_row, axis=-1, keepdims=True)
                ys[t].append(jnp.sum(eye * y_col, axis=0, keepdims=True))
            hout_ref[b, pr] = hs
        for t in range(T):
            y = jnp.concatenate(ys[t], axis=1) + dvec_ref[...] * xc[t][:, :S_W]
            yz = y * _silu(z_ref[t, b])
            for g in range(S_GROUPS):
                part = yz[:, g * gw:(g + 1) * gw]
                ms = jnp.mean(part * part, axis=-1, keepdims=True)
                o_ref[t, b, :, g * gw:(g + 1) * gw] = (part * lax.rsqrt(ms + NORM_EPS)
                                                       * ng_ref[:, g * gw:(g + 1) * gw])
        return carry

    lax.fori_loop(0, nb, per_seq, 0)


def _ssd_sample(proj4, pre4, h0, cw, cb, dtb, alog, dvec, ng, nb=16):
    T, dbsz = proj4.shape[:2]
    const = lambda shape: pl.BlockSpec(shape, lambda j: (0,) * len(shape))
    hspec = pl.BlockSpec((nb, N_PAIRS, 2 * S_HD, S_STATE), lambda j: (j, 0, 0, 0))
    return pl.pallas_call(
        functools.partial(_ssd_sample_kernel, T=T, nb=nb),
        grid=(dbsz // nb,),
        in_specs=[pl.BlockSpec((T, nb, 1, S_W), lambda j: (0, j, 0, C_Z // S_W)),
                  pl.BlockSpec((T, nb, 1, S_CONV_CH), lambda j: (0, j, 0, C_XBC // S_CONV_CH)),
                  pl.BlockSpec((T, nb, 1, LANES), lambda j: (0, j, 0, C_DT // LANES)),
                  pl.BlockSpec((S_CONV - 1, nb, 1, S_CONV_CH), lambda j: (0, j, 0, 0)),
                  hspec,
                  const((S_CONV, S_CONV_CH)), const((1, S_CONV_CH)), const((1, LANES)), const((1, LANES)),
                  const((1, S_W)), const((1, S_W))],
        out_specs=[pl.BlockSpec((T, nb, 1, S_W), lambda j: (0, j, 0, 0)), hspec],
        out_shape=[jax.ShapeDtypeStruct((T, dbsz, 1, S_W), F32),
                   jax.ShapeDtypeStruct((dbsz, N_PAIRS, 2 * S_HD, S_STATE), F32)],
        compiler_params=_cparams(("parallel",)),
        name="ssd_sample",
    )(proj4, proj4, proj4, pre4, h0, cw, cb, dtb, alog, dvec, ng)


def _rwkv_prep_kernel(u_ref, pre_ref, mu_ref, w0_ref, w2_ref, a0_ref, a2_ref, g2_ref, kk_ref, ka_ref,
                      r_out, w_out, k_out, v_out, kk_out, kka_out, g_out, ext_ref,
                      *, tm, P, stride, tiles_per_seq):
    i = pl.program_id(0)
    first = (i % tiles_per_seq) == 0

    @pl.when(first)
    def _():
        ext_ref[0:P, :] = pre_ref[...]

    @pl.when(jnp.logical_not(first))
    def _():
        ext_ref[0:P, :] = ext_ref[tm:tm + P, :]

    u = u_ref[...]
    ext_ref[P:P + tm, :] = u
    prev = ext_ref[P - stride:P - stride + tm, :]
    x = u + (prev - u) * mu_ref[...]
    r = x[:, 0:R_W]
    kr = x[:, R_W:2 * R_W]
    vr = x[:, 2 * R_W:3 * R_W]
    xl = x[:, 3 * R_W:]
    w_log = w0_ref[...] + _dot(jnp.tanh(xl), w2_ref[...])
    log_decay = -jnp.exp(-_softplus(-w_log) - 0.5)
    a = _sigmoid(a0_ref[...] + _dot(xl, a2_ref[...]))
    g = _dot(_sigmoid(xl), g2_ref[...])
    kk = kr * kk_ref[...]
    ss = _dot_hi(kk * kk, _head_ones(R_W, 1.0))
    kk = kk / jnp.maximum(jnp.sqrt(ss), 1e-12)
    r_out[...] = r
    w_out[...] = log_decay
    k_out[...] = kr * (1.0 + (a - 1.0) * ka_ref[...])
    v_out[...] = vr
    kk_out[...] = kk
    kka_out[...] = kk * a
    g_out[...] = g


def _rwkv_prep(proj, pre, mu, w0, w2p, a0, a2p, g2p, kkw, kaw, tm, stride, tiles_per_seq):
    t = proj.shape[0]
    P = pre.shape[0]
    const = lambda shape: pl.BlockSpec(shape, lambda i: (0,) * len(shape))
    outs = pl.pallas_call(
        functools.partial(_rwkv_prep_kernel, tm=tm, P=P, stride=stride, tiles_per_seq=tiles_per_seq),
        grid=(t // tm,),
        in_specs=[pl.BlockSpec((tm, R_IN_W), lambda i: (i, 0)),
                  const((P, R_IN_W)), const((1, R_IN_W)), const((1, R_W)), const((R_LORA, R_W)),
                  const((1, R_W)), const((R_LORA, R_W)), const((R_LORA, R_W)), const((1, R_W)), const((1, R_W))],
        out_specs=[pl.BlockSpec((tm, R_W), lambda i: (i, 0))] * 7,
        out_shape=[jax.ShapeDtypeStruct((t, R_W), F32)] * 7,
        scratch_shapes=[pltpu.VMEM((tm + 2 * P, R_IN_W), F32)],
        compiler_params=_cparams(("arbitrary",)),
        name="rwkv_prep",
    )(proj, pre, mu, w0, w2p, a0, a2p, g2p, kkw, kaw)
    return outs


def _pair_consts():
    lane = lax.broadcasted_iota(jnp.int32, (R_HD, LANES), 1)
    row = lax.broadcasted_iota(jnp.int32, (R_HD, LANES), 0)
    is_a = lane < HALF
    eye2 = (jnp.bitwise_and(lane, HALF - 1) == row).astype(F32)
    return is_a, eye2


def _pair_sum(x, is_a):
    sa = jnp.sum(jnp.where(is_a, x, 0.0), axis=-1, keepdims=True)
    sb = jnp.sum(jnp.where(is_a, 0.0, x), axis=-1, keepdims=True)
    return jnp.where(is_a, sa, sb)


def _rwkv_step(s, rr, ww, kr, vv, kk, kka, is_a, eye2):
    v_col = _pair_sum(eye2 * vv, is_a)
    sk = _pair_sum(s * kk, is_a)
    s = s * ww - sk * kka + v_col * kr
    y_col = _pair_sum(s * rr, is_a)
    return s, jnp.sum(eye2 * y_col, axis=0, keepdims=True)


def _unit_lower_inverse(n):
    L = n.shape[0]
    eye = (lax.broadcasted_iota(jnp.int32, (L, L), 0) == lax.broadcasted_iota(jnp.int32, (L, L), 1)).astype(F32)
    t = eye + n
    pw = n
    for _ in range(int(math.log2(L)) - 1):
        pw = _dot(pw, pw)
        t = t + _dot(t, pw)
    return t


def _rwkv_chunk_prompt_kernel(r_ref, lw_ref, k_ref, v_ref, kk_ref, kka_ref, y_ref, sout_ref, s_ref, *, L):
    c = pl.program_id(1)

    @pl.when(c == 0)
    def _():
        s_ref[...] = jnp.zeros_like(s_ref)

    lane = lax.broadcasted_iota(jnp.int32, (1, LANES), 1)
    is_a = lane < HALF
    rr = lax.broadcasted_iota(jnp.int32, (L, L), 0)
    cc = lax.broadcasted_iota(jnp.int32, (L, L), 1)
    incl = rr >= cc
    strict = rr > cc
    tri = jnp.where(incl, 1.0, 0.0).astype(F32)
    r2 = lax.broadcasted_iota(jnp.int32, (LANES, LANES), 0) < HALF
    c2 = lax.broadcasted_iota(jnp.int32, (LANES, LANES), 1) < HALF
    same_head = r2 == c2

    for pr in range(N_PAIRS):
        sl = slice(pr * LANES, (pr + 1) * LANES)
        r, lw, k, v, kk, kka = (ref[:, sl] for ref in (r_ref, lw_ref, k_ref, v_ref, kk_ref, kka_ref))
        cum = _dot_hi(tri, lw)
        last = cum[L - 1:L, :]
        inv_p = jnp.exp(-cum)
        to_end = jnp.exp(last - cum)
        b_t = kk * jnp.exp(cum - lw)
        a_t = -kka * inv_p
        k_t = k * inv_p
        r_t = r * jnp.exp(cum)
        s0 = s_ref[pr]

        heads = []
        for half in (is_a, jnp.logical_not(is_a)):
            bm = jnp.where(half, b_t, 0.0)
            rm = jnp.where(half, r_t, 0.0)
            n = jnp.where(strict, _dot_nt(bm, a_t), 0.0)
            mk = jnp.where(strict, _dot_nt(bm, k_t), 0.0)
            ra = jnp.where(incl, _dot_nt(rm, a_t), 0.0)
            rk = jnp.where(incl, _dot_nt(rm, k_t), 0.0)
            heads.append((_unit_lower_inverse(n), mk, ra, rk))
        rhs = _dot_nt(b_t, s0) + jnp.where(is_a, _dot(heads[0][1], v), _dot(heads[1][1], v))
        u = jnp.where(is_a, _dot(heads[0][0], rhs), _dot(heads[1][0], rhs))
        y = _dot_nt(r_t, s0) + jnp.where(is_a, _dot(heads[0][2], u) + _dot(heads[0][3], v),
                                         _dot(heads[1][2], u) + _dot(heads[1][3], v))
        y_ref[:, sl] = y
        upd = _dot_tn(u, -kka * to_end) + _dot_tn(v, k * to_end)
        s_ref[pr] = s0 * jnp.exp(last) + jnp.where(same_head, upd, 0.0)

    @pl.when(c == pl.num_programs(1) - 1)
    def _():
        sout_ref[0] = s_ref[...]


def _rwkv_scan_prompt(seqs, bsz, seq, L):
    nc = seq // L
    spec = pl.BlockSpec((L, R_W), lambda b, c: (b * nc + c, 0))
    return pl.pallas_call(
        functools.partial(_rwkv_chunk_prompt_kernel, L=L),
        grid=(bsz, nc),
        in_specs=[spec] * 6,
        out_specs=[spec, pl.BlockSpec((1, N_PAIRS, LANES, LANES), lambda b, c: (b, 0, 0, 0))],
        out_shape=[jax.ShapeDtypeStruct((bsz * seq, R_W), F32),
                   jax.ShapeDtypeStruct((bsz, N_PAIRS, LANES, LANES), F32)],
        scratch_shapes=[pltpu.VMEM((N_PAIRS, LANES, LANES), F32)],
        compiler_params=_cparams(("parallel", "arbitrary")),
        name="rwkv_scan_prompt",
    )(*seqs)


def _rwkv_scan_sample_kernel(r_ref, w_ref, k_ref, v_ref, kk_ref, kka_ref, s0_ref, y_ref, sout_ref, *, T, nb):
    is_a, eye2 = _pair_consts()

    def per_seq(b, carry):
        for pr in range(N_PAIRS):
            sl = slice(pr * LANES, (pr + 1) * LANES)
            s = s0_ref[b, pr]
            for t in range(T):
                rows = [ref[t, b, :, sl] for ref in (r_ref, w_ref, k_ref, v_ref, kk_ref, kka_ref)]
                rows[1] = jnp.exp(rows[1])
                s, y_row = _rwkv_step(s, *rows, is_a, eye2)
                y_ref[t, b, :, sl] = y_row
            sout_ref[b, pr] = s
        return carry

    lax.fori_loop(0, nb, per_seq, 0)


def _rwkv_scan_sample(seqs4, s0, nb=16):
    T, dbsz = seqs4[0].shape[:2]
    spec = pl.BlockSpec((T, nb, 1, R_W), lambda j: (0, j, 0, 0))
    sspec = pl.BlockSpec((nb, N_PAIRS, R_HD, LANES), lambda j: (j, 0, 0, 0))
    return pl.pallas_call(
        functools.partial(_rwkv_scan_sample_kernel, T=T, nb=nb),
        grid=(dbsz // nb,),
        in_specs=[spec] * 6 + [sspec],
        out_specs=[spec, sspec],
        out_shape=[jax.ShapeDtypeStruct((T, dbsz, 1, R_W), F32),
                   jax.ShapeDtypeStruct((dbsz, N_PAIRS, R_HD, LANES), F32)],
        compiler_params=_cparams(("parallel",)),
        name="rwkv_scan_sample",
    )(*seqs4, s0)


def _merge_kernel(x_ref, oa_ref, ob_ref, yr_ref, r_ref, k_ref, v_ref, g_ref, ga_ref, gb_ref, gc_ref,
                  bg_ref, lng_ref, lnb_ref, rk_ref, wpa_ref, wpb_ref, wpc_ref, wo_ref, o_ref):
    mean_m = _head_ones(R_W, 1.0 / R_HD)
    yr = yr_ref[...]
    d = yr - _dot_hi(yr, mean_m)
    var = _dot_hi(d * d, mean_m)
    yn = d * lax.rsqrt(var + GN_EPS) * lng_ref[...] + lnb_ref[...]
    v = v_ref[...]
    bonus = _dot_hi(r_ref[...] * k_ref[...] * rk_ref[...], _head_ones(R_W, 1.0))
    oc = (yn + bonus * v) * g_ref[...]
    merged = (_sigmoid(ga_ref[...] + bg_ref[:, 0:D_MODEL]) * _dot(oa_ref[...], wpa_ref[...])
              + _sigmoid(gb_ref[...] + bg_ref[:, D_MODEL:2 * D_MODEL]) * _dot(ob_ref[...], wpb_ref[...])
              + _sigmoid(gc_ref[...] + bg_ref[:, 2 * D_MODEL:]) * _dot(oc, wpc_ref[...]))
    o_ref[...] = x_ref[...] + _dot(merged, wo_ref[...])


def _merge(x, proj, oa, ob, yr, r, k2, v, g, bg, lng, lnb, rk, wpa, wpb, wpc, wo, tm):
    t = x.shape[0]
    row = lambda w: pl.BlockSpec((tm, w), lambda i: (i, 0))
    const = lambda shape: pl.BlockSpec(shape, lambda i: (0,) * len(shape))
    gcol = C_GATE // D_MODEL
    gate = lambda j: pl.BlockSpec((tm, D_MODEL), lambda i: (i, gcol + j))
    return pl.pallas_call(
        _merge_kernel,
        grid=(t // tm,),
        in_specs=[row(D_MODEL), row(A_W), row(S_W), row(R_W), row(R_W), row(R_W), row(R_W), row(R_W),
                  gate(0), gate(1), gate(2),
                  const((1, 3 * D_MODEL)), const((1, R_W)), const((1, R_W)), const((1, R_W)),
                  const((A_W, D_MODEL)), const((S_W, D_MODEL)), const((R_W, D_MODEL)),
                  const((D_MODEL, D_MODEL))],
        out_specs=row(D_MODEL),
        out_shape=jax.ShapeDtypeStruct((t, D_MODEL), F32),
        compiler_params=_cparams(("parallel",)),
        name="merge",
    )(x, oa, ob, yr, r, k2, v, g, proj, proj, proj, bg, lng, lnb, rk, wpa, wpb, wpc, wo)


def _ffn_kernel(x_ref, g_ref, wug_ref, wuv_ref, wd_ref, cw_ref, cb_ref, pre_ref, gf_ref,
                o_ref, tail_ref, xn_ref, ext_ref, *, tm, P, stride, tiles_per_seq, final_norm):
    i = pl.program_id(0)
    f = pl.program_id(1)
    first = (i % tiles_per_seq) == 0

    @pl.when(f == 0)
    def _():
        x = x_ref[...]
        ms = jnp.mean(x * x, axis=-1, keepdims=True)
        xn_ref[...] = (x * lax.rsqrt(ms + NORM_EPS) * g_ref[...]).astype(BF16)

    @pl.when(first)
    def _():
        ext_ref[f, 0:P, :] = pre_ref[...]

    @pl.when(jnp.logical_not(first))
    def _():
        ext_ref[f, 0:P, :] = ext_ref[f, tm:tm + P, :]

    xn = xn_ref[...]
    ug = jnp.dot(xn, wug_ref[...], preferred_element_type=F32)
    ext_ref[f, P:P + tm, :] = ug
    tail_ref[0] = ext_ref[f, tm:tm + P, :]
    acc = cb_ref[...] + cw_ref[F_CONV - 1:F_CONV, :] * ug
    for d in range(1, F_CONV):
        acc = acc + cw_ref[F_CONV - 1 - d:F_CONV - d, :] * ext_ref[f, P - d * stride:P - d * stride + tm, :]
    uv = jnp.dot(xn, wuv_ref[...], preferred_element_type=F32)
    contrib = _dot(_silu(acc) * uv, wd_ref[...])

    @pl.when(f == 0)
    def _():
        o_ref[...] = x_ref[...] + contrib

    @pl.when(f > 0)
    def _():
        o_ref[...] = o_ref[...] + contrib

    if final_norm:
        @pl.when(f == pl.num_programs(1) - 1)
        def _():
            y = o_ref[...]
            ms = jnp.mean(y * y, axis=-1, keepdims=True)
            o_ref[...] = y * lax.rsqrt(ms + NORM_EPS) * gf_ref[...]


def _ffn(x, g, wup, wd, cw, cb, pre, gf, tm, tf, stride, tiles_per_seq, final_norm):
    t = x.shape[0]
    P = pre.shape[0]
    nf = D_FF // tf
    return pl.pallas_call(
        functools.partial(_ffn_kernel, tm=tm, P=P, stride=stride, tiles_per_seq=tiles_per_seq,
                          final_norm=final_norm),
        grid=(t // tm, nf),
        in_specs=[pl.BlockSpec((tm, D_MODEL), lambda i, f: (i, 0)),
                  pl.BlockSpec((1, D_MODEL), lambda i, f: (0, 0)),
                  pl.BlockSpec((D_MODEL, tf), lambda i, f: (0, f)),
                  pl.BlockSpec((D_MODEL, tf), lambda i, f: (0, nf + f)),
                  pl.BlockSpec((tf, D_MODEL), lambda i, f: (f, 0)),
                  pl.BlockSpec((F_CONV, tf), lambda i, f: (0, f)),
                  pl.BlockSpec((1, tf), lambda i, f: (0, f)),
                  pl.BlockSpec((P, tf), lambda i, f: (0, f)),
                  pl.BlockSpec((1, D_MODEL), lambda i, f: (0, 0))],
        out_specs=[pl.BlockSpec((tm, D_MODEL), lambda i, f: (i, 0)),
                   pl.BlockSpec((1, P, tf), lambda i, f: (i, 0, f))],
        out_shape=[jax.ShapeDtypeStruct((t, D_MODEL), F32),
                   jax.ShapeDtypeStruct((t // tm, P, D_FF), F32)],
        scratch_shapes=[pltpu.VMEM((tm, D_MODEL), BF16), pltpu.VMEM((nf, tm + 2 * P, tf), F32)],
        compiler_params=_cparams(("arbitrary", "arbitrary")),
        name="conv_ffn",
    )(x, g, wup, wup, wd, cw, cb, pre, gf)


def _pack_rwkv_state(s):
    n = s.shape[0]
    return s.reshape(n, N_PAIRS, 2, R_HD, R_HD).transpose(0, 1, 3, 2, 4).reshape(n, N_PAIRS, R_HD, LANES)


def _unpack_rwkv_state(s):
    n = s.shape[0]
    return s.reshape(n, N_PAIRS, R_HD, 2, R_HD).transpose(0, 1, 3, 2, 4).reshape(n, R_HEADS, R_HD, R_HD)


def _unpack_rwkv_blockdiag(s):
    n = s.shape[0]
    return jnp.stack([s[:, :, :R_HD, :R_HD], s[:, :, R_HD:, R_HD:]], axis=2).reshape(n, R_HEADS, R_HD, R_HD)


def _prep_layer_params(l, p):
    w_in = p['w_in'][l]
    c_dt_src = 3 * A_W + S_W + S_CONV_CH
    c_rw_src = c_dt_src + S_HEADS
    c_gate_src = c_rw_src + R_IN_W
    w_proj = jnp.concatenate([
        w_in[:, c_rw_src:c_gate_src],
        w_in[:, c_dt_src:c_rw_src], jnp.zeros((D_MODEL, C_Q - C_DT - S_HEADS), F32),
        w_in[:, :c_dt_src],
        w_in[:, c_gate_src:]], axis=1).astype(BF16)
    pad_lane = lambda v: jnp.pad(v, (0, LANES - v.shape[0])).reshape(1, LANES)
    zl = lambda r0, w: jnp.zeros((R_LORA, R_W), F32).at[r0:r0 + w.shape[0]].set(w).astype(BF16)
    return dict(
        norm1_g=p['norm1_g'][l].reshape(1, D_MODEL), w_proj=w_proj,
        b_gate=p['b_gate'][l].reshape(1, 3 * D_MODEL),
        w_pa=p['w_pa'][l].astype(BF16), w_pb=p['w_pb'][l].astype(BF16), w_pc=p['w_pc'][l].astype(BF16),
        w_o=p['w_o'][l].astype(BF16),
        ssm_conv_w=p['ssm_conv_w'][l], ssm_conv_b=p['ssm_conv_b'][l].reshape(1, S_CONV_CH),
        ssm_dt_bias=pad_lane(p['ssm_dt_bias'][l]), ssm_a_log=pad_lane(p['ssm_a_log'][l]),
        ssm_dvec=jnp.repeat(p['ssm_d'][l], S_HD).reshape(1, S_W),
        ssm_norm_g=p['ssm_norm_g'][l].reshape(1, S_W),
        rw_mu=p['rw_mu'][l].reshape(1, R_IN_W), rw_w0=p['rw_w0'][l].reshape(1, R_W),
        rw_w2p=zl(0, p['rw_w2'][l]), rw_a0=p['rw_a0'][l].reshape(1, R_W),
        rw_a2p=zl(R_LORA_W, p['rw_a2'][l]), rw_g2p=zl(R_LORA_W + R_LORA_A, p['rw_g2'][l]),
        rw_kk=p['rw_kk'][l].reshape(1, R_W), rw_ka=p['rw_ka'][l].reshape(1, R_W),
        rw_rk=p['rw_rk'][l].reshape(1, R_W), rw_ln_g=p['rw_ln_g'][l].reshape(1, R_W),
        rw_ln_b=p['rw_ln_b'][l].reshape(1, R_W),
        norm2_g=p['norm2_g'][l].reshape(1, D_MODEL), w_up=p['w_up'][l].astype(BF16),
        w_down=p['w_down'][l].astype(BF16), ffn_conv_w=p['ffn_conv_w'][l],
        ffn_conv_b=p['ffn_conv_b'][l].reshape(1, D_FF))


def _row_tile(t):
    for tm in (512, 256, 128):
        if t % tm == 0:
            return tm
    raise ValueError(t)


def _ffn_tf():
    return D_FF // 2


def _prompt_layer(x, lp, bsz, seq, slopes_pair, gf, final_norm):
    t = bsz * seq
    tm = _row_tile(seq)
    tiles = seq // tm
    proj = _rms_matmul(x, lp['norm1_g'], lp['w_proj'], tm, 1024)
    proj3 = proj.reshape(bsz, seq, N_PROJ)
    oa = _moba_prompt(proj3, slopes_pair).reshape(t, A_W)
    ob, ssm_new = _ssd_prompt(proj, bsz, seq, lp['ssm_conv_w'], lp['ssm_conv_b'], lp['ssm_dt_bias'],
                              lp['ssm_a_log'], lp['ssm_dvec'], lp['ssm_norm_g'])
    r, w, k2, v, kk, kka, g = _rwkv_prep(
        proj, jnp.zeros((SUBLANES, R_IN_W), F32), lp['rw_mu'], lp['rw_w0'], lp['rw_w2p'], lp['rw_a0'],
        lp['rw_a2p'], lp['rw_g2p'], lp['rw_kk'], lp['rw_ka'], tm, 1, tiles)
    yr, rw_new = _rwkv_scan_prompt((r, w, k2, v, kk, kka), bsz, seq, LANES)
    x = _merge(x, proj, oa, ob, yr, r, k2, v, g, lp['b_gate'], lp['rw_ln_g'], lp['rw_ln_b'], lp['rw_rk'],
               lp['w_pa'], lp['w_pb'], lp['w_pc'], lp['w_o'], min(tm, 256))
    x, tail = _ffn(x, lp['norm2_g'], lp['w_up'], lp['w_down'], lp['ffn_conv_w'], lp['ffn_conv_b'],
                   jnp.zeros((SUBLANES, D_FF), F32), gf, tm, _ffn_tf(), 1, tiles, final_norm)
    k_new = proj3[:, :, C_K:C_K + A_W].reshape(bsz, seq, A_HEADS, A_HD)
    v_new = proj3[:, :, C_V:C_V + A_W].reshape(bsz, seq, A_HEADS, A_HD)
    ssm_conv_new = proj3[:, seq - (S_CONV - 1):, C_XBC:C_XBC + S_CONV_CH]
    shift_new = proj3[:, seq - 1:, C_RW:C_RW + R_IN_W]
    ffn_conv_new = tail.reshape(bsz, tiles, SUBLANES, D_FF)[:, tiles - 1, SUBLANES - (F_CONV - 1):]
    state = (k_new, v_new, ssm_new.reshape(bsz, S_HEADS, S_HD, S_STATE), ssm_conv_new,
             _unpack_rwkv_blockdiag(rw_new), shift_new, ffn_conv_new)
    return x, state


def _sample_layer(x, lp, dbsz, tnew, st, cache_k4, cache_v4, page_table, layer, gf, final_norm):
    t = tnew * dbsz
    ssm0, ssm_conv0, rwkv0, shift0, ffn_conv0 = st
    proj = _rms_matmul(x, lp['norm1_g'], lp['w_proj'], t, 1024)
    proj3 = proj.reshape(tnew, dbsz, N_PROJ)

    qkv = proj3[:, :, C_Q:C_Q + 3 * A_W].transpose(1, 0, 2)
    qkv8 = jnp.pad(qkv, ((0, 0), (0, SUBLANES - tnew), (0, 0)))
    oa8 = _moba_sample(qkv8[:, :, :A_W], qkv8[:, :, A_W:2 * A_W], qkv8[:, :, 2 * A_W:],
                       cache_k4, cache_v4, page_table, layer, tnew)
    oa = oa8[:, :tnew].transpose(1, 0, 2).reshape(t, A_W)

    pre4 = ssm_conv0.transpose(1, 0, 2).reshape(S_CONV - 1, dbsz, 1, S_CONV_CH)
    h0 = ssm0.reshape(dbsz, N_PAIRS, 2 * S_HD, S_STATE)
    ob4, ssm_new = _ssd_sample(proj.reshape(tnew, dbsz, 1, N_PROJ), pre4, h0, lp['ssm_conv_w'],
                               lp['ssm_conv_b'], lp['ssm_dt_bias'], lp['ssm_a_log'], lp['ssm_dvec'],
                               lp['ssm_norm_g'])
    ob = ob4.reshape(t, S_W)
    ssm_new = ssm_new.reshape(dbsz, S_HEADS, S_HD, S_STATE)

    r, w, k2, v, kk, kka, g = _rwkv_prep(
        proj, shift0.reshape(dbsz, R_IN_W), lp['rw_mu'], lp['rw_w0'], lp['rw_w2p'], lp['rw_a0'],
        lp['rw_a2p'], lp['rw_g2p'], lp['rw_kk'], lp['rw_ka'], t, dbsz, 1)
    to4 = lambda a: a.reshape(tnew, dbsz, 1, R_W)
    yr4, rw_new = _rwkv_scan_sample(tuple(to4(a) for a in (r, w, k2, v, kk, kka)), _pack_rwkv_state(rwkv0))
    yr = yr4.reshape(t, R_W)

    x = _merge(x, proj, oa, ob, yr, r, k2, v, g, lp['b_gate'], lp['rw_ln_g'], lp['rw_ln_b'], lp['rw_rk'],
               lp['w_pa'], lp['w_pb'], lp['w_pc'], lp['w_o'], min(t, 256))
    pre = ffn_conv0.transpose(1, 0, 2).reshape((F_CONV - 1) * dbsz, D_FF)
    x, tail = _ffn(x, lp['norm2_g'], lp['w_up'], lp['w_down'], lp['ffn_conv_w'], lp['ffn_conv_b'],
                   pre, gf, t, _ffn_tf(), dbsz, 1, final_norm)

    k_new = qkv[:, :, A_W:2 * A_W].reshape(dbsz, tnew, A_HEADS, A_HD)
    v_new = qkv[:, :, 2 * A_W:].reshape(dbsz, tnew, A_HEADS, A_HD)
    ssm_conv_new = proj3[tnew - (S_CONV - 1):, :, C_XBC:C_XBC + S_CONV_CH].transpose(1, 0, 2)
    shift_new = proj3[tnew - 1:, :, C_RW:C_RW + R_IN_W].transpose(1, 0, 2)
    ffn_conv_new = tail.reshape(F_CONV - 1, dbsz, D_FF).transpose(1, 0, 2)
    state = (k_new, v_new, ssm_new, ssm_conv_new, _unpack_rwkv_state(rw_new), shift_new, ffn_conv_new)
    return x, state


def kernel(x_prompt, x_sample, cache_k, cache_v, state_ssm, state_ssm_conv, state_rwkv, state_rwkv_shift, state_ffn_conv, page_table, norm1_g, w_in, b_gate, w_pa, ssm_conv_w, ssm_conv_b, ssm_dt_bias, ssm_a_log, ssm_d, ssm_norm_g, w_pb, rw_mu, rw_w0, rw_w2, rw_a0, rw_a2, rw_g2, rw_kk, rw_ka, rw_rk, rw_ln_g, rw_ln_b, w_pc, w_o, norm2_g, w_up, ffn_conv_w, ffn_conv_b, w_down, norm_f_g):
    params = dict(norm1_g=norm1_g, w_in=w_in, b_gate=b_gate, w_pa=w_pa, ssm_conv_w=ssm_conv_w,
                  ssm_conv_b=ssm_conv_b, ssm_dt_bias=ssm_dt_bias, ssm_a_log=ssm_a_log, ssm_d=ssm_d,
                  ssm_norm_g=ssm_norm_g, w_pb=w_pb, rw_mu=rw_mu, rw_w0=rw_w0, rw_w2=rw_w2, rw_a0=rw_a0,
                  rw_a2=rw_a2, rw_g2=rw_g2, rw_kk=rw_kk, rw_ka=rw_ka, rw_rk=rw_rk, rw_ln_g=rw_ln_g,
                  rw_ln_b=rw_ln_b, w_pc=w_pc, w_o=w_o, norm2_g=norm2_g, w_up=w_up, ffn_conv_w=ffn_conv_w,
                  ffn_conv_b=ffn_conv_b, w_down=w_down)
    depth = w_in.shape[0]
    bsz, seq, _ = x_prompt.shape
    dbsz, tnew, _ = x_sample.shape
    head = jnp.arange(A_HEADS, dtype=F32) + 1.0
    slopes = jnp.exp2(-8.0 * head / A_HEADS)
    slopes_pair = jnp.repeat(slopes, A_HD).reshape(N_PAIRS, 1, LANES)
    cache_k4 = cache_k.transpose(0, 1, 3, 4, 2)
    cache_v4 = cache_v.transpose(0, 1, 3, 4, 2)
    gf = norm_f_g.reshape(1, D_MODEL)

    hp = x_prompt.reshape(bsz * seq, D_MODEL)
    hs = x_sample.transpose(1, 0, 2).reshape(tnew * dbsz, D_MODEL)
    new_p = [[] for _ in range(7)]
    new_s = [[] for _ in range(7)]
    for l in range(depth):
        lp = _prep_layer_params(l, params)
        last = l == depth - 1
        hp, sp = _prompt_layer(hp, lp, bsz, seq, slopes_pair, gf, last)
        st = (state_ssm[l], state_ssm_conv[l], state_rwkv[l], state_rwkv_shift[l], state_ffn_conv[l])
        hs, ss = _sample_layer(hs, lp, dbsz, tnew, st, cache_k4, cache_v4, page_table, l, gf, last)
        for j in range(7):
            new_p[j].append(sp[j])
            new_s[j].append(ss[j])
    y_prompt = hp.reshape(bsz, seq, D_MODEL)
    y_sample = hs.reshape(tnew, dbsz, D_MODEL).transpose(1, 0, 2)
    outs = [y_prompt, y_sample]
    for j in range(7):
        outs += [jnp.stack(new_p[j]), jnp.stack(new_s[j])]
    return tuple(outs)
```

```python
import functools
import math

import jax
import jax.numpy as jnp
from jax import lax
from jax.experimental import pallas as pl
from jax.experimental.pallas import tpu as pltpu

F32 = jnp.float32
BF16 = jnp.bfloat16

D_MODEL = 1024
A_HEADS = 8
A_HD = 64
A_W = A_HEADS * A_HD
MOBA_BLOCK = 256
MOBA_TOPK = 3
Q_BLOCK = MOBA_BLOCK
ATTN_SCALE = A_HD ** -0.5
S_HEADS = 8
S_HD = 64
S_W = S_HEADS * S_HD
S_GROUPS = 2
S_STATE = 128
S_CONV = 4
S_CONV_CH = S_W + 2 * S_GROUPS * S_STATE
SSD_CHUNK = 128
R_HEADS = 8
R_HD = 64
R_W = R_HEADS * R_HD
R_LORA_W = 64
R_LORA_A = 64
R_LORA_G = 128
R_LORA = R_LORA_W + R_LORA_A + R_LORA_G
R_IN_W = 3 * R_W + R_LORA
D_FF = ((8 * D_MODEL // 3 + 127) // 128) * 128
F_CONV = 3
NORM_EPS = 1e-6
GN_EPS = 64e-5
NEG_INF = -1e30

LANES = 128
SUBLANES = 8
HALF = 64
HALF_SHIFT = 6
SUBLANE_SHIFT = 3
N_PAIRS = 4

C_RW = 0
C_DT = R_IN_W
C_Q = 2048
C_K = C_Q + A_W
C_V = C_K + A_W
C_Z = C_V + A_W
C_XBC = C_Z + S_W
C_GATE = C_XBC + S_CONV_CH
N_PROJ = C_GATE + 3 * D_MODEL

VMEM_LIMIT = 56 * 1024 * 1024


def _cparams(sem):
    return pltpu.CompilerParams(dimension_semantics=sem, vmem_limit_bytes=VMEM_LIMIT)


def _dot(a, b):
    return jnp.dot(a.astype(BF16), b.astype(BF16), preferred_element_type=F32)


def _dot_nt(a, b):
    return lax.dot_general(a.astype(BF16), b.astype(BF16), (((1,), (1,)), ((), ())),
                           preferred_element_type=F32)


def _dot_tn(a, b):
    return lax.dot_general(a.astype(BF16), b.astype(BF16), (((0,), (0,)), ((), ())),
                           preferred_element_type=F32)


def _dot_hi(a, b):
    return jnp.dot(a, b, preferred_element_type=F32, precision=lax.Precision.HIGHEST)


def _dot_nt_hi(a, b):
    return lax.dot_general(a, b, (((1,), (1,)), ((), ())), preferred_element_type=F32,
                           precision=lax.Precision.HIGHEST)


def _sigmoid(x):
    return 1.0 / (1.0 + jnp.exp(-x))


def _silu(x):
    return x * _sigmoid(x)


def _softplus(x):
    return jnp.maximum(x, 0.0) + jnp.log(1.0 + jnp.exp(-jnp.abs(x)))


def _head_ones(width, scale):
    r = lax.shift_right_logical(lax.broadcasted_iota(jnp.int32, (width, width), 0), HALF_SHIFT)
    c = lax.shift_right_logical(lax.broadcasted_iota(jnp.int32, (width, width), 1), HALF_SHIFT)
    return jnp.where(r == c, scale, 0.0).astype(F32)


def _rms_matmul_kernel(x_ref, g_ref, w_ref, o_ref, xn_ref):
    @pl.when(pl.program_id(1) == 0)
    def _():
        x = x_ref[...]
        ms = jnp.mean(x * x, axis=-1, keepdims=True)
        xn_ref[...] = (x * lax.rsqrt(ms + NORM_EPS) * g_ref[...]).astype(BF16)

    o_ref[...] = jnp.dot(xn_ref[...], w_ref[...], preferred_element_type=F32)


def _rms_matmul(x, g, w, tm, tn):
    t, d = x.shape
    n = w.shape[1]
    return pl.pallas_call(
        _rms_matmul_kernel,
        grid=(t // tm, n // tn),
        in_specs=[pl.BlockSpec((tm, d), lambda i, j: (i, 0)),
                  pl.BlockSpec((1, d), lambda i, j: (0, 0)),
                  pl.BlockSpec((d, tn), lambda i, j: (0, j))],
        out_specs=pl.BlockSpec((tm, tn), lambda i, j: (i, j)),
        out_shape=jax.ShapeDtypeStruct((t, n), F32),
        scratch_shapes=[pltpu.VMEM((tm, d), BF16)],
        compiler_params=_cparams(("parallel", "arbitrary")),
        name="in_proj",
    )(x, g, w)


def _topk_bias(gate, n_valid, axis=1):
    pos = lax.broadcasted_iota(jnp.int32, gate.shape, axis)
    pos_f = pos.astype(F32)
    gm = jnp.where(pos < n_valid, gate, NEG_INF)
    selected = jnp.zeros(gate.shape, jnp.bool_)
    for j in range(MOBA_TOPK):
        m = jnp.max(gm, axis=axis, keepdims=True)
        idx = jnp.min(jnp.where(gm == m, pos_f, 1e9), axis=axis, keepdims=True)
        hit = pos_f == idx
        selected = jnp.logical_or(selected, jnp.logical_and(hit, j < n_valid))
        gm = jnp.where(hit, -jnp.inf, gm)
    return jnp.where(selected, 0.0, NEG_INF).astype(F32)


def _moba_prompt_kernel(q_ref, k_ref, v_ref, sl_ref, o_ref, kb_ref, vt_ref, kmean_ref, sel_ref, acc_ref,
                        raw_a, raw_b, m_ref, l_ref, *, nb):
    i = pl.program_id(2)
    B = MOBA_BLOCK

    @pl.when(i == 0)
    def _():
        kb_ref[...] = k_ref[0].astype(BF16)
        kmean_ref[...] = jnp.zeros_like(kmean_ref)
        for n in range(nb):
            kmean_ref[n:n + 1, :] = jnp.mean(k_ref[0, n * B:(n + 1) * B, :], axis=0, keepdims=True)
            vt_ref[n] = v_ref[0, n * B:(n + 1) * B, :].T.astype(BF16)

    own = i
    q = q_ref[0]
    lane = lax.broadcasted_iota(jnp.int32, (1, LANES), 1)
    halves = (lane < HALF, lane >= HALF)
    log2e = 1.0 / math.log(2.0)
    slope2 = (sl_ref[0, :, 0:1] * log2e, sl_ref[0, :, HALF:HALF + 1] * log2e)
    kmean = kmean_ref[...]
    qb = []
    for h in range(2):
        q_m = jnp.where(halves[h], q, 0.0)
        bias_t = _topk_bias(_dot_nt_hi(kmean, q_m), own, axis=0)
        for n in range(nb):
            sel_ref[h, n] = jnp.broadcast_to(bias_t[n:n + 1, :], (SUBLANES, Q_BLOCK))
        qb.append((q_m * (ATTN_SCALE * log2e)).astype(BF16))

    rel = (lax.broadcasted_iota(jnp.int32, (B, Q_BLOCK), 0) - lax.broadcasted_iota(jnp.int32, (B, Q_BLOCK), 1))
    alibi = [slope2[h] * rel.astype(F32) for h in range(2)]
    row_a = lax.broadcasted_iota(jnp.int32, (LANES, 1), 0) < HALF

    def put_scores(n, dst_ref):
        kn = kb_ref[pl.ds(pl.multiple_of(n * B, B), B), :]
        for h in range(2):
            dst_ref[h] = _dot_nt(kn, qb[h])

    def row_of(ref, h):
        return ref[h][0:1, :]

    def put_row(ref, h, x):
        ref[h] = jnp.broadcast_to(x, (SUBLANES, Q_BLOCK))

    def attend(n, src_ref):
        shift = ((n - i) * B).astype(F32)
        ps, alphas = [], []
        for h in range(2):
            row = sel_ref[h, n][0:1, :] + slope2[h] * shift
            s = src_ref[h] + alibi[h] + row
            m0 = row_of(m_ref, h)
            m1 = jnp.maximum(m0, jnp.max(s, axis=0, keepdims=True))
            alpha = jnp.exp2(m0 - m1)
            p = jnp.exp2(s - m1)
            put_row(m_ref, h, m1)
            put_row(l_ref, h, alpha * row_of(l_ref, h) + jnp.sum(p, axis=0, keepdims=True))
            ps.append(p.astype(BF16))
            alphas.append(alpha)
        vtn = vt_ref[n]
        pv = [jnp.dot(vtn, ps[h], preferred_element_type=F32) for h in range(2)]
        for h in range(2):
            acc_ref[h] = alphas[h] * acc_ref[h] + pv[h]

    put_scores(0, raw_a)

    k_own = kb_ref[pl.ds(pl.multiple_of(own * B, B), B), :]
    own_raw = [_dot_nt(k_own, qb[h]) for h in range(2)]
    own_p = []
    for h in range(2):
        s = jnp.where(rel <= 0, own_raw[h] + alibi[h], NEG_INF)
        m = jnp.max(s, axis=0, keepdims=True)
        p = jnp.exp2(s - m)
        put_row(m_ref, h, m)
        put_row(l_ref, h, jnp.sum(p, axis=0, keepdims=True))
        own_p.append(p.astype(BF16))
    vt_own = vt_ref[own]
    for h in range(2):
        acc_ref[h] = jnp.dot(vt_own, own_p[h], preferred_element_type=F32)

    def body(j, carry):
        n0 = 2 * j
        put_scores(n0 + 1, raw_b)
        attend(n0, raw_a)
        put_scores(jnp.minimum(n0 + 2, nb - 1), raw_a)
        attend(n0 + 1, raw_b)
        return carry

    lax.fori_loop(0, own // 2, body, 0)

    @pl.when(own % 2 == 1)
    def _():
        attend(own - 1, raw_a)

    out_t = jnp.where(row_a, acc_ref[0] / row_of(l_ref, 0), acc_ref[1] / row_of(l_ref, 1))
    o_ref[0] = out_t.T


def _moba_prompt(proj3, slopes_pair):
    bsz, seq, _ = proj3.shape
    assert seq % MOBA_BLOCK == 0 and seq // MOBA_BLOCK >= MOBA_TOPK
    nb = seq // MOBA_BLOCK
    nbp = -(-nb // SUBLANES) * SUBLANES
    nq = seq // Q_BLOCK
    qc, kc, vc = C_Q // LANES, C_K // LANES, C_V // LANES
    return pl.pallas_call(
        functools.partial(_moba_prompt_kernel, nb=nb),
        grid=(bsz, N_PAIRS, nq),
        in_specs=[pl.BlockSpec((1, Q_BLOCK, LANES), lambda b, p, i: (b, i, qc + p)),
                  pl.BlockSpec((1, seq, LANES), lambda b, p, i: (b, 0, kc + p)),
                  pl.BlockSpec((1, seq, LANES), lambda b, p, i: (b, 0, vc + p)),
                  pl.BlockSpec((1, 1, LANES), lambda b, p, i: (p, 0, 0))],
        out_specs=pl.BlockSpec((1, Q_BLOCK, LANES), lambda b, p, i: (b, i, p)),
        out_shape=jax.ShapeDtypeStruct((bsz, seq, A_W), F32),
        scratch_shapes=[pltpu.VMEM((seq, LANES), BF16), pltpu.VMEM((nb, LANES, MOBA_BLOCK), BF16),
                        pltpu.VMEM((nbp, LANES), F32), pltpu.VMEM((2, nb, SUBLANES, Q_BLOCK), F32),
                        pltpu.VMEM((2, LANES, Q_BLOCK), F32),
                        pltpu.VMEM((2, MOBA_BLOCK, Q_BLOCK), F32), pltpu.VMEM((2, MOBA_BLOCK, Q_BLOCK), F32),
                        pltpu.VMEM((2, SUBLANES, Q_BLOCK), F32), pltpu.VMEM((2, SUBLANES, Q_BLOCK), F32)],
        compiler_params=_cparams(("parallel", "parallel", "arbitrary")),
        name="moba_prompt",
    )(proj3, proj3, proj3, slopes_pair)


def _moba_sample_kernel(pt_ref, q_ref, kn_ref, vn_ref, *refs, n_blk, ppb, page, tnew):
    del pt_ref
    n_pages = n_blk * ppb
    k_refs, v_refs = refs[:n_pages], refs[n_pages:2 * n_pages]
    o_ref, m_ref, l_ref, g_ref, acc_ref = refs[2 * n_pages:]
    past = n_blk * MOBA_BLOCK
    rows = A_HEADS * SUBLANES
    lane_head = lax.shift_right_logical(lax.broadcasted_iota(jnp.int32, (SUBLANES, A_W), 1), HALF_SHIFT)
    q8 = q_ref[0] * ATTN_SCALE
    qbd = jnp.concatenate([jnp.where(lane_head == h, q8, 0.0) for h in range(A_HEADS)], axis=0)
    q_hi = qbd.astype(BF16)
    q_lo = (qbd - q_hi.astype(F32)).astype(BF16)
    r = lax.broadcasted_iota(jnp.int32, (rows, 1), 0)
    tok = jnp.bitwise_and(r, SUBLANES - 1).astype(F32)
    slope = jnp.exp2(-(8.0 / A_HEADS) * (lax.shift_right_logical(r, SUBLANE_SHIFT) + 1).astype(F32))
    key = lax.broadcasted_iota(jnp.int32, (1, MOBA_BLOCK), 1).astype(F32)

    for n in range(n_blk):
        pages = range(n * ppb, (n + 1) * ppb)
        kt = jnp.concatenate([k_refs[j][0, 0].reshape(A_W, page) for j in pages], axis=1).astype(BF16)
        vt = jnp.concatenate([v_refs[j][0, 0].reshape(A_W, page) for j in pages], axis=1).astype(BF16)
        s_raw = jnp.dot(q_hi, kt, preferred_element_type=F32)
        s_fix = jnp.dot(q_lo, kt, preferred_element_type=F32)
        g_ref[n] = jnp.broadcast_to(jnp.sum(s_raw + s_fix, axis=-1, keepdims=True) * (1.0 / MOBA_BLOCK),
                                    (rows, LANES))
        s = s_raw - slope * ((past - n * MOBA_BLOCK + tok) - key)
        m = jnp.max(s, axis=-1, keepdims=True)
        e = jnp.exp(s - m)
        m_ref[n] = jnp.broadcast_to(m, (rows, LANES))
        l_ref[n] = jnp.broadcast_to(jnp.sum(e, axis=-1, keepdims=True), (rows, LANES))
        acc_ref[n] = lax.dot_general(e.astype(BF16), vt, (((1,), (1,)), ((), ())), preferred_element_type=F32)

    lane = lax.broadcasted_iota(jnp.int32, (rows, LANES), 1)
    gate = jnp.zeros((rows, LANES), F32)
    for j in range(n_blk):
        gate = jnp.where(lane == j, g_ref[j], gate)
    bias = _topk_bias(gate, n_blk)

    kn = kn_ref[0]
    vn = vn_ref[0]
    s_own = []
    for j in range(tnew):
        sj = jnp.sum(qbd * kn[j:j + 1, :], axis=-1, keepdims=True) - slope * (tok - j)
        s_own.append(jnp.where(tok >= j, sj, NEG_INF))
    mx = s_own[0]
    for j in range(1, tnew):
        mx = jnp.maximum(mx, s_own[j])
    mb = []
    for j in range(n_blk):
        mj = m_ref[j][:, 0:1] + bias[:, j:j + 1]
        mb.append(mj)
        mx = jnp.maximum(mx, mj)
    lsum = jnp.zeros((rows, 1), F32)
    acc = jnp.zeros((rows, A_W), F32)
    for j in range(tnew):
        w = jnp.exp(s_own[j] - mx)
        lsum = lsum + w
        acc = acc + w * vn[j:j + 1, :]
    for j in range(n_blk):
        w = jnp.exp(mb[j] - mx)
        lsum = lsum + w * l_ref[j][:, 0:1]
        acc = acc + w * acc_ref[j]
    out = acc / lsum
    o8 = jnp.zeros((SUBLANES, A_W), F32)
    for h in range(A_HEADS):
        o8 = o8 + jnp.where(lane_head == h, out[h * SUBLANES:(h + 1) * SUBLANES, :], 0.0)
    o_ref[0] = o8


def _moba_sample(q8, k8, v8, cache_kt, cache_vt, page_table, layer, tnew):
    dbsz = q8.shape[0]
    n_pages = page_table.shape[1]
    page = cache_kt.shape[4]
    assert MOBA_BLOCK % page == 0 and (n_pages * page) % MOBA_BLOCK == 0 and page % LANES == 0
    ppb = MOBA_BLOCK // page
    n_blk = n_pages // ppb
    assert n_blk >= MOBA_TOPK and tnew <= SUBLANES
    rows = A_HEADS * SUBLANES
    tok_spec = pl.BlockSpec((1, SUBLANES, A_W), lambda b, pt: (b, 0, 0))

    def page_spec(j):
        return pl.BlockSpec((1, 1, A_HEADS, A_HD, page), lambda b, pt: (layer, pt[b * n_pages + j], 0, 0, 0))

    pages = [page_spec(j) for j in range(n_pages)]
    grid_spec = pltpu.PrefetchScalarGridSpec(
        num_scalar_prefetch=1,
        grid=(dbsz,),
        in_specs=[tok_spec, tok_spec, tok_spec] + pages + pages,
        out_specs=tok_spec,
        scratch_shapes=[pltpu.VMEM((n_blk, rows, LANES), F32), pltpu.VMEM((n_blk, rows, LANES), F32),
                        pltpu.VMEM((n_blk, rows, LANES), F32), pltpu.VMEM((n_blk, rows, A_W), F32)],
    )
    return pl.pallas_call(
        functools.partial(_moba_sample_kernel, n_blk=n_blk, ppb=ppb, page=page, tnew=tnew),
        grid_spec=grid_spec,
        out_shape=jax.ShapeDtypeStruct((dbsz, SUBLANES, A_W), F32),
        compiler_params=_cparams(("parallel",)),
        name="moba_sample",
    )(page_table.reshape(-1), q8, k8, v8, *([cache_kt] * n_pages), *([cache_vt] * n_pages))


def _ssd_prompt_kernel(z_ref, xbc_ref, dt_ref, cw_ref, cb_ref, dtb_ref, alog_ref, dvec_ref, ng_ref,
                       o_ref, hout_ref, ext_ref, h_ref, *, L):
    c = pl.program_id(1)
    P = SUBLANES

    @pl.when(c == 0)
    def _():
        ext_ref[0:P, :] = jnp.zeros((P, S_CONV_CH), F32)
        h_ref[...] = jnp.zeros_like(h_ref)

    @pl.when(c > 0)
    def _():
        ext_ref[0:P, :] = ext_ref[L:L + P, :]

    ext_ref[P:P + L, :] = xbc_ref[...]
    acc = cb_ref[...] + cw_ref[S_CONV - 1:S_CONV, :] * ext_ref[P:P + L, :]
    for d in range(1, S_CONV):
        acc = acc + cw_ref[S_CONV - 1 - d:S_CONV - d, :] * ext_ref[P - d:P - d + L, :]
    xbc = _silu(acc)
    xs = xbc[:, :S_W]
    bm = xbc[:, S_W:S_W + S_GROUPS * S_STATE]
    cm = xbc[:, S_W + S_GROUPS * S_STATE:]

    lane = lax.broadcasted_iota(jnp.int32, (1, LANES), 1)
    is_a = lane < HALF
    dt = jnp.where(lane < S_HEADS, _softplus(dt_ref[...] + dtb_ref[...]), 0.0)
    da = dt * (-jnp.exp(alog_ref[...]))
    rr = lax.broadcasted_iota(jnp.int32, (L, L), 0)
    cc = lax.broadcasted_iota(jnp.int32, (L, L), 1)
    causal = rr >= cc
    cum = _dot_hi(jnp.where(causal, 1.0, 0.0).astype(F32), da)
    cum_t = cum.T
    dt_t = dt.T
    row_a = lax.broadcasted_iota(jnp.int32, (2 * S_HD, 1), 0) < S_HD

    ys = []
    cb_g = [None] * S_GROUPS
    for pr in range(N_PAIRS):
        g = (2 * pr * S_GROUPS) // S_HEADS
        bg = bm[:, g * S_STATE:(g + 1) * S_STATE]
        cg = cm[:, g * S_STATE:(g + 1) * S_STATE]
        if cb_g[g] is None:
            cb_g[g] = _dot_nt(cg, bg)
        xs_p = xs[:, pr * LANES:(pr + 1) * LANES]
        heads = (2 * pr, 2 * pr + 1)
        halves = (is_a, jnp.logical_not(is_a))
        yp = jnp.zeros((L, LANES), F32)
        cum_c = [cum[:, h:h + 1] for h in heads]
        for h, half, cc_h in zip(heads, halves, cum_c):
            seg = cc_h - cum_t[h:h + 1, :]
            dec = jnp.exp(jnp.where(causal, seg, -jnp.inf))
            wts = cb_g[g] * dec * dt_t[h:h + 1, :]
            yp = yp + _dot(wts, jnp.where(half, xs_p, 0.0))
        hp = h_ref[pr]
        yp = yp + _dot_nt(cg, hp) * jnp.where(is_a, jnp.exp(cum_c[0]), jnp.exp(cum_c[1]))
        last = [cum[L - 1:L, h:h + 1] for h in heads]
        te = jnp.where(is_a, jnp.exp(last[0] - cum_c[0]) * dt[:, heads[0]:heads[0] + 1],
                       jnp.exp(last[1] - cum_c[1]) * dt[:, heads[1]:heads[1] + 1])
        st = _dot_tn(xs_p * te, bg)
        h_ref[pr] = hp * jnp.where(row_a, jnp.exp(last[0]), jnp.exp(last[1])) + st
        ys.append(yp)
    y = jnp.concatenate(ys, axis=1) + dvec_ref[...] * xs
    yz = y * _silu(z_ref[...])
    gw = S_W // S_GROUPS
    for g in range(S_GROUPS):
        part = yz[:, g * gw:(g + 1) * gw]
        ms = jnp.mean(part * part, axis=-1, keepdims=True)
        o_ref[:, g * gw:(g + 1) * gw] = part * lax.rsqrt(ms + NORM_EPS) * ng_ref[:, g * gw:(g + 1) * gw]

    @pl.when(c == pl.num_programs(1) - 1)
    def _():
        hout_ref[0] = h_ref[...]


def _ssd_prompt(proj, bsz, seq, cw, cb, dtb, alog, dvec, ng):
    L = SSD_CHUNK
    nc = seq // L
    const = lambda shape: pl.BlockSpec(shape, lambda b, c: (0,) * len(shape))
    return pl.pallas_call(
        functools.partial(_ssd_prompt_kernel, L=L),
        grid=(bsz, nc),
        in_specs=[pl.BlockSpec((L, S_W), lambda b, c: (b * nc + c, C_Z // S_W)),
                  pl.BlockSpec((L, S_CONV_CH), lambda b, c: (b * nc + c, C_XBC // S_CONV_CH)),
                  pl.BlockSpec((L, LANES), lambda b, c: (b * nc + c, C_DT // LANES)),
                  const((S_CONV, S_CONV_CH)), const((1, S_CONV_CH)), const((1, LANES)), const((1, LANES)),
                  const((1, S_W)), const((1, S_W))],
        out_specs=[pl.BlockSpec((L, S_W), lambda b, c: (b * nc + c, 0)),
                   pl.BlockSpec((1, N_PAIRS, 2 * S_HD, S_STATE), lambda b, c: (b, 0, 0, 0))],
        out_shape=[jax.ShapeDtypeStruct((bsz * seq, S_W), F32),
                   jax.ShapeDtypeStruct((bsz, N_PAIRS, 2 * S_HD, S_STATE), F32)],
        scratch_shapes=[pltpu.VMEM((L + 2 * SUBLANES, S_CONV_CH), F32),
                        pltpu.VMEM((N_PAIRS, 2 * S_HD, S_STATE), F32)],
        compiler_params=_cparams(("parallel", "arbitrary")),
        name="ssd_prompt",
    )(proj, proj, proj, cw, cb, dtb, alog, dvec, ng)


def _ssd_sample_kernel(z_ref, xbc_ref, dt_ref, pre_ref, h0_ref, cw_ref, cb_ref, dtb_ref, alog_ref,
                       dvec_ref, ng_ref, o_ref, hout_ref, *, T, nb):
    lane = lax.broadcasted_iota(jnp.int32, (1, LANES), 1)
    a_row = -jnp.exp(alog_ref[...])
    rows2 = 2 * S_HD
    eye = (lax.broadcasted_iota(jnp.int32, (rows2, LANES), 0)
           == lax.broadcasted_iota(jnp.int32, (rows2, LANES), 1)).astype(F32)
    row_a = lax.broadcasted_iota(jnp.int32, (rows2, 1), 0) < S_HD
    gw = S_W // S_GROUPS

    def per_seq(b, carry):
        up = [pre_ref[j, b] for j in range(S_CONV - 1)] + [xbc_ref[t, b] for t in range(T)]
        xc, dts = [], []
        for t in range(T):
            acc = cb_ref[...] + cw_ref[0:1, :] * up[t]
            for j in range(1, S_CONV):
                acc = acc + cw_ref[j:j + 1, :] * up[t + j]
            xc.append(_silu(acc))
            dts.append(jnp.where(lane < S_HEADS, _softplus(dt_ref[t, b] + dtb_ref[...]), 0.0))
        ys = [[] for _ in range(T)]
        hs = [h0_ref[b, pr] for pr in range(N_PAIRS)]
        for t in range(T):
            for pr in range(N_PAIRS):
                g = (2 * pr * S_GROUPS) // S_HEADS
                ha, hb = 2 * pr, 2 * pr + 1
                x_row = xc[t][:, pr * LANES:(pr + 1) * LANES]
                b_row = xc[t][:, S_W + g * S_STATE:S_W + (g + 1) * S_STATE]
                c_row = xc[t][:, S_W + (S_GROUPS + g) * S_STATE:S_W + (S_GROUPS + g + 1) * S_STATE]
                dt_col = jnp.where(row_a, dts[t][:, ha:ha + 1], dts[t][:, hb:hb + 1])
                a_col = jnp.where(row_a, a_row[:, ha:ha + 1], a_row[:, hb:hb + 1])
                x_col = jnp.sum(eye * x_row, axis=-1, keepdims=True)
                hs[pr] = hs[pr] * jnp.exp(dt_col * a_col) + (x_col * dt_col) * b_row
                y_col = jnp.sum(hs[pr] * c_row, axis=-1, keepdims=True)
                ys[t].append(jnp.sum(eye * y_col, axis=0, keepdims=True))
        for pr in range(N_PAIRS):
            hout_ref[b, pr] = hs[pr]
        for t in range(T):
            y = jnp.concatenate(ys[t], axis=1) + dvec_ref[...] * xc[t][:, :S_W]
            yz = y * _silu(z_ref[t, b])
            for g in range(S_GROUPS):
                part = yz[:, g * gw:(g + 1) * gw]
                ms = jnp.mean(part * part, axis=-1, keepdims=True)
                o_ref[t, b, :, g * gw:(g + 1) * gw] = (part * lax.rsqrt(ms + NORM_EPS)
                                                       * ng_ref[:, g * gw:(g + 1) * gw])
        return carry

    lax.fori_loop(0, nb, per_seq, 0)


def _ssd_sample(proj4, pre4, h0, cw, cb, dtb, alog, dvec, ng, nb=16):
    T, dbsz = proj4.shape[:2]
    const = lambda shape: pl.BlockSpec(shape, lambda j: (0,) * len(shape))
    hspec = pl.BlockSpec((nb, N_PAIRS, 2 * S_HD, S_STATE), lambda j: (j, 0, 0, 0))
    return pl.pallas_call(
        functools.partial(_ssd_sample_kernel, T=T, nb=nb),
        grid=(dbsz // nb,),
        in_specs=[pl.BlockSpec((T, nb, 1, S_W), lambda j: (0, j, 0, C_Z // S_W)),
                  pl.BlockSpec((T, nb, 1, S_CONV_CH), lambda j: (0, j, 0, C_XBC // S_CONV_CH)),
                  pl.BlockSpec((T, nb, 1, LANES), lambda j: (0, j, 0, C_DT // LANES)),
                  pl.BlockSpec((S_CONV - 1, nb, 1, S_CONV_CH), lambda j: (0, j, 0, 0)),
                  hspec,
                  const((S_CONV, S_CONV_CH)), const((1, S_CONV_CH)), const((1, LANES)), const((1, LANES)),
                  const((1, S_W)), const((1, S_W))],
        out_specs=[pl.BlockSpec((T, nb, 1, S_W), lambda j: (0, j, 0, 0)), hspec],
        out_shape=[jax.ShapeDtypeStruct((T, dbsz, 1, S_W), F32),
                   jax.ShapeDtypeStruct((dbsz, N_PAIRS, 2 * S_HD, S_STATE), F32)],
        compiler_params=_cparams(("parallel",)),
        name="ssd_sample",
    )(proj4, proj4, proj4, pre4, h0, cw, cb, dtb, alog, dvec, ng)


def _rwkv_prep_kernel(u_ref, pre_ref, mu_ref, w0_ref, w2_ref, a0_ref, a2_ref, g2_ref, kk_ref, ka_ref,
                      r_out, w_out, k_out, v_out, kk_out, kka_out, g_out, ext_ref,
                      *, tm, P, stride, tiles_per_seq):
    i = pl.program_id(0)
    first = (i % tiles_per_seq) == 0

    @pl.when(first)
    def _():
        ext_ref[0:P, :] = pre_ref[...]

    @pl.when(jnp.logical_not(first))
    def _():
        ext_ref[0:P, :] = ext_ref[tm:tm + P, :]

    u = u_ref[...]
    ext_ref[P:P + tm, :] = u
    prev = ext_ref[P - stride:P - stride + tm, :]
    x = u + (prev - u) * mu_ref[...]
    r = x[:, 0:R_W]
    kr = x[:, R_W:2 * R_W]
    vr = x[:, 2 * R_W:3 * R_W]
    xl = x[:, 3 * R_W:]
    w_log = w0_ref[...] + _dot(jnp.tanh(xl), w2_ref[...])
    log_decay = -jnp.exp(-_softplus(-w_log) - 0.5)
    a = _sigmoid(a0_ref[...] + _dot(xl, a2_ref[...]))
    g = _dot(_sigmoid(xl), g2_ref[...])
    kk = kr * kk_ref[...]
    ss = _dot_hi(kk * kk, _head_ones(R_W, 1.0))
    kk = kk / jnp.maximum(jnp.sqrt(ss), 1e-12)
    r_out[...] = r
    w_out[...] = log_decay
    k_out[...] = kr * (1.0 + (a - 1.0) * ka_ref[...])
    v_out[...] = vr
    kk_out[...] = kk
    kka_out[...] = kk * a
    g_out[...] = g


def _rwkv_prep(proj, pre, mu, w0, w2p, a0, a2p, g2p, kkw, kaw, tm, stride, tiles_per_seq):
    t = proj.shape[0]
    P = pre.shape[0]
    const = lambda shape: pl.BlockSpec(shape, lambda i: (0,) * len(shape))
    outs = pl.pallas_call(
        functools.partial(_rwkv_prep_kernel, tm=tm, P=P, stride=stride, tiles_per_seq=tiles_per_seq),
        grid=(t // tm,),
        in_specs=[pl.BlockSpec((tm, R_IN_W), lambda i: (i, 0)),
                  const((P, R_IN_W)), const((1, R_IN_W)), const((1, R_W)), const((R_LORA, R_W)),
                  const((1, R_W)), const((R_LORA, R_W)), const((R_LORA, R_W)), const((1, R_W)), const((1, R_W))],
        out_specs=[pl.BlockSpec((tm, R_W), lambda i: (i, 0))] * 7,
        out_shape=[jax.ShapeDtypeStruct((t, R_W), F32)] * 7,
        scratch_shapes=[pltpu.VMEM((tm + 2 * P, R_IN_W), F32)],
        compiler_params=_cparams(("arbitrary",)),
        name="rwkv_prep",
    )(proj, pre, mu, w0, w2p, a0, a2p, g2p, kkw, kaw)
    return outs


def _pair_consts():
    lane = lax.broadcasted_iota(jnp.int32, (R_HD, LANES), 1)
    row = lax.broadcasted_iota(jnp.int32, (R_HD, LANES), 0)
    is_a = lane < HALF
    eye2 = (jnp.bitwise_and(lane, HALF - 1) == row).astype(F32)
    return is_a, eye2


def _pair_sum(x, is_a):
    sa = jnp.sum(jnp.where(is_a, x, 0.0), axis=-1, keepdims=True)
    sb = jnp.sum(jnp.where(is_a, 0.0, x), axis=-1, keepdims=True)
    return jnp.where(is_a, sa, sb)


def _rwkv_step(s, rr, ww, kr, vv, kk, kka, is_a, eye2):
    v_col = _pair_sum(eye2 * vv, is_a)
    sk = _pair_sum(s * kk, is_a)
    s = s * ww - sk * kka + v_col * kr
    y_col = _pair_sum(s * rr, is_a)
    return s, jnp.sum(eye2 * y_col, axis=0, keepdims=True)


def _unit_lower_inverses(ns):
    L = ns[0].shape[0]
    eye = (lax.broadcasted_iota(jnp.int32, (L, L), 0) == lax.broadcasted_iota(jnp.int32, (L, L), 1)).astype(F32)
    ts = [eye + n for n in ns]
    pws = list(ns)
    for _ in range(int(math.log2(L)) - 1):
        pws = [_dot(p, p) for p in pws]
        ts = [t + _dot(t, p) for t, p in zip(ts, pws)]
    return ts


def _rwkv_chunk_prompt_kernel(r_ref, lw_ref, k_ref, v_ref, kk_ref, kka_ref, y_ref, sout_ref, s_ref, *, L):
    c = pl.program_id(1)

    @pl.when(c == 0)
    def _():
        s_ref[...] = jnp.zeros_like(s_ref)

    lane = lax.broadcasted_iota(jnp.int32, (1, LANES), 1)
    is_a = lane < HALF
    rr = lax.broadcasted_iota(jnp.int32, (L, L), 0)
    cc = lax.broadcasted_iota(jnp.int32, (L, L), 1)
    incl = rr >= cc
    strict = rr > cc
    tri = jnp.where(incl, 1.0, 0.0).astype(F32)
    r2 = lax.broadcasted_iota(jnp.int32, (LANES, LANES), 0) < HALF
    c2 = lax.broadcasted_iota(jnp.int32, (LANES, LANES), 1) < HALF
    same_head = r2 == c2

    r, lw, k, v, kk, kka = (ref[...] for ref in (r_ref, lw_ref, k_ref, v_ref, kk_ref, kka_ref))
    cum = _dot_hi(tri, lw)
    last = cum[L - 1:L, :]
    inv_p = jnp.exp(-cum)
    to_end = jnp.exp(last - cum)
    b_t = kk * jnp.exp(cum - lw)
    a_t = -kka * inv_p
    k_t = k * inv_p
    r_t = r * jnp.exp(cum)
    a_end = -kka * to_end
    k_end = k * to_end
    decay_end = jnp.exp(last)
    pairs = [slice(pr * LANES, (pr + 1) * LANES) for pr in range(N_PAIRS)]
    halves = (is_a, jnp.logical_not(is_a))
    s0 = [s_ref[pr] for pr in range(N_PAIRS)]

    ns, mks, ras, rks = [], [], [], []
    for sl in pairs:
        for half in halves:
            bm = jnp.where(half, b_t[:, sl], 0.0)
            rm = jnp.where(half, r_t[:, sl], 0.0)
            ns.append(jnp.where(strict, _dot_nt(bm, a_t[:, sl]), 0.0))
            mks.append(jnp.where(strict, _dot_nt(bm, k_t[:, sl]), 0.0))
            ras.append(jnp.where(incl, _dot_nt(rm, a_t[:, sl]), 0.0))
            rks.append(jnp.where(incl, _dot_nt(rm, k_t[:, sl]), 0.0))
    ts = _unit_lower_inverses(ns)
    rhs = [_dot_nt(b_t[:, sl], s0[pr]) + jnp.where(is_a, _dot(mks[2 * pr], v[:, sl]), _dot(mks[2 * pr + 1], v[:, sl]))
           for pr, sl in enumerate(pairs)]
    us = [jnp.where(is_a, _dot(ts[2 * pr], rhs[pr]), _dot(ts[2 * pr + 1], rhs[pr])) for pr in range(N_PAIRS)]
    for pr, sl in enumerate(pairs):
        ha, hb = 2 * pr, 2 * pr + 1
        y_ref[:, sl] = _dot_nt(r_t[:, sl], s0[pr]) + jnp.where(
            is_a, _dot(ras[ha], us[pr]) + _dot(rks[ha], v[:, sl]), _dot(ras[hb], us[pr]) + _dot(rks[hb], v[:, sl]))
    for pr, sl in enumerate(pairs):
        upd = _dot_tn(us[pr], a_end[:, sl]) + _dot_tn(v[:, sl], k_end[:, sl])
        s_ref[pr] = s0[pr] * decay_end[:, sl] + jnp.where(same_head, upd, 0.0)

    @pl.when(c == pl.num_programs(1) - 1)
    def _():
        sout_ref[0] = s_ref[...]


def _rwkv_scan_prompt(seqs, bsz, seq, L):
    nc = seq // L
    spec = pl.BlockSpec((L, R_W), lambda b, c: (b * nc + c, 0))
    return pl.pallas_call(
        functools.partial(_rwkv_chunk_prompt_kernel, L=L),
        grid=(bsz, nc),
        in_specs=[spec] * 6,
        out_specs=[spec, pl.BlockSpec((1, N_PAIRS, LANES, LANES), lambda b, c: (b, 0, 0, 0))],
        out_shape=[jax.ShapeDtypeStruct((bsz * seq, R_W), F32),
                   jax.ShapeDtypeStruct((bsz, N_PAIRS, LANES, LANES), F32)],
        scratch_shapes=[pltpu.VMEM((N_PAIRS, LANES, LANES), F32)],
        compiler_params=_cparams(("parallel", "arbitrary")),
        name="rwkv_scan_prompt",
    )(*seqs)


def _rwkv_scan_sample_kernel(r_ref, w_ref, k_ref, v_ref, kk_ref, kka_ref, s0_ref, y_ref, sout_ref, *, T, nb):
    is_a, eye2 = _pair_consts()

    def per_seq(b, carry):
        states = [s0_ref[b, pr] for pr in range(N_PAIRS)]
        for t in range(T):
            for pr in range(N_PAIRS):
                sl = slice(pr * LANES, (pr + 1) * LANES)
                rows = [ref[t, b, :, sl] for ref in (r_ref, w_ref, k_ref, v_ref, kk_ref, kka_ref)]
                rows[1] = jnp.exp(rows[1])
                states[pr], y_row = _rwkv_step(states[pr], *rows, is_a, eye2)
                y_ref[t, b, :, sl] = y_row
        for pr in range(N_PAIRS):
            sout_ref[b, pr] = states[pr]
        return carry

    lax.fori_loop(0, nb, per_seq, 0)


def _rwkv_scan_sample(seqs4, s0, nb=16):
    T, dbsz = seqs4[0].shape[:2]
    spec = pl.BlockSpec((T, nb, 1, R_W), lambda j: (0, j, 0, 0))
    sspec = pl.BlockSpec((nb, N_PAIRS, R_HD, LANES), lambda j: (j, 0, 0, 0))
    return pl.pallas_call(
        functools.partial(_rwkv_scan_sample_kernel, T=T, nb=nb),
        grid=(dbsz // nb,),
        in_specs=[spec] * 6 + [sspec],
        out_specs=[spec, sspec],
        out_shape=[jax.ShapeDtypeStruct((T, dbsz, 1, R_W), F32),
                   jax.ShapeDtypeStruct((dbsz, N_PAIRS, R_HD, LANES), F32)],
        compiler_params=_cparams(("parallel",)),
        name="rwkv_scan_sample",
    )(*seqs4, s0)


def _merge_kernel(x_ref, oa_ref, ob_ref, yr_ref, r_ref, k_ref, v_ref, g_ref, ga_ref, gb_ref, gc_ref,
                  bg_ref, lng_ref, lnb_ref, rk_ref, wpa_ref, wpb_ref, wpc_ref, wo_ref, o_ref):
    mean_m = _head_ones(R_W, 1.0 / R_HD)
    yr = yr_ref[...]
    d = yr - _dot_hi(yr, mean_m)
    var = _dot_hi(d * d, mean_m)
    yn = d * lax.rsqrt(var + GN_EPS) * lng_ref[...] + lnb_ref[...]
    v = v_ref[...]
    bonus = _dot_hi(r_ref[...] * k_ref[...] * rk_ref[...], _head_ones(R_W, 1.0))
    oc = (yn + bonus * v) * g_ref[...]
    merged = (_sigmoid(ga_ref[...] + bg_ref[:, 0:D_MODEL]) * _dot(oa_ref[...], wpa_ref[...])
              + _sigmoid(gb_ref[...] + bg_ref[:, D_MODEL:2 * D_MODEL]) * _dot(ob_ref[...], wpb_ref[...])
              + _sigmoid(gc_ref[...] + bg_ref[:, 2 * D_MODEL:]) * _dot(oc, wpc_ref[...]))
    o_ref[...] = x_ref[...] + _dot(merged, wo_ref[...])


def _merge(x, proj, oa, ob, yr, r, k2, v, g, bg, lng, lnb, rk, wpa, wpb, wpc, wo, tm):
    t = x.shape[0]
    row = lambda w: pl.BlockSpec((tm, w), lambda i: (i, 0))
    const = lambda shape: pl.BlockSpec(shape, lambda i: (0,) * len(shape))
    gcol = C_GATE // D_MODEL
    gate = lambda j: pl.BlockSpec((tm, D_MODEL), lambda i: (i, gcol + j))
    return pl.pallas_call(
        _merge_kernel,
        grid=(t // tm,),
        in_specs=[row(D_MODEL), row(A_W), row(S_W), row(R_W), row(R_W), row(R_W), row(R_W), row(R_W),
                  gate(0), gate(1), gate(2),
                  const((1, 3 * D_MODEL)), const((1, R_W)), const((1, R_W)), const((1, R_W)),
                  const((A_W, D_MODEL)), const((S_W, D_MODEL)), const((R_W, D_MODEL)),
                  const((D_MODEL, D_MODEL))],
        out_specs=row(D_MODEL),
        out_shape=jax.ShapeDtypeStruct((t, D_MODEL), F32),
        compiler_params=_cparams(("parallel",)),
        name="merge",
    )(x, oa, ob, yr, r, k2, v, g, proj, proj, proj, bg, lng, lnb, rk, wpa, wpb, wpc, wo)


def _ffn_kernel(x_ref, g_ref, wug_ref, wuv_ref, wd_ref, cw_ref, cb_ref, pre_ref, gf_ref,
                o_ref, tail_ref, xn_ref, ext_ref, *, tm, P, stride, tiles_per_seq, final_norm):
    i = pl.program_id(0)
    f = pl.program_id(1)
    first = (i % tiles_per_seq) == 0

    @pl.when(f == 0)
    def _():
        x = x_ref[...]
        ms = jnp.mean(x * x, axis=-1, keepdims=True)
        xn_ref[...] = (x * lax.rsqrt(ms + NORM_EPS) * g_ref[...]).astype(BF16)

    @pl.when(first)
    def _():
        ext_ref[f, 0:P, :] = pre_ref[...]

    @pl.when(jnp.logical_not(first))
    def _():
        ext_ref[f, 0:P, :] = ext_ref[f, tm:tm + P, :]

    xn = xn_ref[...]
    ug = jnp.dot(xn, wug_ref[...], preferred_element_type=F32)
    ext_ref[f, P:P + tm, :] = ug
    tail_ref[0] = ext_ref[f, tm:tm + P, :]
    acc = cb_ref[...] + cw_ref[F_CONV - 1:F_CONV, :] * ug
    for d in range(1, F_CONV):
        acc = acc + cw_ref[F_CONV - 1 - d:F_CONV - d, :] * ext_ref[f, P - d * stride:P - d * stride + tm, :]
    uv = jnp.dot(xn, wuv_ref[...], preferred_element_type=F32)
    contrib = _dot(_silu(acc) * uv, wd_ref[...])

    @pl.when(f == 0)
    def _():
        o_ref[...] = x_ref[...] + contrib

    @pl.when(f > 0)
    def _():
        o_ref[...] = o_ref[...] + contrib

    if final_norm:
        @pl.when(f == pl.num_programs(1) - 1)
        def _():
            y = o_ref[...]
            ms = jnp.mean(y * y, axis=-1, keepdims=True)
            o_ref[...] = y * lax.rsqrt(ms + NORM_EPS) * gf_ref[...]


def _ffn(x, g, wup, wd, cw, cb, pre, gf, tm, tf, stride, tiles_per_seq, final_norm):
    t = x.shape[0]
    P = pre.shape[0]
    nf = D_FF // tf
    return pl.pallas_call(
        functools.partial(_ffn_kernel, tm=tm, P=P, stride=stride, tiles_per_seq=tiles_per_seq,
                          final_norm=final_norm),
        grid=(t // tm, nf),
        in_specs=[pl.BlockSpec((tm, D_MODEL), lambda i, f: (i, 0)),
                  pl.BlockSpec((1, D_MODEL), lambda i, f: (0, 0)),
                  pl.BlockSpec((D_MODEL, tf), lambda i, f: (0, f)),
                  pl.BlockSpec((D_MODEL, tf), lambda i, f: (0, nf + f)),
                  pl.BlockSpec((tf, D_MODEL), lambda i, f: (f, 0)),
                  pl.BlockSpec((F_CONV, tf), lambda i, f: (0, f)),
                  pl.BlockSpec((1, tf), lambda i, f: (0, f)),
                  pl.BlockSpec((P, tf), lambda i, f: (0, f)),
                  pl.BlockSpec((1, D_MODEL), lambda i, f: (0, 0))],
        out_specs=[pl.BlockSpec((tm, D_MODEL), lambda i, f: (i, 0)),
                   pl.BlockSpec((1, P, tf), lambda i, f: (i, 0, f))],
        out_shape=[jax.ShapeDtypeStruct((t, D_MODEL), F32),
                   jax.ShapeDtypeStruct((t // tm, P, D_FF), F32)],
        scratch_shapes=[pltpu.VMEM((tm, D_MODEL), BF16), pltpu.VMEM((nf, tm + 2 * P, tf), F32)],
        compiler_params=_cparams(("arbitrary", "arbitrary")),
        name="conv_ffn",
    )(x, g, wup, wup, wd, cw, cb, pre, gf)


def _pack_rwkv_state(s):
    n = s.shape[0]
    return s.reshape(n, N_PAIRS, 2, R_HD, R_HD).transpose(0, 1, 3, 2, 4).reshape(n, N_PAIRS, R_HD, LANES)


def _unpack_rwkv_state(s):
    n = s.shape[0]
    return s.reshape(n, N_PAIRS, R_HD, 2, R_HD).transpose(0, 1, 3, 2, 4).reshape(n, R_HEADS, R_HD, R_HD)


def _unpack_rwkv_blockdiag(s):
    n = s.shape[0]
    return jnp.stack([s[:, :, :R_HD, :R_HD], s[:, :, R_HD:, R_HD:]], axis=2).reshape(n, R_HEADS, R_HD, R_HD)


def _prep_layer_params(l, p):
    w_in = p['w_in'][l]
    c_dt_src = 3 * A_W + S_W + S_CONV_CH
    c_rw_src = c_dt_src + S_HEADS
    c_gate_src = c_rw_src + R_IN_W
    w_proj = jnp.concatenate([
        w_in[:, c_rw_src:c_gate_src],
        w_in[:, c_dt_src:c_rw_src], jnp.zeros((D_MODEL, C_Q - C_DT - S_HEADS), F32),
        w_in[:, :c_dt_src],
        w_in[:, c_gate_src:]], axis=1).astype(BF16)
    pad_lane = lambda v: jnp.pad(v, (0, LANES - v.shape[0])).reshape(1, LANES)
    zl = lambda r0, w: jnp.zeros((R_LORA, R_W), F32).at[r0:r0 + w.shape[0]].set(w).astype(BF16)
    return dict(
        norm1_g=p['norm1_g'][l].reshape(1, D_MODEL), w_proj=w_proj,
        b_gate=p['b_gate'][l].reshape(1, 3 * D_MODEL),
        w_pa=p['w_pa'][l].astype(BF16), w_pb=p['w_pb'][l].astype(BF16), w_pc=p['w_pc'][l].astype(BF16),
        w_o=p['w_o'][l].astype(BF16),
        ssm_conv_w=p['ssm_conv_w'][l], ssm_conv_b=p['ssm_conv_b'][l].reshape(1, S_CONV_CH),
        ssm_dt_bias=pad_lane(p['ssm_dt_bias'][l]), ssm_a_log=pad_lane(p['ssm_a_log'][l]),
        ssm_dvec=jnp.repeat(p['ssm_d'][l], S_HD).reshape(1, S_W),
        ssm_norm_g=p['ssm_norm_g'][l].reshape(1, S_W),
        rw_mu=p['rw_mu'][l].reshape(1, R_IN_W), rw_w0=p['rw_w0'][l].reshape(1, R_W),
        rw_w2p=zl(0, p['rw_w2'][l]), rw_a0=p['rw_a0'][l].reshape(1, R_W),
        rw_a2p=zl(R_LORA_W, p['rw_a2'][l]), rw_g2p=zl(R_LORA_W + R_LORA_A, p['rw_g2'][l]),
        rw_kk=p['rw_kk'][l].reshape(1, R_W), rw_ka=p['rw_ka'][l].reshape(1, R_W),
        rw_rk=p['rw_rk'][l].reshape(1, R_W), rw_ln_g=p['rw_ln_g'][l].reshape(1, R_W),
        rw_ln_b=p['rw_ln_b'][l].reshape(1, R_W),
        norm2_g=p['norm2_g'][l].reshape(1, D_MODEL), w_up=p['w_up'][l].astype(BF16),
        w_down=p['w_down'][l].astype(BF16), ffn_conv_w=p['ffn_conv_w'][l],
        ffn_conv_b=p['ffn_conv_b'][l].reshape(1, D_FF))


def _row_tile(t):
    for tm in (512, 256, 128):
        if t % tm == 0:
            return tm
    raise ValueError(t)


def _ffn_tf():
    return D_FF // 2


def _prompt_layer(x, lp, bsz, seq, slopes_pair, gf, final_norm):
    t = bsz * seq
    tm = _row_tile(seq)
    tiles = seq // tm
    proj = _rms_matmul(x, lp['norm1_g'], lp['w_proj'], tm, 1024)
    proj3 = proj.reshape(bsz, seq, N_PROJ)
    oa = _moba_prompt(proj3, slopes_pair).reshape(t, A_W)
    ob, ssm_new = _ssd_prompt(proj, bsz, seq, lp['ssm_conv_w'], lp['ssm_conv_b'], lp['ssm_dt_bias'],
                              lp['ssm_a_log'], lp['ssm_dvec'], lp['ssm_norm_g'])
    r, w, k2, v, kk, kka, g = _rwkv_prep(
        proj, jnp.zeros((SUBLANES, R_IN_W), F32), lp['rw_mu'], lp['rw_w0'], lp['rw_w2p'], lp['rw_a0'],
        lp['rw_a2p'], lp['rw_g2p'], lp['rw_kk'], lp['rw_ka'], tm, 1, tiles)
    yr, rw_new = _rwkv_scan_prompt((r, w, k2, v, kk, kka), bsz, seq, LANES)
    x = _merge(x, proj, oa, ob, yr, r, k2, v, g, lp['b_gate'], lp['rw_ln_g'], lp['rw_ln_b'], lp['rw_rk'],
               lp['w_pa'], lp['w_pb'], lp['w_pc'], lp['w_o'], min(tm, 256))
    x, tail = _ffn(x, lp['norm2_g'], lp['w_up'], lp['w_down'], lp['ffn_conv_w'], lp['ffn_conv_b'],
                   jnp.zeros((SUBLANES, D_FF), F32), gf, tm, _ffn_tf(), 1, tiles, final_norm)
    k_new = proj3[:, :, C_K:C_K + A_W].reshape(bsz, seq, A_HEADS, A_HD)
    v_new = proj3[:, :, C_V:C_V + A_W].reshape(bsz, seq, A_HEADS, A_HD)
    ssm_conv_new = proj3[:, seq - (S_CONV - 1):, C_XBC:C_XBC + S_CONV_CH]
    shift_new = proj3[:, seq - 1:, C_RW:C_RW + R_IN_W]
    ffn_conv_new = tail.reshape(bsz, tiles, SUBLANES, D_FF)[:, tiles - 1, SUBLANES - (F_CONV - 1):]
    state = (k_new, v_new, ssm_new.reshape(bsz, S_HEADS, S_HD, S_STATE), ssm_conv_new,
             _unpack_rwkv_blockdiag(rw_new), shift_new, ffn_conv_new)
    return x, state


def _sample_layer(x, lp, dbsz, tnew, st, cache_k4, cache_v4, page_table, layer, gf, final_norm):
    t = tnew * dbsz
    ssm0, ssm_conv0, rwkv0, shift0, ffn_conv0 = st
    proj = _rms_matmul(x, lp['norm1_g'], lp['w_proj'], t, 1024)
    proj3 = proj.reshape(tnew, dbsz, N_PROJ)

    qkv = proj3[:, :, C_Q:C_Q + 3 * A_W].transpose(1, 0, 2)
    qkv8 = jnp.pad(qkv, ((0, 0), (0, SUBLANES - tnew), (0, 0)))
    oa8 = _moba_sample(qkv8[:, :, :A_W], qkv8[:, :, A_W:2 * A_W], qkv8[:, :, 2 * A_W:],
                       cache_k4, cache_v4, page_table, layer, tnew)
    oa = oa8[:, :tnew].transpose(1, 0, 2).reshape(t, A_W)

    pre4 = ssm_conv0.transpose(1, 0, 2).reshape(S_CONV - 1, dbsz, 1, S_CONV_CH)
    h0 = ssm0.reshape(dbsz, N_PAIRS, 2 * S_HD, S_STATE)
    ob4, ssm_new = _ssd_sample(proj.reshape(tnew, dbsz, 1, N_PROJ), pre4, h0, lp['ssm_conv_w'],
                               lp['ssm_conv_b'], lp['ssm_dt_bias'], lp['ssm_a_log'], lp['ssm_dvec'],
                               lp['ssm_norm_g'])
    ob = ob4.reshape(t, S_W)
    ssm_new = ssm_new.reshape(dbsz, S_HEADS, S_HD, S_STATE)

    r, w, k2, v, kk, kka, g = _rwkv_prep(
        proj, shift0.reshape(dbsz, R_IN_W), lp['rw_mu'], lp['rw_w0'], lp['rw_w2p'], lp['rw_a0'],
        lp['rw_a2p'], lp['rw_g2p'], lp['rw_kk'], lp['rw_ka'], t, dbsz, 1)
    to4 = lambda a: a.reshape(tnew, dbsz, 1, R_W)
    yr4, rw_new = _rwkv_scan_sample(tuple(to4(a) for a in (r, w, k2, v, kk, kka)), _pack_rwkv_state(rwkv0))
    yr = yr4.reshape(t, R_W)

    x = _merge(x, proj, oa, ob, yr, r, k2, v, g, lp['b_gate'], lp['rw_ln_g'], lp['rw_ln_b'], lp['rw_rk'],
               lp['w_pa'], lp['w_pb'], lp['w_pc'], lp['w_o'], min(t, 256))
    pre = ffn_conv0.transpose(1, 0, 2).reshape((F_CONV - 1) * dbsz, D_FF)
    x, tail = _ffn(x, lp['norm2_g'], lp['w_up'], lp['w_down'], lp['ffn_conv_w'], lp['ffn_conv_b'],
                   pre, gf, t, _ffn_tf(), dbsz, 1, final_norm)

    k_new = qkv[:, :, A_W:2 * A_W].reshape(dbsz, tnew, A_HEADS, A_HD)
    v_new = qkv[:, :, 2 * A_W:].reshape(dbsz, tnew, A_HEADS, A_HD)
    ssm_conv_new = proj3[tnew - (S_CONV - 1):, :, C_XBC:C_XBC + S_CONV_CH].transpose(1, 0, 2)
    shift_new = proj3[tnew - 1:, :, C_RW:C_RW + R_IN_W].transpose(1, 0, 2)
    ffn_conv_new = tail.reshape(F_CONV - 1, dbsz, D_FF).transpose(1, 0, 2)
    state = (k_new, v_new, ssm_new, ssm_conv_new, _unpack_rwkv_state(rw_new), shift_new, ffn_conv_new)
    return x, state


def kernel(x_prompt, x_sample, cache_k, cache_v, state_ssm, state_ssm_conv, state_rwkv, state_rwkv_shift, state_ffn_conv, page_table, norm1_g, w_in, b_gate, w_pa, ssm_conv_w, ssm_conv_b, ssm_dt_bias, ssm_a_log, ssm_d, ssm_norm_g, w_pb, rw_mu, rw_w0, rw_w2, rw_a0, rw_a2, rw_g2, rw_kk, rw_ka, rw_rk, rw_ln_g, rw_ln_b, w_pc, w_o, norm2_g, w_up, ffn_conv_w, ffn_conv_b, w_down, norm_f_g):
    params = dict(norm1_g=norm1_g, w_in=w_in, b_gate=b_gate, w_pa=w_pa, ssm_conv_w=ssm_conv_w,
                  ssm_conv_b=ssm_conv_b, ssm_dt_bias=ssm_dt_bias, ssm_a_log=ssm_a_log, ssm_d=ssm_d,
                  ssm_norm_g=ssm_norm_g, w_pb=w_pb, rw_mu=rw_mu, rw_w0=rw_w0, rw_w2=rw_w2, rw_a0=rw_a0,
                  rw_a2=rw_a2, rw_g2=rw_g2, rw_kk=rw_kk, rw_ka=rw_ka, rw_rk=rw_rk, rw_ln_g=rw_ln_g,
                  rw_ln_b=rw_ln_b, w_pc=w_pc, w_o=w_o, norm2_g=norm2_g, w_up=w_up, ffn_conv_w=ffn_conv_w,
                  ffn_conv_b=ffn_conv_b, w_down=w_down)
    depth = w_in.shape[0]
    bsz, seq, _ = x_prompt.shape
    dbsz, tnew, _ = x_sample.shape
    head = jnp.arange(A_HEADS, dtype=F32) + 1.0
    slopes = jnp.exp2(-8.0 * head / A_HEADS)
    slopes_pair = jnp.repeat(slopes, A_HD).reshape(N_PAIRS, 1, LANES)
    cache_k4 = cache_k.transpose(0, 1, 3, 4, 2)
    cache_v4 = cache_v.transpose(0, 1, 3, 4, 2)
    gf = norm_f_g.reshape(1, D_MODEL)

    hp = x_prompt.reshape(bsz * seq, D_MODEL)
    hs = x_sample.transpose(1, 0, 2).reshape(tnew * dbsz, D_MODEL)
    new_p = [[] for _ in range(7)]
    new_s = [[] for _ in range(7)]
    for l in range(depth):
        lp = _prep_layer_params(l, params)
        last = l == depth - 1
        hp, sp = _prompt_layer(hp, lp, bsz, seq, slopes_pair, gf, last)
        st = (state_ssm[l], state_ssm_conv[l], state_rwkv[l], state_rwkv_shift[l], state_ffn_conv[l])
        hs, ss = _sample_layer(hs, lp, dbsz, tnew, st, cache_k4, cache_v4, page_table, l, gf, last)
        for j in range(7):
            new_p[j].append(sp[j])
            new_s[j].append(ss[j])
    y_prompt = hp.reshape(bsz, seq, D_MODEL)
    y_sample = hs.reshape(tnew, dbsz, D_MODEL).transpose(1, 0, 2)
    outs = [y_prompt, y_sample]
    for j in range(7):
        outs += [jnp.stack(new_p[j]), jnp.stack(new_s[j])]
    return tuple(outs)
```

```python
import functools
import math

import jax
import jax.numpy as jnp
from jax import lax
from jax.experimental import pallas as pl
from jax.experimental.pallas import tpu as pltpu

F32 = jnp.float32
BF16 = jnp.bfloat16

D_MODEL = 1024
A_HEADS = 8
A_HD = 64
A_W = A_HEADS * A_HD
MOBA_BLOCK = 256
MOBA_TOPK = 3
Q_BLOCK = MOBA_BLOCK
ATTN_SCALE = A_HD ** -0.5
S_HEADS = 8
S_HD = 64
S_W = S_HEADS * S_HD
S_GROUPS = 2
S_STATE = 128
S_CONV = 4
S_CONV_CH = S_W + 2 * S_GROUPS * S_STATE
SSD_CHUNK = 128
R_HEADS = 8
R_HD = 64
R_W = R_HEADS * R_HD
R_LORA_W = 64
R_LORA_A = 64
R_LORA_G = 128
R_LORA = R_LORA_W + R_LORA_A + R_LORA_G
R_IN_W = 3 * R_W + R_LORA
D_FF = ((8 * D_MODEL // 3 + 127) // 128) * 128
F_CONV = 3
NORM_EPS = 1e-6
GN_EPS = 64e-5
NEG_INF = -1e30

LANES = 128
SUBLANES = 8
HALF = 64
HALF_SHIFT = 6
SUBLANE_SHIFT = 3
N_PAIRS = 4

C_RW = 0
C_DT = R_IN_W
C_Q = 2048
C_K = C_Q + A_W
C_V = C_K + A_W
C_Z = C_V + A_W
C_XBC = C_Z + S_W
C_GATE = C_XBC + S_CONV_CH
N_PROJ = C_GATE + 3 * D_MODEL

VMEM_LIMIT = 56 * 1024 * 1024


def _cparams(sem):
    return pltpu.CompilerParams(dimension_semantics=sem, vmem_limit_bytes=VMEM_LIMIT)


def _dot(a, b):
    return jnp.dot(a.astype(BF16), b.astype(BF16), preferred_element_type=F32)


def _dot_nt(a, b):
    return lax.dot_general(a.astype(BF16), b.astype(BF16), (((1,), (1,)), ((), ())),
                           preferred_element_type=F32)


def _dot_tn(a, b):
    return lax.dot_general(a.astype(BF16), b.astype(BF16), (((0,), (0,)), ((), ())),
                           preferred_element_type=F32)


def _split_dot(x, w, passes, x_is_lhs=True):
    w = w.astype(BF16)
    acc = None
    rem = x
    for _ in range(passes):
        piece = rem.astype(BF16)
        term = (jnp.dot(piece, w, preferred_element_type=F32) if x_is_lhs
                else jnp.dot(w, piece, preferred_element_type=F32))
        acc = term if acc is None else acc + term
        rem = rem - piece.astype(F32)
    return acc


def _dot_nt_hi(a, b):
    return lax.dot_general(a, b, (((1,), (1,)), ((), ())), preferred_element_type=F32,
                           precision=lax.Precision.HIGHEST)


def _sigmoid(x):
    return 1.0 / (1.0 + jnp.exp(-x))


def _silu(x):
    return x * _sigmoid(x)


def _softplus(x):
    return jnp.maximum(x, 0.0) + jnp.log(1.0 + jnp.exp(-jnp.abs(x)))


def _head_ones(width, scale):
    r = lax.shift_right_logical(lax.broadcasted_iota(jnp.int32, (width, width), 0), HALF_SHIFT)
    c = lax.shift_right_logical(lax.broadcasted_iota(jnp.int32, (width, width), 1), HALF_SHIFT)
    return jnp.where(r == c, scale, 0.0).astype(F32)


def _rms_matmul_kernel(x_ref, g_ref, w_ref, o_ref, xn_ref):
    @pl.when(pl.program_id(1) == 0)
    def _():
        x = x_ref[...]
        ms = jnp.mean(x * x, axis=-1, keepdims=True)
        xn_ref[...] = (x * lax.rsqrt(ms + NORM_EPS) * g_ref[...]).astype(BF16)

    o_ref[...] = jnp.dot(xn_ref[...], w_ref[...], preferred_element_type=F32)


def _rms_matmul(x, g, w, tm, tn):
    t, d = x.shape
    n = w.shape[1]
    return pl.pallas_call(
        _rms_matmul_kernel,
        grid=(t // tm, n // tn),
        in_specs=[pl.BlockSpec((tm, d), lambda i, j: (i, 0)),
                  pl.BlockSpec((1, d), lambda i, j: (0, 0)),
                  pl.BlockSpec((d, tn), lambda i, j: (0, j))],
        out_specs=pl.BlockSpec((tm, tn), lambda i, j: (i, j)),
        out_shape=jax.ShapeDtypeStruct((t, n), F32),
        scratch_shapes=[pltpu.VMEM((tm, d), BF16)],
        compiler_params=_cparams(("parallel", "arbitrary")),
        name="in_proj",
    )(x, g, w)


def _topk_bias(gate, n_valid, axis=1):
    pos = lax.broadcasted_iota(jnp.int32, gate.shape, axis)
    pos_f = pos.astype(F32)
    gm = jnp.where(pos < n_valid, gate, NEG_INF)
    selected = jnp.zeros(gate.shape, jnp.bool_)
    for j in range(MOBA_TOPK):
        m = jnp.max(gm, axis=axis, keepdims=True)
        idx = jnp.min(jnp.where(gm == m, pos_f, 1e9), axis=axis, keepdims=True)
        hit = pos_f == idx
        selected = jnp.logical_or(selected, jnp.logical_and(hit, j < n_valid))
        gm = jnp.where(hit, -jnp.inf, gm)
    return jnp.where(selected, 0.0, NEG_INF).astype(F32)


def _moba_prompt_kernel(q_ref, k_ref, v_ref, sl_ref, o_ref, kb_ref, vt_ref, kmean_ref, sel_ref, acc_ref,
                        raw_a, raw_b, m_ref, l_ref, *, nb):
    i = pl.program_id(2)
    B = MOBA_BLOCK

    @pl.when(i == 0)
    def _():
        kb_ref[...] = k_ref[0].astype(BF16)
        kmean_ref[...] = jnp.zeros_like(kmean_ref)
        for n in range(nb):
            kmean_ref[n:n + 1, :] = jnp.mean(k_ref[0, n * B:(n + 1) * B, :], axis=0, keepdims=True)
            vt_ref[n] = v_ref[0, n * B:(n + 1) * B, :].T.astype(BF16)

    own = i
    q = q_ref[0]
    lane = lax.broadcasted_iota(jnp.int32, (1, LANES), 1)
    halves = (lane < HALF, lane >= HALF)
    log2e = 1.0 / math.log(2.0)
    slope2 = (sl_ref[0, :, 0:1] * log2e, sl_ref[0, :, HALF:HALF + 1] * log2e)
    kmean = kmean_ref[...]
    qb = []
    for h in range(2):
        q_m = jnp.where(halves[h], q, 0.0)
        bias_t = _topk_bias(_dot_nt_hi(kmean, q_m), own, axis=0)
        for n in range(nb):
            sel_ref[h, n] = jnp.broadcast_to(bias_t[n:n + 1, :], (SUBLANES, Q_BLOCK))
        qb.append((q_m * (ATTN_SCALE * log2e)).astype(BF16))

    rel = (lax.broadcasted_iota(jnp.int32, (B, Q_BLOCK), 0) - lax.broadcasted_iota(jnp.int32, (B, Q_BLOCK), 1))
    alibi = [slope2[h] * rel.astype(F32) for h in range(2)]
    row_a = lax.broadcasted_iota(jnp.int32, (LANES, 1), 0) < HALF

    def put_scores(n, dst_ref):
        kn = kb_ref[pl.ds(pl.multiple_of(n * B, B), B), :]
        for h in range(2):
            dst_ref[h] = _dot_nt(kn, qb[h])

    def row_of(ref, h):
        return ref[h][0:1, :]

    def put_row(ref, h, x):
        ref[h] = jnp.broadcast_to(x, (SUBLANES, Q_BLOCK))

    def attend(n, src_ref):
        shift = ((n - i) * B).astype(F32)
        ps, alphas = [], []
        for h in range(2):
            row = sel_ref[h, n][0:1, :] + slope2[h] * shift
            s = src_ref[h] + alibi[h] + row
            m0 = row_of(m_ref, h)
            m1 = jnp.maximum(m0, jnp.max(s, axis=0, keepdims=True))
            alpha = jnp.exp2(m0 - m1)
            p = jnp.exp2(s - m1)
            put_row(m_ref, h, m1)
            put_row(l_ref, h, alpha * row_of(l_ref, h) + jnp.sum(p, axis=0, keepdims=True))
            ps.append(p.astype(BF16))
            alphas.append(alpha)
        vtn = vt_ref[n]
        pv = [jnp.dot(vtn, ps[h], preferred_element_type=F32) for h in range(2)]
        for h in range(2):
            acc_ref[h] = alphas[h] * acc_ref[h] + pv[h]

    put_scores(0, raw_a)

    k_own = kb_ref[pl.ds(pl.multiple_of(own * B, B), B), :]
    own_raw = [_dot_nt(k_own, qb[h]) for h in range(2)]
    own_p = []
    for h in range(2):
        s = jnp.where(rel <= 0, own_raw[h] + alibi[h], NEG_INF)
        m = jnp.max(s, axis=0, keepdims=True)
        p = jnp.exp2(s - m)
        put_row(m_ref, h, m)
        put_row(l_ref, h, jnp.sum(p, axis=0, keepdims=True))
        own_p.append(p.astype(BF16))
    vt_own = vt_ref[own]
    for h in range(2):
        acc_ref[h] = jnp.dot(vt_own, own_p[h], preferred_element_type=F32)

    def body(j, carry):
        n0 = 2 * j
        put_scores(n0 + 1, raw_b)
        attend(n0, raw_a)
        put_scores(jnp.minimum(n0 + 2, nb - 1), raw_a)
        attend(n0 + 1, raw_b)
        return carry

    lax.fori_loop(0, own // 2, body, 0)

    @pl.when(own % 2 == 1)
    def _():
        attend(own - 1, raw_a)

    out_t = jnp.where(row_a, acc_ref[0] / row_of(l_ref, 0), acc_ref[1] / row_of(l_ref, 1))
    o_ref[0] = out_t.T


def _moba_prompt(proj3, slopes_pair):
    bsz, seq, _ = proj3.shape
    assert seq % MOBA_BLOCK == 0 and seq // MOBA_BLOCK >= MOBA_TOPK
    nb = seq // MOBA_BLOCK
    nbp = -(-nb // SUBLANES) * SUBLANES
    nq = seq // Q_BLOCK
    qc, kc, vc = C_Q // LANES, C_K // LANES, C_V // LANES
    return pl.pallas_call(
        functools.partial(_moba_prompt_kernel, nb=nb),
        grid=(bsz, N_PAIRS, nq),
        in_specs=[pl.BlockSpec((1, Q_BLOCK, LANES), lambda b, p, i: (b, i, qc + p)),
                  pl.BlockSpec((1, seq, LANES), lambda b, p, i: (b, 0, kc + p)),
                  pl.BlockSpec((1, seq, LANES), lambda b, p, i: (b, 0, vc + p)),
                  pl.BlockSpec((1, 1, LANES), lambda b, p, i: (p, 0, 0))],
        out_specs=pl.BlockSpec((1, Q_BLOCK, LANES), lambda b, p, i: (b, i, p)),
        out_shape=jax.ShapeDtypeStruct((bsz, seq, A_W), F32),
        scratch_shapes=[pltpu.VMEM((seq, LANES), BF16), pltpu.VMEM((nb, LANES, MOBA_BLOCK), BF16),
                        pltpu.VMEM((nbp, LANES), F32), pltpu.VMEM((2, nb, SUBLANES, Q_BLOCK), F32),
                        pltpu.VMEM((2, LANES, Q_BLOCK), F32),
                        pltpu.VMEM((2, MOBA_BLOCK, Q_BLOCK), F32), pltpu.VMEM((2, MOBA_BLOCK, Q_BLOCK), F32),
                        pltpu.VMEM((2, SUBLANES, Q_BLOCK), F32), pltpu.VMEM((2, SUBLANES, Q_BLOCK), F32)],
        compiler_params=_cparams(("parallel", "parallel", "arbitrary")),
        name="moba_prompt",
    )(proj3, proj3, proj3, slopes_pair)


def _moba_sample_kernel(pt_ref, q_ref, kn_ref, vn_ref, *refs, n_blk, ppb, page, tnew):
    del pt_ref
    n_pages = n_blk * ppb
    k_refs, v_refs = refs[:n_pages], refs[n_pages:2 * n_pages]
    o_ref, m_ref, l_ref, g_ref, acc_ref = refs[2 * n_pages:]
    past = n_blk * MOBA_BLOCK
    rows = A_HEADS * SUBLANES
    lane_head = lax.shift_right_logical(lax.broadcasted_iota(jnp.int32, (SUBLANES, A_W), 1), HALF_SHIFT)
    q8 = q_ref[0] * ATTN_SCALE
    qbd = jnp.concatenate([jnp.where(lane_head == h, q8, 0.0) for h in range(A_HEADS)], axis=0)
    q_hi = qbd.astype(BF16)
    q_lo = (qbd - q_hi.astype(F32)).astype(BF16)
    r = lax.broadcasted_iota(jnp.int32, (rows, 1), 0)
    tok = jnp.bitwise_and(r, SUBLANES - 1).astype(F32)
    slope = jnp.exp2(-(8.0 / A_HEADS) * (lax.shift_right_logical(r, SUBLANE_SHIFT) + 1).astype(F32))
    key = lax.broadcasted_iota(jnp.int32, (1, MOBA_BLOCK), 1).astype(F32)

    def block_t(refs_, n):
        return jnp.concatenate([refs_[j][0, 0].reshape(A_W, page) for j in range(n * ppb, (n + 1) * ppb)],
                               axis=1).astype(BF16)

    raws, fixes = [], []
    for n in range(n_blk):
        kt = block_t(k_refs, n)
        raws.append(jnp.dot(q_hi, kt, preferred_element_type=F32))
        fixes.append(jnp.dot(q_lo, kt, preferred_element_type=F32))
    es = []
    for n in range(n_blk):
        g_ref[n] = jnp.broadcast_to(jnp.sum(raws[n] + fixes[n], axis=-1, keepdims=True) * (1.0 / MOBA_BLOCK),
                                    (rows, LANES))
        s = raws[n] - slope * ((past - n * MOBA_BLOCK + tok) - key)
        m = jnp.max(s, axis=-1, keepdims=True)
        e = jnp.exp(s - m)
        m_ref[n] = jnp.broadcast_to(m, (rows, LANES))
        l_ref[n] = jnp.broadcast_to(jnp.sum(e, axis=-1, keepdims=True), (rows, LANES))
        es.append(e.astype(BF16))
    for n in range(n_blk):
        acc_ref[n] = lax.dot_general(es[n], block_t(v_refs, n), (((1,), (1,)), ((), ())),
                                     preferred_element_type=F32)

    lane = lax.broadcasted_iota(jnp.int32, (rows, LANES), 1)
    gate = jnp.zeros((rows, LANES), F32)
    for j in range(n_blk):
        gate = jnp.where(lane == j, g_ref[j], gate)
    bias = _topk_bias(gate, n_blk)

    kn = kn_ref[0]
    vn = vn_ref[0]
    s_own = []
    for j in range(tnew):
        sj = jnp.sum(qbd * kn[j:j + 1, :], axis=-1, keepdims=True) - slope * (tok - j)
        s_own.append(jnp.where(tok >= j, sj, NEG_INF))
    mx = s_own[0]
    for j in range(1, tnew):
        mx = jnp.maximum(mx, s_own[j])
    mb = []
    for j in range(n_blk):
        mj = m_ref[j][:, 0:1] + bias[:, j:j + 1]
        mb.append(mj)
        mx = jnp.maximum(mx, mj)
    lsum = jnp.zeros((rows, 1), F32)
    acc = jnp.zeros((rows, A_W), F32)
    for j in range(tnew):
        w = jnp.exp(s_own[j] - mx)
        lsum = lsum + w
        acc = acc + w * vn[j:j + 1, :]
    for j in range(n_blk):
        w = jnp.exp(mb[j] - mx)
        lsum = lsum + w * l_ref[j][:, 0:1]
        acc = acc + w * acc_ref[j]
    out = acc / lsum
    o8 = jnp.zeros((SUBLANES, A_W), F32)
    for h in range(A_HEADS):
        o8 = o8 + jnp.where(lane_head == h, out[h * SUBLANES:(h + 1) * SUBLANES, :], 0.0)
    o_ref[0] = o8


def _moba_sample(q8, k8, v8, cache_kt, cache_vt, page_table, layer, tnew):
    dbsz = q8.shape[0]
    n_pages = page_table.shape[1]
    page = cache_kt.shape[4]
    assert MOBA_BLOCK % page == 0 and (n_pages * page) % MOBA_BLOCK == 0 and page % LANES == 0
    ppb = MOBA_BLOCK // page
    n_blk = n_pages // ppb
    assert n_blk >= MOBA_TOPK and tnew <= SUBLANES
    rows = A_HEADS * SUBLANES
    tok_spec = pl.BlockSpec((1, SUBLANES, A_W), lambda b, pt: (b, 0, 0))

    def page_spec(j):
        return pl.BlockSpec((1, 1, A_HEADS, A_HD, page), lambda b, pt: (layer, pt[b * n_pages + j], 0, 0, 0))

    pages = [page_spec(j) for j in range(n_pages)]
    grid_spec = pltpu.PrefetchScalarGridSpec(
        num_scalar_prefetch=1,
        grid=(dbsz,),
        in_specs=[tok_spec, tok_spec, tok_spec] + pages + pages,
        out_specs=tok_spec,
        scratch_shapes=[pltpu.VMEM((n_blk, rows, LANES), F32), pltpu.VMEM((n_blk, rows, LANES), F32),
                        pltpu.VMEM((n_blk, rows, LANES), F32), pltpu.VMEM((n_blk, rows, A_W), F32)],
    )
    return pl.pallas_call(
        functools.partial(_moba_sample_kernel, n_blk=n_blk, ppb=ppb, page=page, tnew=tnew),
        grid_spec=grid_spec,
        out_shape=jax.ShapeDtypeStruct((dbsz, SUBLANES, A_W), F32),
        compiler_params=_cparams(("parallel",)),
        name="moba_sample",
    )(page_table.reshape(-1), q8, k8, v8, *([cache_kt] * n_pages), *([cache_vt] * n_pages))


def _ssd_prompt_kernel(z_ref, xbc_ref, dt_ref, cw_ref, cb_ref, dtb_ref, alog_ref, dvec_ref, ng_ref,
                       o_ref, hout_ref, ext_ref, h_ref, *, L):
    c = pl.program_id(1)
    P = SUBLANES

    @pl.when(c == 0)
    def _():
        ext_ref[0:P, :] = jnp.zeros((P, S_CONV_CH), F32)
        h_ref[...] = jnp.zeros_like(h_ref)

    @pl.when(c > 0)
    def _():
        ext_ref[0:P, :] = ext_ref[L:L + P, :]

    ext_ref[P:P + L, :] = xbc_ref[...]
    acc = cb_ref[...] + cw_ref[S_CONV - 1:S_CONV, :] * ext_ref[P:P + L, :]
    for d in range(1, S_CONV):
        acc = acc + cw_ref[S_CONV - 1 - d:S_CONV - d, :] * ext_ref[P - d:P - d + L, :]
    xbc = _silu(acc)
    xs = xbc[:, :S_W]
    bm = xbc[:, S_W:S_W + S_GROUPS * S_STATE]
    cm = xbc[:, S_W + S_GROUPS * S_STATE:]

    lane = lax.broadcasted_iota(jnp.int32, (1, LANES), 1)
    is_a = lane < HALF
    dt = jnp.where(lane < S_HEADS, _softplus(dt_ref[...] + dtb_ref[...]), 0.0)
    da = dt * (-jnp.exp(alog_ref[...]))
    rr = lax.broadcasted_iota(jnp.int32, (L, L), 0)
    cc = lax.broadcasted_iota(jnp.int32, (L, L), 1)
    causal = rr >= cc
    cum = _split_dot(da, jnp.where(causal, 1.0, 0.0), 3, x_is_lhs=False)
    cum_t = cum.T
    dt_t = dt.T
    row_a = lax.broadcasted_iota(jnp.int32, (2 * S_HD, 1), 0) < S_HD

    ys = []
    cb_g = [None] * S_GROUPS
    for pr in range(N_PAIRS):
        g = (2 * pr * S_GROUPS) // S_HEADS
        bg = bm[:, g * S_STATE:(g + 1) * S_STATE]
        cg = cm[:, g * S_STATE:(g + 1) * S_STATE]
        if cb_g[g] is None:
            cb_g[g] = _dot_nt(cg, bg)
        xs_p = xs[:, pr * LANES:(pr + 1) * LANES]
        heads = (2 * pr, 2 * pr + 1)
        halves = (is_a, jnp.logical_not(is_a))
        yp = jnp.zeros((L, LANES), F32)
        cum_c = [cum[:, h:h + 1] for h in heads]
        for h, half, cc_h in zip(heads, halves, cum_c):
            seg = cc_h - cum_t[h:h + 1, :]
            dec = jnp.exp(jnp.where(causal, seg, -jnp.inf))
            wts = cb_g[g] * dec * dt_t[h:h + 1, :]
            yp = yp + _dot(wts, jnp.where(half, xs_p, 0.0))
        hp = h_ref[pr]
        yp = yp + _dot_nt(cg, hp) * jnp.where(is_a, jnp.exp(cum_c[0]), jnp.exp(cum_c[1]))
        last = [cum[L - 1:L, h:h + 1] for h in heads]
        te = jnp.where(is_a, jnp.exp(last[0] - cum_c[0]) * dt[:, heads[0]:heads[0] + 1],
                       jnp.exp(last[1] - cum_c[1]) * dt[:, heads[1]:heads[1] + 1])
        st = _dot_tn(xs_p * te, bg)
        h_ref[pr] = hp * jnp.where(row_a, jnp.exp(last[0]), jnp.exp(last[1])) + st
        ys.append(yp)
    y = jnp.concatenate(ys, axis=1) + dvec_ref[...] * xs
    yz = y * _silu(z_ref[...])
    gw = S_W // S_GROUPS
    for g in range(S_GROUPS):
        part = yz[:, g * gw:(g + 1) * gw]
        ms = jnp.mean(part * part, axis=-1, keepdims=True)
        o_ref[:, g * gw:(g + 1) * gw] = part * lax.rsqrt(ms + NORM_EPS) * ng_ref[:, g * gw:(g + 1) * gw]

    @pl.when(c == pl.num_programs(1) - 1)
    def _():
        hout_ref[0] = h_ref[...]


def _ssd_prompt(proj, bsz, seq, cw, cb, dtb, alog, dvec, ng):
    L = SSD_CHUNK
    nc = seq // L
    const = lambda shape: pl.BlockSpec(shape, lambda b, c: (0,) * len(shape))
    return pl.pallas_call(
        functools.partial(_ssd_prompt_kernel, L=L),
        grid=(bsz, nc),
        in_specs=[pl.BlockSpec((L, S_W), lambda b, c: (b * nc + c, C_Z // S_W)),
                  pl.BlockSpec((L, S_CONV_CH), lambda b, c: (b * nc + c, C_XBC // S_CONV_CH)),
                  pl.BlockSpec((L, LANES), lambda b, c: (b * nc + c, C_DT // LANES)),
                  const((S_CONV, S_CONV_CH)), const((1, S_CONV_CH)), const((1, LANES)), const((1, LANES)),
                  const((1, S_W)), const((1, S_W))],
        out_specs=[pl.BlockSpec((L, S_W), lambda b, c: (b * nc + c, 0)),
                   pl.BlockSpec((1, N_PAIRS, 2 * S_HD, S_STATE), lambda b, c: (b, 0, 0, 0))],
        out_shape=[jax.ShapeDtypeStruct((bsz * seq, S_W), F32),
                   jax.ShapeDtypeStruct((bsz, N_PAIRS, 2 * S_HD, S_STATE), F32)],
        scratch_shapes=[pltpu.VMEM((L + 2 * SUBLANES, S_CONV_CH), F32),
                        pltpu.VMEM((N_PAIRS, 2 * S_HD, S_STATE), F32)],
        compiler_params=_cparams(("parallel", "arbitrary")),
        name="ssd_prompt",
    )(proj, proj, proj, cw, cb, dtb, alog, dvec, ng)


def _ssd_sample_kernel(z_ref, xbc_ref, dt_ref, pre_ref, h0_ref, cw_ref, cb_ref, dtb_ref, alog_ref,
                       dvec_ref, ng_ref, o_ref, hout_ref, *, T, nb):
    lane = lax.broadcasted_iota(jnp.int32, (1, LANES), 1)
    a_row = -jnp.exp(alog_ref[...])
    rows2 = 2 * S_HD
    eye = (lax.broadcasted_iota(jnp.int32, (rows2, LANES), 0)
           == lax.broadcasted_iota(jnp.int32, (rows2, LANES), 1)).astype(F32)
    row_a = lax.broadcasted_iota(jnp.int32, (rows2, 1), 0) < S_HD
    gw = S_W // S_GROUPS

    def per_seq(b, carry):
        up = [pre_ref[j, b] for j in range(S_CONV - 1)] + [xbc_ref[t, b] for t in range(T)]
        xc, dts = [], []
        for t in range(T):
            acc = cb_ref[...] + cw_ref[0:1, :] * up[t]
            for j in range(1, S_CONV):
                acc = acc + cw_ref[j:j + 1, :] * up[t + j]
            xc.append(_silu(acc))
            dts.append(jnp.where(lane < S_HEADS, _softplus(dt_ref[t, b] + dtb_ref[...]), 0.0))
        ys = [[] for _ in range(T)]
        hs = [h0_ref[0, b, pr] for pr in range(N_PAIRS)]
        for t in range(T):
            for pr in range(N_PAIRS):
                g = (2 * pr * S_GROUPS) // S_HEADS
                ha, hb = 2 * pr, 2 * pr + 1
                x_row = xc[t][:, pr * LANES:(pr + 1) * LANES]
                b_row = xc[t][:, S_W + g * S_STATE:S_W + (g + 1) * S_STATE]
                c_row = xc[t][:, S_W + (S_GROUPS + g) * S_STATE:S_W + (S_GROUPS + g + 1) * S_STATE]
                dt_col = jnp.where(row_a, dts[t][:, ha:ha + 1], dts[t][:, hb:hb + 1])
                a_col = jnp.where(row_a, a_row[:, ha:ha + 1], a_row[:, hb:hb + 1])
                x_col = jnp.sum(eye * x_row, axis=-1, keepdims=True)
                hs[pr] = hs[pr] * jnp.exp(dt_col * a_col) + (x_col * dt_col) * b_row
                y_col = jnp.sum(hs[pr] * c_row, axis=-1, keepdims=True)
                ys[t].append(jnp.sum(eye * y_col, axis=0, keepdims=True))
        for pr in range(N_PAIRS):
            hout_ref[b, pr] = hs[pr]
        for t in range(T):
            y = jnp.concatenate(ys[t], axis=1) + dvec_ref[...] * xc[t][:, :S_W]
            yz = y * _silu(z_ref[t, b])
            for g in range(S_GROUPS):
                part = yz[:, g * gw:(g + 1) * gw]
                ms = jnp.mean(part * part, axis=-1, keepdims=True)
                o_ref[t, b, :, g * gw:(g + 1) * gw] = (part * lax.rsqrt(ms + NORM_EPS)
                                                       * ng_ref[:, g * gw:(g + 1) * gw])
        return carry

    lax.fori_loop(0, nb, per_seq, 0)


def _ssd_sample(z4, xbc4, dt4, pre4, h0, layer, cw, cb, dtb, alog, dvec, ng, nb=16):
    T, dbsz = z4.shape[:2]
    const = lambda shape: pl.BlockSpec(shape, lambda j: (0,) * len(shape))
    hspec = pl.BlockSpec((nb, N_PAIRS, 2 * S_HD, S_STATE), lambda j: (j, 0, 0, 0))
    h0spec = pl.BlockSpec((1, nb, N_PAIRS, 2 * S_HD, S_STATE), lambda j: (layer, j, 0, 0, 0))
    return pl.pallas_call(
        functools.partial(_ssd_sample_kernel, T=T, nb=nb),
        grid=(dbsz // nb,),
        in_specs=[pl.BlockSpec((T, nb, 1, S_W), lambda j: (0, j, 0, 0)),
                  pl.BlockSpec((T, nb, 1, S_CONV_CH), lambda j: (0, j, 0, 0)),
                  pl.BlockSpec((T, nb, 1, LANES), lambda j: (0, j, 0, 0)),
                  pl.BlockSpec((S_CONV - 1, nb, 1, S_CONV_CH), lambda j: (0, j, 0, 0)),
                  h0spec,
                  const((S_CONV, S_CONV_CH)), const((1, S_CONV_CH)), const((1, LANES)), const((1, LANES)),
                  const((1, S_W)), const((1, S_W))],
        out_specs=[pl.BlockSpec((T, nb, 1, S_W), lambda j: (0, j, 0, 0)), hspec],
        out_shape=[jax.ShapeDtypeStruct((T, dbsz, 1, S_W), F32),
                   jax.ShapeDtypeStruct((dbsz, N_PAIRS, 2 * S_HD, S_STATE), F32)],
        compiler_params=_cparams(("parallel",)),
        name="ssd_sample",
    )(z4, xbc4, dt4, pre4, h0, cw, cb, dtb, alog, dvec, ng)


def _rwkv_prep_kernel(u_ref, pre_ref, mu_ref, w0_ref, w2_ref, a0_ref, a2_ref, g2_ref, kk_ref, ka_ref,
                      r_out, w_out, k_out, v_out, kk_out, kka_out, g_out, ext_ref,
                      *, tm, P, stride, tiles_per_seq):
    i = pl.program_id(0)
    first = (i % tiles_per_seq) == 0

    @pl.when(first)
    def _():
        ext_ref[0:P, :] = pre_ref[...]

    @pl.when(jnp.logical_not(first))
    def _():
        ext_ref[0:P, :] = ext_ref[tm:tm + P, :]

    u = u_ref[...]
    ext_ref[P:P + tm, :] = u
    prev = ext_ref[P - stride:P - stride + tm, :]
    x = u + (prev - u) * mu_ref[...]
    r = x[:, 0:R_W]
    kr = x[:, R_W:2 * R_W]
    vr = x[:, 2 * R_W:3 * R_W]
    xl = x[:, 3 * R_W:]
    w_log = w0_ref[...] + _dot(jnp.tanh(xl), w2_ref[...])
    log_decay = -jnp.exp(-_softplus(-w_log) - 0.5)
    a = _sigmoid(a0_ref[...] + _dot(xl, a2_ref[...]))
    g = _dot(_sigmoid(xl), g2_ref[...])
    kk = kr * kk_ref[...]
    ss = _split_dot(kk * kk, _head_ones(R_W, 1.0), 2)
    kk = kk / jnp.maximum(jnp.sqrt(ss), 1e-12)
    r_out[...] = r
    w_out[...] = log_decay
    k_out[...] = kr * (1.0 + (a - 1.0) * ka_ref[...])
    v_out[...] = vr
    kk_out[...] = kk
    kka_out[...] = kk * a
    g_out[...] = g


def _rwkv_prep(proj, pre, mu, w0, w2p, a0, a2p, g2p, kkw, kaw, tm, stride, tiles_per_seq):
    t = proj.shape[0]
    P = pre.shape[0]
    const = lambda shape: pl.BlockSpec(shape, lambda i: (0,) * len(shape))
    outs = pl.pallas_call(
        functools.partial(_rwkv_prep_kernel, tm=tm, P=P, stride=stride, tiles_per_seq=tiles_per_seq),
        grid=(t // tm,),
        in_specs=[pl.BlockSpec((tm, R_IN_W), lambda i: (i, 0)),
                  const((P, R_IN_W)), const((1, R_IN_W)), const((1, R_W)), const((R_LORA, R_W)),
                  const((1, R_W)), const((R_LORA, R_W)), const((R_LORA, R_W)), const((1, R_W)), const((1, R_W))],
        out_specs=[pl.BlockSpec((tm, R_W), lambda i: (i, 0))] * 7,
        out_shape=[jax.ShapeDtypeStruct((t, R_W), F32)] * 7,
        scratch_shapes=[pltpu.VMEM((tm + 2 * P, R_IN_W), F32)],
        compiler_params=_cparams(("arbitrary",)),
        name="rwkv_prep",
    )(proj, pre, mu, w0, w2p, a0, a2p, g2p, kkw, kaw)
    return outs


def _pair_consts():
    lane = lax.broadcasted_iota(jnp.int32, (R_HD, LANES), 1)
    row = lax.broadcasted_iota(jnp.int32, (R_HD, LANES), 0)
    is_a = lane < HALF
    eye2 = (jnp.bitwise_and(lane, HALF - 1) == row).astype(F32)
    return is_a, eye2


def _pair_sum(x, is_a):
    sa = jnp.sum(jnp.where(is_a, x, 0.0), axis=-1, keepdims=True)
    sb = jnp.sum(jnp.where(is_a, 0.0, x), axis=-1, keepdims=True)
    return jnp.where(is_a, sa, sb)


def _rwkv_step(s, rr, ww, kr, vv, kk, kka, is_a, eye2):
    v_col = _pair_sum(eye2 * vv, is_a)
    sk = _pair_sum(s * kk, is_a)
    s = s * ww - sk * kka + v_col * kr
    y_col = _pair_sum(s * rr, is_a)
    return s, jnp.sum(eye2 * y_col, axis=0, keepdims=True)


def _unit_lower_inverses(ns):
    L = ns[0].shape[0]
    eye = (lax.broadcasted_iota(jnp.int32, (L, L), 0) == lax.broadcasted_iota(jnp.int32, (L, L), 1)).astype(F32)
    ts = [eye + n for n in ns]
    pws = list(ns)
    for _ in range(int(math.log2(L)) - 1):
        pws = [_dot(p, p) for p in pws]
        ts = [t + _dot(t, p) for t, p in zip(ts, pws)]
    return ts


def _rwkv_chunk_prompt_kernel(r_ref, lw_ref, k_ref, v_ref, kk_ref, kka_ref, y_ref, sout_ref, s_ref, *, L):
    c = pl.program_id(1)

    @pl.when(c == 0)
    def _():
        s_ref[...] = jnp.zeros_like(s_ref)

    lane = lax.broadcasted_iota(jnp.int32, (1, LANES), 1)
    is_a = lane < HALF
    rr = lax.broadcasted_iota(jnp.int32, (L, L), 0)
    cc = lax.broadcasted_iota(jnp.int32, (L, L), 1)
    incl = rr >= cc
    strict = rr > cc
    tri = jnp.where(incl, 1.0, 0.0).astype(F32)
    r2 = lax.broadcasted_iota(jnp.int32, (LANES, LANES), 0) < HALF
    c2 = lax.broadcasted_iota(jnp.int32, (LANES, LANES), 1) < HALF
    same_head = r2 == c2

    r, lw, k, v, kk, kka = (ref[...] for ref in (r_ref, lw_ref, k_ref, v_ref, kk_ref, kka_ref))
    cum = _split_dot(lw, tri, 3, x_is_lhs=False)
    last = cum[L - 1:L, :]
    inv_p = jnp.exp(-cum)
    to_end = jnp.exp(last - cum)
    b_t = kk * jnp.exp(cum - lw)
    a_t = -kka * inv_p
    k_t = k * inv_p
    r_t = r * jnp.exp(cum)
    a_end = -kka * to_end
    k_end = k * to_end
    decay_end = jnp.exp(last)
    pairs = [slice(pr * LANES, (pr + 1) * LANES) for pr in range(N_PAIRS)]
    halves = (is_a, jnp.logical_not(is_a))
    s0 = [s_ref[pr] for pr in range(N_PAIRS)]

    ns, mks, ras, rks = [], [], [], []
    for sl in pairs:
        for half in halves:
            bm = jnp.where(half, b_t[:, sl], 0.0)
            rm = jnp.where(half, r_t[:, sl], 0.0)
            ns.append(jnp.where(strict, _dot_nt(bm, a_t[:, sl]), 0.0))
            mks.append(jnp.where(strict, _dot_nt(bm, k_t[:, sl]), 0.0))
            ras.append(jnp.where(incl, _dot_nt(rm, a_t[:, sl]), 0.0))
            rks.append(jnp.where(incl, _dot_nt(rm, k_t[:, sl]), 0.0))
    ts = _unit_lower_inverses(ns)
    rhs = [_dot_nt(b_t[:, sl], s0[pr]) + jnp.where(is_a, _dot(mks[2 * pr], v[:, sl]), _dot(mks[2 * pr + 1], v[:, sl]))
           for pr, sl in enumerate(pairs)]
    us = [jnp.where(is_a, _dot(ts[2 * pr], rhs[pr]), _dot(ts[2 * pr + 1], rhs[pr])) for pr in range(N_PAIRS)]
    for pr, sl in enumerate(pairs):
        ha, hb = 2 * pr, 2 * pr + 1
        y_ref[:, sl] = _dot_nt(r_t[:, sl], s0[pr]) + jnp.where(
            is_a, _dot(ras[ha], us[pr]) + _dot(rks[ha], v[:, sl]), _dot(ras[hb], us[pr]) + _dot(rks[hb], v[:, sl]))
    for pr, sl in enumerate(pairs):
        upd = _dot_tn(us[pr], a_end[:, sl]) + _dot_tn(v[:, sl], k_end[:, sl])
        s_ref[pr] = s0[pr] * decay_end[:, sl] + jnp.where(same_head, upd, 0.0)

    @pl.when(c == pl.num_programs(1) - 1)
    def _():
        sout_ref[0] = s_ref[...]


def _rwkv_scan_prompt(seqs, bsz, seq, L):
    nc = seq // L
    spec = pl.BlockSpec((L, R_W), lambda b, c: (b * nc + c, 0))
    return pl.pallas_call(
        functools.partial(_rwkv_chunk_prompt_kernel, L=L),
        grid=(bsz, nc),
        in_specs=[spec] * 6,
        out_specs=[spec, pl.BlockSpec((1, N_PAIRS, LANES, LANES), lambda b, c: (b, 0, 0, 0))],
        out_shape=[jax.ShapeDtypeStruct((bsz * seq, R_W), F32),
                   jax.ShapeDtypeStruct((bsz, N_PAIRS, LANES, LANES), F32)],
        scratch_shapes=[pltpu.VMEM((N_PAIRS, LANES, LANES), F32)],
        compiler_params=_cparams(("parallel", "arbitrary")),
        name="rwkv_scan_prompt",
    )(*seqs)


def _rwkv_scan_sample_kernel(r_ref, w_ref, k_ref, v_ref, kk_ref, kka_ref, s0_ref, y_ref, sout_ref, *, T, nb):
    is_a, eye2 = _pair_consts()

    def per_seq(b, carry):
        states = [s0_ref[b, pr] for pr in range(N_PAIRS)]
        for t in range(T):
            for pr in range(N_PAIRS):
                sl = slice(pr * LANES, (pr + 1) * LANES)
                rows = [ref[t, b, :, sl] for ref in (r_ref, w_ref, k_ref, v_ref, kk_ref, kka_ref)]
                rows[1] = jnp.exp(rows[1])
                states[pr], y_row = _rwkv_step(states[pr], *rows, is_a, eye2)
                y_ref[t, b, :, sl] = y_row
        for pr in range(N_PAIRS):
            sout_ref[b, pr] = states[pr]
        return carry

    lax.fori_loop(0, nb, per_seq, 0)


def _rwkv_scan_sample(seqs4, s0, nb=16):
    T, dbsz = seqs4[0].shape[:2]
    spec = pl.BlockSpec((T, nb, 1, R_W), lambda j: (0, j, 0, 0))
    sspec = pl.BlockSpec((nb, N_PAIRS, R_HD, LANES), lambda j: (j, 0, 0, 0))
    return pl.pallas_call(
        functools.partial(_rwkv_scan_sample_kernel, T=T, nb=nb),
        grid=(dbsz // nb,),
        in_specs=[spec] * 6 + [sspec],
        out_specs=[spec, sspec],
        out_shape=[jax.ShapeDtypeStruct((T, dbsz, 1, R_W), F32),
                   jax.ShapeDtypeStruct((dbsz, N_PAIRS, R_HD, LANES), F32)],
        compiler_params=_cparams(("parallel",)),
        name="rwkv_scan_sample",
    )(*seqs4, s0)


def _merge_kernel(x_ref, oa_ref, ob_ref, yr_ref, r_ref, k_ref, v_ref, g_ref, ga_ref, gb_ref, gc_ref,
                  bg_ref, lng_ref, lnb_ref, rk_ref, wpa_ref, wpb_ref, wpc_ref, wo_ref, o_ref):
    mean_m = _head_ones(R_W, 1.0 / R_HD)
    yr = yr_ref[...]
    d = yr - _split_dot(yr, mean_m, 2)
    var = _split_dot(d * d, mean_m, 2)
    yn = d * lax.rsqrt(var + GN_EPS) * lng_ref[...] + lnb_ref[...]
    v = v_ref[...]
    bonus = _split_dot(r_ref[...] * k_ref[...] * rk_ref[...], _head_ones(R_W, 1.0), 2)
    oc = (yn + bonus * v) * g_ref[...]
    merged = (_sigmoid(ga_ref[...] + bg_ref[:, 0:D_MODEL]) * _dot(oa_ref[...], wpa_ref[...])
              + _sigmoid(gb_ref[...] + bg_ref[:, D_MODEL:2 * D_MODEL]) * _dot(ob_ref[...], wpb_ref[...])
              + _sigmoid(gc_ref[...] + bg_ref[:, 2 * D_MODEL:]) * _dot(oc, wpc_ref[...]))
    o_ref[...] = x_ref[...] + _dot(merged, wo_ref[...])


def _merge(x, proj, oa, ob, yr, r, k2, v, g, bg, lng, lnb, rk, wpa, wpb, wpc, wo, tm):
    t = x.shape[0]
    row = lambda w: pl.BlockSpec((tm, w), lambda i: (i, 0))
    const = lambda shape: pl.BlockSpec(shape, lambda i: (0,) * len(shape))
    gcol = C_GATE // D_MODEL
    gate = lambda j: pl.BlockSpec((tm, D_MODEL), lambda i: (i, gcol + j))
    return pl.pallas_call(
        _merge_kernel,
        grid=(t // tm,),
        in_specs=[row(D_MODEL), row(A_W), row(S_W), row(R_W), row(R_W), row(R_W), row(R_W), row(R_W),
                  gate(0), gate(1), gate(2),
                  const((1, 3 * D_MODEL)), const((1, R_W)), const((1, R_W)), const((1, R_W)),
                  const((A_W, D_MODEL)), const((S_W, D_MODEL)), const((R_W, D_MODEL)),
                  const((D_MODEL, D_MODEL))],
        out_specs=row(D_MODEL),
        out_shape=jax.ShapeDtypeStruct((t, D_MODEL), F32),
        compiler_params=_cparams(("parallel",)),
        name="merge",
    )(x, oa, ob, yr, r, k2, v, g, proj, proj, proj, bg, lng, lnb, rk, wpa, wpb, wpc, wo)


def _ffn_kernel(x_ref, g_ref, wug_ref, wuv_ref, wd_ref, cw_ref, cb_ref, pre_ref, gf_ref,
                o_ref, tail_ref, xn_ref, ext_ref, *, tm, P, stride, tiles_per_seq, final_norm):
    i = pl.program_id(0)
    f = pl.program_id(1)
    first = (i % tiles_per_seq) == 0

    @pl.when(f == 0)
    def _():
        x = x_ref[...]
        ms = jnp.mean(x * x, axis=-1, keepdims=True)
        xn_ref[...] = (x * lax.rsqrt(ms + NORM_EPS) * g_ref[...]).astype(BF16)

    @pl.when(first)
    def _():
        ext_ref[f, 0:P, :] = pre_ref[...]

    @pl.when(jnp.logical_not(first))
    def _():
        ext_ref[f, 0:P, :] = ext_ref[f, tm:tm + P, :]

    xn = xn_ref[...]
    ug = jnp.dot(xn, wug_ref[...], preferred_element_type=F32)
    ext_ref[f, P:P + tm, :] = ug
    tail_ref[0] = ext_ref[f, tm:tm + P, :]
    acc = cb_ref[...] + cw_ref[F_CONV - 1:F_CONV, :] * ug
    for d in range(1, F_CONV):
        acc = acc + cw_ref[F_CONV - 1 - d:F_CONV - d, :] * ext_ref[f, P - d * stride:P - d * stride + tm, :]
    uv = jnp.dot(xn, wuv_ref[...], preferred_element_type=F32)
    contrib = _dot(_silu(acc) * uv, wd_ref[...])

    @pl.when(f == 0)
    def _():
        o_ref[...] = x_ref[...] + contrib

    @pl.when(f > 0)
    def _():
        o_ref[...] = o_ref[...] + contrib

    if final_norm:
        @pl.when(f == pl.num_programs(1) - 1)
        def _():
            y = o_ref[...]
            ms = jnp.mean(y * y, axis=-1, keepdims=True)
            o_ref[...] = y * lax.rsqrt(ms + NORM_EPS) * gf_ref[...]


def _ffn(x, g, wup, wd, cw, cb, pre, gf, tm, tf, stride, tiles_per_seq, final_norm):
    t = x.shape[0]
    P = pre.shape[0]
    nf = D_FF // tf
    return pl.pallas_call(
        functools.partial(_ffn_kernel, tm=tm, P=P, stride=stride, tiles_per_seq=tiles_per_seq,
                          final_norm=final_norm),
        grid=(t // tm, nf),
        in_specs=[pl.BlockSpec((tm, D_MODEL), lambda i, f: (i, 0)),
                  pl.BlockSpec((1, D_MODEL), lambda i, f: (0, 0)),
                  pl.BlockSpec((D_MODEL, tf), lambda i, f: (0, f)),
                  pl.BlockSpec((D_MODEL, tf), lambda i, f: (0, nf + f)),
                  pl.BlockSpec((tf, D_MODEL), lambda i, f: (f, 0)),
                  pl.BlockSpec((F_CONV, tf), lambda i, f: (0, f)),
                  pl.BlockSpec((1, tf), lambda i, f: (0, f)),
                  pl.BlockSpec((P, tf), lambda i, f: (0, f)),
                  pl.BlockSpec((1, D_MODEL), lambda i, f: (0, 0))],
        out_specs=[pl.BlockSpec((tm, D_MODEL), lambda i, f: (i, 0)),
                   pl.BlockSpec((1, P, tf), lambda i, f: (i, 0, f))],
        out_shape=[jax.ShapeDtypeStruct((t, D_MODEL), F32),
                   jax.ShapeDtypeStruct((t // tm, P, D_FF), F32)],
        scratch_shapes=[pltpu.VMEM((tm, D_MODEL), BF16), pltpu.VMEM((nf, tm + 2 * P, tf), F32)],
        compiler_params=_cparams(("arbitrary", "arbitrary")),
        name="conv_ffn",
    )(x, g, wup, wup, wd, cw, cb, pre, gf)


def _pack_rwkv_state(s):
    n = s.shape[0]
    return s.reshape(n, N_PAIRS, 2, R_HD, R_HD).transpose(0, 1, 3, 2, 4).reshape(n, N_PAIRS, R_HD, LANES)


def _unpack_rwkv_state(s):
    n = s.shape[0]
    return s.reshape(n, N_PAIRS, R_HD, 2, R_HD).transpose(0, 1, 3, 2, 4).reshape(n, R_HEADS, R_HD, R_HD)


def _unpack_rwkv_blockdiag(s):
    n = s.shape[0]
    return jnp.stack([s[:, :, :R_HD, :R_HD], s[:, :, R_HD:, R_HD:]], axis=2).reshape(n, R_HEADS, R_HD, R_HD)


def _prep_layer_params(l, p):
    w_in = p['w_in'][l]
    c_dt_src = 3 * A_W + S_W + S_CONV_CH
    c_rw_src = c_dt_src + S_HEADS
    c_gate_src = c_rw_src + R_IN_W
    w_proj = jnp.concatenate([
        w_in[:, c_rw_src:c_gate_src],
        w_in[:, c_dt_src:c_rw_src], jnp.zeros((D_MODEL, C_Q - C_DT - S_HEADS), F32),
        w_in[:, :c_dt_src],
        w_in[:, c_gate_src:]], axis=1).astype(BF16)
    pad_lane = lambda v: jnp.pad(v, (0, LANES - v.shape[0])).reshape(1, LANES)
    zl = lambda r0, w: jnp.zeros((R_LORA, R_W), F32).at[r0:r0 + w.shape[0]].set(w).astype(BF16)
    return dict(
        norm1_g=p['norm1_g'][l].reshape(1, D_MODEL), w_proj=w_proj,
        b_gate=p['b_gate'][l].reshape(1, 3 * D_MODEL),
        w_pa=p['w_pa'][l].astype(BF16), w_pb=p['w_pb'][l].astype(BF16), w_pc=p['w_pc'][l].astype(BF16),
        w_o=p['w_o'][l].astype(BF16),
        ssm_conv_w=p['ssm_conv_w'][l], ssm_conv_b=p['ssm_conv_b'][l].reshape(1, S_CONV_CH),
        ssm_dt_bias=pad_lane(p['ssm_dt_bias'][l]), ssm_a_log=pad_lane(p['ssm_a_log'][l]),
        ssm_dvec=jnp.repeat(p['ssm_d'][l], S_HD).reshape(1, S_W),
        ssm_norm_g=p['ssm_norm_g'][l].reshape(1, S_W),
        rw_mu=p['rw_mu'][l].reshape(1, R_IN_W), rw_w0=p['rw_w0'][l].reshape(1, R_W),
        rw_w2p=zl(0, p['rw_w2'][l]), rw_a0=p['rw_a0'][l].reshape(1, R_W),
        rw_a2p=zl(R_LORA_W, p['rw_a2'][l]), rw_g2p=zl(R_LORA_W + R_LORA_A, p['rw_g2'][l]),
        rw_kk=p['rw_kk'][l].reshape(1, R_W), rw_ka=p['rw_ka'][l].reshape(1, R_W),
        rw_rk=p['rw_rk'][l].reshape(1, R_W), rw_ln_g=p['rw_ln_g'][l].reshape(1, R_W),
        rw_ln_b=p['rw_ln_b'][l].reshape(1, R_W),
        norm2_g=p['norm2_g'][l].reshape(1, D_MODEL), w_up=p['w_up'][l].astype(BF16),
        w_down=p['w_down'][l].astype(BF16), ffn_conv_w=p['ffn_conv_w'][l],
        ffn_conv_b=p['ffn_conv_b'][l].reshape(1, D_FF))


def _row_tile(t):
    for tm in (512, 256, 128):
        if t % tm == 0:
            return tm
    raise ValueError(t)


def _ffn_tf():
    return D_FF // 2


def _prompt_layer(x, lp, bsz, seq, slopes_pair, gf, final_norm):
    t = bsz * seq
    tm = _row_tile(seq)
    tiles = seq // tm
    proj = _rms_matmul(x, lp['norm1_g'], lp['w_proj'], 1024 if t % 1024 == 0 else tm, 1024)
    proj3 = proj.reshape(bsz, seq, N_PROJ)
    oa = _moba_prompt(proj3, slopes_pair).reshape(t, A_W)
    ob, ssm_new = _ssd_prompt(proj, bsz, seq, lp['ssm_conv_w'], lp['ssm_conv_b'], lp['ssm_dt_bias'],
                              lp['ssm_a_log'], lp['ssm_dvec'], lp['ssm_norm_g'])
    r, w, k2, v, kk, kka, g = _rwkv_prep(
        proj, jnp.zeros((SUBLANES, R_IN_W), F32), lp['rw_mu'], lp['rw_w0'], lp['rw_w2p'], lp['rw_a0'],
        lp['rw_a2p'], lp['rw_g2p'], lp['rw_kk'], lp['rw_ka'], tm, 1, tiles)
    yr, rw_new = _rwkv_scan_prompt((r, w, k2, v, kk, kka), bsz, seq, LANES)
    x = _merge(x, proj, oa, ob, yr, r, k2, v, g, lp['b_gate'], lp['rw_ln_g'], lp['rw_ln_b'], lp['rw_rk'],
               lp['w_pa'], lp['w_pb'], lp['w_pc'], lp['w_o'], min(tm, 256))
    x, tail = _ffn(x, lp['norm2_g'], lp['w_up'], lp['w_down'], lp['ffn_conv_w'], lp['ffn_conv_b'],
                   jnp.zeros((SUBLANES, D_FF), F32), gf, tm, _ffn_tf(), 1, tiles, final_norm)
    k_new = proj3[:, :, C_K:C_K + A_W].reshape(bsz, seq, A_HEADS, A_HD)
    v_new = proj3[:, :, C_V:C_V + A_W].reshape(bsz, seq, A_HEADS, A_HD)
    ssm_conv_new = proj3[:, seq - (S_CONV - 1):, C_XBC:C_XBC + S_CONV_CH]
    shift_new = proj3[:, seq - 1:, C_RW:C_RW + R_IN_W]
    ffn_conv_new = tail.reshape(bsz, tiles, SUBLANES, D_FF)[:, tiles - 1, SUBLANES - (F_CONV - 1):]
    state = (k_new, v_new, ssm_new.reshape(bsz, S_HEADS, S_HD, S_STATE), ssm_conv_new,
             _unpack_rwkv_blockdiag(rw_new), shift_new, ffn_conv_new)
    return x, state


def _sample_layer(x, lp, dbsz, tnew, st, cache_k4, cache_v4, page_table, layer, gf, final_norm):
    t = tnew * dbsz
    ssm0, ssm_conv0, rwkv0, shift0, ffn_conv0 = st
    proj = _rms_matmul(x, lp['norm1_g'], lp['w_proj'], t, 1024)
    proj3 = proj.reshape(tnew, dbsz, N_PROJ)

    qkv = proj3[:, :, C_Q:C_Q + 3 * A_W].transpose(1, 0, 2)
    qkv8 = jnp.pad(qkv, ((0, 0), (0, SUBLANES - tnew), (0, 0)))
    oa8 = _moba_sample(qkv8[:, :, :A_W], qkv8[:, :, A_W:2 * A_W], qkv8[:, :, 2 * A_W:],
                       cache_k4, cache_v4, page_table, layer, tnew)
    oa = oa8[:, :tnew].transpose(1, 0, 2).reshape(t, A_W)

    pre4 = ssm_conv0.transpose(1, 0, 2).reshape(S_CONV - 1, dbsz, 1, S_CONV_CH)
    h0 = ssm0.reshape(-1, dbsz, N_PAIRS, 2 * S_HD, S_STATE)
    cols4 = lambda c0, w: proj3[:, :, c0:c0 + w].reshape(tnew, dbsz, 1, w)
    ob4, ssm_new = _ssd_sample(cols4(C_Z, S_W), cols4(C_XBC, S_CONV_CH), cols4(C_DT, LANES), pre4, h0, layer,
                               lp['ssm_conv_w'], lp['ssm_conv_b'], lp['ssm_dt_bias'], lp['ssm_a_log'],
                               lp['ssm_dvec'], lp['ssm_norm_g'])
    ob = ob4.reshape(t, S_W)
    ssm_new = ssm_new.reshape(dbsz, S_HEADS, S_HD, S_STATE)

    r, w, k2, v, kk, kka, g = _rwkv_prep(
        proj, shift0.reshape(dbsz, R_IN_W), lp['rw_mu'], lp['rw_w0'], lp['rw_w2p'], lp['rw_a0'],
        lp['rw_a2p'], lp['rw_g2p'], lp['rw_kk'], lp['rw_ka'], t, dbsz, 1)
    to4 = lambda a: a.reshape(tnew, dbsz, 1, R_W)
    yr4, rw_new = _rwkv_scan_sample(tuple(to4(a) for a in (r, w, k2, v, kk, kka)), _pack_rwkv_state(rwkv0))
    yr = yr4.reshape(t, R_W)

    x = _merge(x, proj, oa, ob, yr, r, k2, v, g, lp['b_gate'], lp['rw_ln_g'], lp['rw_ln_b'], lp['rw_rk'],
               lp['w_pa'], lp['w_pb'], lp['w_pc'], lp['w_o'], min(t, 256))
    pre = ffn_conv0.transpose(1, 0, 2).reshape((F_CONV - 1) * dbsz, D_FF)
    x, tail = _ffn(x, lp['norm2_g'], lp['w_up'], lp['w_down'], lp['ffn_conv_w'], lp['ffn_conv_b'],
                   pre, gf, t, _ffn_tf(), dbsz, 1, final_norm)

    k_new = qkv[:, :, A_W:2 * A_W].reshape(dbsz, tnew, A_HEADS, A_HD)
    v_new = qkv[:, :, 2 * A_W:].reshape(dbsz, tnew, A_HEADS, A_HD)
    ssm_conv_new = proj3[tnew - (S_CONV - 1):, :, C_XBC:C_XBC + S_CONV_CH].transpose(1, 0, 2)
    shift_new = proj3[tnew - 1:, :, C_RW:C_RW + R_IN_W].transpose(1, 0, 2)
    ffn_conv_new = tail.reshape(F_CONV - 1, dbsz, D_FF).transpose(1, 0, 2)
    state = (k_new, v_new, ssm_new, ssm_conv_new, _unpack_rwkv_state(rw_new), shift_new, ffn_conv_new)
    return x, state


def kernel(x_prompt, x_sample, cache_k, cache_v, state_ssm, state_ssm_conv, state_rwkv, state_rwkv_shift, state_ffn_conv, page_table, norm1_g, w_in, b_gate, w_pa, ssm_conv_w, ssm_conv_b, ssm_dt_bias, ssm_a_log, ssm_d, ssm_norm_g, w_pb, rw_mu, rw_w0, rw_w2, rw_a0, rw_a2, rw_g2, rw_kk, rw_ka, rw_rk, rw_ln_g, rw_ln_b, w_pc, w_o, norm2_g, w_up, ffn_conv_w, ffn_conv_b, w_down, norm_f_g):
    params = dict(norm1_g=norm1_g, w_in=w_in, b_gate=b_gate, w_pa=w_pa, ssm_conv_w=ssm_conv_w,
                  ssm_conv_b=ssm_conv_b, ssm_dt_bias=ssm_dt_bias, ssm_a_log=ssm_a_log, ssm_d=ssm_d,
                  ssm_norm_g=ssm_norm_g, w_pb=w_pb, rw_mu=rw_mu, rw_w0=rw_w0, rw_w2=rw_w2, rw_a0=rw_a0,
                  rw_a2=rw_a2, rw_g2=rw_g2, rw_kk=rw_kk, rw_ka=rw_ka, rw_rk=rw_rk, rw_ln_g=rw_ln_g,
                  rw_ln_b=rw_ln_b, w_pc=w_pc, w_o=w_o, norm2_g=norm2_g, w_up=w_up, ffn_conv_w=ffn_conv_w,
                  ffn_conv_b=ffn_conv_b, w_down=w_down)
    depth = w_in.shape[0]
    bsz, seq, _ = x_prompt.shape
    dbsz, tnew, _ = x_sample.shape
    head = jnp.arange(A_HEADS, dtype=F32) + 1.0
    slopes = jnp.exp2(-8.0 * head / A_HEADS)
    slopes_pair = jnp.repeat(slopes, A_HD).reshape(N_PAIRS, 1, LANES)
    cache_k4 = cache_k.transpose(0, 1, 3, 4, 2)
    cache_v4 = cache_v.transpose(0, 1, 3, 4, 2)
    gf = norm_f_g.reshape(1, D_MODEL)

    hp = x_prompt.reshape(bsz * seq, D_MODEL)
    hs = x_sample.transpose(1, 0, 2).reshape(tnew * dbsz, D_MODEL)
    new_p = [[] for _ in range(7)]
    new_s = [[] for _ in range(7)]
    for l in range(depth):
        lp = _prep_layer_params(l, params)
        last = l == depth - 1
        hp, sp = _prompt_layer(hp, lp, bsz, seq, slopes_pair, gf, last)
        st = (state_ssm, state_ssm_conv[l], state_rwkv[l], state_rwkv_shift[l], state_ffn_conv[l])
        hs, ss = _sample_layer(hs, lp, dbsz, tnew, st, cache_k4, cache_v4, page_table, l, gf, last)
        for j in range(7):
            new_p[j].append(sp[j])
            new_s[j].append(ss[j])
    y_prompt = hp.reshape(bsz, seq, D_MODEL)
    y_sample = hs.reshape(tnew, dbsz, D_MODEL).transpose(1, 0, 2)
    outs = [y_prompt, y_sample]
    for j in range(7):
        outs += [jnp.stack(new_p[j]), jnp.stack(new_s[j])]
    return tuple(outs)
```

```python
import functools
import math

import jax
import jax.numpy as jnp
from jax import lax
from jax.experimental import pallas as pl
from jax.experimental.pallas import tpu as pltpu

F32 = jnp.float32
BF16 = jnp.bfloat16

D_MODEL = 1024
A_HEADS = 8
A_HD = 64
A_W = A_HEADS * A_HD
MOBA_BLOCK = 256
MOBA_TOPK = 3
Q_BLOCK = MOBA_BLOCK
ATTN_SCALE = A_HD ** -0.5
S_HEADS = 8
S_HD = 64
S_W = S_HEADS * S_HD
S_GROUPS = 2
S_STATE = 128
S_CONV = 4
S_CONV_CH = S_W + 2 * S_GROUPS * S_STATE
SSD_CHUNK = 128
R_HEADS = 8
R_HD = 64
R_W = R_HEADS * R_HD
R_LORA_W = 64
R_LORA_A = 64
R_LORA_G = 128
R_LORA = R_LORA_W + R_LORA_A + R_LORA_G
R_IN_W = 3 * R_W + R_LORA
D_FF = ((8 * D_MODEL // 3 + 127) // 128) * 128
F_CONV = 3
NORM_EPS = 1e-6
GN_EPS = 64e-5
NEG_INF = -1e30

LANES = 128
SUBLANES = 8
HALF = 64
HALF_SHIFT = 6
SUBLANE_SHIFT = 3
N_PAIRS = 4

C_RW = 0
C_DT = R_IN_W
C_Q = 2048
C_K = C_Q + A_W
C_V = C_K + A_W
C_Z = C_V + A_W
C_XBC = C_Z + S_W
C_GATE = C_XBC + S_CONV_CH
N_PROJ = C_GATE + 3 * D_MODEL

VMEM_LIMIT = 56 * 1024 * 1024


def _cparams(sem):
    return pltpu.CompilerParams(dimension_semantics=sem, vmem_limit_bytes=VMEM_LIMIT)


def _dot(a, b):
    return jnp.dot(a.astype(BF16), b.astype(BF16), preferred_element_type=F32)


def _dot_nt(a, b):
    return lax.dot_general(a.astype(BF16), b.astype(BF16), (((1,), (1,)), ((), ())),
                           preferred_element_type=F32)


def _dot_tn(a, b):
    return lax.dot_general(a.astype(BF16), b.astype(BF16), (((0,), (0,)), ((), ())),
                           preferred_element_type=F32)


def _split_dot(x, w, passes, x_is_lhs=True):
    w = w.astype(BF16)
    acc = None
    rem = x
    for _ in range(passes):
        piece = rem.astype(BF16)
        term = (jnp.dot(piece, w, preferred_element_type=F32) if x_is_lhs
                else jnp.dot(w, piece, preferred_element_type=F32))
        acc = term if acc is None else acc + term
        rem = rem - piece.astype(F32)
    return acc


def _dot_nt_hi(a, b):
    return lax.dot_general(a, b, (((1,), (1,)), ((), ())), preferred_element_type=F32,
                           precision=lax.Precision.HIGHEST)


def _sigmoid(x):
    return 1.0 / (1.0 + jnp.exp(-x))


def _silu(x):
    return x * _sigmoid(x)


def _softplus(x):
    return jnp.maximum(x, 0.0) + jnp.log(1.0 + jnp.exp(-jnp.abs(x)))


def _head_ones(width, scale):
    r = lax.shift_right_logical(lax.broadcasted_iota(jnp.int32, (width, width), 0), HALF_SHIFT)
    c = lax.shift_right_logical(lax.broadcasted_iota(jnp.int32, (width, width), 1), HALF_SHIFT)
    return jnp.where(r == c, scale, 0.0).astype(F32)


def _rms_matmul_kernel(x_ref, g_ref, w_ref, o_ref, xn_ref):
    @pl.when(pl.program_id(1) == 0)
    def _():
        x = x_ref[...]
        ms = jnp.mean(x * x, axis=-1, keepdims=True)
        xn_ref[...] = (x * lax.rsqrt(ms + NORM_EPS) * g_ref[...]).astype(BF16)

    o_ref[...] = jnp.dot(xn_ref[...], w_ref[...], preferred_element_type=F32)


def _rms_matmul(x, g, w, tm, tn):
    t, d = x.shape
    n = w.shape[1]
    return pl.pallas_call(
        _rms_matmul_kernel,
        grid=(t // tm, n // tn),
        in_specs=[pl.BlockSpec((tm, d), lambda i, j: (i, 0)),
                  pl.BlockSpec((1, d), lambda i, j: (0, 0)),
                  pl.BlockSpec((d, tn), lambda i, j: (0, j))],
        out_specs=pl.BlockSpec((tm, tn), lambda i, j: (i, j)),
        out_shape=jax.ShapeDtypeStruct((t, n), F32),
        scratch_shapes=[pltpu.VMEM((tm, d), BF16)],
        compiler_params=_cparams(("parallel", "arbitrary")),
        name="in_proj",
    )(x, g, w)


def _topk_bias(gate, n_valid, axis=1):
    pos = lax.broadcasted_iota(jnp.int32, gate.shape, axis)
    pos_f = pos.astype(F32)
    gm = jnp.where(pos < n_valid, gate, NEG_INF)
    selected = jnp.zeros(gate.shape, jnp.bool_)
    for j in range(MOBA_TOPK):
        m = jnp.max(gm, axis=axis, keepdims=True)
        idx = jnp.min(jnp.where(gm == m, pos_f, 1e9), axis=axis, keepdims=True)
        hit = pos_f == idx
        selected = jnp.logical_or(selected, jnp.logical_and(hit, j < n_valid))
        gm = jnp.where(hit, -jnp.inf, gm)
    return jnp.where(selected, 0.0, NEG_INF).astype(F32)


def _moba_prompt_kernel(q_ref, k_ref, v_ref, sl_ref, o_ref, kb_ref, vt_ref, kmean_ref, sel_ref, acc_ref,
                        raw_a, raw_b, m_ref, l_ref, *, nb):
    i = pl.program_id(2)
    B = MOBA_BLOCK

    @pl.when(i == 0)
    def _():
        kb_ref[...] = k_ref[0].astype(BF16)
        kmean_ref[...] = jnp.zeros_like(kmean_ref)
        for n in range(nb):
            kmean_ref[n:n + 1, :] = jnp.mean(k_ref[0, n * B:(n + 1) * B, :], axis=0, keepdims=True)
            vt_ref[n] = v_ref[0, n * B:(n + 1) * B, :].T.astype(BF16)

    own = i
    q = q_ref[0]
    lane = lax.broadcasted_iota(jnp.int32, (1, LANES), 1)
    halves = (lane < HALF, lane >= HALF)
    log2e = 1.0 / math.log(2.0)
    slope2 = (sl_ref[0, :, 0:1] * log2e, sl_ref[0, :, HALF:HALF + 1] * log2e)
    kmean = kmean_ref[...]
    qb = []
    for h in range(2):
        q_m = jnp.where(halves[h], q, 0.0)
        bias_t = _topk_bias(_dot_nt_hi(kmean, q_m), own, axis=0)
        for n in range(nb):
            sel_ref[h, n] = jnp.broadcast_to(bias_t[n:n + 1, :], (SUBLANES, Q_BLOCK))
        qb.append((q_m * (ATTN_SCALE * log2e)).astype(BF16))

    rel = (lax.broadcasted_iota(jnp.int32, (B, Q_BLOCK), 0) - lax.broadcasted_iota(jnp.int32, (B, Q_BLOCK), 1))
    alibi = [slope2[h] * rel.astype(F32) for h in range(2)]
    row_a = lax.broadcasted_iota(jnp.int32, (LANES, 1), 0) < HALF

    def put_scores(n, dst_ref):
        kn = kb_ref[pl.ds(pl.multiple_of(n * B, B), B), :]
        for h in range(2):
            dst_ref[h] = _dot_nt(kn, qb[h])

    def row_of(ref, h):
        return ref[h][0:1, :]

    def put_row(ref, h, x):
        ref[h] = jnp.broadcast_to(x, (SUBLANES, Q_BLOCK))

    def attend(n, src_ref):
        shift = ((n - i) * B).astype(F32)
        ps, alphas = [], []
        for h in range(2):
            row = sel_ref[h, n][0:1, :] + slope2[h] * shift
            s = src_ref[h] + alibi[h] + row
            m0 = row_of(m_ref, h)
            m1 = jnp.maximum(m0, jnp.max(s, axis=0, keepdims=True))
            alpha = jnp.exp2(m0 - m1)
            p = jnp.exp2(s - m1)
            put_row(m_ref, h, m1)
            put_row(l_ref, h, alpha * row_of(l_ref, h) + jnp.sum(p, axis=0, keepdims=True))
            ps.append(p.astype(BF16))
            alphas.append(alpha)
        vtn = vt_ref[n]
        pv = [jnp.dot(vtn, ps[h], preferred_element_type=F32) for h in range(2)]
        for h in range(2):
            acc_ref[h] = alphas[h] * acc_ref[h] + pv[h]

    put_scores(0, raw_a)

    k_own = kb_ref[pl.ds(pl.multiple_of(own * B, B), B), :]
    own_raw = [_dot_nt(k_own, qb[h]) for h in range(2)]
    own_p = []
    for h in range(2):
        s = jnp.where(rel <= 0, own_raw[h] + alibi[h], NEG_INF)
        m = jnp.max(s, axis=0, keepdims=True)
        p = jnp.exp2(s - m)
        put_row(m_ref, h, m)
        put_row(l_ref, h, jnp.sum(p, axis=0, keepdims=True))
        own_p.append(p.astype(BF16))
    vt_own = vt_ref[own]
    for h in range(2):
        acc_ref[h] = jnp.dot(vt_own, own_p[h], preferred_element_type=F32)

    def body(j, carry):
        n0 = 2 * j
        put_scores(n0 + 1, raw_b)
        attend(n0, raw_a)
        put_scores(jnp.minimum(n0 + 2, nb - 1), raw_a)
        attend(n0 + 1, raw_b)
        return carry

    lax.fori_loop(0, own // 2, body, 0)

    @pl.when(own % 2 == 1)
    def _():
        attend(own - 1, raw_a)

    out_t = jnp.where(row_a, acc_ref[0] / row_of(l_ref, 0), acc_ref[1] / row_of(l_ref, 1))
    o_ref[0] = out_t.T


def _moba_prompt(proj3, slopes_pair):
    bsz, seq, _ = proj3.shape
    assert seq % MOBA_BLOCK == 0 and seq // MOBA_BLOCK >= MOBA_TOPK
    nb = seq // MOBA_BLOCK
    nbp = -(-nb // SUBLANES) * SUBLANES
    nq = seq // Q_BLOCK
    qc, kc, vc = C_Q // LANES, C_K // LANES, C_V // LANES
    return pl.pallas_call(
        functools.partial(_moba_prompt_kernel, nb=nb),
        grid=(bsz, N_PAIRS, nq),
        in_specs=[pl.BlockSpec((1, Q_BLOCK, LANES), lambda b, p, i: (b, i, qc + p)),
                  pl.BlockSpec((1, seq, LANES), lambda b, p, i: (b, 0, kc + p)),
                  pl.BlockSpec((1, seq, LANES), lambda b, p, i: (b, 0, vc + p)),
                  pl.BlockSpec((1, 1, LANES), lambda b, p, i: (p, 0, 0))],
        out_specs=pl.BlockSpec((1, Q_BLOCK, LANES), lambda b, p, i: (b, i, p)),
        out_shape=jax.ShapeDtypeStruct((bsz, seq, A_W), F32),
        scratch_shapes=[pltpu.VMEM((seq, LANES), BF16), pltpu.VMEM((nb, LANES, MOBA_BLOCK), BF16),
                        pltpu.VMEM((nbp, LANES), F32), pltpu.VMEM((2, nb, SUBLANES, Q_BLOCK), F32),
                        pltpu.VMEM((2, LANES, Q_BLOCK), F32),
                        pltpu.VMEM((2, MOBA_BLOCK, Q_BLOCK), F32), pltpu.VMEM((2, MOBA_BLOCK, Q_BLOCK), F32),
                        pltpu.VMEM((2, SUBLANES, Q_BLOCK), F32), pltpu.VMEM((2, SUBLANES, Q_BLOCK), F32)],
        compiler_params=_cparams(("parallel", "parallel", "arbitrary")),
        name="moba_prompt",
    )(proj3, proj3, proj3, slopes_pair)


def _moba_sample_kernel(pt_ref, q_ref, kn_ref, vn_ref, *refs, n_blk, ppb, page, tnew):
    del pt_ref
    n_pages = n_blk * ppb
    k_refs, v_refs = refs[:n_pages], refs[n_pages:2 * n_pages]
    o_ref, m_ref, l_ref, g_ref, acc_ref = refs[2 * n_pages:]
    past = n_blk * MOBA_BLOCK
    rows = A_HEADS * SUBLANES
    lane_head = lax.shift_right_logical(lax.broadcasted_iota(jnp.int32, (SUBLANES, A_W), 1), HALF_SHIFT)
    q8 = q_ref[0] * ATTN_SCALE
    qbd = jnp.concatenate([jnp.where(lane_head == h, q8, 0.0) for h in range(A_HEADS)], axis=0)
    q_hi = qbd.astype(BF16)
    q_lo = (qbd - q_hi.astype(F32)).astype(BF16)
    r = lax.broadcasted_iota(jnp.int32, (rows, 1), 0)
    tok = jnp.bitwise_and(r, SUBLANES - 1).astype(F32)
    slope = jnp.exp2(-(8.0 / A_HEADS) * (lax.shift_right_logical(r, SUBLANE_SHIFT) + 1).astype(F32))
    key = lax.broadcasted_iota(jnp.int32, (1, MOBA_BLOCK), 1).astype(F32)

    def block_t(refs_, n):
        return jnp.concatenate([refs_[j][0, 0].reshape(A_W, page) for j in range(n * ppb, (n + 1) * ppb)],
                               axis=1).astype(BF16)

    raws, fixes = [], []
    for n in range(n_blk):
        kt = block_t(k_refs, n)
        raws.append(jnp.dot(q_hi, kt, preferred_element_type=F32))
        fixes.append(jnp.dot(q_lo, kt, preferred_element_type=F32))
    es = []
    for n in range(n_blk):
        g_ref[n] = jnp.broadcast_to(jnp.sum(raws[n] + fixes[n], axis=-1, keepdims=True) * (1.0 / MOBA_BLOCK),
                                    (rows, LANES))
        s = raws[n] - slope * ((past - n * MOBA_BLOCK + tok) - key)
        m = jnp.max(s, axis=-1, keepdims=True)
        e = jnp.exp(s - m)
        m_ref[n] = jnp.broadcast_to(m, (rows, LANES))
        l_ref[n] = jnp.broadcast_to(jnp.sum(e, axis=-1, keepdims=True), (rows, LANES))
        es.append(e.astype(BF16))
    for n in range(n_blk):
        acc_ref[n] = lax.dot_general(es[n], block_t(v_refs, n), (((1,), (1,)), ((), ())),
                                     preferred_element_type=F32)

    lane = lax.broadcasted_iota(jnp.int32, (rows, LANES), 1)
    gate = jnp.zeros((rows, LANES), F32)
    for j in range(n_blk):
        gate = jnp.where(lane == j, g_ref[j], gate)
    bias = _topk_bias(gate, n_blk)

    kn = kn_ref[0]
    vn = vn_ref[0]
    s_own = []
    for j in range(tnew):
        sj = jnp.sum(qbd * kn[j:j + 1, :], axis=-1, keepdims=True) - slope * (tok - j)
        s_own.append(jnp.where(tok >= j, sj, NEG_INF))
    mx = s_own[0]
    for j in range(1, tnew):
        mx = jnp.maximum(mx, s_own[j])
    mb = []
    for j in range(n_blk):
        mj = m_ref[j][:, 0:1] + bias[:, j:j + 1]
        mb.append(mj)
        mx = jnp.maximum(mx, mj)
    lsum = jnp.zeros((rows, 1), F32)
    acc = jnp.zeros((rows, A_W), F32)
    for j in range(tnew):
        w = jnp.exp(s_own[j] - mx)
        lsum = lsum + w
        acc = acc + w * vn[j:j + 1, :]
    for j in range(n_blk):
        w = jnp.exp(mb[j] - mx)
        lsum = lsum + w * l_ref[j][:, 0:1]
        acc = acc + w * acc_ref[j]
    out = acc / lsum
    o8 = jnp.zeros((SUBLANES, A_W), F32)
    for h in range(A_HEADS):
        o8 = o8 + jnp.where(lane_head == h, out[h * SUBLANES:(h + 1) * SUBLANES, :], 0.0)
    o_ref[0] = o8


def _moba_sample(q8, k8, v8, cache_kt, cache_vt, page_table, layer, tnew):
    dbsz = q8.shape[0]
    n_pages = page_table.shape[1]
    page = cache_kt.shape[4]
    assert MOBA_BLOCK % page == 0 and (n_pages * page) % MOBA_BLOCK == 0 and page % LANES == 0
    ppb = MOBA_BLOCK // page
    n_blk = n_pages // ppb
    assert n_blk >= MOBA_TOPK and tnew <= SUBLANES
    rows = A_HEADS * SUBLANES
    tok_spec = pl.BlockSpec((1, SUBLANES, A_W), lambda b, pt: (b, 0, 0))

    def page_spec(j):
        return pl.BlockSpec((1, 1, A_HEADS, A_HD, page), lambda b, pt: (layer, pt[b * n_pages + j], 0, 0, 0))

    pages = [page_spec(j) for j in range(n_pages)]
    grid_spec = pltpu.PrefetchScalarGridSpec(
        num_scalar_prefetch=1,
        grid=(dbsz,),
        in_specs=[tok_spec, tok_spec, tok_spec] + pages + pages,
        out_specs=tok_spec,
        scratch_shapes=[pltpu.VMEM((n_blk, rows, LANES), F32), pltpu.VMEM((n_blk, rows, LANES), F32),
                        pltpu.VMEM((n_blk, rows, LANES), F32), pltpu.VMEM((n_blk, rows, A_W), F32)],
    )
    return pl.pallas_call(
        functools.partial(_moba_sample_kernel, n_blk=n_blk, ppb=ppb, page=page, tnew=tnew),
        grid_spec=grid_spec,
        out_shape=jax.ShapeDtypeStruct((dbsz, SUBLANES, A_W), F32),
        compiler_params=_cparams(("parallel",)),
        name="moba_sample",
    )(page_table.reshape(-1), q8, k8, v8, *([cache_kt] * n_pages), *([cache_vt] * n_pages))


def _ssd_prompt_kernel(z_ref, xbc_ref, dt_ref, cw_ref, cb_ref, dtb_ref, alog_ref, dvec_ref, ng_ref,
                       o_ref, hout_ref, ext_ref, h_ref, *, L):
    c = pl.program_id(1)
    P = SUBLANES

    @pl.when(c == 0)
    def _():
        ext_ref[0:P, :] = jnp.zeros((P, S_CONV_CH), F32)
        h_ref[...] = jnp.zeros_like(h_ref)

    @pl.when(c > 0)
    def _():
        ext_ref[0:P, :] = ext_ref[L:L + P, :]

    ext_ref[P:P + L, :] = xbc_ref[...]
    acc = cb_ref[...] + cw_ref[S_CONV - 1:S_CONV, :] * ext_ref[P:P + L, :]
    for d in range(1, S_CONV):
        acc = acc + cw_ref[S_CONV - 1 - d:S_CONV - d, :] * ext_ref[P - d:P - d + L, :]
    xbc = _silu(acc)
    xs = xbc[:, :S_W]
    bm = xbc[:, S_W:S_W + S_GROUPS * S_STATE]
    cm = xbc[:, S_W + S_GROUPS * S_STATE:]

    lane = lax.broadcasted_iota(jnp.int32, (1, LANES), 1)
    is_a = lane < HALF
    dt = jnp.where(lane < S_HEADS, _softplus(dt_ref[...] + dtb_ref[...]), 0.0)
    da = dt * (-jnp.exp(alog_ref[...]))
    rr = lax.broadcasted_iota(jnp.int32, (L, L), 0)
    cc = lax.broadcasted_iota(jnp.int32, (L, L), 1)
    causal = rr >= cc
    cum = _split_dot(da, jnp.where(causal, 1.0, 0.0), 3, x_is_lhs=False)
    cum_t = cum.T
    dt_t = dt.T
    row_a = lax.broadcasted_iota(jnp.int32, (2 * S_HD, 1), 0) < S_HD

    ys = []
    cb_g = [None] * S_GROUPS
    for pr in range(N_PAIRS):
        g = (2 * pr * S_GROUPS) // S_HEADS
        bg = bm[:, g * S_STATE:(g + 1) * S_STATE]
        cg = cm[:, g * S_STATE:(g + 1) * S_STATE]
        if cb_g[g] is None:
            cb_g[g] = _dot_nt(cg, bg)
        xs_p = xs[:, pr * LANES:(pr + 1) * LANES]
        heads = (2 * pr, 2 * pr + 1)
        halves = (is_a, jnp.logical_not(is_a))
        yp = jnp.zeros((L, LANES), F32)
        cum_c = [cum[:, h:h + 1] for h in heads]
        for h, half, cc_h in zip(heads, halves, cum_c):
            seg = cc_h - cum_t[h:h + 1, :]
            dec = jnp.exp(jnp.where(causal, seg, -jnp.inf))
            wts = cb_g[g] * dec * dt_t[h:h + 1, :]
            yp = yp + _dot(wts, jnp.where(half, xs_p, 0.0))
        hp = h_ref[pr]
        yp = yp + _dot_nt(cg, hp) * jnp.where(is_a, jnp.exp(cum_c[0]), jnp.exp(cum_c[1]))
        last = [cum[L - 1:L, h:h + 1] for h in heads]
        te = jnp.where(is_a, jnp.exp(last[0] - cum_c[0]) * dt[:, heads[0]:heads[0] + 1],
                       jnp.exp(last[1] - cum_c[1]) * dt[:, heads[1]:heads[1] + 1])
        st = _dot_tn(xs_p * te, bg)
        h_ref[pr] = hp * jnp.where(row_a, jnp.exp(last[0]), jnp.exp(last[1])) + st
        ys.append(yp)
    y = jnp.concatenate(ys, axis=1) + dvec_ref[...] * xs
    yz = y * _silu(z_ref[...])
    gw = S_W // S_GROUPS
    for g in range(S_GROUPS):
        part = yz[:, g * gw:(g + 1) * gw]
        ms = jnp.mean(part * part, axis=-1, keepdims=True)
        o_ref[:, g * gw:(g + 1) * gw] = part * lax.rsqrt(ms + NORM_EPS) * ng_ref[:, g * gw:(g + 1) * gw]

    @pl.when(c == pl.num_programs(1) - 1)
    def _():
        hout_ref[0] = h_ref[...]


def _ssd_prompt(proj, bsz, seq, cw, cb, dtb, alog, dvec, ng):
    L = SSD_CHUNK
    nc = seq // L
    const = lambda shape: pl.BlockSpec(shape, lambda b, c: (0,) * len(shape))
    return pl.pallas_call(
        functools.partial(_ssd_prompt_kernel, L=L),
        grid=(bsz, nc),
        in_specs=[pl.BlockSpec((L, S_W), lambda b, c: (b * nc + c, C_Z // S_W)),
                  pl.BlockSpec((L, S_CONV_CH), lambda b, c: (b * nc + c, C_XBC // S_CONV_CH)),
                  pl.BlockSpec((L, LANES), lambda b, c: (b * nc + c, C_DT // LANES)),
                  const((S_CONV, S_CONV_CH)), const((1, S_CONV_CH)), const((1, LANES)), const((1, LANES)),
                  const((1, S_W)), const((1, S_W))],
        out_specs=[pl.BlockSpec((L, S_W), lambda b, c: (b * nc + c, 0)),
                   pl.BlockSpec((1, N_PAIRS, 2 * S_HD, S_STATE), lambda b, c: (b, 0, 0, 0))],
        out_shape=[jax.ShapeDtypeStruct((bsz * seq, S_W), F32),
                   jax.ShapeDtypeStruct((bsz, N_PAIRS, 2 * S_HD, S_STATE), F32)],
        scratch_shapes=[pltpu.VMEM((L + 2 * SUBLANES, S_CONV_CH), F32),
                        pltpu.VMEM((N_PAIRS, 2 * S_HD, S_STATE), F32)],
        compiler_params=_cparams(("parallel", "arbitrary")),
        name="ssd_prompt",
    )(proj, proj, proj, cw, cb, dtb, alog, dvec, ng)


def _ssd_sample_kernel(z_ref, xbc_ref, dt_ref, pre_ref, h0_ref, cw_ref, cb_ref, dtbv_ref, alogv_ref,
                       dvec_ref, ng_ref, o_ref, hout_ref, *, T, nb):
    rows2 = 2 * S_HD
    eye = (lax.broadcasted_iota(jnp.int32, (rows2, LANES), 0)
           == lax.broadcasted_iota(jnp.int32, (rows2, LANES), 1)).astype(F32)
    eye_all = jnp.concatenate([eye] * nb, axis=0)
    ones = jnp.ones((LANES, LANES), F32)
    spread = (lax.broadcasted_iota(jnp.int32, (LANES, S_W), 0)
              == lax.shift_right_logical(lax.broadcasted_iota(jnp.int32, (LANES, S_W), 1), HALF_SHIFT)).astype(F32)
    a_vec = -jnp.exp(alogv_ref[...])
    gw = S_W // S_GROUPS

    def as_columns(x, passes):
        return _split_dot(eye_all * _repeat_rows(x, rows2), ones, passes)

    up = [pre_ref[j] for j in range(S_CONV - 1)] + [xbc_ref[t] for t in range(T)]
    hs = [h0_ref[0, :, pr].reshape(nb * rows2, S_STATE) for pr in range(N_PAIRS)]
    for t in range(T):
        acc = cb_ref[...] + cw_ref[0:1, :] * up[t]
        for j in range(1, S_CONV):
            acc = acc + cw_ref[j:j + 1, :] * up[t + j]
        xc = _silu(acc)
        xs = xc[:, :S_W]
        dt = _softplus(_split_dot(dt_ref[t], spread, 3) + dtbv_ref[...])
        decay = jnp.exp(dt * a_vec)
        xdt = xs * dt
        ys = []
        for pr in range(N_PAIRS):
            g = (2 * pr * S_GROUPS) // S_HEADS
            sl = slice(pr * LANES, (pr + 1) * LANES)
            b_rows = _repeat_rows(xc[:, S_W + g * S_STATE:S_W + (g + 1) * S_STATE], rows2)
            c_rows = _repeat_rows(xc[:, S_W + (S_GROUPS + g) * S_STATE:S_W + (S_GROUPS + g + 1) * S_STATE], rows2)
            hs[pr] = hs[pr] * as_columns(decay[:, sl], 3) + as_columns(xdt[:, sl], 2) * b_rows
            y_col = _split_dot(hs[pr] * c_rows, ones, 2)
            ys.append(jnp.sum((eye_all * y_col).reshape(nb, rows2, LANES), axis=1))
        y = jnp.concatenate(ys, axis=1) + dvec_ref[...] * xs
        yz = y * _silu(z_ref[t])
        for g in range(S_GROUPS):
            part = yz[:, g * gw:(g + 1) * gw]
            ms = jnp.mean(part * part, axis=-1, keepdims=True)
            o_ref[t, :, g * gw:(g + 1) * gw] = part * lax.rsqrt(ms + NORM_EPS) * ng_ref[:, g * gw:(g + 1) * gw]
    for pr in range(N_PAIRS):
        hout_ref[:, pr] = hs[pr].reshape(nb, rows2, S_STATE)


def _ssd_sample(proj3, pre3, h0, layer, cw, cb, dtbv, alogv, dvec, ng, nb=8):
    T, dbsz = proj3.shape[:2]
    const = lambda shape: pl.BlockSpec(shape, lambda j: (0,) * len(shape))
    hspec = pl.BlockSpec((nb, N_PAIRS, 2 * S_HD, S_STATE), lambda j: (j, 0, 0, 0))
    h0spec = pl.BlockSpec((1, nb, N_PAIRS, 2 * S_HD, S_STATE), lambda j: (layer, j, 0, 0, 0))
    return pl.pallas_call(
        functools.partial(_ssd_sample_kernel, T=T, nb=nb),
        grid=(dbsz // nb,),
        in_specs=[pl.BlockSpec((T, nb, S_W), lambda j: (0, j, C_Z // S_W)),
                  pl.BlockSpec((T, nb, S_CONV_CH), lambda j: (0, j, C_XBC // S_CONV_CH)),
                  pl.BlockSpec((T, nb, LANES), lambda j: (0, j, C_DT // LANES)),
                  pl.BlockSpec((S_CONV - 1, nb, S_CONV_CH), lambda j: (0, j, 0)),
                  h0spec,
                  const((S_CONV, S_CONV_CH)), const((1, S_CONV_CH)), const((1, S_W)), const((1, S_W)),
                  const((1, S_W)), const((1, S_W))],
        out_specs=[pl.BlockSpec((T, nb, S_W), lambda j: (0, j, 0)), hspec],
        out_shape=[jax.ShapeDtypeStruct((T, dbsz, S_W), F32),
                   jax.ShapeDtypeStruct((dbsz, N_PAIRS, 2 * S_HD, S_STATE), F32)],
        compiler_params=_cparams(("parallel",)),
        name="ssd_sample",
    )(proj3, proj3, proj3, pre3, h0, cw, cb, dtbv, alogv, dvec, ng)


def _rwkv_prep_kernel(u_ref, pre_ref, mu_ref, w0_ref, w2_ref, a0_ref, a2_ref, g2_ref, kk_ref, ka_ref,
                      r_out, w_out, k_out, v_out, kk_out, kka_out, g_out, ext_ref,
                      *, tm, P, stride, tiles_per_seq):
    i = pl.program_id(0)
    first = (i % tiles_per_seq) == 0

    @pl.when(first)
    def _():
        ext_ref[0:P, :] = pre_ref[...]

    @pl.when(jnp.logical_not(first))
    def _():
        ext_ref[0:P, :] = ext_ref[tm:tm + P, :]

    u = u_ref[...]
    ext_ref[P:P + tm, :] = u
    prev = ext_ref[P - stride:P - stride + tm, :]
    x = u + (prev - u) * mu_ref[...]
    r = x[:, 0:R_W]
    kr = x[:, R_W:2 * R_W]
    vr = x[:, 2 * R_W:3 * R_W]
    xl = x[:, 3 * R_W:]
    w_log = w0_ref[...] + _dot(jnp.tanh(xl), w2_ref[...])
    log_decay = -jnp.exp(-_softplus(-w_log) - 0.5)
    a = _sigmoid(a0_ref[...] + _dot(xl, a2_ref[...]))
    g = _dot(_sigmoid(xl), g2_ref[...])
    kk = kr * kk_ref[...]
    ss = _split_dot(kk * kk, _head_ones(R_W, 1.0), 2)
    kk = kk / jnp.maximum(jnp.sqrt(ss), 1e-12)
    r_out[...] = r
    w_out[...] = log_decay
    k_out[...] = kr * (1.0 + (a - 1.0) * ka_ref[...])
    v_out[...] = vr
    kk_out[...] = kk
    kka_out[...] = kk * a
    g_out[...] = g


def _rwkv_prep(proj, pre, mu, w0, w2p, a0, a2p, g2p, kkw, kaw, tm, stride, tiles_per_seq):
    t = proj.shape[0]
    P = pre.shape[0]
    const = lambda shape: pl.BlockSpec(shape, lambda i: (0,) * len(shape))
    outs = pl.pallas_call(
        functools.partial(_rwkv_prep_kernel, tm=tm, P=P, stride=stride, tiles_per_seq=tiles_per_seq),
        grid=(t // tm,),
        in_specs=[pl.BlockSpec((tm, R_IN_W), lambda i: (i, 0)),
                  const((P, R_IN_W)), const((1, R_IN_W)), const((1, R_W)), const((R_LORA, R_W)),
                  const((1, R_W)), const((R_LORA, R_W)), const((R_LORA, R_W)), const((1, R_W)), const((1, R_W))],
        out_specs=[pl.BlockSpec((tm, R_W), lambda i: (i, 0))] * 7,
        out_shape=[jax.ShapeDtypeStruct((t, R_W), F32)] * 7,
        scratch_shapes=[pltpu.VMEM((tm + 2 * P, R_IN_W), F32)],
        compiler_params=_cparams(("arbitrary",)),
        name="rwkv_prep",
    )(proj, pre, mu, w0, w2p, a0, a2p, g2p, kkw, kaw)
    return outs


def _pair_consts():
    lane = lax.broadcasted_iota(jnp.int32, (R_HD, LANES), 1)
    row = lax.broadcasted_iota(jnp.int32, (R_HD, LANES), 0)
    is_a = lane < HALF
    eye2 = (jnp.bitwise_and(lane, HALF - 1) == row).astype(F32)
    return is_a, eye2


def _pair_sum(x, is_a):
    sa = jnp.sum(jnp.where(is_a, x, 0.0), axis=-1, keepdims=True)
    sb = jnp.sum(jnp.where(is_a, 0.0, x), axis=-1, keepdims=True)
    return jnp.where(is_a, sa, sb)


def _rwkv_step(s, rr, ww, kr, vv, kk, kka, is_a, eye2):
    v_col = _pair_sum(eye2 * vv, is_a)
    sk = _pair_sum(s * kk, is_a)
    s = s * ww - sk * kka + v_col * kr
    y_col = _pair_sum(s * rr, is_a)
    return s, jnp.sum(eye2 * y_col, axis=0, keepdims=True)


def _unit_lower_inverses(ns):
    L = ns[0].shape[0]
    eye = (lax.broadcasted_iota(jnp.int32, (L, L), 0) == lax.broadcasted_iota(jnp.int32, (L, L), 1)).astype(F32)
    ts = [eye + n for n in ns]
    pws = list(ns)
    for _ in range(int(math.log2(L)) - 1):
        pws = [_dot(p, p) for p in pws]
        ts = [t + _dot(t, p) for t, p in zip(ts, pws)]
    return ts


def _rwkv_chunk_prompt_kernel(r_ref, lw_ref, k_ref, v_ref, kk_ref, kka_ref, y_ref, sout_ref, s_ref, *, L):
    c = pl.program_id(1)

    @pl.when(c == 0)
    def _():
        s_ref[...] = jnp.zeros_like(s_ref)

    lane = lax.broadcasted_iota(jnp.int32, (1, LANES), 1)
    is_a = lane < HALF
    rr = lax.broadcasted_iota(jnp.int32, (L, L), 0)
    cc = lax.broadcasted_iota(jnp.int32, (L, L), 1)
    incl = rr >= cc
    strict = rr > cc
    tri = jnp.where(incl, 1.0, 0.0).astype(F32)
    r2 = lax.broadcasted_iota(jnp.int32, (LANES, LANES), 0) < HALF
    c2 = lax.broadcasted_iota(jnp.int32, (LANES, LANES), 1) < HALF
    same_head = r2 == c2

    r, lw, k, v, kk, kka = (ref[...] for ref in (r_ref, lw_ref, k_ref, v_ref, kk_ref, kka_ref))
    cum = _split_dot(lw, tri, 3, x_is_lhs=False)
    last = cum[L - 1:L, :]
    inv_p = jnp.exp(-cum)
    to_end = jnp.exp(last - cum)
    b_t = kk * jnp.exp(cum - lw)
    a_t = -kka * inv_p
    k_t = k * inv_p
    r_t = r * jnp.exp(cum)
    a_end = -kka * to_end
    k_end = k * to_end
    decay_end = jnp.exp(last)
    pairs = [slice(pr * LANES, (pr + 1) * LANES) for pr in range(N_PAIRS)]
    halves = (is_a, jnp.logical_not(is_a))
    s0 = [s_ref[pr] for pr in range(N_PAIRS)]

    ns, mks, ras, rks = [], [], [], []
    for sl in pairs:
        for half in halves:
            bm = jnp.where(half, b_t[:, sl], 0.0)
            rm = jnp.where(half, r_t[:, sl], 0.0)
            ns.append(jnp.where(strict, _dot_nt(bm, a_t[:, sl]), 0.0))
            mks.append(jnp.where(strict, _dot_nt(bm, k_t[:, sl]), 0.0))
            ras.append(jnp.where(incl, _dot_nt(rm, a_t[:, sl]), 0.0))
            rks.append(jnp.where(incl, _dot_nt(rm, k_t[:, sl]), 0.0))
    ts = _unit_lower_inverses(ns)
    rhs = [_dot_nt(b_t[:, sl], s0[pr]) + jnp.where(is_a, _dot(mks[2 * pr], v[:, sl]), _dot(mks[2 * pr + 1], v[:, sl]))
           for pr, sl in enumerate(pairs)]
    us = [jnp.where(is_a, _dot(ts[2 * pr], rhs[pr]), _dot(ts[2 * pr + 1], rhs[pr])) for pr in range(N_PAIRS)]
    for pr, sl in enumerate(pairs):
        ha, hb = 2 * pr, 2 * pr + 1
        y_ref[:, sl] = _dot_nt(r_t[:, sl], s0[pr]) + jnp.where(
            is_a, _dot(ras[ha], us[pr]) + _dot(rks[ha], v[:, sl]), _dot(ras[hb], us[pr]) + _dot(rks[hb], v[:, sl]))
    for pr, sl in enumerate(pairs):
        upd = _dot_tn(us[pr], a_end[:, sl]) + _dot_tn(v[:, sl], k_end[:, sl])
        s_ref[pr] = s0[pr] * decay_end[:, sl] + jnp.where(same_head, upd, 0.0)

    @pl.when(c == pl.num_programs(1) - 1)
    def _():
        sout_ref[0] = s_ref[...]


def _rwkv_scan_prompt(seqs, bsz, seq, L):
    nc = seq // L
    spec = pl.BlockSpec((L, R_W), lambda b, c: (b * nc + c, 0))
    return pl.pallas_call(
        functools.partial(_rwkv_chunk_prompt_kernel, L=L),
        grid=(bsz, nc),
        in_specs=[spec] * 6,
        out_specs=[spec, pl.BlockSpec((1, N_PAIRS, LANES, LANES), lambda b, c: (b, 0, 0, 0))],
        out_shape=[jax.ShapeDtypeStruct((bsz * seq, R_W), F32),
                   jax.ShapeDtypeStruct((bsz, N_PAIRS, LANES, LANES), F32)],
        scratch_shapes=[pltpu.VMEM((N_PAIRS, LANES, LANES), F32)],
        compiler_params=_cparams(("parallel", "arbitrary")),
        name="rwkv_scan_prompt",
    )(*seqs)


def _repeat_rows(x, reps):
    return jnp.concatenate([jnp.broadcast_to(x[b:b + 1, :], (reps, x.shape[1])) for b in range(x.shape[0])], axis=0)


def _rwkv_scan_sample_kernel(r_ref, w_ref, k_ref, v_ref, kk_ref, kka_ref, s0_ref, y_ref, sout_ref, *, T, nb):
    _, eye2 = _pair_consts()
    eye_all = jnp.concatenate([eye2] * nb, axis=0)
    ones_bd = _head_ones(LANES, 1.0)

    def head_sums(x):
        return _split_dot(x, ones_bd, 2)

    states = [s0_ref[:, pr].reshape(nb * R_HD, LANES) for pr in range(N_PAIRS)]
    for t in range(T):
        for pr in range(N_PAIRS):
            sl = slice(pr * LANES, (pr + 1) * LANES)
            rr, ww, kr, vv, kk, kka = (_repeat_rows(x, R_HD) for x in (
                r_ref[t, :, sl], jnp.exp(w_ref[t, :, sl]), k_ref[t, :, sl], v_ref[t, :, sl],
                kk_ref[t, :, sl], kka_ref[t, :, sl]))
            v_col = head_sums(eye_all * vv)
            s = states[pr]
            s = s * ww - head_sums(s * kk) * kka + v_col * kr
            states[pr] = s
            y_col = head_sums(s * rr)
            y_ref[t, :, sl] = jnp.sum((eye_all * y_col).reshape(nb, R_HD, LANES), axis=1)
    for pr in range(N_PAIRS):
        sout_ref[:, pr] = states[pr].reshape(nb, R_HD, LANES)


def _rwkv_scan_sample(seqs3, s0, nb=16):
    T, dbsz = seqs3[0].shape[:2]
    spec = pl.BlockSpec((T, nb, R_W), lambda j: (0, j, 0))
    sspec = pl.BlockSpec((nb, N_PAIRS, R_HD, LANES), lambda j: (j, 0, 0, 0))
    return pl.pallas_call(
        functools.partial(_rwkv_scan_sample_kernel, T=T, nb=nb),
        grid=(dbsz // nb,),
        in_specs=[spec] * 6 + [sspec],
        out_specs=[spec, sspec],
        out_shape=[jax.ShapeDtypeStruct((T, dbsz, R_W), F32),
                   jax.ShapeDtypeStruct((dbsz, N_PAIRS, R_HD, LANES), F32)],
        compiler_params=_cparams(("parallel",)),
        name="rwkv_scan_sample",
    )(*seqs3, s0)


def _merge_kernel(x_ref, oa_ref, ob_ref, yr_ref, r_ref, k_ref, v_ref, g_ref, ga_ref, gb_ref, gc_ref,
                  bg_ref, lng_ref, lnb_ref, rk_ref, wpa_ref, wpb_ref, wpc_ref, wo_ref, o_ref):
    mean_m = _head_ones(R_W, 1.0 / R_HD)
    yr = yr_ref[...]
    d = yr - _split_dot(yr, mean_m, 2)
    var = _split_dot(d * d, mean_m, 2)
    yn = d * lax.rsqrt(var + GN_EPS) * lng_ref[...] + lnb_ref[...]
    v = v_ref[...]
    bonus = _split_dot(r_ref[...] * k_ref[...] * rk_ref[...], _head_ones(R_W, 1.0), 2)
    oc = (yn + bonus * v) * g_ref[...]
    merged = (_sigmoid(ga_ref[...] + bg_ref[:, 0:D_MODEL]) * _dot(oa_ref[...], wpa_ref[...])
              + _sigmoid(gb_ref[...] + bg_ref[:, D_MODEL:2 * D_MODEL]) * _dot(ob_ref[...], wpb_ref[...])
              + _sigmoid(gc_ref[...] + bg_ref[:, 2 * D_MODEL:]) * _dot(oc, wpc_ref[...]))
    o_ref[...] = x_ref[...] + _dot(merged, wo_ref[...])


def _merge(x, proj, oa, ob, yr, r, k2, v, g, bg, lng, lnb, rk, wpa, wpb, wpc, wo, tm):
    t = x.shape[0]
    row = lambda w: pl.BlockSpec((tm, w), lambda i: (i, 0))
    const = lambda shape: pl.BlockSpec(shape, lambda i: (0,) * len(shape))
    gcol = C_GATE // D_MODEL
    gate = lambda j: pl.BlockSpec((tm, D_MODEL), lambda i: (i, gcol + j))
    return pl.pallas_call(
        _merge_kernel,
        grid=(t // tm,),
        in_specs=[row(D_MODEL), row(A_W), row(S_W), row(R_W), row(R_W), row(R_W), row(R_W), row(R_W),
                  gate(0), gate(1), gate(2),
                  const((1, 3 * D_MODEL)), const((1, R_W)), const((1, R_W)), const((1, R_W)),
                  const((A_W, D_MODEL)), const((S_W, D_MODEL)), const((R_W, D_MODEL)),
                  const((D_MODEL, D_MODEL))],
        out_specs=row(D_MODEL),
        out_shape=jax.ShapeDtypeStruct((t, D_MODEL), F32),
        compiler_params=_cparams(("parallel",)),
        name="merge",
    )(x, oa, ob, yr, r, k2, v, g, proj, proj, proj, bg, lng, lnb, rk, wpa, wpb, wpc, wo)


def _ffn_kernel(x_ref, g_ref, wug_ref, wuv_ref, wd_ref, cw_ref, cb_ref, pre_ref, gf_ref,
                o_ref, tail_ref, xn_ref, ext_ref, *, tm, P, stride, tiles_per_seq, final_norm):
    i = pl.program_id(0)
    f = pl.program_id(1)
    first = (i % tiles_per_seq) == 0

    @pl.when(f == 0)
    def _():
        x = x_ref[...]
        ms = jnp.mean(x * x, axis=-1, keepdims=True)
        xn_ref[...] = (x * lax.rsqrt(ms + NORM_EPS) * g_ref[...]).astype(BF16)

    @pl.when(first)
    def _():
        ext_ref[f, 0:P, :] = pre_ref[...]

    @pl.when(jnp.logical_not(first))
    def _():
        ext_ref[f, 0:P, :] = ext_ref[f, tm:tm + P, :]

    xn = xn_ref[...]
    ug = jnp.dot(xn, wug_ref[...], preferred_element_type=F32)
    ext_ref[f, P:P + tm, :] = ug
    tail_ref[0] = ext_ref[f, tm:tm + P, :]
    acc = cb_ref[...] + cw_ref[F_CONV - 1:F_CONV, :] * ug
    for d in range(1, F_CONV):
        acc = acc + cw_ref[F_CONV - 1 - d:F_CONV - d, :] * ext_ref[f, P - d * stride:P - d * stride + tm, :]
    uv = jnp.dot(xn, wuv_ref[...], preferred_element_type=F32)
    contrib = _dot(_silu(acc) * uv, wd_ref[...])

    @pl.when(f == 0)
    def _():
        o_ref[...] = x_ref[...] + contrib

    @pl.when(f > 0)
    def _():
        o_ref[...] = o_ref[...] + contrib

    if final_norm:
        @pl.when(f == pl.num_programs(1) - 1)
        def _():
            y = o_ref[...]
            ms = jnp.mean(y * y, axis=-1, keepdims=True)
            o_ref[...] = y * lax.rsqrt(ms + NORM_EPS) * gf_ref[...]


def _ffn(x, g, wup, wd, cw, cb, pre, gf, tm, tf, stride, tiles_per_seq, final_norm):
    t = x.shape[0]
    P = pre.shape[0]
    nf = D_FF // tf
    return pl.pallas_call(
        functools.partial(_ffn_kernel, tm=tm, P=P, stride=stride, tiles_per_seq=tiles_per_seq,
                          final_norm=final_norm),
        grid=(t // tm, nf),
        in_specs=[pl.BlockSpec((tm, D_MODEL), lambda i, f: (i, 0)),
                  pl.BlockSpec((1, D_MODEL), lambda i, f: (0, 0)),
                  pl.BlockSpec((D_MODEL, tf), lambda i, f: (0, f)),
                  pl.BlockSpec((D_MODEL, tf), lambda i, f: (0, nf + f)),
                  pl.BlockSpec((tf, D_MODEL), lambda i, f: (f, 0)),
                  pl.BlockSpec((F_CONV, tf), lambda i, f: (0, f)),
                  pl.BlockSpec((1, tf), lambda i, f: (0, f)),
                  pl.BlockSpec((P, tf), lambda i, f: (0, f)),
                  pl.BlockSpec((1, D_MODEL), lambda i, f: (0, 0))],
        out_specs=[pl.BlockSpec((tm, D_MODEL), lambda i, f: (i, 0)),
                   pl.BlockSpec((1, P, tf), lambda i, f: (i, 0, f))],
        out_shape=[jax.ShapeDtypeStruct((t, D_MODEL), F32),
                   jax.ShapeDtypeStruct((t // tm, P, D_FF), F32)],
        scratch_shapes=[pltpu.VMEM((tm, D_MODEL), BF16), pltpu.VMEM((nf, tm + 2 * P, tf), F32)],
        compiler_params=_cparams(("arbitrary", "arbitrary")),
        name="conv_ffn",
    )(x, g, wup, wup, wd, cw, cb, pre, gf)


def _pack_rwkv_state(s):
    n = s.shape[0]
    return s.reshape(n, N_PAIRS, 2, R_HD, R_HD).transpose(0, 1, 3, 2, 4).reshape(n, N_PAIRS, R_HD, LANES)


def _unpack_rwkv_state(s):
    n = s.shape[0]
    return s.reshape(n, N_PAIRS, R_HD, 2, R_HD).transpose(0, 1, 3, 2, 4).reshape(n, R_HEADS, R_HD, R_HD)


def _unpack_rwkv_blockdiag(s):
    n = s.shape[0]
    return jnp.stack([s[:, :, :R_HD, :R_HD], s[:, :, R_HD:, R_HD:]], axis=2).reshape(n, R_HEADS, R_HD, R_HD)


def _prep_layer_params(l, p):
    w_in = p['w_in'][l]
    c_dt_src = 3 * A_W + S_W + S_CONV_CH
    c_rw_src = c_dt_src + S_HEADS
    c_gate_src = c_rw_src + R_IN_W
    w_proj = jnp.concatenate([
        w_in[:, c_rw_src:c_gate_src],
        w_in[:, c_dt_src:c_rw_src], jnp.zeros((D_MODEL, C_Q - C_DT - S_HEADS), F32),
        w_in[:, :c_dt_src],
        w_in[:, c_gate_src:]], axis=1).astype(BF16)
    pad_lane = lambda v: jnp.pad(v, (0, LANES - v.shape[0])).reshape(1, LANES)
    zl = lambda r0, w: jnp.zeros((R_LORA, R_W), F32).at[r0:r0 + w.shape[0]].set(w).astype(BF16)
    return dict(
        norm1_g=p['norm1_g'][l].reshape(1, D_MODEL), w_proj=w_proj,
        b_gate=p['b_gate'][l].reshape(1, 3 * D_MODEL),
        w_pa=p['w_pa'][l].astype(BF16), w_pb=p['w_pb'][l].astype(BF16), w_pc=p['w_pc'][l].astype(BF16),
        w_o=p['w_o'][l].astype(BF16),
        ssm_conv_w=p['ssm_conv_w'][l], ssm_conv_b=p['ssm_conv_b'][l].reshape(1, S_CONV_CH),
        ssm_dt_bias=pad_lane(p['ssm_dt_bias'][l]), ssm_a_log=pad_lane(p['ssm_a_log'][l]),
        ssm_dvec=jnp.repeat(p['ssm_d'][l], S_HD).reshape(1, S_W),
        ssm_dtb_vec=jnp.repeat(p['ssm_dt_bias'][l], S_HD).reshape(1, S_W),
        ssm_alog_vec=jnp.repeat(p['ssm_a_log'][l], S_HD).reshape(1, S_W),
        ssm_norm_g=p['ssm_norm_g'][l].reshape(1, S_W),
        rw_mu=p['rw_mu'][l].reshape(1, R_IN_W), rw_w0=p['rw_w0'][l].reshape(1, R_W),
        rw_w2p=zl(0, p['rw_w2'][l]), rw_a0=p['rw_a0'][l].reshape(1, R_W),
        rw_a2p=zl(R_LORA_W, p['rw_a2'][l]), rw_g2p=zl(R_LORA_W + R_LORA_A, p['rw_g2'][l]),
        rw_kk=p['rw_kk'][l].reshape(1, R_W), rw_ka=p['rw_ka'][l].reshape(1, R_W),
        rw_rk=p['rw_rk'][l].reshape(1, R_W), rw_ln_g=p['rw_ln_g'][l].reshape(1, R_W),
        rw_ln_b=p['rw_ln_b'][l].reshape(1, R_W),
        norm2_g=p['norm2_g'][l].reshape(1, D_MODEL), w_up=p['w_up'][l].astype(BF16),
        w_down=p['w_down'][l].astype(BF16), ffn_conv_w=p['ffn_conv_w'][l],
        ffn_conv_b=p['ffn_conv_b'][l].reshape(1, D_FF))


def _row_tile(t):
    for tm in (512, 256, 128):
        if t % tm == 0:
            return tm
    raise ValueError(t)


def _ffn_tf():
    return D_FF // 2


def _prompt_layer(x, lp, bsz, seq, slopes_pair, gf, final_norm):
    t = bsz * seq
    tm = _row_tile(seq)
    tiles = seq // tm
    proj = _rms_matmul(x, lp['norm1_g'], lp['w_proj'], 1024 if t % 1024 == 0 else tm, 1024)
    proj3 = proj.reshape(bsz, seq, N_PROJ)
    oa = _moba_prompt(proj3, slopes_pair).reshape(t, A_W)
    ob, ssm_new = _ssd_prompt(proj, bsz, seq, lp['ssm_conv_w'], lp['ssm_conv_b'], lp['ssm_dt_bias'],
                              lp['ssm_a_log'], lp['ssm_dvec'], lp['ssm_norm_g'])
    r, w, k2, v, kk, kka, g = _rwkv_prep(
        proj, jnp.zeros((SUBLANES, R_IN_W), F32), lp['rw_mu'], lp['rw_w0'], lp['rw_w2p'], lp['rw_a0'],
        lp['rw_a2p'], lp['rw_g2p'], lp['rw_kk'], lp['rw_ka'], tm, 1, tiles)
    yr, rw_new = _rwkv_scan_prompt((r, w, k2, v, kk, kka), bsz, seq, LANES)
    x = _merge(x, proj, oa, ob, yr, r, k2, v, g, lp['b_gate'], lp['rw_ln_g'], lp['rw_ln_b'], lp['rw_rk'],
               lp['w_pa'], lp['w_pb'], lp['w_pc'], lp['w_o'], min(tm, 256))
    x, tail = _ffn(x, lp['norm2_g'], lp['w_up'], lp['w_down'], lp['ffn_conv_w'], lp['ffn_conv_b'],
                   jnp.zeros((SUBLANES, D_FF), F32), gf, tm, _ffn_tf(), 1, tiles, final_norm)
    k_new = proj3[:, :, C_K:C_K + A_W].reshape(bsz, seq, A_HEADS, A_HD)
    v_new = proj3[:, :, C_V:C_V + A_W].reshape(bsz, seq, A_HEADS, A_HD)
    ssm_conv_new = proj3[:, seq - (S_CONV - 1):, C_XBC:C_XBC + S_CONV_CH]
    shift_new = proj3[:, seq - 1:, C_RW:C_RW + R_IN_W]
    ffn_conv_new = tail.reshape(bsz, tiles, SUBLANES, D_FF)[:, tiles - 1, SUBLANES - (F_CONV - 1):]
    state = (k_new, v_new, ssm_new.reshape(bsz, S_HEADS, S_HD, S_STATE), ssm_conv_new,
             _unpack_rwkv_blockdiag(rw_new), shift_new, ffn_conv_new)
    return x, state


def _sample_layer(x, lp, dbsz, tnew, st, cache_k4, cache_v4, page_table, layer, gf, final_norm):
    t = tnew * dbsz
    ssm0, ssm_conv0, rwkv0, shift0, ffn_conv0 = st
    proj = _rms_matmul(x, lp['norm1_g'], lp['w_proj'], t, 1024)
    proj3 = proj.reshape(tnew, dbsz, N_PROJ)

    qkv = proj3[:, :, C_Q:C_Q + 3 * A_W].transpose(1, 0, 2)
    qkv8 = jnp.pad(qkv, ((0, 0), (0, SUBLANES - tnew), (0, 0)))
    oa8 = _moba_sample(qkv8[:, :, :A_W], qkv8[:, :, A_W:2 * A_W], qkv8[:, :, 2 * A_W:],
                       cache_k4, cache_v4, page_table, layer, tnew)
    oa = oa8[:, :tnew].transpose(1, 0, 2).reshape(t, A_W)

    h0 = ssm0.reshape(-1, dbsz, N_PAIRS, 2 * S_HD, S_STATE)
    ob3, ssm_new = _ssd_sample(proj3, ssm_conv0.transpose(1, 0, 2), h0, layer, lp['ssm_conv_w'],
                               lp['ssm_conv_b'], lp['ssm_dtb_vec'], lp['ssm_alog_vec'], lp['ssm_dvec'],
                               lp['ssm_norm_g'])
    ob = ob3.reshape(t, S_W)
    ssm_new = ssm_new.reshape(dbsz, S_HEADS, S_HD, S_STATE)

    r, w, k2, v, kk, kka, g = _rwkv_prep(
        proj, shift0.reshape(dbsz, R_IN_W), lp['rw_mu'], lp['rw_w0'], lp['rw_w2p'], lp['rw_a0'],
        lp['rw_a2p'], lp['rw_g2p'], lp['rw_kk'], lp['rw_ka'], t, dbsz, 1)
    to3 = lambda a: a.reshape(tnew, dbsz, R_W)
    yr3, rw_new = _rwkv_scan_sample(tuple(to3(a) for a in (r, w, k2, v, kk, kka)), _pack_rwkv_state(rwkv0))
    yr = yr3.reshape(t, R_W)

    x = _merge(x, proj, oa, ob, yr, r, k2, v, g, lp['b_gate'], lp['rw_ln_g'], lp['rw_ln_b'], lp['rw_rk'],
               lp['w_pa'], lp['w_pb'], lp['w_pc'], lp['w_o'], min(t, 256))
    pre = ffn_conv0.transpose(1, 0, 2).reshape((F_CONV - 1) * dbsz, D_FF)
    x, tail = _ffn(x, lp['norm2_g'], lp['w_up'], lp['w_down'], lp['ffn_conv_w'], lp['ffn_conv_b'],
                   pre, gf, t, _ffn_tf(), dbsz, 1, final_norm)

    k_new = qkv[:, :, A_W:2 * A_W].reshape(dbsz, tnew, A_HEADS, A_HD)
    v_new = qkv[:, :, 2 * A_W:].reshape(dbsz, tnew, A_HEADS, A_HD)
    ssm_conv_new = proj3[tnew - (S_CONV - 1):, :, C_XBC:C_XBC + S_CONV_CH].transpose(1, 0, 2)
    shift_new = proj3[tnew - 1:, :, C_RW:C_RW + R_IN_W].transpose(1, 0, 2)
    ffn_conv_new = tail.reshape(F_CONV - 1, dbsz, D_FF).transpose(1, 0, 2)
    state = (k_new, v_new, ssm_new, ssm_conv_new, _unpack_rwkv_state(rw_new), shift_new, ffn_conv_new)
    return x, state


def kernel(x_prompt, x_sample, cache_k, cache_v, state_ssm, state_ssm_conv, state_rwkv, state_rwkv_shift, state_ffn_conv, page_table, norm1_g, w_in, b_gate, w_pa, ssm_conv_w, ssm_conv_b, ssm_dt_bias, ssm_a_log, ssm_d, ssm_norm_g, w_pb, rw_mu, rw_w0, rw_w2, rw_a0, rw_a2, rw_g2, rw_kk, rw_ka, rw_rk, rw_ln_g, rw_ln_b, w_pc, w_o, norm2_g, w_up, ffn_conv_w, ffn_conv_b, w_down, norm_f_g):
    params = dict(norm1_g=norm1_g, w_in=w_in, b_gate=b_gate, w_pa=w_pa, ssm_conv_w=ssm_conv_w,
                  ssm_conv_b=ssm_conv_b, ssm_dt_bias=ssm_dt_bias, ssm_a_log=ssm_a_log, ssm_d=ssm_d,
                  ssm_norm_g=ssm_norm_g, w_pb=w_pb, rw_mu=rw_mu, rw_w0=rw_w0, rw_w2=rw_w2, rw_a0=rw_a0,
                  rw_a2=rw_a2, rw_g2=rw_g2, rw_kk=rw_kk, rw_ka=rw_ka, rw_rk=rw_rk, rw_ln_g=rw_ln_g,
                  rw_ln_b=rw_ln_b, w_pc=w_pc, w_o=w_o, norm2_g=norm2_g, w_up=w_up, ffn_conv_w=ffn_conv_w,
                  ffn_conv_b=ffn_conv_b, w_down=w_down)
    depth = w_in.shape[0]
    bsz, seq, _ = x_prompt.shape
    dbsz, tnew, _ = x_sample.shape
    head = jnp.arange(A_HEADS, dtype=F32) + 1.0
    slopes = jnp.exp2(-8.0 * head / A_HEADS)
    slopes_pair = jnp.repeat(slopes, A_HD).reshape(N_PAIRS, 1, LANES)
    cache_k4 = cache_k.transpose(0, 1, 3, 4, 2)
    cache_v4 = cache_v.transpose(0, 1, 3, 4, 2)
    gf = norm_f_g.reshape(1, D_MODEL)

    hp = x_prompt.reshape(bsz * seq, D_MODEL)
    hs = x_sample.transpose(1, 0, 2).reshape(tnew * dbsz, D_MODEL)
    new_p = [[] for _ in range(7)]
    new_s = [[] for _ in range(7)]
    for l in range(depth):
        lp = _prep_layer_params(l, params)
        last = l == depth - 1
        hp, sp = _prompt_layer(hp, lp, bsz, seq, slopes_pair, gf, last)
        st = (state_ssm, state_ssm_conv[l], state_rwkv[l], state_rwkv_shift[l], state_ffn_conv[l])
        hs, ss = _sample_layer(hs, lp, dbsz, tnew, st, cache_k4, cache_v4, page_table, l, gf, last)
        for j in range(7):
            new_p[j].append(sp[j])
            new_s[j].append(ss[j])
    y_prompt = hp.reshape(bsz, seq, D_MODEL)
    y_sample = hs.reshape(tnew, dbsz, D_MODEL).transpose(1, 0, 2)
    outs = [y_prompt, y_sample]
    for j in range(7):
        outs += [jnp.stack(new_p[j]), jnp.stack(new_s[j])]
    return tuple(outs)
```

```python
import functools
import math

import jax
import jax.numpy as jnp
from jax import lax
from jax.experimental import pallas as pl
from jax.experimental.pallas import tpu as pltpu

F32 = jnp.float32
BF16 = jnp.bfloat16

D_MODEL = 1024
A_HEADS = 8
A_HD = 64
A_W = A_HEADS * A_HD
MOBA_BLOCK = 256
MOBA_TOPK = 3
Q_BLOCK = MOBA_BLOCK
ATTN_SCALE = A_HD ** -0.5
S_HEADS = 8
S_HD = 64
S_W = S_HEADS * S_HD
S_GROUPS = 2
S_STATE = 128
S_CONV = 4
S_CONV_CH = S_W + 2 * S_GROUPS * S_STATE
SSD_CHUNK = 128
R_HEADS = 8
R_HD = 64
R_W = R_HEADS * R_HD
R_LORA_W = 64
R_LORA_A = 64
R_LORA_G = 128
R_LORA = R_LORA_W + R_LORA_A + R_LORA_G
R_IN_W = 3 * R_W + R_LORA
D_FF = ((8 * D_MODEL // 3 + 127) // 128) * 128
F_CONV = 3
NORM_EPS = 1e-6
GN_EPS = 64e-5
NEG_INF = -1e30

LANES = 128
SUBLANES = 8
HALF = 64
HALF_SHIFT = 6
SUBLANE_SHIFT = 3
N_PAIRS = 4

C_RW = 0
C_DT = R_IN_W
C_Q = 2048
C_K = C_Q + A_W
C_V = C_K + A_W
C_Z = C_V + A_W
C_XBC = C_Z + S_W
C_GATE = C_XBC + S_CONV_CH
N_PROJ = C_GATE + 3 * D_MODEL

VMEM_LIMIT = 56 * 1024 * 1024


def _cparams(sem):
    return pltpu.CompilerParams(dimension_semantics=sem, vmem_limit_bytes=VMEM_LIMIT)


def _dot(a, b):
    return jnp.dot(a.astype(BF16), b.astype(BF16), preferred_element_type=F32)


def _dot_nt(a, b):
    return lax.dot_general(a.astype(BF16), b.astype(BF16), (((1,), (1,)), ((), ())),
                           preferred_element_type=F32)


def _dot_tn(a, b):
    return lax.dot_general(a.astype(BF16), b.astype(BF16), (((0,), (0,)), ((), ())),
                           preferred_element_type=F32)


def _split_dot(x, w, passes, x_is_lhs=True):
    w = w.astype(BF16)
    acc = None
    rem = x
    for _ in range(passes):
        piece = rem.astype(BF16)
        term = (jnp.dot(piece, w, preferred_element_type=F32) if x_is_lhs
                else jnp.dot(w, piece, preferred_element_type=F32))
        acc = term if acc is None else acc + term
        rem = rem - piece.astype(F32)
    return acc


def _dot_nt_hi(a, b):
    return lax.dot_general(a, b, (((1,), (1,)), ((), ())), preferred_element_type=F32,
                           precision=lax.Precision.HIGHEST)


def _sigmoid(x):
    return 1.0 / (1.0 + jnp.exp(-x))


def _silu(x):
    return x * _sigmoid(x)


def _softplus(x):
    return jnp.maximum(x, 0.0) + jnp.log(1.0 + jnp.exp(-jnp.abs(x)))


def _head_ones(width, scale):
    r = lax.shift_right_logical(lax.broadcasted_iota(jnp.int32, (width, width), 0), HALF_SHIFT)
    c = lax.shift_right_logical(lax.broadcasted_iota(jnp.int32, (width, width), 1), HALF_SHIFT)
    return jnp.where(r == c, scale, 0.0).astype(F32)


def _rms_matmul_kernel(x_ref, g_ref, w_ref, o_ref, xn_ref):
    @pl.when(pl.program_id(1) == 0)
    def _():
        x = x_ref[...]
        ms = jnp.mean(x * x, axis=-1, keepdims=True)
        xn_ref[...] = (x * lax.rsqrt(ms + NORM_EPS) * g_ref[...]).astype(BF16)

    o_ref[...] = jnp.dot(xn_ref[...], w_ref[...], preferred_element_type=F32)


def _rms_matmul(x, g, w, tm, tn):
    t, d = x.shape
    n = w.shape[1]
    return pl.pallas_call(
        _rms_matmul_kernel,
        grid=(t // tm, n // tn),
        in_specs=[pl.BlockSpec((tm, d), lambda i, j: (i, 0)),
                  pl.BlockSpec((1, d), lambda i, j: (0, 0)),
                  pl.BlockSpec((d, tn), lambda i, j: (0, j))],
        out_specs=pl.BlockSpec((tm, tn), lambda i, j: (i, j)),
        out_shape=jax.ShapeDtypeStruct((t, n), F32),
        scratch_shapes=[pltpu.VMEM((tm, d), BF16)],
        compiler_params=_cparams(("parallel", "arbitrary")),
        name="in_proj",
    )(x, g, w)


def _topk_bias(gate, n_valid, axis=1):
    pos = lax.broadcasted_iota(jnp.int32, gate.shape, axis)
    pos_f = pos.astype(F32)
    gm = jnp.where(pos < n_valid, gate, NEG_INF)
    selected = jnp.zeros(gate.shape, jnp.bool_)
    for j in range(MOBA_TOPK):
        m = jnp.max(gm, axis=axis, keepdims=True)
        idx = jnp.min(jnp.where(gm == m, pos_f, 1e9), axis=axis, keepdims=True)
        hit = pos_f == idx
        selected = jnp.logical_or(selected, jnp.logical_and(hit, j < n_valid))
        gm = jnp.where(hit, -jnp.inf, gm)
    return jnp.where(selected, 0.0, NEG_INF).astype(F32)


def _bf16_pieces(x, n):
    out = []
    rem = x
    for _ in range(n):
        piece = rem.astype(BF16).astype(F32)
        out.append(piece)
        rem = rem - piece
    return out


def _moba_prompt_kernel(q_ref, k_ref, v_ref, sl_ref, o_ref, kf_ref, vth_ref, kmean_ref, sel_ref, acc_ref,
                        raw_a, raw_b, m_ref, knorm_ref, *, nb):
    i = pl.program_id(2)
    B = MOBA_BLOCK
    n_piece = 3
    lane = lax.broadcasted_iota(jnp.int32, (1, LANES), 1)
    halves = (lane < HALF, lane >= HALF)

    @pl.when(i == 0)
    def _():
        kmean_ref[...] = jnp.zeros_like(kmean_ref)
        key_local = jnp.bitwise_and(lax.broadcasted_iota(jnp.int32, (k_ref.shape[1], 1), 0), B - 1).astype(F32)
        extra = jnp.where(lane < n_piece, key_local, jnp.where(lane < 2 * n_piece, 1.0, 0.0))
        kf_ref[...] = jnp.concatenate([k_ref[0], extra], axis=1).astype(BF16)
        ones_rows = jnp.ones((SUBLANES, B), F32)
        knorm = [jnp.zeros((1, LANES), F32), jnp.zeros((1, LANES), F32)]
        for n in range(nb):
            kblk = k_ref[0, n * B:(n + 1) * B, :]
            kmean_ref[n:n + 1, :] = jnp.mean(kblk, axis=0, keepdims=True)
            vt = v_ref[0, n * B:(n + 1) * B, :].T
            for h in range(2):
                vth_ref[h, n] = jnp.concatenate([vt[h * HALF:(h + 1) * HALF], ones_rows], axis=0).astype(BF16)
                norm2 = jnp.sum(jnp.where(halves[h], kblk * kblk, 0.0), axis=-1, keepdims=True)
                knorm[h] = jnp.where(lane == n, jnp.max(norm2, axis=0, keepdims=True), knorm[h])
        for h in range(2):
            knorm_ref[h] = jnp.broadcast_to(knorm[h], (SUBLANES, LANES))

    own = i
    q = q_ref[0]
    log2e = 1.0 / math.log(2.0)
    slope2 = (sl_ref[0, :, 0:1] * log2e, sl_ref[0, :, HALF:HALF + 1] * log2e)
    kmean = kmean_ref[...]
    q_local = lax.broadcasted_iota(jnp.int32, (Q_BLOCK, 1), 0).astype(F32)
    q_rows = []
    for h in range(2):
        q_m = jnp.where(halves[h], q, 0.0)
        bias_t = _topk_bias(_dot_nt_hi(kmean, q_m), own, axis=0)
        for n in range(nb):
            sel_ref[h, n] = jnp.broadcast_to(bias_t[n:n + 1, :], (SUBLANES, Q_BLOCK))
        pieces = _bf16_pieces(slope2[h], n_piece) + _bf16_pieces(-slope2[h] * q_local, n_piece)
        extra = jnp.zeros((Q_BLOCK, LANES), F32)
        for j, piece in enumerate(pieces):
            extra = jnp.where(lane == j, piece, extra)
        q_rows.append(jnp.concatenate([q_m * (ATTN_SCALE * log2e), extra], axis=1))
    qf = jnp.concatenate(q_rows, axis=0).astype(BF16)

    rel = (lax.broadcasted_iota(jnp.int32, (B, Q_BLOCK), 0) - lax.broadcasted_iota(jnp.int32, (B, Q_BLOCK), 1))

    def scores(n):
        both = _dot_nt(kf_ref[pl.ds(pl.multiple_of(n * B, B), B), :], qf)
        return [both[:, h * Q_BLOCK:(h + 1) * Q_BLOCK] for h in range(2)]

    def put_scores(n, dst_ref):
        for h, s in enumerate(scores(n)):
            dst_ref[h] = s

    def row_of(ref, h):
        return ref[h][0:1, :]

    def put_row(ref, h, x):
        ref[h] = jnp.broadcast_to(x, (SUBLANES, Q_BLOCK))

    def attend(n, src_ref):
        shift = ((n - i) * B).astype(F32)
        ps, alphas = [], []
        for h in range(2):
            s = src_ref[h] + (sel_ref[h, n][0:1, :] + slope2[h] * shift)
            m0 = row_of(m_ref, h)
            m1 = jnp.maximum(m0, jnp.max(s, axis=0, keepdims=True))
            alphas.append(jnp.exp2(m0 - m1))
            ps.append(jnp.exp2(s - m1).astype(BF16))
            put_row(m_ref, h, m1)
        pv = [jnp.dot(vth_ref[h, n], ps[h], preferred_element_type=F32) for h in range(2)]
        for h in range(2):
            acc_ref[h] = alphas[h] * acc_ref[h] + pv[h]

    lane_f = lane.astype(F32)
    bound_c = ATTN_SCALE * log2e * 1.02
    skips = []
    for h in range(2):
        q_m = jnp.where(halves[h], q, 0.0)
        qn2 = jnp.max(jnp.sum(q_m * q_m, axis=-1, keepdims=True), axis=0, keepdims=True)
        ub = jnp.sqrt(qn2 * knorm_ref[h][0:1, :]) * bound_c
        ub_own = jnp.sum(jnp.where(lane == own, ub, 0.0), axis=-1, keepdims=True)
        far = slope2[h] * ((lane_f - own.astype(F32)) * B + (B - 1))
        skips.append(ub + far < -ub_own - 160.0)
    keep = jnp.logical_and(lane < own, jnp.logical_not(jnp.logical_and(skips[0], skips[1])))
    first = jnp.min(jnp.where(keep, lane_f, own.astype(F32))).astype(jnp.int32)
    n_visit = own - first

    put_scores(jnp.minimum(first, nb - 1), raw_a)

    own_raw = scores(own)
    own_p = []
    for h in range(2):
        s = jnp.where(rel <= 0, own_raw[h], NEG_INF)
        m = jnp.max(s, axis=0, keepdims=True)
        put_row(m_ref, h, m)
        own_p.append(jnp.exp2(s - m).astype(BF16))
    for h in range(2):
        acc_ref[h] = jnp.dot(vth_ref[h, own], own_p[h], preferred_element_type=F32)

    def body(j, carry):
        n0 = first + 2 * j
        put_scores(n0 + 1, raw_b)
        attend(n0, raw_a)
        put_scores(jnp.minimum(n0 + 2, nb - 1), raw_a)
        attend(n0 + 1, raw_b)
        return carry

    lax.fori_loop(0, n_visit // 2, body, 0)

    @pl.when(n_visit % 2 == 1)
    def _():
        attend(own - 1, raw_a)

    out_t = jnp.concatenate([acc_ref[h][0:HALF] / acc_ref[h][HALF:HALF + 1] for h in range(2)], axis=0)
    o_ref[0] = out_t.T


def _moba_prompt(proj3, slopes_pair):
    bsz, seq, _ = proj3.shape
    assert seq % MOBA_BLOCK == 0 and seq // MOBA_BLOCK >= MOBA_TOPK
    nb = seq // MOBA_BLOCK
    nbp = -(-nb // SUBLANES) * SUBLANES
    nq = seq // Q_BLOCK
    qc, kc, vc = C_Q // LANES, C_K // LANES, C_V // LANES
    return pl.pallas_call(
        functools.partial(_moba_prompt_kernel, nb=nb),
        grid=(bsz, N_PAIRS, nq),
        in_specs=[pl.BlockSpec((1, Q_BLOCK, LANES), lambda b, p, i: (b, i, qc + p)),
                  pl.BlockSpec((1, seq, LANES), lambda b, p, i: (b, 0, kc + p)),
                  pl.BlockSpec((1, seq, LANES), lambda b, p, i: (b, 0, vc + p)),
                  pl.BlockSpec((1, 1, LANES), lambda b, p, i: (p, 0, 0))],
        out_specs=pl.BlockSpec((1, Q_BLOCK, LANES), lambda b, p, i: (b, i, p)),
        out_shape=jax.ShapeDtypeStruct((bsz, seq, A_W), F32),
        scratch_shapes=[pltpu.VMEM((seq, 2 * LANES), BF16),
                        pltpu.VMEM((2, nb, HALF + SUBLANES, MOBA_BLOCK), BF16),
                        pltpu.VMEM((nbp, LANES), F32), pltpu.VMEM((2, nb, SUBLANES, Q_BLOCK), F32),
                        pltpu.VMEM((2, HALF + SUBLANES, Q_BLOCK), F32),
                        pltpu.VMEM((2, MOBA_BLOCK, Q_BLOCK), F32), pltpu.VMEM((2, MOBA_BLOCK, Q_BLOCK), F32),
                        pltpu.VMEM((2, SUBLANES, Q_BLOCK), F32), pltpu.VMEM((2, SUBLANES, LANES), F32)],
        compiler_params=_cparams(("parallel", "parallel", "arbitrary")),
        name="moba_prompt",
    )(proj3, proj3, proj3, slopes_pair)


def _moba_sample_kernel(pt_ref, q_ref, kn_ref, vn_ref, *refs, n_blk, ppb, page, tnew):
    del pt_ref
    n_pages = n_blk * ppb
    k_refs, v_refs = refs[:n_pages], refs[n_pages:2 * n_pages]
    o_ref, m_ref, l_ref, g_ref, acc_ref = refs[2 * n_pages:]
    past = n_blk * MOBA_BLOCK
    rows = A_HEADS * SUBLANES
    lane_head = lax.shift_right_logical(lax.broadcasted_iota(jnp.int32, (SUBLANES, A_W), 1), HALF_SHIFT)
    q8 = q_ref[0] * ATTN_SCALE
    qbd = jnp.concatenate([jnp.where(lane_head == h, q8, 0.0) for h in range(A_HEADS)], axis=0)
    q_hi = qbd.astype(BF16)
    q_lo = (qbd - q_hi.astype(F32)).astype(BF16)
    r = lax.broadcasted_iota(jnp.int32, (rows, 1), 0)
    tok = jnp.bitwise_and(r, SUBLANES - 1).astype(F32)
    slope = jnp.exp2(-(8.0 / A_HEADS) * (lax.shift_right_logical(r, SUBLANE_SHIFT) + 1).astype(F32))
    key = lax.broadcasted_iota(jnp.int32, (1, MOBA_BLOCK), 1).astype(F32)

    def block_t(refs_, n):
        return jnp.concatenate([refs_[j][0, 0].reshape(A_W, page) for j in range(n * ppb, (n + 1) * ppb)],
                               axis=1).astype(BF16)

    raws, fixes = [], []
    for n in range(n_blk):
        kt = block_t(k_refs, n)
        raws.append(jnp.dot(q_hi, kt, preferred_element_type=F32))
        fixes.append(jnp.dot(q_lo, kt, preferred_element_type=F32))
    es = []
    for n in range(n_blk):
        g_ref[n] = jnp.broadcast_to(jnp.sum(raws[n] + fixes[n], axis=-1, keepdims=True) * (1.0 / MOBA_BLOCK),
                                    (rows, LANES))
        s = raws[n] - slope * ((past - n * MOBA_BLOCK + tok) - key)
        m = jnp.max(s, axis=-1, keepdims=True)
        e = jnp.exp(s - m)
        m_ref[n] = jnp.broadcast_to(m, (rows, LANES))
        l_ref[n] = jnp.broadcast_to(jnp.sum(e, axis=-1, keepdims=True), (rows, LANES))
        es.append(e.astype(BF16))
    for n in range(n_blk):
        acc_ref[n] = lax.dot_general(es[n], block_t(v_refs, n), (((1,), (1,)), ((), ())),
                                     preferred_element_type=F32)

    lane = lax.broadcasted_iota(jnp.int32, (rows, LANES), 1)
    gate = jnp.zeros((rows, LANES), F32)
    for j in range(n_blk):
        gate = jnp.where(lane == j, g_ref[j], gate)
    bias = _topk_bias(gate, n_blk)

    kn = kn_ref[0]
    vn = vn_ref[0]
    s_own = []
    for j in range(tnew):
        sj = jnp.sum(qbd * kn[j:j + 1, :], axis=-1, keepdims=True) - slope * (tok - j)
        s_own.append(jnp.where(tok >= j, sj, NEG_INF))
    mx = s_own[0]
    for j in range(1, tnew):
        mx = jnp.maximum(mx, s_own[j])
    mb = []
    for j in range(n_blk):
        mj = m_ref[j][:, 0:1] + bias[:, j:j + 1]
        mb.append(mj)
        mx = jnp.maximum(mx, mj)
    lsum = jnp.zeros((rows, 1), F32)
    acc = jnp.zeros((rows, A_W), F32)
    for j in range(tnew):
        w = jnp.exp(s_own[j] - mx)
        lsum = lsum + w
        acc = acc + w * vn[j:j + 1, :]
    for j in range(n_blk):
        w = jnp.exp(mb[j] - mx)
        lsum = lsum + w * l_ref[j][:, 0:1]
        acc = acc + w * acc_ref[j]
    out = acc / lsum
    o8 = jnp.zeros((SUBLANES, A_W), F32)
    for h in range(A_HEADS):
        o8 = o8 + jnp.where(lane_head == h, out[h * SUBLANES:(h + 1) * SUBLANES, :], 0.0)
    o_ref[0] = o8


def _moba_sample(q8, k8, v8, cache_kt, cache_vt, page_table, layer, tnew):
    dbsz = q8.shape[0]
    n_pages = page_table.shape[1]
    page = cache_kt.shape[4]
    assert MOBA_BLOCK % page == 0 and (n_pages * page) % MOBA_BLOCK == 0 and page % LANES == 0
    ppb = MOBA_BLOCK // page
    n_blk = n_pages // ppb
    assert n_blk >= MOBA_TOPK and tnew <= SUBLANES
    rows = A_HEADS * SUBLANES
    tok_spec = pl.BlockSpec((1, SUBLANES, A_W), lambda b, pt: (b, 0, 0))

    def page_spec(j):
        return pl.BlockSpec((1, 1, A_HEADS, A_HD, page), lambda b, pt: (layer, pt[b * n_pages + j], 0, 0, 0))

    pages = [page_spec(j) for j in range(n_pages)]
    grid_spec = pltpu.PrefetchScalarGridSpec(
        num_scalar_prefetch=1,
        grid=(dbsz,),
        in_specs=[tok_spec, tok_spec, tok_spec] + pages + pages,
        out_specs=tok_spec,
        scratch_shapes=[pltpu.VMEM((n_blk, rows, LANES), F32), pltpu.VMEM((n_blk, rows, LANES), F32),
                        pltpu.VMEM((n_blk, rows, LANES), F32), pltpu.VMEM((n_blk, rows, A_W), F32)],
    )
    return pl.pallas_call(
        functools.partial(_moba_sample_kernel, n_blk=n_blk, ppb=ppb, page=page, tnew=tnew),
        grid_spec=grid_spec,
        out_shape=jax.ShapeDtypeStruct((dbsz, SUBLANES, A_W), F32),
        compiler_params=_cparams(("parallel",)),
        name="moba_sample",
    )(page_table.reshape(-1), q8, k8, v8, *([cache_kt] * n_pages), *([cache_vt] * n_pages))


def _ssd_prompt_kernel(z_ref, xbc_ref, dt_ref, cw_ref, cb_ref, dtb_ref, alog_ref, dvec_ref, ng_ref,
                       o_ref, hout_ref, ext_ref, h_ref, *, L):
    c = pl.program_id(1)
    P = SUBLANES

    @pl.when(c == 0)
    def _():
        ext_ref[0:P, :] = jnp.zeros((P, S_CONV_CH), F32)
        h_ref[...] = jnp.zeros_like(h_ref)

    @pl.when(c > 0)
    def _():
        ext_ref[0:P, :] = ext_ref[L:L + P, :]

    ext_ref[P:P + L, :] = xbc_ref[...]
    acc = cb_ref[...] + cw_ref[S_CONV - 1:S_CONV, :] * ext_ref[P:P + L, :]
    for d in range(1, S_CONV):
        acc = acc + cw_ref[S_CONV - 1 - d:S_CONV - d, :] * ext_ref[P - d:P - d + L, :]
    xbc = _silu(acc)
    xs = xbc[:, :S_W]
    bm = xbc[:, S_W:S_W + S_GROUPS * S_STATE]
    cm = xbc[:, S_W + S_GROUPS * S_STATE:]

    lane = lax.broadcasted_iota(jnp.int32, (1, LANES), 1)
    is_a = lane < HALF
    dt = jnp.where(lane < S_HEADS, _softplus(dt_ref[...] + dtb_ref[...]), 0.0)
    da = dt * (-jnp.exp(alog_ref[...]))
    rr = lax.broadcasted_iota(jnp.int32, (L, L), 0)
    cc = lax.broadcasted_iota(jnp.int32, (L, L), 1)
    causal = rr >= cc
    cum = _split_dot(da, jnp.where(causal, 1.0, 0.0), 3, x_is_lhs=False)
    cum_t = cum.T
    dt_t = dt.T
    row_a = lax.broadcasted_iota(jnp.int32, (2 * S_HD, 1), 0) < S_HD

    ys = []
    cb_g = [None] * S_GROUPS
    for pr in range(N_PAIRS):
        g = (2 * pr * S_GROUPS) // S_HEADS
        bg = bm[:, g * S_STATE:(g + 1) * S_STATE]
        cg = cm[:, g * S_STATE:(g + 1) * S_STATE]
        if cb_g[g] is None:
            cb_g[g] = _dot_nt(cg, bg)
        xs_p = xs[:, pr * LANES:(pr + 1) * LANES]
        heads = (2 * pr, 2 * pr + 1)
        halves = (is_a, jnp.logical_not(is_a))
        yp = jnp.zeros((L, LANES), F32)
        cum_c = [cum[:, h:h + 1] for h in heads]
        for h, half, cc_h in zip(heads, halves, cum_c):
            seg = cc_h - cum_t[h:h + 1, :]
            dec = jnp.exp(jnp.where(causal, seg, -jnp.inf))
            wts = cb_g[g] * dec * dt_t[h:h + 1, :]
            yp = yp + _dot(wts, jnp.where(half, xs_p, 0.0))
        hp = h_ref[pr]
        yp = yp + _dot_nt(cg, hp) * jnp.where(is_a, jnp.exp(cum_c[0]), jnp.exp(cum_c[1]))
        last = [cum[L - 1:L, h:h + 1] for h in heads]
        te = jnp.where(is_a, jnp.exp(last[0] - cum_c[0]) * dt[:, heads[0]:heads[0] + 1],
                       jnp.exp(last[1] - cum_c[1]) * dt[:, heads[1]:heads[1] + 1])
        st = _dot_tn(xs_p * te, bg)
        h_ref[pr] = hp * jnp.where(row_a, jnp.exp(last[0]), jnp.exp(last[1])) + st
        ys.append(yp)
    y = jnp.concatenate(ys, axis=1) + dvec_ref[...] * xs
    yz = y * _silu(z_ref[...])
    gw = S_W // S_GROUPS
    for g in range(S_GROUPS):
        part = yz[:, g * gw:(g + 1) * gw]
        ms = jnp.mean(part * part, axis=-1, keepdims=True)
        o_ref[:, g * gw:(g + 1) * gw] = part * lax.rsqrt(ms + NORM_EPS) * ng_ref[:, g * gw:(g + 1) * gw]

    @pl.when(c == pl.num_programs(1) - 1)
    def _():
        hout_ref[0] = h_ref[...]


def _ssd_prompt(proj, bsz, seq, cw, cb, dtb, alog, dvec, ng):
    L = SSD_CHUNK
    nc = seq // L
    const = lambda shape: pl.BlockSpec(shape, lambda b, c: (0,) * len(shape))
    return pl.pallas_call(
        functools.partial(_ssd_prompt_kernel, L=L),
        grid=(bsz, nc),
        in_specs=[pl.BlockSpec((L, S_W), lambda b, c: (b * nc + c, C_Z // S_W)),
                  pl.BlockSpec((L, S_CONV_CH), lambda b, c: (b * nc + c, C_XBC // S_CONV_CH)),
                  pl.BlockSpec((L, LANES), lambda b, c: (b * nc + c, C_DT // LANES)),
                  const((S_CONV, S_CONV_CH)), const((1, S_CONV_CH)), const((1, LANES)), const((1, LANES)),
                  const((1, S_W)), const((1, S_W))],
        out_specs=[pl.BlockSpec((L, S_W), lambda b, c: (b * nc + c, 0)),
                   pl.BlockSpec((1, N_PAIRS, 2 * S_HD, S_STATE), lambda b, c: (b, 0, 0, 0))],
        out_shape=[jax.ShapeDtypeStruct((bsz * seq, S_W), F32),
                   jax.ShapeDtypeStruct((bsz, N_PAIRS, 2 * S_HD, S_STATE), F32)],
        scratch_shapes=[pltpu.VMEM((L + 2 * SUBLANES, S_CONV_CH), F32),
                        pltpu.VMEM((N_PAIRS, 2 * S_HD, S_STATE), F32)],
        compiler_params=_cparams(("parallel", "arbitrary")),
        name="ssd_prompt",
    )(proj, proj, proj, cw, cb, dtb, alog, dvec, ng)


def _ssd_sample_kernel(z_ref, xbc_ref, dt_ref, pre_ref, h0_ref, cw_ref, cb_ref, dtbv_ref, alogv_ref,
                       dvec_ref, ng_ref, o_ref, hout_ref, *, T, nb):
    rows2 = 2 * S_HD
    eye = (lax.broadcasted_iota(jnp.int32, (rows2, LANES), 0)
           == lax.broadcasted_iota(jnp.int32, (rows2, LANES), 1)).astype(F32)
    eye_all = jnp.concatenate([eye] * nb, axis=0)
    ones = jnp.ones((LANES, LANES), F32)
    spread = (lax.broadcasted_iota(jnp.int32, (LANES, S_W), 0)
              == lax.shift_right_logical(lax.broadcasted_iota(jnp.int32, (LANES, S_W), 1), HALF_SHIFT)).astype(F32)
    a_vec = -jnp.exp(alogv_ref[...])
    gw = S_W // S_GROUPS

    def as_columns(x, passes):
        return _split_dot(eye_all * _repeat_rows(x, rows2), ones, passes)

    up = [pre_ref[j] for j in range(S_CONV - 1)] + [xbc_ref[t] for t in range(T)]
    hs = [h0_ref[0, :, pr].reshape(nb * rows2, S_STATE) for pr in range(N_PAIRS)]
    for t in range(T):
        acc = cb_ref[...] + cw_ref[0:1, :] * up[t]
        for j in range(1, S_CONV):
            acc = acc + cw_ref[j:j + 1, :] * up[t + j]
        xc = _silu(acc)
        xs = xc[:, :S_W]
        dt = _softplus(_split_dot(dt_ref[t], spread, 3) + dtbv_ref[...])
        decay = jnp.exp(dt * a_vec)
        xdt = xs * dt
        ys = []
        for pr in range(N_PAIRS):
            g = (2 * pr * S_GROUPS) // S_HEADS
            sl = slice(pr * LANES, (pr + 1) * LANES)
            b_rows = _repeat_rows(xc[:, S_W + g * S_STATE:S_W + (g + 1) * S_STATE], rows2)
            c_rows = _repeat_rows(xc[:, S_W + (S_GROUPS + g) * S_STATE:S_W + (S_GROUPS + g + 1) * S_STATE], rows2)
            hs[pr] = hs[pr] * as_columns(decay[:, sl], 3) + as_columns(xdt[:, sl], 2) * b_rows
            y_col = _split_dot(hs[pr] * c_rows, ones, 2)
            ys.append(jnp.sum((eye_all * y_col).reshape(nb, rows2, LANES), axis=1))
        y = jnp.concatenate(ys, axis=1) + dvec_ref[...] * xs
        yz = y * _silu(z_ref[t])
        for g in range(S_GROUPS):
            part = yz[:, g * gw:(g + 1) * gw]
            ms = jnp.mean(part * part, axis=-1, keepdims=True)
            o_ref[t, :, g * gw:(g + 1) * gw] = part * lax.rsqrt(ms + NORM_EPS) * ng_ref[:, g * gw:(g + 1) * gw]
    for pr in range(N_PAIRS):
        hout_ref[:, pr] = hs[pr].reshape(nb, rows2, S_STATE)


def _ssd_sample(proj3, pre3, h0, layer, cw, cb, dtbv, alogv, dvec, ng, nb=8):
    T, dbsz = proj3.shape[:2]
    const = lambda shape: pl.BlockSpec(shape, lambda j: (0,) * len(shape))
    hspec = pl.BlockSpec((nb, N_PAIRS, 2 * S_HD, S_STATE), lambda j: (j, 0, 0, 0))
    h0spec = pl.BlockSpec((1, nb, N_PAIRS, 2 * S_HD, S_STATE), lambda j: (layer, j, 0, 0, 0))
    return pl.pallas_call(
        functools.partial(_ssd_sample_kernel, T=T, nb=nb),
        grid=(dbsz // nb,),
        in_specs=[pl.BlockSpec((T, nb, S_W), lambda j: (0, j, C_Z // S_W)),
                  pl.BlockSpec((T, nb, S_CONV_CH), lambda j: (0, j, C_XBC // S_CONV_CH)),
                  pl.BlockSpec((T, nb, LANES), lambda j: (0, j, C_DT // LANES)),
                  pl.BlockSpec((S_CONV - 1, nb, S_CONV_CH), lambda j: (0, j, 0)),
                  h0spec,
                  const((S_CONV, S_CONV_CH)), const((1, S_CONV_CH)), const((1, S_W)), const((1, S_W)),
                  const((1, S_W)), const((1, S_W))],
        out_specs=[pl.BlockSpec((T, nb, S_W), lambda j: (0, j, 0)), hspec],
        out_shape=[jax.ShapeDtypeStruct((T, dbsz, S_W), F32),
                   jax.ShapeDtypeStruct((dbsz, N_PAIRS, 2 * S_HD, S_STATE), F32)],
        compiler_params=_cparams(("parallel",)),
        name="ssd_sample",
    )(proj3, proj3, proj3, pre3, h0, cw, cb, dtbv, alogv, dvec, ng)


def _rwkv_prep_kernel(u_ref, pre_ref, mu_ref, w0_ref, w2_ref, a0_ref, a2_ref, g2_ref, kk_ref, ka_ref,
                      r_out, w_out, k_out, v_out, kk_out, kka_out, g_out, ext_ref,
                      *, tm, P, stride, tiles_per_seq):
    i = pl.program_id(0)
    first = (i % tiles_per_seq) == 0

    @pl.when(first)
    def _():
        ext_ref[0:P, :] = pre_ref[...]

    @pl.when(jnp.logical_not(first))
    def _():
        ext_ref[0:P, :] = ext_ref[tm:tm + P, :]

    u = u_ref[...]
    ext_ref[P:P + tm, :] = u
    prev = ext_ref[P - stride:P - stride + tm, :]
    x = u + (prev - u) * mu_ref[...]
    r = x[:, 0:R_W]
    kr = x[:, R_W:2 * R_W]
    vr = x[:, 2 * R_W:3 * R_W]
    xl = x[:, 3 * R_W:]
    w_log = w0_ref[...] + _dot(jnp.tanh(xl), w2_ref[...])
    log_decay = -jnp.exp(-_softplus(-w_log) - 0.5)
    a = _sigmoid(a0_ref[...] + _dot(xl, a2_ref[...]))
    g = _dot(_sigmoid(xl), g2_ref[...])
    kk = kr * kk_ref[...]
    ss = _split_dot(kk * kk, _head_ones(R_W, 1.0), 2)
    kk = kk / jnp.maximum(jnp.sqrt(ss), 1e-12)
    r_out[...] = r
    w_out[...] = log_decay
    k_out[...] = kr * (1.0 + (a - 1.0) * ka_ref[...])
    v_out[...] = vr
    kk_out[...] = kk
    kka_out[...] = kk * a
    g_out[...] = g


def _rwkv_prep(proj, pre, mu, w0, w2p, a0, a2p, g2p, kkw, kaw, tm, stride, tiles_per_seq):
    t = proj.shape[0]
    P = pre.shape[0]
    const = lambda shape: pl.BlockSpec(shape, lambda i: (0,) * len(shape))
    outs = pl.pallas_call(
        functools.partial(_rwkv_prep_kernel, tm=tm, P=P, stride=stride, tiles_per_seq=tiles_per_seq),
        grid=(t // tm,),
        in_specs=[pl.BlockSpec((tm, R_IN_W), lambda i: (i, 0)),
                  const((P, R_IN_W)), const((1, R_IN_W)), const((1, R_W)), const((R_LORA, R_W)),
                  const((1, R_W)), const((R_LORA, R_W)), const((R_LORA, R_W)), const((1, R_W)), const((1, R_W))],
        out_specs=[pl.BlockSpec((tm, R_W), lambda i: (i, 0))] * 7,
        out_shape=[jax.ShapeDtypeStruct((t, R_W), F32)] * 7,
        scratch_shapes=[pltpu.VMEM((tm + 2 * P, R_IN_W), F32)],
        compiler_params=_cparams(("arbitrary",)),
        name="rwkv_prep",
    )(proj, pre, mu, w0, w2p, a0, a2p, g2p, kkw, kaw)
    return outs


def _pair_consts():
    lane = lax.broadcasted_iota(jnp.int32, (R_HD, LANES), 1)
    row = lax.broadcasted_iota(jnp.int32, (R_HD, LANES), 0)
    is_a = lane < HALF
    eye2 = (jnp.bitwise_and(lane, HALF - 1) == row).astype(F32)
    return is_a, eye2


def _pair_sum(x, is_a):
    sa = jnp.sum(jnp.where(is_a, x, 0.0), axis=-1, keepdims=True)
    sb = jnp.sum(jnp.where(is_a, 0.0, x), axis=-1, keepdims=True)
    return jnp.where(is_a, sa, sb)


def _rwkv_step(s, rr, ww, kr, vv, kk, kka, is_a, eye2):
    v_col = _pair_sum(eye2 * vv, is_a)
    sk = _pair_sum(s * kk, is_a)
    s = s * ww - sk * kka + v_col * kr
    y_col = _pair_sum(s * rr, is_a)
    return s, jnp.sum(eye2 * y_col, axis=0, keepdims=True)


def _unit_lower_inverses(ns):
    L = ns[0].shape[0]
    eye = (lax.broadcasted_iota(jnp.int32, (L, L), 0) == lax.broadcasted_iota(jnp.int32, (L, L), 1)).astype(F32)
    ts = [eye + n for n in ns]
    pws = list(ns)
    for _ in range(int(math.log2(L)) - 1):
        pws = [_dot(p, p) for p in pws]
        ts = [t + _dot(t, p) for t, p in zip(ts, pws)]
    return ts


def _rwkv_chunk_prompt_kernel(r_ref, lw_ref, k_ref, v_ref, kk_ref, kka_ref, y_ref, sout_ref, s_ref, *, L):
    c = pl.program_id(1)

    @pl.when(c == 0)
    def _():
        s_ref[...] = jnp.zeros_like(s_ref)

    lane = lax.broadcasted_iota(jnp.int32, (1, LANES), 1)
    is_a = lane < HALF
    rr = lax.broadcasted_iota(jnp.int32, (L, L), 0)
    cc = lax.broadcasted_iota(jnp.int32, (L, L), 1)
    incl = rr >= cc
    strict = rr > cc
    tri = jnp.where(incl, 1.0, 0.0).astype(F32)
    r2 = lax.broadcasted_iota(jnp.int32, (LANES, LANES), 0) < HALF
    c2 = lax.broadcasted_iota(jnp.int32, (LANES, LANES), 1) < HALF
    same_head = r2 == c2

    r, lw, k, v, kk, kka = (ref[...] for ref in (r_ref, lw_ref, k_ref, v_ref, kk_ref, kka_ref))
    cum = _split_dot(lw, tri, 3, x_is_lhs=False)
    last = cum[L - 1:L, :]
    inv_p = jnp.exp(-cum)
    to_end = jnp.exp(last - cum)
    b_t = kk * jnp.exp(cum - lw)
    a_t = -kka * inv_p
    k_t = k * inv_p
    r_t = r * jnp.exp(cum)
    a_end = -kka * to_end
    k_end = k * to_end
    decay_end = jnp.exp(last)
    pairs = [slice(pr * LANES, (pr + 1) * LANES) for pr in range(N_PAIRS)]
    halves = (is_a, jnp.logical_not(is_a))
    s0 = [s_ref[pr] for pr in range(N_PAIRS)]

    ns, mks, ras, rks = [], [], [], []
    for sl in pairs:
        for half in halves:
            bm = jnp.where(half, b_t[:, sl], 0.0)
            rm = jnp.where(half, r_t[:, sl], 0.0)
            ns.append(jnp.where(strict, _dot_nt(bm, a_t[:, sl]), 0.0))
            mks.append(jnp.where(strict, _dot_nt(bm, k_t[:, sl]), 0.0))
            ras.append(jnp.where(incl, _dot_nt(rm, a_t[:, sl]), 0.0))
            rks.append(jnp.where(incl, _dot_nt(rm, k_t[:, sl]), 0.0))
    ts = _unit_lower_inverses(ns)
    rhs = [_dot_nt(b_t[:, sl], s0[pr]) + jnp.where(is_a, _dot(mks[2 * pr], v[:, sl]), _dot(mks[2 * pr + 1], v[:, sl]))
           for pr, sl in enumerate(pairs)]
    us = [jnp.where(is_a, _dot(ts[2 * pr], rhs[pr]), _dot(ts[2 * pr + 1], rhs[pr])) for pr in range(N_PAIRS)]
    for pr, sl in enumerate(pairs):
        ha, hb = 2 * pr, 2 * pr + 1
        y_ref[:, sl] = _dot_nt(r_t[:, sl], s0[pr]) + jnp.where(
            is_a, _dot(ras[ha], us[pr]) + _dot(rks[ha], v[:, sl]), _dot(ras[hb], us[pr]) + _dot(rks[hb], v[:, sl]))
    for pr, sl in enumerate(pairs):
        upd = _dot_tn(us[pr], a_end[:, sl]) + _dot_tn(v[:, sl], k_end[:, sl])
        s_ref[pr] = s0[pr] * decay_end[:, sl] + jnp.where(same_head, upd, 0.0)

    @pl.when(c == pl.num_programs(1) - 1)
    def _():
        sout_ref[0] = s_ref[...]


def _rwkv_scan_prompt(seqs, bsz, seq, L):
    nc = seq // L
    spec = pl.BlockSpec((L, R_W), lambda b, c: (b * nc + c, 0))
    return pl.pallas_call(
        functools.partial(_rwkv_chunk_prompt_kernel, L=L),
        grid=(bsz, nc),
        in_specs=[spec] * 6,
        out_specs=[spec, pl.BlockSpec((1, N_PAIRS, LANES, LANES), lambda b, c: (b, 0, 0, 0))],
        out_shape=[jax.ShapeDtypeStruct((bsz * seq, R_W), F32),
                   jax.ShapeDtypeStruct((bsz, N_PAIRS, LANES, LANES), F32)],
        scratch_shapes=[pltpu.VMEM((N_PAIRS, LANES, LANES), F32)],
        compiler_params=_cparams(("parallel", "arbitrary")),
        name="rwkv_scan_prompt",
    )(*seqs)


def _repeat_rows(x, reps):
    return jnp.concatenate([jnp.broadcast_to(x[b:b + 1, :], (reps, x.shape[1])) for b in range(x.shape[0])], axis=0)


def _rwkv_scan_sample_kernel(r_ref, w_ref, k_ref, v_ref, kk_ref, kka_ref, s0_ref, y_ref, sout_ref, *, T, nb):
    _, eye2 = _pair_consts()
    eye_all = jnp.concatenate([eye2] * nb, axis=0)
    ones_bd = _head_ones(LANES, 1.0)

    def head_sums(x):
        return _split_dot(x, ones_bd, 2)

    states = [s0_ref[:, pr].reshape(nb * R_HD, LANES) for pr in range(N_PAIRS)]
    for t in range(T):
        for pr in range(N_PAIRS):
            sl = slice(pr * LANES, (pr + 1) * LANES)
            rr, ww, kr, vv, kk, kka = (_repeat_rows(x, R_HD) for x in (
                r_ref[t, :, sl], jnp.exp(w_ref[t, :, sl]), k_ref[t, :, sl], v_ref[t, :, sl],
                kk_ref[t, :, sl], kka_ref[t, :, sl]))
            v_col = head_sums(eye_all * vv)
            s = states[pr]
            s = s * ww - head_sums(s * kk) * kka + v_col * kr
            states[pr] = s
            y_col = head_sums(s * rr)
            y_ref[t, :, sl] = jnp.sum((eye_all * y_col).reshape(nb, R_HD, LANES), axis=1)
    for pr in range(N_PAIRS):
        sout_ref[:, pr] = states[pr].reshape(nb, R_HD, LANES)


def _rwkv_scan_sample(seqs3, s0, nb=16):
    T, dbsz = seqs3[0].shape[:2]
    spec = pl.BlockSpec((T, nb, R_W), lambda j: (0, j, 0))
    sspec = pl.BlockSpec((nb, N_PAIRS, R_HD, LANES), lambda j: (j, 0, 0, 0))
    return pl.pallas_call(
        functools.partial(_rwkv_scan_sample_kernel, T=T, nb=nb),
        grid=(dbsz // nb,),
        in_specs=[spec] * 6 + [sspec],
        out_specs=[spec, sspec],
        out_shape=[jax.ShapeDtypeStruct((T, dbsz, R_W), F32),
                   jax.ShapeDtypeStruct((dbsz, N_PAIRS, R_HD, LANES), F32)],
        compiler_params=_cparams(("parallel",)),
        name="rwkv_scan_sample",
    )(*seqs3, s0)


def _merge_kernel(x_ref, oa_ref, ob_ref, yr_ref, r_ref, k_ref, v_ref, g_ref, ga_ref, gb_ref, gc_ref,
                  bg_ref, lng_ref, lnb_ref, rk_ref, wpa_ref, wpb_ref, wpc_ref, wo_ref, o_ref):
    mean_m = _head_ones(R_W, 1.0 / R_HD)
    yr = yr_ref[...]
    d = yr - _split_dot(yr, mean_m, 2)
    var = _split_dot(d * d, mean_m, 2)
    yn = d * lax.rsqrt(var + GN_EPS) * lng_ref[...] + lnb_ref[...]
    v = v_ref[...]
    bonus = _split_dot(r_ref[...] * k_ref[...] * rk_ref[...], _head_ones(R_W, 1.0), 2)
    oc = (yn + bonus * v) * g_ref[...]
    merged = (_sigmoid(ga_ref[...] + bg_ref[:, 0:D_MODEL]) * _dot(oa_ref[...], wpa_ref[...])
              + _sigmoid(gb_ref[...] + bg_ref[:, D_MODEL:2 * D_MODEL]) * _dot(ob_ref[...], wpb_ref[...])
              + _sigmoid(gc_ref[...] + bg_ref[:, 2 * D_MODEL:]) * _dot(oc, wpc_ref[...]))
    o_ref[...] = x_ref[...] + _dot(merged, wo_ref[...])


def _merge(x, proj, oa, ob, yr, r, k2, v, g, bg, lng, lnb, rk, wpa, wpb, wpc, wo, tm):
    t = x.shape[0]
    row = lambda w: pl.BlockSpec((tm, w), lambda i: (i, 0))
    const = lambda shape: pl.BlockSpec(shape, lambda i: (0,) * len(shape))
    gcol = C_GATE // D_MODEL
    gate = lambda j: pl.BlockSpec((tm, D_MODEL), lambda i: (i, gcol + j))
    return pl.pallas_call(
        _merge_kernel,
        grid=(t // tm,),
        in_specs=[row(D_MODEL), row(A_W), row(S_W), row(R_W), row(R_W), row(R_W), row(R_W), row(R_W),
                  gate(0), gate(1), gate(2),
                  const((1, 3 * D_MODEL)), const((1, R_W)), const((1, R_W)), const((1, R_W)),
                  const((A_W, D_MODEL)), const((S_W, D_MODEL)), const((R_W, D_MODEL)),
                  const((D_MODEL, D_MODEL))],
        out_specs=row(D_MODEL),
        out_shape=jax.ShapeDtypeStruct((t, D_MODEL), F32),
        compiler_params=_cparams(("parallel",)),
        name="merge",
    )(x, oa, ob, yr, r, k2, v, g, proj, proj, proj, bg, lng, lnb, rk, wpa, wpb, wpc, wo)


def _ffn_kernel(x_ref, g_ref, wug_ref, wuv_ref, wd_ref, cw_ref, cb_ref, pre_ref, gf_ref,
                o_ref, tail_ref, xn_ref, ext_ref, *, tm, P, stride, tiles_per_seq, final_norm):
    i = pl.program_id(0)
    f = pl.program_id(1)
    first = (i % tiles_per_seq) == 0

    @pl.when(f == 0)
    def _():
        x = x_ref[...]
        ms = jnp.mean(x * x, axis=-1, keepdims=True)
        xn_ref[...] = (x * lax.rsqrt(ms + NORM_EPS) * g_ref[...]).astype(BF16)

    @pl.when(first)
    def _():
        ext_ref[f, 0:P, :] = pre_ref[...]

    @pl.when(jnp.logical_not(first))
    def _():
        ext_ref[f, 0:P, :] = ext_ref[f, tm:tm + P, :]

    xn = xn_ref[...]
    ug = jnp.dot(xn, wug_ref[...], preferred_element_type=F32)
    ext_ref[f, P:P + tm, :] = ug
    tail_ref[0] = ext_ref[f, tm:tm + P, :]
    acc = cb_ref[...] + cw_ref[F_CONV - 1:F_CONV, :] * ug
    for d in range(1, F_CONV):
        acc = acc + cw_ref[F_CONV - 1 - d:F_CONV - d, :] * ext_ref[f, P - d * stride:P - d * stride + tm, :]
    uv = jnp.dot(xn, wuv_ref[...], preferred_element_type=F32)
    contrib = _dot(_silu(acc) * uv, wd_ref[...])

    @pl.when(f == 0)
    def _():
        o_ref[...] = x_ref[...] + contrib

    @pl.when(f > 0)
    def _():
        o_ref[...] = o_ref[...] + contrib

    if final_norm:
        @pl.when(f == pl.num_programs(1) - 1)
        def _():
            y = o_ref[...]
            ms = jnp.mean(y * y, axis=-1, keepdims=True)
            o_ref[...] = y * lax.rsqrt(ms + NORM_EPS) * gf_ref[...]


def _ffn(x, g, wup, wd, cw, cb, pre, gf, tm, tf, stride, tiles_per_seq, final_norm):
    t = x.shape[0]
    P = pre.shape[0]
    nf = D_FF // tf
    return pl.pallas_call(
        functools.partial(_ffn_kernel, tm=tm, P=P, stride=stride, tiles_per_seq=tiles_per_seq,
                          final_norm=final_norm),
        grid=(t // tm, nf),
        in_specs=[pl.BlockSpec((tm, D_MODEL), lambda i, f: (i, 0)),
                  pl.BlockSpec((1, D_MODEL), lambda i, f: (0, 0)),
                  pl.BlockSpec((D_MODEL, tf), lambda i, f: (0, f)),
                  pl.BlockSpec((D_MODEL, tf), lambda i, f: (0, nf + f)),
                  pl.BlockSpec((tf, D_MODEL), lambda i, f: (f, 0)),
                  pl.BlockSpec((F_CONV, tf), lambda i, f: (0, f)),
                  pl.BlockSpec((1, tf), lambda i, f: (0, f)),
                  pl.BlockSpec((P, tf), lambda i, f: (0, f)),
                  pl.BlockSpec((1, D_MODEL), lambda i, f: (0, 0))],
        out_specs=[pl.BlockSpec((tm, D_MODEL), lambda i, f: (i, 0)),
                   pl.BlockSpec((1, P, tf), lambda i, f: (i, 0, f))],
        out_shape=[jax.ShapeDtypeStruct((t, D_MODEL), F32),
                   jax.ShapeDtypeStruct((t // tm, P, D_FF), F32)],
        scratch_shapes=[pltpu.VMEM((tm, D_MODEL), BF16), pltpu.VMEM((nf, tm + 2 * P, tf), F32)],
        compiler_params=_cparams(("arbitrary", "arbitrary")),
        name="conv_ffn",
    )(x, g, wup, wup, wd, cw, cb, pre, gf)


def _pack_rwkv_state(s):
    n = s.shape[0]
    return s.reshape(n, N_PAIRS, 2, R_HD, R_HD).transpose(0, 1, 3, 2, 4).reshape(n, N_PAIRS, R_HD, LANES)


def _unpack_rwkv_state(s):
    n = s.shape[0]
    return s.reshape(n, N_PAIRS, R_HD, 2, R_HD).transpose(0, 1, 3, 2, 4).reshape(n, R_HEADS, R_HD, R_HD)


def _unpack_rwkv_blockdiag(s):
    n = s.shape[0]
    return jnp.stack([s[:, :, :R_HD, :R_HD], s[:, :, R_HD:, R_HD:]], axis=2).reshape(n, R_HEADS, R_HD, R_HD)


def _prep_layer_params(l, p):
    w_in = p['w_in'][l]
    c_dt_src = 3 * A_W + S_W + S_CONV_CH
    c_rw_src = c_dt_src + S_HEADS
    c_gate_src = c_rw_src + R_IN_W
    w_proj = jnp.concatenate([
        w_in[:, c_rw_src:c_gate_src],
        w_in[:, c_dt_src:c_rw_src], jnp.zeros((D_MODEL, C_Q - C_DT - S_HEADS), F32),
        w_in[:, :c_dt_src],
        w_in[:, c_gate_src:]], axis=1).astype(BF16)
    pad_lane = lambda v: jnp.pad(v, (0, LANES - v.shape[0])).reshape(1, LANES)
    zl = lambda r0, w: jnp.zeros((R_LORA, R_W), F32).at[r0:r0 + w.shape[0]].set(w).astype(BF16)
    return dict(
        norm1_g=p['norm1_g'][l].reshape(1, D_MODEL), w_proj=w_proj,
        b_gate=p['b_gate'][l].reshape(1, 3 * D_MODEL),
        w_pa=p['w_pa'][l].astype(BF16), w_pb=p['w_pb'][l].astype(BF16), w_pc=p['w_pc'][l].astype(BF16),
        w_o=p['w_o'][l].astype(BF16),
        ssm_conv_w=p['ssm_conv_w'][l], ssm_conv_b=p['ssm_conv_b'][l].reshape(1, S_CONV_CH),
        ssm_dt_bias=pad_lane(p['ssm_dt_bias'][l]), ssm_a_log=pad_lane(p['ssm_a_log'][l]),
        ssm_dvec=jnp.repeat(p['ssm_d'][l], S_HD).reshape(1, S_W),
        ssm_dtb_vec=jnp.repeat(p['ssm_dt_bias'][l], S_HD).reshape(1, S_W),
        ssm_alog_vec=jnp.repeat(p['ssm_a_log'][l], S_HD).reshape(1, S_W),
        ssm_norm_g=p['ssm_norm_g'][l].reshape(1, S_W),
        rw_mu=p['rw_mu'][l].reshape(1, R_IN_W), rw_w0=p['rw_w0'][l].reshape(1, R_W),
        rw_w2p=zl(0, p['rw_w2'][l]), rw_a0=p['rw_a0'][l].reshape(1, R_W),
        rw_a2p=zl(R_LORA_W, p['rw_a2'][l]), rw_g2p=zl(R_LORA_W + R_LORA_A, p['rw_g2'][l]),
        rw_kk=p['rw_kk'][l].reshape(1, R_W), rw_ka=p['rw_ka'][l].reshape(1, R_W),
        rw_rk=p['rw_rk'][l].reshape(1, R_W), rw_ln_g=p['rw_ln_g'][l].reshape(1, R_W),
        rw_ln_b=p['rw_ln_b'][l].reshape(1, R_W),
        norm2_g=p['norm2_g'][l].reshape(1, D_MODEL), w_up=p['w_up'][l].astype(BF16),
        w_down=p['w_down'][l].astype(BF16), ffn_conv_w=p['ffn_conv_w'][l],
        ffn_conv_b=p['ffn_conv_b'][l].reshape(1, D_FF))


def _row_tile(t):
    for tm in (512, 256, 128):
        if t % tm == 0:
            return tm
    raise ValueError(t)


def _ffn_tf():
    return D_FF // 2


def _prompt_layer(x, lp, bsz, seq, slopes_pair, gf, final_norm):
    t = bsz * seq
    tm = _row_tile(seq)
    tiles = seq // tm
    proj = _rms_matmul(x, lp['norm1_g'], lp['w_proj'], 1024 if t % 1024 == 0 else tm, 1024)
    proj3 = proj.reshape(bsz, seq, N_PROJ)
    oa = _moba_prompt(proj3, slopes_pair).reshape(t, A_W)
    ob, ssm_new = _ssd_prompt(proj, bsz, seq, lp['ssm_conv_w'], lp['ssm_conv_b'], lp['ssm_dt_bias'],
                              lp['ssm_a_log'], lp['ssm_dvec'], lp['ssm_norm_g'])
    r, w, k2, v, kk, kka, g = _rwkv_prep(
        proj, jnp.zeros((SUBLANES, R_IN_W), F32), lp['rw_mu'], lp['rw_w0'], lp['rw_w2p'], lp['rw_a0'],
        lp['rw_a2p'], lp['rw_g2p'], lp['rw_kk'], lp['rw_ka'], tm, 1, tiles)
    yr, rw_new = _rwkv_scan_prompt((r, w, k2, v, kk, kka), bsz, seq, LANES)
    x = _merge(x, proj, oa, ob, yr, r, k2, v, g, lp['b_gate'], lp['rw_ln_g'], lp['rw_ln_b'], lp['rw_rk'],
               lp['w_pa'], lp['w_pb'], lp['w_pc'], lp['w_o'], min(tm, 256))
    x, tail = _ffn(x, lp['norm2_g'], lp['w_up'], lp['w_down'], lp['ffn_conv_w'], lp['ffn_conv_b'],
                   jnp.zeros((SUBLANES, D_FF), F32), gf, tm, _ffn_tf(), 1, tiles, final_norm)
    k_new = proj3[:, :, C_K:C_K + A_W].reshape(bsz, seq, A_HEADS, A_HD)
    v_new = proj3[:, :, C_V:C_V + A_W].reshape(bsz, seq, A_HEADS, A_HD)
    ssm_conv_new = proj3[:, seq - (S_CONV - 1):, C_XBC:C_XBC + S_CONV_CH]
    shift_new = proj3[:, seq - 1:, C_RW:C_RW + R_IN_W]
    ffn_conv_new = tail.reshape(bsz, tiles, SUBLANES, D_FF)[:, tiles - 1, SUBLANES - (F_CONV - 1):]
    state = (k_new, v_new, ssm_new.reshape(bsz, S_HEADS, S_HD, S_STATE), ssm_conv_new,
             _unpack_rwkv_blockdiag(rw_new), shift_new, ffn_conv_new)
    return x, state


def _sample_layer(x, lp, dbsz, tnew, st, cache_k4, cache_v4, page_table, layer, gf, final_norm):
    t = tnew * dbsz
    ssm0, ssm_conv0, rwkv0, shift0, ffn_conv0 = st
    proj = _rms_matmul(x, lp['norm1_g'], lp['w_proj'], t, 1024)
    proj3 = proj.reshape(tnew, dbsz, N_PROJ)

    qkv = proj3[:, :, C_Q:C_Q + 3 * A_W].transpose(1, 0, 2)
    qkv8 = jnp.pad(qkv, ((0, 0), (0, SUBLANES - tnew), (0, 0)))
    oa8 = _moba_sample(qkv8[:, :, :A_W], qkv8[:, :, A_W:2 * A_W], qkv8[:, :, 2 * A_W:],
                       cache_k4, cache_v4, page_table, layer, tnew)
    oa = oa8[:, :tnew].transpose(1, 0, 2).reshape(t, A_W)

    h0 = ssm0.reshape(-1, dbsz, N_PAIRS, 2 * S_HD, S_STATE)
    ob3, ssm_new = _ssd_sample(proj3, ssm_conv0.transpose(1, 0, 2), h0, layer, lp['ssm_conv_w'],
                               lp['ssm_conv_b'], lp['ssm_dtb_vec'], lp['ssm_alog_vec'], lp['ssm_dvec'],
                               lp['ssm_norm_g'])
    ob = ob3.reshape(t, S_W)
    ssm_new = ssm_new.reshape(dbsz, S_HEADS, S_HD, S_STATE)

    r, w, k2, v, kk, kka, g = _rwkv_prep(
        proj, shift0.reshape(dbsz, R_IN_W), lp['rw_mu'], lp['rw_w0'], lp['rw_w2p'], lp['rw_a0'],
        lp['rw_a2p'], lp['rw_g2p'], lp['rw_kk'], lp['rw_ka'], t, dbsz, 1)
    to3 = lambda a: a.reshape(tnew, dbsz, R_W)
    yr3, rw_new = _rwkv_scan_sample(tuple(to3(a) for a in (r, w, k2, v, kk, kka)), _pack_rwkv_state(rwkv0))
    yr = yr3.reshape(t, R_W)

    x = _merge(x, proj, oa, ob, yr, r, k2, v, g, lp['b_gate'], lp['rw_ln_g'], lp['rw_ln_b'], lp['rw_rk'],
               lp['w_pa'], lp['w_pb'], lp['w_pc'], lp['w_o'], min(t, 256))
    pre = ffn_conv0.transpose(1, 0, 2).reshape((F_CONV - 1) * dbsz, D_FF)
    x, tail = _ffn(x, lp['norm2_g'], lp['w_up'], lp['w_down'], lp['ffn_conv_w'], lp['ffn_conv_b'],
                   pre, gf, t, _ffn_tf(), dbsz, 1, final_norm)

    k_new = qkv[:, :, A_W:2 * A_W].reshape(dbsz, tnew, A_HEADS, A_HD)
    v_new = qkv[:, :, 2 * A_W:].reshape(dbsz, tnew, A_HEADS, A_HD)
    ssm_conv_new = proj3[tnew - (S_CONV - 1):, :, C_XBC:C_XBC + S_CONV_CH].transpose(1, 0, 2)
    shift_new = proj3[tnew - 1:, :, C_RW:C_RW + R_IN_W].transpose(1, 0, 2)
    ffn_conv_new = tail.reshape(F_CONV - 1, dbsz, D_FF).transpose(1, 0, 2)
    state = (k_new, v_new, ssm_new, ssm_conv_new, _unpack_rwkv_state(rw_new), shift_new, ffn_conv_new)
    return x, state


def kernel(x_prompt, x_sample, cache_k, cache_v, state_ssm, state_ssm_conv, state_rwkv, state_rwkv_shift, state_ffn_conv, page_table, norm1_g, w_in, b_gate, w_pa, ssm_conv_w, ssm_conv_b, ssm_dt_bias, ssm_a_log, ssm_d, ssm_norm_g, w_pb, rw_mu, rw_w0, rw_w2, rw_a0, rw_a2, rw_g2, rw_kk, rw_ka, rw_rk, rw_ln_g, rw_ln_b, w_pc, w_o, norm2_g, w_up, ffn_conv_w, ffn_conv_b, w_down, norm_f_g):
    params = dict(norm1_g=norm1_g, w_in=w_in, b_gate=b_gate, w_pa=w_pa, ssm_conv_w=ssm_conv_w,
                  ssm_conv_b=ssm_conv_b, ssm_dt_bias=ssm_dt_bias, ssm_a_log=ssm_a_log, ssm_d=ssm_d,
                  ssm_norm_g=ssm_norm_g, w_pb=w_pb, rw_mu=rw_mu, rw_w0=rw_w0, rw_w2=rw_w2, rw_a0=rw_a0,
                  rw_a2=rw_a2, rw_g2=rw_g2, rw_kk=rw_kk, rw_ka=rw_ka, rw_rk=rw_rk, rw_ln_g=rw_ln_g,
                  rw_ln_b=rw_ln_b, w_pc=w_pc, w_o=w_o, norm2_g=norm2_g, w_up=w_up, ffn_conv_w=ffn_conv_w,
                  ffn_conv_b=ffn_conv_b, w_down=w_down)
    depth = w_in.shape[0]
    bsz, seq, _ = x_prompt.shape
    dbsz, tnew, _ = x_sample.shape
    head = jnp.arange(A_HEADS, dtype=F32) + 1.0
    slopes = jnp.exp2(-8.0 * head / A_HEADS)
    slopes_pair = jnp.repeat(slopes, A_HD).reshape(N_PAIRS, 1, LANES)
    cache_k4 = cache_k.transpose(0, 1, 3, 4, 2)
    cache_v4 = cache_v.transpose(0, 1, 3, 4, 2)
    gf = norm_f_g.reshape(1, D_MODEL)

    hp = x_prompt.reshape(bsz * seq, D_MODEL)
    hs = x_sample.transpose(1, 0, 2).reshape(tnew * dbsz, D_MODEL)
    new_p = [[] for _ in range(7)]
    new_s = [[] for _ in range(7)]
    for l in range(depth):
        lp = _prep_layer_params(l, params)
        last = l == depth - 1
        hp, sp = _prompt_layer(hp, lp, bsz, seq, slopes_pair, gf, last)
        st = (state_ssm, state_ssm_conv[l], state_rwkv[l], state_rwkv_shift[l], state_ffn_conv[l])
        hs, ss = _sample_layer(hs, lp, dbsz, tnew, st, cache_k4, cache_v4, page_table, l, gf, last)
        for j in range(7):
            new_p[j].append(sp[j])
            new_s[j].append(ss[j])
    y_prompt = hp.reshape(bsz, seq, D_MODEL)
    y_sample = hs.reshape(tnew, dbsz, D_MODEL).transpose(1, 0, 2)
    outs = [y_prompt, y_sample]
    for j in range(7):
        outs += [jnp.stack(new_p[j]), jnp.stack(new_s[j])]
    return tuple(outs)
```

```python
import functools
import math

import jax
import jax.numpy as jnp
from jax import lax
from jax.experimental import pallas as pl
from jax.experimental.pallas import tpu as pltpu

F32 = jnp.float32
BF16 = jnp.bfloat16

D_MODEL = 1024
A_HEADS = 8
A_HD = 64
A_W = A_HEADS * A_HD
MOBA_BLOCK = 256
MOBA_TOPK = 3
Q_BLOCK = MOBA_BLOCK
ATTN_SCALE = A_HD ** -0.5
S_HEADS = 8
S_HD = 64
S_W = S_HEADS * S_HD
S_GROUPS = 2
S_STATE = 128
S_CONV = 4
S_CONV_CH = S_W + 2 * S_GROUPS * S_STATE
SSD_CHUNK = 128
R_HEADS = 8
R_HD = 64
R_W = R_HEADS * R_HD
R_LORA_W = 64
R_LORA_A = 64
R_LORA_G = 128
R_LORA = R_LORA_W + R_LORA_A + R_LORA_G
R_IN_W = 3 * R_W + R_LORA
D_FF = ((8 * D_MODEL // 3 + 127) // 128) * 128
F_CONV = 3
NORM_EPS = 1e-6
GN_EPS = 64e-5
NEG_INF = -1e30

LANES = 128
SUBLANES = 8
HALF = 64
HALF_SHIFT = 6
SUBLANE_SHIFT = 3
N_PAIRS = 4

C_RW = 0
C_DT = R_IN_W
C_Q = 2048
C_K = C_Q + A_W
C_V = C_K + A_W
C_Z = C_V + A_W
C_XBC = C_Z + S_W
C_GATE = C_XBC + S_CONV_CH
N_PROJ = C_GATE + 3 * D_MODEL

VMEM_LIMIT = 56 * 1024 * 1024


def _cparams(sem):
    return pltpu.CompilerParams(dimension_semantics=sem, vmem_limit_bytes=VMEM_LIMIT)


def _dot(a, b):
    return jnp.dot(a.astype(BF16), b.astype(BF16), preferred_element_type=F32)


def _dot_nt(a, b):
    return lax.dot_general(a.astype(BF16), b.astype(BF16), (((1,), (1,)), ((), ())),
                           preferred_element_type=F32)


def _dot_tn(a, b):
    return lax.dot_general(a.astype(BF16), b.astype(BF16), (((0,), (0,)), ((), ())),
                           preferred_element_type=F32)


def _split_dot(x, w, passes, x_is_lhs=True):
    w = w.astype(BF16)
    acc = None
    rem = x
    for _ in range(passes):
        piece = rem.astype(BF16)
        term = (jnp.dot(piece, w, preferred_element_type=F32) if x_is_lhs
                else jnp.dot(w, piece, preferred_element_type=F32))
        acc = term if acc is None else acc + term
        rem = rem - piece.astype(F32)
    return acc


def _dot_nt_hi(a, b):
    return lax.dot_general(a, b, (((1,), (1,)), ((), ())), preferred_element_type=F32,
                           precision=lax.Precision.HIGHEST)


def _sigmoid(x):
    return 1.0 / (1.0 + jnp.exp(-x))


def _silu(x):
    return x * _sigmoid(x)


def _softplus(x):
    return jnp.maximum(x, 0.0) + jnp.log(1.0 + jnp.exp(-jnp.abs(x)))


def _head_ones(width, scale):
    r = lax.shift_right_logical(lax.broadcasted_iota(jnp.int32, (width, width), 0), HALF_SHIFT)
    c = lax.shift_right_logical(lax.broadcasted_iota(jnp.int32, (width, width), 1), HALF_SHIFT)
    return jnp.where(r == c, scale, 0.0).astype(F32)


def _rms_matmul_kernel(x_ref, g_ref, w_ref, o_ref, xn_ref):
    @pl.when(pl.program_id(1) == 0)
    def _():
        x = x_ref[...]
        ms = jnp.mean(x * x, axis=-1, keepdims=True)
        xn_ref[...] = (x * lax.rsqrt(ms + NORM_EPS) * g_ref[...]).astype(BF16)

    o_ref[...] = jnp.dot(xn_ref[...], w_ref[...], preferred_element_type=F32)


def _rms_matmul(x, g, w, tm, tn):
    t, d = x.shape
    n = w.shape[1]
    return pl.pallas_call(
        _rms_matmul_kernel,
        grid=(t // tm, n // tn),
        in_specs=[pl.BlockSpec((tm, d), lambda i, j: (i, 0)),
                  pl.BlockSpec((1, d), lambda i, j: (0, 0)),
                  pl.BlockSpec((d, tn), lambda i, j: (0, j))],
        out_specs=pl.BlockSpec((tm, tn), lambda i, j: (i, j)),
        out_shape=jax.ShapeDtypeStruct((t, n), F32),
        scratch_shapes=[pltpu.VMEM((tm, d), BF16)],
        compiler_params=_cparams(("parallel", "arbitrary")),
        name="in_proj",
    )(x, g, w)


def _topk_bias(gate, n_valid, axis=1):
    pos = lax.broadcasted_iota(jnp.int32, gate.shape, axis)
    pos_f = pos.astype(F32)
    gm = jnp.where(pos < n_valid, gate, NEG_INF)
    selected = jnp.zeros(gate.shape, jnp.bool_)
    for j in range(MOBA_TOPK):
        m = jnp.max(gm, axis=axis, keepdims=True)
        idx = jnp.min(jnp.where(gm == m, pos_f, 1e9), axis=axis, keepdims=True)
        hit = pos_f == idx
        selected = jnp.logical_or(selected, jnp.logical_and(hit, j < n_valid))
        gm = jnp.where(hit, -jnp.inf, gm)
    return jnp.where(selected, 0.0, NEG_INF).astype(F32)


def _bf16_pieces(x, n):
    out = []
    rem = x
    for _ in range(n):
        piece = rem.astype(BF16).astype(F32)
        out.append(piece)
        rem = rem - piece
    return out


def _moba_prompt_kernel(q_ref, k_ref, v_ref, sl_ref, o_ref, kf_ref, vth_ref, kmean_ref, sel_ref, acc_ref,
                        raw_a, raw_b, m_ref, knorm_ref, *, nb):
    i = pl.program_id(2)
    B = MOBA_BLOCK
    n_piece = 3
    lane = lax.broadcasted_iota(jnp.int32, (1, LANES), 1)
    halves = (lane < HALF, lane >= HALF)

    @pl.when(i == 0)
    def _():
        kmean_ref[...] = jnp.zeros_like(kmean_ref)
        key_local = jnp.bitwise_and(lax.broadcasted_iota(jnp.int32, (k_ref.shape[1], 1), 0), B - 1).astype(F32)
        extra = jnp.where(lane < n_piece, key_local, jnp.where(lane < 2 * n_piece, 1.0, 0.0))
        kf_ref[...] = jnp.concatenate([k_ref[0], extra], axis=1).astype(BF16)
        ones_rows = jnp.ones((SUBLANES, B), F32)
        knorm = [jnp.zeros((1, LANES), F32), jnp.zeros((1, LANES), F32)]
        for n in range(nb):
            kblk = k_ref[0, n * B:(n + 1) * B, :]
            kmean_ref[n:n + 1, :] = jnp.mean(kblk, axis=0, keepdims=True)
            vt = v_ref[0, n * B:(n + 1) * B, :].T
            for h in range(2):
                vth_ref[h, n] = jnp.concatenate([vt[h * HALF:(h + 1) * HALF], ones_rows], axis=0).astype(BF16)
                norm2 = jnp.sum(jnp.where(halves[h], kblk * kblk, 0.0), axis=-1, keepdims=True)
                knorm[h] = jnp.where(lane == n, jnp.max(norm2, axis=0, keepdims=True), knorm[h])
        for h in range(2):
            knorm_ref[h] = jnp.broadcast_to(knorm[h], (SUBLANES, LANES))

    own = i
    q = q_ref[0]
    log2e = 1.0 / math.log(2.0)
    slope2 = (sl_ref[0, :, 0:1] * log2e, sl_ref[0, :, HALF:HALF + 1] * log2e)
    kmean = kmean_ref[...]
    q_local = lax.broadcasted_iota(jnp.int32, (Q_BLOCK, 1), 0).astype(F32)
    q_rows = []
    for h in range(2):
        q_m = jnp.where(halves[h], q, 0.0)
        bias_t = _topk_bias(_dot_nt_hi(kmean, q_m), own, axis=0)
        for n in range(nb):
            sel_ref[h, n] = jnp.broadcast_to(bias_t[n:n + 1, :], (SUBLANES, Q_BLOCK))
        pieces = _bf16_pieces(slope2[h], n_piece) + _bf16_pieces(-slope2[h] * q_local, n_piece)
        extra = jnp.zeros((Q_BLOCK, LANES), F32)
        for j, piece in enumerate(pieces):
            extra = jnp.where(lane == j, piece, extra)
        q_rows.append(jnp.concatenate([q_m * (ATTN_SCALE * log2e), extra], axis=1))
    qf = jnp.concatenate(q_rows, axis=0).astype(BF16)

    rel = (lax.broadcasted_iota(jnp.int32, (B, Q_BLOCK), 0) - lax.broadcasted_iota(jnp.int32, (B, Q_BLOCK), 1))

    def scores(n):
        both = _dot_nt(kf_ref[pl.ds(pl.multiple_of(n * B, B), B), :], qf)
        return [both[:, h * Q_BLOCK:(h + 1) * Q_BLOCK] for h in range(2)]

    def put_scores(n, dst_ref):
        for h, s in enumerate(scores(n)):
            dst_ref[h] = s

    def row_of(ref, h):
        return ref[h][0:1, :]

    def put_row(ref, h, x):
        ref[h] = jnp.broadcast_to(x, (SUBLANES, Q_BLOCK))

    def attend(n, src_ref):
        shift = ((n - i) * B).astype(F32)
        ps, alphas = [], []
        for h in range(2):
            s = src_ref[h] + (sel_ref[h, n][0:1, :] + slope2[h] * shift)
            m0 = row_of(m_ref, h)
            m1 = jnp.maximum(m0, jnp.max(s, axis=0, keepdims=True))
            alphas.append(jnp.exp2(m0 - m1))
            ps.append(jnp.exp2(s - m1).astype(BF16))
            put_row(m_ref, h, m1)
        pv = [jnp.dot(vth_ref[h, n], ps[h], preferred_element_type=F32) for h in range(2)]
        for h in range(2):
            acc_ref[h] = alphas[h] * acc_ref[h] + pv[h]

    lane_f = lane.astype(F32)
    bound_c = ATTN_SCALE * log2e * 1.02
    skips = []
    for h in range(2):
        q_m = jnp.where(halves[h], q, 0.0)
        qn2 = jnp.max(jnp.sum(q_m * q_m, axis=-1, keepdims=True), axis=0, keepdims=True)
        ub = jnp.sqrt(qn2 * knorm_ref[h][0:1, :]) * bound_c
        ub_own = jnp.sum(jnp.where(lane == own, ub, 0.0), axis=-1, keepdims=True)
        far = slope2[h] * ((lane_f - own.astype(F32)) * B + (B - 1))
        skips.append(ub + far < -ub_own - 160.0)
    keep = jnp.logical_and(lane < own, jnp.logical_not(jnp.logical_and(skips[0], skips[1])))
    first = jnp.min(jnp.where(keep, lane_f, own.astype(F32))).astype(jnp.int32)
    n_visit = own - first

    put_scores(jnp.minimum(first, nb - 1), raw_a)

    own_raw = scores(own)
    own_p = []
    for h in range(2):
        s = jnp.where(rel <= 0, own_raw[h], NEG_INF)
        m = jnp.max(s, axis=0, keepdims=True)
        put_row(m_ref, h, m)
        own_p.append(jnp.exp2(s - m).astype(BF16))
    for h in range(2):
        acc_ref[h] = jnp.dot(vth_ref[h, own], own_p[h], preferred_element_type=F32)

    def body(j, carry):
        n0 = first + 2 * j
        put_scores(n0 + 1, raw_b)
        attend(n0, raw_a)
        put_scores(jnp.minimum(n0 + 2, nb - 1), raw_a)
        attend(n0 + 1, raw_b)
        return carry

    lax.fori_loop(0, n_visit // 2, body, 0)

    @pl.when(n_visit % 2 == 1)
    def _():
        attend(own - 1, raw_a)

    out_t = jnp.concatenate([acc_ref[h][0:HALF] / acc_ref[h][HALF:HALF + 1] for h in range(2)], axis=0)
    o_ref[0] = out_t.T


def _moba_prompt(proj3, slopes_pair):
    bsz, seq, _ = proj3.shape
    assert seq % MOBA_BLOCK == 0 and seq // MOBA_BLOCK >= MOBA_TOPK
    nb = seq // MOBA_BLOCK
    nbp = -(-nb // SUBLANES) * SUBLANES
    nq = seq // Q_BLOCK
    qc, kc, vc = C_Q // LANES, C_K // LANES, C_V // LANES
    return pl.pallas_call(
        functools.partial(_moba_prompt_kernel, nb=nb),
        grid=(bsz, N_PAIRS, nq),
        in_specs=[pl.BlockSpec((1, Q_BLOCK, LANES), lambda b, p, i: (b, i, qc + p)),
                  pl.BlockSpec((1, seq, LANES), lambda b, p, i: (b, 0, kc + p)),
                  pl.BlockSpec((1, seq, LANES), lambda b, p, i: (b, 0, vc + p)),
                  pl.BlockSpec((1, 1, LANES), lambda b, p, i: (p, 0, 0))],
        out_specs=pl.BlockSpec((1, Q_BLOCK, LANES), lambda b, p, i: (b, i, p)),
        out_shape=jax.ShapeDtypeStruct((bsz, seq, A_W), F32),
        scratch_shapes=[pltpu.VMEM((seq, 2 * LANES), BF16),
                        pltpu.VMEM((2, nb, HALF + SUBLANES, MOBA_BLOCK), BF16),
                        pltpu.VMEM((nbp, LANES), F32), pltpu.VMEM((2, nb, SUBLANES, Q_BLOCK), F32),
                        pltpu.VMEM((2, HALF + SUBLANES, Q_BLOCK), F32),
                        pltpu.VMEM((2, MOBA_BLOCK, Q_BLOCK), F32), pltpu.VMEM((2, MOBA_BLOCK, Q_BLOCK), F32),
                        pltpu.VMEM((2, SUBLANES, Q_BLOCK), F32), pltpu.VMEM((2, SUBLANES, LANES), F32)],
        compiler_params=_cparams(("parallel", "parallel", "arbitrary")),
        name="moba_prompt",
    )(proj3, proj3, proj3, slopes_pair)


def _moba_sample_kernel(pt_ref, q_ref, kn_ref, vn_ref, *refs, n_blk, ppb, page, tnew):
    del pt_ref
    n_pages = n_blk * ppb
    k_refs, v_refs = refs[:n_pages], refs[n_pages:2 * n_pages]
    o_ref, m_ref, l_ref, g_ref, acc_ref = refs[2 * n_pages:]
    past = n_blk * MOBA_BLOCK
    rows = A_HEADS * SUBLANES
    lane_head = lax.shift_right_logical(lax.broadcasted_iota(jnp.int32, (SUBLANES, A_W), 1), HALF_SHIFT)
    q8 = q_ref[0] * ATTN_SCALE
    qbd = jnp.concatenate([jnp.where(lane_head == h, q8, 0.0) for h in range(A_HEADS)], axis=0)
    q_hi = qbd.astype(BF16)
    q_lo = (qbd - q_hi.astype(F32)).astype(BF16)
    r = lax.broadcasted_iota(jnp.int32, (rows, 1), 0)
    tok = jnp.bitwise_and(r, SUBLANES - 1).astype(F32)
    slope = jnp.exp2(-(8.0 / A_HEADS) * (lax.shift_right_logical(r, SUBLANE_SHIFT) + 1).astype(F32))
    key = lax.broadcasted_iota(jnp.int32, (1, MOBA_BLOCK), 1).astype(F32)

    def block_t(refs_, n):
        return jnp.concatenate([refs_[j][0, 0].reshape(A_W, page) for j in range(n * ppb, (n + 1) * ppb)],
                               axis=1).astype(BF16)

    raws, fixes = [], []
    for n in range(n_blk):
        kt = block_t(k_refs, n)
        raws.append(jnp.dot(q_hi, kt, preferred_element_type=F32))
        fixes.append(jnp.dot(q_lo, kt, preferred_element_type=F32))
    es = []
    for n in range(n_blk):
        g_ref[n] = jnp.broadcast_to(jnp.sum(raws[n] + fixes[n], axis=-1, keepdims=True) * (1.0 / MOBA_BLOCK),
                                    (rows, LANES))
        s = raws[n] - slope * ((past - n * MOBA_BLOCK + tok) - key)
        m = jnp.max(s, axis=-1, keepdims=True)
        e = jnp.exp(s - m)
        m_ref[n] = jnp.broadcast_to(m, (rows, LANES))
        l_ref[n] = jnp.broadcast_to(jnp.sum(e, axis=-1, keepdims=True), (rows, LANES))
        es.append(e.astype(BF16))
    for n in range(n_blk):
        acc_ref[n] = lax.dot_general(es[n], block_t(v_refs, n), (((1,), (1,)), ((), ())),
                                     preferred_element_type=F32)

    lane = lax.broadcasted_iota(jnp.int32, (rows, LANES), 1)
    gate = jnp.zeros((rows, LANES), F32)
    for j in range(n_blk):
        gate = jnp.where(lane == j, g_ref[j], gate)
    bias = _topk_bias(gate, n_blk)

    kn = kn_ref[0]
    vn = vn_ref[0]
    s_own = []
    for j in range(tnew):
        sj = jnp.sum(qbd * kn[j:j + 1, :], axis=-1, keepdims=True) - slope * (tok - j)
        s_own.append(jnp.where(tok >= j, sj, NEG_INF))
    mx = s_own[0]
    for j in range(1, tnew):
        mx = jnp.maximum(mx, s_own[j])
    mb = []
    for j in range(n_blk):
        mj = m_ref[j][:, 0:1] + bias[:, j:j + 1]
        mb.append(mj)
        mx = jnp.maximum(mx, mj)
    lsum = jnp.zeros((rows, 1), F32)
    acc = jnp.zeros((rows, A_W), F32)
    for j in range(tnew):
        w = jnp.exp(s_own[j] - mx)
        lsum = lsum + w
        acc = acc + w * vn[j:j + 1, :]
    for j in range(n_blk):
        w = jnp.exp(mb[j] - mx)
        lsum = lsum + w * l_ref[j][:, 0:1]
        acc = acc + w * acc_ref[j]
    out = acc / lsum
    o8 = jnp.zeros((SUBLANES, A_W), F32)
    for h in range(A_HEADS):
        o8 = o8 + jnp.where(lane_head == h, out[h * SUBLANES:(h + 1) * SUBLANES, :], 0.0)
    o_ref[0] = o8


def _moba_sample(q8, k8, v8, cache_kt, cache_vt, page_table, layer, tnew):
    dbsz = q8.shape[0]
    n_pages = page_table.shape[1]
    page = cache_kt.shape[4]
    assert MOBA_BLOCK % page == 0 and (n_pages * page) % MOBA_BLOCK == 0 and page % LANES == 0
    ppb = MOBA_BLOCK // page
    n_blk = n_pages // ppb
    assert n_blk >= MOBA_TOPK and tnew <= SUBLANES
    rows = A_HEADS * SUBLANES
    tok_spec = pl.BlockSpec((1, SUBLANES, A_W), lambda b, pt: (b, 0, 0))

    def page_spec(j):
        return pl.BlockSpec((1, 1, A_HEADS, A_HD, page), lambda b, pt: (layer, pt[b * n_pages + j], 0, 0, 0))

    pages = [page_spec(j) for j in range(n_pages)]
    grid_spec = pltpu.PrefetchScalarGridSpec(
        num_scalar_prefetch=1,
        grid=(dbsz,),
        in_specs=[tok_spec, tok_spec, tok_spec] + pages + pages,
        out_specs=tok_spec,
        scratch_shapes=[pltpu.VMEM((n_blk, rows, LANES), F32), pltpu.VMEM((n_blk, rows, LANES), F32),
                        pltpu.VMEM((n_blk, rows, LANES), F32), pltpu.VMEM((n_blk, rows, A_W), F32)],
    )
    return pl.pallas_call(
        functools.partial(_moba_sample_kernel, n_blk=n_blk, ppb=ppb, page=page, tnew=tnew),
        grid_spec=grid_spec,
        out_shape=jax.ShapeDtypeStruct((dbsz, SUBLANES, A_W), F32),
        compiler_params=_cparams(("parallel",)),
        name="moba_sample",
    )(page_table.reshape(-1), q8, k8, v8, *([cache_kt] * n_pages), *([cache_vt] * n_pages))


def _ssd_prompt_kernel(z_ref, xbc_ref, dt_ref, cw_ref, cb_ref, dtb_ref, alog_ref, dvec_ref, ng_ref,
                       o_ref, hout_ref, ext_ref, h_ref, *, L):
    c = pl.program_id(1)
    P = SUBLANES

    @pl.when(c == 0)
    def _():
        ext_ref[0:P, :] = jnp.zeros((P, S_CONV_CH), F32)
        h_ref[...] = jnp.zeros_like(h_ref)

    @pl.when(c > 0)
    def _():
        ext_ref[0:P, :] = ext_ref[L:L + P, :]

    ext_ref[P:P + L, :] = xbc_ref[...]
    acc = cb_ref[...] + cw_ref[S_CONV - 1:S_CONV, :] * ext_ref[P:P + L, :]
    for d in range(1, S_CONV):
        acc = acc + cw_ref[S_CONV - 1 - d:S_CONV - d, :] * ext_ref[P - d:P - d + L, :]
    xbc = _silu(acc)
    xs = xbc[:, :S_W]
    bm = xbc[:, S_W:S_W + S_GROUPS * S_STATE]
    cm = xbc[:, S_W + S_GROUPS * S_STATE:]

    lane = lax.broadcasted_iota(jnp.int32, (1, LANES), 1)
    is_a = lane < HALF
    dt = jnp.where(lane < S_HEADS, _softplus(dt_ref[...] + dtb_ref[...]), 0.0)
    da = dt * (-jnp.exp(alog_ref[...]))
    rr = lax.broadcasted_iota(jnp.int32, (L, L), 0)
    cc = lax.broadcasted_iota(jnp.int32, (L, L), 1)
    causal = rr >= cc
    cum = _split_dot(da, jnp.where(causal, 1.0, 0.0), 3, x_is_lhs=False)
    cum_t = cum.T
    dt_t = dt.T
    row_a = lax.broadcasted_iota(jnp.int32, (2 * S_HD, 1), 0) < S_HD

    group_of = [(2 * pr * S_GROUPS) // S_HEADS for pr in range(N_PAIRS)]
    bgs = [bm[:, g * S_STATE:(g + 1) * S_STATE] for g in range(S_GROUPS)]
    cgs = [cm[:, g * S_STATE:(g + 1) * S_STATE] for g in range(S_GROUPS)]
    cb_g = [_dot_nt(cgs[g], bgs[g]) for g in range(S_GROUPS)]
    xs_ps = [xs[:, pr * LANES:(pr + 1) * LANES] for pr in range(N_PAIRS)]
    hps = [h_ref[pr] for pr in range(N_PAIRS)]
    halves = (is_a, jnp.logical_not(is_a))
    cum_c = [cum[:, h:h + 1] for h in range(S_HEADS)]
    last = [cum[L - 1:L, h:h + 1] for h in range(S_HEADS)]
    wts = []
    for h in range(S_HEADS):
        seg = cum_c[h] - cum_t[h:h + 1, :]
        dec = jnp.exp(jnp.where(causal, seg, -jnp.inf))
        wts.append(cb_g[group_of[h // 2]] * dec * dt_t[h:h + 1, :])
    intra = [_dot(wts[h], jnp.where(halves[h % 2], xs_ps[h // 2], 0.0)) for h in range(S_HEADS)]
    inter = [_dot_nt(cgs[group_of[pr]], hps[pr]) for pr in range(N_PAIRS)]
    ys = [intra[2 * pr] + intra[2 * pr + 1]
          + inter[pr] * jnp.where(is_a, jnp.exp(cum_c[2 * pr]), jnp.exp(cum_c[2 * pr + 1])) for pr in range(N_PAIRS)]
    tes = [jnp.where(is_a, jnp.exp(last[2 * pr] - cum_c[2 * pr]) * dt[:, 2 * pr:2 * pr + 1],
                     jnp.exp(last[2 * pr + 1] - cum_c[2 * pr + 1]) * dt[:, 2 * pr + 1:2 * pr + 2])
           for pr in range(N_PAIRS)]
    sts = [_dot_tn(xs_ps[pr] * tes[pr], bgs[group_of[pr]]) for pr in range(N_PAIRS)]
    for pr in range(N_PAIRS):
        h_ref[pr] = hps[pr] * jnp.where(row_a, jnp.exp(last[2 * pr]), jnp.exp(last[2 * pr + 1])) + sts[pr]
    y = jnp.concatenate(ys, axis=1) + dvec_ref[...] * xs
    yz = y * _silu(z_ref[...])
    gw = S_W // S_GROUPS
    for g in range(S_GROUPS):
        part = yz[:, g * gw:(g + 1) * gw]
        ms = jnp.mean(part * part, axis=-1, keepdims=True)
        o_ref[:, g * gw:(g + 1) * gw] = part * lax.rsqrt(ms + NORM_EPS) * ng_ref[:, g * gw:(g + 1) * gw]

    @pl.when(c == pl.num_programs(1) - 1)
    def _():
        hout_ref[0] = h_ref[...]


def _ssd_prompt(proj, bsz, seq, cw, cb, dtb, alog, dvec, ng):
    L = SSD_CHUNK
    nc = seq // L
    const = lambda shape: pl.BlockSpec(shape, lambda b, c: (0,) * len(shape))
    return pl.pallas_call(
        functools.partial(_ssd_prompt_kernel, L=L),
        grid=(bsz, nc),
        in_specs=[pl.BlockSpec((L, S_W), lambda b, c: (b * nc + c, C_Z // S_W)),
                  pl.BlockSpec((L, S_CONV_CH), lambda b, c: (b * nc + c, C_XBC // S_CONV_CH)),
                  pl.BlockSpec((L, LANES), lambda b, c: (b * nc + c, C_DT // LANES)),
                  const((S_CONV, S_CONV_CH)), const((1, S_CONV_CH)), const((1, LANES)), const((1, LANES)),
                  const((1, S_W)), const((1, S_W))],
        out_specs=[pl.BlockSpec((L, S_W), lambda b, c: (b * nc + c, 0)),
                   pl.BlockSpec((1, N_PAIRS, 2 * S_HD, S_STATE), lambda b, c: (b, 0, 0, 0))],
        out_shape=[jax.ShapeDtypeStruct((bsz * seq, S_W), F32),
                   jax.ShapeDtypeStruct((bsz, N_PAIRS, 2 * S_HD, S_STATE), F32)],
        scratch_shapes=[pltpu.VMEM((L + 2 * SUBLANES, S_CONV_CH), F32),
                        pltpu.VMEM((N_PAIRS, 2 * S_HD, S_STATE), F32)],
        compiler_params=_cparams(("parallel", "arbitrary")),
        name="ssd_prompt",
    )(proj, proj, proj, cw, cb, dtb, alog, dvec, ng)


def _ssd_sample_kernel(z_ref, xbc_ref, dt_ref, pre_ref, h0_ref, cw_ref, cb_ref, dtbv_ref, alogv_ref,
                       dvec_ref, ng_ref, o_ref, hout_ref, *, T, nb):
    rows2 = 2 * S_HD
    eye = (lax.broadcasted_iota(jnp.int32, (rows2, LANES), 0)
           == lax.broadcasted_iota(jnp.int32, (rows2, LANES), 1)).astype(F32)
    eye_all = jnp.concatenate([eye] * nb, axis=0)
    ones = jnp.ones((LANES, LANES), F32)
    spread = (lax.broadcasted_iota(jnp.int32, (LANES, S_W), 0)
              == lax.shift_right_logical(lax.broadcasted_iota(jnp.int32, (LANES, S_W), 1), HALF_SHIFT)).astype(F32)
    a_vec = -jnp.exp(alogv_ref[...])
    gw = S_W // S_GROUPS

    def as_columns(x, passes):
        return _split_dot(eye_all * _repeat_rows(x, rows2), ones, passes)

    up = [pre_ref[j] for j in range(S_CONV - 1)] + [xbc_ref[t] for t in range(T)]
    hs = [h0_ref[0, :, pr].reshape(nb * rows2, S_STATE) for pr in range(N_PAIRS)]
    for t in range(T):
        acc = cb_ref[...] + cw_ref[0:1, :] * up[t]
        for j in range(1, S_CONV):
            acc = acc + cw_ref[j:j + 1, :] * up[t + j]
        xc = _silu(acc)
        xs = xc[:, :S_W]
        dt = _softplus(_split_dot(dt_ref[t], spread, 3) + dtbv_ref[...])
        decay = jnp.exp(dt * a_vec)
        xdt = xs * dt
        ys = []
        for pr in range(N_PAIRS):
            g = (2 * pr * S_GROUPS) // S_HEADS
            sl = slice(pr * LANES, (pr + 1) * LANES)
            b_rows = _repeat_rows(xc[:, S_W + g * S_STATE:S_W + (g + 1) * S_STATE], rows2)
            c_rows = _repeat_rows(xc[:, S_W + (S_GROUPS + g) * S_STATE:S_W + (S_GROUPS + g + 1) * S_STATE], rows2)
            hs[pr] = hs[pr] * as_columns(decay[:, sl], 3) + as_columns(xdt[:, sl], 2) * b_rows
            y_col = _split_dot(hs[pr] * c_rows, ones, 2)
            ys.append(jnp.sum((eye_all * y_col).reshape(nb, rows2, LANES), axis=1))
        y = jnp.concatenate(ys, axis=1) + dvec_ref[...] * xs
        yz = y * _silu(z_ref[t])
        for g in range(S_GROUPS):
            part = yz[:, g * gw:(g + 1) * gw]
            ms = jnp.mean(part * part, axis=-1, keepdims=True)
            o_ref[t, :, g * gw:(g + 1) * gw] = part * lax.rsqrt(ms + NORM_EPS) * ng_ref[:, g * gw:(g + 1) * gw]
    for pr in range(N_PAIRS):
        hout_ref[:, pr] = hs[pr].reshape(nb, rows2, S_STATE)


def _ssd_sample(proj3, pre3, h0, layer, cw, cb, dtbv, alogv, dvec, ng, nb=8):
    T, dbsz = proj3.shape[:2]
    const = lambda shape: pl.BlockSpec(shape, lambda j: (0,) * len(shape))
    hspec = pl.BlockSpec((nb, N_PAIRS, 2 * S_HD, S_STATE), lambda j: (j, 0, 0, 0))
    h0spec = pl.BlockSpec((1, nb, N_PAIRS, 2 * S_HD, S_STATE), lambda j: (layer, j, 0, 0, 0))
    return pl.pallas_call(
        functools.partial(_ssd_sample_kernel, T=T, nb=nb),
        grid=(dbsz // nb,),
        in_specs=[pl.BlockSpec((T, nb, S_W), lambda j: (0, j, C_Z // S_W)),
                  pl.BlockSpec((T, nb, S_CONV_CH), lambda j: (0, j, C_XBC // S_CONV_CH)),
                  pl.BlockSpec((T, nb, LANES), lambda j: (0, j, C_DT // LANES)),
                  pl.BlockSpec((S_CONV - 1, nb, S_CONV_CH), lambda j: (0, j, 0)),
                  h0spec,
                  const((S_CONV, S_CONV_CH)), const((1, S_CONV_CH)), const((1, S_W)), const((1, S_W)),
                  const((1, S_W)), const((1, S_W))],
        out_specs=[pl.BlockSpec((T, nb, S_W), lambda j: (0, j, 0)), hspec],
        out_shape=[jax.ShapeDtypeStruct((T, dbsz, S_W), F32),
                   jax.ShapeDtypeStruct((dbsz, N_PAIRS, 2 * S_HD, S_STATE), F32)],
        compiler_params=_cparams(("parallel",)),
        name="ssd_sample",
    )(proj3, proj3, proj3, pre3, h0, cw, cb, dtbv, alogv, dvec, ng)


def _rwkv_prep_kernel(u_ref, pre_ref, mu_ref, w0_ref, w2_ref, a0_ref, a2_ref, g2_ref, kk_ref, ka_ref,
                      r_out, w_out, k_out, v_out, kk_out, kka_out, g_out, ext_ref,
                      *, tm, P, stride, tiles_per_seq):
    i = pl.program_id(0)
    first = (i % tiles_per_seq) == 0

    @pl.when(first)
    def _():
        ext_ref[0:P, :] = pre_ref[...]

    @pl.when(jnp.logical_not(first))
    def _():
        ext_ref[0:P, :] = ext_ref[tm:tm + P, :]

    u = u_ref[...]
    ext_ref[P:P + tm, :] = u
    prev = ext_ref[P - stride:P - stride + tm, :]
    x = u + (prev - u) * mu_ref[...]
    r = x[:, 0:R_W]
    kr = x[:, R_W:2 * R_W]
    vr = x[:, 2 * R_W:3 * R_W]
    xl = x[:, 3 * R_W:]
    w_log = w0_ref[...] + _dot(jnp.tanh(xl), w2_ref[...])
    log_decay = -jnp.exp(-_softplus(-w_log) - 0.5)
    a = _sigmoid(a0_ref[...] + _dot(xl, a2_ref[...]))
    g = _dot(_sigmoid(xl), g2_ref[...])
    kk = kr * kk_ref[...]
    ss = _split_dot(kk * kk, _head_ones(R_W, 1.0), 2)
    kk = kk / jnp.maximum(jnp.sqrt(ss), 1e-12)
    r_out[...] = r
    w_out[...] = log_decay
    k_out[...] = kr * (1.0 + (a - 1.0) * ka_ref[...])
    v_out[...] = vr
    kk_out[...] = kk
    kka_out[...] = kk * a
    g_out[...] = g


def _rwkv_prep(proj, pre, mu, w0, w2p, a0, a2p, g2p, kkw, kaw, tm, stride, tiles_per_seq):
    t = proj.shape[0]
    P = pre.shape[0]
    const = lambda shape: pl.BlockSpec(shape, lambda i: (0,) * len(shape))
    outs = pl.pallas_call(
        functools.partial(_rwkv_prep_kernel, tm=tm, P=P, stride=stride, tiles_per_seq=tiles_per_seq),
        grid=(t // tm,),
        in_specs=[pl.BlockSpec((tm, R_IN_W), lambda i: (i, 0)),
                  const((P, R_IN_W)), const((1, R_IN_W)), const((1, R_W)), const((R_LORA, R_W)),
                  const((1, R_W)), const((R_LORA, R_W)), const((R_LORA, R_W)), const((1, R_W)), const((1, R_W))],
        out_specs=[pl.BlockSpec((tm, R_W), lambda i: (i, 0))] * 7,
        out_shape=[jax.ShapeDtypeStruct((t, R_W), F32)] * 7,
        scratch_shapes=[pltpu.VMEM((tm + 2 * P, R_IN_W), F32)],
        compiler_params=_cparams(("arbitrary",)),
        name="rwkv_prep",
    )(proj, pre, mu, w0, w2p, a0, a2p, g2p, kkw, kaw)
    return outs


def _pair_consts():
    lane = lax.broadcasted_iota(jnp.int32, (R_HD, LANES), 1)
    row = lax.broadcasted_iota(jnp.int32, (R_HD, LANES), 0)
    is_a = lane < HALF
    eye2 = (jnp.bitwise_and(lane, HALF - 1) == row).astype(F32)
    return is_a, eye2


def _pair_sum(x, is_a):
    sa = jnp.sum(jnp.where(is_a, x, 0.0), axis=-1, keepdims=True)
    sb = jnp.sum(jnp.where(is_a, 0.0, x), axis=-1, keepdims=True)
    return jnp.where(is_a, sa, sb)


def _rwkv_step(s, rr, ww, kr, vv, kk, kka, is_a, eye2):
    v_col = _pair_sum(eye2 * vv, is_a)
    sk = _pair_sum(s * kk, is_a)
    s = s * ww - sk * kka + v_col * kr
    y_col = _pair_sum(s * rr, is_a)
    return s, jnp.sum(eye2 * y_col, axis=0, keepdims=True)


def _unit_lower_inverses(ns):
    L = ns[0].shape[0]
    eye = (lax.broadcasted_iota(jnp.int32, (L, L), 0) == lax.broadcasted_iota(jnp.int32, (L, L), 1)).astype(F32)
    ts = [eye + n for n in ns]
    pws = [_dot(n, n) for n in ns]
    for _ in range(int(math.log2(L)) - 2):
        both = [_dot(jnp.concatenate([t, p], axis=0), p) for t, p in zip(ts, pws)]
        ts = [t + b[:L] for t, b in zip(ts, both)]
        pws = [b[L:] for b in both]
    return [t + _dot(t, p) for t, p in zip(ts, pws)]


def _rwkv_chunk_prompt_kernel(r_ref, lw_ref, k_ref, v_ref, kk_ref, kka_ref, y_ref, sout_ref, s_ref, *, L):
    c = pl.program_id(1)

    @pl.when(c == 0)
    def _():
        s_ref[...] = jnp.zeros_like(s_ref)

    lane = lax.broadcasted_iota(jnp.int32, (1, LANES), 1)
    is_a = lane < HALF
    rr = lax.broadcasted_iota(jnp.int32, (L, L), 0)
    cc = lax.broadcasted_iota(jnp.int32, (L, L), 1)
    incl = rr >= cc
    strict = rr > cc
    tri = jnp.where(incl, 1.0, 0.0).astype(F32)
    r2 = lax.broadcasted_iota(jnp.int32, (LANES, LANES), 0) < HALF
    c2 = lax.broadcasted_iota(jnp.int32, (LANES, LANES), 1) < HALF
    same_head = r2 == c2

    r, lw, k, v, kk, kka = (ref[...] for ref in (r_ref, lw_ref, k_ref, v_ref, kk_ref, kka_ref))
    cum = _split_dot(lw, tri, 3, x_is_lhs=False)
    last = cum[L - 1:L, :]
    inv_p = jnp.exp(-cum)
    to_end = jnp.exp(last - cum)
    b_t = kk * jnp.exp(cum - lw)
    a_t = -kka * inv_p
    k_t = k * inv_p
    r_t = r * jnp.exp(cum)
    a_end = -kka * to_end
    k_end = k * to_end
    decay_end = jnp.exp(last)
    pairs = [slice(pr * LANES, (pr + 1) * LANES) for pr in range(N_PAIRS)]
    halves = (is_a, jnp.logical_not(is_a))
    s0 = [s_ref[pr] for pr in range(N_PAIRS)]

    ns, mks, rak = [], [], []
    for sl in pairs:
        cols = jnp.concatenate([a_t[:, sl], k_t[:, sl]], axis=0)
        for half in halves:
            lhs = jnp.concatenate([jnp.where(half, b_t[:, sl], 0.0), jnp.where(half, r_t[:, sl], 0.0)], axis=0)
            g4 = _dot_nt(lhs, cols)
            ns.append(jnp.where(strict, g4[:L, :L], 0.0))
            mks.append(jnp.where(strict, g4[:L, L:], 0.0))
            rak.append(jnp.concatenate([jnp.where(incl, g4[L:, :L], 0.0), jnp.where(incl, g4[L:, L:], 0.0)], axis=1))
    ts = _unit_lower_inverses(ns)
    from_state = [_dot_nt(jnp.concatenate([b_t[:, sl], r_t[:, sl]], axis=0), s0[pr]) for pr, sl in enumerate(pairs)]
    rhs = []
    for pr, sl in enumerate(pairs):
        mv = _dot(jnp.concatenate([mks[2 * pr], mks[2 * pr + 1]], axis=0), v[:, sl])
        rhs.append(from_state[pr][:L] + jnp.where(is_a, mv[:L], mv[L:]))
    us = []
    for pr in range(N_PAIRS):
        tu = _dot(jnp.concatenate([ts[2 * pr], ts[2 * pr + 1]], axis=0), rhs[pr])
        us.append(jnp.where(is_a, tu[:L], tu[L:]))
    for pr, sl in enumerate(pairs):
        uv = jnp.concatenate([us[pr], v[:, sl]], axis=0)
        y_ref[:, sl] = from_state[pr][L:] + jnp.where(is_a, _dot(rak[2 * pr], uv), _dot(rak[2 * pr + 1], uv))
    for pr, sl in enumerate(pairs):
        upd = _dot_tn(jnp.concatenate([us[pr], v[:, sl]], axis=0),
                      jnp.concatenate([a_end[:, sl], k_end[:, sl]], axis=0))
        s_ref[pr] = s0[pr] * decay_end[:, sl] + jnp.where(same_head, upd, 0.0)

    @pl.when(c == pl.num_programs(1) - 1)
    def _():
        sout_ref[0] = s_ref[...]


def _rwkv_scan_prompt(seqs, bsz, seq, L):
    nc = seq // L
    spec = pl.BlockSpec((L, R_W), lambda b, c: (b * nc + c, 0))
    return pl.pallas_call(
        functools.partial(_rwkv_chunk_prompt_kernel, L=L),
        grid=(bsz, nc),
        in_specs=[spec] * 6,
        out_specs=[spec, pl.BlockSpec((1, N_PAIRS, LANES, LANES), lambda b, c: (b, 0, 0, 0))],
        out_shape=[jax.ShapeDtypeStruct((bsz * seq, R_W), F32),
                   jax.ShapeDtypeStruct((bsz, N_PAIRS, LANES, LANES), F32)],
        scratch_shapes=[pltpu.VMEM((N_PAIRS, LANES, LANES), F32)],
        compiler_params=_cparams(("parallel", "arbitrary")),
        name="rwkv_scan_prompt",
    )(*seqs)


def _repeat_rows(x, reps):
    return jnp.concatenate([jnp.broadcast_to(x[b:b + 1, :], (reps, x.shape[1])) for b in range(x.shape[0])], axis=0)


def _rwkv_scan_sample_kernel(r_ref, w_ref, k_ref, v_ref, kk_ref, kka_ref, s0_ref, y_ref, sout_ref, *, T, nb):
    _, eye2 = _pair_consts()
    eye_all = jnp.concatenate([eye2] * nb, axis=0)
    ones_bd = _head_ones(LANES, 1.0)

    def head_sums(x):
        return _split_dot(x, ones_bd, 2)

    states = [s0_ref[:, pr].reshape(nb * R_HD, LANES) for pr in range(N_PAIRS)]
    for t in range(T):
        for pr in range(N_PAIRS):
            sl = slice(pr * LANES, (pr + 1) * LANES)
            rr, ww, kr, vv, kk, kka = (_repeat_rows(x, R_HD) for x in (
                r_ref[t, :, sl], jnp.exp(w_ref[t, :, sl]), k_ref[t, :, sl], v_ref[t, :, sl],
                kk_ref[t, :, sl], kka_ref[t, :, sl]))
            v_col = head_sums(eye_all * vv)
            s = states[pr]
            s = s * ww - head_sums(s * kk) * kka + v_col * kr
            states[pr] = s
            y_col = head_sums(s * rr)
            y_ref[t, :, sl] = jnp.sum((eye_all * y_col).reshape(nb, R_HD, LANES), axis=1)
    for pr in range(N_PAIRS):
        sout_ref[:, pr] = states[pr].reshape(nb, R_HD, LANES)


def _rwkv_scan_sample(seqs3, s0, nb=16):
    T, dbsz = seqs3[0].shape[:2]
    spec = pl.BlockSpec((T, nb, R_W), lambda j: (0, j, 0))
    sspec = pl.BlockSpec((nb, N_PAIRS, R_HD, LANES), lambda j: (j, 0, 0, 0))
    return pl.pallas_call(
        functools.partial(_rwkv_scan_sample_kernel, T=T, nb=nb),
        grid=(dbsz // nb,),
        in_specs=[spec] * 6 + [sspec],
        out_specs=[spec, sspec],
        out_shape=[jax.ShapeDtypeStruct((T, dbsz, R_W), F32),
                   jax.ShapeDtypeStruct((dbsz, N_PAIRS, R_HD, LANES), F32)],
        compiler_params=_cparams(("parallel",)),
        name="rwkv_scan_sample",
    )(*seqs3, s0)


def _merge_kernel(x_ref, oa_ref, ob_ref, yr_ref, r_ref, k_ref, v_ref, g_ref, ga_ref, gb_ref, gc_ref,
                  bg_ref, lng_ref, lnb_ref, rk_ref, wpa_ref, wpb_ref, wpc_ref, wo_ref, o_ref):
    mean_m = _head_ones(R_W, 1.0 / R_HD)
    yr = yr_ref[...]
    d = yr - _split_dot(yr, mean_m, 2)
    var = _split_dot(d * d, mean_m, 2)
    yn = d * lax.rsqrt(var + GN_EPS) * lng_ref[...] + lnb_ref[...]
    v = v_ref[...]
    bonus = _split_dot(r_ref[...] * k_ref[...] * rk_ref[...], _head_ones(R_W, 1.0), 2)
    oc = (yn + bonus * v) * g_ref[...]
    merged = (_sigmoid(ga_ref[...] + bg_ref[:, 0:D_MODEL]) * _dot(oa_ref[...], wpa_ref[...])
              + _sigmoid(gb_ref[...] + bg_ref[:, D_MODEL:2 * D_MODEL]) * _dot(ob_ref[...], wpb_ref[...])
              + _sigmoid(gc_ref[...] + bg_ref[:, 2 * D_MODEL:]) * _dot(oc, wpc_ref[...]))
    o_ref[...] = x_ref[...] + _dot(merged, wo_ref[...])


def _merge(x, proj, oa, ob, yr, r, k2, v, g, bg, lng, lnb, rk, wpa, wpb, wpc, wo, tm):
    t = x.shape[0]
    row = lambda w: pl.BlockSpec((tm, w), lambda i: (i, 0))
    const = lambda shape: pl.BlockSpec(shape, lambda i: (0,) * len(shape))
    gcol = C_GATE // D_MODEL
    gate = lambda j: pl.BlockSpec((tm, D_MODEL), lambda i: (i, gcol + j))
    return pl.pallas_call(
        _merge_kernel,
        grid=(t // tm,),
        in_specs=[row(D_MODEL), row(A_W), row(S_W), row(R_W), row(R_W), row(R_W), row(R_W), row(R_W),
                  gate(0), gate(1), gate(2),
                  const((1, 3 * D_MODEL)), const((1, R_W)), const((1, R_W)), const((1, R_W)),
                  const((A_W, D_MODEL)), const((S_W, D_MODEL)), const((R_W, D_MODEL)),
                  const((D_MODEL, D_MODEL))],
        out_specs=row(D_MODEL),
        out_shape=jax.ShapeDtypeStruct((t, D_MODEL), F32),
        compiler_params=_cparams(("parallel",)),
        name="merge",
    )(x, oa, ob, yr, r, k2, v, g, proj, proj, proj, bg, lng, lnb, rk, wpa, wpb, wpc, wo)


def _ffn_kernel(x_ref, g_ref, wug_ref, wuv_ref, wd_ref, cw_ref, cb_ref, pre_ref, gf_ref,
                o_ref, tail_ref, xn_ref, ext_ref, *, tm, P, stride, tiles_per_seq, final_norm):
    i = pl.program_id(0)
    f = pl.program_id(1)
    first = (i % tiles_per_seq) == 0

    @pl.when(f == 0)
    def _():
        x = x_ref[...]
        ms = jnp.mean(x * x, axis=-1, keepdims=True)
        xn_ref[...] = (x * lax.rsqrt(ms + NORM_EPS) * g_ref[...]).astype(BF16)

    @pl.when(first)
    def _():
        ext_ref[f, 0:P, :] = pre_ref[...]

    @pl.when(jnp.logical_not(first))
    def _():
        ext_ref[f, 0:P, :] = ext_ref[f, tm:tm + P, :]

    xn = xn_ref[...]
    ug = jnp.dot(xn, wug_ref[...], preferred_element_type=F32)
    ext_ref[f, P:P + tm, :] = ug
    tail_ref[0] = ext_ref[f, tm:tm + P, :]
    acc = cb_ref[...] + cw_ref[F_CONV - 1:F_CONV, :] * ug
    for d in range(1, F_CONV):
        acc = acc + cw_ref[F_CONV - 1 - d:F_CONV - d, :] * ext_ref[f, P - d * stride:P - d * stride + tm, :]
    uv = jnp.dot(xn, wuv_ref[...], preferred_element_type=F32)
    contrib = _dot(_silu(acc) * uv, wd_ref[...])

    @pl.when(f == 0)
    def _():
        o_ref[...] = x_ref[...] + contrib

    @pl.when(f > 0)
    def _():
        o_ref[...] = o_ref[...] + contrib

    if final_norm:
        @pl.when(f == pl.num_programs(1) - 1)
        def _():
            y = o_ref[...]
            ms = jnp.mean(y * y, axis=-1, keepdims=True)
            o_ref[...] = y * lax.rsqrt(ms + NORM_EPS) * gf_ref[...]


def _ffn(x, g, wup, wd, cw, cb, pre, gf, tm, tf, stride, tiles_per_seq, final_norm):
    t = x.shape[0]
    P = pre.shape[0]
    nf = D_FF // tf
    return pl.pallas_call(
        functools.partial(_ffn_kernel, tm=tm, P=P, stride=stride, tiles_per_seq=tiles_per_seq,
                          final_norm=final_norm),
        grid=(t // tm, nf),
        in_specs=[pl.BlockSpec((tm, D_MODEL), lambda i, f: (i, 0)),
                  pl.BlockSpec((1, D_MODEL), lambda i, f: (0, 0)),
                  pl.BlockSpec((D_MODEL, tf), lambda i, f: (0, f)),
                  pl.BlockSpec((D_MODEL, tf), lambda i, f: (0, nf + f)),
                  pl.BlockSpec((tf, D_MODEL), lambda i, f: (f, 0)),
                  pl.BlockSpec((F_CONV, tf), lambda i, f: (0, f)),
                  pl.BlockSpec((1, tf), lambda i, f: (0, f)),
                  pl.BlockSpec((P, tf), lambda i, f: (0, f)),
                  pl.BlockSpec((1, D_MODEL), lambda i, f: (0, 0))],
        out_specs=[pl.BlockSpec((tm, D_MODEL), lambda i, f: (i, 0)),
                   pl.BlockSpec((1, P, tf), lambda i, f: (i, 0, f))],
        out_shape=[jax.ShapeDtypeStruct((t, D_MODEL), F32),
                   jax.ShapeDtypeStruct((t // tm, P, D_FF), F32)],
        scratch_shapes=[pltpu.VMEM((tm, D_MODEL), BF16), pltpu.VMEM((nf, tm + 2 * P, tf), F32)],
        compiler_params=_cparams(("arbitrary", "arbitrary")),
        name="conv_ffn",
    )(x, g, wup, wup, wd, cw, cb, pre, gf)


def _pack_rwkv_state(s):
    n = s.shape[0]
    return s.reshape(n, N_PAIRS, 2, R_HD, R_HD).transpose(0, 1, 3, 2, 4).reshape(n, N_PAIRS, R_HD, LANES)


def _unpack_rwkv_state(s):
    n = s.shape[0]
    return s.reshape(n, N_PAIRS, R_HD, 2, R_HD).transpose(0, 1, 3, 2, 4).reshape(n, R_HEADS, R_HD, R_HD)


def _unpack_rwkv_blockdiag(s):
    n = s.shape[0]
    return jnp.stack([s[:, :, :R_HD, :R_HD], s[:, :, R_HD:, R_HD:]], axis=2).reshape(n, R_HEADS, R_HD, R_HD)


def _prep_layer_params(l, p):
    w_in = p['w_in'][l]
    c_dt_src = 3 * A_W + S_W + S_CONV_CH
    c_rw_src = c_dt_src + S_HEADS
    c_gate_src = c_rw_src + R_IN_W
    w_proj = jnp.concatenate([
        w_in[:, c_rw_src:c_gate_src],
        w_in[:, c_dt_src:c_rw_src], jnp.zeros((D_MODEL, C_Q - C_DT - S_HEADS), F32),
        w_in[:, :c_dt_src],
        w_in[:, c_gate_src:]], axis=1).astype(BF16)
    pad_lane = lambda v: jnp.pad(v, (0, LANES - v.shape[0])).reshape(1, LANES)
    zl = lambda r0, w: jnp.zeros((R_LORA, R_W), F32).at[r0:r0 + w.shape[0]].set(w).astype(BF16)
    return dict(
        norm1_g=p['norm1_g'][l].reshape(1, D_MODEL), w_proj=w_proj,
        b_gate=p['b_gate'][l].reshape(1, 3 * D_MODEL),
        w_pa=p['w_pa'][l].astype(BF16), w_pb=p['w_pb'][l].astype(BF16), w_pc=p['w_pc'][l].astype(BF16),
        w_o=p['w_o'][l].astype(BF16),
        ssm_conv_w=p['ssm_conv_w'][l], ssm_conv_b=p['ssm_conv_b'][l].reshape(1, S_CONV_CH),
        ssm_dt_bias=pad_lane(p['ssm_dt_bias'][l]), ssm_a_log=pad_lane(p['ssm_a_log'][l]),
        ssm_dvec=jnp.repeat(p['ssm_d'][l], S_HD).reshape(1, S_W),
        ssm_dtb_vec=jnp.repeat(p['ssm_dt_bias'][l], S_HD).reshape(1, S_W),
        ssm_alog_vec=jnp.repeat(p['ssm_a_log'][l], S_HD).reshape(1, S_W),
        ssm_norm_g=p['ssm_norm_g'][l].reshape(1, S_W),
        rw_mu=p['rw_mu'][l].reshape(1, R_IN_W), rw_w0=p['rw_w0'][l].reshape(1, R_W),
        rw_w2p=zl(0, p['rw_w2'][l]), rw_a0=p['rw_a0'][l].reshape(1, R_W),
        rw_a2p=zl(R_LORA_W, p['rw_a2'][l]), rw_g2p=zl(R_LORA_W + R_LORA_A, p['rw_g2'][l]),
        rw_kk=p['rw_kk'][l].reshape(1, R_W), rw_ka=p['rw_ka'][l].reshape(1, R_W),
        rw_rk=p['rw_rk'][l].reshape(1, R_W), rw_ln_g=p['rw_ln_g'][l].reshape(1, R_W),
        rw_ln_b=p['rw_ln_b'][l].reshape(1, R_W),
        norm2_g=p['norm2_g'][l].reshape(1, D_MODEL), w_up=p['w_up'][l].astype(BF16),
        w_down=p['w_down'][l].astype(BF16), ffn_conv_w=p['ffn_conv_w'][l],
        ffn_conv_b=p['ffn_conv_b'][l].reshape(1, D_FF))


def _row_tile(t):
    for tm in (512, 256, 128):
        if t % tm == 0:
            return tm
    raise ValueError(t)


def _ffn_tf():
    return D_FF // 2


def _prompt_layer(x, lp, bsz, seq, slopes_pair, gf, final_norm):
    t = bsz * seq
    tm = _row_tile(seq)
    tiles = seq // tm
    proj = _rms_matmul(x, lp['norm1_g'], lp['w_proj'], 1024 if t % 1024 == 0 else tm, 2048)
    proj3 = proj.reshape(bsz, seq, N_PROJ)
    oa = _moba_prompt(proj3, slopes_pair).reshape(t, A_W)
    ob, ssm_new = _ssd_prompt(proj, bsz, seq, lp['ssm_conv_w'], lp['ssm_conv_b'], lp['ssm_dt_bias'],
                              lp['ssm_a_log'], lp['ssm_dvec'], lp['ssm_norm_g'])
    r, w, k2, v, kk, kka, g = _rwkv_prep(
        proj, jnp.zeros((SUBLANES, R_IN_W), F32), lp['rw_mu'], lp['rw_w0'], lp['rw_w2p'], lp['rw_a0'],
        lp['rw_a2p'], lp['rw_g2p'], lp['rw_kk'], lp['rw_ka'], tm, 1, tiles)
    yr, rw_new = _rwkv_scan_prompt((r, w, k2, v, kk, kka), bsz, seq, LANES)
    x = _merge(x, proj, oa, ob, yr, r, k2, v, g, lp['b_gate'], lp['rw_ln_g'], lp['rw_ln_b'], lp['rw_rk'],
               lp['w_pa'], lp['w_pb'], lp['w_pc'], lp['w_o'], min(tm, 256))
    x, tail = _ffn(x, lp['norm2_g'], lp['w_up'], lp['w_down'], lp['ffn_conv_w'], lp['ffn_conv_b'],
                   jnp.zeros((SUBLANES, D_FF), F32), gf, tm, _ffn_tf(), 1, tiles, final_norm)
    k_new = proj3[:, :, C_K:C_K + A_W].reshape(bsz, seq, A_HEADS, A_HD)
    v_new = proj3[:, :, C_V:C_V + A_W].reshape(bsz, seq, A_HEADS, A_HD)
    ssm_conv_new = proj3[:, seq - (S_CONV - 1):, C_XBC:C_XBC + S_CONV_CH]
    shift_new = proj3[:, seq - 1:, C_RW:C_RW + R_IN_W]
    ffn_conv_new = tail.reshape(bsz, tiles, SUBLANES, D_FF)[:, tiles - 1, SUBLANES - (F_CONV - 1):]
    state = (k_new, v_new, ssm_new.reshape(bsz, S_HEADS, S_HD, S_STATE), ssm_conv_new,
             _unpack_rwkv_blockdiag(rw_new), shift_new, ffn_conv_new)
    return x, state


def _sample_layer(x, lp, dbsz, tnew, st, cache_k4, cache_v4, page_table, layer, gf, final_norm):
    t = tnew * dbsz
    ssm0, ssm_conv0, rwkv0, shift0, ffn_conv0 = st
    proj = _rms_matmul(x, lp['norm1_g'], lp['w_proj'], t, 1024)
    proj3 = proj.reshape(tnew, dbsz, N_PROJ)

    qkv = proj3[:, :, C_Q:C_Q + 3 * A_W].transpose(1, 0, 2)
    qkv8 = jnp.pad(qkv, ((0, 0), (0, SUBLANES - tnew), (0, 0)))
    oa8 = _moba_sample(qkv8[:, :, :A_W], qkv8[:, :, A_W:2 * A_W], qkv8[:, :, 2 * A_W:],
                       cache_k4, cache_v4, page_table, layer, tnew)
    oa = oa8[:, :tnew].transpose(1, 0, 2).reshape(t, A_W)

    h0 = ssm0.reshape(-1, dbsz, N_PAIRS, 2 * S_HD, S_STATE)
    ob3, ssm_new = _ssd_sample(proj3, ssm_conv0.transpose(1, 0, 2), h0, layer, lp['ssm_conv_w'],
                               lp['ssm_conv_b'], lp['ssm_dtb_vec'], lp['ssm_alog_vec'], lp['ssm_dvec'],
                               lp['ssm_norm_g'])
    ob = ob3.reshape(t, S_W)
    ssm_new = ssm_new.reshape(dbsz, S_HEADS, S_HD, S_STATE)

    r, w, k2, v, kk, kka, g = _rwkv_prep(
        proj, shift0.reshape(dbsz, R_IN_W), lp['rw_mu'], lp['rw_w0'], lp['rw_w2p'], lp['rw_a0'],
        lp['rw_a2p'], lp['rw_g2p'], lp['rw_kk'], lp['rw_ka'], t, dbsz, 1)
    to3 = lambda a: a.reshape(tnew, dbsz, R_W)
    yr3, rw_new = _rwkv_scan_sample(tuple(to3(a) for a in (r, w, k2, v, kk, kka)), _pack_rwkv_state(rwkv0))
    yr = yr3.reshape(t, R_W)

    x = _merge(x, proj, oa, ob, yr, r, k2, v, g, lp['b_gate'], lp['rw_ln_g'], lp['rw_ln_b'], lp['rw_rk'],
               lp['w_pa'], lp['w_pb'], lp['w_pc'], lp['w_o'], min(t, 256))
    pre = ffn_conv0.transpose(1, 0, 2).reshape((F_CONV - 1) * dbsz, D_FF)
    x, tail = _ffn(x, lp['norm2_g'], lp['w_up'], lp['w_down'], lp['ffn_conv_w'], lp['ffn_conv_b'],
                   pre, gf, t, _ffn_tf(), dbsz, 1, final_norm)

    k_new = qkv[:, :, A_W:2 * A_W].reshape(dbsz, tnew, A_HEADS, A_HD)
    v_new = qkv[:, :, 2 * A_W:].reshape(dbsz, tnew, A_HEADS, A_HD)
    ssm_conv_new = proj3[tnew - (S_CONV - 1):, :, C_XBC:C_XBC + S_CONV_CH].transpose(1, 0, 2)
    shift_new = proj3[tnew - 1:, :, C_RW:C_RW + R_IN_W].transpose(1, 0, 2)
    ffn_conv_new = tail.reshape(F_CONV - 1, dbsz, D_FF).transpose(1, 0, 2)
    state = (k_new, v_new, ssm_new, ssm_conv_new, _unpack_rwkv_state(rw_new), shift_new, ffn_conv_new)
    return x, state


def kernel(x_prompt, x_sample, cache_k, cache_v, state_ssm, state_ssm_conv, state_rwkv, state_rwkv_shift, state_ffn_conv, page_table, norm1_g, w_in, b_gate, w_pa, ssm_conv_w, ssm_conv_b, ssm_dt_bias, ssm_a_log, ssm_d, ssm_norm_g, w_pb, rw_mu, rw_w0, rw_w2, rw_a0, rw_a2, rw_g2, rw_kk, rw_ka, rw_rk, rw_ln_g, rw_ln_b, w_pc, w_o, norm2_g, w_up, ffn_conv_w, ffn_conv_b, w_down, norm_f_g):
    params = dict(norm1_g=norm1_g, w_in=w_in, b_gate=b_gate, w_pa=w_pa, ssm_conv_w=ssm_conv_w,
                  ssm_conv_b=ssm_conv_b, ssm_dt_bias=ssm_dt_bias, ssm_a_log=ssm_a_log, ssm_d=ssm_d,
                  ssm_norm_g=ssm_norm_g, w_pb=w_pb, rw_mu=rw_mu, rw_w0=rw_w0, rw_w2=rw_w2, rw_a0=rw_a0,
                  rw_a2=rw_a2, rw_g2=rw_g2, rw_kk=rw_kk, rw_ka=rw_ka, rw_rk=rw_rk, rw_ln_g=rw_ln_g,
                  rw_ln_b=rw_ln_b, w_pc=w_pc, w_o=w_o, norm2_g=norm2_g, w_up=w_up, ffn_conv_w=ffn_conv_w,
                  ffn_conv_b=ffn_conv_b, w_down=w_down)
    depth = w_in.shape[0]
    bsz, seq, _ = x_prompt.shape
    dbsz, tnew, _ = x_sample.shape
    head = jnp.arange(A_HEADS, dtype=F32) + 1.0
    slopes = jnp.exp2(-8.0 * head / A_HEADS)
    slopes_pair = jnp.repeat(slopes, A_HD).reshape(N_PAIRS, 1, LANES)
    cache_k4 = cache_k.transpose(0, 1, 3, 4, 2)
    cache_v4 = cache_v.transpose(0, 1, 3, 4, 2)
    gf = norm_f_g.reshape(1, D_MODEL)

    hp = x_prompt.reshape(bsz * seq, D_MODEL)
    hs = x_sample.transpose(1, 0, 2).reshape(tnew * dbsz, D_MODEL)
    new_p = [[] for _ in range(7)]
    new_s = [[] for _ in range(7)]
    for l in range(depth):
        lp = _prep_layer_params(l, params)
        last = l == depth - 1
        hp, sp = _prompt_layer(hp, lp, bsz, seq, slopes_pair, gf, last)
        st = (state_ssm, state_ssm_conv[l], state_rwkv[l], state_rwkv_shift[l], state_ffn_conv[l])
        hs, ss = _sample_layer(hs, lp, dbsz, tnew, st, cache_k4, cache_v4, page_table, l, gf, last)
        for j in range(7):
            new_p[j].append(sp[j])
            new_s[j].append(ss[j])
    y_prompt = hp.reshape(bsz, seq, D_MODEL)
    y_sample = hs.reshape(tnew, dbsz, D_MODEL).transpose(1, 0, 2)
    outs = [y_prompt, y_sample]
    for j in range(7):
        outs += [jnp.stack(new_p[j]), jnp.stack(new_s[j])]
    return tuple(outs)
```

```python
import functools
import math

import jax
import jax.numpy as jnp
from jax import lax
from jax.experimental import pallas as pl
from jax.experimental.pallas import tpu as pltpu

F32 = jnp.float32
BF16 = jnp.bfloat16

D_MODEL = 1024
A_HEADS = 8
A_HD = 64
A_W = A_HEADS * A_HD
MOBA_BLOCK = 256
MOBA_TOPK = 3
Q_BLOCK = MOBA_BLOCK
ATTN_SCALE = A_HD ** -0.5
S_HEADS = 8
S_HD = 64
S_W = S_HEADS * S_HD
S_GROUPS = 2
S_STATE = 128
S_CONV = 4
S_CONV_CH = S_W + 2 * S_GROUPS * S_STATE
SSD_CHUNK = 128
R_HEADS = 8
R_HD = 64
R_W = R_HEADS * R_HD
R_LORA_W = 64
R_LORA_A = 64
R_LORA_G = 128
R_LORA = R_LORA_W + R_LORA_A + R_LORA_G
R_IN_W = 3 * R_W + R_LORA
D_FF = ((8 * D_MODEL // 3 + 127) // 128) * 128
F_CONV = 3
NORM_EPS = 1e-6
GN_EPS = 64e-5
NEG_INF = -1e30

LANES = 128
SUBLANES = 8
HALF = 64
HALF_SHIFT = 6
SUBLANE_SHIFT = 3
N_PAIRS = 4

C_RW = 0
C_DT = R_IN_W
C_Q = 2048
C_K = C_Q + A_W
C_V = C_K + A_W
C_Z = C_V + A_W
C_XBC = C_Z + S_W
C_GATE = C_XBC + S_CONV_CH
N_PROJ = C_GATE + 3 * D_MODEL

VMEM_LIMIT = 56 * 1024 * 1024


def _cparams(sem):
    return pltpu.CompilerParams(dimension_semantics=sem, vmem_limit_bytes=VMEM_LIMIT)


def _dot(a, b):
    return jnp.dot(a.astype(BF16), b.astype(BF16), preferred_element_type=F32)


def _dot_nt(a, b):
    return lax.dot_general(a.astype(BF16), b.astype(BF16), (((1,), (1,)), ((), ())),
                           preferred_element_type=F32)


def _dot_tn(a, b):
    return lax.dot_general(a.astype(BF16), b.astype(BF16), (((0,), (0,)), ((), ())),
                           preferred_element_type=F32)


def _split_dot(x, w, passes, x_is_lhs=True):
    w = w.astype(BF16)
    acc = None
    rem = x
    for _ in range(passes):
        piece = rem.astype(BF16)
        term = (jnp.dot(piece, w, preferred_element_type=F32) if x_is_lhs
                else jnp.dot(w, piece, preferred_element_type=F32))
        acc = term if acc is None else acc + term
        rem = rem - piece.astype(F32)
    return acc


def _dot_nt_hi(a, b):
    return lax.dot_general(a, b, (((1,), (1,)), ((), ())), preferred_element_type=F32,
                           precision=lax.Precision.HIGHEST)


def _sigmoid(x):
    return 1.0 / (1.0 + jnp.exp(-x))


def _silu(x):
    return x * _sigmoid(x)


def _softplus(x):
    return jnp.maximum(x, 0.0) + jnp.log(1.0 + jnp.exp(-jnp.abs(x)))


def _head_ones(width, scale):
    r = lax.shift_right_logical(lax.broadcasted_iota(jnp.int32, (width, width), 0), HALF_SHIFT)
    c = lax.shift_right_logical(lax.broadcasted_iota(jnp.int32, (width, width), 1), HALF_SHIFT)
    return jnp.where(r == c, scale, 0.0).astype(F32)


def _rms_matmul_kernel(x_ref, g_ref, w_ref, o_ref, xn_ref):
    @pl.when(pl.program_id(1) == 0)
    def _():
        x = x_ref[...]
        ms = jnp.mean(x * x, axis=-1, keepdims=True)
        xn_ref[...] = (x * lax.rsqrt(ms + NORM_EPS) * g_ref[...]).astype(BF16)

    o_ref[...] = jnp.dot(xn_ref[...], w_ref[...], preferred_element_type=F32)


def _rms_matmul(x, g, w, tm, tn):
    t, d = x.shape
    n = w.shape[1]
    return pl.pallas_call(
        _rms_matmul_kernel,
        grid=(t // tm, n // tn),
        in_specs=[pl.BlockSpec((tm, d), lambda i, j: (i, 0)),
                  pl.BlockSpec((1, d), lambda i, j: (0, 0)),
                  pl.BlockSpec((d, tn), lambda i, j: (0, j))],
        out_specs=pl.BlockSpec((tm, tn), lambda i, j: (i, j)),
        out_shape=jax.ShapeDtypeStruct((t, n), F32),
        scratch_shapes=[pltpu.VMEM((tm, d), BF16)],
        compiler_params=_cparams(("parallel", "arbitrary")),
        name="in_proj",
    )(x, g, w)


def _topk_bias(gate, n_valid, axis=1):
    pos = lax.broadcasted_iota(jnp.int32, gate.shape, axis)
    pos_f = pos.astype(F32)
    gm = jnp.where(pos < n_valid, gate, NEG_INF)
    selected = jnp.zeros(gate.shape, jnp.bool_)
    for j in range(MOBA_TOPK):
        m = jnp.max(gm, axis=axis, keepdims=True)
        idx = jnp.min(jnp.where(gm == m, pos_f, 1e9), axis=axis, keepdims=True)
        hit = pos_f == idx
        selected = jnp.logical_or(selected, jnp.logical_and(hit, j < n_valid))
        gm = jnp.where(hit, -jnp.inf, gm)
    return jnp.where(selected, 0.0, NEG_INF).astype(F32)


def _bf16_pieces(x, n):
    out = []
    rem = x
    for _ in range(n):
        piece = rem.astype(BF16).astype(F32)
        out.append(piece)
        rem = rem - piece
    return out


def _moba_prompt_kernel(q_ref, k_ref, v_ref, sl_ref, o_ref, kf_ref, vth_ref, kmean_ref, sel_ref, acc_ref,
                        raw_a, raw_b, m_ref, knorm_ref, *, nb):
    i = pl.program_id(2)
    B = MOBA_BLOCK
    n_piece = 3
    lane = lax.broadcasted_iota(jnp.int32, (1, LANES), 1)
    halves = (lane < HALF, lane >= HALF)

    @pl.when(i == 0)
    def _():
        kmean_ref[...] = jnp.zeros_like(kmean_ref)
        key_local = jnp.bitwise_and(lax.broadcasted_iota(jnp.int32, (k_ref.shape[1], 1), 0), B - 1).astype(F32)
        extra = jnp.where(lane < n_piece, key_local, jnp.where(lane < 2 * n_piece, 1.0, 0.0))
        kf_ref[...] = jnp.concatenate([k_ref[0], extra], axis=1).astype(BF16)
        ones_rows = jnp.ones((SUBLANES, B), F32)
        knorm = [jnp.zeros((1, LANES), F32), jnp.zeros((1, LANES), F32)]
        for n in range(nb):
            kblk = k_ref[0, n * B:(n + 1) * B, :]
            kmean_ref[n:n + 1, :] = jnp.mean(kblk, axis=0, keepdims=True)
            vt = v_ref[0, n * B:(n + 1) * B, :].T
            for h in range(2):
                vth_ref[h, n] = jnp.concatenate([vt[h * HALF:(h + 1) * HALF], ones_rows], axis=0).astype(BF16)
                norm2 = jnp.sum(jnp.where(halves[h], kblk * kblk, 0.0), axis=-1, keepdims=True)
                knorm[h] = jnp.where(lane == n, jnp.max(norm2, axis=0, keepdims=True), knorm[h])
        for h in range(2):
            knorm_ref[h] = jnp.broadcast_to(knorm[h], (SUBLANES, LANES))

    own = i
    q = q_ref[0]
    log2e = 1.0 / math.log(2.0)
    slope2 = (sl_ref[0, :, 0:1] * log2e, sl_ref[0, :, HALF:HALF + 1] * log2e)
    kmean = kmean_ref[...]
    q_local = lax.broadcasted_iota(jnp.int32, (Q_BLOCK, 1), 0).astype(F32)
    q_rows = []
    for h in range(2):
        q_m = jnp.where(halves[h], q, 0.0)
        bias_t = _topk_bias(_dot_nt_hi(kmean, q_m), own, axis=0)
        for n in range(nb):
            sel_ref[h, n] = jnp.broadcast_to(bias_t[n:n + 1, :], (SUBLANES, Q_BLOCK))
        pieces = _bf16_pieces(slope2[h], n_piece) + _bf16_pieces(-slope2[h] * q_local, n_piece)
        extra = jnp.zeros((Q_BLOCK, LANES), F32)
        for j, piece in enumerate(pieces):
            extra = jnp.where(lane == j, piece, extra)
        q_rows.append(jnp.concatenate([q_m * (ATTN_SCALE * log2e), extra], axis=1))
    qf = jnp.concatenate(q_rows, axis=0).astype(BF16)

    rel = (lax.broadcasted_iota(jnp.int32, (B, Q_BLOCK), 0) - lax.broadcasted_iota(jnp.int32, (B, Q_BLOCK), 1))

    def scores(n):
        both = _dot_nt(kf_ref[pl.ds(pl.multiple_of(n * B, B), B), :], qf)
        return [both[:, h * Q_BLOCK:(h + 1) * Q_BLOCK] for h in range(2)]

    def put_scores(n, dst_ref):
        for h, s in enumerate(scores(n)):
            dst_ref[h] = s

    def row_of(ref, h):
        return ref[h][0:1, :]

    def put_row(ref, h, x):
        ref[h] = jnp.broadcast_to(x, (SUBLANES, Q_BLOCK))

    def attend(n, src_ref):
        shift = ((n - i) * B).astype(F32)
        ps, alphas = [], []
        for h in range(2):
            s = src_ref[h] + (sel_ref[h, n][0:1, :] + slope2[h] * shift)
            m0 = row_of(m_ref, h)
            m1 = jnp.maximum(m0, jnp.max(s, axis=0, keepdims=True))
            alphas.append(jnp.exp2(m0 - m1))
            ps.append(jnp.exp2(s - m1).astype(BF16))
            put_row(m_ref, h, m1)
        pv = [jnp.dot(vth_ref[h, n], ps[h], preferred_element_type=F32) for h in range(2)]
        for h in range(2):
            acc_ref[h] = alphas[h] * acc_ref[h] + pv[h]

    own_raw = scores(own)
    lane_f = lane.astype(F32)
    bound_c = ATTN_SCALE * log2e * 1.02
    skips = []
    for h in range(2):
        q_m = jnp.where(halves[h], q, 0.0)
        qn2 = jnp.max(jnp.sum(q_m * q_m, axis=-1, keepdims=True), axis=0, keepdims=True)
        ub = jnp.sqrt(qn2 * knorm_ref[h][0:1, :]) * bound_c
        ub_own = jnp.sum(jnp.where(lane == own, ub, 0.0), axis=-1, keepdims=True)
        far = slope2[h] * ((lane_f - own.astype(F32)) * B + (B - 1))
        skips.append(ub + far < -ub_own - 160.0)
    keep = jnp.logical_and(lane < own, jnp.logical_not(jnp.logical_and(skips[0], skips[1])))
    first = jnp.min(jnp.where(keep, lane_f, own.astype(F32))).astype(jnp.int32)
    n_visit = own - first

    put_scores(jnp.minimum(first, nb - 1), raw_a)

    own_p = []
    for h in range(2):
        s = jnp.where(rel <= 0, own_raw[h], NEG_INF)
        m = jnp.max(s, axis=0, keepdims=True)
        put_row(m_ref, h, m)
        own_p.append(jnp.exp2(s - m).astype(BF16))
    for h in range(2):
        acc_ref[h] = jnp.dot(vth_ref[h, own], own_p[h], preferred_element_type=F32)

    def body(j, carry):
        n0 = first + 2 * j
        put_scores(n0 + 1, raw_b)
        attend(n0, raw_a)
        put_scores(jnp.minimum(n0 + 2, nb - 1), raw_a)
        attend(n0 + 1, raw_b)
        return carry

    lax.fori_loop(0, n_visit // 2, body, 0)

    @pl.when(n_visit % 2 == 1)
    def _():
        attend(own - 1, raw_a)

    out_t = jnp.concatenate([acc_ref[h][0:HALF] / acc_ref[h][HALF:HALF + 1] for h in range(2)], axis=0)
    o_ref[0] = out_t.T


def _moba_prompt(proj3, slopes_pair):
    bsz, seq, _ = proj3.shape
    assert seq % MOBA_BLOCK == 0 and seq // MOBA_BLOCK >= MOBA_TOPK
    nb = seq // MOBA_BLOCK
    nbp = -(-nb // SUBLANES) * SUBLANES
    nq = seq // Q_BLOCK
    qc, kc, vc = C_Q // LANES, C_K // LANES, C_V // LANES
    return pl.pallas_call(
        functools.partial(_moba_prompt_kernel, nb=nb),
        grid=(bsz, N_PAIRS, nq),
        in_specs=[pl.BlockSpec((1, Q_BLOCK, LANES), lambda b, p, i: (b, i, qc + p)),
                  pl.BlockSpec((1, seq, LANES), lambda b, p, i: (b, 0, kc + p)),
                  pl.BlockSpec((1, seq, LANES), lambda b, p, i: (b, 0, vc + p)),
                  pl.BlockSpec((1, 1, LANES), lambda b, p, i: (p, 0, 0))],
        out_specs=pl.BlockSpec((1, Q_BLOCK, LANES), lambda b, p, i: (b, i, p)),
        out_shape=jax.ShapeDtypeStruct((bsz, seq, A_W), F32),
        scratch_shapes=[pltpu.VMEM((seq, 2 * LANES), BF16),
                        pltpu.VMEM((2, nb, HALF + SUBLANES, MOBA_BLOCK), BF16),
                        pltpu.VMEM((nbp, LANES), F32), pltpu.VMEM((2, nb, SUBLANES, Q_BLOCK), F32),
                        pltpu.VMEM((2, HALF + SUBLANES, Q_BLOCK), F32),
                        pltpu.VMEM((2, MOBA_BLOCK, Q_BLOCK), F32), pltpu.VMEM((2, MOBA_BLOCK, Q_BLOCK), F32),
                        pltpu.VMEM((2, SUBLANES, Q_BLOCK), F32), pltpu.VMEM((2, SUBLANES, LANES), F32)],
        compiler_params=_cparams(("parallel", "parallel", "arbitrary")),
        name="moba_prompt",
    )(proj3, proj3, proj3, slopes_pair)


def _moba_sample_kernel(pt_ref, q_ref, kn_ref, vn_ref, *refs, n_blk, ppb, page, tnew):
    del pt_ref
    n_pages = n_blk * ppb
    k_refs, v_refs = refs[:n_pages], refs[n_pages:2 * n_pages]
    o_ref, m_ref, l_ref, g_ref, acc_ref = refs[2 * n_pages:]
    past = n_blk * MOBA_BLOCK
    rows = A_HEADS * SUBLANES
    lane_head = lax.shift_right_logical(lax.broadcasted_iota(jnp.int32, (SUBLANES, A_W), 1), HALF_SHIFT)
    q8 = q_ref[0] * ATTN_SCALE
    qbd = jnp.concatenate([jnp.where(lane_head == h, q8, 0.0) for h in range(A_HEADS)], axis=0)
    q_hi = qbd.astype(BF16)
    q_lo = (qbd - q_hi.astype(F32)).astype(BF16)
    r = lax.broadcasted_iota(jnp.int32, (rows, 1), 0)
    tok = jnp.bitwise_and(r, SUBLANES - 1).astype(F32)
    slope = jnp.exp2(-(8.0 / A_HEADS) * (lax.shift_right_logical(r, SUBLANE_SHIFT) + 1).astype(F32))
    key = lax.broadcasted_iota(jnp.int32, (1, MOBA_BLOCK), 1).astype(F32)

    def block_t(refs_, n):
        return jnp.concatenate([refs_[j][0, 0].reshape(A_W, page) for j in range(n * ppb, (n + 1) * ppb)],
                               axis=1).astype(BF16)

    raws, fixes = [], []
    for n in range(n_blk):
        kt = block_t(k_refs, n)
        raws.append(jnp.dot(q_hi, kt, preferred_element_type=F32))
        fixes.append(jnp.dot(q_lo, kt, preferred_element_type=F32))
    es = []
    for n in range(n_blk):
        g_ref[n] = jnp.broadcast_to(jnp.sum(raws[n] + fixes[n], axis=-1, keepdims=True) * (1.0 / MOBA_BLOCK),
                                    (rows, LANES))
        s = raws[n] - slope * ((past - n * MOBA_BLOCK + tok) - key)
        m = jnp.max(s, axis=-1, keepdims=True)
        e = jnp.exp(s - m)
        m_ref[n] = jnp.broadcast_to(m, (rows, LANES))
        l_ref[n] = jnp.broadcast_to(jnp.sum(e, axis=-1, keepdims=True), (rows, LANES))
        es.append(e.astype(BF16))
    for n in range(n_blk):
        acc_ref[n] = lax.dot_general(es[n], block_t(v_refs, n), (((1,), (1,)), ((), ())),
                                     preferred_element_type=F32)

    lane = lax.broadcasted_iota(jnp.int32, (rows, LANES), 1)
    gate = jnp.zeros((rows, LANES), F32)
    for j in range(n_blk):
        gate = jnp.where(lane == j, g_ref[j], gate)
    bias = _topk_bias(gate, n_blk)

    kn = kn_ref[0]
    vn = vn_ref[0]
    s_own = []
    for j in range(tnew):
        sj = jnp.sum(qbd * kn[j:j + 1, :], axis=-1, keepdims=True) - slope * (tok - j)
        s_own.append(jnp.where(tok >= j, sj, NEG_INF))
    mx = s_own[0]
    for j in range(1, tnew):
        mx = jnp.maximum(mx, s_own[j])
    mb = []
    for j in range(n_blk):
        mj = m_ref[j][:, 0:1] + bias[:, j:j + 1]
        mb.append(mj)
        mx = jnp.maximum(mx, mj)
    lsum = jnp.zeros((rows, 1), F32)
    acc = jnp.zeros((rows, A_W), F32)
    for j in range(tnew):
        w = jnp.exp(s_own[j] - mx)
        lsum = lsum + w
        acc = acc + w * vn[j:j + 1, :]
    for j in range(n_blk):
        w = jnp.exp(mb[j] - mx)
        lsum = lsum + w * l_ref[j][:, 0:1]
        acc = acc + w * acc_ref[j]
    out = acc / lsum
    o8 = jnp.zeros((SUBLANES, A_W), F32)
    for h in range(A_HEADS):
        o8 = o8 + jnp.where(lane_head == h, out[h * SUBLANES:(h + 1) * SUBLANES, :], 0.0)
    o_ref[0] = o8


def _moba_sample(q8, k8, v8, cache_kt, cache_vt, page_table, layer, tnew):
    dbsz = q8.shape[0]
    n_pages = page_table.shape[1]
    page = cache_kt.shape[4]
    assert MOBA_BLOCK % page == 0 and (n_pages * page) % MOBA_BLOCK == 0 and page % LANES == 0
    ppb = MOBA_BLOCK // page
    n_blk = n_pages // ppb
    assert n_blk >= MOBA_TOPK and tnew <= SUBLANES
    rows = A_HEADS * SUBLANES
    tok_spec = pl.BlockSpec((1, SUBLANES, A_W), lambda b, pt: (b, 0, 0))

    def page_spec(j):
        return pl.BlockSpec((1, 1, A_HEADS, A_HD, page), lambda b, pt: (layer, pt[b * n_pages + j], 0, 0, 0))

    pages = [page_spec(j) for j in range(n_pages)]
    grid_spec = pltpu.PrefetchScalarGridSpec(
        num_scalar_prefetch=1,
        grid=(dbsz,),
        in_specs=[tok_spec, tok_spec, tok_spec] + pages + pages,
        out_specs=tok_spec,
        scratch_shapes=[pltpu.VMEM((n_blk, rows, LANES), F32), pltpu.VMEM((n_blk, rows, LANES), F32),
                        pltpu.VMEM((n_blk, rows, LANES), F32), pltpu.VMEM((n_blk, rows, A_W), F32)],
    )
    return pl.pallas_call(
        functools.partial(_moba_sample_kernel, n_blk=n_blk, ppb=ppb, page=page, tnew=tnew),
        grid_spec=grid_spec,
        out_shape=jax.ShapeDtypeStruct((dbsz, SUBLANES, A_W), F32),
        compiler_params=_cparams(("parallel",)),
        name="moba_sample",
    )(page_table.reshape(-1), q8, k8, v8, *([cache_kt] * n_pages), *([cache_vt] * n_pages))


def _ssd_prompt_kernel(z_ref, xbc_ref, dt_ref, cw_ref, cb_ref, dtb_ref, alog_ref, dvec_ref, ng_ref,
                       o_ref, hout_ref, ext_ref, h_ref, *, L):
    c = pl.program_id(1)
    P = SUBLANES

    @pl.when(c == 0)
    def _():
        ext_ref[0:P, :] = jnp.zeros((P, S_CONV_CH), F32)
        h_ref[...] = jnp.zeros_like(h_ref)

    @pl.when(c > 0)
    def _():
        ext_ref[0:P, :] = ext_ref[L:L + P, :]

    ext_ref[P:P + L, :] = xbc_ref[...]
    acc = cb_ref[...] + cw_ref[S_CONV - 1:S_CONV, :] * ext_ref[P:P + L, :]
    for d in range(1, S_CONV):
        acc = acc + cw_ref[S_CONV - 1 - d:S_CONV - d, :] * ext_ref[P - d:P - d + L, :]
    xbc = _silu(acc)
    xs = xbc[:, :S_W]
    bm = xbc[:, S_W:S_W + S_GROUPS * S_STATE]
    cm = xbc[:, S_W + S_GROUPS * S_STATE:]

    lane = lax.broadcasted_iota(jnp.int32, (1, LANES), 1)
    is_a = lane < HALF
    dt = jnp.where(lane < S_HEADS, _softplus(dt_ref[...] + dtb_ref[...]), 0.0)
    da = dt * (-jnp.exp(alog_ref[...]))
    rr = lax.broadcasted_iota(jnp.int32, (L, L), 0)
    cc = lax.broadcasted_iota(jnp.int32, (L, L), 1)
    causal = rr >= cc
    cum = _split_dot(da, jnp.where(causal, 1.0, 0.0), 3, x_is_lhs=False)
    cum_t = cum.T
    dt_t = dt.T
    row_a = lax.broadcasted_iota(jnp.int32, (2 * S_HD, 1), 0) < S_HD

    group_of = [(2 * pr * S_GROUPS) // S_HEADS for pr in range(N_PAIRS)]
    bgs = [bm[:, g * S_STATE:(g + 1) * S_STATE] for g in range(S_GROUPS)]
    cgs = [cm[:, g * S_STATE:(g + 1) * S_STATE] for g in range(S_GROUPS)]
    cb_g = [_dot_nt(cgs[g], bgs[g]) for g in range(S_GROUPS)]
    xs_ps = [xs[:, pr * LANES:(pr + 1) * LANES] for pr in range(N_PAIRS)]
    hps = [h_ref[pr] for pr in range(N_PAIRS)]
    halves = (is_a, jnp.logical_not(is_a))
    cum_c = [cum[:, h:h + 1] for h in range(S_HEADS)]
    last = [cum[L - 1:L, h:h + 1] for h in range(S_HEADS)]
    wts = []
    for h in range(S_HEADS):
        seg = cum_c[h] - cum_t[h:h + 1, :]
        dec = jnp.exp(jnp.where(causal, seg, -jnp.inf))
        wts.append(cb_g[group_of[h // 2]] * dec * dt_t[h:h + 1, :])
    intra = [_dot(wts[h], jnp.where(halves[h % 2], xs_ps[h // 2], 0.0)) for h in range(S_HEADS)]
    inter = [_dot_nt(cgs[group_of[pr]], hps[pr]) for pr in range(N_PAIRS)]
    ys = [intra[2 * pr] + intra[2 * pr + 1]
          + inter[pr] * jnp.where(is_a, jnp.exp(cum_c[2 * pr]), jnp.exp(cum_c[2 * pr + 1])) for pr in range(N_PAIRS)]
    tes = [jnp.where(is_a, jnp.exp(last[2 * pr] - cum_c[2 * pr]) * dt[:, 2 * pr:2 * pr + 1],
                     jnp.exp(last[2 * pr + 1] - cum_c[2 * pr + 1]) * dt[:, 2 * pr + 1:2 * pr + 2])
           for pr in range(N_PAIRS)]
    sts = [_dot_tn(xs_ps[pr] * tes[pr], bgs[group_of[pr]]) for pr in range(N_PAIRS)]
    for pr in range(N_PAIRS):
        h_ref[pr] = hps[pr] * jnp.where(row_a, jnp.exp(last[2 * pr]), jnp.exp(last[2 * pr + 1])) + sts[pr]
    y = jnp.concatenate(ys, axis=1) + dvec_ref[...] * xs
    yz = y * _silu(z_ref[...])
    gw = S_W // S_GROUPS
    for g in range(S_GROUPS):
        part = yz[:, g * gw:(g + 1) * gw]
        ms = jnp.mean(part * part, axis=-1, keepdims=True)
        o_ref[:, g * gw:(g + 1) * gw] = part * lax.rsqrt(ms + NORM_EPS) * ng_ref[:, g * gw:(g + 1) * gw]

    @pl.when(c == pl.num_programs(1) - 1)
    def _():
        hout_ref[0] = h_ref[...]


def _ssd_prompt(proj, bsz, seq, cw, cb, dtb, alog, dvec, ng):
    L = SSD_CHUNK
    nc = seq // L
    const = lambda shape: pl.BlockSpec(shape, lambda b, c: (0,) * len(shape))
    return pl.pallas_call(
        functools.partial(_ssd_prompt_kernel, L=L),
        grid=(bsz, nc),
        in_specs=[pl.BlockSpec((L, S_W), lambda b, c: (b * nc + c, C_Z // S_W)),
                  pl.BlockSpec((L, S_CONV_CH), lambda b, c: (b * nc + c, C_XBC // S_CONV_CH)),
                  pl.BlockSpec((L, LANES), lambda b, c: (b * nc + c, C_DT // LANES)),
                  const((S_CONV, S_CONV_CH)), const((1, S_CONV_CH)), const((1, LANES)), const((1, LANES)),
                  const((1, S_W)), const((1, S_W))],
        out_specs=[pl.BlockSpec((L, S_W), lambda b, c: (b * nc + c, 0)),
                   pl.BlockSpec((1, N_PAIRS, 2 * S_HD, S_STATE), lambda b, c: (b, 0, 0, 0))],
        out_shape=[jax.ShapeDtypeStruct((bsz * seq, S_W), F32),
                   jax.ShapeDtypeStruct((bsz, N_PAIRS, 2 * S_HD, S_STATE), F32)],
        scratch_shapes=[pltpu.VMEM((L + 2 * SUBLANES, S_CONV_CH), F32),
                        pltpu.VMEM((N_PAIRS, 2 * S_HD, S_STATE), F32)],
        compiler_params=_cparams(("parallel", "arbitrary")),
        name="ssd_prompt",
    )(proj, proj, proj, cw, cb, dtb, alog, dvec, ng)


def _ssd_sample_kernel(z_ref, xbc_ref, dt_ref, pre_ref, h0_ref, cw_ref, cb_ref, dtbv_ref, alogv_ref,
                       dvec_ref, ng_ref, o_ref, hout_ref, *, T, nb):
    rows2 = 2 * S_HD
    eye = (lax.broadcasted_iota(jnp.int32, (rows2, LANES), 0)
           == lax.broadcasted_iota(jnp.int32, (rows2, LANES), 1)).astype(F32)
    eye_all = jnp.concatenate([eye] * nb, axis=0)
    ones = jnp.ones((LANES, LANES), F32)
    spread = (lax.broadcasted_iota(jnp.int32, (LANES, S_W), 0)
              == lax.shift_right_logical(lax.broadcasted_iota(jnp.int32, (LANES, S_W), 1), HALF_SHIFT)).astype(F32)
    a_vec = -jnp.exp(alogv_ref[...])
    gw = S_W // S_GROUPS

    def as_columns(x, passes):
        return _split_dot(eye_all * _repeat_rows(x, rows2), ones, passes)

    up = [pre_ref[j] for j in range(S_CONV - 1)] + [xbc_ref[t] for t in range(T)]
    hs = [h0_ref[0, :, pr].reshape(nb * rows2, S_STATE) for pr in range(N_PAIRS)]
    for t in range(T):
        acc = cb_ref[...] + cw_ref[0:1, :] * up[t]
        for j in range(1, S_CONV):
            acc = acc + cw_ref[j:j + 1, :] * up[t + j]
        xc = _silu(acc)
        xs = xc[:, :S_W]
        dt = _softplus(_split_dot(dt_ref[t], spread, 3) + dtbv_ref[...])
        decay = jnp.exp(dt * a_vec)
        xdt = xs * dt
        ys = []
        for pr in range(N_PAIRS):
            g = (2 * pr * S_GROUPS) // S_HEADS
            sl = slice(pr * LANES, (pr + 1) * LANES)
            b_rows = _repeat_rows(xc[:, S_W + g * S_STATE:S_W + (g + 1) * S_STATE], rows2)
            c_rows = _repeat_rows(xc[:, S_W + (S_GROUPS + g) * S_STATE:S_W + (S_GROUPS + g + 1) * S_STATE], rows2)
            hs[pr] = hs[pr] * as_columns(decay[:, sl], 3) + as_columns(xdt[:, sl], 2) * b_rows
            y_col = _split_dot(hs[pr] * c_rows, ones, 2)
            ys.append(jnp.sum((eye_all * y_col).reshape(nb, rows2, LANES), axis=1))
        y = jnp.concatenate(ys, axis=1) + dvec_ref[...] * xs
        yz = y * _silu(z_ref[t])
        for g in range(S_GROUPS):
            part = yz[:, g * gw:(g + 1) * gw]
            ms = jnp.mean(part * part, axis=-1, keepdims=True)
            o_ref[t, :, g * gw:(g + 1) * gw] = part * lax.rsqrt(ms + NORM_EPS) * ng_ref[:, g * gw:(g + 1) * gw]
    for pr in range(N_PAIRS):
        hout_ref[:, pr] = hs[pr].reshape(nb, rows2, S_STATE)


def _ssd_sample(proj3, pre3, h0, layer, cw, cb, dtbv, alogv, dvec, ng, nb=8):
    T, dbsz = proj3.shape[:2]
    const = lambda shape: pl.BlockSpec(shape, lambda j: (0,) * len(shape))
    hspec = pl.BlockSpec((nb, N_PAIRS, 2 * S_HD, S_STATE), lambda j: (j, 0, 0, 0))
    h0spec = pl.BlockSpec((1, nb, N_PAIRS, 2 * S_HD, S_STATE), lambda j: (layer, j, 0, 0, 0))
    return pl.pallas_call(
        functools.partial(_ssd_sample_kernel, T=T, nb=nb),
        grid=(dbsz // nb,),
        in_specs=[pl.BlockSpec((T, nb, S_W), lambda j: (0, j, C_Z // S_W)),
                  pl.BlockSpec((T, nb, S_CONV_CH), lambda j: (0, j, C_XBC // S_CONV_CH)),
                  pl.BlockSpec((T, nb, LANES), lambda j: (0, j, C_DT // LANES)),
                  pl.BlockSpec((S_CONV - 1, nb, S_CONV_CH), lambda j: (0, j, 0)),
                  h0spec,
                  const((S_CONV, S_CONV_CH)), const((1, S_CONV_CH)), const((1, S_W)), const((1, S_W)),
                  const((1, S_W)), const((1, S_W))],
        out_specs=[pl.BlockSpec((T, nb, S_W), lambda j: (0, j, 0)), hspec],
        out_shape=[jax.ShapeDtypeStruct((T, dbsz, S_W), F32),
                   jax.ShapeDtypeStruct((dbsz, N_PAIRS, 2 * S_HD, S_STATE), F32)],
        compiler_params=_cparams(("parallel",)),
        name="ssd_sample",
    )(proj3, proj3, proj3, pre3, h0, cw, cb, dtbv, alogv, dvec, ng)


def _rwkv_prep_kernel(u_ref, pre_ref, mu_ref, w0_ref, w2_ref, a0_ref, a2_ref, g2_ref, kk_ref, ka_ref,
                      r_out, w_out, k_out, v_out, kk_out, kka_out, g_out, ext_ref,
                      *, tm, P, stride, tiles_per_seq):
    i = pl.program_id(0)
    first = (i % tiles_per_seq) == 0

    @pl.when(first)
    def _():
        ext_ref[0:P, :] = pre_ref[...]

    @pl.when(jnp.logical_not(first))
    def _():
        ext_ref[0:P, :] = ext_ref[tm:tm + P, :]

    u = u_ref[...]
    ext_ref[P:P + tm, :] = u
    prev = ext_ref[P - stride:P - stride + tm, :]
    x = u + (prev - u) * mu_ref[...]
    r = x[:, 0:R_W]
    kr = x[:, R_W:2 * R_W]
    vr = x[:, 2 * R_W:3 * R_W]
    xl = x[:, 3 * R_W:]
    w_log = w0_ref[...] + _dot(jnp.tanh(xl), w2_ref[...])
    log_decay = -jnp.exp(-_softplus(-w_log) - 0.5)
    a = _sigmoid(a0_ref[...] + _dot(xl, a2_ref[...]))
    g = _dot(_sigmoid(xl), g2_ref[...])
    kk = kr * kk_ref[...]
    ss = _split_dot(kk * kk, _head_ones(R_W, 1.0), 2)
    kk = kk / jnp.maximum(jnp.sqrt(ss), 1e-12)
    r_out[...] = r
    w_out[...] = log_decay
    k_out[...] = kr * (1.0 + (a - 1.0) * ka_ref[...])
    v_out[...] = vr
    kk_out[...] = kk
    kka_out[...] = kk * a
    g_out[...] = g


def _rwkv_prep(proj, pre, mu, w0, w2p, a0, a2p, g2p, kkw, kaw, tm, stride, tiles_per_seq):
    t = proj.shape[0]
    P = pre.shape[0]
    const = lambda shape: pl.BlockSpec(shape, lambda i: (0,) * len(shape))
    outs = pl.pallas_call(
        functools.partial(_rwkv_prep_kernel, tm=tm, P=P, stride=stride, tiles_per_seq=tiles_per_seq),
        grid=(t // tm,),
        in_specs=[pl.BlockSpec((tm, R_IN_W), lambda i: (i, 0)),
                  const((P, R_IN_W)), const((1, R_IN_W)), const((1, R_W)), const((R_LORA, R_W)),
                  const((1, R_W)), const((R_LORA, R_W)), const((R_LORA, R_W)), const((1, R_W)), const((1, R_W))],
        out_specs=[pl.BlockSpec((tm, R_W), lambda i: (i, 0))] * 7,
        out_shape=[jax.ShapeDtypeStruct((t, R_W), F32)] * 7,
        scratch_shapes=[pltpu.VMEM((tm + 2 * P, R_IN_W), F32)],
        compiler_params=_cparams(("arbitrary",)),
        name="rwkv_prep",
    )(proj, pre, mu, w0, w2p, a0, a2p, g2p, kkw, kaw)
    return outs


def _pair_consts():
    lane = lax.broadcasted_iota(jnp.int32, (R_HD, LANES), 1)
    row = lax.broadcasted_iota(jnp.int32, (R_HD, LANES), 0)
    is_a = lane < HALF
    eye2 = (jnp.bitwise_and(lane, HALF - 1) == row).astype(F32)
    return is_a, eye2


def _unit_lower_inverses(ns):
    L = ns[0].shape[0]
    eye = (lax.broadcasted_iota(jnp.int32, (L, L), 0) == lax.broadcasted_iota(jnp.int32, (L, L), 1)).astype(F32)
    ts = [eye + n for n in ns]
    pws = [_dot(n, n) for n in ns]
    for _ in range(int(math.log2(L)) - 2):
        both = [_dot(jnp.concatenate([t, p], axis=0), p) for t, p in zip(ts, pws)]
        ts = [t + b[:L] for t, b in zip(ts, both)]
        pws = [b[L:] for b in both]
    return [t + _dot(t, p) for t, p in zip(ts, pws)]


def _rwkv_chunk_prompt_kernel(r_ref, lw_ref, k_ref, v_ref, kk_ref, kka_ref, y_ref, sout_ref, s_ref, *, L):
    c = pl.program_id(1)

    @pl.when(c == 0)
    def _():
        s_ref[...] = jnp.zeros_like(s_ref)

    lane = lax.broadcasted_iota(jnp.int32, (1, LANES), 1)
    is_a = lane < HALF
    rr = lax.broadcasted_iota(jnp.int32, (L, L), 0)
    cc = lax.broadcasted_iota(jnp.int32, (L, L), 1)
    incl = rr >= cc
    strict = rr > cc
    tri = jnp.where(incl, 1.0, 0.0).astype(F32)
    r2 = lax.broadcasted_iota(jnp.int32, (LANES, LANES), 0) < HALF
    c2 = lax.broadcasted_iota(jnp.int32, (LANES, LANES), 1) < HALF
    same_head = r2 == c2

    r, lw, k, v, kk, kka = (ref[...] for ref in (r_ref, lw_ref, k_ref, v_ref, kk_ref, kka_ref))
    cum = _split_dot(lw, tri, 3, x_is_lhs=False)
    last = cum[L - 1:L, :]
    inv_p = jnp.exp(-cum)
    to_end = jnp.exp(last - cum)
    b_t = kk * jnp.exp(cum - lw)
    a_t = -kka * inv_p
    k_t = k * inv_p
    r_t = r * jnp.exp(cum)
    a_end = -kka * to_end
    k_end = k * to_end
    decay_end = jnp.exp(last)
    pairs = [slice(pr * LANES, (pr + 1) * LANES) for pr in range(N_PAIRS)]
    halves = (is_a, jnp.logical_not(is_a))
    s0 = [s_ref[pr] for pr in range(N_PAIRS)]

    ns, mks, rak = [], [], []
    for sl in pairs:
        cols = jnp.concatenate([a_t[:, sl], k_t[:, sl]], axis=0)
        for half in halves:
            lhs = jnp.concatenate([jnp.where(half, b_t[:, sl], 0.0), jnp.where(half, r_t[:, sl], 0.0)], axis=0)
            g4 = _dot_nt(lhs, cols)
            ns.append(jnp.where(strict, g4[:L, :L], 0.0))
            mks.append(jnp.where(strict, g4[:L, L:], 0.0))
            rak.append(jnp.concatenate([jnp.where(incl, g4[L:, :L], 0.0), jnp.where(incl, g4[L:, L:], 0.0)], axis=1))
    ts = _unit_lower_inverses(ns)
    from_state = [_dot_nt(jnp.concatenate([b_t[:, sl], r_t[:, sl]], axis=0), s0[pr]) for pr, sl in enumerate(pairs)]
    rhs = []
    for pr, sl in enumerate(pairs):
        mv = _dot(jnp.concatenate([mks[2 * pr], mks[2 * pr + 1]], axis=0), v[:, sl])
        rhs.append(from_state[pr][:L] + jnp.where(is_a, mv[:L], mv[L:]))
    us = []
    for pr in range(N_PAIRS):
        tu = _dot(jnp.concatenate([ts[2 * pr], ts[2 * pr + 1]], axis=0), rhs[pr])
        us.append(jnp.where(is_a, tu[:L], tu[L:]))
    for pr, sl in enumerate(pairs):
        uv = jnp.concatenate([us[pr], v[:, sl]], axis=0)
        y_ref[:, sl] = from_state[pr][L:] + jnp.where(is_a, _dot(rak[2 * pr], uv), _dot(rak[2 * pr + 1], uv))
    for pr, sl in enumerate(pairs):
        upd = _dot_tn(jnp.concatenate([us[pr], v[:, sl]], axis=0),
                      jnp.concatenate([a_end[:, sl], k_end[:, sl]], axis=0))
        s_ref[pr] = s0[pr] * decay_end[:, sl] + jnp.where(same_head, upd, 0.0)

    @pl.when(c == pl.num_programs(1) - 1)
    def _():
        sout_ref[0] = s_ref[...]


def _rwkv_scan_prompt(seqs, bsz, seq, L):
    nc = seq // L
    spec = pl.BlockSpec((L, R_W), lambda b, c: (b * nc + c, 0))
    return pl.pallas_call(
        functools.partial(_rwkv_chunk_prompt_kernel, L=L),
        grid=(bsz, nc),
        in_specs=[spec] * 6,
        out_specs=[spec, pl.BlockSpec((1, N_PAIRS, LANES, LANES), lambda b, c: (b, 0, 0, 0))],
        out_shape=[jax.ShapeDtypeStruct((bsz * seq, R_W), F32),
                   jax.ShapeDtypeStruct((bsz, N_PAIRS, LANES, LANES), F32)],
        scratch_shapes=[pltpu.VMEM((N_PAIRS, LANES, LANES), F32)],
        compiler_params=_cparams(("parallel", "arbitrary")),
        name="rwkv_scan_prompt",
    )(*seqs)


def _repeat_rows(x, reps):
    return jnp.concatenate([jnp.broadcast_to(x[b:b + 1, :], (reps, x.shape[1])) for b in range(x.shape[0])], axis=0)


def _rwkv_scan_sample_kernel(r_ref, w_ref, k_ref, v_ref, kk_ref, kka_ref, s0_ref, y_ref, sout_ref, *, T, nb):
    _, eye2 = _pair_consts()
    eye_all = jnp.concatenate([eye2] * nb, axis=0)
    ones_bd = _head_ones(LANES, 1.0)

    def head_sums(x):
        return _split_dot(x, ones_bd, 2)

    states = [s0_ref[:, pr].reshape(nb * R_HD, LANES) for pr in range(N_PAIRS)]
    for t in range(T):
        for pr in range(N_PAIRS):
            sl = slice(pr * LANES, (pr + 1) * LANES)
            rr, ww, kr, vv, kk, kka = (_repeat_rows(x, R_HD) for x in (
                r_ref[t, :, sl], jnp.exp(w_ref[t, :, sl]), k_ref[t, :, sl], v_ref[t, :, sl],
                kk_ref[t, :, sl], kka_ref[t, :, sl]))
            v_col = head_sums(eye_all * vv)
            s = states[pr]
            s = s * ww - head_sums(s * kk) * kka + v_col * kr
            states[pr] = s
            y_col = head_sums(s * rr)
            y_ref[t, :, sl] = jnp.sum((eye_all * y_col).reshape(nb, R_HD, LANES), axis=1)
    for pr in range(N_PAIRS):
        sout_ref[:, pr] = states[pr].reshape(nb, R_HD, LANES)


def _rwkv_scan_sample(seqs3, s0, nb=16):
    T, dbsz = seqs3[0].shape[:2]
    spec = pl.BlockSpec((T, nb, R_W), lambda j: (0, j, 0))
    sspec = pl.BlockSpec((nb, N_PAIRS, R_HD, LANES), lambda j: (j, 0, 0, 0))
    return pl.pallas_call(
        functools.partial(_rwkv_scan_sample_kernel, T=T, nb=nb),
        grid=(dbsz // nb,),
        in_specs=[spec] * 6 + [sspec],
        out_specs=[spec, sspec],
        out_shape=[jax.ShapeDtypeStruct((T, dbsz, R_W), F32),
                   jax.ShapeDtypeStruct((dbsz, N_PAIRS, R_HD, LANES), F32)],
        compiler_params=_cparams(("parallel",)),
        name="rwkv_scan_sample",
    )(*seqs3, s0)


def _merge_kernel(x_ref, oa_ref, ob_ref, yr_ref, r_ref, k_ref, v_ref, g_ref, ga_ref, gb_ref, gc_ref,
                  bg_ref, lng_ref, lnb_ref, rk_ref, wpa_ref, wpb_ref, wpc_ref, wo_ref, o_ref):
    pa = _dot(oa_ref[...], wpa_ref[...])
    pb = _dot(ob_ref[...], wpb_ref[...])
    mean_m = _head_ones(R_W, 1.0 / R_HD)
    yr = yr_ref[...]
    bonus = _split_dot(r_ref[...] * k_ref[...] * rk_ref[...], _head_ones(R_W, 1.0), 2)
    d = yr - _split_dot(yr, mean_m, 2)
    var = _split_dot(d * d, mean_m, 2)
    yn = d * lax.rsqrt(var + GN_EPS) * lng_ref[...] + lnb_ref[...]
    oc = (yn + bonus * v_ref[...]) * g_ref[...]
    merged = (_sigmoid(ga_ref[...] + bg_ref[:, 0:D_MODEL]) * pa
              + _sigmoid(gb_ref[...] + bg_ref[:, D_MODEL:2 * D_MODEL]) * pb
              + _sigmoid(gc_ref[...] + bg_ref[:, 2 * D_MODEL:]) * _dot(oc, wpc_ref[...]))
    o_ref[...] = x_ref[...] + _dot(merged, wo_ref[...])


def _merge(x, proj, oa, ob, yr, r, k2, v, g, bg, lng, lnb, rk, wpa, wpb, wpc, wo, tm):
    t = x.shape[0]
    row = lambda w: pl.BlockSpec((tm, w), lambda i: (i, 0))
    const = lambda shape: pl.BlockSpec(shape, lambda i: (0,) * len(shape))
    gcol = C_GATE // D_MODEL
    gate = lambda j: pl.BlockSpec((tm, D_MODEL), lambda i: (i, gcol + j))
    return pl.pallas_call(
        _merge_kernel,
        grid=(t // tm,),
        in_specs=[row(D_MODEL), row(A_W), row(S_W), row(R_W), row(R_W), row(R_W), row(R_W), row(R_W),
                  gate(0), gate(1), gate(2),
                  const((1, 3 * D_MODEL)), const((1, R_W)), const((1, R_W)), const((1, R_W)),
                  const((A_W, D_MODEL)), const((S_W, D_MODEL)), const((R_W, D_MODEL)),
                  const((D_MODEL, D_MODEL))],
        out_specs=row(D_MODEL),
        out_shape=jax.ShapeDtypeStruct((t, D_MODEL), F32),
        compiler_params=_cparams(("parallel",)),
        name="merge",
    )(x, oa, ob, yr, r, k2, v, g, proj, proj, proj, bg, lng, lnb, rk, wpa, wpb, wpc, wo)


def _ffn_kernel(x_ref, g_ref, wug_ref, wuv_ref, wd_ref, cw_ref, cb_ref, pre_ref, gf_ref,
                o_ref, tail_ref, xn_ref, ext_ref, *, tm, P, stride, tiles_per_seq, final_norm):
    i = pl.program_id(0)
    f = pl.program_id(1)
    first = (i % tiles_per_seq) == 0

    @pl.when(f == 0)
    def _():
        x = x_ref[...]
        ms = jnp.mean(x * x, axis=-1, keepdims=True)
        xn_ref[...] = (x * lax.rsqrt(ms + NORM_EPS) * g_ref[...]).astype(BF16)

    @pl.when(first)
    def _():
        ext_ref[f, 0:P, :] = pre_ref[...]

    @pl.when(jnp.logical_not(first))
    def _():
        ext_ref[f, 0:P, :] = ext_ref[f, tm:tm + P, :]

    xn = xn_ref[...]
    ug = jnp.dot(xn, wug_ref[...], preferred_element_type=F32)
    uv = jnp.dot(xn, wuv_ref[...], preferred_element_type=F32)
    ext_ref[f, P:P + tm, :] = ug
    tail_ref[0] = ext_ref[f, tm:tm + P, :]
    acc = cb_ref[...] + cw_ref[F_CONV - 1:F_CONV, :] * ug
    for d in range(1, F_CONV):
        acc = acc + cw_ref[F_CONV - 1 - d:F_CONV - d, :] * ext_ref[f, P - d * stride:P - d * stride + tm, :]
    contrib = _dot(_silu(acc) * uv, wd_ref[...])

    @pl.when(f == 0)
    def _():
        o_ref[...] = x_ref[...] + contrib

    @pl.when(f > 0)
    def _():
        o_ref[...] = o_ref[...] + contrib

    if final_norm:
        @pl.when(f == pl.num_programs(1) - 1)
        def _():
            y = o_ref[...]
            ms = jnp.mean(y * y, axis=-1, keepdims=True)
            o_ref[...] = y * lax.rsqrt(ms + NORM_EPS) * gf_ref[...]


def _ffn(x, g, wup, wd, cw, cb, pre, gf, tm, tf, stride, tiles_per_seq, final_norm):
    t = x.shape[0]
    P = pre.shape[0]
    nf = D_FF // tf
    return pl.pallas_call(
        functools.partial(_ffn_kernel, tm=tm, P=P, stride=stride, tiles_per_seq=tiles_per_seq,
                          final_norm=final_norm),
        grid=(t // tm, nf),
        in_specs=[pl.BlockSpec((tm, D_MODEL), lambda i, f: (i, 0)),
                  pl.BlockSpec((1, D_MODEL), lambda i, f: (0, 0)),
                  pl.BlockSpec((D_MODEL, tf), lambda i, f: (0, f)),
                  pl.BlockSpec((D_MODEL, tf), lambda i, f: (0, nf + f)),
                  pl.BlockSpec((tf, D_MODEL), lambda i, f: (f, 0)),
                  pl.BlockSpec((F_CONV, tf), lambda i, f: (0, f)),
                  pl.BlockSpec((1, tf), lambda i, f: (0, f)),
                  pl.BlockSpec((P, tf), lambda i, f: (0, f)),
                  pl.BlockSpec((1, D_MODEL), lambda i, f: (0, 0))],
        out_specs=[pl.BlockSpec((tm, D_MODEL), lambda i, f: (i, 0)),
                   pl.BlockSpec((1, P, tf), lambda i, f: (i, 0, f))],
        out_shape=[jax.ShapeDtypeStruct((t, D_MODEL), F32),
                   jax.ShapeDtypeStruct((t // tm, P, D_FF), F32)],
        scratch_shapes=[pltpu.VMEM((tm, D_MODEL), BF16), pltpu.VMEM((nf, tm + 2 * P, tf), F32)],
        compiler_params=_cparams(("arbitrary", "arbitrary")),
        name="conv_ffn",
    )(x, g, wup, wup, wd, cw, cb, pre, gf)


def _pack_rwkv_state(s):
    n = s.shape[0]
    return s.reshape(n, N_PAIRS, 2, R_HD, R_HD).transpose(0, 1, 3, 2, 4).reshape(n, N_PAIRS, R_HD, LANES)


def _unpack_rwkv_state(s):
    n = s.shape[0]
    return s.reshape(n, N_PAIRS, R_HD, 2, R_HD).transpose(0, 1, 3, 2, 4).reshape(n, R_HEADS, R_HD, R_HD)


def _unpack_rwkv_blockdiag(s):
    n = s.shape[0]
    return jnp.stack([s[:, :, :R_HD, :R_HD], s[:, :, R_HD:, R_HD:]], axis=2).reshape(n, R_HEADS, R_HD, R_HD)


def _prep_layer_params(l, p):
    w_in = p['w_in'][l]
    c_dt_src = 3 * A_W + S_W + S_CONV_CH
    c_rw_src = c_dt_src + S_HEADS
    c_gate_src = c_rw_src + R_IN_W
    w_proj = jnp.concatenate([
        w_in[:, c_rw_src:c_gate_src],
        w_in[:, c_dt_src:c_rw_src], jnp.zeros((D_MODEL, C_Q - C_DT - S_HEADS), F32),
        w_in[:, :c_dt_src],
        w_in[:, c_gate_src:]], axis=1).astype(BF16)
    pad_lane = lambda v: jnp.pad(v, (0, LANES - v.shape[0])).reshape(1, LANES)
    zl = lambda r0, w: jnp.zeros((R_LORA, R_W), F32).at[r0:r0 + w.shape[0]].set(w).astype(BF16)
    return dict(
        norm1_g=p['norm1_g'][l].reshape(1, D_MODEL), w_proj=w_proj,
        b_gate=p['b_gate'][l].reshape(1, 3 * D_MODEL),
        w_pa=p['w_pa'][l].astype(BF16), w_pb=p['w_pb'][l].astype(BF16), w_pc=p['w_pc'][l].astype(BF16),
        w_o=p['w_o'][l].astype(BF16),
        ssm_conv_w=p['ssm_conv_w'][l], ssm_conv_b=p['ssm_conv_b'][l].reshape(1, S_CONV_CH),
        ssm_dt_bias=pad_lane(p['ssm_dt_bias'][l]), ssm_a_log=pad_lane(p['ssm_a_log'][l]),
        ssm_dvec=jnp.repeat(p['ssm_d'][l], S_HD).reshape(1, S_W),
        ssm_dtb_vec=jnp.repeat(p['ssm_dt_bias'][l], S_HD).reshape(1, S_W),
        ssm_alog_vec=jnp.repeat(p['ssm_a_log'][l], S_HD).reshape(1, S_W),
        ssm_norm_g=p['ssm_norm_g'][l].reshape(1, S_W),
        rw_mu=p['rw_mu'][l].reshape(1, R_IN_W), rw_w0=p['rw_w0'][l].reshape(1, R_W),
        rw_w2p=zl(0, p['rw_w2'][l]), rw_a0=p['rw_a0'][l].reshape(1, R_W),
        rw_a2p=zl(R_LORA_W, p['rw_a2'][l]), rw_g2p=zl(R_LORA_W + R_LORA_A, p['rw_g2'][l]),
        rw_kk=p['rw_kk'][l].reshape(1, R_W), rw_ka=p['rw_ka'][l].reshape(1, R_W),
        rw_rk=p['rw_rk'][l].reshape(1, R_W), rw_ln_g=p['rw_ln_g'][l].reshape(1, R_W),
        rw_ln_b=p['rw_ln_b'][l].reshape(1, R_W),
        norm2_g=p['norm2_g'][l].reshape(1, D_MODEL), w_up=p['w_up'][l].astype(BF16),
        w_down=p['w_down'][l].astype(BF16), ffn_conv_w=p['ffn_conv_w'][l],
        ffn_conv_b=p['ffn_conv_b'][l].reshape(1, D_FF))


def _row_tile(t):
    for tm in (512, 256, 128):
        if t % tm == 0:
            return tm
    raise ValueError(t)


def _ffn_tf():
    return D_FF // 2


def _prompt_layer(x, lp, bsz, seq, slopes_pair, gf, final_norm):
    t = bsz * seq
    tm = _row_tile(seq)
    tiles = seq // tm
    proj = _rms_matmul(x, lp['norm1_g'], lp['w_proj'], 1024 if t % 1024 == 0 else tm, 2048)
    proj3 = proj.reshape(bsz, seq, N_PROJ)
    oa = _moba_prompt(proj3, slopes_pair).reshape(t, A_W)
    ob, ssm_new = _ssd_prompt(proj, bsz, seq, lp['ssm_conv_w'], lp['ssm_conv_b'], lp['ssm_dt_bias'],
                              lp['ssm_a_log'], lp['ssm_dvec'], lp['ssm_norm_g'])
    r, w, k2, v, kk, kka, g = _rwkv_prep(
        proj, jnp.zeros((SUBLANES, R_IN_W), F32), lp['rw_mu'], lp['rw_w0'], lp['rw_w2p'], lp['rw_a0'],
        lp['rw_a2p'], lp['rw_g2p'], lp['rw_kk'], lp['rw_ka'], tm, 1, tiles)
    yr, rw_new = _rwkv_scan_prompt((r, w, k2, v, kk, kka), bsz, seq, LANES)
    x = _merge(x, proj, oa, ob, yr, r, k2, v, g, lp['b_gate'], lp['rw_ln_g'], lp['rw_ln_b'], lp['rw_rk'],
               lp['w_pa'], lp['w_pb'], lp['w_pc'], lp['w_o'], min(tm, 256))
    x, tail = _ffn(x, lp['norm2_g'], lp['w_up'], lp['w_down'], lp['ffn_conv_w'], lp['ffn_conv_b'],
                   jnp.zeros((SUBLANES, D_FF), F32), gf, tm, _ffn_tf(), 1, tiles, final_norm)
    k_new = proj3[:, :, C_K:C_K + A_W].reshape(bsz, seq, A_HEADS, A_HD)
    v_new = proj3[:, :, C_V:C_V + A_W].reshape(bsz, seq, A_HEADS, A_HD)
    ssm_conv_new = proj3[:, seq - (S_CONV - 1):, C_XBC:C_XBC + S_CONV_CH]
    shift_new = proj3[:, seq - 1:, C_RW:C_RW + R_IN_W]
    ffn_conv_new = tail.reshape(bsz, tiles, SUBLANES, D_FF)[:, tiles - 1, SUBLANES - (F_CONV - 1):]
    state = (k_new, v_new, ssm_new.reshape(bsz, S_HEADS, S_HD, S_STATE), ssm_conv_new,
             _unpack_rwkv_blockdiag(rw_new), shift_new, ffn_conv_new)
    return x, state


def _sample_layer(x, lp, dbsz, tnew, st, cache_k4, cache_v4, page_table, layer, gf, final_norm):
    t = tnew * dbsz
    ssm0, ssm_conv0, rwkv0, shift0, ffn_conv0 = st
    proj = _rms_matmul(x, lp['norm1_g'], lp['w_proj'], t, 1024)
    proj3 = proj.reshape(tnew, dbsz, N_PROJ)

    qkv = proj3[:, :, C_Q:C_Q + 3 * A_W].transpose(1, 0, 2)
    qkv8 = jnp.pad(qkv, ((0, 0), (0, SUBLANES - tnew), (0, 0)))
    oa8 = _moba_sample(qkv8[:, :, :A_W], qkv8[:, :, A_W:2 * A_W], qkv8[:, :, 2 * A_W:],
                       cache_k4, cache_v4, page_table, layer, tnew)
    oa = oa8[:, :tnew].transpose(1, 0, 2).reshape(t, A_W)

    h0 = ssm0.reshape(-1, dbsz, N_PAIRS, 2 * S_HD, S_STATE)
    ob3, ssm_new = _ssd_sample(proj3, ssm_conv0.transpose(1, 0, 2), h0, layer, lp['ssm_conv_w'],
                               lp['ssm_conv_b'], lp['ssm_dtb_vec'], lp['ssm_alog_vec'], lp['ssm_dvec'],
                               lp['ssm_norm_g'])
    ob = ob3.reshape(t, S_W)
    ssm_new = ssm_new.reshape(dbsz, S_HEADS, S_HD, S_STATE)

    r, w, k2, v, kk, kka, g = _rwkv_prep(
        proj, shift0.reshape(dbsz, R_IN_W), lp['rw_mu'], lp['rw_w0'], lp['rw_w2p'], lp['rw_a0'],
        lp['rw_a2p'], lp['rw_g2p'], lp['rw_kk'], lp['rw_ka'], t, dbsz, 1)
    to3 = lambda a: a.reshape(tnew, dbsz, R_W)
    yr3, rw_new = _rwkv_scan_sample(tuple(to3(a) for a in (r, w, k2, v, kk, kka)), _pack_rwkv_state(rwkv0))
    yr = yr3.reshape(t, R_W)

    x = _merge(x, proj, oa, ob, yr, r, k2, v, g, lp['b_gate'], lp['rw_ln_g'], lp['rw_ln_b'], lp['rw_rk'],
               lp['w_pa'], lp['w_pb'], lp['w_pc'], lp['w_o'], min(t, 256))
    pre = ffn_conv0.transpose(1, 0, 2).reshape((F_CONV - 1) * dbsz, D_FF)
    x, tail = _ffn(x, lp['norm2_g'], lp['w_up'], lp['w_down'], lp['ffn_conv_w'], lp['ffn_conv_b'],
                   pre, gf, t, _ffn_tf(), dbsz, 1, final_norm)

    k_new = qkv[:, :, A_W:2 * A_W].reshape(dbsz, tnew, A_HEADS, A_HD)
    v_new = qkv[:, :, 2 * A_W:].reshape(dbsz, tnew, A_HEADS, A_HD)
    ssm_conv_new = proj3[tnew - (S_CONV - 1):, :, C_XBC:C_XBC + S_CONV_CH].transpose(1, 0, 2)
    shift_new = proj3[tnew - 1:, :, C_RW:C_RW + R_IN_W].transpose(1, 0, 2)
    ffn_conv_new = tail.reshape(F_CONV - 1, dbsz, D_FF).transpose(1, 0, 2)
    state = (k_new, v_new, ssm_new, ssm_conv_new, _unpack_rwkv_state(rw_new), shift_new, ffn_conv_new)
    return x, state


def kernel(x_prompt, x_sample, cache_k, cache_v, state_ssm, state_ssm_conv, state_rwkv, state_rwkv_shift, state_ffn_conv, page_table, norm1_g, w_in, b_gate, w_pa, ssm_conv_w, ssm_conv_b, ssm_dt_bias, ssm_a_log, ssm_d, ssm_norm_g, w_pb, rw_mu, rw_w0, rw_w2, rw_a0, rw_a2, rw_g2, rw_kk, rw_ka, rw_rk, rw_ln_g, rw_ln_b, w_pc, w_o, norm2_g, w_up, ffn_conv_w, ffn_conv_b, w_down, norm_f_g):
    params = dict(norm1_g=norm1_g, w_in=w_in, b_gate=b_gate, w_pa=w_pa, ssm_conv_w=ssm_conv_w,
                  ssm_conv_b=ssm_conv_b, ssm_dt_bias=ssm_dt_bias, ssm_a_log=ssm_a_log, ssm_d=ssm_d,
                  ssm_norm_g=ssm_norm_g, w_pb=w_pb, rw_mu=rw_mu, rw_w0=rw_w0, rw_w2=rw_w2, rw_a0=rw_a0,
                  rw_a2=rw_a2, rw_g2=rw_g2, rw_kk=rw_kk, rw_ka=rw_ka, rw_rk=rw_rk, rw_ln_g=rw_ln_g,
                  rw_ln_b=rw_ln_b, w_pc=w_pc, w_o=w_o, norm2_g=norm2_g, w_up=w_up, ffn_conv_w=ffn_conv_w,
                  ffn_conv_b=ffn_conv_b, w_down=w_down)
    depth = w_in.shape[0]
    bsz, seq, _ = x_prompt.shape
    dbsz, tnew, _ = x_sample.shape
    head = jnp.arange(A_HEADS, dtype=F32) + 1.0
    slopes = jnp.exp2(-8.0 * head / A_HEADS)
    slopes_pair = jnp.repeat(slopes, A_HD).reshape(N_PAIRS, 1, LANES)
    cache_k4 = cache_k.transpose(0, 1, 3, 4, 2)
    cache_v4 = cache_v.transpose(0, 1, 3, 4, 2)
    gf = norm_f_g.reshape(1, D_MODEL)

    hp = x_prompt.reshape(bsz * seq, D_MODEL)
    hs = x_sample.transpose(1, 0, 2).reshape(tnew * dbsz, D_MODEL)
    new_p = [[] for _ in range(7)]
    new_s = [[] for _ in range(7)]
    for l in range(depth):
        lp = _prep_layer_params(l, params)
        last = l == depth - 1
        hp, sp = _prompt_layer(hp, lp, bsz, seq, slopes_pair, gf, last)
        st = (state_ssm, state_ssm_conv[l], state_rwkv[l], state_rwkv_shift[l], state_ffn_conv[l])
        hs, ss = _sample_layer(hs, lp, dbsz, tnew, st, cache_k4, cache_v4, page_table, l, gf, last)
        for j in range(7):
            new_p[j].append(sp[j])
            new_s[j].append(ss[j])
    y_prompt = hp.reshape(bsz, seq, D_MODEL)
    y_sample = hs.reshape(tnew, dbsz, D_MODEL).transpose(1, 0, 2)
    outs = [y_prompt, y_sample]
    for j in range(7):
        outs += [jnp.stack(new_p[j]), jnp.stack(new_s[j])]
    return tuple(outs)
```

```python
import functools
import math

import jax
import jax.numpy as jnp
from jax import lax
from jax.experimental import pallas as pl
from jax.experimental.pallas import tpu as pltpu

F32 = jnp.float32
BF16 = jnp.bfloat16

D_MODEL = 1024
A_HEADS = 8
A_HD = 64
A_W = A_HEADS * A_HD
MOBA_BLOCK = 256
MOBA_TOPK = 3
Q_BLOCK = MOBA_BLOCK
ATTN_SCALE = A_HD ** -0.5
S_HEADS = 8
S_HD = 64
S_W = S_HEADS * S_HD
S_GROUPS = 2
S_STATE = 128
S_CONV = 4
S_CONV_CH = S_W + 2 * S_GROUPS * S_STATE
SSD_CHUNK = 128
R_HEADS = 8
R_HD = 64
R_W = R_HEADS * R_HD
R_LORA_W = 64
R_LORA_A = 64
R_LORA_G = 128
R_LORA = R_LORA_W + R_LORA_A + R_LORA_G
R_IN_W = 3 * R_W + R_LORA
D_FF = ((8 * D_MODEL // 3 + 127) // 128) * 128
F_CONV = 3
NORM_EPS = 1e-6
GN_EPS = 64e-5
NEG_INF = -1e30

LANES = 128
SUBLANES = 8
HALF = 64
HALF_SHIFT = 6
SUBLANE_SHIFT = 3
N_PAIRS = 4

C_RW = 0
C_DT = R_IN_W
C_Q = 2048
C_K = C_Q + A_W
C_V = C_K + A_W
C_Z = C_V + A_W
C_XBC = C_Z + S_W
C_GATE = C_XBC + S_CONV_CH
N_PROJ = C_GATE + 3 * D_MODEL

VMEM_LIMIT = 56 * 1024 * 1024


def _cparams(sem):
    return pltpu.CompilerParams(dimension_semantics=sem, vmem_limit_bytes=VMEM_LIMIT)


def _dot(a, b):
    return jnp.dot(a.astype(BF16), b.astype(BF16), preferred_element_type=F32)


def _dot_nt(a, b):
    return lax.dot_general(a.astype(BF16), b.astype(BF16), (((1,), (1,)), ((), ())),
                           preferred_element_type=F32)


def _dot_tn(a, b):
    return lax.dot_general(a.astype(BF16), b.astype(BF16), (((0,), (0,)), ((), ())),
                           preferred_element_type=F32)


def _split_dot(x, w, passes, x_is_lhs=True):
    w = w.astype(BF16)
    acc = None
    rem = x
    for _ in range(passes):
        piece = rem.astype(BF16)
        term = (jnp.dot(piece, w, preferred_element_type=F32) if x_is_lhs
                else jnp.dot(w, piece, preferred_element_type=F32))
        acc = term if acc is None else acc + term
        rem = rem - piece.astype(F32)
    return acc


def _dot_nt_hi(a, b):
    return lax.dot_general(a, b, (((1,), (1,)), ((), ())), preferred_element_type=F32,
                           precision=lax.Precision.HIGHEST)


def _sigmoid(x):
    return 1.0 / (1.0 + jnp.exp(-x))


def _silu(x):
    return x * _sigmoid(x)


def _softplus(x):
    return jnp.maximum(x, 0.0) + jnp.log(1.0 + jnp.exp(-jnp.abs(x)))


def _head_ones(width, scale):
    r = lax.shift_right_logical(lax.broadcasted_iota(jnp.int32, (width, width), 0), HALF_SHIFT)
    c = lax.shift_right_logical(lax.broadcasted_iota(jnp.int32, (width, width), 1), HALF_SHIFT)
    return jnp.where(r == c, scale, 0.0).astype(F32)


def _rms_matmul_kernel(x_ref, g_ref, w_ref, o_ref, xn_ref):
    @pl.when(pl.program_id(1) == 0)
    def _():
        x = x_ref[...]
        ms = jnp.mean(x * x, axis=-1, keepdims=True)
        xn_ref[...] = (x * lax.rsqrt(ms + NORM_EPS) * g_ref[...]).astype(BF16)

    o_ref[...] = jnp.dot(xn_ref[...], w_ref[...], preferred_element_type=F32)


def _rms_matmul(x, g, w, tm, tn):
    t, d = x.shape
    n = w.shape[1]
    return pl.pallas_call(
        _rms_matmul_kernel,
        grid=(t // tm, n // tn),
        in_specs=[pl.BlockSpec((tm, d), lambda i, j: (i, 0)),
                  pl.BlockSpec((1, d), lambda i, j: (0, 0)),
                  pl.BlockSpec((d, tn), lambda i, j: (0, j))],
        out_specs=pl.BlockSpec((tm, tn), lambda i, j: (i, j)),
        out_shape=jax.ShapeDtypeStruct((t, n), F32),
        scratch_shapes=[pltpu.VMEM((tm, d), BF16)],
        compiler_params=_cparams(("parallel", "arbitrary")),
        name="in_proj",
    )(x, g, w)


def _topk_bias(gate, n_valid, axis=1):
    pos = lax.broadcasted_iota(jnp.int32, gate.shape, axis)
    pos_f = pos.astype(F32)
    gm = jnp.where(pos < n_valid, gate, NEG_INF)
    selected = jnp.zeros(gate.shape, jnp.bool_)
    for j in range(MOBA_TOPK):
        m = jnp.max(gm, axis=axis, keepdims=True)
        idx = jnp.min(jnp.where(gm == m, pos_f, 1e9), axis=axis, keepdims=True)
        hit = pos_f == idx
        selected = jnp.logical_or(selected, jnp.logical_and(hit, j < n_valid))
        gm = jnp.where(hit, -jnp.inf, gm)
    return jnp.where(selected, 0.0, NEG_INF).astype(F32)


def _bf16_pieces(x, n):
    out = []
    rem = x
    for _ in range(n):
        piece = rem.astype(BF16).astype(F32)
        out.append(piece)
        rem = rem - piece
    return out


def _moba_prompt_kernel(q_ref, k_ref, v_ref, sl_ref, o_ref, kt_out, vt_out, kf_ref, vth_ref, kmean_ref, sel_ref,
                        acc_ref, raw_a, raw_b, m_ref, knorm_ref, *, nb):
    i = pl.program_id(2)
    B = MOBA_BLOCK
    n_piece = 3
    lane = lax.broadcasted_iota(jnp.int32, (1, LANES), 1)
    halves = (lane < HALF, lane >= HALF)

    @pl.when(i == 0)
    def _():
        kmean_ref[...] = jnp.zeros_like(kmean_ref)
        key_local = jnp.bitwise_and(lax.broadcasted_iota(jnp.int32, (k_ref.shape[1], 1), 0), B - 1).astype(F32)
        extra = jnp.where(lane < n_piece, key_local, jnp.where(lane < 2 * n_piece, 1.0, 0.0))
        kf_ref[...] = jnp.concatenate([k_ref[0], extra], axis=1).astype(BF16)
        ones_rows = jnp.ones((SUBLANES, B), F32)
        knorm = [jnp.zeros((1, LANES), F32), jnp.zeros((1, LANES), F32)]
        for n in range(nb):
            kblk = k_ref[0, n * B:(n + 1) * B, :]
            kmean_ref[n:n + 1, :] = jnp.mean(kblk, axis=0, keepdims=True)
            vt = v_ref[0, n * B:(n + 1) * B, :].T
            kt_out[0, :, n * B:(n + 1) * B] = kblk.T
            vt_out[0, :, n * B:(n + 1) * B] = vt
            for h in range(2):
                vth_ref[h, n] = jnp.concatenate([vt[h * HALF:(h + 1) * HALF], ones_rows], axis=0).astype(BF16)
                norm2 = jnp.sum(jnp.where(halves[h], kblk * kblk, 0.0), axis=-1, keepdims=True)
                knorm[h] = jnp.where(lane == n, jnp.max(norm2, axis=0, keepdims=True), knorm[h])
        for h in range(2):
            knorm_ref[h] = jnp.broadcast_to(knorm[h], (SUBLANES, LANES))

    own = i
    q = q_ref[0]
    log2e = 1.0 / math.log(2.0)
    slope2 = (sl_ref[0, :, 0:1] * log2e, sl_ref[0, :, HALF:HALF + 1] * log2e)
    kmean = kmean_ref[...]
    q_local = lax.broadcasted_iota(jnp.int32, (Q_BLOCK, 1), 0).astype(F32)
    q_rows = []
    for h in range(2):
        q_m = jnp.where(halves[h], q, 0.0)
        bias_t = _topk_bias(_dot_nt_hi(kmean, q_m), own, axis=0)
        for n in range(nb):
            sel_ref[h, n] = jnp.broadcast_to(bias_t[n:n + 1, :], (SUBLANES, Q_BLOCK))
        pieces = _bf16_pieces(slope2[h], n_piece) + _bf16_pieces(-slope2[h] * q_local, n_piece)
        extra = jnp.zeros((Q_BLOCK, LANES), F32)
        for j, piece in enumerate(pieces):
            extra = jnp.where(lane == j, piece, extra)
        q_rows.append(jnp.concatenate([q_m * (ATTN_SCALE * log2e), extra], axis=1))
    qf = jnp.concatenate(q_rows, axis=0).astype(BF16)

    rel = (lax.broadcasted_iota(jnp.int32, (B, Q_BLOCK), 0) - lax.broadcasted_iota(jnp.int32, (B, Q_BLOCK), 1))

    def scores(n):
        both = _dot_nt(kf_ref[pl.ds(pl.multiple_of(n * B, B), B), :], qf)
        return [both[:, h * Q_BLOCK:(h + 1) * Q_BLOCK] for h in range(2)]

    def put_scores(n, dst_ref):
        for h, s in enumerate(scores(n)):
            dst_ref[h] = s

    def row_of(ref, h):
        return ref[h][0:1, :]

    def put_row(ref, h, x):
        ref[h] = jnp.broadcast_to(x, (SUBLANES, Q_BLOCK))

    def attend(n, src_ref):
        shift = ((n - i) * B).astype(F32)
        ps, alphas = [], []
        for h in range(2):
            s = src_ref[h] + (sel_ref[h, n][0:1, :] + slope2[h] * shift)
            m0 = row_of(m_ref, h)
            m1 = jnp.maximum(m0, jnp.max(s, axis=0, keepdims=True))
            alphas.append(jnp.exp2(m0 - m1))
            ps.append(jnp.exp2(s - m1).astype(BF16))
            put_row(m_ref, h, m1)
        pv = [jnp.dot(vth_ref[h, n], ps[h], preferred_element_type=F32) for h in range(2)]
        for h in range(2):
            acc_ref[h] = alphas[h] * acc_ref[h] + pv[h]

    own_raw = scores(own)
    lane_f = lane.astype(F32)
    bound_c = ATTN_SCALE * log2e * 1.02
    skips = []
    for h in range(2):
        q_m = jnp.where(halves[h], q, 0.0)
        qn2 = jnp.max(jnp.sum(q_m * q_m, axis=-1, keepdims=True), axis=0, keepdims=True)
        ub = jnp.sqrt(qn2 * knorm_ref[h][0:1, :]) * bound_c
        ub_own = jnp.sum(jnp.where(lane == own, ub, 0.0), axis=-1, keepdims=True)
        far = slope2[h] * ((lane_f - own.astype(F32)) * B + (B - 1))
        skips.append(ub + far < -ub_own - 160.0)
    keep = jnp.logical_and(lane < own, jnp.logical_not(jnp.logical_and(skips[0], skips[1])))
    first = jnp.min(jnp.where(keep, lane_f, own.astype(F32))).astype(jnp.int32)
    n_visit = own - first

    put_scores(jnp.minimum(first, nb - 1), raw_a)

    own_p = []
    for h in range(2):
        s = jnp.where(rel <= 0, own_raw[h], NEG_INF)
        m = jnp.max(s, axis=0, keepdims=True)
        put_row(m_ref, h, m)
        own_p.append(jnp.exp2(s - m).astype(BF16))
    for h in range(2):
        acc_ref[h] = jnp.dot(vth_ref[h, own], own_p[h], preferred_element_type=F32)

    def body(j, carry):
        n0 = first + 2 * j
        put_scores(n0 + 1, raw_b)
        attend(n0, raw_a)
        put_scores(jnp.minimum(n0 + 2, nb - 1), raw_a)
        attend(n0 + 1, raw_b)
        return carry

    lax.fori_loop(0, n_visit // 2, body, 0)

    @pl.when(n_visit % 2 == 1)
    def _():
        attend(own - 1, raw_a)

    out_t = jnp.concatenate([acc_ref[h][0:HALF] / acc_ref[h][HALF:HALF + 1] for h in range(2)], axis=0)
    o_ref[0] = out_t.T


def _moba_prompt(proj3, slopes_pair):
    bsz, seq, _ = proj3.shape
    assert seq % MOBA_BLOCK == 0 and seq // MOBA_BLOCK >= MOBA_TOPK
    nb = seq // MOBA_BLOCK
    nbp = -(-nb // SUBLANES) * SUBLANES
    nq = seq // Q_BLOCK
    qc, kc, vc = C_Q // LANES, C_K // LANES, C_V // LANES
    return pl.pallas_call(
        functools.partial(_moba_prompt_kernel, nb=nb),
        grid=(bsz, N_PAIRS, nq),
        in_specs=[pl.BlockSpec((1, Q_BLOCK, LANES), lambda b, p, i: (b, i, qc + p)),
                  pl.BlockSpec((1, seq, LANES), lambda b, p, i: (b, 0, kc + p)),
                  pl.BlockSpec((1, seq, LANES), lambda b, p, i: (b, 0, vc + p)),
                  pl.BlockSpec((1, 1, LANES), lambda b, p, i: (p, 0, 0))],
        out_specs=[pl.BlockSpec((1, Q_BLOCK, LANES), lambda b, p, i: (b, i, p)),
                   pl.BlockSpec((1, LANES, seq), lambda b, p, i: (b, p, 0)),
                   pl.BlockSpec((1, LANES, seq), lambda b, p, i: (b, p, 0))],
        out_shape=[jax.ShapeDtypeStruct((bsz, seq, A_W), F32),
                   jax.ShapeDtypeStruct((bsz, A_W, seq), F32), jax.ShapeDtypeStruct((bsz, A_W, seq), F32)],
        scratch_shapes=[pltpu.VMEM((seq, 2 * LANES), BF16),
                        pltpu.VMEM((2, nb, HALF + SUBLANES, MOBA_BLOCK), BF16),
                        pltpu.VMEM((nbp, LANES), F32), pltpu.VMEM((2, nb, SUBLANES, Q_BLOCK), F32),
                        pltpu.VMEM((2, HALF + SUBLANES, Q_BLOCK), F32),
                        pltpu.VMEM((2, MOBA_BLOCK, Q_BLOCK), F32), pltpu.VMEM((2, MOBA_BLOCK, Q_BLOCK), F32),
                        pltpu.VMEM((2, SUBLANES, Q_BLOCK), F32), pltpu.VMEM((2, SUBLANES, LANES), F32)],
        compiler_params=_cparams(("parallel", "parallel", "arbitrary")),
        name="moba_prompt",
    )(proj3, proj3, proj3, slopes_pair)


def _moba_sample_kernel(pt_ref, q_ref, kn_ref, vn_ref, *refs, n_blk, ppb, page, tnew):
    del pt_ref
    n_pages = n_blk * ppb
    k_refs, v_refs = refs[:n_pages], refs[n_pages:2 * n_pages]
    o_ref, m_ref, l_ref, g_ref, acc_ref = refs[2 * n_pages:]
    past = n_blk * MOBA_BLOCK
    rows = A_HEADS * SUBLANES
    lane_head = lax.shift_right_logical(lax.broadcasted_iota(jnp.int32, (SUBLANES, A_W), 1), HALF_SHIFT)
    q8 = q_ref[0] * ATTN_SCALE
    qbd = jnp.concatenate([jnp.where(lane_head == h, q8, 0.0) for h in range(A_HEADS)], axis=0)
    q_hi = qbd.astype(BF16)
    q_lo = (qbd - q_hi.astype(F32)).astype(BF16)
    r = lax.broadcasted_iota(jnp.int32, (rows, 1), 0)
    tok = jnp.bitwise_and(r, SUBLANES - 1).astype(F32)
    slope = jnp.exp2(-(8.0 / A_HEADS) * (lax.shift_right_logical(r, SUBLANE_SHIFT) + 1).astype(F32))
    key = lax.broadcasted_iota(jnp.int32, (1, MOBA_BLOCK), 1).astype(F32)

    def block_t(refs_, n):
        return jnp.concatenate([refs_[j][0, 0].reshape(A_W, page) for j in range(n * ppb, (n + 1) * ppb)],
                               axis=1).astype(BF16)

    raws, fixes = [], []
    for n in range(n_blk):
        kt = block_t(k_refs, n)
        raws.append(jnp.dot(q_hi, kt, preferred_element_type=F32))
        fixes.append(jnp.dot(q_lo, kt, preferred_element_type=F32))
    es = []
    for n in range(n_blk):
        g_ref[n] = jnp.broadcast_to(jnp.sum(raws[n] + fixes[n], axis=-1, keepdims=True) * (1.0 / MOBA_BLOCK),
                                    (rows, LANES))
        s = raws[n] - slope * ((past - n * MOBA_BLOCK + tok) - key)
        m = jnp.max(s, axis=-1, keepdims=True)
        e = jnp.exp(s - m)
        m_ref[n] = jnp.broadcast_to(m, (rows, LANES))
        l_ref[n] = jnp.broadcast_to(jnp.sum(e, axis=-1, keepdims=True), (rows, LANES))
        es.append(e.astype(BF16))
    for n in range(n_blk):
        acc_ref[n] = lax.dot_general(es[n], block_t(v_refs, n), (((1,), (1,)), ((), ())),
                                     preferred_element_type=F32)

    lane = lax.broadcasted_iota(jnp.int32, (rows, LANES), 1)
    gate = jnp.zeros((rows, LANES), F32)
    for j in range(n_blk):
        gate = jnp.where(lane == j, g_ref[j], gate)
    bias = _topk_bias(gate, n_blk)

    kn = kn_ref[0]
    vn = vn_ref[0]
    s_own = []
    for j in range(tnew):
        sj = jnp.sum(qbd * kn[j:j + 1, :], axis=-1, keepdims=True) - slope * (tok - j)
        s_own.append(jnp.where(tok >= j, sj, NEG_INF))
    mx = s_own[0]
    for j in range(1, tnew):
        mx = jnp.maximum(mx, s_own[j])
    mb = []
    for j in range(n_blk):
        mj = m_ref[j][:, 0:1] + bias[:, j:j + 1]
        mb.append(mj)
        mx = jnp.maximum(mx, mj)
    lsum = jnp.zeros((rows, 1), F32)
    acc = jnp.zeros((rows, A_W), F32)
    for j in range(tnew):
        w = jnp.exp(s_own[j] - mx)
        lsum = lsum + w
        acc = acc + w * vn[j:j + 1, :]
    for j in range(n_blk):
        w = jnp.exp(mb[j] - mx)
        lsum = lsum + w * l_ref[j][:, 0:1]
        acc = acc + w * acc_ref[j]
    out = acc / lsum
    o8 = jnp.zeros((SUBLANES, A_W), F32)
    for h in range(A_HEADS):
        o8 = o8 + jnp.where(lane_head == h, out[h * SUBLANES:(h + 1) * SUBLANES, :], 0.0)
    o_ref[0] = o8


def _moba_sample(q8, k8, v8, cache_kt, cache_vt, page_table, layer, tnew):
    dbsz = q8.shape[0]
    n_pages = page_table.shape[1]
    page = cache_kt.shape[4]
    assert MOBA_BLOCK % page == 0 and (n_pages * page) % MOBA_BLOCK == 0 and page % LANES == 0
    ppb = MOBA_BLOCK // page
    n_blk = n_pages // ppb
    assert n_blk >= MOBA_TOPK and tnew <= SUBLANES
    rows = A_HEADS * SUBLANES
    tok_spec = pl.BlockSpec((1, SUBLANES, A_W), lambda b, pt: (b, 0, 0))

    def page_spec(j):
        return pl.BlockSpec((1, 1, A_HEADS, A_HD, page), lambda b, pt: (layer, pt[b * n_pages + j], 0, 0, 0))

    pages = [page_spec(j) for j in range(n_pages)]
    grid_spec = pltpu.PrefetchScalarGridSpec(
        num_scalar_prefetch=1,
        grid=(dbsz,),
        in_specs=[tok_spec, tok_spec, tok_spec] + pages + pages,
        out_specs=tok_spec,
        scratch_shapes=[pltpu.VMEM((n_blk, rows, LANES), F32), pltpu.VMEM((n_blk, rows, LANES), F32),
                        pltpu.VMEM((n_blk, rows, LANES), F32), pltpu.VMEM((n_blk, rows, A_W), F32)],
    )
    return pl.pallas_call(
        functools.partial(_moba_sample_kernel, n_blk=n_blk, ppb=ppb, page=page, tnew=tnew),
        grid_spec=grid_spec,
        out_shape=jax.ShapeDtypeStruct((dbsz, SUBLANES, A_W), F32),
        compiler_params=_cparams(("parallel",)),
        name="moba_sample",
    )(page_table.reshape(-1), q8, k8, v8, *([cache_kt] * n_pages), *([cache_vt] * n_pages))


def _ssd_prompt_kernel(z_ref, xbc_ref, dt_ref, cw_ref, cb_ref, dtb_ref, alog_ref, dvec_ref, ng_ref,
                       o_ref, hout_ref, ext_ref, h_ref, *, L):
    c = pl.program_id(1)
    P = SUBLANES

    @pl.when(c == 0)
    def _():
        ext_ref[0:P, :] = jnp.zeros((P, S_CONV_CH), F32)
        h_ref[...] = jnp.zeros_like(h_ref)

    @pl.when(c > 0)
    def _():
        ext_ref[0:P, :] = ext_ref[L:L + P, :]

    ext_ref[P:P + L, :] = xbc_ref[...]
    acc = cb_ref[...] + cw_ref[S_CONV - 1:S_CONV, :] * ext_ref[P:P + L, :]
    for d in range(1, S_CONV):
        acc = acc + cw_ref[S_CONV - 1 - d:S_CONV - d, :] * ext_ref[P - d:P - d + L, :]
    xbc = _silu(acc)
    xs = xbc[:, :S_W]
    bm = xbc[:, S_W:S_W + S_GROUPS * S_STATE]
    cm = xbc[:, S_W + S_GROUPS * S_STATE:]

    lane = lax.broadcasted_iota(jnp.int32, (1, LANES), 1)
    is_a = lane < HALF
    dt = jnp.where(lane < S_HEADS, _softplus(dt_ref[...] + dtb_ref[...]), 0.0)
    da = dt * (-jnp.exp(alog_ref[...]))
    rr = lax.broadcasted_iota(jnp.int32, (L, L), 0)
    cc = lax.broadcasted_iota(jnp.int32, (L, L), 1)
    causal = rr >= cc
    cum = _split_dot(da, jnp.where(causal, 1.0, 0.0), 3, x_is_lhs=False)
    cum_t = cum.T
    dt_t = dt.T
    row_a = lax.broadcasted_iota(jnp.int32, (2 * S_HD, 1), 0) < S_HD

    group_of = [(2 * pr * S_GROUPS) // S_HEADS for pr in range(N_PAIRS)]
    bgs = [bm[:, g * S_STATE:(g + 1) * S_STATE] for g in range(S_GROUPS)]
    cgs = [cm[:, g * S_STATE:(g + 1) * S_STATE] for g in range(S_GROUPS)]
    cb_g = [_dot_nt(cgs[g], bgs[g]) for g in range(S_GROUPS)]
    xs_ps = [xs[:, pr * LANES:(pr + 1) * LANES] for pr in range(N_PAIRS)]
    hps = [h_ref[pr] for pr in range(N_PAIRS)]
    halves = (is_a, jnp.logical_not(is_a))
    cum_c = [cum[:, h:h + 1] for h in range(S_HEADS)]
    last = [cum[L - 1:L, h:h + 1] for h in range(S_HEADS)]
    wts = []
    for h in range(S_HEADS):
        seg = cum_c[h] - cum_t[h:h + 1, :]
        dec = jnp.exp(jnp.where(causal, seg, -jnp.inf))
        wts.append(cb_g[group_of[h // 2]] * dec * dt_t[h:h + 1, :])
    intra = [_dot(wts[h], jnp.where(halves[h % 2], xs_ps[h // 2], 0.0)) for h in range(S_HEADS)]
    inter = [_dot_nt(cgs[group_of[pr]], hps[pr]) for pr in range(N_PAIRS)]
    ys = [intra[2 * pr] + intra[2 * pr + 1]
          + inter[pr] * jnp.where(is_a, jnp.exp(cum_c[2 * pr]), jnp.exp(cum_c[2 * pr + 1])) for pr in range(N_PAIRS)]
    tes = [jnp.where(is_a, jnp.exp(last[2 * pr] - cum_c[2 * pr]) * dt[:, 2 * pr:2 * pr + 1],
                     jnp.exp(last[2 * pr + 1] - cum_c[2 * pr + 1]) * dt[:, 2 * pr + 1:2 * pr + 2])
           for pr in range(N_PAIRS)]
    sts = [_dot_tn(xs_ps[pr] * tes[pr], bgs[group_of[pr]]) for pr in range(N_PAIRS)]
    for pr in range(N_PAIRS):
        h_ref[pr] = hps[pr] * jnp.where(row_a, jnp.exp(last[2 * pr]), jnp.exp(last[2 * pr + 1])) + sts[pr]
    y = jnp.concatenate(ys, axis=1) + dvec_ref[...] * xs
    yz = y * _silu(z_ref[...])
    gw = S_W // S_GROUPS
    for g in range(S_GROUPS):
        part = yz[:, g * gw:(g + 1) * gw]
        ms = jnp.mean(part * part, axis=-1, keepdims=True)
        o_ref[:, g * gw:(g + 1) * gw] = part * lax.rsqrt(ms + NORM_EPS) * ng_ref[:, g * gw:(g + 1) * gw]

    @pl.when(c == pl.num_programs(1) - 1)
    def _():
        hout_ref[0] = h_ref[...]


def _ssd_prompt(proj, bsz, seq, cw, cb, dtb, alog, dvec, ng):
    L = SSD_CHUNK
    nc = seq // L
    const = lambda shape: pl.BlockSpec(shape, lambda b, c: (0,) * len(shape))
    return pl.pallas_call(
        functools.partial(_ssd_prompt_kernel, L=L),
        grid=(bsz, nc),
        in_specs=[pl.BlockSpec((L, S_W), lambda b, c: (b * nc + c, C_Z // S_W)),
                  pl.BlockSpec((L, S_CONV_CH), lambda b, c: (b * nc + c, C_XBC // S_CONV_CH)),
                  pl.BlockSpec((L, LANES), lambda b, c: (b * nc + c, C_DT // LANES)),
                  const((S_CONV, S_CONV_CH)), const((1, S_CONV_CH)), const((1, LANES)), const((1, LANES)),
                  const((1, S_W)), const((1, S_W))],
        out_specs=[pl.BlockSpec((L, S_W), lambda b, c: (b * nc + c, 0)),
                   pl.BlockSpec((1, N_PAIRS, 2 * S_HD, S_STATE), lambda b, c: (b, 0, 0, 0))],
        out_shape=[jax.ShapeDtypeStruct((bsz * seq, S_W), F32),
                   jax.ShapeDtypeStruct((bsz, N_PAIRS, 2 * S_HD, S_STATE), F32)],
        scratch_shapes=[pltpu.VMEM((L + 2 * SUBLANES, S_CONV_CH), F32),
                        pltpu.VMEM((N_PAIRS, 2 * S_HD, S_STATE), F32)],
        compiler_params=_cparams(("parallel", "arbitrary")),
        name="ssd_prompt",
    )(proj, proj, proj, cw, cb, dtb, alog, dvec, ng)


def _ssd_sample_kernel(z_ref, xbc_ref, dt_ref, pre_ref, h0_ref, cw_ref, cb_ref, dtbv_ref, alogv_ref,
                       dvec_ref, ng_ref, o_ref, hout_ref, *, T, nb):
    rows2 = 2 * S_HD
    eye = (lax.broadcasted_iota(jnp.int32, (rows2, LANES), 0)
           == lax.broadcasted_iota(jnp.int32, (rows2, LANES), 1)).astype(F32)
    eye_all = jnp.concatenate([eye] * nb, axis=0)
    ones = jnp.ones((LANES, LANES), F32)
    spread = (lax.broadcasted_iota(jnp.int32, (LANES, S_W), 0)
              == lax.shift_right_logical(lax.broadcasted_iota(jnp.int32, (LANES, S_W), 1), HALF_SHIFT)).astype(F32)
    a_vec = -jnp.exp(alogv_ref[...])
    gw = S_W // S_GROUPS

    def as_columns(x, passes):
        return _split_dot(eye_all * _repeat_rows(x, rows2), ones, passes)

    up = [pre_ref[j] for j in range(S_CONV - 1)] + [xbc_ref[t] for t in range(T)]
    hs = [h0_ref[0, :, pr].reshape(nb * rows2, S_STATE) for pr in range(N_PAIRS)]
    for t in range(T):
        acc = cb_ref[...] + cw_ref[0:1, :] * up[t]
        for j in range(1, S_CONV):
            acc = acc + cw_ref[j:j + 1, :] * up[t + j]
        xc = _silu(acc)
        xs = xc[:, :S_W]
        dt = _softplus(_split_dot(dt_ref[t], spread, 3) + dtbv_ref[...])
        decay = jnp.exp(dt * a_vec)
        xdt = xs * dt
        ys = []
        for pr in range(N_PAIRS):
            g = (2 * pr * S_GROUPS) // S_HEADS
            sl = slice(pr * LANES, (pr + 1) * LANES)
            b_rows = _repeat_rows(xc[:, S_W + g * S_STATE:S_W + (g + 1) * S_STATE], rows2)
            c_rows = _repeat_rows(xc[:, S_W + (S_GROUPS + g) * S_STATE:S_W + (S_GROUPS + g + 1) * S_STATE], rows2)
            hs[pr] = hs[pr] * as_columns(decay[:, sl], 3) + as_columns(xdt[:, sl], 2) * b_rows
            y_col = _split_dot(hs[pr] * c_rows, ones, 2)
            ys.append(jnp.sum((eye_all * y_col).reshape(nb, rows2, LANES), axis=1))
        y = jnp.concatenate(ys, axis=1) + dvec_ref[...] * xs
        yz = y * _silu(z_ref[t])
        for g in range(S_GROUPS):
            part = yz[:, g * gw:(g + 1) * gw]
            ms = jnp.mean(part * part, axis=-1, keepdims=True)
            o_ref[t, :, g * gw:(g + 1) * gw] = part * lax.rsqrt(ms + NORM_EPS) * ng_ref[:, g * gw:(g + 1) * gw]
    for pr in range(N_PAIRS):
        hout_ref[:, pr] = hs[pr].reshape(nb, rows2, S_STATE)


def _ssd_sample(proj3, pre3, h0, layer, cw, cb, dtbv, alogv, dvec, ng, nb=8):
    T, dbsz = proj3.shape[:2]
    const = lambda shape: pl.BlockSpec(shape, lambda j: (0,) * len(shape))
    hspec = pl.BlockSpec((nb, N_PAIRS, 2 * S_HD, S_STATE), lambda j: (j, 0, 0, 0))
    h0spec = pl.BlockSpec((1, nb, N_PAIRS, 2 * S_HD, S_STATE), lambda j: (layer, j, 0, 0, 0))
    return pl.pallas_call(
        functools.partial(_ssd_sample_kernel, T=T, nb=nb),
        grid=(dbsz // nb,),
        in_specs=[pl.BlockSpec((T, nb, S_W), lambda j: (0, j, C_Z // S_W)),
                  pl.BlockSpec((T, nb, S_CONV_CH), lambda j: (0, j, C_XBC // S_CONV_CH)),
                  pl.BlockSpec((T, nb, LANES), lambda j: (0, j, C_DT // LANES)),
                  pl.BlockSpec((S_CONV - 1, nb, S_CONV_CH), lambda j: (0, j, 0)),
                  h0spec,
                  const((S_CONV, S_CONV_CH)), const((1, S_CONV_CH)), const((1, S_W)), const((1, S_W)),
                  const((1, S_W)), const((1, S_W))],
        out_specs=[pl.BlockSpec((T, nb, S_W), lambda j: (0, j, 0)), hspec],
        out_shape=[jax.ShapeDtypeStruct((T, dbsz, S_W), F32),
                   jax.ShapeDtypeStruct((dbsz, N_PAIRS, 2 * S_HD, S_STATE), F32)],
        compiler_params=_cparams(("parallel",)),
        name="ssd_sample",
    )(proj3, proj3, proj3, pre3, h0, cw, cb, dtbv, alogv, dvec, ng)


def _rwkv_prep_kernel(u_ref, pre_ref, mu_ref, w0_ref, w2_ref, a0_ref, a2_ref, g2_ref, kk_ref, ka_ref,
                      r_out, w_out, k_out, v_out, kk_out, kka_out, g_out, ext_ref,
                      *, tm, P, stride, tiles_per_seq):
    i = pl.program_id(0)
    first = (i % tiles_per_seq) == 0

    @pl.when(first)
    def _():
        ext_ref[0:P, :] = pre_ref[...]

    @pl.when(jnp.logical_not(first))
    def _():
        ext_ref[0:P, :] = ext_ref[tm:tm + P, :]

    u = u_ref[...]
    ext_ref[P:P + tm, :] = u
    prev = ext_ref[P - stride:P - stride + tm, :]
    x = u + (prev - u) * mu_ref[...]
    r = x[:, 0:R_W]
    kr = x[:, R_W:2 * R_W]
    vr = x[:, 2 * R_W:3 * R_W]
    xl = x[:, 3 * R_W:]
    w_log = w0_ref[...] + _dot(jnp.tanh(xl), w2_ref[...])
    log_decay = -jnp.exp(-_softplus(-w_log) - 0.5)
    a = _sigmoid(a0_ref[...] + _dot(xl, a2_ref[...]))
    g = _dot(_sigmoid(xl), g2_ref[...])
    kk = kr * kk_ref[...]
    ss = _split_dot(kk * kk, _head_ones(R_W, 1.0), 2)
    kk = kk / jnp.maximum(jnp.sqrt(ss), 1e-12)
    r_out[...] = r
    w_out[...] = log_decay
    k_out[...] = kr * (1.0 + (a - 1.0) * ka_ref[...])
    v_out[...] = vr
    kk_out[...] = kk
    kka_out[...] = kk * a
    g_out[...] = g


def _rwkv_prep(proj, pre, mu, w0, w2p, a0, a2p, g2p, kkw, kaw, tm, stride, tiles_per_seq):
    t = proj.shape[0]
    P = pre.shape[0]
    const = lambda shape: pl.BlockSpec(shape, lambda i: (0,) * len(shape))
    outs = pl.pallas_call(
        functools.partial(_rwkv_prep_kernel, tm=tm, P=P, stride=stride, tiles_per_seq=tiles_per_seq),
        grid=(t // tm,),
        in_specs=[pl.BlockSpec((tm, R_IN_W), lambda i: (i, 0)),
                  const((P, R_IN_W)), const((1, R_IN_W)), const((1, R_W)), const((R_LORA, R_W)),
                  const((1, R_W)), const((R_LORA, R_W)), const((R_LORA, R_W)), const((1, R_W)), const((1, R_W))],
        out_specs=[pl.BlockSpec((tm, R_W), lambda i: (i, 0))] * 7,
        out_shape=[jax.ShapeDtypeStruct((t, R_W), F32)] * 7,
        scratch_shapes=[pltpu.VMEM((tm + 2 * P, R_IN_W), F32)],
        compiler_params=_cparams(("arbitrary",)),
        name="rwkv_prep",
    )(proj, pre, mu, w0, w2p, a0, a2p, g2p, kkw, kaw)
    return outs


def _pair_consts():
    lane = lax.broadcasted_iota(jnp.int32, (R_HD, LANES), 1)
    row = lax.broadcasted_iota(jnp.int32, (R_HD, LANES), 0)
    is_a = lane < HALF
    eye2 = (jnp.bitwise_and(lane, HALF - 1) == row).astype(F32)
    return is_a, eye2


def _unit_lower_inverses(ns):
    L = ns[0].shape[0]
    eye = (lax.broadcasted_iota(jnp.int32, (L, L), 0) == lax.broadcasted_iota(jnp.int32, (L, L), 1)).astype(F32)
    ts = [eye + n for n in ns]
    pws = [_dot(n, n) for n in ns]
    for _ in range(int(math.log2(L)) - 2):
        both = [_dot(jnp.concatenate([t, p], axis=0), p) for t, p in zip(ts, pws)]
        ts = [t + b[:L] for t, b in zip(ts, both)]
        pws = [b[L:] for b in both]
    return [t + _dot(t, p) for t, p in zip(ts, pws)]


def _rwkv_chunk_prompt_kernel(r_ref, lw_ref, k_ref, v_ref, kk_ref, kka_ref, y_ref, sout_ref, s_ref, *, L):
    c = pl.program_id(1)

    @pl.when(c == 0)
    def _():
        s_ref[...] = jnp.zeros_like(s_ref)

    lane = lax.broadcasted_iota(jnp.int32, (1, LANES), 1)
    is_a = lane < HALF
    rr = lax.broadcasted_iota(jnp.int32, (L, L), 0)
    cc = lax.broadcasted_iota(jnp.int32, (L, L), 1)
    incl = rr >= cc
    strict = rr > cc
    tri = jnp.where(incl, 1.0, 0.0).astype(F32)
    r2 = lax.broadcasted_iota(jnp.int32, (LANES, LANES), 0) < HALF
    c2 = lax.broadcasted_iota(jnp.int32, (LANES, LANES), 1) < HALF
    same_head = r2 == c2

    r, lw, k, v, kk, kka = (ref[...] for ref in (r_ref, lw_ref, k_ref, v_ref, kk_ref, kka_ref))
    cum = _split_dot(lw, tri, 3, x_is_lhs=False)
    last = cum[L - 1:L, :]
    inv_p = jnp.exp(-cum)
    to_end = jnp.exp(last - cum)
    b_t = kk * jnp.exp(cum - lw)
    a_t = -kka * inv_p
    k_t = k * inv_p
    r_t = r * jnp.exp(cum)
    a_end = -kka * to_end
    k_end = k * to_end
    decay_end = jnp.exp(last)
    pairs = [slice(pr * LANES, (pr + 1) * LANES) for pr in range(N_PAIRS)]
    halves = (is_a, jnp.logical_not(is_a))
    s0 = [s_ref[pr] for pr in range(N_PAIRS)]

    ns, mks, rak = [], [], []
    for sl in pairs:
        cols = jnp.concatenate([a_t[:, sl], k_t[:, sl]], axis=0)
        for half in halves:
            lhs = jnp.concatenate([jnp.where(half, b_t[:, sl], 0.0), jnp.where(half, r_t[:, sl], 0.0)], axis=0)
            g4 = _dot_nt(lhs, cols)
            ns.append(jnp.where(strict, g4[:L, :L], 0.0))
            mks.append(jnp.where(strict, g4[:L, L:], 0.0))
            rak.append(jnp.concatenate([jnp.where(incl, g4[L:, :L], 0.0), jnp.where(incl, g4[L:, L:], 0.0)], axis=1))
    ts = _unit_lower_inverses(ns)
    from_state = [_dot_nt(jnp.concatenate([b_t[:, sl], r_t[:, sl]], axis=0), s0[pr]) for pr, sl in enumerate(pairs)]
    rhs = []
    for pr, sl in enumerate(pairs):
        mv = _dot(jnp.concatenate([mks[2 * pr], mks[2 * pr + 1]], axis=0), v[:, sl])
        rhs.append(from_state[pr][:L] + jnp.where(is_a, mv[:L], mv[L:]))
    us = []
    for pr in range(N_PAIRS):
        tu = _dot(jnp.concatenate([ts[2 * pr], ts[2 * pr + 1]], axis=0), rhs[pr])
        us.append(jnp.where(is_a, tu[:L], tu[L:]))
    for pr, sl in enumerate(pairs):
        uv = jnp.concatenate([us[pr], v[:, sl]], axis=0)
        y_ref[:, sl] = from_state[pr][L:] + jnp.where(is_a, _dot(rak[2 * pr], uv), _dot(rak[2 * pr + 1], uv))
    for pr, sl in enumerate(pairs):
        upd = _dot_tn(jnp.concatenate([us[pr], v[:, sl]], axis=0),
                      jnp.concatenate([a_end[:, sl], k_end[:, sl]], axis=0))
        s_ref[pr] = s0[pr] * decay_end[:, sl] + jnp.where(same_head, upd, 0.0)

    @pl.when(c == pl.num_programs(1) - 1)
    def _():
        sout_ref[0] = s_ref[...]


def _rwkv_scan_prompt(seqs, bsz, seq, L):
    nc = seq // L
    spec = pl.BlockSpec((L, R_W), lambda b, c: (b * nc + c, 0))
    return pl.pallas_call(
        functools.partial(_rwkv_chunk_prompt_kernel, L=L),
        grid=(bsz, nc),
        in_specs=[spec] * 6,
        out_specs=[spec, pl.BlockSpec((1, N_PAIRS, LANES, LANES), lambda b, c: (b, 0, 0, 0))],
        out_shape=[jax.ShapeDtypeStruct((bsz * seq, R_W), F32),
                   jax.ShapeDtypeStruct((bsz, N_PAIRS, LANES, LANES), F32)],
        scratch_shapes=[pltpu.VMEM((N_PAIRS, LANES, LANES), F32)],
        compiler_params=_cparams(("parallel", "arbitrary")),
        name="rwkv_scan_prompt",
    )(*seqs)


def _repeat_rows(x, reps):
    return jnp.concatenate([jnp.broadcast_to(x[b:b + 1, :], (reps, x.shape[1])) for b in range(x.shape[0])], axis=0)


def _rwkv_scan_sample_kernel(r_ref, w_ref, k_ref, v_ref, kk_ref, kka_ref, s0_ref, y_ref, sout_ref, *, T, nb):
    _, eye2 = _pair_consts()
    eye_all = jnp.concatenate([eye2] * nb, axis=0)
    ones_bd = _head_ones(LANES, 1.0)

    def head_sums(x):
        return _split_dot(x, ones_bd, 2)

    states = [s0_ref[:, pr].reshape(nb * R_HD, LANES) for pr in range(N_PAIRS)]
    for t in range(T):
        for pr in range(N_PAIRS):
            sl = slice(pr * LANES, (pr + 1) * LANES)
            rr, ww, kr, vv, kk, kka = (_repeat_rows(x, R_HD) for x in (
                r_ref[t, :, sl], jnp.exp(w_ref[t, :, sl]), k_ref[t, :, sl], v_ref[t, :, sl],
                kk_ref[t, :, sl], kka_ref[t, :, sl]))
            v_col = head_sums(eye_all * vv)
            s = states[pr]
            s = s * ww - head_sums(s * kk) * kka + v_col * kr
            states[pr] = s
            y_col = head_sums(s * rr)
            y_ref[t, :, sl] = jnp.sum((eye_all * y_col).reshape(nb, R_HD, LANES), axis=1)
    for pr in range(N_PAIRS):
        sout_ref[:, pr] = states[pr].reshape(nb, R_HD, LANES)


def _rwkv_scan_sample(seqs3, s0, nb=16):
    T, dbsz = seqs3[0].shape[:2]
    spec = pl.BlockSpec((T, nb, R_W), lambda j: (0, j, 0))
    sspec = pl.BlockSpec((nb, N_PAIRS, R_HD, LANES), lambda j: (j, 0, 0, 0))
    return pl.pallas_call(
        functools.partial(_rwkv_scan_sample_kernel, T=T, nb=nb),
        grid=(dbsz // nb,),
        in_specs=[spec] * 6 + [sspec],
        out_specs=[spec, sspec],
        out_shape=[jax.ShapeDtypeStruct((T, dbsz, R_W), F32),
                   jax.ShapeDtypeStruct((dbsz, N_PAIRS, R_HD, LANES), F32)],
        compiler_params=_cparams(("parallel",)),
        name="rwkv_scan_sample",
    )(*seqs3, s0)


def _merge_kernel(x_ref, oa_ref, ob_ref, yr_ref, r_ref, k_ref, v_ref, g_ref, ga_ref, gb_ref, gc_ref,
                  bg_ref, lng_ref, lnb_ref, rk_ref, wpa_ref, wpb_ref, wpc_ref, wo_ref, o_ref):
    pa = _dot(oa_ref[...], wpa_ref[...])
    pb = _dot(ob_ref[...], wpb_ref[...])
    mean_m = _head_ones(R_W, 1.0 / R_HD)
    yr = yr_ref[...]
    bonus = _split_dot(r_ref[...] * k_ref[...] * rk_ref[...], _head_ones(R_W, 1.0), 2)
    d = yr - _split_dot(yr, mean_m, 2)
    var = _split_dot(d * d, mean_m, 2)
    yn = d * lax.rsqrt(var + GN_EPS) * lng_ref[...] + lnb_ref[...]
    oc = (yn + bonus * v_ref[...]) * g_ref[...]
    merged = (_sigmoid(ga_ref[...] + bg_ref[:, 0:D_MODEL]) * pa
              + _sigmoid(gb_ref[...] + bg_ref[:, D_MODEL:2 * D_MODEL]) * pb
              + _sigmoid(gc_ref[...] + bg_ref[:, 2 * D_MODEL:]) * _dot(oc, wpc_ref[...]))
    o_ref[...] = x_ref[...] + _dot(merged, wo_ref[...])


def _merge(x, proj, oa, ob, yr, r, k2, v, g, bg, lng, lnb, rk, wpa, wpb, wpc, wo, tm):
    t = x.shape[0]
    row = lambda w: pl.BlockSpec((tm, w), lambda i: (i, 0))
    const = lambda shape: pl.BlockSpec(shape, lambda i: (0,) * len(shape))
    gcol = C_GATE // D_MODEL
    gate = lambda j: pl.BlockSpec((tm, D_MODEL), lambda i: (i, gcol + j))
    return pl.pallas_call(
        _merge_kernel,
        grid=(t // tm,),
        in_specs=[row(D_MODEL), row(A_W), row(S_W), row(R_W), row(R_W), row(R_W), row(R_W), row(R_W),
                  gate(0), gate(1), gate(2),
                  const((1, 3 * D_MODEL)), const((1, R_W)), const((1, R_W)), const((1, R_W)),
                  const((A_W, D_MODEL)), const((S_W, D_MODEL)), const((R_W, D_MODEL)),
                  const((D_MODEL, D_MODEL))],
        out_specs=row(D_MODEL),
        out_shape=jax.ShapeDtypeStruct((t, D_MODEL), F32),
        compiler_params=_cparams(("parallel",)),
        name="merge",
    )(x, oa, ob, yr, r, k2, v, g, proj, proj, proj, bg, lng, lnb, rk, wpa, wpb, wpc, wo)


def _ffn_kernel(x_ref, g_ref, wug_ref, wuv_ref, wd_ref, cw_ref, cb_ref, pre_ref, gf_ref,
                o_ref, tail_ref, xn_ref, ext_ref, *, tm, P, stride, tiles_per_seq, final_norm):
    i = pl.program_id(0)
    f = pl.program_id(1)
    first = (i % tiles_per_seq) == 0

    @pl.when(f == 0)
    def _():
        x = x_ref[...]
        ms = jnp.mean(x * x, axis=-1, keepdims=True)
        xn_ref[...] = (x * lax.rsqrt(ms + NORM_EPS) * g_ref[...]).astype(BF16)

    @pl.when(first)
    def _():
        ext_ref[f, 0:P, :] = pre_ref[...]

    @pl.when(jnp.logical_not(first))
    def _():
        ext_ref[f, 0:P, :] = ext_ref[f, tm:tm + P, :]

    xn = xn_ref[...]
    ug = jnp.dot(xn, wug_ref[...], preferred_element_type=F32)
    uv = jnp.dot(xn, wuv_ref[...], preferred_element_type=F32)
    ext_ref[f, P:P + tm, :] = ug
    tail_ref[0] = ext_ref[f, tm:tm + P, :]
    acc = cb_ref[...] + cw_ref[F_CONV - 1:F_CONV, :] * ug
    for d in range(1, F_CONV):
        acc = acc + cw_ref[F_CONV - 1 - d:F_CONV - d, :] * ext_ref[f, P - d * stride:P - d * stride + tm, :]
    contrib = _dot(_silu(acc) * uv, wd_ref[...])

    @pl.when(f == 0)
    def _():
        o_ref[...] = x_ref[...] + contrib

    @pl.when(f > 0)
    def _():
        o_ref[...] = o_ref[...] + contrib

    if final_norm:
        @pl.when(f == pl.num_programs(1) - 1)
        def _():
            y = o_ref[...]
            ms = jnp.mean(y * y, axis=-1, keepdims=True)
            o_ref[...] = y * lax.rsqrt(ms + NORM_EPS) * gf_ref[...]


def _ffn(x, g, wup, wd, cw, cb, pre, gf, tm, tf, stride, tiles_per_seq, final_norm):
    t = x.shape[0]
    P = pre.shape[0]
    nf = D_FF // tf
    return pl.pallas_call(
        functools.partial(_ffn_kernel, tm=tm, P=P, stride=stride, tiles_per_seq=tiles_per_seq,
                          final_norm=final_norm),
        grid=(t // tm, nf),
        in_specs=[pl.BlockSpec((tm, D_MODEL), lambda i, f: (i, 0)),
                  pl.BlockSpec((1, D_MODEL), lambda i, f: (0, 0)),
                  pl.BlockSpec((D_MODEL, tf), lambda i, f: (0, f)),
                  pl.BlockSpec((D_MODEL, tf), lambda i, f: (0, nf + f)),
                  pl.BlockSpec((tf, D_MODEL), lambda i, f: (f, 0)),
                  pl.BlockSpec((F_CONV, tf), lambda i, f: (0, f)),
                  pl.BlockSpec((1, tf), lambda i, f: (0, f)),
                  pl.BlockSpec((P, tf), lambda i, f: (0, f)),
                  pl.BlockSpec((1, D_MODEL), lambda i, f: (0, 0))],
        out_specs=[pl.BlockSpec((tm, D_MODEL), lambda i, f: (i, 0)),
                   pl.BlockSpec((1, P, tf), lambda i, f: (i, 0, f))],
        out_shape=[jax.ShapeDtypeStruct((t, D_MODEL), F32),
                   jax.ShapeDtypeStruct((t // tm, P, D_FF), F32)],
        scratch_shapes=[pltpu.VMEM((tm, D_MODEL), BF16), pltpu.VMEM((nf, tm + 2 * P, tf), F32)],
        compiler_params=_cparams(("arbitrary", "arbitrary")),
        name="conv_ffn",
    )(x, g, wup, wup, wd, cw, cb, pre, gf)


def _pack_rwkv_state(s):
    n = s.shape[0]
    return s.reshape(n, N_PAIRS, 2, R_HD, R_HD).transpose(0, 1, 3, 2, 4).reshape(n, N_PAIRS, R_HD, LANES)


def _unpack_rwkv_state(s):
    n = s.shape[0]
    return s.reshape(n, N_PAIRS, R_HD, 2, R_HD).transpose(0, 1, 3, 2, 4).reshape(n, R_HEADS, R_HD, R_HD)


def _unpack_rwkv_blockdiag(s):
    n = s.shape[0]
    return jnp.stack([s[:, :, :R_HD, :R_HD], s[:, :, R_HD:, R_HD:]], axis=2).reshape(n, R_HEADS, R_HD, R_HD)


def _prep_layer_params(l, p):
    w_in = p['w_in'][l]
    c_dt_src = 3 * A_W + S_W + S_CONV_CH
    c_rw_src = c_dt_src + S_HEADS
    c_gate_src = c_rw_src + R_IN_W
    w_proj = jnp.concatenate([
        w_in[:, c_rw_src:c_gate_src],
        w_in[:, c_dt_src:c_rw_src], jnp.zeros((D_MODEL, C_Q - C_DT - S_HEADS), F32),
        w_in[:, :c_dt_src],
        w_in[:, c_gate_src:]], axis=1).astype(BF16)
    pad_lane = lambda v: jnp.pad(v, (0, LANES - v.shape[0])).reshape(1, LANES)
    zl = lambda r0, w: jnp.zeros((R_LORA, R_W), F32).at[r0:r0 + w.shape[0]].set(w).astype(BF16)
    return dict(
        norm1_g=p['norm1_g'][l].reshape(1, D_MODEL), w_proj=w_proj,
        b_gate=p['b_gate'][l].reshape(1, 3 * D_MODEL),
        w_pa=p['w_pa'][l].astype(BF16), w_pb=p['w_pb'][l].astype(BF16), w_pc=p['w_pc'][l].astype(BF16),
        w_o=p['w_o'][l].astype(BF16),
        ssm_conv_w=p['ssm_conv_w'][l], ssm_conv_b=p['ssm_conv_b'][l].reshape(1, S_CONV_CH),
        ssm_dt_bias=pad_lane(p['ssm_dt_bias'][l]), ssm_a_log=pad_lane(p['ssm_a_log'][l]),
        ssm_dvec=jnp.repeat(p['ssm_d'][l], S_HD).reshape(1, S_W),
        ssm_dtb_vec=jnp.repeat(p['ssm_dt_bias'][l], S_HD).reshape(1, S_W),
        ssm_alog_vec=jnp.repeat(p['ssm_a_log'][l], S_HD).reshape(1, S_W),
        ssm_norm_g=p['ssm_norm_g'][l].reshape(1, S_W),
        rw_mu=p['rw_mu'][l].reshape(1, R_IN_W), rw_w0=p['rw_w0'][l].reshape(1, R_W),
        rw_w2p=zl(0, p['rw_w2'][l]), rw_a0=p['rw_a0'][l].reshape(1, R_W),
        rw_a2p=zl(R_LORA_W, p['rw_a2'][l]), rw_g2p=zl(R_LORA_W + R_LORA_A, p['rw_g2'][l]),
        rw_kk=p['rw_kk'][l].reshape(1, R_W), rw_ka=p['rw_ka'][l].reshape(1, R_W),
        rw_rk=p['rw_rk'][l].reshape(1, R_W), rw_ln_g=p['rw_ln_g'][l].reshape(1, R_W),
        rw_ln_b=p['rw_ln_b'][l].reshape(1, R_W),
        norm2_g=p['norm2_g'][l].reshape(1, D_MODEL), w_up=p['w_up'][l].astype(BF16),
        w_down=p['w_down'][l].astype(BF16), ffn_conv_w=p['ffn_conv_w'][l],
        ffn_conv_b=p['ffn_conv_b'][l].reshape(1, D_FF))


def _row_tile(t):
    for tm in (512, 256, 128):
        if t % tm == 0:
            return tm
    raise ValueError(t)


def _ffn_tf():
    return D_FF // 2


def _prompt_layer(x, lp, bsz, seq, slopes_pair, gf, final_norm):
    t = bsz * seq
    tm = _row_tile(seq)
    tiles = seq // tm
    proj = _rms_matmul(x, lp['norm1_g'], lp['w_proj'], 1024 if t % 1024 == 0 else tm, 2048)
    proj3 = proj.reshape(bsz, seq, N_PROJ)
    oa, k_t, v_t = _moba_prompt(proj3, slopes_pair)
    oa = oa.reshape(t, A_W)
    ob, ssm_new = _ssd_prompt(proj, bsz, seq, lp['ssm_conv_w'], lp['ssm_conv_b'], lp['ssm_dt_bias'],
                              lp['ssm_a_log'], lp['ssm_dvec'], lp['ssm_norm_g'])
    r, w, k2, v, kk, kka, g = _rwkv_prep(
        proj, jnp.zeros((SUBLANES, R_IN_W), F32), lp['rw_mu'], lp['rw_w0'], lp['rw_w2p'], lp['rw_a0'],
        lp['rw_a2p'], lp['rw_g2p'], lp['rw_kk'], lp['rw_ka'], tm, 1, tiles)
    yr, rw_new = _rwkv_scan_prompt((r, w, k2, v, kk, kka), bsz, seq, LANES)
    x = _merge(x, proj, oa, ob, yr, r, k2, v, g, lp['b_gate'], lp['rw_ln_g'], lp['rw_ln_b'], lp['rw_rk'],
               lp['w_pa'], lp['w_pb'], lp['w_pc'], lp['w_o'], min(tm, 256))
    x, tail = _ffn(x, lp['norm2_g'], lp['w_up'], lp['w_down'], lp['ffn_conv_w'], lp['ffn_conv_b'],
                   jnp.zeros((SUBLANES, D_FF), F32), gf, tm, _ffn_tf(), 1, tiles, final_norm)
    k_new = k_t.reshape(bsz, A_HEADS, A_HD, seq).transpose(0, 3, 1, 2)
    v_new = v_t.reshape(bsz, A_HEADS, A_HD, seq).transpose(0, 3, 1, 2)
    ssm_conv_new = proj3[:, seq - (S_CONV - 1):, C_XBC:C_XBC + S_CONV_CH]
    shift_new = proj3[:, seq - 1:, C_RW:C_RW + R_IN_W]
    ffn_conv_new = tail.reshape(bsz, tiles, SUBLANES, D_FF)[:, tiles - 1, SUBLANES - (F_CONV - 1):]
    state = (k_new, v_new, ssm_new.reshape(bsz, S_HEADS, S_HD, S_STATE), ssm_conv_new,
             _unpack_rwkv_blockdiag(rw_new), shift_new, ffn_conv_new)
    return x, state


def _sample_layer(x, lp, dbsz, tnew, st, cache_k4, cache_v4, page_table, layer, gf, final_norm):
    t = tnew * dbsz
    ssm0, ssm_conv0, rwkv0, shift0, ffn_conv0 = st
    proj = _rms_matmul(x, lp['norm1_g'], lp['w_proj'], t, 1024)
    proj3 = proj.reshape(tnew, dbsz, N_PROJ)

    qkv = proj3[:, :, C_Q:C_Q + 3 * A_W].transpose(1, 0, 2)
    qkv8 = jnp.pad(qkv, ((0, 0), (0, SUBLANES - tnew), (0, 0)))
    oa8 = _moba_sample(qkv8[:, :, :A_W], qkv8[:, :, A_W:2 * A_W], qkv8[:, :, 2 * A_W:],
                       cache_k4, cache_v4, page_table, layer, tnew)
    oa = oa8[:, :tnew].transpose(1, 0, 2).reshape(t, A_W)

    h0 = ssm0.reshape(-1, dbsz, N_PAIRS, 2 * S_HD, S_STATE)
    ob3, ssm_new = _ssd_sample(proj3, ssm_conv0.transpose(1, 0, 2), h0, layer, lp['ssm_conv_w'],
                               lp['ssm_conv_b'], lp['ssm_dtb_vec'], lp['ssm_alog_vec'], lp['ssm_dvec'],
                               lp['ssm_norm_g'])
    ob = ob3.reshape(t, S_W)
    ssm_new = ssm_new.reshape(dbsz, S_HEADS, S_HD, S_STATE)

    r, w, k2, v, kk, kka, g = _rwkv_prep(
        proj, shift0.reshape(dbsz, R_IN_W), lp['rw_mu'], lp['rw_w0'], lp['rw_w2p'], lp['rw_a0'],
        lp['rw_a2p'], lp['rw_g2p'], lp['rw_kk'], lp['rw_ka'], t, dbsz, 1)
    to3 = lambda a: a.reshape(tnew, dbsz, R_W)
    yr3, rw_new = _rwkv_scan_sample(tuple(to3(a) for a in (r, w, k2, v, kk, kka)), _pack_rwkv_state(rwkv0))
    yr = yr3.reshape(t, R_W)

    x = _merge(x, proj, oa, ob, yr, r, k2, v, g, lp['b_gate'], lp['rw_ln_g'], lp['rw_ln_b'], lp['rw_rk'],
               lp['w_pa'], lp['w_pb'], lp['w_pc'], lp['w_o'], min(t, 256))
    pre = ffn_conv0.transpose(1, 0, 2).reshape((F_CONV - 1) * dbsz, D_FF)
    x, tail = _ffn(x, lp['norm2_g'], lp['w_up'], lp['w_down'], lp['ffn_conv_w'], lp['ffn_conv_b'],
                   pre, gf, t, _ffn_tf(), dbsz, 1, final_norm)

    k_new = qkv[:, :, A_W:2 * A_W].reshape(dbsz, tnew, A_HEADS, A_HD)
    v_new = qkv[:, :, 2 * A_W:].reshape(dbsz, tnew, A_HEADS, A_HD)
    ssm_conv_new = proj3[tnew - (S_CONV - 1):, :, C_XBC:C_XBC + S_CONV_CH].transpose(1, 0, 2)
    shift_new = proj3[tnew - 1:, :, C_RW:C_RW + R_IN_W].transpose(1, 0, 2)
    ffn_conv_new = tail.reshape(F_CONV - 1, dbsz, D_FF).transpose(1, 0, 2)
    state = (k_new, v_new, ssm_new, ssm_conv_new, _unpack_rwkv_state(rw_new), shift_new, ffn_conv_new)
    return x, state


def kernel(x_prompt, x_sample, cache_k, cache_v, state_ssm, state_ssm_conv, state_rwkv, state_rwkv_shift, state_ffn_conv, page_table, norm1_g, w_in, b_gate, w_pa, ssm_conv_w, ssm_conv_b, ssm_dt_bias, ssm_a_log, ssm_d, ssm_norm_g, w_pb, rw_mu, rw_w0, rw_w2, rw_a0, rw_a2, rw_g2, rw_kk, rw_ka, rw_rk, rw_ln_g, rw_ln_b, w_pc, w_o, norm2_g, w_up, ffn_conv_w, ffn_conv_b, w_down, norm_f_g):
    params = dict(norm1_g=norm1_g, w_in=w_in, b_gate=b_gate, w_pa=w_pa, ssm_conv_w=ssm_conv_w,
                  ssm_conv_b=ssm_conv_b, ssm_dt_bias=ssm_dt_bias, ssm_a_log=ssm_a_log, ssm_d=ssm_d,
                  ssm_norm_g=ssm_norm_g, w_pb=w_pb, rw_mu=rw_mu, rw_w0=rw_w0, rw_w2=rw_w2, rw_a0=rw_a0,
                  rw_a2=rw_a2, rw_g2=rw_g2, rw_kk=rw_kk, rw_ka=rw_ka, rw_rk=rw_rk, rw_ln_g=rw_ln_g,
                  rw_ln_b=rw_ln_b, w_pc=w_pc, w_o=w_o, norm2_g=norm2_g, w_up=w_up, ffn_conv_w=ffn_conv_w,
                  ffn_conv_b=ffn_conv_b, w_down=w_down)
    depth = w_in.shape[0]
    bsz, seq, _ = x_prompt.shape
    dbsz, tnew, _ = x_sample.shape
    head = jnp.arange(A_HEADS, dtype=F32) + 1.0
    slopes = jnp.exp2(-8.0 * head / A_HEADS)
    slopes_pair = jnp.repeat(slopes, A_HD).reshape(N_PAIRS, 1, LANES)
    cache_k4 = cache_k.transpose(0, 1, 3, 4, 2)
    cache_v4 = cache_v.transpose(0, 1, 3, 4, 2)
    gf = norm_f_g.reshape(1, D_MODEL)

    hp = x_prompt.reshape(bsz * seq, D_MODEL)
    hs = x_sample.transpose(1, 0, 2).reshape(tnew * dbsz, D_MODEL)
    new_p = [[] for _ in range(7)]
    new_s = [[] for _ in range(7)]
    for l in range(depth):
        lp = _prep_layer_params(l, params)
        last = l == depth - 1
        hp, sp = _prompt_layer(hp, lp, bsz, seq, slopes_pair, gf, last)
        st = (state_ssm, state_ssm_conv[l], state_rwkv[l], state_rwkv_shift[l], state_ffn_conv[l])
        hs, ss = _sample_layer(hs, lp, dbsz, tnew, st, cache_k4, cache_v4, page_table, l, gf, last)
        for j in range(7):
            new_p[j].append(sp[j])
            new_s[j].append(ss[j])
    y_prompt = hp.reshape(bsz, seq, D_MODEL)
    y_sample = hs.reshape(tnew, dbsz, D_MODEL).transpose(1, 0, 2)
    outs = [y_prompt, y_sample]
    for j in range(7):
        outs += [jnp.stack(new_p[j]), jnp.stack(new_s[j])]
    return tuple(outs)
```

```python
import functools
import math

import jax
import jax.numpy as jnp
from jax import lax
from jax.experimental import pallas as pl
from jax.experimental.pallas import tpu as pltpu

F32 = jnp.float32
BF16 = jnp.bfloat16

D_MODEL = 1024
A_HEADS = 8
A_HD = 64
A_W = A_HEADS * A_HD
MOBA_BLOCK = 256
MOBA_TOPK = 3
Q_BLOCK = MOBA_BLOCK
ATTN_SCALE = A_HD ** -0.5
S_HEADS = 8
S_HD = 64
S_W = S_HEADS * S_HD
S_GROUPS = 2
S_STATE = 128
S_CONV = 4
S_CONV_CH = S_W + 2 * S_GROUPS * S_STATE
SSD_CHUNK = 128
R_HEADS = 8
R_HD = 64
R_W = R_HEADS * R_HD
R_LORA_W = 64
R_LORA_A = 64
R_LORA_G = 128
R_LORA = R_LORA_W + R_LORA_A + R_LORA_G
R_IN_W = 3 * R_W + R_LORA
D_FF = ((8 * D_MODEL // 3 + 127) // 128) * 128
F_CONV = 3
NORM_EPS = 1e-6
GN_EPS = 64e-5
NEG_INF = -1e30

LANES = 128
SUBLANES = 8
HALF = 64
HALF_SHIFT = 6
SUBLANE_SHIFT = 3
N_PAIRS = 4

C_RW = 0
C_DT = R_IN_W
C_Q = 2048
C_K = C_Q + A_W
C_V = C_K + A_W
C_Z = C_V + A_W
C_XBC = C_Z + S_W
C_GATE = C_XBC + S_CONV_CH
N_PROJ = C_GATE + 3 * D_MODEL

VMEM_LIMIT = 56 * 1024 * 1024


def _cparams(sem):
    return pltpu.CompilerParams(dimension_semantics=sem, vmem_limit_bytes=VMEM_LIMIT)


def _dot(a, b):
    return jnp.dot(a.astype(BF16), b.astype(BF16), preferred_element_type=F32)


def _dot_nt(a, b):
    return lax.dot_general(a.astype(BF16), b.astype(BF16), (((1,), (1,)), ((), ())),
                           preferred_element_type=F32)


def _dot_tn(a, b):
    return lax.dot_general(a.astype(BF16), b.astype(BF16), (((0,), (0,)), ((), ())),
                           preferred_element_type=F32)


def _split_dot(x, w, passes, x_is_lhs=True):
    w = w.astype(BF16)
    acc = None
    rem = x
    for _ in range(passes):
        piece = rem.astype(BF16)
        term = (jnp.dot(piece, w, preferred_element_type=F32) if x_is_lhs
                else jnp.dot(w, piece, preferred_element_type=F32))
        acc = term if acc is None else acc + term
        rem = rem - piece.astype(F32)
    return acc


def _dot_nt_hi(a, b):
    return lax.dot_general(a, b, (((1,), (1,)), ((), ())), preferred_element_type=F32,
                           precision=lax.Precision.HIGHEST)


def _sigmoid(x):
    return 1.0 / (1.0 + jnp.exp(-x))


def _silu(x):
    return x * _sigmoid(x)


def _softplus(x):
    return jnp.maximum(x, 0.0) + jnp.log(1.0 + jnp.exp(-jnp.abs(x)))


def _head_ones(width, scale):
    r = lax.shift_right_logical(lax.broadcasted_iota(jnp.int32, (width, width), 0), HALF_SHIFT)
    c = lax.shift_right_logical(lax.broadcasted_iota(jnp.int32, (width, width), 1), HALF_SHIFT)
    return jnp.where(r == c, scale, 0.0).astype(F32)


def _rms_matmul_kernel(x_ref, g_ref, w_ref, o_ref, xn_ref):
    @pl.when(pl.program_id(1) == 0)
    def _():
        x = x_ref[...]
        ms = jnp.mean(x * x, axis=-1, keepdims=True)
        xn_ref[...] = (x * lax.rsqrt(ms + NORM_EPS) * g_ref[...]).astype(BF16)

    o_ref[...] = jnp.dot(xn_ref[...], w_ref[...], preferred_element_type=F32)


def _rms_matmul(x, g, w, tm, tn):
    t, d = x.shape
    n = w.shape[1]
    return pl.pallas_call(
        _rms_matmul_kernel,
        grid=(t // tm, n // tn),
        in_specs=[pl.BlockSpec((tm, d), lambda i, j: (i, 0)),
                  pl.BlockSpec((1, d), lambda i, j: (0, 0)),
                  pl.BlockSpec((d, tn), lambda i, j: (0, j))],
        out_specs=pl.BlockSpec((tm, tn), lambda i, j: (i, j)),
        out_shape=jax.ShapeDtypeStruct((t, n), F32),
        scratch_shapes=[pltpu.VMEM((tm, d), BF16)],
        compiler_params=_cparams(("parallel", "arbitrary")),
        name="in_proj",
    )(x, g, w)


def _topk_bias(gate, n_valid, axis=1):
    pos = lax.broadcasted_iota(jnp.int32, gate.shape, axis)
    pos_f = pos.astype(F32)
    gm = jnp.where(pos < n_valid, gate, NEG_INF)
    selected = jnp.zeros(gate.shape, jnp.bool_)
    for j in range(MOBA_TOPK):
        m = jnp.max(gm, axis=axis, keepdims=True)
        idx = jnp.min(jnp.where(gm == m, pos_f, 1e9), axis=axis, keepdims=True)
        hit = pos_f == idx
        selected = jnp.logical_or(selected, jnp.logical_and(hit, j < n_valid))
        gm = jnp.where(hit, -jnp.inf, gm)
    return jnp.where(selected, 0.0, NEG_INF).astype(F32)


def _bf16_pieces(x, n):
    out = []
    rem = x
    for _ in range(n):
        piece = rem.astype(BF16).astype(F32)
        out.append(piece)
        rem = rem - piece
    return out


def _moba_prompt_kernel(*refs, nb, nq):
    def tile(i, carry):
        _moba_prompt_tile(i, *refs, nb=nb)
        return carry

    lax.fori_loop(0, nq, tile, 0)


def _moba_prompt_tile(i, q_ref, k_ref, v_ref, sl_ref, o_ref, kt_out, vt_out, kf_ref, vth_ref, kmean_ref, sel_ref,
                      acc_ref, raw_a, raw_b, m_ref, knorm_ref, *, nb):
    B = MOBA_BLOCK
    n_piece = 3
    tile_rows = pl.ds(pl.multiple_of(i * Q_BLOCK, Q_BLOCK), Q_BLOCK)
    lane = lax.broadcasted_iota(jnp.int32, (1, LANES), 1)
    halves = (lane < HALF, lane >= HALF)

    @pl.when(i == 0)
    def _():
        kmean_ref[...] = jnp.zeros_like(kmean_ref)
        key_local = jnp.bitwise_and(lax.broadcasted_iota(jnp.int32, (k_ref.shape[1], 1), 0), B - 1).astype(F32)
        extra = jnp.where(lane < n_piece, key_local, jnp.where(lane < 2 * n_piece, 1.0, 0.0))
        kf_ref[...] = jnp.concatenate([k_ref[0], extra], axis=1).astype(BF16)
        ones_rows = jnp.ones((SUBLANES, B), F32)
        knorm = [jnp.zeros((1, LANES), F32), jnp.zeros((1, LANES), F32)]
        for n in range(nb):
            kblk = k_ref[0, n * B:(n + 1) * B, :]
            kmean_ref[n:n + 1, :] = jnp.mean(kblk, axis=0, keepdims=True)
            vt = v_ref[0, n * B:(n + 1) * B, :].T
            kt_out[0, :, n * B:(n + 1) * B] = kblk.T
            vt_out[0, :, n * B:(n + 1) * B] = vt
            for h in range(2):
                vth_ref[h, n] = jnp.concatenate([vt[h * HALF:(h + 1) * HALF], ones_rows], axis=0).astype(BF16)
                norm2 = jnp.sum(jnp.where(halves[h], kblk * kblk, 0.0), axis=-1, keepdims=True)
                knorm[h] = jnp.where(lane == n, jnp.max(norm2, axis=0, keepdims=True), knorm[h])
        for h in range(2):
            knorm_ref[h] = jnp.broadcast_to(knorm[h], (SUBLANES, LANES))

    own = i
    q = q_ref[0, tile_rows, :]
    log2e = 1.0 / math.log(2.0)
    slope2 = (sl_ref[0, :, 0:1] * log2e, sl_ref[0, :, HALF:HALF + 1] * log2e)
    kmean = kmean_ref[...]
    q_local = lax.broadcasted_iota(jnp.int32, (Q_BLOCK, 1), 0).astype(F32)
    q_rows = []
    for h in range(2):
        q_m = jnp.where(halves[h], q, 0.0)
        bias_t = _topk_bias(_dot_nt_hi(kmean, q_m), own, axis=0)
        for n in range(nb):
            sel_ref[h, n] = jnp.broadcast_to(bias_t[n:n + 1, :], (SUBLANES, Q_BLOCK))
        pieces = _bf16_pieces(slope2[h], n_piece) + _bf16_pieces(-slope2[h] * q_local, n_piece)
        extra = jnp.zeros((Q_BLOCK, LANES), F32)
        for j, piece in enumerate(pieces):
            extra = jnp.where(lane == j, piece, extra)
        q_rows.append(jnp.concatenate([q_m * (ATTN_SCALE * log2e), extra], axis=1))
    qf = jnp.concatenate(q_rows, axis=0).astype(BF16)

    rel = (lax.broadcasted_iota(jnp.int32, (B, Q_BLOCK), 0) - lax.broadcasted_iota(jnp.int32, (B, Q_BLOCK), 1))

    def scores(n):
        both = _dot_nt(kf_ref[pl.ds(pl.multiple_of(n * B, B), B), :], qf)
        return [both[:, h * Q_BLOCK:(h + 1) * Q_BLOCK] for h in range(2)]

    def put_scores(n, dst_ref):
        for h, s in enumerate(scores(n)):
            dst_ref[h] = s

    def row_of(ref, h):
        return ref[h][0:1, :]

    def put_row(ref, h, x):
        ref[h] = jnp.broadcast_to(x, (SUBLANES, Q_BLOCK))

    def attend(n, src_ref):
        shift = ((n - i) * B).astype(F32)
        ps, alphas = [], []
        for h in range(2):
            s = src_ref[h] + (sel_ref[h, n][0:1, :] + slope2[h] * shift)
            m0 = row_of(m_ref, h)
            m1 = jnp.maximum(m0, jnp.max(s, axis=0, keepdims=True))
            alphas.append(jnp.exp2(m0 - m1))
            ps.append(jnp.exp2(s - m1).astype(BF16))
            put_row(m_ref, h, m1)
        pv = [jnp.dot(vth_ref[h, n], ps[h], preferred_element_type=F32) for h in range(2)]
        for h in range(2):
            acc_ref[h] = alphas[h] * acc_ref[h] + pv[h]

    own_raw = scores(own)
    lane_f = lane.astype(F32)
    bound_c = ATTN_SCALE * log2e * 1.02
    skips = []
    for h in range(2):
        q_m = jnp.where(halves[h], q, 0.0)
        qn2 = jnp.max(jnp.sum(q_m * q_m, axis=-1, keepdims=True), axis=0, keepdims=True)
        ub = jnp.sqrt(qn2 * knorm_ref[h][0:1, :]) * bound_c
        ub_own = jnp.sum(jnp.where(lane == own, ub, 0.0), axis=-1, keepdims=True)
        far = slope2[h] * ((lane_f - own.astype(F32)) * B + (B - 1))
        skips.append(ub + far < -ub_own - 160.0)
    keep = jnp.logical_and(lane < own, jnp.logical_not(jnp.logical_and(skips[0], skips[1])))
    first = jnp.min(jnp.where(keep, lane_f, own.astype(F32))).astype(jnp.int32)
    n_visit = own - first

    put_scores(jnp.minimum(first, nb - 1), raw_a)

    own_p = []
    for h in range(2):
        s = jnp.where(rel <= 0, own_raw[h], NEG_INF)
        m = jnp.max(s, axis=0, keepdims=True)
        put_row(m_ref, h, m)
        own_p.append(jnp.exp2(s - m).astype(BF16))
    for h in range(2):
        acc_ref[h] = jnp.dot(vth_ref[h, own], own_p[h], preferred_element_type=F32)

    def body(j, carry):
        n0 = first + 2 * j
        put_scores(n0 + 1, raw_b)
        attend(n0, raw_a)
        put_scores(jnp.minimum(n0 + 2, nb - 1), raw_a)
        attend(n0 + 1, raw_b)
        return carry

    lax.fori_loop(0, n_visit // 2, body, 0)

    @pl.when(n_visit % 2 == 1)
    def _():
        attend(own - 1, raw_a)

    out_t = jnp.concatenate([acc_ref[h][0:HALF] / acc_ref[h][HALF:HALF + 1] for h in range(2)], axis=0)
    o_ref[0, tile_rows, :] = out_t.T


def _moba_prompt(proj3, slopes_pair):
    bsz, seq, _ = proj3.shape
    assert seq % MOBA_BLOCK == 0 and seq // MOBA_BLOCK >= MOBA_TOPK
    nb = seq // MOBA_BLOCK
    nbp = -(-nb // SUBLANES) * SUBLANES
    nq = seq // Q_BLOCK
    qc, kc, vc = C_Q // LANES, C_K // LANES, C_V // LANES
    return pl.pallas_call(
        functools.partial(_moba_prompt_kernel, nb=nb, nq=nq),
        grid=(bsz, N_PAIRS),
        in_specs=[pl.BlockSpec((1, seq, LANES), lambda b, p: (b, 0, qc + p)),
                  pl.BlockSpec((1, seq, LANES), lambda b, p: (b, 0, kc + p)),
                  pl.BlockSpec((1, seq, LANES), lambda b, p: (b, 0, vc + p)),
                  pl.BlockSpec((1, 1, LANES), lambda b, p: (p, 0, 0))],
        out_specs=[pl.BlockSpec((1, seq, LANES), lambda b, p: (b, 0, p)),
                   pl.BlockSpec((1, LANES, seq), lambda b, p: (b, p, 0)),
                   pl.BlockSpec((1, LANES, seq), lambda b, p: (b, p, 0))],
        out_shape=[jax.ShapeDtypeStruct((bsz, seq, A_W), F32),
                   jax.ShapeDtypeStruct((bsz, A_W, seq), F32), jax.ShapeDtypeStruct((bsz, A_W, seq), F32)],
        scratch_shapes=[pltpu.VMEM((seq, 2 * LANES), BF16),
                        pltpu.VMEM((2, nb, HALF + SUBLANES, MOBA_BLOCK), BF16),
                        pltpu.VMEM((nbp, LANES), F32), pltpu.VMEM((2, nb, SUBLANES, Q_BLOCK), F32),
                        pltpu.VMEM((2, HALF + SUBLANES, Q_BLOCK), F32),
                        pltpu.VMEM((2, MOBA_BLOCK, Q_BLOCK), F32), pltpu.VMEM((2, MOBA_BLOCK, Q_BLOCK), F32),
                        pltpu.VMEM((2, SUBLANES, Q_BLOCK), F32), pltpu.VMEM((2, SUBLANES, LANES), F32)],
        compiler_params=_cparams(("parallel", "parallel")),
        name="moba_prompt",
    )(proj3, proj3, proj3, slopes_pair)


def _moba_sample_kernel(pt_ref, q_ref, kn_ref, vn_ref, *refs, n_blk, ppb, page, tnew):
    del pt_ref
    n_pages = n_blk * ppb
    k_refs, v_refs = refs[:n_pages], refs[n_pages:2 * n_pages]
    o_ref, m_ref, l_ref, g_ref, acc_ref = refs[2 * n_pages:]
    past = n_blk * MOBA_BLOCK
    rows = A_HEADS * SUBLANES
    lane_head = lax.shift_right_logical(lax.broadcasted_iota(jnp.int32, (SUBLANES, A_W), 1), HALF_SHIFT)
    q8 = q_ref[0] * ATTN_SCALE
    qbd = jnp.concatenate([jnp.where(lane_head == h, q8, 0.0) for h in range(A_HEADS)], axis=0)
    q_hi = qbd.astype(BF16)
    q_lo = (qbd - q_hi.astype(F32)).astype(BF16)
    r = lax.broadcasted_iota(jnp.int32, (rows, 1), 0)
    tok = jnp.bitwise_and(r, SUBLANES - 1).astype(F32)
    slope = jnp.exp2(-(8.0 / A_HEADS) * (lax.shift_right_logical(r, SUBLANE_SHIFT) + 1).astype(F32))
    key = lax.broadcasted_iota(jnp.int32, (1, MOBA_BLOCK), 1).astype(F32)

    def block_t(refs_, n):
        return jnp.concatenate([refs_[j][0, 0].reshape(A_W, page) for j in range(n * ppb, (n + 1) * ppb)],
                               axis=1).astype(BF16)

    raws, fixes = [], []
    for n in range(n_blk):
        kt = block_t(k_refs, n)
        raws.append(jnp.dot(q_hi, kt, preferred_element_type=F32))
        fixes.append(jnp.dot(q_lo, kt, preferred_element_type=F32))
    es = []
    for n in range(n_blk):
        g_ref[n] = jnp.broadcast_to(jnp.sum(raws[n] + fixes[n], axis=-1, keepdims=True) * (1.0 / MOBA_BLOCK),
                                    (rows, LANES))
        s = raws[n] - slope * ((past - n * MOBA_BLOCK + tok) - key)
        m = jnp.max(s, axis=-1, keepdims=True)
        e = jnp.exp(s - m)
        m_ref[n] = jnp.broadcast_to(m, (rows, LANES))
        l_ref[n] = jnp.broadcast_to(jnp.sum(e, axis=-1, keepdims=True), (rows, LANES))
        es.append(e.astype(BF16))
    for n in range(n_blk):
        acc_ref[n] = lax.dot_general(es[n], block_t(v_refs, n), (((1,), (1,)), ((), ())),
                                     preferred_element_type=F32)

    lane = lax.broadcasted_iota(jnp.int32, (rows, LANES), 1)
    gate = jnp.zeros((rows, LANES), F32)
    for j in range(n_blk):
        gate = jnp.where(lane == j, g_ref[j], gate)
    bias = _topk_bias(gate, n_blk)

    kn = kn_ref[0]
    vn = vn_ref[0]
    s_own = []
    for j in range(tnew):
        sj = jnp.sum(qbd * kn[j:j + 1, :], axis=-1, keepdims=True) - slope * (tok - j)
        s_own.append(jnp.where(tok >= j, sj, NEG_INF))
    mx = s_own[0]
    for j in range(1, tnew):
        mx = jnp.maximum(mx, s_own[j])
    mb = []
    for j in range(n_blk):
        mj = m_ref[j][:, 0:1] + bias[:, j:j + 1]
        mb.append(mj)
        mx = jnp.maximum(mx, mj)
    lsum = jnp.zeros((rows, 1), F32)
    acc = jnp.zeros((rows, A_W), F32)
    for j in range(tnew):
        w = jnp.exp(s_own[j] - mx)
        lsum = lsum + w
        acc = acc + w * vn[j:j + 1, :]
    for j in range(n_blk):
        w = jnp.exp(mb[j] - mx)
        lsum = lsum + w * l_ref[j][:, 0:1]
        acc = acc + w * acc_ref[j]
    out = acc / lsum
    o8 = jnp.zeros((SUBLANES, A_W), F32)
    for h in range(A_HEADS):
        o8 = o8 + jnp.where(lane_head == h, out[h * SUBLANES:(h + 1) * SUBLANES, :], 0.0)
    o_ref[0] = o8


def _moba_sample(q8, k8, v8, cache_kt, cache_vt, page_table, layer, tnew):
    dbsz = q8.shape[0]
    n_pages = page_table.shape[1]
    page = cache_kt.shape[4]
    assert MOBA_BLOCK % page == 0 and (n_pages * page) % MOBA_BLOCK == 0 and page % LANES == 0
    ppb = MOBA_BLOCK // page
    n_blk = n_pages // ppb
    assert n_blk >= MOBA_TOPK and tnew <= SUBLANES
    rows = A_HEADS * SUBLANES
    tok_spec = pl.BlockSpec((1, SUBLANES, A_W), lambda b, pt: (b, 0, 0))

    def page_spec(j):
        return pl.BlockSpec((1, 1, A_HEADS, A_HD, page), lambda b, pt: (layer, pt[b * n_pages + j], 0, 0, 0))

    pages = [page_spec(j) for j in range(n_pages)]
    grid_spec = pltpu.PrefetchScalarGridSpec(
        num_scalar_prefetch=1,
        grid=(dbsz,),
        in_specs=[tok_spec, tok_spec, tok_spec] + pages + pages,
        out_specs=tok_spec,
        scratch_shapes=[pltpu.VMEM((n_blk, rows, LANES), F32), pltpu.VMEM((n_blk, rows, LANES), F32),
                        pltpu.VMEM((n_blk, rows, LANES), F32), pltpu.VMEM((n_blk, rows, A_W), F32)],
    )
    return pl.pallas_call(
        functools.partial(_moba_sample_kernel, n_blk=n_blk, ppb=ppb, page=page, tnew=tnew),
        grid_spec=grid_spec,
        out_shape=jax.ShapeDtypeStruct((dbsz, SUBLANES, A_W), F32),
        compiler_params=_cparams(("parallel",)),
        name="moba_sample",
    )(page_table.reshape(-1), q8, k8, v8, *([cache_kt] * n_pages), *([cache_vt] * n_pages))


def _ssd_prompt_kernel(z_ref, xbc_ref, dt_ref, cw_ref, cb_ref, dtb_ref, alog_ref, dvec_ref, ng_ref,
                       o_ref, hout_ref, ext_ref, h_ref, *, L):
    c = pl.program_id(1)
    P = SUBLANES

    @pl.when(c == 0)
    def _():
        ext_ref[0:P, :] = jnp.zeros((P, S_CONV_CH), F32)
        h_ref[...] = jnp.zeros_like(h_ref)

    @pl.when(c > 0)
    def _():
        ext_ref[0:P, :] = ext_ref[L:L + P, :]

    ext_ref[P:P + L, :] = xbc_ref[...]
    acc = cb_ref[...] + cw_ref[S_CONV - 1:S_CONV, :] * ext_ref[P:P + L, :]
    for d in range(1, S_CONV):
        acc = acc + cw_ref[S_CONV - 1 - d:S_CONV - d, :] * ext_ref[P - d:P - d + L, :]
    xbc = _silu(acc)
    xs = xbc[:, :S_W]
    bm = xbc[:, S_W:S_W + S_GROUPS * S_STATE]
    cm = xbc[:, S_W + S_GROUPS * S_STATE:]

    lane = lax.broadcasted_iota(jnp.int32, (1, LANES), 1)
    is_a = lane < HALF
    dt = jnp.where(lane < S_HEADS, _softplus(dt_ref[...] + dtb_ref[...]), 0.0)
    da = dt * (-jnp.exp(alog_ref[...]))
    rr = lax.broadcasted_iota(jnp.int32, (L, L), 0)
    cc = lax.broadcasted_iota(jnp.int32, (L, L), 1)
    causal = rr >= cc
    cum = _split_dot(da, jnp.where(causal, 1.0, 0.0), 3, x_is_lhs=False)
    cum_t = cum.T
    dt_t = dt.T
    row_a = lax.broadcasted_iota(jnp.int32, (2 * S_HD, 1), 0) < S_HD

    group_of = [(2 * pr * S_GROUPS) // S_HEADS for pr in range(N_PAIRS)]
    bgs = [bm[:, g * S_STATE:(g + 1) * S_STATE] for g in range(S_GROUPS)]
    cgs = [cm[:, g * S_STATE:(g + 1) * S_STATE] for g in range(S_GROUPS)]
    cb_g = [_dot_nt(cgs[g], bgs[g]) for g in range(S_GROUPS)]
    xs_ps = [xs[:, pr * LANES:(pr + 1) * LANES] for pr in range(N_PAIRS)]
    hps = [h_ref[pr] for pr in range(N_PAIRS)]
    halves = (is_a, jnp.logical_not(is_a))
    cum_c = [cum[:, h:h + 1] for h in range(S_HEADS)]
    last = [cum[L - 1:L, h:h + 1] for h in range(S_HEADS)]
    wts = []
    for h in range(S_HEADS):
        seg = cum_c[h] - cum_t[h:h + 1, :]
        dec = jnp.exp(jnp.where(causal, seg, -jnp.inf))
        wts.append(cb_g[group_of[h // 2]] * dec * dt_t[h:h + 1, :])
    intra = [_dot(wts[h], jnp.where(halves[h % 2], xs_ps[h // 2], 0.0)) for h in range(S_HEADS)]
    inter = [_dot_nt(cgs[group_of[pr]], hps[pr]) for pr in range(N_PAIRS)]
    ys = [intra[2 * pr] + intra[2 * pr + 1]
          + inter[pr] * jnp.where(is_a, jnp.exp(cum_c[2 * pr]), jnp.exp(cum_c[2 * pr + 1])) for pr in range(N_PAIRS)]
    tes = [jnp.where(is_a, jnp.exp(last[2 * pr] - cum_c[2 * pr]) * dt[:, 2 * pr:2 * pr + 1],
                     jnp.exp(last[2 * pr + 1] - cum_c[2 * pr + 1]) * dt[:, 2 * pr + 1:2 * pr + 2])
           for pr in range(N_PAIRS)]
    sts = [_dot_tn(xs_ps[pr] * tes[pr], bgs[group_of[pr]]) for pr in range(N_PAIRS)]
    for pr in range(N_PAIRS):
        h_ref[pr] = hps[pr] * jnp.where(row_a, jnp.exp(last[2 * pr]), jnp.exp(last[2 * pr + 1])) + sts[pr]
    y = jnp.concatenate(ys, axis=1) + dvec_ref[...] * xs
    yz = y * _silu(z_ref[...])
    gw = S_W // S_GROUPS
    for g in range(S_GROUPS):
        part = yz[:, g * gw:(g + 1) * gw]
        ms = jnp.mean(part * part, axis=-1, keepdims=True)
        o_ref[:, g * gw:(g + 1) * gw] = part * lax.rsqrt(ms + NORM_EPS) * ng_ref[:, g * gw:(g + 1) * gw]

    @pl.when(c == pl.num_programs(1) - 1)
    def _():
        hout_ref[0] = h_ref[...]


def _ssd_prompt(proj, bsz, seq, cw, cb, dtb, alog, dvec, ng):
    L = SSD_CHUNK
    nc = seq // L
    const = lambda shape: pl.BlockSpec(shape, lambda b, c: (0,) * len(shape))
    return pl.pallas_call(
        functools.partial(_ssd_prompt_kernel, L=L),
        grid=(bsz, nc),
        in_specs=[pl.BlockSpec((L, S_W), lambda b, c: (b * nc + c, C_Z // S_W)),
                  pl.BlockSpec((L, S_CONV_CH), lambda b, c: (b * nc + c, C_XBC // S_CONV_CH)),
                  pl.BlockSpec((L, LANES), lambda b, c: (b * nc + c, C_DT // LANES)),
                  const((S_CONV, S_CONV_CH)), const((1, S_CONV_CH)), const((1, LANES)), const((1, LANES)),
                  const((1, S_W)), const((1, S_W))],
        out_specs=[pl.BlockSpec((L, S_W), lambda b, c: (b * nc + c, 0)),
                   pl.BlockSpec((1, N_PAIRS, 2 * S_HD, S_STATE), lambda b, c: (b, 0, 0, 0))],
        out_shape=[jax.ShapeDtypeStruct((bsz * seq, S_W), F32),
                   jax.ShapeDtypeStruct((bsz, N_PAIRS, 2 * S_HD, S_STATE), F32)],
        scratch_shapes=[pltpu.VMEM((L + 2 * SUBLANES, S_CONV_CH), F32),
                        pltpu.VMEM((N_PAIRS, 2 * S_HD, S_STATE), F32)],
        compiler_params=_cparams(("parallel", "arbitrary")),
        name="ssd_prompt",
    )(proj, proj, proj, cw, cb, dtb, alog, dvec, ng)


def _ssd_sample_kernel(z_ref, xbc_ref, dt_ref, pre_ref, h0_ref, cw_ref, cb_ref, dtbv_ref, alogv_ref,
                       dvec_ref, ng_ref, o_ref, hout_ref, *, T, nb):
    rows2 = 2 * S_HD
    eye = (lax.broadcasted_iota(jnp.int32, (rows2, LANES), 0)
           == lax.broadcasted_iota(jnp.int32, (rows2, LANES), 1)).astype(F32)
    eye_all = jnp.concatenate([eye] * nb, axis=0)
    ones = jnp.ones((LANES, LANES), F32)
    spread = (lax.broadcasted_iota(jnp.int32, (LANES, S_W), 0)
              == lax.shift_right_logical(lax.broadcasted_iota(jnp.int32, (LANES, S_W), 1), HALF_SHIFT)).astype(F32)
    a_vec = -jnp.exp(alogv_ref[...])
    gw = S_W // S_GROUPS

    def as_columns(x, passes):
        return _split_dot(eye_all * _repeat_rows(x, rows2), ones, passes)

    up = [pre_ref[j] for j in range(S_CONV - 1)] + [xbc_ref[t] for t in range(T)]
    hs = [h0_ref[0, :, pr].reshape(nb * rows2, S_STATE) for pr in range(N_PAIRS)]
    for t in range(T):
        acc = cb_ref[...] + cw_ref[0:1, :] * up[t]
        for j in range(1, S_CONV):
            acc = acc + cw_ref[j:j + 1, :] * up[t + j]
        xc = _silu(acc)
        xs = xc[:, :S_W]
        dt = _softplus(_split_dot(dt_ref[t], spread, 3) + dtbv_ref[...])
        decay = jnp.exp(dt * a_vec)
        xdt = xs * dt
        ys = []
        for pr in range(N_PAIRS):
            g = (2 * pr * S_GROUPS) // S_HEADS
            sl = slice(pr * LANES, (pr + 1) * LANES)
            b_rows = _repeat_rows(xc[:, S_W + g * S_STATE:S_W + (g + 1) * S_STATE], rows2)
            c_rows = _repeat_rows(xc[:, S_W + (S_GROUPS + g) * S_STATE:S_W + (S_GROUPS + g + 1) * S_STATE], rows2)
            hs[pr] = hs[pr] * as_columns(decay[:, sl], 3) + as_columns(xdt[:, sl], 2) * b_rows
            y_col = _split_dot(hs[pr] * c_rows, ones, 2)
            ys.append(jnp.sum((eye_all * y_col).reshape(nb, rows2, LANES), axis=1))
        y = jnp.concatenate(ys, axis=1) + dvec_ref[...] * xs
        yz = y * _silu(z_ref[t])
        for g in range(S_GROUPS):
            part = yz[:, g * gw:(g + 1) * gw]
            ms = jnp.mean(part * part, axis=-1, keepdims=True)
            o_ref[t, :, g * gw:(g + 1) * gw] = part * lax.rsqrt(ms + NORM_EPS) * ng_ref[:, g * gw:(g + 1) * gw]
    for pr in range(N_PAIRS):
        hout_ref[:, pr] = hs[pr].reshape(nb, rows2, S_STATE)


def _ssd_sample(proj3, pre3, h0, layer, cw, cb, dtbv, alogv, dvec, ng, nb=8):
    T, dbsz = proj3.shape[:2]
    const = lambda shape: pl.BlockSpec(shape, lambda j: (0,) * len(shape))
    hspec = pl.BlockSpec((nb, N_PAIRS, 2 * S_HD, S_STATE), lambda j: (j, 0, 0, 0))
    h0spec = pl.BlockSpec((1, nb, N_PAIRS, 2 * S_HD, S_STATE), lambda j: (layer, j, 0, 0, 0))
    return pl.pallas_call(
        functools.partial(_ssd_sample_kernel, T=T, nb=nb),
        grid=(dbsz // nb,),
        in_specs=[pl.BlockSpec((T, nb, S_W), lambda j: (0, j, C_Z // S_W)),
                  pl.BlockSpec((T, nb, S_CONV_CH), lambda j: (0, j, C_XBC // S_CONV_CH)),
                  pl.BlockSpec((T, nb, LANES), lambda j: (0, j, C_DT // LANES)),
                  pl.BlockSpec((S_CONV - 1, nb, S_CONV_CH), lambda j: (0, j, 0)),
                  h0spec,
                  const((S_CONV, S_CONV_CH)), const((1, S_CONV_CH)), const((1, S_W)), const((1, S_W)),
                  const((1, S_W)), const((1, S_W))],
        out_specs=[pl.BlockSpec((T, nb, S_W), lambda j: (0, j, 0)), hspec],
        out_shape=[jax.ShapeDtypeStruct((T, dbsz, S_W), F32),
                   jax.ShapeDtypeStruct((dbsz, N_PAIRS, 2 * S_HD, S_STATE), F32)],
        compiler_params=_cparams(("parallel",)),
        name="ssd_sample",
    )(proj3, proj3, proj3, pre3, h0, cw, cb, dtbv, alogv, dvec, ng)


def _rwkv_prep_kernel(u_ref, pre_ref, mu_ref, w0_ref, w2_ref, a0_ref, a2_ref, g2_ref, kk_ref, ka_ref,
                      r_out, w_out, k_out, v_out, kk_out, kka_out, g_out, ext_ref,
                      *, tm, P, stride, tiles_per_seq):
    i = pl.program_id(0)
    first = (i % tiles_per_seq) == 0

    @pl.when(first)
    def _():
        ext_ref[0:P, :] = pre_ref[...]

    @pl.when(jnp.logical_not(first))
    def _():
        ext_ref[0:P, :] = ext_ref[tm:tm + P, :]

    u = u_ref[...]
    ext_ref[P:P + tm, :] = u
    prev = ext_ref[P - stride:P - stride + tm, :]
    x = u + (prev - u) * mu_ref[...]
    r = x[:, 0:R_W]
    kr = x[:, R_W:2 * R_W]
    vr = x[:, 2 * R_W:3 * R_W]
    xl = x[:, 3 * R_W:]
    w_log = w0_ref[...] + _dot(jnp.tanh(xl), w2_ref[...])
    log_decay = -jnp.exp(-_softplus(-w_log) - 0.5)
    a = _sigmoid(a0_ref[...] + _dot(xl, a2_ref[...]))
    g = _dot(_sigmoid(xl), g2_ref[...])
    kk = kr * kk_ref[...]
    ss = _split_dot(kk * kk, _head_ones(R_W, 1.0), 2)
    kk = kk / jnp.maximum(jnp.sqrt(ss), 1e-12)
    r_out[...] = r
    w_out[...] = log_decay
    k_out[...] = kr * (1.0 + (a - 1.0) * ka_ref[...])
    v_out[...] = vr
    kk_out[...] = kk
    kka_out[...] = kk * a
    g_out[...] = g


def _rwkv_prep(proj, pre, mu, w0, w2p, a0, a2p, g2p, kkw, kaw, tm, stride, tiles_per_seq):
    t = proj.shape[0]
    P = pre.shape[0]
    const = lambda shape: pl.BlockSpec(shape, lambda i: (0,) * len(shape))
    outs = pl.pallas_call(
        functools.partial(_rwkv_prep_kernel, tm=tm, P=P, stride=stride, tiles_per_seq=tiles_per_seq),
        grid=(t // tm,),
        in_specs=[pl.BlockSpec((tm, R_IN_W), lambda i: (i, 0)),
                  const((P, R_IN_W)), const((1, R_IN_W)), const((1, R_W)), const((R_LORA, R_W)),
                  const((1, R_W)), const((R_LORA, R_W)), const((R_LORA, R_W)), const((1, R_W)), const((1, R_W))],
        out_specs=[pl.BlockSpec((tm, R_W), lambda i: (i, 0))] * 7,
        out_shape=[jax.ShapeDtypeStruct((t, R_W), F32)] * 7,
        scratch_shapes=[pltpu.VMEM((tm + 2 * P, R_IN_W), F32)],
        compiler_params=_cparams(("arbitrary",)),
        name="rwkv_prep",
    )(proj, pre, mu, w0, w2p, a0, a2p, g2p, kkw, kaw)
    return outs


def _pair_consts():
    lane = lax.broadcasted_iota(jnp.int32, (R_HD, LANES), 1)
    row = lax.broadcasted_iota(jnp.int32, (R_HD, LANES), 0)
    is_a = lane < HALF
    eye2 = (jnp.bitwise_and(lane, HALF - 1) == row).astype(F32)
    return is_a, eye2


def _unit_lower_inverses(ns):
    L = ns[0].shape[0]
    eye = (lax.broadcasted_iota(jnp.int32, (L, L), 0) == lax.broadcasted_iota(jnp.int32, (L, L), 1)).astype(F32)
    ts = [eye + n for n in ns]
    pws = [_dot(n, n) for n in ns]
    for _ in range(int(math.log2(L)) - 2):
        both = [_dot(jnp.concatenate([t, p], axis=0), p) for t, p in zip(ts, pws)]
        ts = [t + b[:L] for t, b in zip(ts, both)]
        pws = [b[L:] for b in both]
    return [t + _dot(t, p) for t, p in zip(ts, pws)]


def _rwkv_chunk_prompt_kernel(r_ref, lw_ref, k_ref, v_ref, kk_ref, kka_ref, y_ref, sout_ref, s_ref, *, L):
    c = pl.program_id(1)

    @pl.when(c == 0)
    def _():
        s_ref[...] = jnp.zeros_like(s_ref)

    lane = lax.broadcasted_iota(jnp.int32, (1, LANES), 1)
    is_a = lane < HALF
    rr = lax.broadcasted_iota(jnp.int32, (L, L), 0)
    cc = lax.broadcasted_iota(jnp.int32, (L, L), 1)
    incl = rr >= cc
    strict = rr > cc
    tri = jnp.where(incl, 1.0, 0.0).astype(F32)
    r2 = lax.broadcasted_iota(jnp.int32, (LANES, LANES), 0) < HALF
    c2 = lax.broadcasted_iota(jnp.int32, (LANES, LANES), 1) < HALF
    same_head = r2 == c2

    r, lw, k, v, kk, kka = (ref[...] for ref in (r_ref, lw_ref, k_ref, v_ref, kk_ref, kka_ref))
    cum = _split_dot(lw, tri, 3, x_is_lhs=False)
    last = cum[L - 1:L, :]
    inv_p = jnp.exp(-cum)
    to_end = jnp.exp(last - cum)
    b_t = kk * jnp.exp(cum - lw)
    a_t = -kka * inv_p
    k_t = k * inv_p
    r_t = r * jnp.exp(cum)
    a_end = -kka * to_end
    k_end = k * to_end
    decay_end = jnp.exp(last)
    pairs = [slice(pr * LANES, (pr + 1) * LANES) for pr in range(N_PAIRS)]
    halves = (is_a, jnp.logical_not(is_a))
    s0 = [s_ref[pr] for pr in range(N_PAIRS)]

    ns, mks, rak = [], [], []
    for sl in pairs:
        cols = jnp.concatenate([a_t[:, sl], k_t[:, sl]], axis=0)
        for half in halves:
            lhs = jnp.concatenate([jnp.where(half, b_t[:, sl], 0.0), jnp.where(half, r_t[:, sl], 0.0)], axis=0)
            g4 = _dot_nt(lhs, cols)
            ns.append(jnp.where(strict, g4[:L, :L], 0.0))
            mks.append(jnp.where(strict, g4[:L, L:], 0.0))
            rak.append(jnp.concatenate([jnp.where(incl, g4[L:, :L], 0.0), jnp.where(incl, g4[L:, L:], 0.0)], axis=1))
    ts = _unit_lower_inverses(ns)
    from_state = [_dot_nt(jnp.concatenate([b_t[:, sl], r_t[:, sl]], axis=0), s0[pr]) for pr, sl in enumerate(pairs)]
    rhs = []
    for pr, sl in enumerate(pairs):
        mv = _dot(jnp.concatenate([mks[2 * pr], mks[2 * pr + 1]], axis=0), v[:, sl])
        rhs.append(from_state[pr][:L] + jnp.where(is_a, mv[:L], mv[L:]))
    us = []
    for pr in range(N_PAIRS):
        tu = _dot(jnp.concatenate([ts[2 * pr], ts[2 * pr + 1]], axis=0), rhs[pr])
        us.append(jnp.where(is_a, tu[:L], tu[L:]))
    for pr, sl in enumerate(pairs):
        uv = jnp.concatenate([us[pr], v[:, sl]], axis=0)
        y_ref[:, sl] = from_state[pr][L:] + jnp.where(is_a, _dot(rak[2 * pr], uv), _dot(rak[2 * pr + 1], uv))
    for pr, sl in enumerate(pairs):
        upd = _dot_tn(jnp.concatenate([us[pr], v[:, sl]], axis=0),
                      jnp.concatenate([a_end[:, sl], k_end[:, sl]], axis=0))
        s_ref[pr] = s0[pr] * decay_end[:, sl] + jnp.where(same_head, upd, 0.0)

    @pl.when(c == pl.num_programs(1) - 1)
    def _():
        sout_ref[0] = s_ref[...]


def _rwkv_scan_prompt(seqs, bsz, seq, L):
    nc = seq // L
    spec = pl.BlockSpec((L, R_W), lambda b, c: (b * nc + c, 0))
    return pl.pallas_call(
        functools.partial(_rwkv_chunk_prompt_kernel, L=L),
        grid=(bsz, nc),
        in_specs=[spec] * 6,
        out_specs=[spec, pl.BlockSpec((1, N_PAIRS, LANES, LANES), lambda b, c: (b, 0, 0, 0))],
        out_shape=[jax.ShapeDtypeStruct((bsz * seq, R_W), F32),
                   jax.ShapeDtypeStruct((bsz, N_PAIRS, LANES, LANES), F32)],
        scratch_shapes=[pltpu.VMEM((N_PAIRS, LANES, LANES), F32)],
        compiler_params=_cparams(("parallel", "arbitrary")),
        name="rwkv_scan_prompt",
    )(*seqs)


def _repeat_rows(x, reps):
    return jnp.concatenate([jnp.broadcast_to(x[b:b + 1, :], (reps, x.shape[1])) for b in range(x.shape[0])], axis=0)


def _rwkv_scan_sample_kernel(r_ref, w_ref, k_ref, v_ref, kk_ref, kka_ref, s0_ref, y_ref, sout_ref, *, T, nb):
    _, eye2 = _pair_consts()
    eye_all = jnp.concatenate([eye2] * nb, axis=0)
    ones_bd = _head_ones(LANES, 1.0)

    def head_sums(x):
        return _split_dot(x, ones_bd, 2)

    states = [s0_ref[:, pr].reshape(nb * R_HD, LANES) for pr in range(N_PAIRS)]
    for t in range(T):
        for pr in range(N_PAIRS):
            sl = slice(pr * LANES, (pr + 1) * LANES)
            rr, ww, kr, vv, kk, kka = (_repeat_rows(x, R_HD) for x in (
                r_ref[t, :, sl], jnp.exp(w_ref[t, :, sl]), k_ref[t, :, sl], v_ref[t, :, sl],
                kk_ref[t, :, sl], kka_ref[t, :, sl]))
            v_col = head_sums(eye_all * vv)
            s = states[pr]
            s = s * ww - head_sums(s * kk) * kka + v_col * kr
            states[pr] = s
            y_col = head_sums(s * rr)
            y_ref[t, :, sl] = jnp.sum((eye_all * y_col).reshape(nb, R_HD, LANES), axis=1)
    for pr in range(N_PAIRS):
        sout_ref[:, pr] = states[pr].reshape(nb, R_HD, LANES)


def _rwkv_scan_sample(seqs3, s0, nb=16):
    T, dbsz = seqs3[0].shape[:2]
    spec = pl.BlockSpec((T, nb, R_W), lambda j: (0, j, 0))
    sspec = pl.BlockSpec((nb, N_PAIRS, R_HD, LANES), lambda j: (j, 0, 0, 0))
    return pl.pallas_call(
        functools.partial(_rwkv_scan_sample_kernel, T=T, nb=nb),
        grid=(dbsz // nb,),
        in_specs=[spec] * 6 + [sspec],
        out_specs=[spec, sspec],
        out_shape=[jax.ShapeDtypeStruct((T, dbsz, R_W), F32),
                   jax.ShapeDtypeStruct((dbsz, N_PAIRS, R_HD, LANES), F32)],
        compiler_params=_cparams(("parallel",)),
        name="rwkv_scan_sample",
    )(*seqs3, s0)


def _merge_kernel(x_ref, oa_ref, ob_ref, yr_ref, r_ref, k_ref, v_ref, g_ref, ga_ref, gb_ref, gc_ref,
                  bg_ref, lng_ref, lnb_ref, rk_ref, wpa_ref, wpb_ref, wpc_ref, wo_ref, o_ref):
    pa = _dot(oa_ref[...], wpa_ref[...])
    pb = _dot(ob_ref[...], wpb_ref[...])
    mean_m = _head_ones(R_W, 1.0 / R_HD)
    yr = yr_ref[...]
    bonus = _split_dot(r_ref[...] * k_ref[...] * rk_ref[...], _head_ones(R_W, 1.0), 2)
    d = yr - _split_dot(yr, mean_m, 2)
    var = _split_dot(d * d, mean_m, 2)
    yn = d * lax.rsqrt(var + GN_EPS) * lng_ref[...] + lnb_ref[...]
    oc = (yn + bonus * v_ref[...]) * g_ref[...]
    merged = (_sigmoid(ga_ref[...] + bg_ref[:, 0:D_MODEL]) * pa
              + _sigmoid(gb_ref[...] + bg_ref[:, D_MODEL:2 * D_MODEL]) * pb
              + _sigmoid(gc_ref[...] + bg_ref[:, 2 * D_MODEL:]) * _dot(oc, wpc_ref[...]))
    o_ref[...] = x_ref[...] + _dot(merged, wo_ref[...])


def _merge(x, proj, oa, ob, yr, r, k2, v, g, bg, lng, lnb, rk, wpa, wpb, wpc, wo, tm):
    t = x.shape[0]
    row = lambda w: pl.BlockSpec((tm, w), lambda i: (i, 0))
    const = lambda shape: pl.BlockSpec(shape, lambda i: (0,) * len(shape))
    gcol = C_GATE // D_MODEL
    gate = lambda j: pl.BlockSpec((tm, D_MODEL), lambda i: (i, gcol + j))
    return pl.pallas_call(
        _merge_kernel,
        grid=(t // tm,),
        in_specs=[row(D_MODEL), row(A_W), row(S_W), row(R_W), row(R_W), row(R_W), row(R_W), row(R_W),
                  gate(0), gate(1), gate(2),
                  const((1, 3 * D_MODEL)), const((1, R_W)), const((1, R_W)), const((1, R_W)),
                  const((A_W, D_MODEL)), const((S_W, D_MODEL)), const((R_W, D_MODEL)),
                  const((D_MODEL, D_MODEL))],
        out_specs=row(D_MODEL),
        out_shape=jax.ShapeDtypeStruct((t, D_MODEL), F32),
        compiler_params=_cparams(("parallel",)),
        name="merge",
    )(x, oa, ob, yr, r, k2, v, g, proj, proj, proj, bg, lng, lnb, rk, wpa, wpb, wpc, wo)


def _ffn_kernel(x_ref, g_ref, wug_ref, wuv_ref, wd_ref, cw_ref, cb_ref, pre_ref, gf_ref,
                o_ref, tail_ref, xn_ref, ext_ref, *, tm, P, stride, tiles_per_seq, final_norm):
    i = pl.program_id(0)
    f = pl.program_id(1)
    first = (i % tiles_per_seq) == 0

    @pl.when(f == 0)
    def _():
        x = x_ref[...]
        ms = jnp.mean(x * x, axis=-1, keepdims=True)
        xn_ref[...] = (x * lax.rsqrt(ms + NORM_EPS) * g_ref[...]).astype(BF16)

    @pl.when(first)
    def _():
        ext_ref[f, 0:P, :] = pre_ref[...]

    @pl.when(jnp.logical_not(first))
    def _():
        ext_ref[f, 0:P, :] = ext_ref[f, tm:tm + P, :]

    xn = xn_ref[...]
    ug = jnp.dot(xn, wug_ref[...], preferred_element_type=F32)
    uv = jnp.dot(xn, wuv_ref[...], preferred_element_type=F32)
    ext_ref[f, P:P + tm, :] = ug
    tail_ref[0] = ext_ref[f, tm:tm + P, :]
    acc = cb_ref[...] + cw_ref[F_CONV - 1:F_CONV, :] * ug
    for d in range(1, F_CONV):
        acc = acc + cw_ref[F_CONV - 1 - d:F_CONV - d, :] * ext_ref[f, P - d * stride:P - d * stride + tm, :]
    contrib = _dot(_silu(acc) * uv, wd_ref[...])

    @pl.when(f == 0)
    def _():
        o_ref[...] = x_ref[...] + contrib

    @pl.when(f > 0)
    def _():
        o_ref[...] = o_ref[...] + contrib

    if final_norm:
        @pl.when(f == pl.num_programs(1) - 1)
        def _():
            y = o_ref[...]
            ms = jnp.mean(y * y, axis=-1, keepdims=True)
            o_ref[...] = y * lax.rsqrt(ms + NORM_EPS) * gf_ref[...]


def _ffn(x, g, wup, wd, cw, cb, pre, gf, tm, tf, stride, tiles_per_seq, final_norm):
    t = x.shape[0]
    P = pre.shape[0]
    nf = D_FF // tf
    return pl.pallas_call(
        functools.partial(_ffn_kernel, tm=tm, P=P, stride=stride, tiles_per_seq=tiles_per_seq,
                          final_norm=final_norm),
        grid=(t // tm, nf),
        in_specs=[pl.BlockSpec((tm, D_MODEL), lambda i, f: (i, 0)),
                  pl.BlockSpec((1, D_MODEL), lambda i, f: (0, 0)),
                  pl.BlockSpec((D_MODEL, tf), lambda i, f: (0, f)),
                  pl.BlockSpec((D_MODEL, tf), lambda i, f: (0, nf + f)),
                  pl.BlockSpec((tf, D_MODEL), lambda i, f: (f, 0)),
                  pl.BlockSpec((F_CONV, tf), lambda i, f: (0, f)),
                  pl.BlockSpec((1, tf), lambda i, f: (0, f)),
                  pl.BlockSpec((P, tf), lambda i, f: (0, f)),
                  pl.BlockSpec((1, D_MODEL), lambda i, f: (0, 0))],
        out_specs=[pl.BlockSpec((tm, D_MODEL), lambda i, f: (i, 0)),
                   pl.BlockSpec((1, P, tf), lambda i, f: (i, 0, f))],
        out_shape=[jax.ShapeDtypeStruct((t, D_MODEL), F32),
                   jax.ShapeDtypeStruct((t // tm, P, D_FF), F32)],
        scratch_shapes=[pltpu.VMEM((tm, D_MODEL), BF16), pltpu.VMEM((nf, tm + 2 * P, tf), F32)],
        compiler_params=_cparams(("arbitrary", "arbitrary")),
        name="conv_ffn",
    )(x, g, wup, wup, wd, cw, cb, pre, gf)


def _pack_rwkv_state(s):
    n = s.shape[0]
    return s.reshape(n, N_PAIRS, 2, R_HD, R_HD).transpose(0, 1, 3, 2, 4).reshape(n, N_PAIRS, R_HD, LANES)


def _unpack_rwkv_state(s):
    n = s.shape[0]
    return s.reshape(n, N_PAIRS, R_HD, 2, R_HD).transpose(0, 1, 3, 2, 4).reshape(n, R_HEADS, R_HD, R_HD)


def _unpack_rwkv_blockdiag(s):
    n = s.shape[0]
    return jnp.stack([s[:, :, :R_HD, :R_HD], s[:, :, R_HD:, R_HD:]], axis=2).reshape(n, R_HEADS, R_HD, R_HD)


def _prep_layer_params(l, p):
    w_in = p['w_in'][l]
    c_dt_src = 3 * A_W + S_W + S_CONV_CH
    c_rw_src = c_dt_src + S_HEADS
    c_gate_src = c_rw_src + R_IN_W
    w_proj = jnp.concatenate([
        w_in[:, c_rw_src:c_gate_src],
        w_in[:, c_dt_src:c_rw_src], jnp.zeros((D_MODEL, C_Q - C_DT - S_HEADS), F32),
        w_in[:, :c_dt_src],
        w_in[:, c_gate_src:]], axis=1).astype(BF16)
    pad_lane = lambda v: jnp.pad(v, (0, LANES - v.shape[0])).reshape(1, LANES)
    zl = lambda r0, w: jnp.zeros((R_LORA, R_W), F32).at[r0:r0 + w.shape[0]].set(w).astype(BF16)
    return dict(
        norm1_g=p['norm1_g'][l].reshape(1, D_MODEL), w_proj=w_proj,
        b_gate=p['b_gate'][l].reshape(1, 3 * D_MODEL),
        w_pa=p['w_pa'][l].astype(BF16), w_pb=p['w_pb'][l].astype(BF16), w_pc=p['w_pc'][l].astype(BF16),
        w_o=p['w_o'][l].astype(BF16),
        ssm_conv_w=p['ssm_conv_w'][l], ssm_conv_b=p['ssm_conv_b'][l].reshape(1, S_CONV_CH),
        ssm_dt_bias=pad_lane(p['ssm_dt_bias'][l]), ssm_a_log=pad_lane(p['ssm_a_log'][l]),
        ssm_dvec=jnp.repeat(p['ssm_d'][l], S_HD).reshape(1, S_W),
        ssm_dtb_vec=jnp.repeat(p['ssm_dt_bias'][l], S_HD).reshape(1, S_W),
        ssm_alog_vec=jnp.repeat(p['ssm_a_log'][l], S_HD).reshape(1, S_W),
        ssm_norm_g=p['ssm_norm_g'][l].reshape(1, S_W),
        rw_mu=p['rw_mu'][l].reshape(1, R_IN_W), rw_w0=p['rw_w0'][l].reshape(1, R_W),
        rw_w2p=zl(0, p['rw_w2'][l]), rw_a0=p['rw_a0'][l].reshape(1, R_W),
        rw_a2p=zl(R_LORA_W, p['rw_a2'][l]), rw_g2p=zl(R_LORA_W + R_LORA_A, p['rw_g2'][l]),
        rw_kk=p['rw_kk'][l].reshape(1, R_W), rw_ka=p['rw_ka'][l].reshape(1, R_W),
        rw_rk=p['rw_rk'][l].reshape(1, R_W), rw_ln_g=p['rw_ln_g'][l].reshape(1, R_W),
        rw_ln_b=p['rw_ln_b'][l].reshape(1, R_W),
        norm2_g=p['norm2_g'][l].reshape(1, D_MODEL), w_up=p['w_up'][l].astype(BF16),
        w_down=p['w_down'][l].astype(BF16), ffn_conv_w=p['ffn_conv_w'][l],
        ffn_conv_b=p['ffn_conv_b'][l].reshape(1, D_FF))


def _row_tile(t):
    for tm in (512, 256, 128):
        if t % tm == 0:
            return tm
    raise ValueError(t)


def _ffn_tf():
    return D_FF // 2


def _prompt_layer(x, lp, bsz, seq, slopes_pair, gf, final_norm):
    t = bsz * seq
    tm = _row_tile(seq)
    tiles = seq // tm
    proj = _rms_matmul(x, lp['norm1_g'], lp['w_proj'], 1024 if t % 1024 == 0 else tm, 2048)
    proj3 = proj.reshape(bsz, seq, N_PROJ)
    oa, k_t, v_t = _moba_prompt(proj3, slopes_pair)
    oa = oa.reshape(t, A_W)
    ob, ssm_new = _ssd_prompt(proj, bsz, seq, lp['ssm_conv_w'], lp['ssm_conv_b'], lp['ssm_dt_bias'],
                              lp['ssm_a_log'], lp['ssm_dvec'], lp['ssm_norm_g'])
    r, w, k2, v, kk, kka, g = _rwkv_prep(
        proj, jnp.zeros((SUBLANES, R_IN_W), F32), lp['rw_mu'], lp['rw_w0'], lp['rw_w2p'], lp['rw_a0'],
        lp['rw_a2p'], lp['rw_g2p'], lp['rw_kk'], lp['rw_ka'], tm, 1, tiles)
    yr, rw_new = _rwkv_scan_prompt((r, w, k2, v, kk, kka), bsz, seq, LANES)
    x = _merge(x, proj, oa, ob, yr, r, k2, v, g, lp['b_gate'], lp['rw_ln_g'], lp['rw_ln_b'], lp['rw_rk'],
               lp['w_pa'], lp['w_pb'], lp['w_pc'], lp['w_o'], min(tm, 256))
    x, tail = _ffn(x, lp['norm2_g'], lp['w_up'], lp['w_down'], lp['ffn_conv_w'], lp['ffn_conv_b'],
                   jnp.zeros((SUBLANES, D_FF), F32), gf, tm, _ffn_tf(), 1, tiles, final_norm)
    k_new = k_t.reshape(bsz, A_HEADS, A_HD, seq).transpose(0, 3, 1, 2)
    v_new = v_t.reshape(bsz, A_HEADS, A_HD, seq).transpose(0, 3, 1, 2)
    ssm_conv_new = proj3[:, seq - (S_CONV - 1):, C_XBC:C_XBC + S_CONV_CH]
    shift_new = proj3[:, seq - 1:, C_RW:C_RW + R_IN_W]
    ffn_conv_new = tail.reshape(bsz, tiles, SUBLANES, D_FF)[:, tiles - 1, SUBLANES - (F_CONV - 1):]
    state = (k_new, v_new, ssm_new.reshape(bsz, S_HEADS, S_HD, S_STATE), ssm_conv_new,
             _unpack_rwkv_blockdiag(rw_new), shift_new, ffn_conv_new)
    return x, state


def _sample_layer(x, lp, dbsz, tnew, st, cache_k4, cache_v4, page_table, layer, gf, final_norm):
    t = tnew * dbsz
    ssm0, ssm_conv0, rwkv0, shift0, ffn_conv0 = st
    proj = _rms_matmul(x, lp['norm1_g'], lp['w_proj'], t, 1024)
    proj3 = proj.reshape(tnew, dbsz, N_PROJ)

    qkv = proj3[:, :, C_Q:C_Q + 3 * A_W].transpose(1, 0, 2)
    qkv8 = jnp.pad(qkv, ((0, 0), (0, SUBLANES - tnew), (0, 0)))
    oa8 = _moba_sample(qkv8[:, :, :A_W], qkv8[:, :, A_W:2 * A_W], qkv8[:, :, 2 * A_W:],
                       cache_k4, cache_v4, page_table, layer, tnew)
    oa = oa8[:, :tnew].transpose(1, 0, 2).reshape(t, A_W)

    h0 = ssm0.reshape(-1, dbsz, N_PAIRS, 2 * S_HD, S_STATE)
    ob3, ssm_new = _ssd_sample(proj3, ssm_conv0.transpose(1, 0, 2), h0, layer, lp['ssm_conv_w'],
                               lp['ssm_conv_b'], lp['ssm_dtb_vec'], lp['ssm_alog_vec'], lp['ssm_dvec'],
                               lp['ssm_norm_g'])
    ob = ob3.reshape(t, S_W)
    ssm_new = ssm_new.reshape(dbsz, S_HEADS, S_HD, S_STATE)

    r, w, k2, v, kk, kka, g = _rwkv_prep(
        proj, shift0.reshape(dbsz, R_IN_W), lp['rw_mu'], lp['rw_w0'], lp['rw_w2p'], lp['rw_a0'],
        lp['rw_a2p'], lp['rw_g2p'], lp['rw_kk'], lp['rw_ka'], t, dbsz, 1)
    to3 = lambda a: a.reshape(tnew, dbsz, R_W)
    yr3, rw_new = _rwkv_scan_sample(tuple(to3(a) for a in (r, w, k2, v, kk, kka)), _pack_rwkv_state(rwkv0))
    yr = yr3.reshape(t, R_W)

    x = _merge(x, proj, oa, ob, yr, r, k2, v, g, lp['b_gate'], lp['rw_ln_g'], lp['rw_ln_b'], lp['rw_rk'],
               lp['w_pa'], lp['w_pb'], lp['w_pc'], lp['w_o'], min(t, 256))
    pre = ffn_conv0.transpose(1, 0, 2).reshape((F_CONV - 1) * dbsz, D_FF)
    x, tail = _ffn(x, lp['norm2_g'], lp['w_up'], lp['w_down'], lp['ffn_conv_w'], lp['ffn_conv_b'],
                   pre, gf, t, _ffn_tf(), dbsz, 1, final_norm)

    k_new = qkv[:, :, A_W:2 * A_W].reshape(dbsz, tnew, A_HEADS, A_HD)
    v_new = qkv[:, :, 2 * A_W:].reshape(dbsz, tnew, A_HEADS, A_HD)
    ssm_conv_new = proj3[tnew - (S_CONV - 1):, :, C_XBC:C_XBC + S_CONV_CH].transpose(1, 0, 2)
    shift_new = proj3[tnew - 1:, :, C_RW:C_RW + R_IN_W].transpose(1, 0, 2)
    ffn_conv_new = tail.reshape(F_CONV - 1, dbsz, D_FF).transpose(1, 0, 2)
    state = (k_new, v_new, ssm_new, ssm_conv_new, _unpack_rwkv_state(rw_new), shift_new, ffn_conv_new)
    return x, state


def kernel(x_prompt, x_sample, cache_k, cache_v, state_ssm, state_ssm_conv, state_rwkv, state_rwkv_shift, state_ffn_conv, page_table, norm1_g, w_in, b_gate, w_pa, ssm_conv_w, ssm_conv_b, ssm_dt_bias, ssm_a_log, ssm_d, ssm_norm_g, w_pb, rw_mu, rw_w0, rw_w2, rw_a0, rw_a2, rw_g2, rw_kk, rw_ka, rw_rk, rw_ln_g, rw_ln_b, w_pc, w_o, norm2_g, w_up, ffn_conv_w, ffn_conv_b, w_down, norm_f_g):
    params = dict(norm1_g=norm1_g, w_in=w_in, b_gate=b_gate, w_pa=w_pa, ssm_conv_w=ssm_conv_w,
                  ssm_conv_b=ssm_conv_b, ssm_dt_bias=ssm_dt_bias, ssm_a_log=ssm_a_log, ssm_d=ssm_d,
                  ssm_norm_g=ssm_norm_g, w_pb=w_pb, rw_mu=rw_mu, rw_w0=rw_w0, rw_w2=rw_w2, rw_a0=rw_a0,
                  rw_a2=rw_a2, rw_g2=rw_g2, rw_kk=rw_kk, rw_ka=rw_ka, rw_rk=rw_rk, rw_ln_g=rw_ln_g,
                  rw_ln_b=rw_ln_b, w_pc=w_pc, w_o=w_o, norm2_g=norm2_g, w_up=w_up, ffn_conv_w=ffn_conv_w,
                  ffn_conv_b=ffn_conv_b, w_down=w_down)
    depth = w_in.shape[0]
    bsz, seq, _ = x_prompt.shape
    dbsz, tnew, _ = x_sample.shape
    head = jnp.arange(A_HEADS, dtype=F32) + 1.0
    slopes = jnp.exp2(-8.0 * head / A_HEADS)
    slopes_pair = jnp.repeat(slopes, A_HD).reshape(N_PAIRS, 1, LANES)
    cache_k4 = cache_k.transpose(0, 1, 3, 4, 2)
    cache_v4 = cache_v.transpose(0, 1, 3, 4, 2)
    gf = norm_f_g.reshape(1, D_MODEL)

    hp = x_prompt.reshape(bsz * seq, D_MODEL)
    hs = x_sample.transpose(1, 0, 2).reshape(tnew * dbsz, D_MODEL)
    new_p = [[] for _ in range(7)]
    new_s = [[] for _ in range(7)]
    for l in range(depth):
        lp = _prep_layer_params(l, params)
        last = l == depth - 1
        hp, sp = _prompt_layer(hp, lp, bsz, seq, slopes_pair, gf, last)
        st = (state_ssm, state_ssm_conv[l], state_rwkv[l], state_rwkv_shift[l], state_ffn_conv[l])
        hs, ss = _sample_layer(hs, lp, dbsz, tnew, st, cache_k4, cache_v4, page_table, l, gf, last)
        for j in range(7):
            new_p[j].append(sp[j])
            new_s[j].append(ss[j])
    y_prompt = hp.reshape(bsz, seq, D_MODEL)
    y_sample = hs.reshape(tnew, dbsz, D_MODEL).transpose(1, 0, 2)
    outs = [y_prompt, y_sample]
    for j in range(7):
        outs += [jnp.stack(new_p[j]), jnp.stack(new_s[j])]
    return tuple(outs)
```

```python
import functools
import math

import jax
import jax.numpy as jnp
from jax import lax
from jax.experimental import pallas as pl
from jax.experimental.pallas import tpu as pltpu

F32 = jnp.float32
BF16 = jnp.bfloat16

D_MODEL = 1024
A_HEADS = 8
A_HD = 64
A_W = A_HEADS * A_HD
MOBA_BLOCK = 256
MOBA_TOPK = 3
Q_BLOCK = MOBA_BLOCK
ATTN_SCALE = A_HD ** -0.5
S_HEADS = 8
S_HD = 64
S_W = S_HEADS * S_HD
S_GROUPS = 2
S_STATE = 128
S_CONV = 4
S_CONV_CH = S_W + 2 * S_GROUPS * S_STATE
SSD_CHUNK = 128
R_HEADS = 8
R_HD = 64
R_W = R_HEADS * R_HD
R_LORA_W = 64
R_LORA_A = 64
R_LORA_G = 128
R_LORA = R_LORA_W + R_LORA_A + R_LORA_G
R_IN_W = 3 * R_W + R_LORA
D_FF = ((8 * D_MODEL // 3 + 127) // 128) * 128
F_CONV = 3
NORM_EPS = 1e-6
GN_EPS = 64e-5
NEG_INF = -1e30

LANES = 128
SUBLANES = 8
HALF = 64
HALF_SHIFT = 6
SUBLANE_SHIFT = 3
N_PAIRS = 4

C_RW = 0
C_DT = R_IN_W
C_Q = 2048
C_K = C_Q + A_W
C_V = C_K + A_W
C_Z = C_V + A_W
C_XBC = C_Z + S_W
C_GATE = C_XBC + S_CONV_CH
N_PROJ = C_GATE + 3 * D_MODEL

VMEM_LIMIT = 56 * 1024 * 1024


def _cparams(sem):
    return pltpu.CompilerParams(dimension_semantics=sem, vmem_limit_bytes=VMEM_LIMIT)


def _dot(a, b):
    return jnp.dot(a.astype(BF16), b.astype(BF16), preferred_element_type=F32)


def _dot_nt(a, b):
    return lax.dot_general(a.astype(BF16), b.astype(BF16), (((1,), (1,)), ((), ())),
                           preferred_element_type=F32)


def _dot_tn(a, b):
    return lax.dot_general(a.astype(BF16), b.astype(BF16), (((0,), (0,)), ((), ())),
                           preferred_element_type=F32)


def _split_dot(x, w, passes, x_is_lhs=True):
    w = w.astype(BF16)
    acc = None
    rem = x
    for _ in range(passes):
        piece = rem.astype(BF16)
        term = (jnp.dot(piece, w, preferred_element_type=F32) if x_is_lhs
                else jnp.dot(w, piece, preferred_element_type=F32))
        acc = term if acc is None else acc + term
        rem = rem - piece.astype(F32)
    return acc


def _dot_nt_hi(a, b):
    return lax.dot_general(a, b, (((1,), (1,)), ((), ())), preferred_element_type=F32,
                           precision=lax.Precision.HIGHEST)


def _sigmoid(x):
    return 1.0 / (1.0 + jnp.exp(-x))


def _silu(x):
    return x * _sigmoid(x)


def _softplus(x):
    return jnp.maximum(x, 0.0) + jnp.log(1.0 + jnp.exp(-jnp.abs(x)))


def _head_ones(width, scale):
    r = lax.shift_right_logical(lax.broadcasted_iota(jnp.int32, (width, width), 0), HALF_SHIFT)
    c = lax.shift_right_logical(lax.broadcasted_iota(jnp.int32, (width, width), 1), HALF_SHIFT)
    return jnp.where(r == c, scale, 0.0).astype(F32)


def _rms_matmul_kernel(x_ref, g_ref, w_ref, o_ref, xn_ref):
    @pl.when(pl.program_id(1) == 0)
    def _():
        x = x_ref[...]
        ms = jnp.mean(x * x, axis=-1, keepdims=True)
        xn_ref[...] = (x * lax.rsqrt(ms + NORM_EPS) * g_ref[...]).astype(BF16)

    o_ref[...] = jnp.dot(xn_ref[...], w_ref[...], preferred_element_type=F32)


def _rms_matmul(x, g, w, tm, tn):
    t, d = x.shape
    n = w.shape[1]
    return pl.pallas_call(
        _rms_matmul_kernel,
        grid=(t // tm, n // tn),
        in_specs=[pl.BlockSpec((tm, d), lambda i, j: (i, 0)),
                  pl.BlockSpec((1, d), lambda i, j: (0, 0)),
                  pl.BlockSpec((d, tn), lambda i, j: (0, j))],
        out_specs=pl.BlockSpec((tm, tn), lambda i, j: (i, j)),
        out_shape=jax.ShapeDtypeStruct((t, n), F32),
        scratch_shapes=[pltpu.VMEM((tm, d), BF16)],
        compiler_params=_cparams(("parallel", "arbitrary")),
        name="in_proj",
    )(x, g, w)


def _topk_bias(gate, n_valid, axis=1):
    pos = lax.broadcasted_iota(jnp.int32, gate.shape, axis)
    pos_f = pos.astype(F32)
    gm = jnp.where(pos < n_valid, gate, NEG_INF)
    selected = jnp.zeros(gate.shape, jnp.bool_)
    for j in range(MOBA_TOPK):
        m = jnp.max(gm, axis=axis, keepdims=True)
        idx = jnp.min(jnp.where(gm == m, pos_f, 1e9), axis=axis, keepdims=True)
        hit = pos_f == idx
        selected = jnp.logical_or(selected, jnp.logical_and(hit, j < n_valid))
        gm = jnp.where(hit, -jnp.inf, gm)
    return jnp.where(selected, 0.0, NEG_INF).astype(F32)


def _bf16_pieces(x, n):
    out = []
    rem = x
    for _ in range(n):
        piece = rem.astype(BF16).astype(F32)
        out.append(piece)
        rem = rem - piece
    return out


def _moba_prompt_kernel(*refs, nb, nq):
    def tile(i, carry):
        _moba_prompt_tile(i, *refs, nb=nb)
        return carry

    lax.fori_loop(0, nq, tile, 0)


def _moba_prompt_tile(i, q_ref, k_ref, v_ref, sl_ref, o_ref, kt_out, vt_out, kf_ref, vth_ref, kmean_ref, sel_ref,
                      acc_ref, raw_a, raw_b, m_ref, knorm_ref, *, nb):
    B = MOBA_BLOCK
    n_piece = 3
    tile_rows = pl.ds(pl.multiple_of(i * Q_BLOCK, Q_BLOCK), Q_BLOCK)
    lane = lax.broadcasted_iota(jnp.int32, (1, LANES), 1)
    halves = (lane < HALF, lane >= HALF)

    @pl.when(i == 0)
    def _():
        kmean_ref[...] = jnp.zeros_like(kmean_ref)
        key_local = jnp.bitwise_and(lax.broadcasted_iota(jnp.int32, (k_ref.shape[1], 1), 0), B - 1).astype(F32)
        extra = jnp.where(lane < n_piece, key_local, jnp.where(lane < 2 * n_piece, 1.0, 0.0))
        kf_ref[...] = jnp.concatenate([k_ref[0], extra], axis=1).astype(BF16)
        ones_rows = jnp.ones((SUBLANES, B), F32)
        knorm = [jnp.zeros((1, LANES), F32), jnp.zeros((1, LANES), F32)]
        for n in range(nb):
            kblk = k_ref[0, n * B:(n + 1) * B, :]
            kmean_ref[n:n + 1, :] = jnp.mean(kblk, axis=0, keepdims=True)
            vt = v_ref[0, n * B:(n + 1) * B, :].T
            kt_out[0, :, n * B:(n + 1) * B] = kblk.T
            vt_out[0, :, n * B:(n + 1) * B] = vt
            for h in range(2):
                vth_ref[h, n] = jnp.concatenate([vt[h * HALF:(h + 1) * HALF], ones_rows], axis=0).astype(BF16)
                norm2 = jnp.sum(jnp.where(halves[h], kblk * kblk, 0.0), axis=-1, keepdims=True)
                knorm[h] = jnp.where(lane == n, jnp.max(norm2, axis=0, keepdims=True), knorm[h])
        for h in range(2):
            knorm_ref[h] = jnp.broadcast_to(knorm[h], (SUBLANES, LANES))

    own = i
    q = q_ref[0, tile_rows, :]
    log2e = 1.0 / math.log(2.0)
    slope2 = (sl_ref[0, :, 0:1] * log2e, sl_ref[0, :, HALF:HALF + 1] * log2e)
    kmean = kmean_ref[...]
    q_local = lax.broadcasted_iota(jnp.int32, (Q_BLOCK, 1), 0).astype(F32)
    q_rows = []
    for h in range(2):
        q_m = jnp.where(halves[h], q, 0.0)
        bias_t = _topk_bias(_dot_nt_hi(kmean, q_m), own, axis=0)
        for n in range(nb):
            sel_ref[h, n] = jnp.broadcast_to(bias_t[n:n + 1, :], (SUBLANES, Q_BLOCK))
        pieces = _bf16_pieces(slope2[h], n_piece) + _bf16_pieces(-slope2[h] * q_local, n_piece)
        extra = jnp.zeros((Q_BLOCK, LANES), F32)
        for j, piece in enumerate(pieces):
            extra = jnp.where(lane == j, piece, extra)
        q_rows.append(jnp.concatenate([q_m * (ATTN_SCALE * log2e), extra], axis=1))
    qf = jnp.concatenate(q_rows, axis=0).astype(BF16)

    rel = (lax.broadcasted_iota(jnp.int32, (B, Q_BLOCK), 0) - lax.broadcasted_iota(jnp.int32, (B, Q_BLOCK), 1))

    def scores(n):
        both = _dot_nt(kf_ref[pl.ds(pl.multiple_of(n * B, B), B), :], qf)
        return [both[:, h * Q_BLOCK:(h + 1) * Q_BLOCK] for h in range(2)]

    def put_scores(n, dst_ref):
        for h, s in enumerate(scores(n)):
            dst_ref[h] = s

    def row_of(ref, h):
        return ref[h][0:1, :]

    def put_row(ref, h, x):
        ref[h] = jnp.broadcast_to(x, (SUBLANES, Q_BLOCK))

    def attend(n, src_ref):
        shift = ((n - i) * B).astype(F32)
        ps, alphas = [], []
        for h in range(2):
            s = src_ref[h] + (sel_ref[h, n][0:1, :] + slope2[h] * shift)
            m0 = row_of(m_ref, h)
            m1 = jnp.maximum(m0, jnp.max(s, axis=0, keepdims=True))
            alphas.append(jnp.exp2(m0 - m1))
            ps.append(jnp.exp2(s - m1).astype(BF16))
            put_row(m_ref, h, m1)
        pv = [jnp.dot(vth_ref[h, n], ps[h], preferred_element_type=F32) for h in range(2)]
        for h in range(2):
            acc_ref[h] = alphas[h] * acc_ref[h] + pv[h]

    own_raw = scores(own)
    lane_f = lane.astype(F32)
    bound_c = ATTN_SCALE * log2e * 1.02
    skips = []
    for h in range(2):
        q_m = jnp.where(halves[h], q, 0.0)
        qn2 = jnp.max(jnp.sum(q_m * q_m, axis=-1, keepdims=True), axis=0, keepdims=True)
        ub = jnp.sqrt(qn2 * knorm_ref[h][0:1, :]) * bound_c
        ub_own = jnp.sum(jnp.where(lane == own, ub, 0.0), axis=-1, keepdims=True)
        far = slope2[h] * ((lane_f - own.astype(F32)) * B + (B - 1))
        skips.append(ub + far < -ub_own - 160.0)
    keep = jnp.logical_and(lane < own, jnp.logical_not(jnp.logical_and(skips[0], skips[1])))
    first = jnp.min(jnp.where(keep, lane_f, own.astype(F32))).astype(jnp.int32)
    n_visit = own - first

    put_scores(jnp.minimum(first, nb - 1), raw_a)

    own_p = []
    for h in range(2):
        s = jnp.where(rel <= 0, own_raw[h], NEG_INF)
        m = jnp.max(s, axis=0, keepdims=True)
        put_row(m_ref, h, m)
        own_p.append(jnp.exp2(s - m).astype(BF16))
    for h in range(2):
        acc_ref[h] = jnp.dot(vth_ref[h, own], own_p[h], preferred_element_type=F32)

    def body(j, carry):
        n0 = first + 2 * j
        put_scores(n0 + 1, raw_b)
        attend(n0, raw_a)
        put_scores(jnp.minimum(n0 + 2, nb - 1), raw_a)
        attend(n0 + 1, raw_b)
        return carry

    lax.fori_loop(0, n_visit // 2, body, 0)

    @pl.when(n_visit % 2 == 1)
    def _():
        attend(own - 1, raw_a)

    out_t = jnp.concatenate([acc_ref[h][0:HALF] / acc_ref[h][HALF:HALF + 1] for h in range(2)], axis=0)
    o_ref[0, tile_rows, :] = out_t.T


def _moba_prompt(proj3, slopes_pair):
    bsz, seq, _ = proj3.shape
    assert seq % MOBA_BLOCK == 0 and seq // MOBA_BLOCK >= MOBA_TOPK
    nb = seq // MOBA_BLOCK
    nbp = -(-nb // SUBLANES) * SUBLANES
    nq = seq // Q_BLOCK
    qc, kc, vc = C_Q // LANES, C_K // LANES, C_V // LANES
    return pl.pallas_call(
        functools.partial(_moba_prompt_kernel, nb=nb, nq=nq),
        grid=(bsz, N_PAIRS),
        in_specs=[pl.BlockSpec((1, seq, LANES), lambda b, p: (b, 0, qc + p)),
                  pl.BlockSpec((1, seq, LANES), lambda b, p: (b, 0, kc + p)),
                  pl.BlockSpec((1, seq, LANES), lambda b, p: (b, 0, vc + p)),
                  pl.BlockSpec((1, 1, LANES), lambda b, p: (p, 0, 0))],
        out_specs=[pl.BlockSpec((1, seq, LANES), lambda b, p: (b, 0, p)),
                   pl.BlockSpec((1, LANES, seq), lambda b, p: (b, p, 0)),
                   pl.BlockSpec((1, LANES, seq), lambda b, p: (b, p, 0))],
        out_shape=[jax.ShapeDtypeStruct((bsz, seq, A_W), F32),
                   jax.ShapeDtypeStruct((bsz, A_W, seq), F32), jax.ShapeDtypeStruct((bsz, A_W, seq), F32)],
        scratch_shapes=[pltpu.VMEM((seq, 2 * LANES), BF16),
                        pltpu.VMEM((2, nb, HALF + SUBLANES, MOBA_BLOCK), BF16),
                        pltpu.VMEM((nbp, LANES), F32), pltpu.VMEM((2, nb, SUBLANES, Q_BLOCK), F32),
                        pltpu.VMEM((2, HALF + SUBLANES, Q_BLOCK), F32),
                        pltpu.VMEM((2, MOBA_BLOCK, Q_BLOCK), F32), pltpu.VMEM((2, MOBA_BLOCK, Q_BLOCK), F32),
                        pltpu.VMEM((2, SUBLANES, Q_BLOCK), F32), pltpu.VMEM((2, SUBLANES, LANES), F32)],
        compiler_params=_cparams(("parallel", "parallel")),
        name="moba_prompt",
    )(proj3, proj3, proj3, slopes_pair)


def _moba_sample_kernel(pt_ref, q_ref, kn_ref, vn_ref, *refs, n_blk, ppb, page, tnew):
    del pt_ref
    n_pages = n_blk * ppb
    k_refs, v_refs = refs[:n_pages], refs[n_pages:2 * n_pages]
    o_ref, m_ref, l_ref, g_ref, acc_ref = refs[2 * n_pages:]
    past = n_blk * MOBA_BLOCK
    rows = A_HEADS * SUBLANES
    lane_head = lax.shift_right_logical(lax.broadcasted_iota(jnp.int32, (SUBLANES, A_W), 1), HALF_SHIFT)
    q8 = q_ref[0] * ATTN_SCALE
    qbd = jnp.concatenate([jnp.where(lane_head == h, q8, 0.0) for h in range(A_HEADS)], axis=0)
    q_hi = qbd.astype(BF16)
    q_lo = (qbd - q_hi.astype(F32)).astype(BF16)
    r = lax.broadcasted_iota(jnp.int32, (rows, 1), 0)
    tok = jnp.bitwise_and(r, SUBLANES - 1).astype(F32)
    slope = jnp.exp2(-(8.0 / A_HEADS) * (lax.shift_right_logical(r, SUBLANE_SHIFT) + 1).astype(F32))
    key = lax.broadcasted_iota(jnp.int32, (1, MOBA_BLOCK), 1).astype(F32)

    def block_t(refs_, n):
        return jnp.concatenate([refs_[j][0, 0].reshape(A_W, page) for j in range(n * ppb, (n + 1) * ppb)],
                               axis=1).astype(BF16)

    raws, fixes = [], []
    for n in range(n_blk):
        kt = block_t(k_refs, n)
        raws.append(jnp.dot(q_hi, kt, preferred_element_type=F32))
        fixes.append(jnp.dot(q_lo, kt, preferred_element_type=F32))
    es = []
    for n in range(n_blk):
        g_ref[n] = jnp.broadcast_to(jnp.sum(raws[n] + fixes[n], axis=-1, keepdims=True) * (1.0 / MOBA_BLOCK),
                                    (rows, LANES))
        s = raws[n] - slope * ((past - n * MOBA_BLOCK + tok) - key)
        m = jnp.max(s, axis=-1, keepdims=True)
        e = jnp.exp(s - m)
        m_ref[n] = jnp.broadcast_to(m, (rows, LANES))
        l_ref[n] = jnp.broadcast_to(jnp.sum(e, axis=-1, keepdims=True), (rows, LANES))
        es.append(e.astype(BF16))
    for n in range(n_blk):
        acc_ref[n] = lax.dot_general(es[n], block_t(v_refs, n), (((1,), (1,)), ((), ())),
                                     preferred_element_type=F32)

    lane = lax.broadcasted_iota(jnp.int32, (rows, LANES), 1)
    gate = jnp.zeros((rows, LANES), F32)
    for j in range(n_blk):
        gate = jnp.where(lane == j, g_ref[j], gate)
    bias = _topk_bias(gate, n_blk)

    kn = kn_ref[0]
    vn = vn_ref[0]
    s_own = []
    for j in range(tnew):
        sj = jnp.sum(qbd * kn[j:j + 1, :], axis=-1, keepdims=True) - slope * (tok - j)
        s_own.append(jnp.where(tok >= j, sj, NEG_INF))
    mx = s_own[0]
    for j in range(1, tnew):
        mx = jnp.maximum(mx, s_own[j])
    mb = []
    for j in range(n_blk):
        mj = m_ref[j][:, 0:1] + bias[:, j:j + 1]
        mb.append(mj)
        mx = jnp.maximum(mx, mj)
    lsum = jnp.zeros((rows, 1), F32)
    acc = jnp.zeros((rows, A_W), F32)
    for j in range(tnew):
        w = jnp.exp(s_own[j] - mx)
        lsum = lsum + w
        acc = acc + w * vn[j:j + 1, :]
    for j in range(n_blk):
        w = jnp.exp(mb[j] - mx)
        lsum = lsum + w * l_ref[j][:, 0:1]
        acc = acc + w * acc_ref[j]
    out = acc / lsum
    o8 = jnp.zeros((SUBLANES, A_W), F32)
    for h in range(A_HEADS):
        o8 = o8 + jnp.where(lane_head == h, out[h * SUBLANES:(h + 1) * SUBLANES, :], 0.0)
    o_ref[0] = o8


def _moba_sample(q8, k8, v8, cache_kt, cache_vt, page_table, layer, tnew):
    dbsz = q8.shape[0]
    n_pages = page_table.shape[1]
    page = cache_kt.shape[4]
    assert MOBA_BLOCK % page == 0 and (n_pages * page) % MOBA_BLOCK == 0 and page % LANES == 0
    ppb = MOBA_BLOCK // page
    n_blk = n_pages // ppb
    assert n_blk >= MOBA_TOPK and tnew <= SUBLANES
    rows = A_HEADS * SUBLANES
    tok_spec = pl.BlockSpec((1, SUBLANES, A_W), lambda b, pt: (b, 0, 0))

    def page_spec(j):
        return pl.BlockSpec((1, 1, A_HEADS, A_HD, page), lambda b, pt: (layer, pt[b * n_pages + j], 0, 0, 0))

    pages = [page_spec(j) for j in range(n_pages)]
    grid_spec = pltpu.PrefetchScalarGridSpec(
        num_scalar_prefetch=1,
        grid=(dbsz,),
        in_specs=[tok_spec, tok_spec, tok_spec] + pages + pages,
        out_specs=tok_spec,
        scratch_shapes=[pltpu.VMEM((n_blk, rows, LANES), F32), pltpu.VMEM((n_blk, rows, LANES), F32),
                        pltpu.VMEM((n_blk, rows, LANES), F32), pltpu.VMEM((n_blk, rows, A_W), F32)],
    )
    return pl.pallas_call(
        functools.partial(_moba_sample_kernel, n_blk=n_blk, ppb=ppb, page=page, tnew=tnew),
        grid_spec=grid_spec,
        out_shape=jax.ShapeDtypeStruct((dbsz, SUBLANES, A_W), F32),
        compiler_params=_cparams(("parallel",)),
        name="moba_sample",
    )(page_table.reshape(-1), q8, k8, v8, *([cache_kt] * n_pages), *([cache_vt] * n_pages))


def _ssd_prompt_kernel(*refs, L, n_sub):
    hout_ref, h_ref = refs[10], refs[12]

    def chunk(j, carry):
        _ssd_chunk(pl.program_id(1) * n_sub + j, pl.ds(pl.multiple_of(j * L, L), L), *refs[:10], *refs[11:], L=L)
        return carry

    lax.fori_loop(0, n_sub, chunk, 0)

    @pl.when(pl.program_id(1) == pl.num_programs(1) - 1)
    def _():
        hout_ref[0] = h_ref[...]


def _ssd_chunk(c, rows, z_ref, xbc_ref, dt_ref, cw_ref, cb_ref, dtb_ref, alog_ref, dvec_ref, ng_ref,
               o_ref, ext_ref, h_ref, *, L):
    P = SUBLANES

    @pl.when(c == 0)
    def _():
        ext_ref[0:P, :] = jnp.zeros((P, S_CONV_CH), F32)
        h_ref[...] = jnp.zeros_like(h_ref)

    @pl.when(c > 0)
    def _():
        ext_ref[0:P, :] = ext_ref[L:L + P, :]

    ext_ref[P:P + L, :] = xbc_ref[rows, :]
    acc = cb_ref[...] + cw_ref[S_CONV - 1:S_CONV, :] * ext_ref[P:P + L, :]
    for d in range(1, S_CONV):
        acc = acc + cw_ref[S_CONV - 1 - d:S_CONV - d, :] * ext_ref[P - d:P - d + L, :]
    xbc = _silu(acc)
    xs = xbc[:, :S_W]
    bm = xbc[:, S_W:S_W + S_GROUPS * S_STATE]
    cm = xbc[:, S_W + S_GROUPS * S_STATE:]

    lane = lax.broadcasted_iota(jnp.int32, (1, LANES), 1)
    is_a = lane < HALF
    dt = jnp.where(lane < S_HEADS, _softplus(dt_ref[rows, :] + dtb_ref[...]), 0.0)
    da = dt * (-jnp.exp(alog_ref[...]))
    rr = lax.broadcasted_iota(jnp.int32, (L, L), 0)
    cc = lax.broadcasted_iota(jnp.int32, (L, L), 1)
    causal = rr >= cc
    cum = _split_dot(da, jnp.where(causal, 1.0, 0.0), 3, x_is_lhs=False)
    cum_t = cum.T
    dt_t = dt.T
    row_a = lax.broadcasted_iota(jnp.int32, (2 * S_HD, 1), 0) < S_HD

    group_of = [(2 * pr * S_GROUPS) // S_HEADS for pr in range(N_PAIRS)]
    bgs = [bm[:, g * S_STATE:(g + 1) * S_STATE] for g in range(S_GROUPS)]
    cgs = [cm[:, g * S_STATE:(g + 1) * S_STATE] for g in range(S_GROUPS)]
    cb_g = [_dot_nt(cgs[g], bgs[g]) for g in range(S_GROUPS)]
    xs_ps = [xs[:, pr * LANES:(pr + 1) * LANES] for pr in range(N_PAIRS)]
    hps = [h_ref[pr] for pr in range(N_PAIRS)]
    halves = (is_a, jnp.logical_not(is_a))
    cum_c = [cum[:, h:h + 1] for h in range(S_HEADS)]
    last = [cum[L - 1:L, h:h + 1] for h in range(S_HEADS)]
    wts = []
    for h in range(S_HEADS):
        seg = cum_c[h] - cum_t[h:h + 1, :]
        dec = jnp.exp(jnp.where(causal, seg, -jnp.inf))
        wts.append(cb_g[group_of[h // 2]] * dec * dt_t[h:h + 1, :])
    intra = [_dot(wts[h], jnp.where(halves[h % 2], xs_ps[h // 2], 0.0)) for h in range(S_HEADS)]
    inter = [_dot_nt(cgs[group_of[pr]], hps[pr]) for pr in range(N_PAIRS)]
    ys = [intra[2 * pr] + intra[2 * pr + 1]
          + inter[pr] * jnp.where(is_a, jnp.exp(cum_c[2 * pr]), jnp.exp(cum_c[2 * pr + 1])) for pr in range(N_PAIRS)]
    tes = [jnp.where(is_a, jnp.exp(last[2 * pr] - cum_c[2 * pr]) * dt[:, 2 * pr:2 * pr + 1],
                     jnp.exp(last[2 * pr + 1] - cum_c[2 * pr + 1]) * dt[:, 2 * pr + 1:2 * pr + 2])
           for pr in range(N_PAIRS)]
    sts = [_dot_tn(xs_ps[pr] * tes[pr], bgs[group_of[pr]]) for pr in range(N_PAIRS)]
    for pr in range(N_PAIRS):
        h_ref[pr] = hps[pr] * jnp.where(row_a, jnp.exp(last[2 * pr]), jnp.exp(last[2 * pr + 1])) + sts[pr]
    y = jnp.concatenate(ys, axis=1) + dvec_ref[...] * xs
    yz = y * _silu(z_ref[rows, :])
    gw = S_W // S_GROUPS
    for g in range(S_GROUPS):
        part = yz[:, g * gw:(g + 1) * gw]
        ms = jnp.mean(part * part, axis=-1, keepdims=True)
        o_ref[rows, g * gw:(g + 1) * gw] = part * lax.rsqrt(ms + NORM_EPS) * ng_ref[:, g * gw:(g + 1) * gw]


def _ssd_prompt(proj, bsz, seq, cw, cb, dtb, alog, dvec, ng):
    L = SSD_CHUNK
    n_sub = 4 if seq % (4 * L) == 0 else 1
    nc = seq // (L * n_sub)
    rows = L * n_sub
    const = lambda shape: pl.BlockSpec(shape, lambda b, c: (0,) * len(shape))
    return pl.pallas_call(
        functools.partial(_ssd_prompt_kernel, L=L, n_sub=n_sub),
        grid=(bsz, nc),
        in_specs=[pl.BlockSpec((rows, S_W), lambda b, c: (b * nc + c, C_Z // S_W)),
                  pl.BlockSpec((rows, S_CONV_CH), lambda b, c: (b * nc + c, C_XBC // S_CONV_CH)),
                  pl.BlockSpec((rows, LANES), lambda b, c: (b * nc + c, C_DT // LANES)),
                  const((S_CONV, S_CONV_CH)), const((1, S_CONV_CH)), const((1, LANES)), const((1, LANES)),
                  const((1, S_W)), const((1, S_W))],
        out_specs=[pl.BlockSpec((rows, S_W), lambda b, c: (b * nc + c, 0)),
                   pl.BlockSpec((1, N_PAIRS, 2 * S_HD, S_STATE), lambda b, c: (b, 0, 0, 0))],
        out_shape=[jax.ShapeDtypeStruct((bsz * seq, S_W), F32),
                   jax.ShapeDtypeStruct((bsz, N_PAIRS, 2 * S_HD, S_STATE), F32)],
        scratch_shapes=[pltpu.VMEM((L + 2 * SUBLANES, S_CONV_CH), F32),
                        pltpu.VMEM((N_PAIRS, 2 * S_HD, S_STATE), F32)],
        compiler_params=_cparams(("parallel", "arbitrary")),
        name="ssd_prompt",
    )(proj, proj, proj, cw, cb, dtb, alog, dvec, ng)


def _ssd_sample_kernel(z_ref, xbc_ref, dt_ref, pre_ref, h0_ref, cw_ref, cb_ref, dtbv_ref, alogv_ref,
                       dvec_ref, ng_ref, o_ref, hout_ref, *, T, nb):
    rows2 = 2 * S_HD
    eye = (lax.broadcasted_iota(jnp.int32, (rows2, LANES), 0)
           == lax.broadcasted_iota(jnp.int32, (rows2, LANES), 1)).astype(F32)
    eye_all = jnp.concatenate([eye] * nb, axis=0)
    ones = jnp.ones((LANES, LANES), F32)
    spread = (lax.broadcasted_iota(jnp.int32, (LANES, S_W), 0)
              == lax.shift_right_logical(lax.broadcasted_iota(jnp.int32, (LANES, S_W), 1), HALF_SHIFT)).astype(F32)
    a_vec = -jnp.exp(alogv_ref[...])
    gw = S_W // S_GROUPS

    def as_columns(x, passes):
        return _split_dot(eye_all * _repeat_rows(x, rows2), ones, passes)

    up = [pre_ref[j] for j in range(S_CONV - 1)] + [xbc_ref[t] for t in range(T)]
    hs = [h0_ref[0, :, pr].reshape(nb * rows2, S_STATE) for pr in range(N_PAIRS)]
    for t in range(T):
        acc = cb_ref[...] + cw_ref[0:1, :] * up[t]
        for j in range(1, S_CONV):
            acc = acc + cw_ref[j:j + 1, :] * up[t + j]
        xc = _silu(acc)
        xs = xc[:, :S_W]
        dt = _softplus(_split_dot(dt_ref[t], spread, 3) + dtbv_ref[...])
        decay = jnp.exp(dt * a_vec)
        xdt = xs * dt
        ys = []
        for pr in range(N_PAIRS):
            g = (2 * pr * S_GROUPS) // S_HEADS
            sl = slice(pr * LANES, (pr + 1) * LANES)
            b_rows = _repeat_rows(xc[:, S_W + g * S_STATE:S_W + (g + 1) * S_STATE], rows2)
            c_rows = _repeat_rows(xc[:, S_W + (S_GROUPS + g) * S_STATE:S_W + (S_GROUPS + g + 1) * S_STATE], rows2)
            hs[pr] = hs[pr] * as_columns(decay[:, sl], 3) + as_columns(xdt[:, sl], 2) * b_rows
            y_col = _split_dot(hs[pr] * c_rows, ones, 2)
            ys.append(jnp.sum((eye_all * y_col).reshape(nb, rows2, LANES), axis=1))
        y = jnp.concatenate(ys, axis=1) + dvec_ref[...] * xs
        yz = y * _silu(z_ref[t])
        for g in range(S_GROUPS):
            part = yz[:, g * gw:(g + 1) * gw]
            ms = jnp.mean(part * part, axis=-1, keepdims=True)
            o_ref[t, :, g * gw:(g + 1) * gw] = part * lax.rsqrt(ms + NORM_EPS) * ng_ref[:, g * gw:(g + 1) * gw]
    for pr in range(N_PAIRS):
        hout_ref[:, pr] = hs[pr].reshape(nb, rows2, S_STATE)


def _ssd_sample(proj3, pre3, h0, layer, cw, cb, dtbv, alogv, dvec, ng, nb=8):
    T, dbsz = proj3.shape[:2]
    const = lambda shape: pl.BlockSpec(shape, lambda j: (0,) * len(shape))
    hspec = pl.BlockSpec((nb, N_PAIRS, 2 * S_HD, S_STATE), lambda j: (j, 0, 0, 0))
    h0spec = pl.BlockSpec((1, nb, N_PAIRS, 2 * S_HD, S_STATE), lambda j: (layer, j, 0, 0, 0))
    return pl.pallas_call(
        functools.partial(_ssd_sample_kernel, T=T, nb=nb),
        grid=(dbsz // nb,),
        in_specs=[pl.BlockSpec((T, nb, S_W), lambda j: (0, j, C_Z // S_W)),
                  pl.BlockSpec((T, nb, S_CONV_CH), lambda j: (0, j, C_XBC // S_CONV_CH)),
                  pl.BlockSpec((T, nb, LANES), lambda j: (0, j, C_DT // LANES)),
                  pl.BlockSpec((S_CONV - 1, nb, S_CONV_CH), lambda j: (0, j, 0)),
                  h0spec,
                  const((S_CONV, S_CONV_CH)), const((1, S_CONV_CH)), const((1, S_W)), const((1, S_W)),
                  const((1, S_W)), const((1, S_W))],
        out_specs=[pl.BlockSpec((T, nb, S_W), lambda j: (0, j, 0)), hspec],
        out_shape=[jax.ShapeDtypeStruct((T, dbsz, S_W), F32),
                   jax.ShapeDtypeStruct((dbsz, N_PAIRS, 2 * S_HD, S_STATE), F32)],
        compiler_params=_cparams(("parallel",)),
        name="ssd_sample",
    )(proj3, proj3, proj3, pre3, h0, cw, cb, dtbv, alogv, dvec, ng)


def _rwkv_prep_kernel(u_ref, pre_ref, mu_ref, w0_ref, w2_ref, a0_ref, a2_ref, g2_ref, kk_ref, ka_ref,
                      r_out, w_out, k_out, v_out, kk_out, kka_out, g_out, ext_ref,
                      *, tm, P, stride, tiles_per_seq):
    i = pl.program_id(0)
    first = (i % tiles_per_seq) == 0

    @pl.when(first)
    def _():
        ext_ref[0:P, :] = pre_ref[...]

    @pl.when(jnp.logical_not(first))
    def _():
        ext_ref[0:P, :] = ext_ref[tm:tm + P, :]

    u = u_ref[...]
    ext_ref[P:P + tm, :] = u
    prev = ext_ref[P - stride:P - stride + tm, :]
    x = u + (prev - u) * mu_ref[...]
    r = x[:, 0:R_W]
    kr = x[:, R_W:2 * R_W]
    vr = x[:, 2 * R_W:3 * R_W]
    xl = x[:, 3 * R_W:]
    w_log = w0_ref[...] + _dot(jnp.tanh(xl), w2_ref[...])
    log_decay = -jnp.exp(-_softplus(-w_log) - 0.5)
    a = _sigmoid(a0_ref[...] + _dot(xl, a2_ref[...]))
    g = _dot(_sigmoid(xl), g2_ref[...])
    kk = kr * kk_ref[...]
    ss = _split_dot(kk * kk, _head_ones(R_W, 1.0), 2)
    kk = kk / jnp.maximum(jnp.sqrt(ss), 1e-12)
    r_out[...] = r
    w_out[...] = log_decay
    k_out[...] = kr * (1.0 + (a - 1.0) * ka_ref[...])
    v_out[...] = vr
    kk_out[...] = kk
    kka_out[...] = kk * a
    g_out[...] = g


def _rwkv_prep(proj, pre, mu, w0, w2p, a0, a2p, g2p, kkw, kaw, tm, stride, tiles_per_seq):
    t = proj.shape[0]
    P = pre.shape[0]
    const = lambda shape: pl.BlockSpec(shape, lambda i: (0,) * len(shape))
    outs = pl.pallas_call(
        functools.partial(_rwkv_prep_kernel, tm=tm, P=P, stride=stride, tiles_per_seq=tiles_per_seq),
        grid=(t // tm,),
        in_specs=[pl.BlockSpec((tm, R_IN_W), lambda i: (i, 0)),
                  const((P, R_IN_W)), const((1, R_IN_W)), const((1, R_W)), const((R_LORA, R_W)),
                  const((1, R_W)), const((R_LORA, R_W)), const((R_LORA, R_W)), const((1, R_W)), const((1, R_W))],
        out_specs=[pl.BlockSpec((tm, R_W), lambda i: (i, 0))] * 7,
        out_shape=[jax.ShapeDtypeStruct((t, R_W), F32)] * 7,
        scratch_shapes=[pltpu.VMEM((tm + 2 * P, R_IN_W), F32)],
        compiler_params=_cparams(("arbitrary",)),
        name="rwkv_prep",
    )(proj, pre, mu, w0, w2p, a0, a2p, g2p, kkw, kaw)
    return outs


def _pair_consts():
    lane = lax.broadcasted_iota(jnp.int32, (R_HD, LANES), 1)
    row = lax.broadcasted_iota(jnp.int32, (R_HD, LANES), 0)
    is_a = lane < HALF
    eye2 = (jnp.bitwise_and(lane, HALF - 1) == row).astype(F32)
    return is_a, eye2


def _unit_lower_inverses(ns):
    L = ns[0].shape[0]
    eye = (lax.broadcasted_iota(jnp.int32, (L, L), 0) == lax.broadcasted_iota(jnp.int32, (L, L), 1)).astype(F32)
    ts = [eye + n for n in ns]
    pws = [_dot(n, n) for n in ns]
    for _ in range(int(math.log2(L)) - 2):
        both = [_dot(jnp.concatenate([t, p], axis=0), p) for t, p in zip(ts, pws)]
        ts = [t + b[:L] for t, b in zip(ts, both)]
        pws = [b[L:] for b in both]
    return [t + _dot(t, p) for t, p in zip(ts, pws)]


def _rwkv_chunk_prompt_kernel(*refs, L, n_sub):
    sout_ref, s_ref = refs[7], refs[8]

    @pl.when(pl.program_id(1) == 0)
    def _():
        s_ref[...] = jnp.zeros_like(s_ref)

    def chunk(j, carry):
        _rwkv_chunk(pl.ds(pl.multiple_of(j * L, L), L), *refs[:7], s_ref, L=L)
        return carry

    lax.fori_loop(0, n_sub, chunk, 0)

    @pl.when(pl.program_id(1) == pl.num_programs(1) - 1)
    def _():
        sout_ref[0] = s_ref[...]


def _rwkv_chunk(rows, r_ref, lw_ref, k_ref, v_ref, kk_ref, kka_ref, y_ref, s_ref, *, L):
    lane = lax.broadcasted_iota(jnp.int32, (1, LANES), 1)
    is_a = lane < HALF
    rr = lax.broadcasted_iota(jnp.int32, (L, L), 0)
    cc = lax.broadcasted_iota(jnp.int32, (L, L), 1)
    incl = rr >= cc
    strict = rr > cc
    tri = jnp.where(incl, 1.0, 0.0).astype(F32)
    r2 = lax.broadcasted_iota(jnp.int32, (LANES, LANES), 0) < HALF
    c2 = lax.broadcasted_iota(jnp.int32, (LANES, LANES), 1) < HALF
    same_head = r2 == c2

    r, lw, k, v, kk, kka = (ref[rows, :] for ref in (r_ref, lw_ref, k_ref, v_ref, kk_ref, kka_ref))
    cum = _split_dot(lw, tri, 3, x_is_lhs=False)
    last = cum[L - 1:L, :]
    inv_p = jnp.exp(-cum)
    to_end = jnp.exp(last - cum)
    b_t = kk * jnp.exp(cum - lw)
    a_t = -kka * inv_p
    k_t = k * inv_p
    r_t = r * jnp.exp(cum)
    a_end = -kka * to_end
    k_end = k * to_end
    decay_end = jnp.exp(last)
    pairs = [slice(pr * LANES, (pr + 1) * LANES) for pr in range(N_PAIRS)]
    halves = (is_a, jnp.logical_not(is_a))
    s0 = [s_ref[pr] for pr in range(N_PAIRS)]

    ns, mks, rak = [], [], []
    for sl in pairs:
        cols = jnp.concatenate([a_t[:, sl], k_t[:, sl]], axis=0)
        for half in halves:
            lhs = jnp.concatenate([jnp.where(half, b_t[:, sl], 0.0), jnp.where(half, r_t[:, sl], 0.0)], axis=0)
            g4 = _dot_nt(lhs, cols)
            ns.append(jnp.where(strict, g4[:L, :L], 0.0))
            mks.append(jnp.where(strict, g4[:L, L:], 0.0))
            rak.append(jnp.concatenate([jnp.where(incl, g4[L:, :L], 0.0), jnp.where(incl, g4[L:, L:], 0.0)], axis=1))
    ts = _unit_lower_inverses(ns)
    from_state = [_dot_nt(jnp.concatenate([b_t[:, sl], r_t[:, sl]], axis=0), s0[pr]) for pr, sl in enumerate(pairs)]
    rhs = []
    for pr, sl in enumerate(pairs):
        mv = _dot(jnp.concatenate([mks[2 * pr], mks[2 * pr + 1]], axis=0), v[:, sl])
        rhs.append(from_state[pr][:L] + jnp.where(is_a, mv[:L], mv[L:]))
    us = []
    for pr in range(N_PAIRS):
        tu = _dot(jnp.concatenate([ts[2 * pr], ts[2 * pr + 1]], axis=0), rhs[pr])
        us.append(jnp.where(is_a, tu[:L], tu[L:]))
    for pr, sl in enumerate(pairs):
        uv = jnp.concatenate([us[pr], v[:, sl]], axis=0)
        y_ref[rows, sl] = from_state[pr][L:] + jnp.where(is_a, _dot(rak[2 * pr], uv), _dot(rak[2 * pr + 1], uv))
    for pr, sl in enumerate(pairs):
        upd = _dot_tn(jnp.concatenate([us[pr], v[:, sl]], axis=0),
                      jnp.concatenate([a_end[:, sl], k_end[:, sl]], axis=0))
        s_ref[pr] = s0[pr] * decay_end[:, sl] + jnp.where(same_head, upd, 0.0)


def _rwkv_scan_prompt(seqs, bsz, seq, L):
    n_sub = 4 if seq % (4 * L) == 0 else 1
    nc = seq // (L * n_sub)
    spec = pl.BlockSpec((L * n_sub, R_W), lambda b, c: (b * nc + c, 0))
    return pl.pallas_call(
        functools.partial(_rwkv_chunk_prompt_kernel, L=L, n_sub=n_sub),
        grid=(bsz, nc),
        in_specs=[spec] * 6,
        out_specs=[spec, pl.BlockSpec((1, N_PAIRS, LANES, LANES), lambda b, c: (b, 0, 0, 0))],
        out_shape=[jax.ShapeDtypeStruct((bsz * seq, R_W), F32),
                   jax.ShapeDtypeStruct((bsz, N_PAIRS, LANES, LANES), F32)],
        scratch_shapes=[pltpu.VMEM((N_PAIRS, LANES, LANES), F32)],
        compiler_params=_cparams(("parallel", "arbitrary")),
        name="rwkv_scan_prompt",
    )(*seqs)


def _repeat_rows(x, reps):
    return jnp.concatenate([jnp.broadcast_to(x[b:b + 1, :], (reps, x.shape[1])) for b in range(x.shape[0])], axis=0)


def _rwkv_scan_sample_kernel(r_ref, w_ref, k_ref, v_ref, kk_ref, kka_ref, s0_ref, y_ref, sout_ref, *, T, nb):
    _, eye2 = _pair_consts()
    eye_all = jnp.concatenate([eye2] * nb, axis=0)
    ones_bd = _head_ones(LANES, 1.0)

    def head_sums(x):
        return _split_dot(x, ones_bd, 2)

    states = [s0_ref[:, pr].reshape(nb * R_HD, LANES) for pr in range(N_PAIRS)]
    for t in range(T):
        for pr in range(N_PAIRS):
            sl = slice(pr * LANES, (pr + 1) * LANES)
            rr, ww, kr, vv, kk, kka = (_repeat_rows(x, R_HD) for x in (
                r_ref[t, :, sl], jnp.exp(w_ref[t, :, sl]), k_ref[t, :, sl], v_ref[t, :, sl],
                kk_ref[t, :, sl], kka_ref[t, :, sl]))
            v_col = head_sums(eye_all * vv)
            s = states[pr]
            s = s * ww - head_sums(s * kk) * kka + v_col * kr
            states[pr] = s
            y_col = head_sums(s * rr)
            y_ref[t, :, sl] = jnp.sum((eye_all * y_col).reshape(nb, R_HD, LANES), axis=1)
    for pr in range(N_PAIRS):
        sout_ref[:, pr] = states[pr].reshape(nb, R_HD, LANES)


def _rwkv_scan_sample(seqs3, s0, nb=16):
    T, dbsz = seqs3[0].shape[:2]
    spec = pl.BlockSpec((T, nb, R_W), lambda j: (0, j, 0))
    sspec = pl.BlockSpec((nb, N_PAIRS, R_HD, LANES), lambda j: (j, 0, 0, 0))
    return pl.pallas_call(
        functools.partial(_rwkv_scan_sample_kernel, T=T, nb=nb),
        grid=(dbsz // nb,),
        in_specs=[spec] * 6 + [sspec],
        out_specs=[spec, sspec],
        out_shape=[jax.ShapeDtypeStruct((T, dbsz, R_W), F32),
                   jax.ShapeDtypeStruct((dbsz, N_PAIRS, R_HD, LANES), F32)],
        compiler_params=_cparams(("parallel",)),
        name="rwkv_scan_sample",
    )(*seqs3, s0)


def _merge_kernel(x_ref, oa_ref, ob_ref, yr_ref, r_ref, k_ref, v_ref, g_ref, ga_ref, gb_ref, gc_ref,
                  bg_ref, lng_ref, lnb_ref, rk_ref, wpa_ref, wpb_ref, wpc_ref, wo_ref, o_ref):
    pa = _dot(oa_ref[...], wpa_ref[...])
    pb = _dot(ob_ref[...], wpb_ref[...])
    mean_m = _head_ones(R_W, 1.0 / R_HD)
    yr = yr_ref[...]
    bonus = _split_dot(r_ref[...] * k_ref[...] * rk_ref[...], _head_ones(R_W, 1.0), 2)
    d = yr - _split_dot(yr, mean_m, 2)
    var = _split_dot(d * d, mean_m, 2)
    yn = d * lax.rsqrt(var + GN_EPS) * lng_ref[...] + lnb_ref[...]
    oc = (yn + bonus * v_ref[...]) * g_ref[...]
    merged = (_sigmoid(ga_ref[...] + bg_ref[:, 0:D_MODEL]) * pa
              + _sigmoid(gb_ref[...] + bg_ref[:, D_MODEL:2 * D_MODEL]) * pb
              + _sigmoid(gc_ref[...] + bg_ref[:, 2 * D_MODEL:]) * _dot(oc, wpc_ref[...]))
    o_ref[...] = x_ref[...] + _dot(merged, wo_ref[...])


def _merge(x, proj, oa, ob, yr, r, k2, v, g, bg, lng, lnb, rk, wpa, wpb, wpc, wo, tm):
    t = x.shape[0]
    row = lambda w: pl.BlockSpec((tm, w), lambda i: (i, 0))
    const = lambda shape: pl.BlockSpec(shape, lambda i: (0,) * len(shape))
    gcol = C_GATE // D_MODEL
    gate = lambda j: pl.BlockSpec((tm, D_MODEL), lambda i: (i, gcol + j))
    return pl.pallas_call(
        _merge_kernel,
        grid=(t // tm,),
        in_specs=[row(D_MODEL), row(A_W), row(S_W), row(R_W), row(R_W), row(R_W), row(R_W), row(R_W),
                  gate(0), gate(1), gate(2),
                  const((1, 3 * D_MODEL)), const((1, R_W)), const((1, R_W)), const((1, R_W)),
                  const((A_W, D_MODEL)), const((S_W, D_MODEL)), const((R_W, D_MODEL)),
                  const((D_MODEL, D_MODEL))],
        out_specs=row(D_MODEL),
        out_shape=jax.ShapeDtypeStruct((t, D_MODEL), F32),
        compiler_params=_cparams(("parallel",)),
        name="merge",
    )(x, oa, ob, yr, r, k2, v, g, proj, proj, proj, bg, lng, lnb, rk, wpa, wpb, wpc, wo)


def _ffn_kernel(x_ref, g_ref, wug_ref, wuv_ref, wd_ref, cw_ref, cb_ref, pre_ref, gf_ref,
                o_ref, tail_ref, xn_ref, ext_ref, *, tm, P, stride, tiles_per_seq, final_norm):
    i = pl.program_id(0)
    f = pl.program_id(1)
    first = (i % tiles_per_seq) == 0

    @pl.when(f == 0)
    def _():
        x = x_ref[...]
        ms = jnp.mean(x * x, axis=-1, keepdims=True)
        xn_ref[...] = (x * lax.rsqrt(ms + NORM_EPS) * g_ref[...]).astype(BF16)

    @pl.when(first)
    def _():
        ext_ref[f, 0:P, :] = pre_ref[...]

    @pl.when(jnp.logical_not(first))
    def _():
        ext_ref[f, 0:P, :] = ext_ref[f, tm:tm + P, :]

    xn = xn_ref[...]
    ug = jnp.dot(xn, wug_ref[...], preferred_element_type=F32)
    uv = jnp.dot(xn, wuv_ref[...], preferred_element_type=F32)
    ext_ref[f, P:P + tm, :] = ug
    tail_ref[0] = ext_ref[f, tm:tm + P, :]
    acc = cb_ref[...] + cw_ref[F_CONV - 1:F_CONV, :] * ug
    for d in range(1, F_CONV):
        acc = acc + cw_ref[F_CONV - 1 - d:F_CONV - d, :] * ext_ref[f, P - d * stride:P - d * stride + tm, :]
    contrib = _dot(_silu(acc) * uv, wd_ref[...])

    @pl.when(f == 0)
    def _():
        o_ref[...] = x_ref[...] + contrib

    @pl.when(f > 0)
    def _():
        o_ref[...] = o_ref[...] + contrib

    if final_norm:
        @pl.when(f == pl.num_programs(1) - 1)
        def _():
            y = o_ref[...]
            ms = jnp.mean(y * y, axis=-1, keepdims=True)
            o_ref[...] = y * lax.rsqrt(ms + NORM_EPS) * gf_ref[...]


def _ffn(x, g, wup, wd, cw, cb, pre, gf, tm, tf, stride, tiles_per_seq, final_norm):
    t = x.shape[0]
    P = pre.shape[0]
    nf = D_FF // tf
    return pl.pallas_call(
        functools.partial(_ffn_kernel, tm=tm, P=P, stride=stride, tiles_per_seq=tiles_per_seq,
                          final_norm=final_norm),
        grid=(t // tm, nf),
        in_specs=[pl.BlockSpec((tm, D_MODEL), lambda i, f: (i, 0)),
                  pl.BlockSpec((1, D_MODEL), lambda i, f: (0, 0)),
                  pl.BlockSpec((D_MODEL, tf), lambda i, f: (0, f)),
                  pl.BlockSpec((D_MODEL, tf), lambda i, f: (0, nf + f)),
                  pl.BlockSpec((tf, D_MODEL), lambda i, f: (f, 0)),
                  pl.BlockSpec((F_CONV, tf), lambda i, f: (0, f)),
                  pl.BlockSpec((1, tf), lambda i, f: (0, f)),
                  pl.BlockSpec((P, tf), lambda i, f: (0, f)),
                  pl.BlockSpec((1, D_MODEL), lambda i, f: (0, 0))],
        out_specs=[pl.BlockSpec((tm, D_MODEL), lambda i, f: (i, 0)),
                   pl.BlockSpec((1, P, tf), lambda i, f: (i, 0, f))],
        out_shape=[jax.ShapeDtypeStruct((t, D_MODEL), F32),
                   jax.ShapeDtypeStruct((t // tm, P, D_FF), F32)],
        scratch_shapes=[pltpu.VMEM((tm, D_MODEL), BF16), pltpu.VMEM((nf, tm + 2 * P, tf), F32)],
        compiler_params=_cparams(("arbitrary", "arbitrary")),
        name="conv_ffn",
    )(x, g, wup, wup, wd, cw, cb, pre, gf)


def _pack_rwkv_state(s):
    n = s.shape[0]
    return s.reshape(n, N_PAIRS, 2, R_HD, R_HD).transpose(0, 1, 3, 2, 4).reshape(n, N_PAIRS, R_HD, LANES)


def _unpack_rwkv_state(s):
    n = s.shape[0]
    return s.reshape(n, N_PAIRS, R_HD, 2, R_HD).transpose(0, 1, 3, 2, 4).reshape(n, R_HEADS, R_HD, R_HD)


def _unpack_rwkv_blockdiag(s):
    n = s.shape[0]
    return jnp.stack([s[:, :, :R_HD, :R_HD], s[:, :, R_HD:, R_HD:]], axis=2).reshape(n, R_HEADS, R_HD, R_HD)


def _prep_layer_params(l, p):
    w_in = p['w_in'][l]
    c_dt_src = 3 * A_W + S_W + S_CONV_CH
    c_rw_src = c_dt_src + S_HEADS
    c_gate_src = c_rw_src + R_IN_W
    w_proj = jnp.concatenate([
        w_in[:, c_rw_src:c_gate_src],
        w_in[:, c_dt_src:c_rw_src], jnp.zeros((D_MODEL, C_Q - C_DT - S_HEADS), F32),
        w_in[:, :c_dt_src],
        w_in[:, c_gate_src:]], axis=1).astype(BF16)
    pad_lane = lambda v: jnp.pad(v, (0, LANES - v.shape[0])).reshape(1, LANES)
    zl = lambda r0, w: jnp.zeros((R_LORA, R_W), F32).at[r0:r0 + w.shape[0]].set(w).astype(BF16)
    return dict(
        norm1_g=p['norm1_g'][l].reshape(1, D_MODEL), w_proj=w_proj,
        b_gate=p['b_gate'][l].reshape(1, 3 * D_MODEL),
        w_pa=p['w_pa'][l].astype(BF16), w_pb=p['w_pb'][l].astype(BF16), w_pc=p['w_pc'][l].astype(BF16),
        w_o=p['w_o'][l].astype(BF16),
        ssm_conv_w=p['ssm_conv_w'][l], ssm_conv_b=p['ssm_conv_b'][l].reshape(1, S_CONV_CH),
        ssm_dt_bias=pad_lane(p['ssm_dt_bias'][l]), ssm_a_log=pad_lane(p['ssm_a_log'][l]),
        ssm_dvec=jnp.repeat(p['ssm_d'][l], S_HD).reshape(1, S_W),
        ssm_dtb_vec=jnp.repeat(p['ssm_dt_bias'][l], S_HD).reshape(1, S_W),
        ssm_alog_vec=jnp.repeat(p['ssm_a_log'][l], S_HD).reshape(1, S_W),
        ssm_norm_g=p['ssm_norm_g'][l].reshape(1, S_W),
        rw_mu=p['rw_mu'][l].reshape(1, R_IN_W), rw_w0=p['rw_w0'][l].reshape(1, R_W),
        rw_w2p=zl(0, p['rw_w2'][l]), rw_a0=p['rw_a0'][l].reshape(1, R_W),
        rw_a2p=zl(R_LORA_W, p['rw_a2'][l]), rw_g2p=zl(R_LORA_W + R_LORA_A, p['rw_g2'][l]),
        rw_kk=p['rw_kk'][l].reshape(1, R_W), rw_ka=p['rw_ka'][l].reshape(1, R_W),
        rw_rk=p['rw_rk'][l].reshape(1, R_W), rw_ln_g=p['rw_ln_g'][l].reshape(1, R_W),
        rw_ln_b=p['rw_ln_b'][l].reshape(1, R_W),
        norm2_g=p['norm2_g'][l].reshape(1, D_MODEL), w_up=p['w_up'][l].astype(BF16),
        w_down=p['w_down'][l].astype(BF16), ffn_conv_w=p['ffn_conv_w'][l],
        ffn_conv_b=p['ffn_conv_b'][l].reshape(1, D_FF))


def _row_tile(t):
    for tm in (512, 256, 128):
        if t % tm == 0:
            return tm
    raise ValueError(t)


def _ffn_tf():
    return D_FF // 2


def _prompt_layer(x, lp, bsz, seq, slopes_pair, gf, final_norm):
    t = bsz * seq
    tm = _row_tile(seq)
    tiles = seq // tm
    proj = _rms_matmul(x, lp['norm1_g'], lp['w_proj'], 1024 if t % 1024 == 0 else tm, 2048)
    proj3 = proj.reshape(bsz, seq, N_PROJ)
    oa, k_t, v_t = _moba_prompt(proj3, slopes_pair)
    oa = oa.reshape(t, A_W)
    ob, ssm_new = _ssd_prompt(proj, bsz, seq, lp['ssm_conv_w'], lp['ssm_conv_b'], lp['ssm_dt_bias'],
                              lp['ssm_a_log'], lp['ssm_dvec'], lp['ssm_norm_g'])
    r, w, k2, v, kk, kka, g = _rwkv_prep(
        proj, jnp.zeros((SUBLANES, R_IN_W), F32), lp['rw_mu'], lp['rw_w0'], lp['rw_w2p'], lp['rw_a0'],
        lp['rw_a2p'], lp['rw_g2p'], lp['rw_kk'], lp['rw_ka'], tm, 1, tiles)
    yr, rw_new = _rwkv_scan_prompt((r, w, k2, v, kk, kka), bsz, seq, LANES)
    x = _merge(x, proj, oa, ob, yr, r, k2, v, g, lp['b_gate'], lp['rw_ln_g'], lp['rw_ln_b'], lp['rw_rk'],
               lp['w_pa'], lp['w_pb'], lp['w_pc'], lp['w_o'], min(tm, 256))
    x, tail = _ffn(x, lp['norm2_g'], lp['w_up'], lp['w_down'], lp['ffn_conv_w'], lp['ffn_conv_b'],
                   jnp.zeros((SUBLANES, D_FF), F32), gf, tm, _ffn_tf(), 1, tiles, final_norm)
    k_new = k_t.reshape(bsz, A_HEADS, A_HD, seq).transpose(0, 3, 1, 2)
    v_new = v_t.reshape(bsz, A_HEADS, A_HD, seq).transpose(0, 3, 1, 2)
    ssm_conv_new = proj3[:, seq - (S_CONV - 1):, C_XBC:C_XBC + S_CONV_CH]
    shift_new = proj3[:, seq - 1:, C_RW:C_RW + R_IN_W]
    ffn_conv_new = tail.reshape(bsz, tiles, SUBLANES, D_FF)[:, tiles - 1, SUBLANES - (F_CONV - 1):]
    state = (k_new, v_new, ssm_new.reshape(bsz, S_HEADS, S_HD, S_STATE), ssm_conv_new,
             _unpack_rwkv_blockdiag(rw_new), shift_new, ffn_conv_new)
    return x, state


def _sample_layer(x, lp, dbsz, tnew, st, cache_k4, cache_v4, page_table, layer, gf, final_norm):
    t = tnew * dbsz
    ssm0, ssm_conv0, rwkv0, shift0, ffn_conv0 = st
    proj = _rms_matmul(x, lp['norm1_g'], lp['w_proj'], t, 1024)
    proj3 = proj.reshape(tnew, dbsz, N_PROJ)

    qkv = proj3[:, :, C_Q:C_Q + 3 * A_W].transpose(1, 0, 2)
    qkv8 = jnp.pad(qkv, ((0, 0), (0, SUBLANES - tnew), (0, 0)))
    oa8 = _moba_sample(qkv8[:, :, :A_W], qkv8[:, :, A_W:2 * A_W], qkv8[:, :, 2 * A_W:],
                       cache_k4, cache_v4, page_table, layer, tnew)
    oa = oa8[:, :tnew].transpose(1, 0, 2).reshape(t, A_W)

    h0 = ssm0.reshape(-1, dbsz, N_PAIRS, 2 * S_HD, S_STATE)
    ob3, ssm_new = _ssd_sample(proj3, ssm_conv0.transpose(1, 0, 2), h0, layer, lp['ssm_conv_w'],
                               lp['ssm_conv_b'], lp['ssm_dtb_vec'], lp['ssm_alog_vec'], lp['ssm_dvec'],
                               lp['ssm_norm_g'])
    ob = ob3.reshape(t, S_W)
    ssm_new = ssm_new.reshape(dbsz, S_HEADS, S_HD, S_STATE)

    r, w, k2, v, kk, kka, g = _rwkv_prep(
        proj, shift0.reshape(dbsz, R_IN_W), lp['rw_mu'], lp['rw_w0'], lp['rw_w2p'], lp['rw_a0'],
        lp['rw_a2p'], lp['rw_g2p'], lp['rw_kk'], lp['rw_ka'], t, dbsz, 1)
    to3 = lambda a: a.reshape(tnew, dbsz, R_W)
    yr3, rw_new = _rwkv_scan_sample(tuple(to3(a) for a in (r, w, k2, v, kk, kka)), _pack_rwkv_state(rwkv0))
    yr = yr3.reshape(t, R_W)

    x = _merge(x, proj, oa, ob, yr, r, k2, v, g, lp['b_gate'], lp['rw_ln_g'], lp['rw_ln_b'], lp['rw_rk'],
               lp['w_pa'], lp['w_pb'], lp['w_pc'], lp['w_o'], min(t, 256))
    pre = ffn_conv0.transpose(1, 0, 2).reshape((F_CONV - 1) * dbsz, D_FF)
    x, tail = _ffn(x, lp['norm2_g'], lp['w_up'], lp['w_down'], lp['ffn_conv_w'], lp['ffn_conv_b'],
                   pre, gf, t, _ffn_tf(), dbsz, 1, final_norm)

    k_new = qkv[:, :, A_W:2 * A_W].reshape(dbsz, tnew, A_HEADS, A_HD)
    v_new = qkv[:, :, 2 * A_W:].reshape(dbsz, tnew, A_HEADS, A_HD)
    ssm_conv_new = proj3[tnew - (S_CONV - 1):, :, C_XBC:C_XBC + S_CONV_CH].transpose(1, 0, 2)
    shift_new = proj3[tnew - 1:, :, C_RW:C_RW + R_IN_W].transpose(1, 0, 2)
    ffn_conv_new = tail.reshape(F_CONV - 1, dbsz, D_FF).transpose(1, 0, 2)
    state = (k_new, v_new, ssm_new, ssm_conv_new, _unpack_rwkv_state(rw_new), shift_new, ffn_conv_new)
    return x, state


def kernel(x_prompt, x_sample, cache_k, cache_v, state_ssm, state_ssm_conv, state_rwkv, state_rwkv_shift, state_ffn_conv, page_table, norm1_g, w_in, b_gate, w_pa, ssm_conv_w, ssm_conv_b, ssm_dt_bias, ssm_a_log, ssm_d, ssm_norm_g, w_pb, rw_mu, rw_w0, rw_w2, rw_a0, rw_a2, rw_g2, rw_kk, rw_ka, rw_rk, rw_ln_g, rw_ln_b, w_pc, w_o, norm2_g, w_up, ffn_conv_w, ffn_conv_b, w_down, norm_f_g):
    params = dict(norm1_g=norm1_g, w_in=w_in, b_gate=b_gate, w_pa=w_pa, ssm_conv_w=ssm_conv_w,
                  ssm_conv_b=ssm_conv_b, ssm_dt_bias=ssm_dt_bias, ssm_a_log=ssm_a_log, ssm_d=ssm_d,
                  ssm_norm_g=ssm_norm_g, w_pb=w_pb, rw_mu=rw_mu, rw_w0=rw_w0, rw_w2=rw_w2, rw_a0=rw_a0,
                  rw_a2=rw_a2, rw_g2=rw_g2, rw_kk=rw_kk, rw_ka=rw_ka, rw_rk=rw_rk, rw_ln_g=rw_ln_g,
                  rw_ln_b=rw_ln_b, w_pc=w_pc, w_o=w_o, norm2_g=norm2_g, w_up=w_up, ffn_conv_w=ffn_conv_w,
                  ffn_conv_b=ffn_conv_b, w_down=w_down)
    depth = w_in.shape[0]
    bsz, seq, _ = x_prompt.shape
    dbsz, tnew, _ = x_sample.shape
    head = jnp.arange(A_HEADS, dtype=F32) + 1.0
    slopes = jnp.exp2(-8.0 * head / A_HEADS)
    slopes_pair = jnp.repeat(slopes, A_HD).reshape(N_PAIRS, 1, LANES)
    cache_k4 = cache_k.transpose(0, 1, 3, 4, 2)
    cache_v4 = cache_v.transpose(0, 1, 3, 4, 2)
    gf = norm_f_g.reshape(1, D_MODEL)

    hp = x_prompt.reshape(bsz * seq, D_MODEL)
    hs = x_sample.transpose(1, 0, 2).reshape(tnew * dbsz, D_MODEL)
    new_p = [[] for _ in range(7)]
    new_s = [[] for _ in range(7)]
    for l in range(depth):
        lp = _prep_layer_params(l, params)
        last = l == depth - 1
        hp, sp = _prompt_layer(hp, lp, bsz, seq, slopes_pair, gf, last)
        st = (state_ssm, state_ssm_conv[l], state_rwkv[l], state_rwkv_shift[l], state_ffn_conv[l])
        hs, ss = _sample_layer(hs, lp, dbsz, tnew, st, cache_k4, cache_v4, page_table, l, gf, last)
        for j in range(7):
            new_p[j].append(sp[j])
            new_s[j].append(ss[j])
    y_prompt = hp.reshape(bsz, seq, D_MODEL)
    y_sample = hs.reshape(tnew, dbsz, D_MODEL).transpose(1, 0, 2)
    outs = [y_prompt, y_sample]
    for j in range(7):
        outs += [jnp.stack(new_p[j]), jnp.stack(new_s[j])]
    return tuple(outs)
```

```python
import functools
import math

import jax
import jax.numpy as jnp
from jax import lax
from jax.experimental import pallas as pl
from jax.experimental.pallas import tpu as pltpu

F32 = jnp.float32
BF16 = jnp.bfloat16

D_MODEL = 1024
A_HEADS = 8
A_HD = 64
A_W = A_HEADS * A_HD
MOBA_BLOCK = 256
MOBA_TOPK = 3
Q_BLOCK = MOBA_BLOCK
ATTN_SCALE = A_HD ** -0.5
S_HEADS = 8
S_HD = 64
S_W = S_HEADS * S_HD
S_GROUPS = 2
S_STATE = 128
S_CONV = 4
S_CONV_CH = S_W + 2 * S_GROUPS * S_STATE
SSD_CHUNK = 128
R_HEADS = 8
R_HD = 64
R_W = R_HEADS * R_HD
R_LORA_W = 64
R_LORA_A = 64
R_LORA_G = 128
R_LORA = R_LORA_W + R_LORA_A + R_LORA_G
R_IN_W = 3 * R_W + R_LORA
D_FF = ((8 * D_MODEL // 3 + 127) // 128) * 128
F_CONV = 3
NORM_EPS = 1e-6
GN_EPS = 64e-5
NEG_INF = -1e30

LANES = 128
SUBLANES = 8
HALF = 64
HALF_SHIFT = 6
SUBLANE_SHIFT = 3
N_PAIRS = 4

C_RW = 0
C_DT = R_IN_W
C_Q = 2048
C_K = C_Q + A_W
C_V = C_K + A_W
C_Z = C_V + A_W
C_XBC = C_Z + S_W
C_GATE = C_XBC + S_CONV_CH
N_PROJ = C_GATE + 3 * D_MODEL

VMEM_LIMIT = 56 * 1024 * 1024


def _cparams(sem):
    return pltpu.CompilerParams(dimension_semantics=sem, vmem_limit_bytes=VMEM_LIMIT)


def _dot(a, b):
    return jnp.dot(a.astype(BF16), b.astype(BF16), preferred_element_type=F32)


def _dot_nt(a, b):
    return lax.dot_general(a.astype(BF16), b.astype(BF16), (((1,), (1,)), ((), ())),
                           preferred_element_type=F32)


def _dot_tn(a, b):
    return lax.dot_general(a.astype(BF16), b.astype(BF16), (((0,), (0,)), ((), ())),
                           preferred_element_type=F32)


def _split_dot(x, w, passes, x_is_lhs=True):
    w = w.astype(BF16)
    acc = None
    rem = x
    for _ in range(passes):
        piece = rem.astype(BF16)
        term = (jnp.dot(piece, w, preferred_element_type=F32) if x_is_lhs
                else jnp.dot(w, piece, preferred_element_type=F32))
        acc = term if acc is None else acc + term
        rem = rem - piece.astype(F32)
    return acc


def _dot_nt_hi(a, b):
    return lax.dot_general(a, b, (((1,), (1,)), ((), ())), preferred_element_type=F32,
                           precision=lax.Precision.HIGHEST)


def _sigmoid(x):
    return 1.0 / (1.0 + jnp.exp(-x))


def _silu(x):
    return x * _sigmoid(x)


def _softplus(x):
    return jnp.maximum(x, 0.0) + jnp.log(1.0 + jnp.exp(-jnp.abs(x)))


def _head_ones(width, scale):
    r = lax.shift_right_logical(lax.broadcasted_iota(jnp.int32, (width, width), 0), HALF_SHIFT)
    c = lax.shift_right_logical(lax.broadcasted_iota(jnp.int32, (width, width), 1), HALF_SHIFT)
    return jnp.where(r == c, scale, 0.0).astype(F32)


def _rms_matmul_kernel(x_ref, g_ref, w_ref, o_ref, xn_ref):
    @pl.when(pl.program_id(1) == 0)
    def _():
        x = x_ref[...]
        ms = jnp.mean(x * x, axis=-1, keepdims=True)
        xn_ref[...] = (x * lax.rsqrt(ms + NORM_EPS) * g_ref[...]).astype(BF16)

    o_ref[...] = jnp.dot(xn_ref[...], w_ref[...], preferred_element_type=F32)


def _rms_matmul(x, g, w, tm, tn):
    t, d = x.shape
    n = w.shape[1]
    return pl.pallas_call(
        _rms_matmul_kernel,
        grid=(t // tm, n // tn),
        in_specs=[pl.BlockSpec((tm, d), lambda i, j: (i, 0)),
                  pl.BlockSpec((1, d), lambda i, j: (0, 0)),
                  pl.BlockSpec((d, tn), lambda i, j: (0, j))],
        out_specs=pl.BlockSpec((tm, tn), lambda i, j: (i, j)),
        out_shape=jax.ShapeDtypeStruct((t, n), F32),
        scratch_shapes=[pltpu.VMEM((tm, d), BF16)],
        compiler_params=_cparams(("parallel", "arbitrary")),
        name="in_proj",
    )(x, g, w)


def _topk_bias(gate, n_valid, axis=1):
    pos = lax.broadcasted_iota(jnp.int32, gate.shape, axis)
    pos_f = pos.astype(F32)
    gm = jnp.where(pos < n_valid, gate, NEG_INF)
    selected = jnp.zeros(gate.shape, jnp.bool_)
    for j in range(MOBA_TOPK):
        m = jnp.max(gm, axis=axis, keepdims=True)
        idx = jnp.min(jnp.where(gm == m, pos_f, 1e9), axis=axis, keepdims=True)
        hit = pos_f == idx
        selected = jnp.logical_or(selected, jnp.logical_and(hit, j < n_valid))
        gm = jnp.where(hit, -jnp.inf, gm)
    return jnp.where(selected, 0.0, NEG_INF).astype(F32)


def _bf16_pieces(x, n):
    out = []
    rem = x
    for _ in range(n):
        piece = rem.astype(BF16).astype(F32)
        out.append(piece)
        rem = rem - piece
    return out


def _moba_prompt_kernel(*refs, nb, nq):
    def tile(i, carry):
        _moba_prompt_tile(i, *refs, nb=nb)
        return carry

    lax.fori_loop(0, nq, tile, 0)


def _moba_prompt_tile(i, q_ref, k_ref, v_ref, sl_ref, o_ref, kt_out, vt_out, kf_ref, vth_ref, kmean_ref, sel_ref,
                      acc_ref, raw_a, raw_b, m_ref, knorm_ref, *, nb):
    B = MOBA_BLOCK
    n_piece = 3
    tile_rows = pl.ds(pl.multiple_of(i * Q_BLOCK, Q_BLOCK), Q_BLOCK)
    lane = lax.broadcasted_iota(jnp.int32, (1, LANES), 1)
    halves = (lane < HALF, lane >= HALF)

    @pl.when(i == 0)
    def _():
        kmean_ref[...] = jnp.zeros_like(kmean_ref)
        key_local = jnp.bitwise_and(lax.broadcasted_iota(jnp.int32, (k_ref.shape[1], 1), 0), B - 1).astype(F32)
        extra = jnp.where(lane < n_piece, key_local, jnp.where(lane < 2 * n_piece, 1.0, 0.0))
        kf_ref[...] = jnp.concatenate([k_ref[0], extra], axis=1).astype(BF16)
        ones_rows = jnp.ones((SUBLANES, B), F32)
        knorm = [jnp.zeros((1, LANES), F32), jnp.zeros((1, LANES), F32)]
        for n in range(nb):
            kblk = k_ref[0, n * B:(n + 1) * B, :]
            kmean_ref[n:n + 1, :] = jnp.mean(kblk, axis=0, keepdims=True)
            vt = v_ref[0, n * B:(n + 1) * B, :].T
            kt_out[0, :, n * B:(n + 1) * B] = kblk.T
            vt_out[0, :, n * B:(n + 1) * B] = vt
            for h in range(2):
                vth_ref[h, n] = jnp.concatenate([vt[h * HALF:(h + 1) * HALF], ones_rows], axis=0).astype(BF16)
                norm2 = jnp.sum(jnp.where(halves[h], kblk * kblk, 0.0), axis=-1, keepdims=True)
                knorm[h] = jnp.where(lane == n, jnp.max(norm2, axis=0, keepdims=True), knorm[h])
        for h in range(2):
            knorm_ref[h] = jnp.broadcast_to(knorm[h], (SUBLANES, LANES))

    own = i
    q = q_ref[0, tile_rows, :]
    log2e = 1.0 / math.log(2.0)
    slope2 = (sl_ref[0, :, 0:1] * log2e, sl_ref[0, :, HALF:HALF + 1] * log2e)
    kmean = kmean_ref[...]
    q_local = lax.broadcasted_iota(jnp.int32, (Q_BLOCK, 1), 0).astype(F32)
    q_rows = []
    for h in range(2):
        q_m = jnp.where(halves[h], q, 0.0)
        bias_t = _topk_bias(_dot_nt_hi(kmean, q_m), own, axis=0)
        for n in range(nb):
            sel_ref[h, n] = jnp.broadcast_to(bias_t[n:n + 1, :], (SUBLANES, Q_BLOCK))
        pieces = _bf16_pieces(slope2[h], n_piece) + _bf16_pieces(-slope2[h] * q_local, n_piece)
        extra = jnp.zeros((Q_BLOCK, LANES), F32)
        for j, piece in enumerate(pieces):
            extra = jnp.where(lane == j, piece, extra)
        q_rows.append(jnp.concatenate([q_m * (ATTN_SCALE * log2e), extra], axis=1))
    qf = jnp.concatenate(q_rows, axis=0).astype(BF16)

    rel = (lax.broadcasted_iota(jnp.int32, (B, Q_BLOCK), 0) - lax.broadcasted_iota(jnp.int32, (B, Q_BLOCK), 1))

    def scores(n):
        both = _dot_nt(kf_ref[pl.ds(pl.multiple_of(n * B, B), B), :], qf)
        return [both[:, h * Q_BLOCK:(h + 1) * Q_BLOCK] for h in range(2)]

    def put_scores(n, dst_ref):
        for h, s in enumerate(scores(n)):
            dst_ref[h] = s

    def row_of(ref, h):
        return ref[h][0:1, :]

    def put_row(ref, h, x):
        ref[h] = jnp.broadcast_to(x, (SUBLANES, Q_BLOCK))

    def attend(n, src_ref):
        shift = ((n - i) * B).astype(F32)
        ps, alphas = [], []
        for h in range(2):
            s = src_ref[h] + (sel_ref[h, n][0:1, :] + slope2[h] * shift)
            m0 = row_of(m_ref, h)
            m1 = jnp.maximum(m0, jnp.max(s, axis=0, keepdims=True))
            alphas.append(jnp.exp2(m0 - m1))
            ps.append(jnp.exp2(s - m1).astype(BF16))
            put_row(m_ref, h, m1)
        pv = [jnp.dot(vth_ref[h, n], ps[h], preferred_element_type=F32) for h in range(2)]
        for h in range(2):
            acc_ref[h] = alphas[h] * acc_ref[h] + pv[h]

    own_raw = scores(own)
    lane_f = lane.astype(F32)
    bound_c = ATTN_SCALE * log2e * 1.02
    skips = []
    for h in range(2):
        q_m = jnp.where(halves[h], q, 0.0)
        qn2 = jnp.max(jnp.sum(q_m * q_m, axis=-1, keepdims=True), axis=0, keepdims=True)
        ub = jnp.sqrt(qn2 * knorm_ref[h][0:1, :]) * bound_c
        ub_own = jnp.sum(jnp.where(lane == own, ub, 0.0), axis=-1, keepdims=True)
        far = slope2[h] * ((lane_f - own.astype(F32)) * B + (B - 1))
        skips.append(ub + far < -ub_own - 160.0)
    keep = jnp.logical_and(lane < own, jnp.logical_not(jnp.logical_and(skips[0], skips[1])))
    first = jnp.min(jnp.where(keep, lane_f, own.astype(F32))).astype(jnp.int32)
    n_visit = own - first

    put_scores(jnp.minimum(first, nb - 1), raw_a)

    own_p = []
    for h in range(2):
        s = jnp.where(rel <= 0, own_raw[h], NEG_INF)
        m = jnp.max(s, axis=0, keepdims=True)
        put_row(m_ref, h, m)
        own_p.append(jnp.exp2(s - m).astype(BF16))
    for h in range(2):
        acc_ref[h] = jnp.dot(vth_ref[h, own], own_p[h], preferred_element_type=F32)

    def body(j, carry):
        n0 = first + 2 * j
        put_scores(n0 + 1, raw_b)
        attend(n0, raw_a)
        put_scores(jnp.minimum(n0 + 2, nb - 1), raw_a)
        attend(n0 + 1, raw_b)
        return carry

    lax.fori_loop(0, n_visit // 2, body, 0)

    @pl.when(n_visit % 2 == 1)
    def _():
        attend(own - 1, raw_a)

    out_t = jnp.concatenate([acc_ref[h][0:HALF] / acc_ref[h][HALF:HALF + 1] for h in range(2)], axis=0)
    o_ref[0, tile_rows, :] = out_t.T


def _moba_prompt(proj3, slopes_pair):
    bsz, seq, _ = proj3.shape
    assert seq % MOBA_BLOCK == 0 and seq // MOBA_BLOCK >= MOBA_TOPK
    nb = seq // MOBA_BLOCK
    nbp = -(-nb // SUBLANES) * SUBLANES
    nq = seq // Q_BLOCK
    qc, kc, vc = C_Q // LANES, C_K // LANES, C_V // LANES
    return pl.pallas_call(
        functools.partial(_moba_prompt_kernel, nb=nb, nq=nq),
        grid=(bsz, N_PAIRS),
        in_specs=[pl.BlockSpec((1, seq, LANES), lambda b, p: (b, 0, qc + p)),
                  pl.BlockSpec((1, seq, LANES), lambda b, p: (b, 0, kc + p)),
                  pl.BlockSpec((1, seq, LANES), lambda b, p: (b, 0, vc + p)),
                  pl.BlockSpec((1, 1, LANES), lambda b, p: (p, 0, 0))],
        out_specs=[pl.BlockSpec((1, seq, LANES), lambda b, p: (b, 0, p)),
                   pl.BlockSpec((1, LANES, seq), lambda b, p: (b, p, 0)),
                   pl.BlockSpec((1, LANES, seq), lambda b, p: (b, p, 0))],
        out_shape=[jax.ShapeDtypeStruct((bsz, seq, A_W), F32),
                   jax.ShapeDtypeStruct((bsz, A_W, seq), F32), jax.ShapeDtypeStruct((bsz, A_W, seq), F32)],
        scratch_shapes=[pltpu.VMEM((seq, 2 * LANES), BF16),
                        pltpu.VMEM((2, nb, HALF + SUBLANES, MOBA_BLOCK), BF16),
                        pltpu.VMEM((nbp, LANES), F32), pltpu.VMEM((2, nb, SUBLANES, Q_BLOCK), F32),
                        pltpu.VMEM((2, HALF + SUBLANES, Q_BLOCK), F32),
                        pltpu.VMEM((2, MOBA_BLOCK, Q_BLOCK), F32), pltpu.VMEM((2, MOBA_BLOCK, Q_BLOCK), F32),
                        pltpu.VMEM((2, SUBLANES, Q_BLOCK), F32), pltpu.VMEM((2, SUBLANES, LANES), F32)],
        compiler_params=_cparams(("parallel", "parallel")),
        name="moba_prompt",
    )(proj3, proj3, proj3, slopes_pair)


def _moba_sample_kernel(pt_ref, q_ref, kn_ref, vn_ref, *refs, n_blk, ppb, page, tnew):
    del pt_ref
    n_pages = n_blk * ppb
    k_refs, v_refs = refs[:n_pages], refs[n_pages:2 * n_pages]
    o_ref, m_ref, l_ref, g_ref, acc_ref = refs[2 * n_pages:]
    past = n_blk * MOBA_BLOCK
    rows = A_HEADS * SUBLANES
    lane_head = lax.shift_right_logical(lax.broadcasted_iota(jnp.int32, (SUBLANES, A_W), 1), HALF_SHIFT)
    q8 = q_ref[0] * ATTN_SCALE
    qbd = jnp.concatenate([jnp.where(lane_head == h, q8, 0.0) for h in range(A_HEADS)], axis=0)
    q_hi = qbd.astype(BF16)
    q_lo = (qbd - q_hi.astype(F32)).astype(BF16)
    r = lax.broadcasted_iota(jnp.int32, (rows, 1), 0)
    tok = jnp.bitwise_and(r, SUBLANES - 1).astype(F32)
    slope = jnp.exp2(-(8.0 / A_HEADS) * (lax.shift_right_logical(r, SUBLANE_SHIFT) + 1).astype(F32))
    key = lax.broadcasted_iota(jnp.int32, (1, MOBA_BLOCK), 1).astype(F32)

    def block_t(refs_, n):
        return jnp.concatenate([refs_[j][0, 0].reshape(A_W, page) for j in range(n * ppb, (n + 1) * ppb)],
                               axis=1).astype(BF16)

    raws, fixes = [], []
    for n in range(n_blk):
        kt = block_t(k_refs, n)
        raws.append(jnp.dot(q_hi, kt, preferred_element_type=F32))
        fixes.append(jnp.dot(q_lo, kt, preferred_element_type=F32))
    es = []
    for n in range(n_blk):
        g_ref[n] = jnp.broadcast_to(jnp.sum(raws[n] + fixes[n], axis=-1, keepdims=True) * (1.0 / MOBA_BLOCK),
                                    (rows, LANES))
        s = raws[n] - slope * ((past - n * MOBA_BLOCK + tok) - key)
        m = jnp.max(s, axis=-1, keepdims=True)
        e = jnp.exp(s - m)
        m_ref[n] = jnp.broadcast_to(m, (rows, LANES))
        l_ref[n] = jnp.broadcast_to(jnp.sum(e, axis=-1, keepdims=True), (rows, LANES))
        es.append(e.astype(BF16))
    for n in range(n_blk):
        acc_ref[n] = lax.dot_general(es[n], block_t(v_refs, n), (((1,), (1,)), ((), ())),
                                     preferred_element_type=F32)

    lane = lax.broadcasted_iota(jnp.int32, (rows, LANES), 1)
    gate = jnp.zeros((rows, LANES), F32)
    for j in range(n_blk):
        gate = jnp.where(lane == j, g_ref[j], gate)
    bias = _topk_bias(gate, n_blk)

    kn = kn_ref[0]
    vn = vn_ref[0]
    s_own = []
    for j in range(tnew):
        sj = jnp.sum(qbd * kn[j:j + 1, :], axis=-1, keepdims=True) - slope * (tok - j)
        s_own.append(jnp.where(tok >= j, sj, NEG_INF))
    mx = s_own[0]
    for j in range(1, tnew):
        mx = jnp.maximum(mx, s_own[j])
    mb = []
    for j in range(n_blk):
        mj = m_ref[j][:, 0:1] + bias[:, j:j + 1]
        mb.append(mj)
        mx = jnp.maximum(mx, mj)
    lsum = jnp.zeros((rows, 1), F32)
    acc = jnp.zeros((rows, A_W), F32)
    for j in range(tnew):
        w = jnp.exp(s_own[j] - mx)
        lsum = lsum + w
        acc = acc + w * vn[j:j + 1, :]
    for j in range(n_blk):
        w = jnp.exp(mb[j] - mx)
        lsum = lsum + w * l_ref[j][:, 0:1]
        acc = acc + w * acc_ref[j]
    out = acc / lsum
    o8 = jnp.zeros((SUBLANES, A_W), F32)
    for h in range(A_HEADS):
        o8 = o8 + jnp.where(lane_head == h, out[h * SUBLANES:(h + 1) * SUBLANES, :], 0.0)
    o_ref[0] = o8


def _moba_sample(q8, k8, v8, cache_kt, cache_vt, page_table, layer, tnew):
    dbsz = q8.shape[0]
    n_pages = page_table.shape[1]
    page = cache_kt.shape[4]
    assert MOBA_BLOCK % page == 0 and (n_pages * page) % MOBA_BLOCK == 0 and page % LANES == 0
    ppb = MOBA_BLOCK // page
    n_blk = n_pages // ppb
    assert n_blk >= MOBA_TOPK and tnew <= SUBLANES
    rows = A_HEADS * SUBLANES
    tok_spec = pl.BlockSpec((1, SUBLANES, A_W), lambda b, pt: (b, 0, 0))

    def page_spec(j):
        return pl.BlockSpec((1, 1, A_HEADS, A_HD, page), lambda b, pt: (layer, pt[b * n_pages + j], 0, 0, 0))

    pages = [page_spec(j) for j in range(n_pages)]
    grid_spec = pltpu.PrefetchScalarGridSpec(
        num_scalar_prefetch=1,
        grid=(dbsz,),
        in_specs=[tok_spec, tok_spec, tok_spec] + pages + pages,
        out_specs=tok_spec,
        scratch_shapes=[pltpu.VMEM((n_blk, rows, LANES), F32), pltpu.VMEM((n_blk, rows, LANES), F32),
                        pltpu.VMEM((n_blk, rows, LANES), F32), pltpu.VMEM((n_blk, rows, A_W), F32)],
    )
    return pl.pallas_call(
        functools.partial(_moba_sample_kernel, n_blk=n_blk, ppb=ppb, page=page, tnew=tnew),
        grid_spec=grid_spec,
        out_shape=jax.ShapeDtypeStruct((dbsz, SUBLANES, A_W), F32),
        compiler_params=_cparams(("parallel",)),
        name="moba_sample",
    )(page_table.reshape(-1), q8, k8, v8, *([cache_kt] * n_pages), *([cache_vt] * n_pages))


def _ssd_prompt_kernel(*refs, L, n_sub):
    hout_ref, h_ref = refs[10], refs[12]

    def chunk(j, carry):
        _ssd_chunk(pl.program_id(1) * n_sub + j, pl.ds(pl.multiple_of(j * L, L), L), *refs[:10], *refs[11:], L=L)
        return carry

    lax.fori_loop(0, n_sub, chunk, 0)

    @pl.when(pl.program_id(1) == pl.num_programs(1) - 1)
    def _():
        hout_ref[0] = h_ref[...]


def _ssd_chunk(c, rows, z_ref, xbc_ref, dt_ref, cw_ref, cb_ref, dtb_ref, alog_ref, dvec_ref, ng_ref,
               o_ref, ext_ref, h_ref, *, L):
    P = SUBLANES

    @pl.when(c == 0)
    def _():
        ext_ref[0:P, :] = jnp.zeros((P, S_CONV_CH), F32)
        h_ref[...] = jnp.zeros_like(h_ref)

    @pl.when(c > 0)
    def _():
        ext_ref[0:P, :] = ext_ref[L:L + P, :]

    ext_ref[P:P + L, :] = xbc_ref[rows, :]
    acc = cb_ref[...] + cw_ref[S_CONV - 1:S_CONV, :] * ext_ref[P:P + L, :]
    for d in range(1, S_CONV):
        acc = acc + cw_ref[S_CONV - 1 - d:S_CONV - d, :] * ext_ref[P - d:P - d + L, :]
    xbc = _silu(acc)
    xs = xbc[:, :S_W]
    bm = xbc[:, S_W:S_W + S_GROUPS * S_STATE]
    cm = xbc[:, S_W + S_GROUPS * S_STATE:]

    lane = lax.broadcasted_iota(jnp.int32, (1, LANES), 1)
    is_a = lane < HALF
    dt = jnp.where(lane < S_HEADS, _softplus(dt_ref[rows, :] + dtb_ref[...]), 0.0)
    da = dt * (-jnp.exp(alog_ref[...]))
    rr = lax.broadcasted_iota(jnp.int32, (L, L), 0)
    cc = lax.broadcasted_iota(jnp.int32, (L, L), 1)
    causal = rr >= cc
    cum = _split_dot(da, jnp.where(causal, 1.0, 0.0), 3, x_is_lhs=False)
    cum_t = cum.T
    dt_t = dt.T
    row_a = lax.broadcasted_iota(jnp.int32, (2 * S_HD, 1), 0) < S_HD

    group_of = [(2 * pr * S_GROUPS) // S_HEADS for pr in range(N_PAIRS)]
    bgs = [bm[:, g * S_STATE:(g + 1) * S_STATE] for g in range(S_GROUPS)]
    cgs = [cm[:, g * S_STATE:(g + 1) * S_STATE] for g in range(S_GROUPS)]
    cb_g = [_dot_nt(cgs[g], bgs[g]) for g in range(S_GROUPS)]
    xs_ps = [xs[:, pr * LANES:(pr + 1) * LANES] for pr in range(N_PAIRS)]
    hps = [h_ref[pr] for pr in range(N_PAIRS)]
    halves = (is_a, jnp.logical_not(is_a))
    cum_c = [cum[:, h:h + 1] for h in range(S_HEADS)]
    last = [cum[L - 1:L, h:h + 1] for h in range(S_HEADS)]
    wts = []
    for h in range(S_HEADS):
        seg = cum_c[h] - cum_t[h:h + 1, :]
        dec = jnp.exp(jnp.where(causal, seg, -jnp.inf))
        wts.append(cb_g[group_of[h // 2]] * dec * dt_t[h:h + 1, :])
    intra = [_dot(wts[h], jnp.where(halves[h % 2], xs_ps[h // 2], 0.0)) for h in range(S_HEADS)]
    inter = [_dot_nt(cgs[group_of[pr]], hps[pr]) for pr in range(N_PAIRS)]
    ys = [intra[2 * pr] + intra[2 * pr + 1]
          + inter[pr] * jnp.where(is_a, jnp.exp(cum_c[2 * pr]), jnp.exp(cum_c[2 * pr + 1])) for pr in range(N_PAIRS)]
    tes = [jnp.where(is_a, jnp.exp(last[2 * pr] - cum_c[2 * pr]) * dt[:, 2 * pr:2 * pr + 1],
                     jnp.exp(last[2 * pr + 1] - cum_c[2 * pr + 1]) * dt[:, 2 * pr + 1:2 * pr + 2])
           for pr in range(N_PAIRS)]
    sts = [_dot_tn(xs_ps[pr] * tes[pr], bgs[group_of[pr]]) for pr in range(N_PAIRS)]
    for pr in range(N_PAIRS):
        h_ref[pr] = hps[pr] * jnp.where(row_a, jnp.exp(last[2 * pr]), jnp.exp(last[2 * pr + 1])) + sts[pr]
    y = jnp.concatenate(ys, axis=1) + dvec_ref[...] * xs
    yz = y * _silu(z_ref[rows, :])
    gw = S_W // S_GROUPS
    for g in range(S_GROUPS):
        part = yz[:, g * gw:(g + 1) * gw]
        ms = jnp.mean(part * part, axis=-1, keepdims=True)
        o_ref[rows, g * gw:(g + 1) * gw] = part * lax.rsqrt(ms + NORM_EPS) * ng_ref[:, g * gw:(g + 1) * gw]


def _ssd_prompt(proj, bsz, seq, cw, cb, dtb, alog, dvec, ng):
    L = SSD_CHUNK
    n_sub = 4 if seq % (4 * L) == 0 else 1
    nc = seq // (L * n_sub)
    rows = L * n_sub
    const = lambda shape: pl.BlockSpec(shape, lambda b, c: (0,) * len(shape))
    return pl.pallas_call(
        functools.partial(_ssd_prompt_kernel, L=L, n_sub=n_sub),
        grid=(bsz, nc),
        in_specs=[pl.BlockSpec((rows, S_W), lambda b, c: (b * nc + c, C_Z // S_W)),
                  pl.BlockSpec((rows, S_CONV_CH), lambda b, c: (b * nc + c, C_XBC // S_CONV_CH)),
                  pl.BlockSpec((rows, LANES), lambda b, c: (b * nc + c, C_DT // LANES)),
                  const((S_CONV, S_CONV_CH)), const((1, S_CONV_CH)), const((1, LANES)), const((1, LANES)),
                  const((1, S_W)), const((1, S_W))],
        out_specs=[pl.BlockSpec((rows, S_W), lambda b, c: (b * nc + c, 0)),
                   pl.BlockSpec((1, N_PAIRS, 2 * S_HD, S_STATE), lambda b, c: (b, 0, 0, 0))],
        out_shape=[jax.ShapeDtypeStruct((bsz * seq, S_W), F32),
                   jax.ShapeDtypeStruct((bsz, N_PAIRS, 2 * S_HD, S_STATE), F32)],
        scratch_shapes=[pltpu.VMEM((L + 2 * SUBLANES, S_CONV_CH), F32),
                        pltpu.VMEM((N_PAIRS, 2 * S_HD, S_STATE), F32)],
        compiler_params=_cparams(("parallel", "arbitrary")),
        name="ssd_prompt",
    )(proj, proj, proj, cw, cb, dtb, alog, dvec, ng)


def _ssd_sample_kernel(z_ref, xbc_ref, dt_ref, pre_ref, h0_ref, cw_ref, cb_ref, dtbv_ref, alogv_ref,
                       dvec_ref, ng_ref, o_ref, hout_ref, *, T, nb):
    rows2 = 2 * S_HD
    eye = (lax.broadcasted_iota(jnp.int32, (rows2, LANES), 0)
           == lax.broadcasted_iota(jnp.int32, (rows2, LANES), 1)).astype(F32)
    eye_all = jnp.concatenate([eye] * nb, axis=0)
    ones = jnp.ones((LANES, LANES), F32)
    spread = (lax.broadcasted_iota(jnp.int32, (LANES, S_W), 0)
              == lax.shift_right_logical(lax.broadcasted_iota(jnp.int32, (LANES, S_W), 1), HALF_SHIFT)).astype(F32)
    a_vec = -jnp.exp(alogv_ref[...])
    gw = S_W // S_GROUPS

    def as_columns(x, passes):
        return _split_dot(eye_all * _repeat_rows(x, rows2), ones, passes)

    up = [pre_ref[j] for j in range(S_CONV - 1)] + [xbc_ref[t] for t in range(T)]
    hs = [h0_ref[0, :, pr].reshape(nb * rows2, S_STATE) for pr in range(N_PAIRS)]
    for t in range(T):
        acc = cb_ref[...] + cw_ref[0:1, :] * up[t]
        for j in range(1, S_CONV):
            acc = acc + cw_ref[j:j + 1, :] * up[t + j]
        xc = _silu(acc)
        xs = xc[:, :S_W]
        dt = _softplus(_split_dot(dt_ref[t], spread, 3) + dtbv_ref[...])
        decay = jnp.exp(dt * a_vec)
        xdt = xs * dt
        ys = []
        for pr in range(N_PAIRS):
            g = (2 * pr * S_GROUPS) // S_HEADS
            sl = slice(pr * LANES, (pr + 1) * LANES)
            b_rows = _repeat_rows(xc[:, S_W + g * S_STATE:S_W + (g + 1) * S_STATE], rows2)
            c_rows = _repeat_rows(xc[:, S_W + (S_GROUPS + g) * S_STATE:S_W + (S_GROUPS + g + 1) * S_STATE], rows2)
            hs[pr] = hs[pr] * as_columns(decay[:, sl], 3) + as_columns(xdt[:, sl], 2) * b_rows
            y_col = _split_dot(hs[pr] * c_rows, ones, 2)
            ys.append(jnp.sum((eye_all * y_col).reshape(nb, rows2, LANES), axis=1))
        y = jnp.concatenate(ys, axis=1) + dvec_ref[...] * xs
        yz = y * _silu(z_ref[t])
        for g in range(S_GROUPS):
            part = yz[:, g * gw:(g + 1) * gw]
            ms = jnp.mean(part * part, axis=-1, keepdims=True)
            o_ref[t, :, g * gw:(g + 1) * gw] = part * lax.rsqrt(ms + NORM_EPS) * ng_ref[:, g * gw:(g + 1) * gw]
    for pr in range(N_PAIRS):
        hout_ref[:, pr] = hs[pr].reshape(nb, rows2, S_STATE)


def _ssd_sample(proj3, pre3, h0, layer, cw, cb, dtbv, alogv, dvec, ng, nb=8):
    T, dbsz = proj3.shape[:2]
    const = lambda shape: pl.BlockSpec(shape, lambda j: (0,) * len(shape))
    hspec = pl.BlockSpec((nb, N_PAIRS, 2 * S_HD, S_STATE), lambda j: (j, 0, 0, 0))
    h0spec = pl.BlockSpec((1, nb, N_PAIRS, 2 * S_HD, S_STATE), lambda j: (layer, j, 0, 0, 0))
    return pl.pallas_call(
        functools.partial(_ssd_sample_kernel, T=T, nb=nb),
        grid=(dbsz // nb,),
        in_specs=[pl.BlockSpec((T, nb, S_W), lambda j: (0, j, C_Z // S_W)),
                  pl.BlockSpec((T, nb, S_CONV_CH), lambda j: (0, j, C_XBC // S_CONV_CH)),
                  pl.BlockSpec((T, nb, LANES), lambda j: (0, j, C_DT // LANES)),
                  pl.BlockSpec((S_CONV - 1, nb, S_CONV_CH), lambda j: (0, j, 0)),
                  h0spec,
                  const((S_CONV, S_CONV_CH)), const((1, S_CONV_CH)), const((1, S_W)), const((1, S_W)),
                  const((1, S_W)), const((1, S_W))],
        out_specs=[pl.BlockSpec((T, nb, S_W), lambda j: (0, j, 0)), hspec],
        out_shape=[jax.ShapeDtypeStruct((T, dbsz, S_W), F32),
                   jax.ShapeDtypeStruct((dbsz, N_PAIRS, 2 * S_HD, S_STATE), F32)],
        compiler_params=_cparams(("parallel",)),
        name="ssd_sample",
    )(proj3, proj3, proj3, pre3, h0, cw, cb, dtbv, alogv, dvec, ng)


def _rwkv_prep_kernel(u_ref, pre_ref, mu_ref, w0_ref, w2_ref, a0_ref, a2_ref, g2_ref, kk_ref, ka_ref,
                      r_out, w_out, k_out, v_out, kk_out, kka_out, g_out, ext_ref,
                      *, tm, P, stride, tiles_per_seq):
    i = pl.program_id(0)
    first = (i % tiles_per_seq) == 0

    @pl.when(first)
    def _():
        ext_ref[0:P, :] = pre_ref[...]

    @pl.when(jnp.logical_not(first))
    def _():
        ext_ref[0:P, :] = ext_ref[tm:tm + P, :]

    u = u_ref[...]
    ext_ref[P:P + tm, :] = u
    prev = ext_ref[P - stride:P - stride + tm, :]
    x = u + (prev - u) * mu_ref[...]
    r = x[:, 0:R_W]
    kr = x[:, R_W:2 * R_W]
    vr = x[:, 2 * R_W:3 * R_W]
    xl = x[:, 3 * R_W:]
    w_log = w0_ref[...] + _dot(jnp.tanh(xl), w2_ref[...])
    log_decay = -jnp.exp(-_softplus(-w_log) - 0.5)
    a = _sigmoid(a0_ref[...] + _dot(xl, a2_ref[...]))
    g = _dot(_sigmoid(xl), g2_ref[...])
    kk = kr * kk_ref[...]
    ss = _split_dot(kk * kk, _head_ones(R_W, 1.0), 2)
    kk = kk / jnp.maximum(jnp.sqrt(ss), 1e-12)
    r_out[...] = r
    w_out[...] = log_decay
    k_out[...] = kr * (1.0 + (a - 1.0) * ka_ref[...])
    v_out[...] = vr
    kk_out[...] = kk
    kka_out[...] = kk * a
    g_out[...] = g


def _rwkv_prep(proj, pre, mu, w0, w2p, a0, a2p, g2p, kkw, kaw, tm, stride, tiles_per_seq):
    t = proj.shape[0]
    P = pre.shape[0]
    const = lambda shape: pl.BlockSpec(shape, lambda i: (0,) * len(shape))
    outs = pl.pallas_call(
        functools.partial(_rwkv_prep_kernel, tm=tm, P=P, stride=stride, tiles_per_seq=tiles_per_seq),
        grid=(t // tm,),
        in_specs=[pl.BlockSpec((tm, R_IN_W), lambda i: (i, 0)),
                  const((P, R_IN_W)), const((1, R_IN_W)), const((1, R_W)), const((R_LORA, R_W)),
                  const((1, R_W)), const((R_LORA, R_W)), const((R_LORA, R_W)), const((1, R_W)), const((1, R_W))],
        out_specs=[pl.BlockSpec((tm, R_W), lambda i: (i, 0))] * 7,
        out_shape=[jax.ShapeDtypeStruct((t, R_W), F32)] * 7,
        scratch_shapes=[pltpu.VMEM((tm + 2 * P, R_IN_W), F32)],
        compiler_params=_cparams(("arbitrary",)),
        name="rwkv_prep",
    )(proj, pre, mu, w0, w2p, a0, a2p, g2p, kkw, kaw)
    return outs


def _pair_consts():
    lane = lax.broadcasted_iota(jnp.int32, (R_HD, LANES), 1)
    row = lax.broadcasted_iota(jnp.int32, (R_HD, LANES), 0)
    is_a = lane < HALF
    eye2 = (jnp.bitwise_and(lane, HALF - 1) == row).astype(F32)
    return is_a, eye2


def _unit_lower_inverses(ns):
    L = ns[0].shape[0]
    eye = (lax.broadcasted_iota(jnp.int32, (L, L), 0) == lax.broadcasted_iota(jnp.int32, (L, L), 1)).astype(F32)
    ts = [eye + n for n in ns]
    pws = [_dot(n, n) for n in ns]
    for _ in range(int(math.log2(L)) - 2):
        both = [_dot(jnp.concatenate([t, p], axis=0), p) for t, p in zip(ts, pws)]
        ts = [t + b[:L] for t, b in zip(ts, both)]
        pws = [b[L:] for b in both]
    return [t + _dot(t, p) for t, p in zip(ts, pws)]


def _rwkv_chunk_prompt_kernel(*refs, L, n_sub):
    sout_ref, s_ref = refs[7], refs[8]

    @pl.when(pl.program_id(1) == 0)
    def _():
        s_ref[...] = jnp.zeros_like(s_ref)

    def chunk(j, carry):
        _rwkv_chunk(pl.ds(pl.multiple_of(j * L, L), L), *refs[:7], s_ref, L=L)
        return carry

    lax.fori_loop(0, n_sub, chunk, 0)

    @pl.when(pl.program_id(1) == pl.num_programs(1) - 1)
    def _():
        sout_ref[0] = s_ref[...]


def _rwkv_chunk(rows, r_ref, lw_ref, k_ref, v_ref, kk_ref, kka_ref, y_ref, s_ref, *, L):
    lane = lax.broadcasted_iota(jnp.int32, (1, LANES), 1)
    is_a = lane < HALF
    rr = lax.broadcasted_iota(jnp.int32, (L, L), 0)
    cc = lax.broadcasted_iota(jnp.int32, (L, L), 1)
    incl = rr >= cc
    strict = rr > cc
    tri = jnp.where(incl, 1.0, 0.0).astype(F32)
    r2 = lax.broadcasted_iota(jnp.int32, (LANES, LANES), 0) < HALF
    c2 = lax.broadcasted_iota(jnp.int32, (LANES, LANES), 1) < HALF
    same_head = r2 == c2

    r, lw, k, v, kk, kka = (ref[rows, :] for ref in (r_ref, lw_ref, k_ref, v_ref, kk_ref, kka_ref))
    cum = _split_dot(lw, tri, 3, x_is_lhs=False)
    last = cum[L - 1:L, :]
    inv_p = jnp.exp(-cum)
    to_end = jnp.exp(last - cum)
    b_t = kk * jnp.exp(cum - lw)
    a_t = -kka * inv_p
    k_t = k * inv_p
    r_t = r * jnp.exp(cum)
    a_end = -kka * to_end
    k_end = k * to_end
    decay_end = jnp.exp(last)
    pairs = [slice(pr * LANES, (pr + 1) * LANES) for pr in range(N_PAIRS)]
    halves = (is_a, jnp.logical_not(is_a))
    s0 = [s_ref[pr] for pr in range(N_PAIRS)]

    ns, mks, rak = [], [], []
    for sl in pairs:
        cols = jnp.concatenate([a_t[:, sl], k_t[:, sl]], axis=0)
        for half in halves:
            lhs = jnp.concatenate([jnp.where(half, b_t[:, sl], 0.0), jnp.where(half, r_t[:, sl], 0.0)], axis=0)
            g4 = _dot_nt(lhs, cols)
            ns.append(jnp.where(strict, g4[:L, :L], 0.0))
            mks.append(jnp.where(strict, g4[:L, L:], 0.0))
            rak.append(jnp.concatenate([jnp.where(incl, g4[L:, :L], 0.0), jnp.where(incl, g4[L:, L:], 0.0)], axis=1))
    ts = _unit_lower_inverses(ns)
    from_state = [_dot_nt(jnp.concatenate([b_t[:, sl], r_t[:, sl]], axis=0), s0[pr]) for pr, sl in enumerate(pairs)]
    rhs = []
    for pr, sl in enumerate(pairs):
        mv = _dot(jnp.concatenate([mks[2 * pr], mks[2 * pr + 1]], axis=0), v[:, sl])
        rhs.append(from_state[pr][:L] + jnp.where(is_a, mv[:L], mv[L:]))
    us = []
    for pr in range(N_PAIRS):
        tu = _dot(jnp.concatenate([ts[2 * pr], ts[2 * pr + 1]], axis=0), rhs[pr])
        us.append(jnp.where(is_a, tu[:L], tu[L:]))
    for pr, sl in enumerate(pairs):
        uv = jnp.concatenate([us[pr], v[:, sl]], axis=0)
        y_ref[rows, sl] = from_state[pr][L:] + jnp.where(is_a, _dot(rak[2 * pr], uv), _dot(rak[2 * pr + 1], uv))
    for pr, sl in enumerate(pairs):
        upd = _dot_tn(jnp.concatenate([us[pr], v[:, sl]], axis=0),
                      jnp.concatenate([a_end[:, sl], k_end[:, sl]], axis=0))
        s_ref[pr] = s0[pr] * decay_end[:, sl] + jnp.where(same_head, upd, 0.0)


def _rwkv_scan_prompt(seqs, bsz, seq, L):
    n_sub = 4 if seq % (4 * L) == 0 else 1
    nc = seq // (L * n_sub)
    spec = pl.BlockSpec((L * n_sub, R_W), lambda b, c: (b * nc + c, 0))
    return pl.pallas_call(
        functools.partial(_rwkv_chunk_prompt_kernel, L=L, n_sub=n_sub),
        grid=(bsz, nc),
        in_specs=[spec] * 6,
        out_specs=[spec, pl.BlockSpec((1, N_PAIRS, LANES, LANES), lambda b, c: (b, 0, 0, 0))],
        out_shape=[jax.ShapeDtypeStruct((bsz * seq, R_W), F32),
                   jax.ShapeDtypeStruct((bsz, N_PAIRS, LANES, LANES), F32)],
        scratch_shapes=[pltpu.VMEM((N_PAIRS, LANES, LANES), F32)],
        compiler_params=_cparams(("parallel", "arbitrary")),
        name="rwkv_scan_prompt",
    )(*seqs)


def _repeat_rows(x, reps):
    return jnp.concatenate([jnp.broadcast_to(x[b:b + 1, :], (reps, x.shape[1])) for b in range(x.shape[0])], axis=0)


def _rwkv_scan_sample_kernel(r_ref, w_ref, k_ref, v_ref, kk_ref, kka_ref, s0_ref, y_ref, sout_ref, *, T, nb):
    _, eye2 = _pair_consts()
    eye_all = jnp.concatenate([eye2] * nb, axis=0)
    ones_bd = _head_ones(LANES, 1.0)

    def head_sums(x):
        return _split_dot(x, ones_bd, 2)

    states = [s0_ref[0, :, pr].reshape(nb * R_HD, LANES) for pr in range(N_PAIRS)]
    for t in range(T):
        for pr in range(N_PAIRS):
            sl = slice(pr * LANES, (pr + 1) * LANES)
            rr, ww, kr, vv, kk, kka = (_repeat_rows(x, R_HD) for x in (
                r_ref[t, :, sl], jnp.exp(w_ref[t, :, sl]), k_ref[t, :, sl], v_ref[t, :, sl],
                kk_ref[t, :, sl], kka_ref[t, :, sl]))
            v_col = head_sums(eye_all * vv)
            s = states[pr]
            s = s * ww - head_sums(s * kk) * kka + v_col * kr
            states[pr] = s
            y_col = head_sums(s * rr)
            y_ref[t, :, sl] = jnp.sum((eye_all * y_col).reshape(nb, R_HD, LANES), axis=1)
    for pr in range(N_PAIRS):
        sout_ref[:, pr] = states[pr].reshape(nb, R_HD, LANES)


def _rwkv_scan_sample(seqs3, s0, layer, nb=16):
    T, dbsz = seqs3[0].shape[:2]
    spec = pl.BlockSpec((T, nb, R_W), lambda j: (0, j, 0))
    sspec = pl.BlockSpec((nb, N_PAIRS, R_HD, LANES), lambda j: (j, 0, 0, 0))
    s0spec = pl.BlockSpec((1, nb, N_PAIRS, R_HD, LANES), lambda j: (layer, j, 0, 0, 0))
    return pl.pallas_call(
        functools.partial(_rwkv_scan_sample_kernel, T=T, nb=nb),
        grid=(dbsz // nb,),
        in_specs=[spec] * 6 + [s0spec],
        out_specs=[spec, sspec],
        out_shape=[jax.ShapeDtypeStruct((T, dbsz, R_W), F32),
                   jax.ShapeDtypeStruct((dbsz, N_PAIRS, R_HD, LANES), F32)],
        compiler_params=_cparams(("parallel",)),
        name="rwkv_scan_sample",
    )(*seqs3, s0)


def _merge_kernel(x_ref, oa_ref, ob_ref, yr_ref, r_ref, k_ref, v_ref, g_ref, ga_ref, gb_ref, gc_ref,
                  bg_ref, lng_ref, lnb_ref, rk_ref, wpa_ref, wpb_ref, wpc_ref, wo_ref, o_ref):
    pa = _dot(oa_ref[...], wpa_ref[...])
    pb = _dot(ob_ref[...], wpb_ref[...])
    mean_m = _head_ones(R_W, 1.0 / R_HD)
    yr = yr_ref[...]
    bonus = _split_dot(r_ref[...] * k_ref[...] * rk_ref[...], _head_ones(R_W, 1.0), 2)
    d = yr - _split_dot(yr, mean_m, 2)
    var = _split_dot(d * d, mean_m, 2)
    yn = d * lax.rsqrt(var + GN_EPS) * lng_ref[...] + lnb_ref[...]
    oc = (yn + bonus * v_ref[...]) * g_ref[...]
    merged = (_sigmoid(ga_ref[...] + bg_ref[:, 0:D_MODEL]) * pa
              + _sigmoid(gb_ref[...] + bg_ref[:, D_MODEL:2 * D_MODEL]) * pb
              + _sigmoid(gc_ref[...] + bg_ref[:, 2 * D_MODEL:]) * _dot(oc, wpc_ref[...]))
    o_ref[...] = x_ref[...] + _dot(merged, wo_ref[...])


def _merge(x, proj, oa, ob, yr, r, k2, v, g, bg, lng, lnb, rk, wpa, wpb, wpc, wo, tm):
    t = x.shape[0]
    row = lambda w: pl.BlockSpec((tm, w), lambda i: (i, 0))
    const = lambda shape: pl.BlockSpec(shape, lambda i: (0,) * len(shape))
    gcol = C_GATE // D_MODEL
    gate = lambda j: pl.BlockSpec((tm, D_MODEL), lambda i: (i, gcol + j))
    return pl.pallas_call(
        _merge_kernel,
        grid=(t // tm,),
        in_specs=[row(D_MODEL), row(A_W), row(S_W), row(R_W), row(R_W), row(R_W), row(R_W), row(R_W),
                  gate(0), gate(1), gate(2),
                  const((1, 3 * D_MODEL)), const((1, R_W)), const((1, R_W)), const((1, R_W)),
                  const((A_W, D_MODEL)), const((S_W, D_MODEL)), const((R_W, D_MODEL)),
                  const((D_MODEL, D_MODEL))],
        out_specs=row(D_MODEL),
        out_shape=jax.ShapeDtypeStruct((t, D_MODEL), F32),
        compiler_params=_cparams(("parallel",)),
        name="merge",
    )(x, oa, ob, yr, r, k2, v, g, proj, proj, proj, bg, lng, lnb, rk, wpa, wpb, wpc, wo)


def _ffn_kernel(x_ref, g_ref, wug_ref, wuv_ref, wd_ref, cw_ref, cb_ref, pre_ref, gf_ref,
                o_ref, tail_ref, xn_ref, ext_ref, *, tm, P, stride, tiles_per_seq, final_norm):
    i = pl.program_id(0)
    f = pl.program_id(1)
    first = (i % tiles_per_seq) == 0

    @pl.when(f == 0)
    def _():
        x = x_ref[...]
        ms = jnp.mean(x * x, axis=-1, keepdims=True)
        xn_ref[...] = (x * lax.rsqrt(ms + NORM_EPS) * g_ref[...]).astype(BF16)

    @pl.when(first)
    def _():
        ext_ref[f, 0:P, :] = pre_ref[...]

    @pl.when(jnp.logical_not(first))
    def _():
        ext_ref[f, 0:P, :] = ext_ref[f, tm:tm + P, :]

    xn = xn_ref[...]
    ug = jnp.dot(xn, wug_ref[...], preferred_element_type=F32)
    uv = jnp.dot(xn, wuv_ref[...], preferred_element_type=F32)
    ext_ref[f, P:P + tm, :] = ug
    tail_ref[0] = ext_ref[f, tm:tm + P, :]
    acc = cb_ref[...] + cw_ref[F_CONV - 1:F_CONV, :] * ug
    for d in range(1, F_CONV):
        acc = acc + cw_ref[F_CONV - 1 - d:F_CONV - d, :] * ext_ref[f, P - d * stride:P - d * stride + tm, :]
    contrib = _dot(_silu(acc) * uv, wd_ref[...])

    @pl.when(f == 0)
    def _():
        o_ref[...] = x_ref[...] + contrib

    @pl.when(f > 0)
    def _():
        o_ref[...] = o_ref[...] + contrib

    if final_norm:
        @pl.when(f == pl.num_programs(1) - 1)
        def _():
            y = o_ref[...]
            ms = jnp.mean(y * y, axis=-1, keepdims=True)
            o_ref[...] = y * lax.rsqrt(ms + NORM_EPS) * gf_ref[...]


def _ffn(x, g, wup, wd, cw, cb, pre, gf, tm, tf, stride, tiles_per_seq, final_norm):
    t = x.shape[0]
    P = pre.shape[0]
    nf = D_FF // tf
    return pl.pallas_call(
        functools.partial(_ffn_kernel, tm=tm, P=P, stride=stride, tiles_per_seq=tiles_per_seq,
                          final_norm=final_norm),
        grid=(t // tm, nf),
        in_specs=[pl.BlockSpec((tm, D_MODEL), lambda i, f: (i, 0)),
                  pl.BlockSpec((1, D_MODEL), lambda i, f: (0, 0)),
                  pl.BlockSpec((D_MODEL, tf), lambda i, f: (0, f)),
                  pl.BlockSpec((D_MODEL, tf), lambda i, f: (0, nf + f)),
                  pl.BlockSpec((tf, D_MODEL), lambda i, f: (f, 0)),
                  pl.BlockSpec((F_CONV, tf), lambda i, f: (0, f)),
                  pl.BlockSpec((1, tf), lambda i, f: (0, f)),
                  pl.BlockSpec((P, tf), lambda i, f: (0, f)),
                  pl.BlockSpec((1, D_MODEL), lambda i, f: (0, 0))],
        out_specs=[pl.BlockSpec((tm, D_MODEL), lambda i, f: (i, 0)),
                   pl.BlockSpec((1, P, tf), lambda i, f: (i, 0, f))],
        out_shape=[jax.ShapeDtypeStruct((t, D_MODEL), F32),
                   jax.ShapeDtypeStruct((t // tm, P, D_FF), F32)],
        scratch_shapes=[pltpu.VMEM((tm, D_MODEL), BF16), pltpu.VMEM((nf, tm + 2 * P, tf), F32)],
        compiler_params=_cparams(("arbitrary", "arbitrary")),
        name="conv_ffn",
    )(x, g, wup, wup, wd, cw, cb, pre, gf)


def _pack_rwkv_state(s):
    d, n = s.shape[:2]
    return s.reshape(d, n, N_PAIRS, 2, R_HD, R_HD).transpose(0, 1, 2, 4, 3, 5).reshape(d, n, N_PAIRS, R_HD, LANES)


def _unpack_rwkv_state(s):
    n = s.shape[0]
    return s.reshape(n, N_PAIRS, R_HD, 2, R_HD).transpose(0, 1, 3, 2, 4).reshape(n, R_HEADS, R_HD, R_HD)


def _unpack_rwkv_blockdiag(s):
    n = s.shape[0]
    return jnp.stack([s[:, :, :R_HD, :R_HD], s[:, :, R_HD:, R_HD:]], axis=2).reshape(n, R_HEADS, R_HD, R_HD)


def _prep_layer_params(l, p):
    w_in = p['w_in'][l]
    c_dt_src = 3 * A_W + S_W + S_CONV_CH
    c_rw_src = c_dt_src + S_HEADS
    c_gate_src = c_rw_src + R_IN_W
    w_proj = jnp.concatenate([
        w_in[:, c_rw_src:c_gate_src],
        w_in[:, c_dt_src:c_rw_src], jnp.zeros((D_MODEL, C_Q - C_DT - S_HEADS), F32),
        w_in[:, :c_dt_src],
        w_in[:, c_gate_src:]], axis=1).astype(BF16)
    pad_lane = lambda v: jnp.pad(v, (0, LANES - v.shape[0])).reshape(1, LANES)
    zl = lambda r0, w: jnp.zeros((R_LORA, R_W), F32).at[r0:r0 + w.shape[0]].set(w).astype(BF16)
    return dict(
        norm1_g=p['norm1_g'][l].reshape(1, D_MODEL), w_proj=w_proj,
        b_gate=p['b_gate'][l].reshape(1, 3 * D_MODEL),
        w_pa=p['w_pa'][l].astype(BF16), w_pb=p['w_pb'][l].astype(BF16), w_pc=p['w_pc'][l].astype(BF16),
        w_o=p['w_o'][l].astype(BF16),
        ssm_conv_w=p['ssm_conv_w'][l], ssm_conv_b=p['ssm_conv_b'][l].reshape(1, S_CONV_CH),
        ssm_dt_bias=pad_lane(p['ssm_dt_bias'][l]), ssm_a_log=pad_lane(p['ssm_a_log'][l]),
        ssm_dvec=jnp.repeat(p['ssm_d'][l], S_HD).reshape(1, S_W),
        ssm_dtb_vec=jnp.repeat(p['ssm_dt_bias'][l], S_HD).reshape(1, S_W),
        ssm_alog_vec=jnp.repeat(p['ssm_a_log'][l], S_HD).reshape(1, S_W),
        ssm_norm_g=p['ssm_norm_g'][l].reshape(1, S_W),
        rw_mu=p['rw_mu'][l].reshape(1, R_IN_W), rw_w0=p['rw_w0'][l].reshape(1, R_W),
        rw_w2p=zl(0, p['rw_w2'][l]), rw_a0=p['rw_a0'][l].reshape(1, R_W),
        rw_a2p=zl(R_LORA_W, p['rw_a2'][l]), rw_g2p=zl(R_LORA_W + R_LORA_A, p['rw_g2'][l]),
        rw_kk=p['rw_kk'][l].reshape(1, R_W), rw_ka=p['rw_ka'][l].reshape(1, R_W),
        rw_rk=p['rw_rk'][l].reshape(1, R_W), rw_ln_g=p['rw_ln_g'][l].reshape(1, R_W),
        rw_ln_b=p['rw_ln_b'][l].reshape(1, R_W),
        norm2_g=p['norm2_g'][l].reshape(1, D_MODEL), w_up=p['w_up'][l].astype(BF16),
        w_down=p['w_down'][l].astype(BF16), ffn_conv_w=p['ffn_conv_w'][l],
        ffn_conv_b=p['ffn_conv_b'][l].reshape(1, D_FF))


def _row_tile(t):
    for tm in (512, 256, 128):
        if t % tm == 0:
            return tm
    raise ValueError(t)


def _ffn_tf():
    return D_FF // 2


def _prompt_layer(x, lp, bsz, seq, slopes_pair, gf, final_norm):
    t = bsz * seq
    tm = _row_tile(seq)
    tiles = seq // tm
    proj = _rms_matmul(x, lp['norm1_g'], lp['w_proj'], 1024 if t % 1024 == 0 else tm, 2048)
    proj3 = proj.reshape(bsz, seq, N_PROJ)
    oa, k_t, v_t = _moba_prompt(proj3, slopes_pair)
    oa = oa.reshape(t, A_W)
    ob, ssm_new = _ssd_prompt(proj, bsz, seq, lp['ssm_conv_w'], lp['ssm_conv_b'], lp['ssm_dt_bias'],
                              lp['ssm_a_log'], lp['ssm_dvec'], lp['ssm_norm_g'])
    r, w, k2, v, kk, kka, g = _rwkv_prep(
        proj, jnp.zeros((SUBLANES, R_IN_W), F32), lp['rw_mu'], lp['rw_w0'], lp['rw_w2p'], lp['rw_a0'],
        lp['rw_a2p'], lp['rw_g2p'], lp['rw_kk'], lp['rw_ka'], tm, 1, tiles)
    yr, rw_new = _rwkv_scan_prompt((r, w, k2, v, kk, kka), bsz, seq, LANES)
    x = _merge(x, proj, oa, ob, yr, r, k2, v, g, lp['b_gate'], lp['rw_ln_g'], lp['rw_ln_b'], lp['rw_rk'],
               lp['w_pa'], lp['w_pb'], lp['w_pc'], lp['w_o'], min(tm, 256))
    x, tail = _ffn(x, lp['norm2_g'], lp['w_up'], lp['w_down'], lp['ffn_conv_w'], lp['ffn_conv_b'],
                   jnp.zeros((SUBLANES, D_FF), F32), gf, tm, _ffn_tf(), 1, tiles, final_norm)
    k_new = k_t.reshape(bsz, A_HEADS, A_HD, seq).transpose(0, 3, 1, 2)
    v_new = v_t.reshape(bsz, A_HEADS, A_HD, seq).transpose(0, 3, 1, 2)
    ssm_conv_new = proj3[:, seq - (S_CONV - 1):, C_XBC:C_XBC + S_CONV_CH]
    shift_new = proj3[:, seq - 1:, C_RW:C_RW + R_IN_W]
    ffn_conv_new = tail.reshape(bsz, tiles, SUBLANES, D_FF)[:, tiles - 1, SUBLANES - (F_CONV - 1):]
    state = (k_new, v_new, ssm_new.reshape(bsz, S_HEADS, S_HD, S_STATE), ssm_conv_new,
             _unpack_rwkv_blockdiag(rw_new), shift_new, ffn_conv_new)
    return x, state


def _sample_layer(x, lp, dbsz, tnew, st, cache_k4, cache_v4, page_table, layer, gf, final_norm):
    t = tnew * dbsz
    ssm0, ssm_conv0, rwkv0, shift0, ffn_conv0 = st
    proj = _rms_matmul(x, lp['norm1_g'], lp['w_proj'], t, 1024)
    proj3 = proj.reshape(tnew, dbsz, N_PROJ)

    qkv = proj3[:, :, C_Q:C_Q + 3 * A_W].transpose(1, 0, 2)
    qkv8 = jnp.pad(qkv, ((0, 0), (0, SUBLANES - tnew), (0, 0)))
    oa8 = _moba_sample(qkv8[:, :, :A_W], qkv8[:, :, A_W:2 * A_W], qkv8[:, :, 2 * A_W:],
                       cache_k4, cache_v4, page_table, layer, tnew)
    oa = oa8[:, :tnew].transpose(1, 0, 2).reshape(t, A_W)

    h0 = ssm0.reshape(-1, dbsz, N_PAIRS, 2 * S_HD, S_STATE)
    ob3, ssm_new = _ssd_sample(proj3, ssm_conv0.transpose(1, 0, 2), h0, layer, lp['ssm_conv_w'],
                               lp['ssm_conv_b'], lp['ssm_dtb_vec'], lp['ssm_alog_vec'], lp['ssm_dvec'],
                               lp['ssm_norm_g'])
    ob = ob3.reshape(t, S_W)
    ssm_new = ssm_new.reshape(dbsz, S_HEADS, S_HD, S_STATE)

    r, w, k2, v, kk, kka, g = _rwkv_prep(
        proj, shift0.reshape(dbsz, R_IN_W), lp['rw_mu'], lp['rw_w0'], lp['rw_w2p'], lp['rw_a0'],
        lp['rw_a2p'], lp['rw_g2p'], lp['rw_kk'], lp['rw_ka'], t, dbsz, 1)
    to3 = lambda a: a.reshape(tnew, dbsz, R_W)
    yr3, rw_new = _rwkv_scan_sample(tuple(to3(a) for a in (r, w, k2, v, kk, kka)), rwkv0, layer)
    yr = yr3.reshape(t, R_W)

    x = _merge(x, proj, oa, ob, yr, r, k2, v, g, lp['b_gate'], lp['rw_ln_g'], lp['rw_ln_b'], lp['rw_rk'],
               lp['w_pa'], lp['w_pb'], lp['w_pc'], lp['w_o'], min(t, 256))
    pre = ffn_conv0.transpose(1, 0, 2).reshape((F_CONV - 1) * dbsz, D_FF)
    x, tail = _ffn(x, lp['norm2_g'], lp['w_up'], lp['w_down'], lp['ffn_conv_w'], lp['ffn_conv_b'],
                   pre, gf, t, _ffn_tf(), dbsz, 1, final_norm)

    k_new = qkv[:, :, A_W:2 * A_W].reshape(dbsz, tnew, A_HEADS, A_HD)
    v_new = qkv[:, :, 2 * A_W:].reshape(dbsz, tnew, A_HEADS, A_HD)
    ssm_conv_new = proj3[tnew - (S_CONV - 1):, :, C_XBC:C_XBC + S_CONV_CH].transpose(1, 0, 2)
    shift_new = proj3[tnew - 1:, :, C_RW:C_RW + R_IN_W].transpose(1, 0, 2)
    ffn_conv_new = tail.reshape(F_CONV - 1, dbsz, D_FF).transpose(1, 0, 2)
    state = (k_new, v_new, ssm_new, ssm_conv_new, _unpack_rwkv_state(rw_new), shift_new, ffn_conv_new)
    return x, state


def kernel(x_prompt, x_sample, cache_k, cache_v, state_ssm, state_ssm_conv, state_rwkv, state_rwkv_shift, state_ffn_conv, page_table, norm1_g, w_in, b_gate, w_pa, ssm_conv_w, ssm_conv_b, ssm_dt_bias, ssm_a_log, ssm_d, ssm_norm_g, w_pb, rw_mu, rw_w0, rw_w2, rw_a0, rw_a2, rw_g2, rw_kk, rw_ka, rw_rk, rw_ln_g, rw_ln_b, w_pc, w_o, norm2_g, w_up, ffn_conv_w, ffn_conv_b, w_down, norm_f_g):
    params = dict(norm1_g=norm1_g, w_in=w_in, b_gate=b_gate, w_pa=w_pa, ssm_conv_w=ssm_conv_w,
                  ssm_conv_b=ssm_conv_b, ssm_dt_bias=ssm_dt_bias, ssm_a_log=ssm_a_log, ssm_d=ssm_d,
                  ssm_norm_g=ssm_norm_g, w_pb=w_pb, rw_mu=rw_mu, rw_w0=rw_w0, rw_w2=rw_w2, rw_a0=rw_a0,
                  rw_a2=rw_a2, rw_g2=rw_g2, rw_kk=rw_kk, rw_ka=rw_ka, rw_rk=rw_rk, rw_ln_g=rw_ln_g,
                  rw_ln_b=rw_ln_b, w_pc=w_pc, w_o=w_o, norm2_g=norm2_g, w_up=w_up, ffn_conv_w=ffn_conv_w,
                  ffn_conv_b=ffn_conv_b, w_down=w_down)
    depth = w_in.shape[0]
    bsz, seq, _ = x_prompt.shape
    dbsz, tnew, _ = x_sample.shape
    head = jnp.arange(A_HEADS, dtype=F32) + 1.0
    slopes = jnp.exp2(-8.0 * head / A_HEADS)
    slopes_pair = jnp.repeat(slopes, A_HD).reshape(N_PAIRS, 1, LANES)
    cache_k4 = cache_k.transpose(0, 1, 3, 4, 2)
    cache_v4 = cache_v.transpose(0, 1, 3, 4, 2)
    gf = norm_f_g.reshape(1, D_MODEL)

    hp = x_prompt.reshape(bsz * seq, D_MODEL)
    hs = x_sample.transpose(1, 0, 2).reshape(tnew * dbsz, D_MODEL)
    new_p = [[] for _ in range(7)]
    new_s = [[] for _ in range(7)]
    rwkv_packed = _pack_rwkv_state(state_rwkv)
    for l in range(depth):
        lp = _prep_layer_params(l, params)
        last = l == depth - 1
        hp, sp = _prompt_layer(hp, lp, bsz, seq, slopes_pair, gf, last)
        st = (state_ssm, state_ssm_conv[l], rwkv_packed, state_rwkv_shift[l], state_ffn_conv[l])
        hs, ss = _sample_layer(hs, lp, dbsz, tnew, st, cache_k4, cache_v4, page_table, l, gf, last)
        for j in range(7):
            new_p[j].append(sp[j])
            new_s[j].append(ss[j])
    y_prompt = hp.reshape(bsz, seq, D_MODEL)
    y_sample = hs.reshape(tnew, dbsz, D_MODEL).transpose(1, 0, 2)
    outs = [y_prompt, y_sample]
    for j in range(7):
        outs += [jnp.stack(new_p[j]), jnp.stack(new_s[j])]
    return tuple(outs)
```

```python
import functools
import math

import jax
import jax.numpy as jnp
from jax import lax
from jax.experimental import pallas as pl
from jax.experimental.pallas import tpu as pltpu

F32 = jnp.float32
BF16 = jnp.bfloat16

D_MODEL = 1024
A_HEADS = 8
A_HD = 64
A_W = A_HEADS * A_HD
MOBA_BLOCK = 256
MOBA_TOPK = 3
Q_BLOCK = MOBA_BLOCK
ATTN_SCALE = A_HD ** -0.5
S_HEADS = 8
S_HD = 64
S_W = S_HEADS * S_HD
S_GROUPS = 2
S_STATE = 128
S_CONV = 4
S_CONV_CH = S_W + 2 * S_GROUPS * S_STATE
SSD_CHUNK = 128
R_HEADS = 8
R_HD = 64
R_W = R_HEADS * R_HD
R_LORA_W = 64
R_LORA_A = 64
R_LORA_G = 128
R_LORA = R_LORA_W + R_LORA_A + R_LORA_G
R_IN_W = 3 * R_W + R_LORA
D_FF = ((8 * D_MODEL // 3 + 127) // 128) * 128
F_CONV = 3
NORM_EPS = 1e-6
GN_EPS = 64e-5
NEG_INF = -1e30

LANES = 128
SUBLANES = 8
HALF = 64
HALF_SHIFT = 6
SUBLANE_SHIFT = 3
N_PAIRS = 4

C_RW = 0
C_DT = R_IN_W
C_Q = 2048
C_K = C_Q + A_W
C_V = C_K + A_W
C_Z = C_V + A_W
C_XBC = C_Z + S_W
C_GATE = C_XBC + S_CONV_CH
N_PROJ = C_GATE + 3 * D_MODEL

VMEM_LIMIT = 56 * 1024 * 1024


def _cparams(sem):
    return pltpu.CompilerParams(dimension_semantics=sem, vmem_limit_bytes=VMEM_LIMIT)


def _dot(a, b):
    return jnp.dot(a.astype(BF16), b.astype(BF16), preferred_element_type=F32)


def _dot_nt(a, b):
    return lax.dot_general(a.astype(BF16), b.astype(BF16), (((1,), (1,)), ((), ())),
                           preferred_element_type=F32)


def _dot_tn(a, b):
    return lax.dot_general(a.astype(BF16), b.astype(BF16), (((0,), (0,)), ((), ())),
                           preferred_element_type=F32)


def _split_dot(x, w, passes, x_is_lhs=True):
    w = w.astype(BF16)
    acc = None
    rem = x
    for _ in range(passes):
        piece = rem.astype(BF16)
        term = (jnp.dot(piece, w, preferred_element_type=F32) if x_is_lhs
                else jnp.dot(w, piece, preferred_element_type=F32))
        acc = term if acc is None else acc + term
        rem = rem - piece.astype(F32)
    return acc


def _dot_nt_hi(a, b):
    return lax.dot_general(a, b, (((1,), (1,)), ((), ())), preferred_element_type=F32,
                           precision=lax.Precision.HIGHEST)


def _sigmoid(x):
    return 1.0 / (1.0 + jnp.exp(-x))


def _silu(x):
    return x * _sigmoid(x)


def _softplus(x):
    return jnp.maximum(x, 0.0) + jnp.log(1.0 + jnp.exp(-jnp.abs(x)))


def _head_ones(width, scale):
    r = lax.shift_right_logical(lax.broadcasted_iota(jnp.int32, (width, width), 0), HALF_SHIFT)
    c = lax.shift_right_logical(lax.broadcasted_iota(jnp.int32, (width, width), 1), HALF_SHIFT)
    return jnp.where(r == c, scale, 0.0).astype(F32)


def _rms_matmul_kernel(x_ref, g_ref, w_ref, o_ref, xn_ref):
    @pl.when(pl.program_id(1) == 0)
    def _():
        x = x_ref[...]
        ms = jnp.mean(x * x, axis=-1, keepdims=True)
        xn_ref[...] = (x * lax.rsqrt(ms + NORM_EPS) * g_ref[...]).astype(BF16)

    o_ref[...] = jnp.dot(xn_ref[...], w_ref[...], preferred_element_type=F32)


def _rms_matmul(x, g, w, tm, tn):
    t, d = x.shape
    n = w.shape[1]
    return pl.pallas_call(
        _rms_matmul_kernel,
        grid=(t // tm, n // tn),
        in_specs=[pl.BlockSpec((tm, d), lambda i, j: (i, 0)),
                  pl.BlockSpec((1, d), lambda i, j: (0, 0)),
                  pl.BlockSpec((d, tn), lambda i, j: (0, j))],
        out_specs=pl.BlockSpec((tm, tn), lambda i, j: (i, j)),
        out_shape=jax.ShapeDtypeStruct((t, n), F32),
        scratch_shapes=[pltpu.VMEM((tm, d), BF16)],
        compiler_params=_cparams(("parallel", "arbitrary")),
        name="in_proj",
    )(x, g, w)


def _topk_bias(gate, n_valid, axis=1):
    pos = lax.broadcasted_iota(jnp.int32, gate.shape, axis)
    pos_f = pos.astype(F32)
    gm = jnp.where(pos < n_valid, gate, NEG_INF)
    selected = jnp.zeros(gate.shape, jnp.bool_)
    for j in range(MOBA_TOPK):
        m = jnp.max(gm, axis=axis, keepdims=True)
        idx = jnp.min(jnp.where(gm == m, pos_f, 1e9), axis=axis, keepdims=True)
        hit = pos_f == idx
        selected = jnp.logical_or(selected, jnp.logical_and(hit, j < n_valid))
        gm = jnp.where(hit, -jnp.inf, gm)
    return jnp.where(selected, 0.0, NEG_INF).astype(F32)


def _bf16_pieces(x, n):
    out = []
    rem = x
    for _ in range(n):
        piece = rem.astype(BF16).astype(F32)
        out.append(piece)
        rem = rem - piece
    return out


def _moba_prompt_kernel(*refs, nb, nq):
    def tile(i, carry):
        _moba_prompt_tile(i, *refs, nb=nb)
        return carry

    lax.fori_loop(0, nq, tile, 0)


def _moba_prompt_tile(i, q_ref, k_ref, v_ref, sl_ref, o_ref, kt_out, vt_out, kf_ref, vth_ref, kmean_ref, sel_ref,
                      acc_ref, raw_a, raw_b, m_ref, knorm_ref, *, nb):
    B = MOBA_BLOCK
    n_piece = 3
    tile_rows = pl.ds(pl.multiple_of(i * Q_BLOCK, Q_BLOCK), Q_BLOCK)
    lane = lax.broadcasted_iota(jnp.int32, (1, LANES), 1)
    halves = (lane < HALF, lane >= HALF)

    @pl.when(i == 0)
    def _():
        kmean_ref[...] = jnp.zeros_like(kmean_ref)
        key_local = jnp.bitwise_and(lax.broadcasted_iota(jnp.int32, (k_ref.shape[1], 1), 0), B - 1).astype(F32)
        extra = jnp.where(lane < n_piece, key_local, jnp.where(lane < 2 * n_piece, 1.0, 0.0))
        kf_ref[...] = jnp.concatenate([k_ref[0], extra], axis=1).astype(BF16)
        ones_rows = jnp.ones((SUBLANES, B), F32)
        knorm = [jnp.zeros((1, LANES), F32), jnp.zeros((1, LANES), F32)]
        for n in range(nb):
            kblk = k_ref[0, n * B:(n + 1) * B, :]
            kmean_ref[n:n + 1, :] = jnp.mean(kblk, axis=0, keepdims=True)
            vt = v_ref[0, n * B:(n + 1) * B, :].T
            kt_out[0, :, n * B:(n + 1) * B] = kblk.T
            vt_out[0, :, n * B:(n + 1) * B] = vt
            for h in range(2):
                vth_ref[h, n] = jnp.concatenate([vt[h * HALF:(h + 1) * HALF], ones_rows], axis=0).astype(BF16)
                norm2 = jnp.sum(jnp.where(halves[h], kblk * kblk, 0.0), axis=-1, keepdims=True)
                knorm[h] = jnp.where(lane == n, jnp.max(norm2, axis=0, keepdims=True), knorm[h])
        for h in range(2):
            knorm_ref[h] = jnp.broadcast_to(knorm[h], (SUBLANES, LANES))

    own = i
    q = q_ref[0, tile_rows, :]
    log2e = 1.0 / math.log(2.0)
    slope2 = (sl_ref[0, :, 0:1] * log2e, sl_ref[0, :, HALF:HALF + 1] * log2e)
    kmean = kmean_ref[...]
    q_local = lax.broadcasted_iota(jnp.int32, (Q_BLOCK, 1), 0).astype(F32)
    q_rows = []
    for h in range(2):
        q_m = jnp.where(halves[h], q, 0.0)
        bias_t = _topk_bias(_dot_nt_hi(kmean, q_m), own, axis=0)
        for n in range(nb):
            sel_ref[h, n] = jnp.broadcast_to(bias_t[n:n + 1, :], (SUBLANES, Q_BLOCK))
        pieces = _bf16_pieces(slope2[h], n_piece) + _bf16_pieces(-slope2[h] * q_local, n_piece)
        extra = jnp.zeros((Q_BLOCK, LANES), F32)
        for j, piece in enumerate(pieces):
            extra = jnp.where(lane == j, piece, extra)
        q_rows.append(jnp.concatenate([q_m * (ATTN_SCALE * log2e), extra], axis=1))
    qf = jnp.concatenate(q_rows, axis=0).astype(BF16)

    rel = (lax.broadcasted_iota(jnp.int32, (B, Q_BLOCK), 0) - lax.broadcasted_iota(jnp.int32, (B, Q_BLOCK), 1))

    def scores(n):
        both = _dot_nt(kf_ref[pl.ds(pl.multiple_of(n * B, B), B), :], qf)
        return [both[:, h * Q_BLOCK:(h + 1) * Q_BLOCK] for h in range(2)]

    def put_scores(n, dst_ref):
        for h, s in enumerate(scores(n)):
            dst_ref[h] = s

    def row_of(ref, h):
        return ref[h][0:1, :]

    def put_row(ref, h, x):
        ref[h] = jnp.broadcast_to(x, (SUBLANES, Q_BLOCK))

    def attend(n, src_ref):
        shift = ((n - i) * B).astype(F32)
        ps, alphas = [], []
        for h in range(2):
            s = src_ref[h] + (sel_ref[h, n][0:1, :] + slope2[h] * shift)
            m0 = row_of(m_ref, h)
            m1 = jnp.maximum(m0, jnp.max(s, axis=0, keepdims=True))
            alphas.append(jnp.exp2(m0 - m1))
            ps.append(jnp.exp2(s - m1).astype(BF16))
            put_row(m_ref, h, m1)
        pv = [jnp.dot(vth_ref[h, n], ps[h], preferred_element_type=F32) for h in range(2)]
        for h in range(2):
            acc_ref[h] = alphas[h] * acc_ref[h] + pv[h]

    own_raw = scores(own)
    lane_f = lane.astype(F32)
    bound_c = ATTN_SCALE * log2e * 1.02
    skips = []
    for h in range(2):
        q_m = jnp.where(halves[h], q, 0.0)
        qn2 = jnp.max(jnp.sum(q_m * q_m, axis=-1, keepdims=True), axis=0, keepdims=True)
        ub = jnp.sqrt(qn2 * knorm_ref[h][0:1, :]) * bound_c
        ub_own = jnp.sum(jnp.where(lane == own, ub, 0.0), axis=-1, keepdims=True)
        far = slope2[h] * ((lane_f - own.astype(F32)) * B + (B - 1))
        skips.append(ub + far < -ub_own - 160.0)
    keep = jnp.logical_and(lane < own, jnp.logical_not(jnp.logical_and(skips[0], skips[1])))
    first = jnp.min(jnp.where(keep, lane_f, own.astype(F32))).astype(jnp.int32)
    n_visit = own - first

    put_scores(jnp.minimum(first, nb - 1), raw_a)

    own_p = []
    for h in range(2):
        s = jnp.where(rel <= 0, own_raw[h], NEG_INF)
        m = jnp.max(s, axis=0, keepdims=True)
        put_row(m_ref, h, m)
        own_p.append(jnp.exp2(s - m).astype(BF16))
    for h in range(2):
        acc_ref[h] = jnp.dot(vth_ref[h, own], own_p[h], preferred_element_type=F32)

    def body(j, carry):
        n0 = first + 2 * j
        put_scores(n0 + 1, raw_b)
        attend(n0, raw_a)
        put_scores(jnp.minimum(n0 + 2, nb - 1), raw_a)
        attend(n0 + 1, raw_b)
        return carry

    lax.fori_loop(0, n_visit // 2, body, 0)

    @pl.when(n_visit % 2 == 1)
    def _():
        attend(own - 1, raw_a)

    out_t = jnp.concatenate([acc_ref[h][0:HALF] / acc_ref[h][HALF:HALF + 1] for h in range(2)], axis=0)
    o_ref[0, tile_rows, :] = out_t.T


def _moba_prompt(proj3, slopes_pair):
    bsz, seq, _ = proj3.shape
    assert seq % MOBA_BLOCK == 0 and seq // MOBA_BLOCK >= MOBA_TOPK
    nb = seq // MOBA_BLOCK
    nbp = -(-nb // SUBLANES) * SUBLANES
    nq = seq // Q_BLOCK
    qc, kc, vc = C_Q // LANES, C_K // LANES, C_V // LANES
    return pl.pallas_call(
        functools.partial(_moba_prompt_kernel, nb=nb, nq=nq),
        grid=(bsz, N_PAIRS),
        in_specs=[pl.BlockSpec((1, seq, LANES), lambda b, p: (b, 0, qc + p)),
                  pl.BlockSpec((1, seq, LANES), lambda b, p: (b, 0, kc + p)),
                  pl.BlockSpec((1, seq, LANES), lambda b, p: (b, 0, vc + p)),
                  pl.BlockSpec((1, 1, LANES), lambda b, p: (p, 0, 0))],
        out_specs=[pl.BlockSpec((1, seq, LANES), lambda b, p: (b, 0, p)),
                   pl.BlockSpec((1, LANES, seq), lambda b, p: (b, p, 0)),
                   pl.BlockSpec((1, LANES, seq), lambda b, p: (b, p, 0))],
        out_shape=[jax.ShapeDtypeStruct((bsz, seq, A_W), F32),
                   jax.ShapeDtypeStruct((bsz, A_W, seq), F32), jax.ShapeDtypeStruct((bsz, A_W, seq), F32)],
        scratch_shapes=[pltpu.VMEM((seq, 2 * LANES), BF16),
                        pltpu.VMEM((2, nb, HALF + SUBLANES, MOBA_BLOCK), BF16),
                        pltpu.VMEM((nbp, LANES), F32), pltpu.VMEM((2, nb, SUBLANES, Q_BLOCK), F32),
                        pltpu.VMEM((2, HALF + SUBLANES, Q_BLOCK), F32),
                        pltpu.VMEM((2, MOBA_BLOCK, Q_BLOCK), F32), pltpu.VMEM((2, MOBA_BLOCK, Q_BLOCK), F32),
                        pltpu.VMEM((2, SUBLANES, Q_BLOCK), F32), pltpu.VMEM((2, SUBLANES, LANES), F32)],
        compiler_params=_cparams(("parallel", "parallel")),
        name="moba_prompt",
    )(proj3, proj3, proj3, slopes_pair)


def _moba_sample_kernel(pt_ref, q_ref, kn_ref, vn_ref, *refs, n_blk, ppb, page, tnew):
    del pt_ref
    n_pages = n_blk * ppb
    k_refs, v_refs = refs[:n_pages], refs[n_pages:2 * n_pages]
    o_ref, m_ref, l_ref, g_ref, acc_ref = refs[2 * n_pages:]
    past = n_blk * MOBA_BLOCK
    rows = A_HEADS * SUBLANES
    lane_head = lax.shift_right_logical(lax.broadcasted_iota(jnp.int32, (SUBLANES, A_W), 1), HALF_SHIFT)
    q8 = q_ref[0] * ATTN_SCALE
    qbd = jnp.concatenate([jnp.where(lane_head == h, q8, 0.0) for h in range(A_HEADS)], axis=0)
    q_hi = qbd.astype(BF16)
    q_lo = (qbd - q_hi.astype(F32)).astype(BF16)
    r = lax.broadcasted_iota(jnp.int32, (rows, 1), 0)
    tok = jnp.bitwise_and(r, SUBLANES - 1).astype(F32)
    slope = jnp.exp2(-(8.0 / A_HEADS) * (lax.shift_right_logical(r, SUBLANE_SHIFT) + 1).astype(F32))
    key = lax.broadcasted_iota(jnp.int32, (1, MOBA_BLOCK), 1).astype(F32)

    def block_t(refs_, n):
        return jnp.concatenate([refs_[j][0, 0].reshape(A_W, page) for j in range(n * ppb, (n + 1) * ppb)],
                               axis=1).astype(BF16)

    raws, fixes = [], []
    for n in range(n_blk):
        kt = block_t(k_refs, n)
        raws.append(jnp.dot(q_hi, kt, preferred_element_type=F32))
        fixes.append(jnp.dot(q_lo, kt, preferred_element_type=F32))
    es = []
    for n in range(n_blk):
        g_ref[n] = jnp.broadcast_to(jnp.sum(raws[n] + fixes[n], axis=-1, keepdims=True) * (1.0 / MOBA_BLOCK),
                                    (rows, LANES))
        s = raws[n] - slope * ((past - n * MOBA_BLOCK + tok) - key)
        m = jnp.max(s, axis=-1, keepdims=True)
        e = jnp.exp(s - m)
        m_ref[n] = jnp.broadcast_to(m, (rows, LANES))
        l_ref[n] = jnp.broadcast_to(jnp.sum(e, axis=-1, keepdims=True), (rows, LANES))
        es.append(e.astype(BF16))
    for n in range(n_blk):
        acc_ref[n] = lax.dot_general(es[n], block_t(v_refs, n), (((1,), (1,)), ((), ())),
                                     preferred_element_type=F32)

    lane = lax.broadcasted_iota(jnp.int32, (rows, LANES), 1)
    gate = jnp.zeros((rows, LANES), F32)
    for j in range(n_blk):
        gate = jnp.where(lane == j, g_ref[j], gate)
    bias = _topk_bias(gate, n_blk)

    kn = kn_ref[0]
    vn = vn_ref[0]
    s_own = []
    for j in range(tnew):
        sj = jnp.sum(qbd * kn[j:j + 1, :], axis=-1, keepdims=True) - slope * (tok - j)
        s_own.append(jnp.where(tok >= j, sj, NEG_INF))
    mx = s_own[0]
    for j in range(1, tnew):
        mx = jnp.maximum(mx, s_own[j])
    mb = []
    for j in range(n_blk):
        mj = m_ref[j][:, 0:1] + bias[:, j:j + 1]
        mb.append(mj)
        mx = jnp.maximum(mx, mj)
    lsum = jnp.zeros((rows, 1), F32)
    acc = jnp.zeros((rows, A_W), F32)
    for j in range(tnew):
        w = jnp.exp(s_own[j] - mx)
        lsum = lsum + w
        acc = acc + w * vn[j:j + 1, :]
    for j in range(n_blk):
        w = jnp.exp(mb[j] - mx)
        lsum = lsum + w * l_ref[j][:, 0:1]
        acc = acc + w * acc_ref[j]
    out = acc / lsum
    o8 = jnp.zeros((SUBLANES, A_W), F32)
    for h in range(A_HEADS):
        o8 = o8 + jnp.where(lane_head == h, out[h * SUBLANES:(h + 1) * SUBLANES, :], 0.0)
    o_ref[0] = o8


def _moba_sample(q8, k8, v8, cache_kt, cache_vt, page_table, layer, tnew):
    dbsz = q8.shape[0]
    n_pages = page_table.shape[1]
    page = cache_kt.shape[4]
    assert MOBA_BLOCK % page == 0 and (n_pages * page) % MOBA_BLOCK == 0 and page % LANES == 0
    ppb = MOBA_BLOCK // page
    n_blk = n_pages // ppb
    assert n_blk >= MOBA_TOPK and tnew <= SUBLANES
    rows = A_HEADS * SUBLANES
    tok_spec = pl.BlockSpec((1, SUBLANES, A_W), lambda b, pt: (b, 0, 0))

    def page_spec(j):
        return pl.BlockSpec((1, 1, A_HEADS, A_HD, page), lambda b, pt: (layer, pt[b * n_pages + j], 0, 0, 0))

    pages = [page_spec(j) for j in range(n_pages)]
    grid_spec = pltpu.PrefetchScalarGridSpec(
        num_scalar_prefetch=1,
        grid=(dbsz,),
        in_specs=[tok_spec, tok_spec, tok_spec] + pages + pages,
        out_specs=tok_spec,
        scratch_shapes=[pltpu.VMEM((n_blk, rows, LANES), F32), pltpu.VMEM((n_blk, rows, LANES), F32),
                        pltpu.VMEM((n_blk, rows, LANES), F32), pltpu.VMEM((n_blk, rows, A_W), F32)],
    )
    return pl.pallas_call(
        functools.partial(_moba_sample_kernel, n_blk=n_blk, ppb=ppb, page=page, tnew=tnew),
        grid_spec=grid_spec,
        out_shape=jax.ShapeDtypeStruct((dbsz, SUBLANES, A_W), F32),
        compiler_params=_cparams(("parallel",)),
        name="moba_sample",
    )(page_table.reshape(-1), q8, k8, v8, *([cache_kt] * n_pages), *([cache_vt] * n_pages))


def _ssd_prompt_kernel(*refs, L, n_sub):
    hout_ref, h_ref = refs[10], refs[12]

    def chunk(j, carry):
        _ssd_chunk(pl.program_id(1) * n_sub + j, pl.ds(pl.multiple_of(j * L, L), L), *refs[:10], *refs[11:], L=L)
        return carry

    lax.fori_loop(0, n_sub, chunk, 0)

    @pl.when(pl.program_id(1) == pl.num_programs(1) - 1)
    def _():
        hout_ref[0] = h_ref[...]


def _ssd_chunk(c, rows, z_ref, xbc_ref, dt_ref, cw_ref, cb_ref, dtb_ref, alog_ref, dvec_ref, ng_ref,
               o_ref, ext_ref, h_ref, *, L):
    P = SUBLANES

    @pl.when(c == 0)
    def _():
        ext_ref[0:P, :] = jnp.zeros((P, S_CONV_CH), F32)
        h_ref[...] = jnp.zeros_like(h_ref)

    @pl.when(c > 0)
    def _():
        ext_ref[0:P, :] = ext_ref[L:L + P, :]

    ext_ref[P:P + L, :] = xbc_ref[rows, :]
    acc = cb_ref[...] + cw_ref[S_CONV - 1:S_CONV, :] * ext_ref[P:P + L, :]
    for d in range(1, S_CONV):
        acc = acc + cw_ref[S_CONV - 1 - d:S_CONV - d, :] * ext_ref[P - d:P - d + L, :]
    xbc = _silu(acc)
    xs = xbc[:, :S_W]
    bm = xbc[:, S_W:S_W + S_GROUPS * S_STATE]
    cm = xbc[:, S_W + S_GROUPS * S_STATE:]

    lane = lax.broadcasted_iota(jnp.int32, (1, LANES), 1)
    is_a = lane < HALF
    dt = jnp.where(lane < S_HEADS, _softplus(dt_ref[rows, :] + dtb_ref[...]), 0.0)
    da = dt * (-jnp.exp(alog_ref[...]))
    rr = lax.broadcasted_iota(jnp.int32, (L, L), 0)
    cc = lax.broadcasted_iota(jnp.int32, (L, L), 1)
    causal = rr >= cc
    cum = _split_dot(da, jnp.where(causal, 1.0, 0.0), 3, x_is_lhs=False)
    cum_t = cum.T
    dt_t = dt.T
    row_a = lax.broadcasted_iota(jnp.int32, (2 * S_HD, 1), 0) < S_HD

    group_of = [(2 * pr * S_GROUPS) // S_HEADS for pr in range(N_PAIRS)]
    bgs = [bm[:, g * S_STATE:(g + 1) * S_STATE] for g in range(S_GROUPS)]
    cgs = [cm[:, g * S_STATE:(g + 1) * S_STATE] for g in range(S_GROUPS)]
    cb_g = [_dot_nt(cgs[g], bgs[g]) for g in range(S_GROUPS)]
    xs_ps = [xs[:, pr * LANES:(pr + 1) * LANES] for pr in range(N_PAIRS)]
    hps = [h_ref[pr] for pr in range(N_PAIRS)]
    halves = (is_a, jnp.logical_not(is_a))
    cum_c = [cum[:, h:h + 1] for h in range(S_HEADS)]
    last = [cum[L - 1:L, h:h + 1] for h in range(S_HEADS)]
    wts = []
    for h in range(S_HEADS):
        seg = cum_c[h] - cum_t[h:h + 1, :]
        dec = jnp.exp(jnp.where(causal, seg, -jnp.inf))
        wts.append(cb_g[group_of[h // 2]] * dec * dt_t[h:h + 1, :])
    intra = [_dot(wts[h], jnp.where(halves[h % 2], xs_ps[h // 2], 0.0)) for h in range(S_HEADS)]
    inter = [_dot_nt(cgs[group_of[pr]], hps[pr]) for pr in range(N_PAIRS)]
    ys = [intra[2 * pr] + intra[2 * pr + 1]
          + inter[pr] * jnp.where(is_a, jnp.exp(cum_c[2 * pr]), jnp.exp(cum_c[2 * pr + 1])) for pr in range(N_PAIRS)]
    tes = [jnp.where(is_a, jnp.exp(last[2 * pr] - cum_c[2 * pr]) * dt[:, 2 * pr:2 * pr + 1],
                     jnp.exp(last[2 * pr + 1] - cum_c[2 * pr + 1]) * dt[:, 2 * pr + 1:2 * pr + 2])
           for pr in range(N_PAIRS)]
    sts = [_dot_tn(xs_ps[pr] * tes[pr], bgs[group_of[pr]]) for pr in range(N_PAIRS)]
    for pr in range(N_PAIRS):
        h_ref[pr] = hps[pr] * jnp.where(row_a, jnp.exp(last[2 * pr]), jnp.exp(last[2 * pr + 1])) + sts[pr]
    y = jnp.concatenate(ys, axis=1) + dvec_ref[...] * xs
    yz = y * _silu(z_ref[rows, :])
    gw = S_W // S_GROUPS
    for g in range(S_GROUPS):
        part = yz[:, g * gw:(g + 1) * gw]
        ms = jnp.mean(part * part, axis=-1, keepdims=True)
        o_ref[rows, g * gw:(g + 1) * gw] = part * lax.rsqrt(ms + NORM_EPS) * ng_ref[:, g * gw:(g + 1) * gw]


def _ssd_prompt(proj, bsz, seq, cw, cb, dtb, alog, dvec, ng):
    L = SSD_CHUNK
    n_sub = 4 if seq % (4 * L) == 0 else 1
    nc = seq // (L * n_sub)
    rows = L * n_sub
    const = lambda shape: pl.BlockSpec(shape, lambda b, c: (0,) * len(shape))
    return pl.pallas_call(
        functools.partial(_ssd_prompt_kernel, L=L, n_sub=n_sub),
        grid=(bsz, nc),
        in_specs=[pl.BlockSpec((rows, S_W), lambda b, c: (b * nc + c, C_Z // S_W)),
                  pl.BlockSpec((rows, S_CONV_CH), lambda b, c: (b * nc + c, C_XBC // S_CONV_CH)),
                  pl.BlockSpec((rows, LANES), lambda b, c: (b * nc + c, C_DT // LANES)),
                  const((S_CONV, S_CONV_CH)), const((1, S_CONV_CH)), const((1, LANES)), const((1, LANES)),
                  const((1, S_W)), const((1, S_W))],
        out_specs=[pl.BlockSpec((rows, S_W), lambda b, c: (b * nc + c, 0)),
                   pl.BlockSpec((1, N_PAIRS, 2 * S_HD, S_STATE), lambda b, c: (b, 0, 0, 0))],
        out_shape=[jax.ShapeDtypeStruct((bsz * seq, S_W), F32),
                   jax.ShapeDtypeStruct((bsz, N_PAIRS, 2 * S_HD, S_STATE), F32)],
        scratch_shapes=[pltpu.VMEM((L + 2 * SUBLANES, S_CONV_CH), F32),
                        pltpu.VMEM((N_PAIRS, 2 * S_HD, S_STATE), F32)],
        compiler_params=_cparams(("parallel", "arbitrary")),
        name="ssd_prompt",
    )(proj, proj, proj, cw, cb, dtb, alog, dvec, ng)


def _ssd_sample_kernel(z_ref, xbc_ref, dt_ref, pre_ref, h0_ref, cw_ref, cb_ref, dtbv_ref, alogv_ref,
                       dvec_ref, ng_ref, o_ref, hout_ref, *, T, nb):
    rows2 = 2 * S_HD
    eye = (lax.broadcasted_iota(jnp.int32, (rows2, LANES), 0)
           == lax.broadcasted_iota(jnp.int32, (rows2, LANES), 1)).astype(F32)
    eye_all = jnp.concatenate([eye] * nb, axis=0)
    ones = jnp.ones((LANES, LANES), F32)
    spread = (lax.broadcasted_iota(jnp.int32, (LANES, S_W), 0)
              == lax.shift_right_logical(lax.broadcasted_iota(jnp.int32, (LANES, S_W), 1), HALF_SHIFT)).astype(F32)
    a_vec = -jnp.exp(alogv_ref[...])
    gw = S_W // S_GROUPS

    def as_columns(x, passes):
        return _split_dot(eye_all * _repeat_rows(x, rows2), ones, passes)

    up = [pre_ref[j] for j in range(S_CONV - 1)] + [xbc_ref[t] for t in range(T)]
    hs = [h0_ref[0, :, pr].reshape(nb * rows2, S_STATE) for pr in range(N_PAIRS)]
    for t in range(T):
        acc = cb_ref[...] + cw_ref[0:1, :] * up[t]
        for j in range(1, S_CONV):
            acc = acc + cw_ref[j:j + 1, :] * up[t + j]
        xc = _silu(acc)
        xs = xc[:, :S_W]
        dt = _softplus(_split_dot(dt_ref[t], spread, 3) + dtbv_ref[...])
        decay = jnp.exp(dt * a_vec)
        xdt = xs * dt
        ys = []
        for pr in range(N_PAIRS):
            g = (2 * pr * S_GROUPS) // S_HEADS
            sl = slice(pr * LANES, (pr + 1) * LANES)
            b_rows = _repeat_rows(xc[:, S_W + g * S_STATE:S_W + (g + 1) * S_STATE], rows2)
            c_rows = _repeat_rows(xc[:, S_W + (S_GROUPS + g) * S_STATE:S_W + (S_GROUPS + g + 1) * S_STATE], rows2)
            hs[pr] = hs[pr] * as_columns(decay[:, sl], 3) + as_columns(xdt[:, sl], 2) * b_rows
            y_col = _split_dot(hs[pr] * c_rows, ones, 2)
            ys.append(jnp.sum((eye_all * y_col).reshape(nb, rows2, LANES), axis=1))
        y = jnp.concatenate(ys, axis=1) + dvec_ref[...] * xs
        yz = y * _silu(z_ref[t])
        for g in range(S_GROUPS):
            part = yz[:, g * gw:(g + 1) * gw]
            ms = jnp.mean(part * part, axis=-1, keepdims=True)
            o_ref[t, :, g * gw:(g + 1) * gw] = part * lax.rsqrt(ms + NORM_EPS) * ng_ref[:, g * gw:(g + 1) * gw]
    for pr in range(N_PAIRS):
        hout_ref[:, pr] = hs[pr].reshape(nb, rows2, S_STATE)


def _ssd_sample(proj3, pre3, h0, layer, cw, cb, dtbv, alogv, dvec, ng, nb=8):
    T, dbsz = proj3.shape[:2]
    const = lambda shape: pl.BlockSpec(shape, lambda j: (0,) * len(shape))
    hspec = pl.BlockSpec((nb, N_PAIRS, 2 * S_HD, S_STATE), lambda j: (j, 0, 0, 0))
    h0spec = pl.BlockSpec((1, nb, N_PAIRS, 2 * S_HD, S_STATE), lambda j: (layer, j, 0, 0, 0))
    return pl.pallas_call(
        functools.partial(_ssd_sample_kernel, T=T, nb=nb),
        grid=(dbsz // nb,),
        in_specs=[pl.BlockSpec((T, nb, S_W), lambda j: (0, j, C_Z // S_W)),
                  pl.BlockSpec((T, nb, S_CONV_CH), lambda j: (0, j, C_XBC // S_CONV_CH)),
                  pl.BlockSpec((T, nb, LANES), lambda j: (0, j, C_DT // LANES)),
                  pl.BlockSpec((S_CONV - 1, nb, S_CONV_CH), lambda j: (0, j, 0)),
                  h0spec,
                  const((S_CONV, S_CONV_CH)), const((1, S_CONV_CH)), const((1, S_W)), const((1, S_W)),
                  const((1, S_W)), const((1, S_W))],
        out_specs=[pl.BlockSpec((T, nb, S_W), lambda j: (0, j, 0)), hspec],
        out_shape=[jax.ShapeDtypeStruct((T, dbsz, S_W), F32),
                   jax.ShapeDtypeStruct((dbsz, N_PAIRS, 2 * S_HD, S_STATE), F32)],
        compiler_params=_cparams(("parallel",)),
        name="ssd_sample",
    )(proj3, proj3, proj3, pre3, h0, cw, cb, dtbv, alogv, dvec, ng)


def _rwkv_prep_kernel(u_ref, pre_ref, mu_ref, w0_ref, w2_ref, a0_ref, a2_ref, g2_ref, kk_ref, ka_ref,
                      r_out, w_out, k_out, v_out, kk_out, kka_out, g_out, ext_ref,
                      *, tm, P, stride, tiles_per_seq):
    i = pl.program_id(0)
    first = (i % tiles_per_seq) == 0

    @pl.when(first)
    def _():
        ext_ref[0:P, :] = pre_ref[...]

    @pl.when(jnp.logical_not(first))
    def _():
        ext_ref[0:P, :] = ext_ref[tm:tm + P, :]

    u = u_ref[...]
    ext_ref[P:P + tm, :] = u
    prev = ext_ref[P - stride:P - stride + tm, :]
    x = u + (prev - u) * mu_ref[...]
    r = x[:, 0:R_W]
    kr = x[:, R_W:2 * R_W]
    vr = x[:, 2 * R_W:3 * R_W]
    xl = x[:, 3 * R_W:]
    w_log = w0_ref[...] + _dot(jnp.tanh(xl), w2_ref[...])
    log_decay = -jnp.exp(-_softplus(-w_log) - 0.5)
    a = _sigmoid(a0_ref[...] + _dot(xl, a2_ref[...]))
    g = _dot(_sigmoid(xl), g2_ref[...])
    kk = kr * kk_ref[...]
    ss = _split_dot(kk * kk, _head_ones(R_W, 1.0), 2)
    kk = kk / jnp.maximum(jnp.sqrt(ss), 1e-12)
    r_out[...] = r
    w_out[...] = log_decay
    k_out[...] = kr * (1.0 + (a - 1.0) * ka_ref[...])
    v_out[...] = vr
    kk_out[...] = kk
    kka_out[...] = kk * a
    g_out[...] = g


def _rwkv_prep(proj, pre, mu, w0, w2p, a0, a2p, g2p, kkw, kaw, tm, stride, tiles_per_seq):
    t = proj.shape[0]
    P = pre.shape[0]
    const = lambda shape: pl.BlockSpec(shape, lambda i: (0,) * len(shape))
    outs = pl.pallas_call(
        functools.partial(_rwkv_prep_kernel, tm=tm, P=P, stride=stride, tiles_per_seq=tiles_per_seq),
        grid=(t // tm,),
        in_specs=[pl.BlockSpec((tm, R_IN_W), lambda i: (i, 0)),
                  const((P, R_IN_W)), const((1, R_IN_W)), const((1, R_W)), const((R_LORA, R_W)),
                  const((1, R_W)), const((R_LORA, R_W)), const((R_LORA, R_W)), const((1, R_W)), const((1, R_W))],
        out_specs=[pl.BlockSpec((tm, R_W), lambda i: (i, 0))] * 7,
        out_shape=[jax.ShapeDtypeStruct((t, R_W), F32)] * 7,
        scratch_shapes=[pltpu.VMEM((tm + 2 * P, R_IN_W), F32)],
        compiler_params=_cparams(("arbitrary",)),
        name="rwkv_prep",
    )(proj, pre, mu, w0, w2p, a0, a2p, g2p, kkw, kaw)
    return outs


def _pair_consts():
    lane = lax.broadcasted_iota(jnp.int32, (R_HD, LANES), 1)
    row = lax.broadcasted_iota(jnp.int32, (R_HD, LANES), 0)
    is_a = lane < HALF
    eye2 = (jnp.bitwise_and(lane, HALF - 1) == row).astype(F32)
    return is_a, eye2


def _unit_lower_inverses(ns):
    L = ns[0].shape[0]
    eye = (lax.broadcasted_iota(jnp.int32, (L, L), 0) == lax.broadcasted_iota(jnp.int32, (L, L), 1)).astype(F32)
    ts = [eye + n for n in ns]
    pws = [_dot(n, n) for n in ns]
    for _ in range(int(math.log2(L)) - 2):
        both = [_dot(jnp.concatenate([t, p], axis=0), p) for t, p in zip(ts, pws)]
        ts = [t + b[:L] for t, b in zip(ts, both)]
        pws = [b[L:] for b in both]
    return [t + _dot(t, p) for t, p in zip(ts, pws)]


def _rwkv_chunk_prompt_kernel(*refs, L, n_sub):
    sout_ref, s_ref = refs[7], refs[8]

    @pl.when(pl.program_id(1) == 0)
    def _():
        s_ref[...] = jnp.zeros_like(s_ref)

    def chunk(j, carry):
        _rwkv_chunk(pl.ds(pl.multiple_of(j * L, L), L), *refs[:7], s_ref, L=L)
        return carry

    lax.fori_loop(0, n_sub, chunk, 0)

    @pl.when(pl.program_id(1) == pl.num_programs(1) - 1)
    def _():
        sout_ref[0] = s_ref[...]


def _rwkv_chunk(rows, r_ref, lw_ref, k_ref, v_ref, kk_ref, kka_ref, y_ref, s_ref, *, L):
    lane = lax.broadcasted_iota(jnp.int32, (1, LANES), 1)
    is_a = lane < HALF
    rr = lax.broadcasted_iota(jnp.int32, (L, L), 0)
    cc = lax.broadcasted_iota(jnp.int32, (L, L), 1)
    incl = rr >= cc
    strict = rr > cc
    tri = jnp.where(incl, 1.0, 0.0).astype(F32)
    r2 = lax.broadcasted_iota(jnp.int32, (LANES, LANES), 0) < HALF
    c2 = lax.broadcasted_iota(jnp.int32, (LANES, LANES), 1) < HALF
    same_head = r2 == c2

    r, lw, k, v, kk, kka = (ref[rows, :] for ref in (r_ref, lw_ref, k_ref, v_ref, kk_ref, kka_ref))
    cum = _split_dot(lw, tri, 3, x_is_lhs=False)
    last = cum[L - 1:L, :]
    inv_p = jnp.exp(-cum)
    to_end = jnp.exp(last - cum)
    b_t = kk * jnp.exp(cum - lw)
    a_t = -kka * inv_p
    k_t = k * inv_p
    r_t = r * jnp.exp(cum)
    a_end = -kka * to_end
    k_end = k * to_end
    decay_end = jnp.exp(last)
    pairs = [slice(pr * LANES, (pr + 1) * LANES) for pr in range(N_PAIRS)]
    halves = (is_a, jnp.logical_not(is_a))
    s0 = [s_ref[pr] for pr in range(N_PAIRS)]

    ns, mks, rak = [], [], []
    for sl in pairs:
        cols = jnp.concatenate([a_t[:, sl], k_t[:, sl]], axis=0)
        for half in halves:
            lhs = jnp.concatenate([jnp.where(half, b_t[:, sl], 0.0), jnp.where(half, r_t[:, sl], 0.0)], axis=0)
            g4 = _dot_nt(lhs, cols)
            ns.append(jnp.where(strict, g4[:L, :L], 0.0))
            mks.append(jnp.where(strict, g4[:L, L:], 0.0))
            rak.append(jnp.concatenate([jnp.where(incl, g4[L:, :L], 0.0), jnp.where(incl, g4[L:, L:], 0.0)], axis=1))
    ts = _unit_lower_inverses(ns)
    from_state = [_dot_nt(jnp.concatenate([b_t[:, sl], r_t[:, sl]], axis=0), s0[pr]) for pr, sl in enumerate(pairs)]
    rhs = []
    for pr, sl in enumerate(pairs):
        mv = _dot(jnp.concatenate([mks[2 * pr], mks[2 * pr + 1]], axis=0), v[:, sl])
        rhs.append(from_state[pr][:L] + jnp.where(is_a, mv[:L], mv[L:]))
    us = []
    for pr in range(N_PAIRS):
        tu = _dot(jnp.concatenate([ts[2 * pr], ts[2 * pr + 1]], axis=0), rhs[pr])
        us.append(jnp.where(is_a, tu[:L], tu[L:]))
    for pr, sl in enumerate(pairs):
        uv = jnp.concatenate([us[pr], v[:, sl]], axis=0)
        y_ref[rows, sl] = from_state[pr][L:] + jnp.where(is_a, _dot(rak[2 * pr], uv), _dot(rak[2 * pr + 1], uv))
    for pr, sl in enumerate(pairs):
        upd = _dot_tn(jnp.concatenate([us[pr], v[:, sl]], axis=0),
                      jnp.concatenate([a_end[:, sl], k_end[:, sl]], axis=0))
        s_ref[pr] = s0[pr] * decay_end[:, sl] + jnp.where(same_head, upd, 0.0)


def _rwkv_scan_prompt(seqs, bsz, seq, L):
    n_sub = 4 if seq % (4 * L) == 0 else 1
    nc = seq // (L * n_sub)
    spec = pl.BlockSpec((L * n_sub, R_W), lambda b, c: (b * nc + c, 0))
    return pl.pallas_call(
        functools.partial(_rwkv_chunk_prompt_kernel, L=L, n_sub=n_sub),
        grid=(bsz, nc),
        in_specs=[spec] * 6,
        out_specs=[spec, pl.BlockSpec((1, N_PAIRS, LANES, LANES), lambda b, c: (b, 0, 0, 0))],
        out_shape=[jax.ShapeDtypeStruct((bsz * seq, R_W), F32),
                   jax.ShapeDtypeStruct((bsz, N_PAIRS, LANES, LANES), F32)],
        scratch_shapes=[pltpu.VMEM((N_PAIRS, LANES, LANES), F32)],
        compiler_params=_cparams(("parallel", "arbitrary")),
        name="rwkv_scan_prompt",
    )(*seqs)


def _repeat_rows(x, reps):
    return jnp.concatenate([jnp.broadcast_to(x[b:b + 1, :], (reps, x.shape[1])) for b in range(x.shape[0])], axis=0)


def _rwkv_scan_sample_kernel(r_ref, w_ref, k_ref, v_ref, kk_ref, kka_ref, s0_ref, y_ref, sout_ref, *, T, nb):
    _, eye2 = _pair_consts()
    eye_all = jnp.concatenate([eye2] * nb, axis=0)
    ones_bd = _head_ones(LANES, 1.0)

    def head_sums(x):
        return _split_dot(x, ones_bd, 2)

    states = [s0_ref[0, :, pr].reshape(nb * R_HD, LANES) for pr in range(N_PAIRS)]
    for t in range(T):
        for pr in range(N_PAIRS):
            sl = slice(pr * LANES, (pr + 1) * LANES)
            rr, ww, kr, vv, kk, kka = (_repeat_rows(x, R_HD) for x in (
                r_ref[t, :, sl], jnp.exp(w_ref[t, :, sl]), k_ref[t, :, sl], v_ref[t, :, sl],
                kk_ref[t, :, sl], kka_ref[t, :, sl]))
            v_col = head_sums(eye_all * vv)
            s = states[pr]
            s = s * ww - head_sums(s * kk) * kka + v_col * kr
            states[pr] = s
            y_col = head_sums(s * rr)
            y_ref[t, :, sl] = jnp.sum((eye_all * y_col).reshape(nb, R_HD, LANES), axis=1)
    for pr in range(N_PAIRS):
        sout_ref[:, pr] = states[pr].reshape(nb, R_HD, LANES)


def _rwkv_scan_sample(seqs3, s0, layer, nb=16):
    T, dbsz = seqs3[0].shape[:2]
    spec = pl.BlockSpec((T, nb, R_W), lambda j: (0, j, 0))
    sspec = pl.BlockSpec((nb, N_PAIRS, R_HD, LANES), lambda j: (j, 0, 0, 0))
    s0spec = pl.BlockSpec((1, nb, N_PAIRS, R_HD, LANES), lambda j: (layer, j, 0, 0, 0))
    return pl.pallas_call(
        functools.partial(_rwkv_scan_sample_kernel, T=T, nb=nb),
        grid=(dbsz // nb,),
        in_specs=[spec] * 6 + [s0spec],
        out_specs=[spec, sspec],
        out_shape=[jax.ShapeDtypeStruct((T, dbsz, R_W), F32),
                   jax.ShapeDtypeStruct((dbsz, N_PAIRS, R_HD, LANES), F32)],
        compiler_params=_cparams(("parallel",)),
        name="rwkv_scan_sample",
    )(*seqs3, s0)


def _merge_kernel(x_ref, oa_ref, ob_ref, yr_ref, r_ref, k_ref, v_ref, g_ref, ga_ref, gb_ref, gc_ref,
                  bg_ref, lng_ref, lnb_ref, rk_ref, wpa_ref, wpb_ref, wpc_ref, wo_ref, o_ref):
    pa = _dot(oa_ref[...], wpa_ref[...])
    pb = _dot(ob_ref[...], wpb_ref[...])
    mean_m = _head_ones(R_W, 1.0 / R_HD)
    yr = yr_ref[...]
    bonus = _split_dot(r_ref[...] * k_ref[...] * rk_ref[...], _head_ones(R_W, 1.0), 2)
    d = yr - _split_dot(yr, mean_m, 2)
    var = _split_dot(d * d, mean_m, 2)
    yn = d * lax.rsqrt(var + GN_EPS) * lng_ref[...] + lnb_ref[...]
    oc = (yn + bonus * v_ref[...]) * g_ref[...]
    merged = (_sigmoid(ga_ref[...] + bg_ref[:, 0:D_MODEL]) * pa
              + _sigmoid(gb_ref[...] + bg_ref[:, D_MODEL:2 * D_MODEL]) * pb
              + _sigmoid(gc_ref[...] + bg_ref[:, 2 * D_MODEL:]) * _dot(oc, wpc_ref[...]))
    o_ref[...] = x_ref[...] + _dot(merged, wo_ref[...])


def _merge(x, proj, oa, ob, yr, r, k2, v, g, bg, lng, lnb, rk, wpa, wpb, wpc, wo, tm):
    t = x.shape[0]
    row = lambda w: pl.BlockSpec((tm, w), lambda i: (i, 0))
    const = lambda shape: pl.BlockSpec(shape, lambda i: (0,) * len(shape))
    gcol = C_GATE // D_MODEL
    gate = lambda j: pl.BlockSpec((tm, D_MODEL), lambda i: (i, gcol + j))
    return pl.pallas_call(
        _merge_kernel,
        grid=(t // tm,),
        in_specs=[row(D_MODEL), row(A_W), row(S_W), row(R_W), row(R_W), row(R_W), row(R_W), row(R_W),
                  gate(0), gate(1), gate(2),
                  const((1, 3 * D_MODEL)), const((1, R_W)), const((1, R_W)), const((1, R_W)),
                  const((A_W, D_MODEL)), const((S_W, D_MODEL)), const((R_W, D_MODEL)),
                  const((D_MODEL, D_MODEL))],
        out_specs=row(D_MODEL),
        out_shape=jax.ShapeDtypeStruct((t, D_MODEL), F32),
        compiler_params=_cparams(("parallel",)),
        name="merge",
    )(x, oa, ob, yr, r, k2, v, g, proj, proj, proj, bg, lng, lnb, rk, wpa, wpb, wpc, wo)


def _ffn_kernel(x_ref, g_ref, wug_ref, wuv_ref, wd_ref, cw_ref, cb_ref, pre_ref, gf_ref,
                o_ref, tail_ref, xn_ref, ext_ref, *, tm, P, stride, tiles_per_seq, final_norm):
    i = pl.program_id(0)
    f = pl.program_id(1)
    first = (i % tiles_per_seq) == 0

    @pl.when(f == 0)
    def _():
        x = x_ref[...]
        ms = jnp.mean(x * x, axis=-1, keepdims=True)
        xn_ref[...] = (x * lax.rsqrt(ms + NORM_EPS) * g_ref[...]).astype(BF16)

    @pl.when(first)
    def _():
        ext_ref[f, 0:P, :] = pre_ref[...]

    @pl.when(jnp.logical_not(first))
    def _():
        ext_ref[f, 0:P, :] = ext_ref[f, tm:tm + P, :]

    xn = xn_ref[...]
    ug = jnp.dot(xn, wug_ref[...], preferred_element_type=F32)
    uv = jnp.dot(xn, wuv_ref[...], preferred_element_type=F32)
    ext_ref[f, P:P + tm, :] = ug
    tail_ref[0] = ext_ref[f, tm:tm + P, :]
    acc = cb_ref[...] + cw_ref[F_CONV - 1:F_CONV, :] * ug
    for d in range(1, F_CONV):
        acc = acc + cw_ref[F_CONV - 1 - d:F_CONV - d, :] * ext_ref[f, P - d * stride:P - d * stride + tm, :]
    contrib = _dot(_silu(acc) * uv, wd_ref[...])

    @pl.when(f == 0)
    def _():
        o_ref[...] = x_ref[...] + contrib

    @pl.when(f > 0)
    def _():
        o_ref[...] = o_ref[...] + contrib

    if final_norm:
        @pl.when(f == pl.num_programs(1) - 1)
        def _():
            y = o_ref[...]
            ms = jnp.mean(y * y, axis=-1, keepdims=True)
            o_ref[...] = y * lax.rsqrt(ms + NORM_EPS) * gf_ref[...]


def _ffn(x, g, wup, wd, cw, cb, pre, gf, tm, tf, stride, tiles_per_seq, final_norm):
    t = x.shape[0]
    P = pre.shape[0]
    nf = D_FF // tf
    return pl.pallas_call(
        functools.partial(_ffn_kernel, tm=tm, P=P, stride=stride, tiles_per_seq=tiles_per_seq,
                          final_norm=final_norm),
        grid=(t // tm, nf),
        in_specs=[pl.BlockSpec((tm, D_MODEL), lambda i, f: (i, 0)),
                  pl.BlockSpec((1, D_MODEL), lambda i, f: (0, 0)),
                  pl.BlockSpec((D_MODEL, tf), lambda i, f: (0, f)),
                  pl.BlockSpec((D_MODEL, tf), lambda i, f: (0, nf + f)),
                  pl.BlockSpec((tf, D_MODEL), lambda i, f: (f, 0)),
                  pl.BlockSpec((F_CONV, tf), lambda i, f: (0, f)),
                  pl.BlockSpec((1, tf), lambda i, f: (0, f)),
                  pl.BlockSpec((P, tf), lambda i, f: (0, f)),
                  pl.BlockSpec((1, D_MODEL), lambda i, f: (0, 0))],
        out_specs=[pl.BlockSpec((tm, D_MODEL), lambda i, f: (i, 0)),
                   pl.BlockSpec((1, P, tf), lambda i, f: (i, 0, f))],
        out_shape=[jax.ShapeDtypeStruct((t, D_MODEL), F32),
                   jax.ShapeDtypeStruct((t // tm, P, D_FF), F32)],
        scratch_shapes=[pltpu.VMEM((tm, D_MODEL), BF16), pltpu.VMEM((nf, tm + 2 * P, tf), F32)],
        compiler_params=_cparams(("arbitrary", "arbitrary")),
        name="conv_ffn",
    )(x, g, wup, wup, wd, cw, cb, pre, gf)


def _pack_rwkv_state(s):
    d, n = s.shape[:2]
    return s.reshape(d, n, N_PAIRS, 2, R_HD, R_HD).transpose(0, 1, 2, 4, 3, 5).reshape(d, n, N_PAIRS, R_HD, LANES)


def _unpack_rwkv_state(s):
    n = s.shape[0]
    return s.reshape(n, N_PAIRS, R_HD, 2, R_HD).transpose(0, 1, 3, 2, 4).reshape(n, R_HEADS, R_HD, R_HD)


def _unpack_rwkv_blockdiag(s):
    n = s.shape[0]
    return jnp.stack([s[:, :, :R_HD, :R_HD], s[:, :, R_HD:, R_HD:]], axis=2).reshape(n, R_HEADS, R_HD, R_HD)


def _prep_layer_params(l, p):
    w_in = p['w_in'][l]
    c_dt_src = 3 * A_W + S_W + S_CONV_CH
    c_rw_src = c_dt_src + S_HEADS
    c_gate_src = c_rw_src + R_IN_W
    w_proj = jnp.concatenate([
        w_in[:, c_rw_src:c_gate_src],
        w_in[:, c_dt_src:c_rw_src], jnp.zeros((D_MODEL, C_Q - C_DT - S_HEADS), F32),
        w_in[:, :c_dt_src],
        w_in[:, c_gate_src:]], axis=1).astype(BF16)
    pad_lane = lambda v: jnp.pad(v, (0, LANES - v.shape[0])).reshape(1, LANES)
    zl = lambda r0, w: jnp.zeros((R_LORA, R_W), F32).at[r0:r0 + w.shape[0]].set(w).astype(BF16)
    return dict(
        norm1_g=p['norm1_g'][l].reshape(1, D_MODEL), w_proj=w_proj,
        b_gate=p['b_gate'][l].reshape(1, 3 * D_MODEL),
        w_pa=p['w_pa'][l].astype(BF16), w_pb=p['w_pb'][l].astype(BF16), w_pc=p['w_pc'][l].astype(BF16),
        w_o=p['w_o'][l].astype(BF16),
        ssm_conv_w=p['ssm_conv_w'][l], ssm_conv_b=p['ssm_conv_b'][l].reshape(1, S_CONV_CH),
        ssm_dt_bias=pad_lane(p['ssm_dt_bias'][l]), ssm_a_log=pad_lane(p['ssm_a_log'][l]),
        ssm_dvec=jnp.repeat(p['ssm_d'][l], S_HD).reshape(1, S_W),
        ssm_dtb_vec=jnp.repeat(p['ssm_dt_bias'][l], S_HD).reshape(1, S_W),
        ssm_alog_vec=jnp.repeat(p['ssm_a_log'][l], S_HD).reshape(1, S_W),
        ssm_norm_g=p['ssm_norm_g'][l].reshape(1, S_W),
        rw_mu=p['rw_mu'][l].reshape(1, R_IN_W), rw_w0=p['rw_w0'][l].reshape(1, R_W),
        rw_w2p=zl(0, p['rw_w2'][l]), rw_a0=p['rw_a0'][l].reshape(1, R_W),
        rw_a2p=zl(R_LORA_W, p['rw_a2'][l]), rw_g2p=zl(R_LORA_W + R_LORA_A, p['rw_g2'][l]),
        rw_kk=p['rw_kk'][l].reshape(1, R_W), rw_ka=p['rw_ka'][l].reshape(1, R_W),
        rw_rk=p['rw_rk'][l].reshape(1, R_W), rw_ln_g=p['rw_ln_g'][l].reshape(1, R_W),
        rw_ln_b=p['rw_ln_b'][l].reshape(1, R_W),
        norm2_g=p['norm2_g'][l].reshape(1, D_MODEL), w_up=p['w_up'][l].astype(BF16),
        w_down=p['w_down'][l].astype(BF16), ffn_conv_w=p['ffn_conv_w'][l],
        ffn_conv_b=p['ffn_conv_b'][l].reshape(1, D_FF))


def _row_tile(t):
    for tm in (512, 256, 128):
        if t % tm == 0:
            return tm
    raise ValueError(t)


def _ffn_tf():
    return D_FF // 2


def _prompt_layer(x, lp, bsz, seq, slopes_pair, gf, final_norm):
    t = bsz * seq
    tm = _row_tile(seq)
    tiles = seq // tm
    proj = _rms_matmul(x, lp['norm1_g'], lp['w_proj'], 2048 if t % 2048 == 0 else tm, 1024)
    proj3 = proj.reshape(bsz, seq, N_PROJ)
    oa, k_t, v_t = _moba_prompt(proj3, slopes_pair)
    oa = oa.reshape(t, A_W)
    ob, ssm_new = _ssd_prompt(proj, bsz, seq, lp['ssm_conv_w'], lp['ssm_conv_b'], lp['ssm_dt_bias'],
                              lp['ssm_a_log'], lp['ssm_dvec'], lp['ssm_norm_g'])
    r, w, k2, v, kk, kka, g = _rwkv_prep(
        proj, jnp.zeros((SUBLANES, R_IN_W), F32), lp['rw_mu'], lp['rw_w0'], lp['rw_w2p'], lp['rw_a0'],
        lp['rw_a2p'], lp['rw_g2p'], lp['rw_kk'], lp['rw_ka'], tm, 1, tiles)
    yr, rw_new = _rwkv_scan_prompt((r, w, k2, v, kk, kka), bsz, seq, LANES)
    x = _merge(x, proj, oa, ob, yr, r, k2, v, g, lp['b_gate'], lp['rw_ln_g'], lp['rw_ln_b'], lp['rw_rk'],
               lp['w_pa'], lp['w_pb'], lp['w_pc'], lp['w_o'], min(tm, 256))
    x, tail = _ffn(x, lp['norm2_g'], lp['w_up'], lp['w_down'], lp['ffn_conv_w'], lp['ffn_conv_b'],
                   jnp.zeros((SUBLANES, D_FF), F32), gf, tm, _ffn_tf(), 1, tiles, final_norm)
    k_new = k_t.reshape(bsz, A_HEADS, A_HD, seq).transpose(0, 3, 1, 2)
    v_new = v_t.reshape(bsz, A_HEADS, A_HD, seq).transpose(0, 3, 1, 2)
    ssm_conv_new = proj3[:, seq - (S_CONV - 1):, C_XBC:C_XBC + S_CONV_CH]
    shift_new = proj3[:, seq - 1:, C_RW:C_RW + R_IN_W]
    ffn_conv_new = tail.reshape(bsz, tiles, SUBLANES, D_FF)[:, tiles - 1, SUBLANES - (F_CONV - 1):]
    state = (k_new, v_new, ssm_new.reshape(bsz, S_HEADS, S_HD, S_STATE), ssm_conv_new,
             _unpack_rwkv_blockdiag(rw_new), shift_new, ffn_conv_new)
    return x, state


def _sample_layer(x, lp, dbsz, tnew, st, cache_k4, cache_v4, page_table, layer, gf, final_norm):
    t = tnew * dbsz
    ssm0, ssm_conv0, rwkv0, shift0, ffn_conv0 = st
    proj = _rms_matmul(x, lp['norm1_g'], lp['w_proj'], t, 1024)
    proj3 = proj.reshape(tnew, dbsz, N_PROJ)

    qkv = proj3[:, :, C_Q:C_Q + 3 * A_W].transpose(1, 0, 2)
    qkv8 = jnp.pad(qkv, ((0, 0), (0, SUBLANES - tnew), (0, 0)))
    oa8 = _moba_sample(qkv8[:, :, :A_W], qkv8[:, :, A_W:2 * A_W], qkv8[:, :, 2 * A_W:],
                       cache_k4, cache_v4, page_table, layer, tnew)
    oa = oa8[:, :tnew].transpose(1, 0, 2).reshape(t, A_W)

    h0 = ssm0.reshape(-1, dbsz, N_PAIRS, 2 * S_HD, S_STATE)
    ob3, ssm_new = _ssd_sample(proj3, ssm_conv0.transpose(1, 0, 2), h0, layer, lp['ssm_conv_w'],
                               lp['ssm_conv_b'], lp['ssm_dtb_vec'], lp['ssm_alog_vec'], lp['ssm_dvec'],
                               lp['ssm_norm_g'])
    ob = ob3.reshape(t, S_W)
    ssm_new = ssm_new.reshape(dbsz, S_HEADS, S_HD, S_STATE)

    r, w, k2, v, kk, kka, g = _rwkv_prep(
        proj, shift0.reshape(dbsz, R_IN_W), lp['rw_mu'], lp['rw_w0'], lp['rw_w2p'], lp['rw_a0'],
        lp['rw_a2p'], lp['rw_g2p'], lp['rw_kk'], lp['rw_ka'], t, dbsz, 1)
    to3 = lambda a: a.reshape(tnew, dbsz, R_W)
    yr3, rw_new = _rwkv_scan_sample(tuple(to3(a) for a in (r, w, k2, v, kk, kka)), rwkv0, layer)
    yr = yr3.reshape(t, R_W)

    x = _merge(x, proj, oa, ob, yr, r, k2, v, g, lp['b_gate'], lp['rw_ln_g'], lp['rw_ln_b'], lp['rw_rk'],
               lp['w_pa'], lp['w_pb'], lp['w_pc'], lp['w_o'], min(t, 256))
    pre = ffn_conv0.transpose(1, 0, 2).reshape((F_CONV - 1) * dbsz, D_FF)
    x, tail = _ffn(x, lp['norm2_g'], lp['w_up'], lp['w_down'], lp['ffn_conv_w'], lp['ffn_conv_b'],
                   pre, gf, t, _ffn_tf(), dbsz, 1, final_norm)

    k_new = qkv[:, :, A_W:2 * A_W].reshape(dbsz, tnew, A_HEADS, A_HD)
    v_new = qkv[:, :, 2 * A_W:].reshape(dbsz, tnew, A_HEADS, A_HD)
    ssm_conv_new = proj3[tnew - (S_CONV - 1):, :, C_XBC:C_XBC + S_CONV_CH].transpose(1, 0, 2)
    shift_new = proj3[tnew - 1:, :, C_RW:C_RW + R_IN_W].transpose(1, 0, 2)
    ffn_conv_new = tail.reshape(F_CONV - 1, dbsz, D_FF).transpose(1, 0, 2)
    state = (k_new, v_new, ssm_new, ssm_conv_new, _unpack_rwkv_state(rw_new), shift_new, ffn_conv_new)
    return x, state


def kernel(x_prompt, x_sample, cache_k, cache_v, state_ssm, state_ssm_conv, state_rwkv, state_rwkv_shift, state_ffn_conv, page_table, norm1_g, w_in, b_gate, w_pa, ssm_conv_w, ssm_conv_b, ssm_dt_bias, ssm_a_log, ssm_d, ssm_norm_g, w_pb, rw_mu, rw_w0, rw_w2, rw_a0, rw_a2, rw_g2, rw_kk, rw_ka, rw_rk, rw_ln_g, rw_ln_b, w_pc, w_o, norm2_g, w_up, ffn_conv_w, ffn_conv_b, w_down, norm_f_g):
    params = dict(norm1_g=norm1_g, w_in=w_in, b_gate=b_gate, w_pa=w_pa, ssm_conv_w=ssm_conv_w,
                  ssm_conv_b=ssm_conv_b, ssm_dt_bias=ssm_dt_bias, ssm_a_log=ssm_a_log, ssm_d=ssm_d,
                  ssm_norm_g=ssm_norm_g, w_pb=w_pb, rw_mu=rw_mu, rw_w0=rw_w0, rw_w2=rw_w2, rw_a0=rw_a0,
                  rw_a2=rw_a2, rw_g2=rw_g2, rw_kk=rw_kk, rw_ka=rw_ka, rw_rk=rw_rk, rw_ln_g=rw_ln_g,
                  rw_ln_b=rw_ln_b, w_pc=w_pc, w_o=w_o, norm2_g=norm2_g, w_up=w_up, ffn_conv_w=ffn_conv_w,
                  ffn_conv_b=ffn_conv_b, w_down=w_down)
    depth = w_in.shape[0]
    bsz, seq, _ = x_prompt.shape
    dbsz, tnew, _ = x_sample.shape
    head = jnp.arange(A_HEADS, dtype=F32) + 1.0
    slopes = jnp.exp2(-8.0 * head / A_HEADS)
    slopes_pair = jnp.repeat(slopes, A_HD).reshape(N_PAIRS, 1, LANES)
    cache_k4 = cache_k.transpose(0, 1, 3, 4, 2)
    cache_v4 = cache_v.transpose(0, 1, 3, 4, 2)
    gf = norm_f_g.reshape(1, D_MODEL)

    hp = x_prompt.reshape(bsz * seq, D_MODEL)
    hs = x_sample.transpose(1, 0, 2).reshape(tnew * dbsz, D_MODEL)
    new_p = [[] for _ in range(7)]
    new_s = [[] for _ in range(7)]
    rwkv_packed = _pack_rwkv_state(state_rwkv)
    for l in range(depth):
        lp = _prep_layer_params(l, params)
        last = l == depth - 1
        hp, sp = _prompt_layer(hp, lp, bsz, seq, slopes_pair, gf, last)
        st = (state_ssm, state_ssm_conv[l], rwkv_packed, state_rwkv_shift[l], state_ffn_conv[l])
        hs, ss = _sample_layer(hs, lp, dbsz, tnew, st, cache_k4, cache_v4, page_table, l, gf, last)
        for j in range(7):
            new_p[j].append(sp[j])
            new_s[j].append(ss[j])
    y_prompt = hp.reshape(bsz, seq, D_MODEL)
    y_sample = hs.reshape(tnew, dbsz, D_MODEL).transpose(1, 0, 2)
    outs = [y_prompt, y_sample]
    for j in range(7):
        outs += [jnp.stack(new_p[j]), jnp.stack(new_s[j])]
    return tuple(outs)
```

```python
import functools
import math

import jax
import jax.numpy as jnp
from jax import lax
from jax.experimental import pallas as pl
from jax.experimental.pallas import tpu as pltpu

F32 = jnp.float32
BF16 = jnp.bfloat16

D_MODEL = 1024
A_HEADS = 8
A_HD = 64
A_W = A_HEADS * A_HD
MOBA_BLOCK = 256
MOBA_TOPK = 3
Q_BLOCK = MOBA_BLOCK
ATTN_SCALE = A_HD ** -0.5
S_HEADS = 8
S_HD = 64
S_W = S_HEADS * S_HD
S_GROUPS = 2
S_STATE = 128
S_CONV = 4
S_CONV_CH = S_W + 2 * S_GROUPS * S_STATE
SSD_CHUNK = 128
R_HEADS = 8
R_HD = 64
R_W = R_HEADS * R_HD
R_LORA_W = 64
R_LORA_A = 64
R_LORA_G = 128
R_LORA = R_LORA_W + R_LORA_A + R_LORA_G
R_IN_W = 3 * R_W + R_LORA
D_FF = ((8 * D_MODEL // 3 + 127) // 128) * 128
F_CONV = 3
NORM_EPS = 1e-6
GN_EPS = 64e-5
NEG_INF = -1e30

LANES = 128
SUBLANES = 8
HALF = 64
HALF_SHIFT = 6
SUBLANE_SHIFT = 3
N_PAIRS = 4

C_RW = 0
C_DT = R_IN_W
C_Q = 2048
C_K = C_Q + A_W
C_V = C_K + A_W
C_Z = C_V + A_W
C_XBC = C_Z + S_W
C_GATE = C_XBC + S_CONV_CH
N_PROJ = C_GATE + 3 * D_MODEL

VMEM_LIMIT = 56 * 1024 * 1024


def _cparams(sem):
    return pltpu.CompilerParams(dimension_semantics=sem, vmem_limit_bytes=VMEM_LIMIT)


def _dot(a, b):
    return jnp.dot(a.astype(BF16), b.astype(BF16), preferred_element_type=F32)


def _dot_nt(a, b):
    return lax.dot_general(a.astype(BF16), b.astype(BF16), (((1,), (1,)), ((), ())),
                           preferred_element_type=F32)


def _dot_tn(a, b):
    return lax.dot_general(a.astype(BF16), b.astype(BF16), (((0,), (0,)), ((), ())),
                           preferred_element_type=F32)


def _split_dot(x, w, passes, x_is_lhs=True):
    w = w.astype(BF16)
    acc = None
    rem = x
    for _ in range(passes):
        piece = rem.astype(BF16)
        term = (jnp.dot(piece, w, preferred_element_type=F32) if x_is_lhs
                else jnp.dot(w, piece, preferred_element_type=F32))
        acc = term if acc is None else acc + term
        rem = rem - piece.astype(F32)
    return acc


def _dot_nt_hi(a, b):
    return lax.dot_general(a, b, (((1,), (1,)), ((), ())), preferred_element_type=F32,
                           precision=lax.Precision.HIGHEST)


def _sigmoid(x):
    return 1.0 / (1.0 + jnp.exp(-x))


def _silu(x):
    return x * _sigmoid(x)


def _softplus(x):
    return jnp.maximum(x, 0.0) + jnp.log(1.0 + jnp.exp(-jnp.abs(x)))


def _head_ones(width, scale):
    r = lax.shift_right_logical(lax.broadcasted_iota(jnp.int32, (width, width), 0), HALF_SHIFT)
    c = lax.shift_right_logical(lax.broadcasted_iota(jnp.int32, (width, width), 1), HALF_SHIFT)
    return jnp.where(r == c, scale, 0.0).astype(F32)


def _rms_matmul_kernel(x_ref, g_ref, w_ref, o_ref, xn_ref):
    @pl.when(pl.program_id(1) == 0)
    def _():
        x = x_ref[...]
        ms = jnp.mean(x * x, axis=-1, keepdims=True)
        xn_ref[...] = (x * lax.rsqrt(ms + NORM_EPS) * g_ref[...]).astype(BF16)

    o_ref[...] = jnp.dot(xn_ref[...], w_ref[...], preferred_element_type=F32)


def _rms_matmul(x, g, w, tm, tn):
    t, d = x.shape
    n = w.shape[1]
    return pl.pallas_call(
        _rms_matmul_kernel,
        grid=(t // tm, n // tn),
        in_specs=[pl.BlockSpec((tm, d), lambda i, j: (i, 0)),
                  pl.BlockSpec((1, d), lambda i, j: (0, 0)),
                  pl.BlockSpec((d, tn), lambda i, j: (0, j))],
        out_specs=pl.BlockSpec((tm, tn), lambda i, j: (i, j)),
        out_shape=jax.ShapeDtypeStruct((t, n), F32),
        scratch_shapes=[pltpu.VMEM((tm, d), BF16)],
        compiler_params=_cparams(("parallel", "arbitrary")),
        name="in_proj",
    )(x, g, w)


def _topk_bias(gate, n_valid, axis=1):
    pos = lax.broadcasted_iota(jnp.int32, gate.shape, axis)
    pos_f = pos.astype(F32)
    gm = jnp.where(pos < n_valid, gate, NEG_INF)
    selected = jnp.zeros(gate.shape, jnp.bool_)
    for j in range(MOBA_TOPK):
        m = jnp.max(gm, axis=axis, keepdims=True)
        idx = jnp.min(jnp.where(gm == m, pos_f, 1e9), axis=axis, keepdims=True)
        hit = pos_f == idx
        selected = jnp.logical_or(selected, jnp.logical_and(hit, j < n_valid))
        gm = jnp.where(hit, -jnp.inf, gm)
    return jnp.where(selected, 0.0, NEG_INF).astype(F32)


def _bf16_pieces(x, n):
    out = []
    rem = x
    for _ in range(n):
        piece = rem.astype(BF16).astype(F32)
        out.append(piece)
        rem = rem - piece
    return out


def _moba_prompt_kernel(*refs, nb, nq):
    def tile(i, carry):
        _moba_prompt_tile(i, *refs, nb=nb)
        return carry

    lax.fori_loop(0, nq, tile, 0)


def _moba_prompt_tile(i, q_ref, k_ref, v_ref, sl_ref, o_ref, kt_out, vt_out, kf_ref, vth_ref, kmean_ref, sel_ref,
                      acc_ref, raw_a, raw_b, m_ref, knorm_ref, *, nb):
    B = MOBA_BLOCK
    n_piece = 3
    tile_rows = pl.ds(pl.multiple_of(i * Q_BLOCK, Q_BLOCK), Q_BLOCK)
    lane = lax.broadcasted_iota(jnp.int32, (1, LANES), 1)
    halves = (lane < HALF, lane >= HALF)

    @pl.when(i == 0)
    def _():
        kmean_ref[...] = jnp.zeros_like(kmean_ref)
        key_local = jnp.bitwise_and(lax.broadcasted_iota(jnp.int32, (k_ref.shape[1], 1), 0), B - 1).astype(F32)
        extra = jnp.where(lane < n_piece, key_local, jnp.where(lane < 2 * n_piece, 1.0, 0.0))
        kf_ref[...] = jnp.concatenate([k_ref[0], extra], axis=1).astype(BF16)
        ones_rows = jnp.ones((SUBLANES, B), F32)
        knorm = [jnp.zeros((1, LANES), F32), jnp.zeros((1, LANES), F32)]
        for n in range(nb):
            kblk = k_ref[0, n * B:(n + 1) * B, :]
            kmean_ref[n:n + 1, :] = jnp.mean(kblk, axis=0, keepdims=True)
            vt = v_ref[0, n * B:(n + 1) * B, :].T
            kt_out[0, :, n * B:(n + 1) * B] = kblk.T
            vt_out[0, :, n * B:(n + 1) * B] = vt
            for h in range(2):
                vth_ref[h, n] = jnp.concatenate([vt[h * HALF:(h + 1) * HALF], ones_rows], axis=0).astype(BF16)
                norm2 = jnp.sum(jnp.where(halves[h], kblk * kblk, 0.0), axis=-1, keepdims=True)
                knorm[h] = jnp.where(lane == n, jnp.max(norm2, axis=0, keepdims=True), knorm[h])
        for h in range(2):
            knorm_ref[h] = jnp.broadcast_to(knorm[h], (SUBLANES, LANES))

    own = i
    q = q_ref[0, tile_rows, :]
    log2e = 1.0 / math.log(2.0)
    slope2 = (sl_ref[0, :, 0:1] * log2e, sl_ref[0, :, HALF:HALF + 1] * log2e)
    kmean = kmean_ref[...]
    q_local = lax.broadcasted_iota(jnp.int32, (Q_BLOCK, 1), 0).astype(F32)
    q_rows = []
    for h in range(2):
        q_m = jnp.where(halves[h], q, 0.0)
        bias_t = _topk_bias(_dot_nt_hi(kmean, q_m), own, axis=0)
        for n in range(nb):
            sel_ref[h, n] = jnp.broadcast_to(bias_t[n:n + 1, :], (SUBLANES, Q_BLOCK))
        pieces = _bf16_pieces(slope2[h], n_piece) + _bf16_pieces(-slope2[h] * q_local, n_piece)
        extra = jnp.zeros((Q_BLOCK, LANES), F32)
        for j, piece in enumerate(pieces):
            extra = jnp.where(lane == j, piece, extra)
        q_rows.append(jnp.concatenate([q_m * (ATTN_SCALE * log2e), extra], axis=1))
    qf = jnp.concatenate(q_rows, axis=0).astype(BF16)

    rel = (lax.broadcasted_iota(jnp.int32, (B, Q_BLOCK), 0) - lax.broadcasted_iota(jnp.int32, (B, Q_BLOCK), 1))

    def scores(n):
        both = _dot_nt(kf_ref[pl.ds(pl.multiple_of(n * B, B), B), :], qf)
        return [both[:, h * Q_BLOCK:(h + 1) * Q_BLOCK] for h in range(2)]

    def put_scores(n, dst_ref):
        for h, s in enumerate(scores(n)):
            dst_ref[h] = s

    def row_of(ref, h):
        return ref[h][0:1, :]

    def put_row(ref, h, x):
        ref[h] = jnp.broadcast_to(x, (SUBLANES, Q_BLOCK))

    def attend(n, src_ref):
        shift = ((n - i) * B).astype(F32)
        ps, alphas = [], []
        for h in range(2):
            s = src_ref[h] + (sel_ref[h, n][0:1, :] + slope2[h] * shift)
            m0 = row_of(m_ref, h)
            m1 = jnp.maximum(m0, jnp.max(s, axis=0, keepdims=True))
            alphas.append(jnp.exp2(m0 - m1))
            ps.append(jnp.exp2(s - m1).astype(BF16))
            put_row(m_ref, h, m1)
        pv = [jnp.dot(vth_ref[h, n], ps[h], preferred_element_type=F32) for h in range(2)]
        for h in range(2):
            acc_ref[h] = alphas[h] * acc_ref[h] + pv[h]

    own_raw = scores(own)
    lane_f = lane.astype(F32)
    bound_c = ATTN_SCALE * log2e * 1.02
    skips = []
    for h in range(2):
        q_m = jnp.where(halves[h], q, 0.0)
        qn2 = jnp.max(jnp.sum(q_m * q_m, axis=-1, keepdims=True), axis=0, keepdims=True)
        ub = jnp.sqrt(qn2 * knorm_ref[h][0:1, :]) * bound_c
        ub_own = jnp.sum(jnp.where(lane == own, ub, 0.0), axis=-1, keepdims=True)
        far = slope2[h] * ((lane_f - own.astype(F32)) * B + (B - 1))
        skips.append(ub + far < -ub_own - 160.0)
    keep = jnp.logical_and(lane < own, jnp.logical_not(jnp.logical_and(skips[0], skips[1])))
    first = jnp.min(jnp.where(keep, lane_f, own.astype(F32))).astype(jnp.int32)
    n_visit = own - first

    put_scores(jnp.minimum(first, nb - 1), raw_a)

    own_p = []
    for h in range(2):
        s = jnp.where(rel <= 0, own_raw[h], NEG_INF)
        m = jnp.max(s, axis=0, keepdims=True)
        put_row(m_ref, h, m)
        own_p.append(jnp.exp2(s - m).astype(BF16))
    for h in range(2):
        acc_ref[h] = jnp.dot(vth_ref[h, own], own_p[h], preferred_element_type=F32)

    def body(j, carry):
        n0 = first + 2 * j
        put_scores(n0 + 1, raw_b)
        attend(n0, raw_a)
        put_scores(jnp.minimum(n0 + 2, nb - 1), raw_a)
        attend(n0 + 1, raw_b)
        return carry

    lax.fori_loop(0, n_visit // 2, body, 0)

    @pl.when(n_visit % 2 == 1)
    def _():
        attend(own - 1, raw_a)

    out_t = jnp.concatenate([acc_ref[h][0:HALF] / acc_ref[h][HALF:HALF + 1] for h in range(2)], axis=0)
    o_ref[0, tile_rows, :] = out_t.T


def _moba_prompt(proj3, slopes_pair):
    bsz, seq, _ = proj3.shape
    assert seq % MOBA_BLOCK == 0 and seq // MOBA_BLOCK >= MOBA_TOPK
    nb = seq // MOBA_BLOCK
    nbp = -(-nb // SUBLANES) * SUBLANES
    nq = seq // Q_BLOCK
    qc, kc, vc = C_Q // LANES, C_K // LANES, C_V // LANES
    return pl.pallas_call(
        functools.partial(_moba_prompt_kernel, nb=nb, nq=nq),
        grid=(bsz, N_PAIRS),
        in_specs=[pl.BlockSpec((1, seq, LANES), lambda b, p: (b, 0, qc + p)),
                  pl.BlockSpec((1, seq, LANES), lambda b, p: (b, 0, kc + p)),
                  pl.BlockSpec((1, seq, LANES), lambda b, p: (b, 0, vc + p)),
                  pl.BlockSpec((1, 1, LANES), lambda b, p: (p, 0, 0))],
        out_specs=[pl.BlockSpec((1, seq, LANES), lambda b, p: (b, 0, p)),
                   pl.BlockSpec((1, LANES, seq), lambda b, p: (b, p, 0)),
                   pl.BlockSpec((1, LANES, seq), lambda b, p: (b, p, 0))],
        out_shape=[jax.ShapeDtypeStruct((bsz, seq, A_W), F32),
                   jax.ShapeDtypeStruct((bsz, A_W, seq), F32), jax.ShapeDtypeStruct((bsz, A_W, seq), F32)],
        scratch_shapes=[pltpu.VMEM((seq, 2 * LANES), BF16),
                        pltpu.VMEM((2, nb, HALF + SUBLANES, MOBA_BLOCK), BF16),
                        pltpu.VMEM((nbp, LANES), F32), pltpu.VMEM((2, nb, SUBLANES, Q_BLOCK), F32),
                        pltpu.VMEM((2, HALF + SUBLANES, Q_BLOCK), F32),
                        pltpu.VMEM((2, MOBA_BLOCK, Q_BLOCK), F32), pltpu.VMEM((2, MOBA_BLOCK, Q_BLOCK), F32),
                        pltpu.VMEM((2, SUBLANES, Q_BLOCK), F32), pltpu.VMEM((2, SUBLANES, LANES), F32)],
        compiler_params=_cparams(("parallel", "parallel")),
        name="moba_prompt",
    )(proj3, proj3, proj3, slopes_pair)


def _moba_sample_kernel(pt_ref, q_ref, kn_ref, vn_ref, *refs, n_blk, ppb, page, tnew):
    del pt_ref
    n_pages = n_blk * ppb
    k_refs, v_refs = refs[:n_pages], refs[n_pages:2 * n_pages]
    o_ref, m_ref, l_ref, g_ref, acc_ref = refs[2 * n_pages:]
    past = n_blk * MOBA_BLOCK
    rows = A_HEADS * SUBLANES
    lane_head = lax.shift_right_logical(lax.broadcasted_iota(jnp.int32, (SUBLANES, A_W), 1), HALF_SHIFT)
    q8 = q_ref[0] * ATTN_SCALE
    qbd = jnp.concatenate([jnp.where(lane_head == h, q8, 0.0) for h in range(A_HEADS)], axis=0)
    q_hi = qbd.astype(BF16)
    q_lo = (qbd - q_hi.astype(F32)).astype(BF16)
    r = lax.broadcasted_iota(jnp.int32, (rows, 1), 0)
    tok = jnp.bitwise_and(r, SUBLANES - 1).astype(F32)
    slope = jnp.exp2(-(8.0 / A_HEADS) * (lax.shift_right_logical(r, SUBLANE_SHIFT) + 1).astype(F32))
    key = lax.broadcasted_iota(jnp.int32, (1, MOBA_BLOCK), 1).astype(F32)

    def block_t(refs_, n):
        return jnp.concatenate([refs_[j][0, 0].reshape(A_W, page) for j in range(n * ppb, (n + 1) * ppb)],
                               axis=1).astype(BF16)

    raws, fixes = [], []
    for n in range(n_blk):
        kt = block_t(k_refs, n)
        raws.append(jnp.dot(q_hi, kt, preferred_element_type=F32))
        fixes.append(jnp.dot(q_lo, kt, preferred_element_type=F32))
    es = []
    for n in range(n_blk):
        g_ref[n] = jnp.broadcast_to(jnp.sum(raws[n] + fixes[n], axis=-1, keepdims=True) * (1.0 / MOBA_BLOCK),
                                    (rows, LANES))
        s = raws[n] - slope * ((past - n * MOBA_BLOCK + tok) - key)
        m = jnp.max(s, axis=-1, keepdims=True)
        e = jnp.exp(s - m)
        m_ref[n] = jnp.broadcast_to(m, (rows, LANES))
        l_ref[n] = jnp.broadcast_to(jnp.sum(e, axis=-1, keepdims=True), (rows, LANES))
        es.append(e.astype(BF16))
    for n in range(n_blk):
        acc_ref[n] = lax.dot_general(es[n], block_t(v_refs, n), (((1,), (1,)), ((), ())),
                                     preferred_element_type=F32)

    lane = lax.broadcasted_iota(jnp.int32, (rows, LANES), 1)
    gate = jnp.zeros((rows, LANES), F32)
    for j in range(n_blk):
        gate = jnp.where(lane == j, g_ref[j], gate)
    bias = _topk_bias(gate, n_blk)

    kn = kn_ref[0]
    vn = vn_ref[0]
    s_own = []
    for j in range(tnew):
        sj = jnp.sum(qbd * kn[j:j + 1, :], axis=-1, keepdims=True) - slope * (tok - j)
        s_own.append(jnp.where(tok >= j, sj, NEG_INF))
    mx = s_own[0]
    for j in range(1, tnew):
        mx = jnp.maximum(mx, s_own[j])
    mb = []
    for j in range(n_blk):
        mj = m_ref[j][:, 0:1] + bias[:, j:j + 1]
        mb.append(mj)
        mx = jnp.maximum(mx, mj)
    lsum = jnp.zeros((rows, 1), F32)
    acc = jnp.zeros((rows, A_W), F32)
    for j in range(tnew):
        w = jnp.exp(s_own[j] - mx)
        lsum = lsum + w
        acc = acc + w * vn[j:j + 1, :]
    for j in range(n_blk):
        w = jnp.exp(mb[j] - mx)
        lsum = lsum + w * l_ref[j][:, 0:1]
        acc = acc + w * acc_ref[j]
    out = acc / lsum
    o8 = jnp.zeros((SUBLANES, A_W), F32)
    for h in range(A_HEADS):
        o8 = o8 + jnp.where(lane_head == h, out[h * SUBLANES:(h + 1) * SUBLANES, :], 0.0)
    o_ref[0] = o8


def _moba_sample(q8, k8, v8, cache_kt, cache_vt, page_table, layer, tnew):
    dbsz = q8.shape[0]
    n_pages = page_table.shape[1]
    page = cache_kt.shape[4]
    assert MOBA_BLOCK % page == 0 and (n_pages * page) % MOBA_BLOCK == 0 and page % LANES == 0
    ppb = MOBA_BLOCK // page
    n_blk = n_pages // ppb
    assert n_blk >= MOBA_TOPK and tnew <= SUBLANES
    rows = A_HEADS * SUBLANES
    tok_spec = pl.BlockSpec((1, SUBLANES, A_W), lambda b, pt: (b, 0, 0))

    def page_spec(j):
        return pl.BlockSpec((1, 1, A_HEADS, A_HD, page), lambda b, pt: (layer, pt[b * n_pages + j], 0, 0, 0))

    pages = [page_spec(j) for j in range(n_pages)]
    grid_spec = pltpu.PrefetchScalarGridSpec(
        num_scalar_prefetch=1,
        grid=(dbsz,),
        in_specs=[tok_spec, tok_spec, tok_spec] + pages + pages,
        out_specs=tok_spec,
        scratch_shapes=[pltpu.VMEM((n_blk, rows, LANES), F32), pltpu.VMEM((n_blk, rows, LANES), F32),
                        pltpu.VMEM((n_blk, rows, LANES), F32), pltpu.VMEM((n_blk, rows, A_W), F32)],
    )
    return pl.pallas_call(
        functools.partial(_moba_sample_kernel, n_blk=n_blk, ppb=ppb, page=page, tnew=tnew),
        grid_spec=grid_spec,
        out_shape=jax.ShapeDtypeStruct((dbsz, SUBLANES, A_W), F32),
        compiler_params=_cparams(("parallel",)),
        name="moba_sample",
    )(page_table.reshape(-1), q8, k8, v8, *([cache_kt] * n_pages), *([cache_vt] * n_pages))


def _ssd_prompt_kernel(*refs, L, n_sub):
    hout_ref, h_ref = refs[10], refs[12]

    def chunk(j, carry):
        _ssd_chunk(pl.program_id(1) * n_sub + j, pl.ds(pl.multiple_of(j * L, L), L), *refs[:10], *refs[11:], L=L)
        return carry

    lax.fori_loop(0, n_sub, chunk, 0)

    @pl.when(pl.program_id(1) == pl.num_programs(1) - 1)
    def _():
        hout_ref[0] = h_ref[...]


def _ssd_chunk(c, rows, z_ref, xbc_ref, dt_ref, cw_ref, cb_ref, dtb_ref, alog_ref, dvec_ref, ng_ref,
               o_ref, ext_ref, h_ref, *, L):
    P = SUBLANES

    @pl.when(c == 0)
    def _():
        ext_ref[0:P, :] = jnp.zeros((P, S_CONV_CH), F32)
        h_ref[...] = jnp.zeros_like(h_ref)

    @pl.when(c > 0)
    def _():
        ext_ref[0:P, :] = ext_ref[L:L + P, :]

    ext_ref[P:P + L, :] = xbc_ref[rows, :]
    acc = cb_ref[...] + cw_ref[S_CONV - 1:S_CONV, :] * ext_ref[P:P + L, :]
    for d in range(1, S_CONV):
        acc = acc + cw_ref[S_CONV - 1 - d:S_CONV - d, :] * ext_ref[P - d:P - d + L, :]
    xbc = _silu(acc)
    xs = xbc[:, :S_W]
    bm = xbc[:, S_W:S_W + S_GROUPS * S_STATE]
    cm = xbc[:, S_W + S_GROUPS * S_STATE:]

    lane = lax.broadcasted_iota(jnp.int32, (1, LANES), 1)
    is_a = lane < HALF
    dt = jnp.where(lane < S_HEADS, _softplus(dt_ref[rows, :] + dtb_ref[...]), 0.0)
    da = dt * (-jnp.exp(alog_ref[...]))
    rr = lax.broadcasted_iota(jnp.int32, (L, L), 0)
    cc = lax.broadcasted_iota(jnp.int32, (L, L), 1)
    causal = rr >= cc
    cum = _split_dot(da, jnp.where(causal, 1.0, 0.0), 3, x_is_lhs=False)
    cum_t = cum.T
    dt_t = dt.T
    row_a = lax.broadcasted_iota(jnp.int32, (2 * S_HD, 1), 0) < S_HD

    group_of = [(2 * pr * S_GROUPS) // S_HEADS for pr in range(N_PAIRS)]
    bgs = [bm[:, g * S_STATE:(g + 1) * S_STATE] for g in range(S_GROUPS)]
    cgs = [cm[:, g * S_STATE:(g + 1) * S_STATE] for g in range(S_GROUPS)]
    cb_g = [_dot_nt(cgs[g], bgs[g]) for g in range(S_GROUPS)]
    xs_ps = [xs[:, pr * LANES:(pr + 1) * LANES] for pr in range(N_PAIRS)]
    hps = [h_ref[pr] for pr in range(N_PAIRS)]
    halves = (is_a, jnp.logical_not(is_a))
    cum_c = [cum[:, h:h + 1] for h in range(S_HEADS)]
    last = [cum[L - 1:L, h:h + 1] for h in range(S_HEADS)]
    wts = []
    for h in range(S_HEADS):
        seg = cum_c[h] - cum_t[h:h + 1, :]
        dec = jnp.exp(jnp.where(causal, seg, -jnp.inf))
        wts.append(cb_g[group_of[h // 2]] * dec * dt_t[h:h + 1, :])
    intra = [_dot(wts[h], jnp.where(halves[h % 2], xs_ps[h // 2], 0.0)) for h in range(S_HEADS)]
    inter = [_dot_nt(cgs[group_of[pr]], hps[pr]) for pr in range(N_PAIRS)]
    ys = [intra[2 * pr] + intra[2 * pr + 1]
          + inter[pr] * jnp.where(is_a, jnp.exp(cum_c[2 * pr]), jnp.exp(cum_c[2 * pr + 1])) for pr in range(N_PAIRS)]
    tes = [jnp.where(is_a, jnp.exp(last[2 * pr] - cum_c[2 * pr]) * dt[:, 2 * pr:2 * pr + 1],
                     jnp.exp(last[2 * pr + 1] - cum_c[2 * pr + 1]) * dt[:, 2 * pr + 1:2 * pr + 2])
           for pr in range(N_PAIRS)]
    sts = [_dot_tn(xs_ps[pr] * tes[pr], bgs[group_of[pr]]) for pr in range(N_PAIRS)]
    for pr in range(N_PAIRS):
        h_ref[pr] = hps[pr] * jnp.where(row_a, jnp.exp(last[2 * pr]), jnp.exp(last[2 * pr + 1])) + sts[pr]
    y = jnp.concatenate(ys, axis=1) + dvec_ref[...] * xs
    yz = y * _silu(z_ref[rows, :])
    gw = S_W // S_GROUPS
    for g in range(S_GROUPS):
        part = yz[:, g * gw:(g + 1) * gw]
        ms = jnp.mean(part * part, axis=-1, keepdims=True)
        o_ref[rows, g * gw:(g + 1) * gw] = part * lax.rsqrt(ms + NORM_EPS) * ng_ref[:, g * gw:(g + 1) * gw]


def _ssd_prompt(proj, bsz, seq, cw, cb, dtb, alog, dvec, ng):
    L = SSD_CHUNK
    n_sub = 4 if seq % (4 * L) == 0 else 1
    nc = seq // (L * n_sub)
    rows = L * n_sub
    const = lambda shape: pl.BlockSpec(shape, lambda b, c: (0,) * len(shape))
    return pl.pallas_call(
        functools.partial(_ssd_prompt_kernel, L=L, n_sub=n_sub),
        grid=(bsz, nc),
        in_specs=[pl.BlockSpec((rows, S_W), lambda b, c: (b * nc + c, C_Z // S_W)),
                  pl.BlockSpec((rows, S_CONV_CH), lambda b, c: (b * nc + c, C_XBC // S_CONV_CH)),
                  pl.BlockSpec((rows, LANES), lambda b, c: (b * nc + c, C_DT // LANES)),
                  const((S_CONV, S_CONV_CH)), const((1, S_CONV_CH)), const((1, LANES)), const((1, LANES)),
                  const((1, S_W)), const((1, S_W))],
        out_specs=[pl.BlockSpec((rows, S_W), lambda b, c: (b * nc + c, 0)),
                   pl.BlockSpec((1, N_PAIRS, 2 * S_HD, S_STATE), lambda b, c: (b, 0, 0, 0))],
        out_shape=[jax.ShapeDtypeStruct((bsz * seq, S_W), F32),
                   jax.ShapeDtypeStruct((bsz, N_PAIRS, 2 * S_HD, S_STATE), F32)],
        scratch_shapes=[pltpu.VMEM((L + 2 * SUBLANES, S_CONV_CH), F32),
                        pltpu.VMEM((N_PAIRS, 2 * S_HD, S_STATE), F32)],
        compiler_params=_cparams(("parallel", "arbitrary")),
        name="ssd_prompt",
    )(proj, proj, proj, cw, cb, dtb, alog, dvec, ng)


def _ssd_sample_kernel(z_ref, xbc_ref, dt_ref, pre_ref, h0_ref, cw_ref, cb_ref, dtbv_ref, alogv_ref,
                       dvec_ref, ng_ref, o_ref, hout_ref, *, T, nb):
    rows2 = 2 * S_HD
    eye = (lax.broadcasted_iota(jnp.int32, (rows2, LANES), 0)
           == lax.broadcasted_iota(jnp.int32, (rows2, LANES), 1)).astype(F32)
    eye_all = jnp.concatenate([eye] * nb, axis=0)
    ones = jnp.ones((LANES, LANES), F32)
    spread = (lax.broadcasted_iota(jnp.int32, (LANES, S_W), 0)
              == lax.shift_right_logical(lax.broadcasted_iota(jnp.int32, (LANES, S_W), 1), HALF_SHIFT)).astype(F32)
    a_vec = -jnp.exp(alogv_ref[...])
    gw = S_W // S_GROUPS

    def as_columns(x, passes):
        return _split_dot(eye_all * _repeat_rows(x, rows2), ones, passes)

    up = [pre_ref[j] for j in range(S_CONV - 1)] + [xbc_ref[t] for t in range(T)]
    hs = [h0_ref[0, :, pr].reshape(nb * rows2, S_STATE) for pr in range(N_PAIRS)]
    for t in range(T):
        acc = cb_ref[...] + cw_ref[0:1, :] * up[t]
        for j in range(1, S_CONV):
            acc = acc + cw_ref[j:j + 1, :] * up[t + j]
        xc = _silu(acc)
        xs = xc[:, :S_W]
        dt = _softplus(_split_dot(dt_ref[t], spread, 3) + dtbv_ref[...])
        decay = jnp.exp(dt * a_vec)
        xdt = xs * dt
        ys = []
        for pr in range(N_PAIRS):
            g = (2 * pr * S_GROUPS) // S_HEADS
            sl = slice(pr * LANES, (pr + 1) * LANES)
            b_rows = _repeat_rows(xc[:, S_W + g * S_STATE:S_W + (g + 1) * S_STATE], rows2)
            c_rows = _repeat_rows(xc[:, S_W + (S_GROUPS + g) * S_STATE:S_W + (S_GROUPS + g + 1) * S_STATE], rows2)
            hs[pr] = hs[pr] * as_columns(decay[:, sl], 3) + as_columns(xdt[:, sl], 2) * b_rows
            y_col = _split_dot(hs[pr] * c_rows, ones, 2)
            ys.append(jnp.sum((eye_all * y_col).reshape(nb, rows2, LANES), axis=1))
        y = jnp.concatenate(ys, axis=1) + dvec_ref[...] * xs
        yz = y * _silu(z_ref[t])
        for g in range(S_GROUPS):
            part = yz[:, g * gw:(g + 1) * gw]
            ms = jnp.mean(part * part, axis=-1, keepdims=True)
            o_ref[t, :, g * gw:(g + 1) * gw] = part * lax.rsqrt(ms + NORM_EPS) * ng_ref[:, g * gw:(g + 1) * gw]
    for pr in range(N_PAIRS):
        hout_ref[:, pr] = hs[pr].reshape(nb, rows2, S_STATE)


def _ssd_sample(proj3, pre3, h0, layer, cw, cb, dtbv, alogv, dvec, ng, nb=8):
    T, dbsz = proj3.shape[:2]
    const = lambda shape: pl.BlockSpec(shape, lambda j: (0,) * len(shape))
    hspec = pl.BlockSpec((nb, N_PAIRS, 2 * S_HD, S_STATE), lambda j: (j, 0, 0, 0))
    h0spec = pl.BlockSpec((1, nb, N_PAIRS, 2 * S_HD, S_STATE), lambda j: (layer, j, 0, 0, 0))
    return pl.pallas_call(
        functools.partial(_ssd_sample_kernel, T=T, nb=nb),
        grid=(dbsz // nb,),
        in_specs=[pl.BlockSpec((T, nb, S_W), lambda j: (0, j, C_Z // S_W)),
                  pl.BlockSpec((T, nb, S_CONV_CH), lambda j: (0, j, C_XBC // S_CONV_CH)),
                  pl.BlockSpec((T, nb, LANES), lambda j: (0, j, C_DT // LANES)),
                  pl.BlockSpec((S_CONV - 1, nb, S_CONV_CH), lambda j: (0, j, 0)),
                  h0spec,
                  const((S_CONV, S_CONV_CH)), const((1, S_CONV_CH)), const((1, S_W)), const((1, S_W)),
                  const((1, S_W)), const((1, S_W))],
        out_specs=[pl.BlockSpec((T, nb, S_W), lambda j: (0, j, 0)), hspec],
        out_shape=[jax.ShapeDtypeStruct((T, dbsz, S_W), F32),
                   jax.ShapeDtypeStruct((dbsz, N_PAIRS, 2 * S_HD, S_STATE), F32)],
        compiler_params=_cparams(("parallel",)),
        name="ssd_sample",
    )(proj3, proj3, proj3, pre3, h0, cw, cb, dtbv, alogv, dvec, ng)


def _rwkv_prep_kernel(u_ref, pre_ref, mu_ref, w0_ref, w2_ref, a0_ref, a2_ref, g2_ref, kk_ref, ka_ref,
                      r_out, w_out, k_out, v_out, kk_out, kka_out, g_out, ext_ref,
                      *, tm, P, stride, tiles_per_seq):
    i = pl.program_id(0)
    first = (i % tiles_per_seq) == 0

    @pl.when(first)
    def _():
        ext_ref[0:P, :] = pre_ref[...]

    @pl.when(jnp.logical_not(first))
    def _():
        ext_ref[0:P, :] = ext_ref[tm:tm + P, :]

    u = u_ref[...]
    ext_ref[P:P + tm, :] = u
    prev = ext_ref[P - stride:P - stride + tm, :]
    x = u + (prev - u) * mu_ref[...]
    r = x[:, 0:R_W]
    kr = x[:, R_W:2 * R_W]
    vr = x[:, 2 * R_W:3 * R_W]
    xl = x[:, 3 * R_W:]
    w_log = w0_ref[...] + _dot(jnp.tanh(xl), w2_ref[...])
    log_decay = -jnp.exp(-_softplus(-w_log) - 0.5)
    a = _sigmoid(a0_ref[...] + _dot(xl, a2_ref[...]))
    g = _dot(_sigmoid(xl), g2_ref[...])
    kk = kr * kk_ref[...]
    ss = _split_dot(kk * kk, _head_ones(R_W, 1.0), 2)
    kk = kk / jnp.maximum(jnp.sqrt(ss), 1e-12)
    r_out[...] = r
    w_out[...] = log_decay
    k_out[...] = kr * (1.0 + (a - 1.0) * ka_ref[...])
    v_out[...] = vr
    kk_out[...] = kk
    kka_out[...] = kk * a
    g_out[...] = g


def _rwkv_prep(proj, pre, mu, w0, w2p, a0, a2p, g2p, kkw, kaw, tm, stride, tiles_per_seq):
    t = proj.shape[0]
    P = pre.shape[0]
    const = lambda shape: pl.BlockSpec(shape, lambda i: (0,) * len(shape))
    outs = pl.pallas_call(
        functools.partial(_rwkv_prep_kernel, tm=tm, P=P, stride=stride, tiles_per_seq=tiles_per_seq),
        grid=(t // tm,),
        in_specs=[pl.BlockSpec((tm, R_IN_W), lambda i: (i, 0)),
                  const((P, R_IN_W)), const((1, R_IN_W)), const((1, R_W)), const((R_LORA, R_W)),
                  const((1, R_W)), const((R_LORA, R_W)), const((R_LORA, R_W)), const((1, R_W)), const((1, R_W))],
        out_specs=[pl.BlockSpec((tm, R_W), lambda i: (i, 0))] * 7,
        out_shape=[jax.ShapeDtypeStruct((t, R_W), F32)] * 7,
        scratch_shapes=[pltpu.VMEM((tm + 2 * P, R_IN_W), F32)],
        compiler_params=_cparams(("arbitrary",)),
        name="rwkv_prep",
    )(proj, pre, mu, w0, w2p, a0, a2p, g2p, kkw, kaw)
    return outs


def _pair_consts():
    lane = lax.broadcasted_iota(jnp.int32, (R_HD, LANES), 1)
    row = lax.broadcasted_iota(jnp.int32, (R_HD, LANES), 0)
    is_a = lane < HALF
    eye2 = (jnp.bitwise_and(lane, HALF - 1) == row).astype(F32)
    return is_a, eye2


def _unit_lower_inverses(ns):
    L = ns[0].shape[0]
    eye = (lax.broadcasted_iota(jnp.int32, (L, L), 0) == lax.broadcasted_iota(jnp.int32, (L, L), 1)).astype(F32)
    ts = [eye + n for n in ns]
    pws = [_dot(n, n) for n in ns]
    for _ in range(int(math.log2(L)) - 2):
        both = [_dot(jnp.concatenate([t, p], axis=0), p) for t, p in zip(ts, pws)]
        ts = [t + b[:L] for t, b in zip(ts, both)]
        pws = [b[L:] for b in both]
    return [t + _dot(t, p) for t, p in zip(ts, pws)]


def _rwkv_chunk_prompt_kernel(*refs, L, n_sub):
    sout_ref, s_ref = refs[7], refs[8]

    @pl.when(pl.program_id(1) == 0)
    def _():
        s_ref[...] = jnp.zeros_like(s_ref)

    def chunk(j, carry):
        _rwkv_chunk(pl.ds(pl.multiple_of(j * L, L), L), *refs[:7], s_ref, L=L)
        return carry

    lax.fori_loop(0, n_sub, chunk, 0)

    @pl.when(pl.program_id(1) == pl.num_programs(1) - 1)
    def _():
        sout_ref[0] = s_ref[...]


def _rwkv_chunk(rows, r_ref, lw_ref, k_ref, v_ref, kk_ref, kka_ref, y_ref, s_ref, *, L):
    lane = lax.broadcasted_iota(jnp.int32, (1, LANES), 1)
    is_a = lane < HALF
    rr = lax.broadcasted_iota(jnp.int32, (L, L), 0)
    cc = lax.broadcasted_iota(jnp.int32, (L, L), 1)
    incl = rr >= cc
    strict = rr > cc
    tri = jnp.where(incl, 1.0, 0.0).astype(F32)
    r2 = lax.broadcasted_iota(jnp.int32, (LANES, LANES), 0) < HALF
    c2 = lax.broadcasted_iota(jnp.int32, (LANES, LANES), 1) < HALF
    same_head = r2 == c2

    r, lw, k, v, kk, kka = (ref[rows, :] for ref in (r_ref, lw_ref, k_ref, v_ref, kk_ref, kka_ref))
    cum = _split_dot(lw, tri, 3, x_is_lhs=False)
    last = cum[L - 1:L, :]
    inv_p = jnp.exp(-cum)
    to_end = jnp.exp(last - cum)
    b_t = kk * jnp.exp(cum - lw)
    a_t = -kka * inv_p
    k_t = k * inv_p
    r_t = r * jnp.exp(cum)
    a_end = -kka * to_end
    k_end = k * to_end
    decay_end = jnp.exp(last)
    pairs = [slice(pr * LANES, (pr + 1) * LANES) for pr in range(N_PAIRS)]
    halves = (is_a, jnp.logical_not(is_a))
    s0 = [s_ref[pr] for pr in range(N_PAIRS)]

    ns, mks, rak = [], [], []
    for sl in pairs:
        cols = jnp.concatenate([a_t[:, sl], k_t[:, sl]], axis=0)
        for half in halves:
            lhs = jnp.concatenate([jnp.where(half, b_t[:, sl], 0.0), jnp.where(half, r_t[:, sl], 0.0)], axis=0)
            g4 = _dot_nt(lhs, cols)
            ns.append(jnp.where(strict, g4[:L, :L], 0.0))
            mks.append(jnp.where(strict, g4[:L, L:], 0.0))
            rak.append(jnp.concatenate([jnp.where(incl, g4[L:, :L], 0.0), jnp.where(incl, g4[L:, L:], 0.0)], axis=1))
    ts = _unit_lower_inverses(ns)
    from_state = [_dot_nt(jnp.concatenate([b_t[:, sl], r_t[:, sl]], axis=0), s0[pr]) for pr, sl in enumerate(pairs)]
    rhs = []
    for pr, sl in enumerate(pairs):
        mv = _dot(jnp.concatenate([mks[2 * pr], mks[2 * pr + 1]], axis=0), v[:, sl])
        rhs.append(from_state[pr][:L] + jnp.where(is_a, mv[:L], mv[L:]))
    us = []
    for pr in range(N_PAIRS):
        tu = _dot(jnp.concatenate([ts[2 * pr], ts[2 * pr + 1]], axis=0), rhs[pr])
        us.append(jnp.where(is_a, tu[:L], tu[L:]))
    for pr, sl in enumerate(pairs):
        uv = jnp.concatenate([us[pr], v[:, sl]], axis=0)
        y_ref[rows, sl] = from_state[pr][L:] + jnp.where(is_a, _dot(rak[2 * pr], uv), _dot(rak[2 * pr + 1], uv))
    for pr, sl in enumerate(pairs):
        upd = _dot_tn(jnp.concatenate([us[pr], v[:, sl]], axis=0),
                      jnp.concatenate([a_end[:, sl], k_end[:, sl]], axis=0))
        s_ref[pr] = s0[pr] * decay_end[:, sl] + jnp.where(same_head, upd, 0.0)


def _rwkv_scan_prompt(seqs, bsz, seq, L):
    n_sub = 4 if seq % (4 * L) == 0 else 1
    nc = seq // (L * n_sub)
    spec = pl.BlockSpec((L * n_sub, R_W), lambda b, c: (b * nc + c, 0))
    return pl.pallas_call(
        functools.partial(_rwkv_chunk_prompt_kernel, L=L, n_sub=n_sub),
        grid=(bsz, nc),
        in_specs=[spec] * 6,
        out_specs=[spec, pl.BlockSpec((1, N_PAIRS, LANES, LANES), lambda b, c: (b, 0, 0, 0))],
        out_shape=[jax.ShapeDtypeStruct((bsz * seq, R_W), F32),
                   jax.ShapeDtypeStruct((bsz, N_PAIRS, LANES, LANES), F32)],
        scratch_shapes=[pltpu.VMEM((N_PAIRS, LANES, LANES), F32)],
        compiler_params=_cparams(("parallel", "arbitrary")),
        name="rwkv_scan_prompt",
    )(*seqs)


def _repeat_rows(x, reps):
    return jnp.concatenate([jnp.broadcast_to(x[b:b + 1, :], (reps, x.shape[1])) for b in range(x.shape[0])], axis=0)


def _rwkv_scan_sample_kernel(r_ref, w_ref, k_ref, v_ref, kk_ref, kka_ref, s0_ref, y_ref, sout_ref, *, T, nb):
    _, eye2 = _pair_consts()
    eye_all = jnp.concatenate([eye2] * nb, axis=0)
    ones_bd = _head_ones(LANES, 1.0)

    def head_sums(x):
        return _split_dot(x, ones_bd, 2)

    states = [s0_ref[0, :, pr].reshape(nb * R_HD, LANES) for pr in range(N_PAIRS)]
    for t in range(T):
        for pr in range(N_PAIRS):
            sl = slice(pr * LANES, (pr + 1) * LANES)
            rr, ww, kr, vv, kk, kka = (_repeat_rows(x, R_HD) for x in (
                r_ref[t, :, sl], jnp.exp(w_ref[t, :, sl]), k_ref[t, :, sl], v_ref[t, :, sl],
                kk_ref[t, :, sl], kka_ref[t, :, sl]))
            v_col = head_sums(eye_all * vv)
            s = states[pr]
            s = s * ww - head_sums(s * kk) * kka + v_col * kr
            states[pr] = s
            y_col = head_sums(s * rr)
            y_ref[t, :, sl] = jnp.sum((eye_all * y_col).reshape(nb, R_HD, LANES), axis=1)
    for pr in range(N_PAIRS):
        sout_ref[:, pr] = states[pr].reshape(nb, R_HD, LANES)


def _rwkv_scan_sample(seqs3, s0, layer, nb=16):
    T, dbsz = seqs3[0].shape[:2]
    spec = pl.BlockSpec((T, nb, R_W), lambda j: (0, j, 0))
    sspec = pl.BlockSpec((nb, N_PAIRS, R_HD, LANES), lambda j: (j, 0, 0, 0))
    s0spec = pl.BlockSpec((1, nb, N_PAIRS, R_HD, LANES), lambda j: (layer, j, 0, 0, 0))
    return pl.pallas_call(
        functools.partial(_rwkv_scan_sample_kernel, T=T, nb=nb),
        grid=(dbsz // nb,),
        in_specs=[spec] * 6 + [s0spec],
        out_specs=[spec, sspec],
        out_shape=[jax.ShapeDtypeStruct((T, dbsz, R_W), F32),
                   jax.ShapeDtypeStruct((dbsz, N_PAIRS, R_HD, LANES), F32)],
        compiler_params=_cparams(("parallel",)),
        name="rwkv_scan_sample",
    )(*seqs3, s0)


def _merge_kernel(x_ref, oa_ref, ob_ref, yr_ref, r_ref, k_ref, v_ref, g_ref, ga_ref, gb_ref, gc_ref,
                  bg_ref, lng_ref, lnb_ref, rk_ref, wpa_ref, wpb_ref, wpc_ref, wo_ref, o_ref):
    pa = _dot(oa_ref[...], wpa_ref[...])
    pb = _dot(ob_ref[...], wpb_ref[...])
    mean_m = _head_ones(R_W, 1.0 / R_HD)
    yr = yr_ref[...]
    bonus = _split_dot(r_ref[...] * k_ref[...] * rk_ref[...], _head_ones(R_W, 1.0), 2)
    d = yr - _split_dot(yr, mean_m, 2)
    var = _split_dot(d * d, mean_m, 2)
    yn = d * lax.rsqrt(var + GN_EPS) * lng_ref[...] + lnb_ref[...]
    oc = (yn + bonus * v_ref[...]) * g_ref[...]
    merged = (_sigmoid(ga_ref[...] + bg_ref[:, 0:D_MODEL]) * pa
              + _sigmoid(gb_ref[...] + bg_ref[:, D_MODEL:2 * D_MODEL]) * pb
              + _sigmoid(gc_ref[...] + bg_ref[:, 2 * D_MODEL:]) * _dot(oc, wpc_ref[...]))
    o_ref[...] = x_ref[...] + _dot(merged, wo_ref[...])


def _merge(x, proj, oa, ob, yr, r, k2, v, g, bg, lng, lnb, rk, wpa, wpb, wpc, wo, tm):
    t = x.shape[0]
    row = lambda w: pl.BlockSpec((tm, w), lambda i: (i, 0))
    const = lambda shape: pl.BlockSpec(shape, lambda i: (0,) * len(shape))
    gcol = C_GATE // D_MODEL
    gate = lambda j: pl.BlockSpec((tm, D_MODEL), lambda i: (i, gcol + j))
    return pl.pallas_call(
        _merge_kernel,
        grid=(t // tm,),
        in_specs=[row(D_MODEL), row(A_W), row(S_W), row(R_W), row(R_W), row(R_W), row(R_W), row(R_W),
                  gate(0), gate(1), gate(2),
                  const((1, 3 * D_MODEL)), const((1, R_W)), const((1, R_W)), const((1, R_W)),
                  const((A_W, D_MODEL)), const((S_W, D_MODEL)), const((R_W, D_MODEL)),
                  const((D_MODEL, D_MODEL))],
        out_specs=row(D_MODEL),
        out_shape=jax.ShapeDtypeStruct((t, D_MODEL), F32),
        compiler_params=_cparams(("parallel",)),
        name="merge",
    )(x, oa, ob, yr, r, k2, v, g, proj, proj, proj, bg, lng, lnb, rk, wpa, wpb, wpc, wo)


def _ffn_kernel(x_ref, g_ref, wug_ref, wuv_ref, wd_ref, cw_ref, cb_ref, pre_ref, gf_ref,
                o_ref, tail_ref, xn_ref, ext_ref, *, tm, P, stride, tiles_per_seq, final_norm):
    i = pl.program_id(0)
    f = pl.program_id(1)
    first = (i % tiles_per_seq) == 0

    @pl.when(f == 0)
    def _():
        x = x_ref[...]
        ms = jnp.mean(x * x, axis=-1, keepdims=True)
        xn_ref[...] = (x * lax.rsqrt(ms + NORM_EPS) * g_ref[...]).astype(BF16)

    @pl.when(first)
    def _():
        ext_ref[f, 0:P, :] = pre_ref[...]

    @pl.when(jnp.logical_not(first))
    def _():
        ext_ref[f, 0:P, :] = ext_ref[f, tm:tm + P, :]

    xn = xn_ref[...]
    ug = jnp.dot(xn, wug_ref[...], preferred_element_type=F32)
    uv = jnp.dot(xn, wuv_ref[...], preferred_element_type=F32)
    ext_ref[f, P:P + tm, :] = ug
    tail_ref[0] = ext_ref[f, tm:tm + P, :]
    acc = cb_ref[...] + cw_ref[F_CONV - 1:F_CONV, :] * ug
    for d in range(1, F_CONV):
        acc = acc + cw_ref[F_CONV - 1 - d:F_CONV - d, :] * ext_ref[f, P - d * stride:P - d * stride + tm, :]
    contrib = _dot(_silu(acc) * uv, wd_ref[...])

    @pl.when(f == 0)
    def _():
        o_ref[...] = x_ref[...] + contrib

    @pl.when(f > 0)
    def _():
        o_ref[...] = o_ref[...] + contrib

    if final_norm:
        @pl.when(f == pl.num_programs(1) - 1)
        def _():
            y = o_ref[...]
            ms = jnp.mean(y * y, axis=-1, keepdims=True)
            o_ref[...] = y * lax.rsqrt(ms + NORM_EPS) * gf_ref[...]


def _ffn(x, g, wup, wd, cw, cb, pre, gf, tm, tf, stride, tiles_per_seq, final_norm):
    t = x.shape[0]
    P = pre.shape[0]
    nf = D_FF // tf
    resident = dict(pipeline_mode=pl.Buffered(1)) if nf == 1 else {}
    return pl.pallas_call(
        functools.partial(_ffn_kernel, tm=tm, P=P, stride=stride, tiles_per_seq=tiles_per_seq,
                          final_norm=final_norm),
        grid=(t // tm, nf),
        in_specs=[pl.BlockSpec((tm, D_MODEL), lambda i, f: (i, 0)),
                  pl.BlockSpec((1, D_MODEL), lambda i, f: (0, 0)),
                  pl.BlockSpec((D_MODEL, tf), lambda i, f: (0, f), **resident),
                  pl.BlockSpec((D_MODEL, tf), lambda i, f: (0, nf + f), **resident),
                  pl.BlockSpec((tf, D_MODEL), lambda i, f: (f, 0), **resident),
                  pl.BlockSpec((F_CONV, tf), lambda i, f: (0, f)),
                  pl.BlockSpec((1, tf), lambda i, f: (0, f)),
                  pl.BlockSpec((P, tf), lambda i, f: (0, f)),
                  pl.BlockSpec((1, D_MODEL), lambda i, f: (0, 0))],
        out_specs=[pl.BlockSpec((tm, D_MODEL), lambda i, f: (i, 0)),
                   pl.BlockSpec((1, P, tf), lambda i, f: (i, 0, f))],
        out_shape=[jax.ShapeDtypeStruct((t, D_MODEL), F32),
                   jax.ShapeDtypeStruct((t // tm, P, D_FF), F32)],
        scratch_shapes=[pltpu.VMEM((tm, D_MODEL), BF16), pltpu.VMEM((nf, tm + 2 * P, tf), F32)],
        compiler_params=_cparams(("arbitrary", "arbitrary")),
        name="conv_ffn",
    )(x, g, wup, wup, wd, cw, cb, pre, gf)


def _pack_rwkv_state(s):
    d, n = s.shape[:2]
    return s.reshape(d, n, N_PAIRS, 2, R_HD, R_HD).transpose(0, 1, 2, 4, 3, 5).reshape(d, n, N_PAIRS, R_HD, LANES)


def _unpack_rwkv_state(s):
    n = s.shape[0]
    return s.reshape(n, N_PAIRS, R_HD, 2, R_HD).transpose(0, 1, 3, 2, 4).reshape(n, R_HEADS, R_HD, R_HD)


def _unpack_rwkv_blockdiag(s):
    n = s.shape[0]
    return jnp.stack([s[:, :, :R_HD, :R_HD], s[:, :, R_HD:, R_HD:]], axis=2).reshape(n, R_HEADS, R_HD, R_HD)


def _prep_layer_params(l, p):
    w_in = p['w_in'][l]
    c_dt_src = 3 * A_W + S_W + S_CONV_CH
    c_rw_src = c_dt_src + S_HEADS
    c_gate_src = c_rw_src + R_IN_W
    w_proj = jnp.concatenate([
        w_in[:, c_rw_src:c_gate_src],
        w_in[:, c_dt_src:c_rw_src], jnp.zeros((D_MODEL, C_Q - C_DT - S_HEADS), F32),
        w_in[:, :c_dt_src],
        w_in[:, c_gate_src:]], axis=1).astype(BF16)
    pad_lane = lambda v: jnp.pad(v, (0, LANES - v.shape[0])).reshape(1, LANES)
    zl = lambda r0, w: jnp.zeros((R_LORA, R_W), F32).at[r0:r0 + w.shape[0]].set(w).astype(BF16)
    return dict(
        norm1_g=p['norm1_g'][l].reshape(1, D_MODEL), w_proj=w_proj,
        b_gate=p['b_gate'][l].reshape(1, 3 * D_MODEL),
        w_pa=p['w_pa'][l].astype(BF16), w_pb=p['w_pb'][l].astype(BF16), w_pc=p['w_pc'][l].astype(BF16),
        w_o=p['w_o'][l].astype(BF16),
        ssm_conv_w=p['ssm_conv_w'][l], ssm_conv_b=p['ssm_conv_b'][l].reshape(1, S_CONV_CH),
        ssm_dt_bias=pad_lane(p['ssm_dt_bias'][l]), ssm_a_log=pad_lane(p['ssm_a_log'][l]),
        ssm_dvec=jnp.repeat(p['ssm_d'][l], S_HD).reshape(1, S_W),
        ssm_dtb_vec=jnp.repeat(p['ssm_dt_bias'][l], S_HD).reshape(1, S_W),
        ssm_alog_vec=jnp.repeat(p['ssm_a_log'][l], S_HD).reshape(1, S_W),
        ssm_norm_g=p['ssm_norm_g'][l].reshape(1, S_W),
        rw_mu=p['rw_mu'][l].reshape(1, R_IN_W), rw_w0=p['rw_w0'][l].reshape(1, R_W),
        rw_w2p=zl(0, p['rw_w2'][l]), rw_a0=p['rw_a0'][l].reshape(1, R_W),
        rw_a2p=zl(R_LORA_W, p['rw_a2'][l]), rw_g2p=zl(R_LORA_W + R_LORA_A, p['rw_g2'][l]),
        rw_kk=p['rw_kk'][l].reshape(1, R_W), rw_ka=p['rw_ka'][l].reshape(1, R_W),
        rw_rk=p['rw_rk'][l].reshape(1, R_W), rw_ln_g=p['rw_ln_g'][l].reshape(1, R_W),
        rw_ln_b=p['rw_ln_b'][l].reshape(1, R_W),
        norm2_g=p['norm2_g'][l].reshape(1, D_MODEL), w_up=p['w_up'][l].astype(BF16),
        w_down=p['w_down'][l].astype(BF16), ffn_conv_w=p['ffn_conv_w'][l],
        ffn_conv_b=p['ffn_conv_b'][l].reshape(1, D_FF))


def _row_tile(t):
    for tm in (512, 256, 128):
        if t % tm == 0:
            return tm
    raise ValueError(t)


def _ffn_tf():
    return D_FF // 2


def _prompt_layer(x, lp, bsz, seq, slopes_pair, gf, final_norm):
    t = bsz * seq
    tm = _row_tile(seq)
    tiles = seq // tm
    proj = _rms_matmul(x, lp['norm1_g'], lp['w_proj'], 2048 if t % 2048 == 0 else tm, 1024)
    proj3 = proj.reshape(bsz, seq, N_PROJ)
    oa, k_t, v_t = _moba_prompt(proj3, slopes_pair)
    oa = oa.reshape(t, A_W)
    ob, ssm_new = _ssd_prompt(proj, bsz, seq, lp['ssm_conv_w'], lp['ssm_conv_b'], lp['ssm_dt_bias'],
                              lp['ssm_a_log'], lp['ssm_dvec'], lp['ssm_norm_g'])
    r, w, k2, v, kk, kka, g = _rwkv_prep(
        proj, jnp.zeros((SUBLANES, R_IN_W), F32), lp['rw_mu'], lp['rw_w0'], lp['rw_w2p'], lp['rw_a0'],
        lp['rw_a2p'], lp['rw_g2p'], lp['rw_kk'], lp['rw_ka'], tm, 1, tiles)
    yr, rw_new = _rwkv_scan_prompt((r, w, k2, v, kk, kka), bsz, seq, LANES)
    x = _merge(x, proj, oa, ob, yr, r, k2, v, g, lp['b_gate'], lp['rw_ln_g'], lp['rw_ln_b'], lp['rw_rk'],
               lp['w_pa'], lp['w_pb'], lp['w_pc'], lp['w_o'], min(tm, 256))
    ftm = min(tm, 256)
    ftiles = seq // ftm
    x, tail = _ffn(x, lp['norm2_g'], lp['w_up'], lp['w_down'], lp['ffn_conv_w'], lp['ffn_conv_b'],
                   jnp.zeros((SUBLANES, D_FF), F32), gf, ftm, D_FF, 1, ftiles, final_norm)
    k_new = k_t.reshape(bsz, A_HEADS, A_HD, seq).transpose(0, 3, 1, 2)
    v_new = v_t.reshape(bsz, A_HEADS, A_HD, seq).transpose(0, 3, 1, 2)
    ssm_conv_new = proj3[:, seq - (S_CONV - 1):, C_XBC:C_XBC + S_CONV_CH]
    shift_new = proj3[:, seq - 1:, C_RW:C_RW + R_IN_W]
    ffn_conv_new = tail.reshape(bsz, ftiles, SUBLANES, D_FF)[:, ftiles - 1, SUBLANES - (F_CONV - 1):]
    state = (k_new, v_new, ssm_new.reshape(bsz, S_HEADS, S_HD, S_STATE), ssm_conv_new,
             _unpack_rwkv_blockdiag(rw_new), shift_new, ffn_conv_new)
    return x, state


def _sample_layer(x, lp, dbsz, tnew, st, cache_k4, cache_v4, page_table, layer, gf, final_norm):
    t = tnew * dbsz
    ssm0, ssm_conv0, rwkv0, shift0, ffn_conv0 = st
    proj = _rms_matmul(x, lp['norm1_g'], lp['w_proj'], t, 1024)
    proj3 = proj.reshape(tnew, dbsz, N_PROJ)

    qkv = proj3[:, :, C_Q:C_Q + 3 * A_W].transpose(1, 0, 2)
    qkv8 = jnp.pad(qkv, ((0, 0), (0, SUBLANES - tnew), (0, 0)))
    oa8 = _moba_sample(qkv8[:, :, :A_W], qkv8[:, :, A_W:2 * A_W], qkv8[:, :, 2 * A_W:],
                       cache_k4, cache_v4, page_table, layer, tnew)
    oa = oa8[:, :tnew].transpose(1, 0, 2).reshape(t, A_W)

    h0 = ssm0.reshape(-1, dbsz, N_PAIRS, 2 * S_HD, S_STATE)
    ob3, ssm_new = _ssd_sample(proj3, ssm_conv0.transpose(1, 0, 2), h0, layer, lp['ssm_conv_w'],
                               lp['ssm_conv_b'], lp['ssm_dtb_vec'], lp['ssm_alog_vec'], lp['ssm_dvec'],
                               lp['ssm_norm_g'])
    ob = ob3.reshape(t, S_W)
    ssm_new = ssm_new.reshape(dbsz, S_HEADS, S_HD, S_STATE)

    r, w, k2, v, kk, kka, g = _rwkv_prep(
        proj, shift0.reshape(dbsz, R_IN_W), lp['rw_mu'], lp['rw_w0'], lp['rw_w2p'], lp['rw_a0'],
        lp['rw_a2p'], lp['rw_g2p'], lp['rw_kk'], lp['rw_ka'], t, dbsz, 1)
    to3 = lambda a: a.reshape(tnew, dbsz, R_W)
    yr3, rw_new = _rwkv_scan_sample(tuple(to3(a) for a in (r, w, k2, v, kk, kka)), rwkv0, layer)
    yr = yr3.reshape(t, R_W)

    x = _merge(x, proj, oa, ob, yr, r, k2, v, g, lp['b_gate'], lp['rw_ln_g'], lp['rw_ln_b'], lp['rw_rk'],
               lp['w_pa'], lp['w_pb'], lp['w_pc'], lp['w_o'], min(t, 256))
    pre = ffn_conv0.transpose(1, 0, 2).reshape((F_CONV - 1) * dbsz, D_FF)
    x, tail = _ffn(x, lp['norm2_g'], lp['w_up'], lp['w_down'], lp['ffn_conv_w'], lp['ffn_conv_b'],
                   pre, gf, t, _ffn_tf(), dbsz, 1, final_norm)

    k_new = qkv[:, :, A_W:2 * A_W].reshape(dbsz, tnew, A_HEADS, A_HD)
    v_new = qkv[:, :, 2 * A_W:].reshape(dbsz, tnew, A_HEADS, A_HD)
    ssm_conv_new = proj3[tnew - (S_CONV - 1):, :, C_XBC:C_XBC + S_CONV_CH].transpose(1, 0, 2)
    shift_new = proj3[tnew - 1:, :, C_RW:C_RW + R_IN_W].transpose(1, 0, 2)
    ffn_conv_new = tail.reshape(F_CONV - 1, dbsz, D_FF).transpose(1, 0, 2)
    state = (k_new, v_new, ssm_new, ssm_conv_new, _unpack_rwkv_state(rw_new), shift_new, ffn_conv_new)
    return x, state


def kernel(x_prompt, x_sample, cache_k, cache_v, state_ssm, state_ssm_conv, state_rwkv, state_rwkv_shift, state_ffn_conv, page_table, norm1_g, w_in, b_gate, w_pa, ssm_conv_w, ssm_conv_b, ssm_dt_bias, ssm_a_log, ssm_d, ssm_norm_g, w_pb, rw_mu, rw_w0, rw_w2, rw_a0, rw_a2, rw_g2, rw_kk, rw_ka, rw_rk, rw_ln_g, rw_ln_b, w_pc, w_o, norm2_g, w_up, ffn_conv_w, ffn_conv_b, w_down, norm_f_g):
    params = dict(norm1_g=norm1_g, w_in=w_in, b_gate=b_gate, w_pa=w_pa, ssm_conv_w=ssm_conv_w,
                  ssm_conv_b=ssm_conv_b, ssm_dt_bias=ssm_dt_bias, ssm_a_log=ssm_a_log, ssm_d=ssm_d,
                  ssm_norm_g=ssm_norm_g, w_pb=w_pb, rw_mu=rw_mu, rw_w0=rw_w0, rw_w2=rw_w2, rw_a0=rw_a0,
                  rw_a2=rw_a2, rw_g2=rw_g2, rw_kk=rw_kk, rw_ka=rw_ka, rw_rk=rw_rk, rw_ln_g=rw_ln_g,
                  rw_ln_b=rw_ln_b, w_pc=w_pc, w_o=w_o, norm2_g=norm2_g, w_up=w_up, ffn_conv_w=ffn_conv_w,
                  ffn_conv_b=ffn_conv_b, w_down=w_down)
    depth = w_in.shape[0]
    bsz, seq, _ = x_prompt.shape
    dbsz, tnew, _ = x_sample.shape
    head = jnp.arange(A_HEADS, dtype=F32) + 1.0
    slopes = jnp.exp2(-8.0 * head / A_HEADS)
    slopes_pair = jnp.repeat(slopes, A_HD).reshape(N_PAIRS, 1, LANES)
    cache_k4 = cache_k.transpose(0, 1, 3, 4, 2)
    cache_v4 = cache_v.transpose(0, 1, 3, 4, 2)
    gf = norm_f_g.reshape(1, D_MODEL)

    hp = x_prompt.reshape(bsz * seq, D_MODEL)
    hs = x_sample.transpose(1, 0, 2).reshape(tnew * dbsz, D_MODEL)
    new_p = [[] for _ in range(7)]
    new_s = [[] for _ in range(7)]
    rwkv_packed = _pack_rwkv_state(state_rwkv)
    for l in range(depth):
        lp = _prep_layer_params(l, params)
        last = l == depth - 1
        hp, sp = _prompt_layer(hp, lp, bsz, seq, slopes_pair, gf, last)
        st = (state_ssm, state_ssm_conv[l], rwkv_packed, state_rwkv_shift[l], state_ffn_conv[l])
        hs, ss = _sample_layer(hs, lp, dbsz, tnew, st, cache_k4, cache_v4, page_table, l, gf, last)
        for j in range(7):
            new_p[j].append(sp[j])
            new_s[j].append(ss[j])
    y_prompt = hp.reshape(bsz, seq, D_MODEL)
    y_sample = hs.reshape(tnew, dbsz, D_MODEL).transpose(1, 0, 2)
    outs = [y_prompt, y_sample]
    for j in range(7):
        outs += [jnp.stack(new_p[j]), jnp.stack(new_s[j])]
    return tuple(outs)
```
